```python
import jax, jax.numpy as jnp
from jax import lax
import numpy as np

D_MODEL = 1024
BATCH = 2
SEQ = 8192
DEPTH = 2

ML_HEADS = 4
ML_DQK = 64
ML_DV = 128
ML_CONV = 4
ML_CHUNK = 64
MOBA_HEADS = 4
MOBA_DH = 64
MOBA_BLOCK = 256
MOBA_TOPK = 3
MOBA_QCHUNK = 64
DSA_HEADS = 4
DSA_DH = 64
IDX_HEADS = 4
IDX_DH = 64
DSA_TOPK_MAX = 256
DSA_QCHUNK = 128
D_FF = 4 * D_MODEL
EPS = 1e-6

ML_QK_W = ML_HEADS * ML_DQK
ML_V_W = ML_HEADS * ML_DV
MOBA_W = MOBA_HEADS * MOBA_DH
DSA_W = DSA_HEADS * DSA_DH
D_MIX = ML_V_W + MOBA_W + DSA_W
IN_SPLITS = (ML_QK_W, ML_QK_W, ML_V_W, ML_V_W, ML_HEADS, ML_HEADS,
             MOBA_W, MOBA_W, MOBA_W,
             DSA_W, DSA_W, DSA_W, IDX_HEADS * IDX_DH, IDX_DH, IDX_HEADS)
D_IN = 2 * ML_QK_W + 2 * ML_V_W + 2 * ML_HEADS + 3 * MOBA_W + 3 * DSA_W + IDX_HEADS * IDX_DH + IDX_DH + IDX_HEADS

kernel_name = 'hybrid_mlstm_moba_dsa_trunk'


def _split_points():
    pts, acc = [], 0
    for n in IN_SPLITS[:-1]:
        acc += n
        pts.append(acc)
    return pts


def rmsnorm(x, g):
    xf = x.astype(jnp.float32)
    y = xf * lax.rsqrt(jnp.mean(xf * xf, axis=-1, keepdims=True) + EPS)
    return (y * g.astype(jnp.float32)).astype(x.dtype)


def causal_conv(x, w):
    K = w.shape[0]
    S = x.shape[1]
    xp = jnp.pad(x, ((0, 0), (K - 1, 0), (0, 0)))
    return sum(xp[:, j:j + S] * w[j] for j in range(K))


def mlstm_chunkwise(q, k, v, ig, fg):
    B, H, S, DK = q.shape
    DV = v.shape[-1]
    L = ML_CHUNK
    NC = S // L
    q = q.reshape(B, H, NC, L, DK) * (DK ** -0.5)
    k = k.reshape(B, H, NC, L, DK)
    v = v.reshape(B, H, NC, L, DV)
    ig = ig.reshape(B, H, NC, L)
    lf = jax.nn.log_sigmoid(fg).reshape(B, H, NC, L)
    b = jnp.cumsum(lf, axis=-1)
    causal = jnp.tril(jnp.ones((L, L), dtype=bool))
    dlog = jnp.where(causal, b[..., :, None] - b[..., None, :] + ig[..., None, :], -jnp.inf)
    bL = b[..., -1]
    a = bL[..., None] - b + ig
    m_loc = jnp.max(a, axis=-1)
    wa = jnp.exp(a - m_loc[..., None])
    C_loc = jnp.einsum('bhcs,bhcsv,bhcsk->bhcvk', wa, v, k)
    n_loc = jnp.einsum('bhcs,bhcsk->bhck', wa, k)

    def step(carry, xs):
        C, n, m = carry
        bL_c, m_c, C_c, n_c = xs
        m_new = jnp.maximum(bL_c + m, m_c)
        s_old = jnp.exp(bL_c + m - m_new)
        s_loc = jnp.exp(m_c - m_new)
        C_new = s_old[..., None, None] * C + s_loc[..., None, None] * C_c
        n_new = s_old[..., None] * n + s_loc[..., None] * n_c
        return (C_new, n_new, m_new), (C, n, m)

    init = (jnp.zeros((B, H, DV, DK), jnp.float32), jnp.zeros((B, H, DK), jnp.float32),
            jnp.zeros((B, H), jnp.float32))
    xs = (jnp.moveaxis(bL, 2, 0), jnp.moveaxis(m_loc, 2, 0),
          jnp.moveaxis(C_loc, 2, 0), jnp.moveaxis(n_loc, 2, 0))
    _, (C0, n0, m0) = lax.scan(step, init, xs)
    C0 = jnp.moveaxis(C0, 0, 2)
    n0 = jnp.moveaxis(n0, 0, 2)
    m0 = jnp.moveaxis(m0, 0, 2)
    inter = b + m0[..., None]
    m_t = jnp.maximum(inter, jnp.max(dlog, axis=-1))
    w_inter = jnp.exp(inter - m_t)
    w_intra = jnp.exp(dlog - m_t[..., None]) * jnp.einsum('bhcjk,bhcsk->bhcjs', q, k)
    num = w_inter[..., None] * jnp.einsum('bhcvk,bhcjk->bhcjv', C0, q) + jnp.einsum('bhcjs,bhcsv->bhcjv', w_intra, v)
    den = w_inter * jnp.einsum('bhck,bhcjk->bhcj', n0, q) + jnp.sum(w_intra, axis=-1)
    h = num / jnp.maximum(jnp.abs(den), jnp.exp(-m_t))[..., None]
    return h.reshape(B, H, S, DV)


def mlstm_group(q, k, v, o, ig, fg, conv_w, head_norm):
    B, S, _ = q.shape
    f32 = jnp.float32
    qk = jax.nn.silu(causal_conv(jnp.concatenate([q, k], axis=-1).astype(f32), conv_w.astype(f32)))
    q, k = qk[..., :ML_QK_W], qk[..., ML_QK_W:]
    heads = lambda t, d: t.reshape(B, S, ML_HEADS, d).transpose(0, 2, 1, 3)
    h = mlstm_chunkwise(heads(q, ML_DQK), heads(k, ML_DQK), heads(v.astype(f32), ML_DV),
                        ig.astype(f32).transpose(0, 2, 1), fg.astype(f32).transpose(0, 2, 1))
    h = h.transpose(0, 2, 1, 3)
    h = h * lax.rsqrt(jnp.mean(h * h, axis=-1, keepdims=True) + EPS)
    h = h.reshape(B, S, ML_V_W) * head_norm.astype(f32)
    return (jax.nn.sigmoid(o.astype(f32)) * h).astype(v.dtype)


def moba_group(q, k, v):
    B, S, _ = q.shape
    H, D, BS = MOBA_HEADS, MOBA_DH, MOBA_BLOCK
    NB = -(-S // BS)
    pad = NB * BS - S
    q = q.reshape(B, S, H, D).transpose(0, 2, 1, 3)
    k = k.reshape(B, S, H, D).transpose(0, 2, 1, 3)
    v = v.reshape(B, S, H, D).transpose(0, 2, 1, 3)
    kb = jnp.pad(k, ((0, 0), (0, 0), (0, pad), (0, 0))).reshape(B, H, NB, BS, D)
    vb = jnp.pad(v, ((0, 0), (0, 0), (0, pad), (0, 0))).reshape(B, H, NB, BS, D)
    k_mean = jnp.mean(kb.astype(jnp.float32), axis=3)
    n_sel = max(min(MOBA_TOPK, NB - 1), 1)
    QC = MOBA_QCHUNK
    NQ = S // QC
    qc = q.reshape(B, H, NQ, QC, D).transpose(2, 0, 1, 3, 4)
    bidx = jnp.arange(B)[:, None, None, None]
    hidx = jnp.arange(H)[None, :, None, None]

    def chunk(args):
        qi, ci = args
        t = ci * QC + jnp.arange(QC)
        own = t // BS
        gate = jnp.einsum('bhqd,bhnd->bhqn', qi.astype(jnp.float32), k_mean)
        past = jnp.arange(NB)[None, :] < own[:, None]
        gate = jnp.where(past, gate, -jnp.inf)
        _, sel = lax.top_k(gate, n_sel)
        blocks = jnp.concatenate([sel, jnp.broadcast_to(own[:, None], (B, H, QC, 1))], axis=-1)
        blk_ok = jnp.concatenate([sel < own[:, None], jnp.ones((B, H, QC, 1), dtype=bool)], axis=-1)
        kg = kb[bidx, hidx, blocks]
        vg = vb[bidx, hidx, blocks]
        s = jnp.einsum('bhqd,bhqnkd->bhqnk', qi, kg).astype(jnp.float32) * (D ** -0.5)
        kpos = blocks[..., None] * BS + jnp.arange(BS)
        mask = blk_ok[..., None] & (kpos <= t[:, None, None])
        s = jnp.where(mask, s, -jnp.inf).reshape(B, H, QC, -1)
        p = jax.nn.softmax(s, axis=-1).reshape(kg.shape[:-1]).astype(v.dtype)
        return jnp.einsum('bhqnk,bhqnkd->bhqd', p, vg)

    out = lax.map(chunk, (qc, jnp.arange(NQ)))
    return out.transpose(1, 0, 3, 2, 4).reshape(B, S, H * D)


def dsa_group(q, k, v, iq, ik, iw):
    B, S, _ = q.shape
    H, D = DSA_HEADS, DSA_DH
    topk = min(DSA_TOPK_MAX, S // 4)
    QC = DSA_QCHUNK
    NQ = S // QC
    qh = q.reshape(B, NQ, QC, H, D).transpose(1, 0, 2, 3, 4)
    kh = k.reshape(B, S, H, D)
    vh = v.reshape(B, S, H, D)
    iqc = iq.reshape(B, NQ, QC, IDX_HEADS, IDX_DH).transpose(1, 0, 2, 3, 4)
    iwc = (iw.astype(jnp.float32) * (IDX_HEADS ** -0.5 * IDX_DH ** -0.5)).reshape(B, NQ, QC, IDX_HEADS).transpose(1, 0, 2, 3)
    bidx = jnp.arange(B)[:, None, None]

    def chunk(args):
        qi, iqi, iwi, ci = args
        t = ci * QC + jnp.arange(QC)
        rel = jax.nn.relu(jnp.einsum('bqhd,bsd->bqhs', iqi, ik).astype(jnp.float32))
        score = jnp.einsum('bqh,bqhs->bqs', iwi, rel)
        score = jnp.where(jnp.arange(S)[None, :] <= t[:, None], score, -jnp.inf)
        _, sel = lax.top_k(score, topk)
        kg = kh[bidx, sel]
        vg = vh[bidx, sel]
        s = jnp.einsum('bqhd,bqkhd->bhqk', qi, kg).astype(jnp.float32) * (D ** -0.5)
        s = jnp.where((sel <= t[:, None])[:, None], s, -jnp.inf)
        p = jax.nn.softmax(s, axis=-1).astype(v.dtype)
        return jnp.einsum('bhqk,bqkhd->bqhd', p, vg)

    out = lax.map(chunk, (qh, iqc, iwc, jnp.arange(NQ)))
    return out.transpose(1, 0, 2, 3, 4).reshape(B, S, H * D)


def sqrelu_mlp(h, w1, w2):
    return jnp.square(jax.nn.relu(h @ w1)) @ w2


def setup_inputs(seed: int = 0) -> dict:
    key = jax.random.key(seed)
    ks = jax.random.split(key, 14)
    f32 = jnp.float32
    gain = lambda kk, shape: 1.0 + 0.05 * jax.random.normal(kk, shape, f32)
    return {
        'x': jax.random.normal(ks[0], (BATCH, SEQ, D_MODEL), f32),
        'norm_mix_pre': gain(ks[1], (DEPTH, D_MODEL)),
        'w_in': jax.random.normal(ks[2], (DEPTH, D_MODEL, D_IN), f32) * D_MODEL ** -0.5,
        'ml_conv': jax.random.normal(ks[3], (DEPTH, ML_CONV, 2 * ML_QK_W), f32) * ML_CONV ** -0.5,
        'ml_i_bias': 0.1 * jax.random.normal(ks[4], (DEPTH, ML_HEADS), f32),
        'ml_f_bias': 3.0 + 3.0 * jax.random.uniform(ks[5], (DEPTH, ML_HEADS), f32),
        'ml_head_norm': gain(ks[6], (DEPTH, ML_V_W)),
        'w_out': jax.random.normal(ks[7], (DEPTH, D_MIX, D_MODEL), f32) * D_MIX ** -0.5,
        'norm_mix_post': gain(ks[8], (DEPTH, D_MODEL)),
        'norm_mlp_pre': gain(ks[9], (DEPTH, D_MODEL)),
        'w_ff1': jax.random.normal(ks[10], (DEPTH, D_MODEL, D_FF), f32) * D_MODEL ** -0.5,
        'w_ff2': jax.random.normal(ks[11], (DEPTH, D_FF, D_MODEL), f32) * D_FF ** -0.5,
        'norm_mlp_post': gain(ks[12], (DEPTH, D_MODEL)),
    }


def reference(x, norm_mix_pre, w_in, ml_conv, ml_i_bias, ml_f_bias, ml_head_norm, w_out,
              norm_mix_post, norm_mlp_pre, w_ff1, w_ff2, norm_mlp_post):
    pts = _split_points()
    for l in range(DEPTH):
        h = rmsnorm(x, norm_mix_pre[l])
        proj = h @ w_in[l]
        (ml_q, ml_k, ml_v, ml_o, ml_i, ml_f, mb_q, mb_k, mb_v,
         ds_q, ds_k, ds_v, ix_q, ix_k, ix_w) = jnp.split(proj, pts, axis=-1)
        y_ml = mlstm_group(ml_q, ml_k, ml_v, ml_o, ml_i + ml_i_bias[l], ml_f + ml_f_bias[l],
                           ml_conv[l], ml_head_norm[l])
        y_mb = moba_group(mb_q, mb_k, mb_v)
        y_ds = dsa_group(ds_q, ds_k, ds_v, ix_q, ix_k, ix_w)
        mix = jnp.concatenate([y_ml, y_mb, y_ds], axis=-1) @ w_out[l]
        x = x + rmsnorm(mix, norm_mix_post[l])
        h = rmsnorm(x, norm_mlp_pre[l])
        x = x + rmsnorm(sqrelu_mlp(h, w_ff1[l], w_ff2[l]), norm_mlp_post[l])
    return x
```

```python
import functools

import jax
import jax.numpy as jnp
from jax import lax
from jax.experimental import pallas as pl
from jax.experimental.pallas import tpu as pltpu

F32 = jnp.float32
BF16 = jnp.bfloat16
I32 = jnp.int32

D_MODEL = 1024
ML_HEADS, ML_DQK, ML_DV, ML_CONV = 4, 64, 128, 4
MOBA_HEADS, MOBA_DH, MOBA_BLOCK, MOBA_TOPK = 4, 64, 256, 3
DSA_HEADS, DSA_DH, IDX_HEADS, IDX_DH, DSA_TOPK_MAX = 4, 64, 4, 64, 256
D_FF = 4 * D_MODEL
EPS = 1e-6

ML_QK_W = ML_HEADS * ML_DQK
ML_V_W = ML_HEADS * ML_DV
MOBA_W = MOBA_HEADS * MOBA_DH
DSA_W = DSA_HEADS * DSA_DH
IN_SPLITS = (ML_QK_W, ML_QK_W, ML_V_W, ML_V_W, ML_HEADS, ML_HEADS,
             MOBA_W, MOBA_W, MOBA_W,
             DSA_W, DSA_W, DSA_W, IDX_HEADS * IDX_DH, IDX_DH, IDX_HEADS)

LANES = 128
SUBLANES = 8
VMEM_LIMIT = 52 * 1024 * 1024

ML_CHUNK = 256
ATT_TQ = 256
PROJ_TM = 512
MLP_TM = 512
MLP_TF = 1024

NEG_INF = float("-inf")
INT_MIN = -(2 ** 31)
INT_MAX = 2 ** 31 - 1
KEY_NEG_INF = -2139095041

_NT = (((1,), (1,)), ((), ()))
_TN = (((0,), (0,)), ((), ()))


def _rms(x, g):
    return x * lax.rsqrt(jnp.mean(x * x, axis=-1, keepdims=True) + EPS) * g


def _log_sigmoid(x):
    return jnp.minimum(x, 0.0) - jnp.log1p(jnp.exp(-jnp.abs(x)))


def _params(*sem):
    return pltpu.CompilerParams(dimension_semantics=sem, vmem_limit_bytes=VMEM_LIMIT)


_N_QK, _N_V, _N_O, _N_MBK, _N_DSK, _N_IXK, _N_SM = 0, 512, 1024, 1536, 1792, 2048, 2176
_N_TOTAL = 2304
_T_MBQ, _T_MBV, _T_DSQ, _T_DSV, _T_IXQ, _T_SM = 0, 256, 512, 768, 1024, 1280
_T_TOTAL = 1296


def _in_proj_kernel(x_ref, g_ref, wn_ref, wt_ref,
                    qk_ref, v_ref, o_ref, sm_ref, mbk_ref, dsk_ref, ixk_ref,
                    mbqT_ref, mbvT_ref, dsqT_ref, dsvT_ref, ixqT_ref, smT_ref):
    h = _rms(x_ref[...], g_ref[...]).astype(BF16)

    def mm(a, n):
        return jnp.dot(h, wn_ref[:, a:a + n], preferred_element_type=F32)

    def mt(a, n):
        return lax.dot_general(wt_ref[a:a + n, :], h, _NT, preferred_element_type=F32)

    qk_ref[...] = mm(_N_QK, 512)
    v_ref[...] = mm(_N_V, 512).astype(BF16)
    o_ref[...] = mm(_N_O, 512)
    mbk_ref[...] = mm(_N_MBK, 256).astype(BF16)
    dsk_ref[...] = mm(_N_DSK, 256).astype(BF16)
    ixk_ref[...] = mm(_N_IXK, 128).astype(BF16)
    sm_ref[...] = mm(_N_SM, 128)
    mbqT_ref[0] = mt(_T_MBQ, 256).astype(BF16)
    mbvT_ref[0] = mt(_T_MBV, 256).astype(BF16)
    dsqT_ref[0] = mt(_T_DSQ, 256).astype(BF16)
    dsvT_ref[0] = mt(_T_DSV, 256).astype(BF16)
    ixqT_ref[0] = mt(_T_IXQ, 256).astype(BF16)
    smT_ref[0] = mt(_T_SM, 16)


def _prep_in_weights(w_in):
    pts, acc = [], 0
    for n in IN_SPLITS:
        pts.append((acc, acc + n))
        acc += n
    col = lambda i: w_in[:, pts[i][0]:pts[i][1]]
    (ml_q, ml_k, ml_v, ml_o, ml_i, ml_f, mb_q, mb_k, mb_v,
     ds_q, ds_k, ds_v, ix_q, ix_k, ix_w) = [col(i) for i in range(len(IN_SPLITS))]
    d = w_in.shape[0]
    small = jnp.concatenate([ml_i, ml_f, ix_w], axis=1)
    wn = jnp.concatenate([
        ml_q, ml_k, ml_v, ml_o, mb_k, ds_k,
        ix_k, jnp.zeros((d, 128 - IDX_DH), w_in.dtype),
        small, jnp.zeros((d, 128 - small.shape[1]), w_in.dtype)], axis=1)
    wt = jnp.concatenate([
        mb_q, mb_v, ds_q, ds_v, ix_q,
        small, jnp.zeros((d, 16 - small.shape[1]), w_in.dtype)], axis=1).T
    assert wn.shape == (d, _N_TOTAL) and wt.shape == (_T_TOTAL, d)
    return wn.astype(BF16), wt.astype(BF16)


def _in_proj(x2, g, wn, wt, B, S):
    T, D = x2.shape
    tm = min(PROJ_TM, S)
    nsb = S // tm
    row = lambda n: pl.BlockSpec((tm, n), lambda i: (i, 0))
    tr = lambda n: pl.BlockSpec((1, n, tm), lambda i: (i // nsb, 0, i % nsb))
    full = lambda a: pl.BlockSpec(a.shape, lambda i: (0,) * a.ndim)
    out_shape = (
        jax.ShapeDtypeStruct((T, 512), F32),
        jax.ShapeDtypeStruct((T, 512), BF16),
        jax.ShapeDtypeStruct((T, 512), F32),
        jax.ShapeDtypeStruct((T, 128), F32),
        jax.ShapeDtypeStruct((T, 256), BF16),
        jax.ShapeDtypeStruct((T, 256), BF16),
        jax.ShapeDtypeStruct((T, 128), BF16),
        jax.ShapeDtypeStruct((B, 256, S), BF16),
        jax.ShapeDtypeStruct((B, 256, S), BF16),
        jax.ShapeDtypeStruct((B, 256, S), BF16),
        jax.ShapeDtypeStruct((B, 256, S), BF16),
        jax.ShapeDtypeStruct((B, 256, S), BF16),
        jax.ShapeDtypeStruct((B, 16, S), F32),
    )
    out_specs = (row(512), row(512), row(512), row(128), row(256), row(256), row(128),
                 tr(256), tr(256), tr(256), tr(256), tr(256), tr(16))
    return pl.pallas_call(
        _in_proj_kernel,
        grid=(T // tm,),
        in_specs=[row(D), full(g), full(wn), full(wt)],
        out_specs=out_specs,
        out_shape=out_shape,
        compiler_params=_params("parallel"),
    )(x2, g, wn, wt)


def _mlstm_kernel(qk_ref, v_ref, o_ref, sm_ref, smT_ref, conv_ref, brow_ref, bcol_ref, hn_ref,
                  y_ref, xp_scr, st_scr, m_scr, *, L):
    c = pl.program_id(1)
    halo = SUBLANES

    @pl.when(c == 0)
    def _():
        xp_scr[0:halo, :] = jnp.zeros((halo, 2 * ML_QK_W), F32)
        st_scr[...] = jnp.zeros_like(st_scr)
        m_scr[...] = jnp.zeros_like(m_scr)

    cur = qk_ref[0]
    xp_scr[halo:halo + L, :] = cur
    base = halo - (ML_CONV - 1)
    acc = conv_ref[0:1, :] * xp_scr[base:base + L, :]
    for j in range(1, ML_CONV):
        acc = acc + conv_ref[j:j + 1, :] * xp_scr[base + j:base + j + L, :]
    xp_scr[0:halo, :] = cur[L - halo:L, :]
    qk = acc * jax.nn.sigmoid(acc)
    q_all = (qk[:, :ML_QK_W] * (ML_DQK ** -0.5)).astype(BF16)
    k_all = qk[:, ML_QK_W:]

    gcol = sm_ref[0] + brow_ref[...]
    grow = smT_ref[0][0:SUBLANES, :] + bcol_ref[...]
    ri = lax.broadcasted_iota(I32, (L, L), 0)
    ci = lax.broadcasted_iota(I32, (L, L), 1)
    causal = ci <= ri
    tril = causal.astype(F32)
    triu = (ri <= ci).astype(F32)
    hp = lax.Precision.HIGHEST
    b_col = jnp.dot(tril, _log_sigmoid(gcol), precision=hp, preferred_element_type=F32)
    b_row = jnp.dot(_log_sigmoid(grow), triu, precision=hp, preferred_element_type=F32)

    v_all = v_ref[0]
    o_all = o_ref[0]
    lane = lax.broadcasted_iota(I32, (L, LANES), 1)
    ones_col = jnp.where(lane == 0, 1.0, 0.0).astype(BF16)
    for h in range(ML_HEADS):
        qh = q_all[:, h * ML_DQK:(h + 1) * ML_DQK]
        kh = k_all[:, h * ML_DQK:(h + 1) * ML_DQK]
        vh = v_all[:, h * ML_DV:(h + 1) * ML_DV]
        bj = b_col[:, ML_HEADS + h:ML_HEADS + h + 1]
        icol = gcol[:, h:h + 1]
        r_row = grow[h:h + 1, :] - b_row[ML_HEADS + h:ML_HEADS + h + 1, :]
        dlog = jnp.where(causal, bj + r_row, NEG_INF)
        m0 = m_scr[h:h + 1, 0:1]
        inter = bj + m0
        m_t = jnp.maximum(inter, jnp.max(dlog, axis=-1, keepdims=True))
        w_inter = jnp.exp(inter - m_t)
        qk_s = lax.dot_general(qh, kh.astype(BF16), _NT, preferred_element_type=F32)
        w_intra = jnp.exp(dlog - m_t) * qk_s
        st = st_scr[h]
        cq = jnp.dot(qh, st.astype(BF16), preferred_element_type=F32)
        num = w_inter * cq[:, :ML_DV] + jnp.dot(w_intra.astype(BF16), vh, preferred_element_type=F32)
        den = w_inter * cq[:, ML_DV:ML_DV + 1] + jnp.sum(w_intra, axis=-1, keepdims=True)
        hh = num / jnp.maximum(jnp.abs(den), jnp.exp(-m_t))
        hh = hh * lax.rsqrt(jnp.mean(hh * hh, axis=-1, keepdims=True) + EPS)
        hh = hh * hn_ref[:, h * ML_DV:(h + 1) * ML_DV]
        y_ref[0, :, h * ML_DV:(h + 1) * ML_DV] = (
            jax.nn.sigmoid(o_all[:, h * ML_DV:(h + 1) * ML_DV]) * hh).astype(y_ref.dtype)

        b_last = bj[L - 1:L, :]
        a = b_last - bj + icol
        m_loc = jnp.max(a, axis=0, keepdims=True)
        wa = jnp.exp(a - m_loc)
        kw = (kh * wa).astype(BF16)
        vaug = jnp.concatenate([vh, ones_col], axis=1)
        c_loc = lax.dot_general(kw, vaug, _TN, preferred_element_type=F32)
        m_new = jnp.maximum(b_last + m0, m_loc)
        s_old = jnp.exp(b_last + m0 - m_new)
        s_loc = jnp.exp(m_loc - m_new)
        st_scr[h] = s_old * st + s_loc * c_loc
        m_scr[h:h + 1, :] = jnp.broadcast_to(m_new, (1, LANES))


def _mlstm(qk, v, o, sm, smT, conv_w, i_bias, f_bias, head_norm, B, S):
    L = min(ML_CHUNK, S)
    nc = S // L
    qk = qk.reshape(B, S, 2 * ML_QK_W)
    v = v.reshape(B, S, ML_V_W)
    o = o.reshape(B, S, ML_V_W)
    sm = sm.reshape(B, S, LANES)
    bias = jnp.concatenate([i_bias, f_bias]).astype(F32)
    brow = jnp.zeros((1, LANES), F32).at[0, :2 * ML_HEADS].set(bias)
    bcol = bias.reshape(2 * ML_HEADS, 1)
    hn = head_norm.reshape(1, ML_V_W).astype(F32)
    blk = lambda n: pl.BlockSpec((1, L, n), lambda b, c: (b, c, 0))
    full = lambda a: pl.BlockSpec(a.shape, lambda b, c: (0,) * a.ndim)
    return pl.pallas_call(
        functools.partial(_mlstm_kernel, L=L),
        grid=(B, nc),
        in_specs=[blk(2 * ML_QK_W), blk(ML_V_W), blk(ML_V_W), blk(LANES),
                  pl.BlockSpec((1, 16, L), lambda b, c: (b, 0, c)),
                  full(conv_w), full(brow), full(bcol), full(hn)],
        out_specs=blk(ML_V_W),
        out_shape=jax.ShapeDtypeStruct((B, S, ML_V_W), BF16),
        scratch_shapes=[pltpu.VMEM((L + SUBLANES, 2 * ML_QK_W), F32),
                        pltpu.VMEM((ML_HEADS, ML_DQK, ML_DV + LANES), F32),
                        pltpu.VMEM((SUBLANES, LANES), F32)],
        compiler_params=_params("parallel", "arbitrary"),
    )(qk, v, o, sm, smT, conv_w.astype(F32), brow, bcol, hn)


def _block_diag_T(qT, heads, dh):
    rid = lax.broadcasted_iota(I32, qT.shape, 0)
    zero = jnp.zeros_like(qT)
    return jnp.concatenate(
        [jnp.where((rid >= h * dh) & (rid < (h + 1) * dh), qT, zero) for h in range(heads)], axis=1)


def _softmax_step(sT, bias, vT_chunk, ml, acc_scr, heads, dh, TQ):
    m_old, l_old = ml
    s = sT + bias
    m_new = jnp.maximum(m_old, jnp.max(s, axis=0, keepdims=True))
    m_safe = jnp.where(m_new == NEG_INF, 0.0, m_new)
    alpha = jnp.exp(m_old - m_safe)
    p = jnp.exp(s - m_safe)
    l_new = alpha * l_old + jnp.sum(p, axis=0, keepdims=True)
    pb = p.astype(BF16)
    for h in range(heads):
        pv = jnp.dot(vT_chunk[h * dh:(h + 1) * dh, :], pb[:, h * TQ:(h + 1) * TQ],
                     preferred_element_type=F32)
        acc_scr[h * dh:(h + 1) * dh, :] = alpha[:, h * TQ:(h + 1) * TQ] * acc_scr[h * dh:(h + 1) * dh, :] + pv
    return m_new, l_new


def _finish_attention(ml, acc_scr, y_ref, heads, dh, TQ):
    _, l = ml
    outs = [acc_scr[h * dh:(h + 1) * dh, :] / l[:, h * TQ:(h + 1) * TQ] for h in range(heads)]
    y_ref[0] = jnp.concatenate(outs, axis=0).T.astype(y_ref.dtype)


def _moba_kernel(qT_ref, k_ref, vT_ref, y_ref, km_scr, acc_scr, *, NB, NBP, n_sel):
    H, DH, BS = MOBA_HEADS, MOBA_DH, MOBA_BLOCK
    TQ = BS
    qi = pl.program_id(1)

    @pl.when(qi == 0)
    def _():
        km_scr[...] = jnp.zeros_like(km_scr)

        def body(j, carry):
            kb = k_ref[0, pl.ds(pl.multiple_of(j * BS, BS), BS), :].astype(F32)
            km_scr[pl.ds(j, 1), :] = jnp.mean(kb, axis=0, keepdims=True)
            return carry
        lax.fori_loop(0, NB, body, 0)

    qT = qT_ref[0]
    qbd = _block_diag_T(qT, H, DH)
    gate = jnp.dot(km_scr[...], qbd.astype(F32), precision=lax.Precision.HIGHEST,
                   preferred_element_type=F32)
    blk = lax.broadcasted_iota(I32, gate.shape, 0)
    g = jnp.where(blk < qi, gate, NEG_INF)
    sels = []
    for _ in range(n_sel):
        mx = jnp.max(g, axis=0, keepdims=True)
        isel = jnp.min(jnp.where(g == mx, blk, NBP), axis=0, keepdims=True)
        sels.append(jnp.where(mx > NEG_INF, isel, -1))
        g = jnp.where(blk == isel, NEG_INF, g)

    qs = (qbd * (DH ** -0.5)).astype(BF16)
    acc_scr[...] = jnp.zeros_like(acc_scr)
    ml0 = (jnp.full((1, H * TQ), NEG_INF, F32), jnp.zeros((1, H * TQ), F32))

    def chunk(j, bias, ml):
        start = pl.multiple_of(j * BS, BS)
        sT = jnp.dot(k_ref[0, pl.ds(start, BS), :], qs, preferred_element_type=F32)
        return _softmax_step(sT, bias, vT_ref[0, :, pl.ds(start, BS)], ml, acc_scr, H, DH, TQ)

    def past(j, ml):
        hit = sels[0] == j
        for s in sels[1:]:
            hit = hit | (s == j)
        return chunk(j, jnp.where(hit, 0.0, NEG_INF), ml)

    ml = lax.fori_loop(0, qi, past, ml0)
    kk = lax.broadcasted_iota(I32, (BS, H * TQ), 0)
    tt = lax.broadcasted_iota(I32, (BS, H * TQ), 1) % TQ
    ml = chunk(qi, jnp.where(kk <= tt, 0.0, NEG_INF), ml)
    _finish_attention(ml, acc_scr, y_ref, H, DH, TQ)


def _moba(qT, k, vT, B, S):
    BS = MOBA_BLOCK
    assert S % BS == 0
    NB = S // BS
    NBP = -(-NB // SUBLANES) * SUBLANES
    n_sel = max(min(MOBA_TOPK, NB - 1), 1)
    k = k.reshape(B, S, MOBA_W)
    return pl.pallas_call(
        functools.partial(_moba_kernel, NB=NB, NBP=NBP, n_sel=n_sel),
        grid=(B, NB),
        in_specs=[pl.BlockSpec((1, MOBA_W, BS), lambda b, i: (b, 0, i)),
                  pl.BlockSpec((1, S, MOBA_W), lambda b, i: (b, 0, 0)),
                  pl.BlockSpec((1, MOBA_W, S), lambda b, i: (b, 0, 0))],
        out_specs=pl.BlockSpec((1, BS, MOBA_W), lambda b, i: (b, i, 0)),
        out_shape=jax.ShapeDtypeStruct((B, S, MOBA_W), BF16),
        scratch_shapes=[pltpu.VMEM((NBP, MOBA_W), F32),
                        pltpu.VMEM((MOBA_W, BS), F32)],
        compiler_params=_params("parallel", "arbitrary"),
    )(qT, k, vT)


def _dsa_kernel(qT_ref, k_ref, vT_ref, iqT_ref, ik_ref, smT_ref, y_ref, key_scr, acc_scr,
                *, TQ, topk, idx_bits):
    H, DH = DSA_HEADS, DSA_DH
    KC = TQ
    qi = pl.program_id(1)
    nch = qi + 1
    chunk_start = lambda c: pl.multiple_of(c * KC, KC)

    iqT = iqT_ref[0]
    iq_cat = jnp.concatenate([iqT[h * IDX_DH:(h + 1) * IDX_DH, :] for h in range(IDX_HEADS)], axis=1)
    iw = smT_ref[0][2 * ML_HEADS:2 * ML_HEADS + IDX_HEADS, :] * (IDX_HEADS ** -0.5 * IDX_DH ** -0.5)
    t_pos = qi * TQ + lax.broadcasted_iota(I32, (KC, TQ), 1)
    s_off = lax.broadcasted_iota(I32, (KC, TQ), 0)

    def score_body(c, carry):
        ikc = ik_ref[0, pl.ds(chunk_start(c), KC), :][:, :IDX_DH]
        rel = jnp.dot(ikc, iq_cat, preferred_element_type=F32)
        sc = iw[0:1, :] * jnp.maximum(rel[:, 0:TQ], 0.0)
        for h in range(1, IDX_HEADS):
            sc = sc + iw[h:h + 1, :] * jnp.maximum(rel[:, h * TQ:(h + 1) * TQ], 0.0)
        sc = jnp.where(c * KC + s_off <= t_pos, sc, NEG_INF)
        u = pltpu.bitcast(sc, I32)
        key_scr[pl.ds(chunk_start(c), KC), :] = u ^ ((u >> 31) & INT_MAX)
        return carry
    lax.fori_loop(0, nch, score_body, 0)

    def count_ge(cand):
        def body(c, acc):
            ge = (key_scr[pl.ds(chunk_start(c), KC), :] >= cand).astype(I32)
            return acc + jnp.sum(ge.reshape(KC // SUBLANES, SUBLANES, TQ), axis=0)
        acc = lax.fori_loop(0, nch, body, jnp.zeros((SUBLANES, TQ), I32))
        return jnp.sum(acc, axis=0, keepdims=True)

    def value_bit(it, ans):
        cand = ans + jnp.left_shift(jnp.int32(1), 31 - it)
        return jnp.where(count_ge(cand) >= topk, cand, ans)
    kstar = lax.fori_loop(0, 32, value_bit, jnp.full((1, TQ), INT_MIN, I32))

    def rewrite(c, carry):
        x = key_scr[pl.ds(chunk_start(c), KC), :]
        kp = jnp.where(x > kstar, -1, jnp.where(x == kstar, c * KC + s_off, INT_MAX))
        key_scr[pl.ds(chunk_start(c), KC), :] = jnp.where(x == KEY_NEG_INF, INT_MAX, kp)
        return carry
    lax.fori_loop(0, nch, rewrite, 0)

    total = nch * KC

    def index_bit(it, ans):
        cand = ans + jnp.left_shift(jnp.int32(1), idx_bits - 1 - it)
        n_below = total - count_ge(cand)
        return jnp.where(n_below >= topk, ans, cand)
    cstar = lax.fori_loop(0, idx_bits, index_bit, jnp.zeros((1, TQ), I32))

    qs = (_block_diag_T(qT_ref[0], H, DH) * (DH ** -0.5)).astype(BF16)
    acc_scr[...] = jnp.zeros_like(acc_scr)
    ml0 = (jnp.full((1, H * TQ), NEG_INF, F32), jnp.zeros((1, H * TQ), F32))

    def att(c, ml):
        start = chunk_start(c)
        b1 = jnp.where(key_scr[pl.ds(start, KC), :] <= cstar, 0.0, NEG_INF)
        bias = jnp.concatenate([b1] * H, axis=1)
        sT = jnp.dot(k_ref[0, pl.ds(start, KC), :], qs, preferred_element_type=F32)
        return _softmax_step(sT, bias, vT_ref[0, :, pl.ds(start, KC)], ml, acc_scr, H, DH, TQ)
    ml = lax.fori_loop(0, nch, att, ml0)
    _finish_attention(ml, acc_scr, y_ref, H, DH, TQ)


def _dsa(qT, k, vT, iqT, ik, smT, B, S):
    TQ = min(ATT_TQ, S)
    assert S % TQ == 0
    topk = min(DSA_TOPK_MAX, S // 4)
    idx_bits = max((S - 1).bit_length(), 1)
    k = k.reshape(B, S, DSA_W)
    ik = ik.reshape(B, S, LANES)
    qblk = lambda n: pl.BlockSpec((1, n, TQ), lambda b, i: (b, 0, i))
    return pl.pallas_call(
        functools.partial(_dsa_kernel, TQ=TQ, topk=topk, idx_bits=idx_bits),
        grid=(B, S // TQ),
        in_specs=[qblk(DSA_W),
                  pl.BlockSpec((1, S, DSA_W), lambda b, i: (b, 0, 0)),
                  pl.BlockSpec((1, DSA_W, S), lambda b, i: (b, 0, 0)),
                  qblk(IDX_HEADS * IDX_DH),
                  pl.BlockSpec((1, S, LANES), lambda b, i: (b, 0, 0)),
                  qblk(16)],
        out_specs=pl.BlockSpec((1, TQ, DSA_W), lambda b, i: (b, i, 0)),
        out_shape=jax.ShapeDtypeStruct((B, S, DSA_W), BF16),
        scratch_shapes=[pltpu.VMEM((S, TQ), I32),
                        pltpu.VMEM((DSA_W, TQ), F32)],
        compiler_params=_params("parallel", "arbitrary"),
    )(qT, k, vT, iqT, ik, smT)


def _out_mlp_kernel(yml_ref, ymb_ref, yds_ref, x_ref, wout_ref, gpost_ref, gpre_ref,
                    w1_ref, w2_ref, g2_ref, out_ref, x1_scr, h_scr, acc_scr):
    f = pl.program_id(1)

    @pl.when(f == 0)
    def _():
        mix = jnp.dot(yml_ref[...], wout_ref[0:ML_V_W, :], preferred_element_type=F32)
        mix = mix + jnp.dot(ymb_ref[...], wout_ref[ML_V_W:ML_V_W + MOBA_W, :], preferred_element_type=F32)
        mix = mix + jnp.dot(yds_ref[...], wout_ref[ML_V_W + MOBA_W:, :], preferred_element_type=F32)
        x1 = x_ref[...] + _rms(mix, gpost_ref[...])
        x1_scr[...] = x1
        h_scr[...] = _rms(x1, gpre_ref[...]).astype(BF16)
        acc_scr[...] = jnp.zeros_like(acc_scr)

    u = jnp.maximum(jnp.dot(h_scr[...], w1_ref[...], preferred_element_type=F32), 0.0)
    acc_scr[...] += jnp.dot((u * u).astype(BF16), w2_ref[...], preferred_element_type=F32)

    @pl.when(f == pl.num_programs(1) - 1)
    def _():
        out_ref[...] = x1_scr[...] + _rms(acc_scr[...], g2_ref[...])


def _out_mlp(yml, ymb, yds, x2, w_out, g_post, g_pre, w1, w2, g2):
    T, D = x2.shape
    tm = min(MLP_TM, T)
    tf = min(MLP_TF, D_FF)
    row = lambda n: pl.BlockSpec((tm, n), lambda i, f: (i, 0))
    full = lambda a: pl.BlockSpec(a.shape, lambda i, f: (0,) * a.ndim)
    return pl.pallas_call(
        _out_mlp_kernel,
        grid=(T // tm, D_FF // tf),
        in_specs=[row(ML_V_W), row(MOBA_W), row(DSA_W), row(D), full(w_out), full(g_post), full(g_pre),
                  pl.BlockSpec((D, tf), lambda i, f: (0, f)),
                  pl.BlockSpec((tf, D), lambda i, f: (f, 0)),
                  full(g2)],
        out_specs=row(D),
        out_shape=jax.ShapeDtypeStruct((T, D), F32),
        scratch_shapes=[pltpu.VMEM((tm, D), F32), pltpu.VMEM((tm, D), BF16), pltpu.VMEM((tm, D), F32)],
        compiler_params=_params("parallel", "arbitrary"),
    )(yml, ymb, yds, x2, w_out, g_post, g_pre, w1, w2, g2)


def kernel(x, norm_mix_pre, w_in, ml_conv, ml_i_bias, ml_f_bias, ml_head_norm, w_out, norm_mix_post,
           norm_mlp_pre, w_ff1, w_ff2, norm_mlp_post):
    B, S, D = x.shape
    depth = w_in.shape[0]
    x2 = x.reshape(B * S, D)
    gain = lambda g: g.reshape(1, D).astype(F32)
    for l in range(depth):
        wn, wt = _prep_in_weights(w_in[l])
        (qk, v, o, sm, mbk, dsk, ixk, mbqT, mbvT, dsqT, dsvT, ixqT, smT) = _in_proj(
            x2, gain(norm_mix_pre[l]), wn, wt, B, S)
        y_ml = _mlstm(qk, v, o, sm, smT, ml_conv[l], ml_i_bias[l], ml_f_bias[l], ml_head_norm[l], B, S)
        y_mb = _moba(mbqT, mbk, mbvT, B, S)
        y_ds = _dsa(dsqT, dsk, dsvT, ixqT, ixk, smT, B, S)
        x2 = _out_mlp(y_ml.reshape(B * S, ML_V_W), y_mb.reshape(B * S, MOBA_W), y_ds.reshape(B * S, DSA_W),
                      x2, w_out[l].astype(BF16), gain(norm_mix_post[l]), gain(norm_mlp_pre[l]),
                      w_ff1[l].astype(BF16), w_ff2[l].astype(BF16), gain(norm_mlp_post[l]))
    return x2.reshape(B, S, D)
```

```python
import functools

import jax
import jax.numpy as jnp
from jax import lax
from jax.experimental import pallas as pl
from jax.experimental.pallas import tpu as pltpu

F32 = jnp.float32
BF16 = jnp.bfloat16
I32 = jnp.int32

D_MODEL = 1024
ML_HEADS, ML_DQK, ML_DV, ML_CONV = 4, 64, 128, 4
MOBA_HEADS, MOBA_DH, MOBA_BLOCK, MOBA_TOPK = 4, 64, 256, 3
DSA_HEADS, DSA_DH, IDX_HEADS, IDX_DH, DSA_TOPK_MAX = 4, 64, 4, 64, 256
D_FF = 4 * D_MODEL
EPS = 1e-6

ML_QK_W = ML_HEADS * ML_DQK
ML_V_W = ML_HEADS * ML_DV
MOBA_W = MOBA_HEADS * MOBA_DH
DSA_W = DSA_HEADS * DSA_DH
IN_SPLITS = (ML_QK_W, ML_QK_W, ML_V_W, ML_V_W, ML_HEADS, ML_HEADS,
             MOBA_W, MOBA_W, MOBA_W,
             DSA_W, DSA_W, DSA_W, IDX_HEADS * IDX_DH, IDX_DH, IDX_HEADS)

LANES = 128
SUBLANES = 8
VMEM_LIMIT = 52 * 1024 * 1024

ML_CHUNK = 256
ATT_TQ = 256
PROJ_TM = 512
MLP_TM = 512
MLP_TF = 1024

NEG_INF = float("-inf")
BISECT_CAP = 300

_NT = (((1,), (1,)), ((), ()))
_TN = (((0,), (0,)), ((), ()))


def _rms(x, g):
    return x * lax.rsqrt(jnp.mean(x * x, axis=-1, keepdims=True) + EPS) * g


def _log_sigmoid(x):
    return jnp.minimum(x, 0.0) - jnp.log1p(jnp.exp(-jnp.abs(x)))


def _params(*sem):
    return pltpu.CompilerParams(dimension_semantics=sem, vmem_limit_bytes=VMEM_LIMIT)


_N_QK, _N_V, _N_O, _N_MBK, _N_DSK, _N_IXK, _N_SM = 0, 512, 1024, 1536, 1792, 2048, 2176
_N_TOTAL = 2304
_T_MBQ, _T_MBV, _T_DSQ, _T_DSV, _T_IXQ, _T_SM = 0, 256, 512, 768, 1024, 1280
_T_TOTAL = 1296


def _in_proj_kernel(x_ref, g_ref, wn_ref, wt_ref,
                    qk_ref, v_ref, o_ref, sm_ref, mbk_ref, dsk_ref, ixk_ref,
                    mbqT_ref, mbvT_ref, dsqT_ref, dsvT_ref, ixqT_ref, smT_ref):
    h = _rms(x_ref[...], g_ref[...]).astype(BF16)

    def mm(a, n):
        return jnp.dot(h, wn_ref[:, a:a + n], preferred_element_type=F32)

    def mt(a, n):
        return lax.dot_general(wt_ref[a:a + n, :], h, _NT, preferred_element_type=F32)

    qk_ref[...] = mm(_N_QK, 512)
    v_ref[...] = mm(_N_V, 512).astype(BF16)
    o_ref[...] = mm(_N_O, 512)
    mbk_ref[...] = mm(_N_MBK, 256).astype(BF16)
    dsk_ref[...] = mm(_N_DSK, 256).astype(BF16)
    ixk_ref[...] = mm(_N_IXK, 128).astype(BF16)
    sm_ref[...] = mm(_N_SM, 128)
    mbqT_ref[0] = mt(_T_MBQ, 256).astype(BF16)
    mbvT_ref[0] = mt(_T_MBV, 256).astype(BF16)
    dsqT_ref[0] = mt(_T_DSQ, 256).astype(BF16)
    dsvT_ref[0] = mt(_T_DSV, 256).astype(BF16)
    ixqT_ref[0] = mt(_T_IXQ, 256).astype(BF16)
    smT_ref[0] = mt(_T_SM, 16)


def _prep_in_weights(w_in):
    pts, acc = [], 0
    for n in IN_SPLITS:
        pts.append((acc, acc + n))
        acc += n
    col = lambda i: w_in[:, pts[i][0]:pts[i][1]]
    (ml_q, ml_k, ml_v, ml_o, ml_i, ml_f, mb_q, mb_k, mb_v,
     ds_q, ds_k, ds_v, ix_q, ix_k, ix_w) = [col(i) for i in range(len(IN_SPLITS))]
    d = w_in.shape[0]
    small = jnp.concatenate([ml_i, ml_f, ix_w], axis=1)
    wn = jnp.concatenate([
        ml_q, ml_k, ml_v, ml_o, mb_k, ds_k,
        ix_k, jnp.zeros((d, 128 - IDX_DH), w_in.dtype),
        small, jnp.zeros((d, 128 - small.shape[1]), w_in.dtype)], axis=1)
    wt = jnp.concatenate([
        mb_q, mb_v, ds_q, ds_v, ix_q,
        small, jnp.zeros((d, 16 - small.shape[1]), w_in.dtype)], axis=1).T
    assert wn.shape == (d, _N_TOTAL) and wt.shape == (_T_TOTAL, d)
    return wn.astype(BF16), wt.astype(BF16)


def _in_proj(x2, g, wn, wt, B, S):
    T, D = x2.shape
    tm = min(PROJ_TM, S)
    nsb = S // tm
    row = lambda n: pl.BlockSpec((tm, n), lambda i: (i, 0))
    tr = lambda n: pl.BlockSpec((1, n, tm), lambda i: (i // nsb, 0, i % nsb))
    full = lambda a: pl.BlockSpec(a.shape, lambda i: (0,) * a.ndim)
    out_shape = (
        jax.ShapeDtypeStruct((T, 512), F32),
        jax.ShapeDtypeStruct((T, 512), BF16),
        jax.ShapeDtypeStruct((T, 512), F32),
        jax.ShapeDtypeStruct((T, 128), F32),
        jax.ShapeDtypeStruct((T, 256), BF16),
        jax.ShapeDtypeStruct((T, 256), BF16),
        jax.ShapeDtypeStruct((T, 128), BF16),
        jax.ShapeDtypeStruct((B, 256, S), BF16),
        jax.ShapeDtypeStruct((B, 256, S), BF16),
        jax.ShapeDtypeStruct((B, 256, S), BF16),
        jax.ShapeDtypeStruct((B, 256, S), BF16),
        jax.ShapeDtypeStruct((B, 256, S), BF16),
        jax.ShapeDtypeStruct((B, 16, S), F32),
    )
    out_specs = (row(512), row(512), row(512), row(128), row(256), row(256), row(128),
                 tr(256), tr(256), tr(256), tr(256), tr(256), tr(16))
    return pl.pallas_call(
        _in_proj_kernel,
        grid=(T // tm,),
        in_specs=[row(D), full(g), full(wn), full(wt)],
        out_specs=out_specs,
        out_shape=out_shape,
        compiler_params=_params("parallel"),
    )(x2, g, wn, wt)


def _mlstm_kernel(qk_ref, v_ref, o_ref, sm_ref, smT_ref, conv_ref, brow_ref, bcol_ref, hn_ref,
                  y_ref, xp_scr, st_scr, m_scr, *, L):
    c = pl.program_id(1)
    halo = SUBLANES

    @pl.when(c == 0)
    def _():
        xp_scr[0:halo, :] = jnp.zeros((halo, 2 * ML_QK_W), F32)
        st_scr[...] = jnp.zeros_like(st_scr)
        m_scr[...] = jnp.zeros_like(m_scr)

    cur = qk_ref[0]
    xp_scr[halo:halo + L, :] = cur
    base = halo - (ML_CONV - 1)
    acc = conv_ref[0:1, :] * xp_scr[base:base + L, :]
    for j in range(1, ML_CONV):
        acc = acc + conv_ref[j:j + 1, :] * xp_scr[base + j:base + j + L, :]
    xp_scr[0:halo, :] = cur[L - halo:L, :]
    qk = acc * jax.nn.sigmoid(acc)
    q_all = (qk[:, :ML_QK_W] * (ML_DQK ** -0.5)).astype(BF16)
    k_all = qk[:, ML_QK_W:]

    gcol = sm_ref[0] + brow_ref[...]
    grow = smT_ref[0][0:SUBLANES, :] + bcol_ref[...]
    ri = lax.broadcasted_iota(I32, (L, L), 0)
    ci = lax.broadcasted_iota(I32, (L, L), 1)
    causal = ci <= ri
    tril = causal.astype(F32)
    triu = (ri <= ci).astype(F32)
    hp = lax.Precision.HIGHEST
    b_col = jnp.dot(tril, _log_sigmoid(gcol), precision=hp, preferred_element_type=F32)
    b_row = jnp.dot(_log_sigmoid(grow), triu, precision=hp, preferred_element_type=F32)

    v_all = v_ref[0]
    o_all = o_ref[0]
    lane = lax.broadcasted_iota(I32, (L, LANES), 1)
    ones_col = jnp.where(lane == 0, 1.0, 0.0).astype(BF16)
    for h in range(ML_HEADS):
        qh = q_all[:, h * ML_DQK:(h + 1) * ML_DQK]
        kh = k_all[:, h * ML_DQK:(h + 1) * ML_DQK]
        vh = v_all[:, h * ML_DV:(h + 1) * ML_DV]
        bj = b_col[:, ML_HEADS + h:ML_HEADS + h + 1]
        icol = gcol[:, h:h + 1]
        r_row = grow[h:h + 1, :] - b_row[ML_HEADS + h:ML_HEADS + h + 1, :]
        dlog = jnp.where(causal, bj + r_row, NEG_INF)
        m0 = m_scr[h:h + 1, 0:1]
        inter = bj + m0
        m_t = jnp.maximum(inter, jnp.max(dlog, axis=-1, keepdims=True))
        w_inter = jnp.exp(inter - m_t)
        qk_s = lax.dot_general(qh, kh.astype(BF16), _NT, preferred_element_type=F32)
        w_intra = jnp.exp(dlog - m_t) * qk_s
        st = st_scr[h]
        cq = jnp.dot(qh, st.astype(BF16), preferred_element_type=F32)
        num = w_inter * cq[:, :ML_DV] + jnp.dot(w_intra.astype(BF16), vh, preferred_element_type=F32)
        den = w_inter * cq[:, ML_DV:ML_DV + 1] + jnp.sum(w_intra, axis=-1, keepdims=True)
        hh = num / jnp.maximum(jnp.abs(den), jnp.exp(-m_t))
        hh = hh * lax.rsqrt(jnp.mean(hh * hh, axis=-1, keepdims=True) + EPS)
        hh = hh * hn_ref[:, h * ML_DV:(h + 1) * ML_DV]
        y_ref[0, :, h * ML_DV:(h + 1) * ML_DV] = (
            jax.nn.sigmoid(o_all[:, h * ML_DV:(h + 1) * ML_DV]) * hh).astype(y_ref.dtype)

        b_last = bj[L - 1:L, :]
        a = b_last - bj + icol
        m_loc = jnp.max(a, axis=0, keepdims=True)
        wa = jnp.exp(a - m_loc)
        kw = (kh * wa).astype(BF16)
        vaug = jnp.concatenate([vh, ones_col], axis=1)
        c_loc = lax.dot_general(kw, vaug, _TN, preferred_element_type=F32)
        m_new = jnp.maximum(b_last + m0, m_loc)
        s_old = jnp.exp(b_last + m0 - m_new)
        s_loc = jnp.exp(m_loc - m_new)
        st_scr[h] = s_old * st + s_loc * c_loc
        m_scr[h:h + 1, :] = jnp.broadcast_to(m_new, (1, LANES))


def _mlstm(qk, v, o, sm, smT, conv_w, i_bias, f_bias, head_norm, B, S):
    L = min(ML_CHUNK, S)
    nc = S // L
    qk = qk.reshape(B, S, 2 * ML_QK_W)
    v = v.reshape(B, S, ML_V_W)
    o = o.reshape(B, S, ML_V_W)
    sm = sm.reshape(B, S, LANES)
    bias = jnp.concatenate([i_bias, f_bias]).astype(F32)
    brow = jnp.zeros((1, LANES), F32).at[0, :2 * ML_HEADS].set(bias)
    bcol = bias.reshape(2 * ML_HEADS, 1)
    hn = head_norm.reshape(1, ML_V_W).astype(F32)
    blk = lambda n: pl.BlockSpec((1, L, n), lambda b, c: (b, c, 0))
    full = lambda a: pl.BlockSpec(a.shape, lambda b, c: (0,) * a.ndim)
    return pl.pallas_call(
        functools.partial(_mlstm_kernel, L=L),
        grid=(B, nc),
        in_specs=[blk(2 * ML_QK_W), blk(ML_V_W), blk(ML_V_W), blk(LANES),
                  pl.BlockSpec((1, 16, L), lambda b, c: (b, 0, c)),
                  full(conv_w), full(brow), full(bcol), full(hn)],
        out_specs=blk(ML_V_W),
        out_shape=jax.ShapeDtypeStruct((B, S, ML_V_W), BF16),
        scratch_shapes=[pltpu.VMEM((L + SUBLANES, 2 * ML_QK_W), F32),
                        pltpu.VMEM((ML_HEADS, ML_DQK, ML_DV + LANES), F32),
                        pltpu.VMEM((SUBLANES, LANES), F32)],
        compiler_params=_params("parallel", "arbitrary"),
    )(qk, v, o, sm, smT, conv_w.astype(F32), brow, bcol, hn)


def _block_diag_T(qT, heads, dh):
    rid = lax.broadcasted_iota(I32, qT.shape, 0)
    zero = jnp.zeros_like(qT)
    return jnp.concatenate(
        [jnp.where((rid >= h * dh) & (rid < (h + 1) * dh), qT, zero) for h in range(heads)], axis=1)


def _scores(k_chunk, qs, h, TQ):
    return jnp.dot(k_chunk, qs[:, h * TQ:(h + 1) * TQ], preferred_element_type=F32)


def _accumulate(h, p, vT_chunk, l, acc_scr, dh):
    rows = slice(h * dh, (h + 1) * dh)
    acc_scr[rows, :] += jnp.dot(vT_chunk[rows, :], p.astype(BF16), preferred_element_type=F32)
    return l + jnp.sum(p, axis=0, keepdims=True)


def _finish_attention(l, acc_scr, y_ref, heads, dh):
    outs = [acc_scr[h * dh:(h + 1) * dh, :] / l[h] for h in range(heads)]
    y_ref[0] = jnp.concatenate(outs, axis=0).T.astype(y_ref.dtype)


def _moba_kernel(qT_ref, k_ref, vT_ref, y_ref, km_scr, acc_scr, *, NB, NBP, n_sel):
    H, DH, BS = MOBA_HEADS, MOBA_DH, MOBA_BLOCK
    TQ = BS
    qi = pl.program_id(1)

    @pl.when(qi == 0)
    def _():
        km_scr[...] = jnp.zeros_like(km_scr)

        def body(j, carry):
            kb = k_ref[0, pl.ds(pl.multiple_of(j * BS, BS), BS), :].astype(F32)
            km_scr[pl.ds(j, 1), :] = jnp.mean(kb, axis=0, keepdims=True)
            return carry
        lax.fori_loop(0, NB, body, 0)

    qT = qT_ref[0]
    qbd = _block_diag_T(qT, H, DH)
    gate = jnp.dot(km_scr[...], qbd.astype(F32), precision=lax.Precision.HIGHEST,
                   preferred_element_type=F32)
    blk = lax.broadcasted_iota(I32, gate.shape, 0)
    g = jnp.where(blk < qi, gate, NEG_INF)
    sels = []
    for _ in range(n_sel):
        mx = jnp.max(g, axis=0, keepdims=True)
        isel = jnp.min(jnp.where(g == mx, blk, NBP), axis=0, keepdims=True)
        sels.append(jnp.where(mx > NEG_INF, isel, -1))
        g = jnp.where(blk == isel, NEG_INF, g)

    qs = (qbd * (DH ** -0.5)).astype(BF16)

    def hit_row(j):
        hit = sels[0] == j
        for s in sels[1:]:
            hit = hit | (s == j)
        return hit

    npair = (qi + 1) // 2
    pair_start = lambda g: pl.multiple_of(g * (2 * BS), 2 * BS)
    hcols = lambda row, h: row[:, h * TQ:(h + 1) * TQ]
    diag_start = pl.multiple_of(qi * BS, BS)
    causal = lax.broadcasted_iota(I32, (BS, TQ), 0) <= lax.broadcasted_iota(I32, (BS, TQ), 1)

    def max_pair(g, m):
        k_pair = k_ref[0, pl.ds(pair_start(g), 2 * BS), :]
        for u in range(2):
            s_all = jnp.dot(k_pair[u * BS:(u + 1) * BS], qs, preferred_element_type=F32)
            m = jnp.maximum(m, jnp.where(hit_row(2 * g + u), jnp.max(s_all, axis=0, keepdims=True), NEG_INF))
        return m

    m_all = lax.fori_loop(0, npair, max_pair, jnp.full((1, H * TQ), NEG_INF, F32))
    k_diag = k_ref[0, pl.ds(diag_start, BS), :]
    s_diag = [jnp.where(causal, _scores(k_diag, qs, h, TQ), NEG_INF) for h in range(H)]
    m = [jnp.maximum(hcols(m_all, h), jnp.max(s_diag[h], axis=0, keepdims=True))
         for h in range(H)]

    acc_scr[...] = jnp.zeros_like(acc_scr)

    def sum_pair(g, l):
        k_pair = k_ref[0, pl.ds(pair_start(g), 2 * BS), :]
        vT_pair = vT_ref[0, :, pl.ds(pair_start(g), 2 * BS)]
        out = list(l)
        for u in range(2):
            hit = hit_row(2 * g + u)
            s_all = jnp.dot(k_pair[u * BS:(u + 1) * BS], qs, preferred_element_type=F32)
            for h in range(H):
                p = jnp.exp(hcols(s_all, h) - jnp.where(hcols(hit, h), m[h], jnp.inf))
                out[h] = _accumulate(h, p, vT_pair[:, u * BS:(u + 1) * BS], out[h], acc_scr, DH)
        return tuple(out)

    l = lax.fori_loop(0, npair, sum_pair, (jnp.zeros((1, TQ), F32),) * H)
    vT_diag = vT_ref[0, :, pl.ds(diag_start, BS)]
    l = [_accumulate(h, jnp.exp(s_diag[h] - m[h]), vT_diag, l[h], acc_scr, DH) for h in range(H)]
    _finish_attention(l, acc_scr, y_ref, H, DH)


def _moba(qT, k, vT, B, S):
    BS = MOBA_BLOCK
    assert S % BS == 0
    NB = S // BS
    NBP = -(-NB // SUBLANES) * SUBLANES
    n_sel = max(min(MOBA_TOPK, NB - 1), 1)
    k = k.reshape(B, S, MOBA_W)
    return pl.pallas_call(
        functools.partial(_moba_kernel, NB=NB, NBP=NBP, n_sel=n_sel),
        grid=(B, NB),
        in_specs=[pl.BlockSpec((1, MOBA_W, BS), lambda b, i: (b, 0, i)),
                  pl.BlockSpec((1, S, MOBA_W), lambda b, i: (b, 0, 0)),
                  pl.BlockSpec((1, MOBA_W, S), lambda b, i: (b, 0, 0))],
        out_specs=pl.BlockSpec((1, BS, MOBA_W), lambda b, i: (b, i, 0)),
        out_shape=jax.ShapeDtypeStruct((B, S, MOBA_W), BF16),
        scratch_shapes=[pltpu.VMEM((NBP, MOBA_W), F32),
                        pltpu.VMEM((MOBA_W, BS), F32)],
        compiler_params=_params("parallel", "arbitrary"),
    )(qT, k, vT)


def _dsa_kernel(qT_ref, k_ref, vT_ref, iqT_ref, ik_ref, smT_ref, y_ref, e_scr, gmax_scr, acc_scr, thr_scr,
                *, TQ, topk, spow):
    H, DH = DSA_HEADS, DSA_DH
    KC = TQ
    qi = pl.program_id(1)
    nch = qi + 1
    npair = (nch + 1) // 2
    chunk_start = lambda c: pl.multiple_of(c * KC, KC)
    sum8 = lambda b: jnp.sum(b.astype(I32).reshape(KC // SUBLANES, SUBLANES, TQ), axis=0)
    any_row = lambda r: jnp.max(r.astype(I32)) > 0

    iqT = iqT_ref[0]
    iq_cat = jnp.concatenate([iqT[h * IDX_DH:(h + 1) * IDX_DH, :] for h in range(IDX_HEADS)], axis=1)
    iw = smT_ref[0][2 * ML_HEADS:2 * ML_HEADS + IDX_HEADS, :] * (IDX_HEADS ** -0.5 * IDX_DH ** -0.5)
    t_pos = qi * TQ + lax.broadcasted_iota(I32, (KC, TQ), 1)
    s_off = lax.broadcasted_iota(I32, (KC, TQ), 0)
    gmax_scr[...] = jnp.full((KC, TQ), NEG_INF, F32)

    def score_pair(g, carry):
        n_pos, n_nonneg = carry
        for u in range(2):
            c = 2 * g + u
            ikc = ik_ref[0, pl.ds(chunk_start(c), KC), :][:, :IDX_DH]
            rel = jnp.dot(ikc, iq_cat, preferred_element_type=F32)
            sc = iw[0:1, :] * jnp.maximum(rel[:, 0:TQ], 0.0)
            for h in range(1, IDX_HEADS):
                sc = sc + iw[h:h + 1, :] * jnp.maximum(rel[:, h * TQ:(h + 1) * TQ], 0.0)
            sc = jnp.where(c * KC + s_off <= t_pos, sc, NEG_INF)
            e_scr[pl.ds(chunk_start(c), KC), :] = sc
            gmax_scr[...] = jnp.maximum(gmax_scr[...], sc)
            n_pos = n_pos + sum8(sc > 0.0)
            n_nonneg = n_nonneg + sum8(sc >= 0.0)
        return n_pos, n_nonneg
    zero8 = jnp.zeros((SUBLANES, TQ), I32)
    n_pos, n_nonneg = lax.fori_loop(0, npair, score_pair, (zero8, zero8))
    n_pos = jnp.sum(n_pos, axis=0, keepdims=True)
    n_nonneg = jnp.sum(n_nonneg, axis=0, keepdims=True)

    def count_ge(cand):
        def body(g, acc):
            for u in range(2):
                acc = acc + sum8(e_scr[pl.ds(chunk_start(2 * g + u), KC), :] >= cand)
            return acc
        return jnp.sum(lax.fori_loop(0, npair, body, zero8), axis=0, keepdims=True)

    def rewrite(rows, thr):
        def body(c, carry):
            x = e_scr[pl.ds(chunk_start(c), KC), :]
            rank = (2 * spow - (c * KC + s_off)).astype(F32)
            new = jnp.where(x > thr, jnp.inf, jnp.where(x == thr, rank, NEG_INF))
            e_scr[pl.ds(chunk_start(c), KC), :] = jnp.where(rows, new, x)
            return carry
        lax.fori_loop(0, nch, body, 0)

    def bisect(lo, hi, act):
        def cond(c):
            return jnp.logical_and(c[0] < BISECT_CAP, c[5] > 0)

        def body(c):
            it, lo, hi, act, tie, _ = c
            mid = lo + (hi - lo) * 0.5
            inside = (mid > lo) & (mid < hi)
            cnt = count_ge(mid)
            live = act > 0
            run = live & inside
            up = cnt >= topk
            lo = jnp.where(run & up, mid, lo)
            hi = jnp.where(run & jnp.logical_not(up), mid, hi)
            collapsed = live & jnp.logical_not(inside)
            tie = jnp.where(collapsed, 1, tie)
            act = jnp.where(collapsed | (run & (cnt == topk)), 0, act)
            return it + 1, lo, hi, act, tie, jnp.max(act)
        out = lax.while_loop(cond, body, (jnp.int32(0), lo, hi, act, jnp.zeros_like(act), jnp.max(act)))
        return out[1], out[4]

    fmax = float(jnp.finfo(F32).max)
    gm = gmax_scr[...]
    g_lo = jnp.min(gm, axis=0, keepdims=True)
    g_hi = jnp.max(gm, axis=0, keepdims=True)
    n_vis = qi * TQ + lax.broadcasted_iota(I32, (1, TQ), 1) + 1
    few = n_vis <= topk
    zero_tie = (n_pos < topk) & (n_nonneg >= topk) & jnp.logical_not(few)

    @pl.when(any_row(zero_tie))
    def _():
        rewrite(zero_tie, jnp.zeros((1, TQ), F32))

    rank_lo, rank_hi = float(spow), float(2 * spow + 1)
    lo0 = jnp.where(zero_tie, rank_lo, jnp.maximum(g_lo, -fmax))
    hi0 = jnp.where(zero_tie, rank_hi, g_hi + jnp.abs(g_hi) * (2.0 ** -10) + 1e-30)
    lo, tie = bisect(lo0, hi0, jnp.where(few, 0, 1).astype(I32))
    thr_scr[0:1, :] = jnp.where(few, -fmax, lo)

    @pl.when(jnp.max(tie) > 0)
    def _():
        rows = tie > 0
        rewrite(rows, lo)
        lo2, _ = bisect(jnp.full((1, TQ), rank_lo, F32), jnp.full((1, TQ), rank_hi, F32), tie)
        thr_scr[0:1, :] = jnp.where(rows, lo2, thr_scr[0:1, :])

    thr = thr_scr[0:1, :]
    qs = (_block_diag_T(qT_ref[0], H, DH) * (DH ** -0.5)).astype(BF16)

    hcols = lambda row, h: row[:, h * TQ:(h + 1) * TQ]

    def chunk_scores(c):
        start = chunk_start(c)
        bias = jnp.where(e_scr[pl.ds(start, KC), :] >= thr, 0.0, NEG_INF)
        s_all = jnp.dot(k_ref[0, pl.ds(start, KC), :], qs, preferred_element_type=F32)
        return [hcols(s_all, h) + bias for h in range(H)]

    def max_pair(g, m):
        for u in range(2):
            s = chunk_scores(2 * g + u)
            m = tuple(jnp.maximum(m[h], jnp.max(s[h], axis=0, keepdims=True)) for h in range(H))
        return m
    m = lax.fori_loop(0, npair, max_pair, (jnp.full((1, TQ), NEG_INF, F32),) * H)
    m = [jnp.where(mh == NEG_INF, 0.0, mh) for mh in m]

    acc_scr[...] = jnp.zeros_like(acc_scr)

    def sum_pair(g, l):
        for u in range(2):
            c = 2 * g + u
            s = chunk_scores(c)
            vT_chunk = vT_ref[0, :, pl.ds(chunk_start(c), KC)]
            l = tuple(_accumulate(h, jnp.exp(s[h] - m[h]), vT_chunk, l[h], acc_scr, DH) for h in range(H))
        return l
    l = lax.fori_loop(0, npair, sum_pair, (jnp.zeros((1, TQ), F32),) * H)
    _finish_attention(l, acc_scr, y_ref, H, DH)


def _dsa(qT, k, vT, iqT, ik, smT, B, S):
    TQ = min(ATT_TQ, S)
    topk = min(DSA_TOPK_MAX, S // 4)
    assert S % (2 * TQ) == 0 and topk <= TQ
    spow = 1 << max((S - 1).bit_length(), 1)
    k = k.reshape(B, S, DSA_W)
    ik = ik.reshape(B, S, LANES)
    qblk = lambda n: pl.BlockSpec((1, n, TQ), lambda b, i: (b, 0, i))
    return pl.pallas_call(
        functools.partial(_dsa_kernel, TQ=TQ, topk=topk, spow=spow),
        grid=(B, S // TQ),
        in_specs=[qblk(DSA_W),
                  pl.BlockSpec((1, S, DSA_W), lambda b, i: (b, 0, 0)),
                  pl.BlockSpec((1, DSA_W, S), lambda b, i: (b, 0, 0)),
                  qblk(IDX_HEADS * IDX_DH),
                  pl.BlockSpec((1, S, LANES), lambda b, i: (b, 0, 0)),
                  qblk(16)],
        out_specs=pl.BlockSpec((1, TQ, DSA_W), lambda b, i: (b, i, 0)),
        out_shape=jax.ShapeDtypeStruct((B, S, DSA_W), BF16),
        scratch_shapes=[pltpu.VMEM((S, TQ), F32),
                        pltpu.VMEM((TQ, TQ), F32),
                        pltpu.VMEM((DSA_W, TQ), F32),
                        pltpu.VMEM((SUBLANES, TQ), F32)],
        compiler_params=_params("parallel", "arbitrary"),
    )(qT, k, vT, iqT, ik, smT)


def _out_mlp_kernel(yml_ref, ymb_ref, yds_ref, x_ref, wout_ref, gpost_ref, gpre_ref,
                    w1_ref, w2_ref, g2_ref, out_ref, x1_scr, h_scr, acc_scr):
    f = pl.program_id(1)

    @pl.when(f == 0)
    def _():
        mix = jnp.dot(yml_ref[...], wout_ref[0:ML_V_W, :], preferred_element_type=F32)
        mix = mix + jnp.dot(ymb_ref[...], wout_ref[ML_V_W:ML_V_W + MOBA_W, :], preferred_element_type=F32)
        mix = mix + jnp.dot(yds_ref[...], wout_ref[ML_V_W + MOBA_W:, :], preferred_element_type=F32)
        x1 = x_ref[...] + _rms(mix, gpost_ref[...])
        x1_scr[...] = x1
        h_scr[...] = _rms(x1, gpre_ref[...]).astype(BF16)
        acc_scr[...] = jnp.zeros_like(acc_scr)

    u = jnp.maximum(jnp.dot(h_scr[...], w1_ref[...], preferred_element_type=F32), 0.0)
    acc_scr[...] += jnp.dot((u * u).astype(BF16), w2_ref[...], preferred_element_type=F32)

    @pl.when(f == pl.num_programs(1) - 1)
    def _():
        out_ref[...] = x1_scr[...] + _rms(acc_scr[...], g2_ref[...])


def _out_mlp(yml, ymb, yds, x2, w_out, g_post, g_pre, w1, w2, g2):
    T, D = x2.shape
    tm = min(MLP_TM, T)
    tf = min(MLP_TF, D_FF)
    row = lambda n: pl.BlockSpec((tm, n), lambda i, f: (i, 0))
    full = lambda a: pl.BlockSpec(a.shape, lambda i, f: (0,) * a.ndim)
    return pl.pallas_call(
        _out_mlp_kernel,
        grid=(T // tm, D_FF // tf),
        in_specs=[row(ML_V_W), row(MOBA_W), row(DSA_W), row(D), full(w_out), full(g_post), full(g_pre),
                  pl.BlockSpec((D, tf), lambda i, f: (0, f)),
                  pl.BlockSpec((tf, D), lambda i, f: (f, 0)),
                  full(g2)],
        out_specs=row(D),
        out_shape=jax.ShapeDtypeStruct((T, D), F32),
        scratch_shapes=[pltpu.VMEM((tm, D), F32), pltpu.VMEM((tm, D), BF16), pltpu.VMEM((tm, D), F32)],
        compiler_params=_params("parallel", "arbitrary"),
    )(yml, ymb, yds, x2, w_out, g_post, g_pre, w1, w2, g2)


def kernel(x, norm_mix_pre, w_in, ml_conv, ml_i_bias, ml_f_bias, ml_head_norm, w_out, norm_mix_post,
           norm_mlp_pre, w_ff1, w_ff2, norm_mlp_post):
    B, S, D = x.shape
    depth = w_in.shape[0]
    x2 = x.reshape(B * S, D)
    gain = lambda g: g.reshape(1, D).astype(F32)
    for l in range(depth):
        wn, wt = _prep_in_weights(w_in[l])
        (qk, v, o, sm, mbk, dsk, ixk, mbqT, mbvT, dsqT, dsvT, ixqT, smT) = _in_proj(
            x2, gain(norm_mix_pre[l]), wn, wt, B, S)
        y_ml = _mlstm(qk, v, o, sm, smT, ml_conv[l], ml_i_bias[l], ml_f_bias[l], ml_head_norm[l], B, S)
        y_mb = _moba(mbqT, mbk, mbvT, B, S)
        y_ds = _dsa(dsqT, dsk, dsvT, ixqT, ixk, smT, B, S)
        x2 = _out_mlp(y_ml.reshape(B * S, ML_V_W), y_mb.reshape(B * S, MOBA_W), y_ds.reshape(B * S, DSA_W),
                      x2, w_out[l].astype(BF16), gain(norm_mix_post[l]), gain(norm_mlp_pre[l]),
                      w_ff1[l].astype(BF16), w_ff2[l].astype(BF16), gain(norm_mlp_post[l]))
    return x2.reshape(B, S, D)
```

```python
import functools

import jax
import jax.numpy as jnp
from jax import lax
from jax.experimental import pallas as pl
from jax.experimental.pallas import tpu as pltpu

F32 = jnp.float32
BF16 = jnp.bfloat16
I32 = jnp.int32

D_MODEL = 1024
ML_HEADS, ML_DQK, ML_DV, ML_CONV = 4, 64, 128, 4
MOBA_HEADS, MOBA_DH, MOBA_BLOCK, MOBA_TOPK = 4, 64, 256, 3
DSA_HEADS, DSA_DH, IDX_HEADS, IDX_DH, DSA_TOPK_MAX = 4, 64, 4, 64, 256
D_FF = 4 * D_MODEL
EPS = 1e-6

ML_QK_W = ML_HEADS * ML_DQK
ML_V_W = ML_HEADS * ML_DV
MOBA_W = MOBA_HEADS * MOBA_DH
DSA_W = DSA_HEADS * DSA_DH
IN_SPLITS = (ML_QK_W, ML_QK_W, ML_V_W, ML_V_W, ML_HEADS, ML_HEADS,
             MOBA_W, MOBA_W, MOBA_W,
             DSA_W, DSA_W, DSA_W, IDX_HEADS * IDX_DH, IDX_DH, IDX_HEADS)

LANES = 128
SUBLANES = 8
VMEM_LIMIT = 52 * 1024 * 1024

ML_CHUNK = 256
ATT_TQ = 256
PROJ_TM = 512
MLP_TM = 512
MLP_TF = 1024

NEG_INF = float("-inf")
BISECT_CAP = 300
BISECT_STEPS_PER_CHECK = 3
SOFTMAX_L_MIN = 1e-30

_NT = (((1,), (1,)), ((), ()))
_TN = (((0,), (0,)), ((), ()))


def _rms(x, g):
    return x * lax.rsqrt(jnp.mean(x * x, axis=-1, keepdims=True) + EPS) * g


def _log_sigmoid(x):
    return jnp.minimum(x, 0.0) - jnp.log1p(jnp.exp(-jnp.abs(x)))


def _params(*sem):
    return pltpu.CompilerParams(dimension_semantics=sem, vmem_limit_bytes=VMEM_LIMIT)


_N_QK, _N_V, _N_O, _N_MBK, _N_DSK, _N_IXK, _N_SM = 0, 512, 1024, 1536, 1792, 2048, 2176
_N_TOTAL = 2304
_T_MBQ, _T_MBV, _T_DSQ, _T_DSV, _T_IXQ, _T_SM = 0, 256, 512, 768, 1024, 1280
_T_TOTAL = 1296


def _in_proj_kernel(x_ref, g_ref, wn_ref, wt_ref,
                    qk_ref, v_ref, o_ref, sm_ref, mbk_ref, dsk_ref, ixk_ref,
                    mbqT_ref, mbvT_ref, dsqT_ref, dsvT_ref, ixqT_ref, smT_ref):
    h = _rms(x_ref[...], g_ref[...]).astype(BF16)

    def mm(a, n):
        return jnp.dot(h, wn_ref[:, a:a + n], preferred_element_type=F32)

    def mt(a, n):
        return lax.dot_general(wt_ref[a:a + n, :], h, _NT, preferred_element_type=F32)

    qk_ref[...] = mm(_N_QK, 512)
    v_ref[...] = mm(_N_V, 512).astype(BF16)
    o_ref[...] = mm(_N_O, 512)
    mbk_ref[...] = mm(_N_MBK, 256).astype(BF16)
    dsk_ref[...] = mm(_N_DSK, 256).astype(BF16)
    ixk_ref[...] = mm(_N_IXK, 128).astype(BF16)
    sm_ref[...] = mm(_N_SM, 128)
    mbqT_ref[0] = mt(_T_MBQ, 256).astype(BF16)
    mbvT_ref[0] = mt(_T_MBV, 256).astype(BF16)
    dsqT_ref[0] = mt(_T_DSQ, 256).astype(BF16)
    dsvT_ref[0] = mt(_T_DSV, 256).astype(BF16)
    ixqT_ref[0] = mt(_T_IXQ, 256).astype(BF16)
    smT_ref[0] = mt(_T_SM, 16)


def _prep_in_weights(w_in):
    pts, acc = [], 0
    for n in IN_SPLITS:
        pts.append((acc, acc + n))
        acc += n
    col = lambda i: w_in[:, pts[i][0]:pts[i][1]]
    (ml_q, ml_k, ml_v, ml_o, ml_i, ml_f, mb_q, mb_k, mb_v,
     ds_q, ds_k, ds_v, ix_q, ix_k, ix_w) = [col(i) for i in range(len(IN_SPLITS))]
    d = w_in.shape[0]
    small = jnp.concatenate([ml_i, ml_f, ix_w], axis=1)
    wn = jnp.concatenate([
        ml_q, ml_k, ml_v, ml_o, mb_k, ds_k,
        ix_k, jnp.zeros((d, 128 - IDX_DH), w_in.dtype),
        small, jnp.zeros((d, 128 - small.shape[1]), w_in.dtype)], axis=1)
    wt = jnp.concatenate([
        mb_q, mb_v, ds_q, ds_v, ix_q,
        small, jnp.zeros((d, 16 - small.shape[1]), w_in.dtype)], axis=1).T
    assert wn.shape == (d, _N_TOTAL) and wt.shape == (_T_TOTAL, d)
    return wn.astype(BF16), wt.astype(BF16)


def _in_proj(x2, g, wn, wt, B, S):
    T, D = x2.shape
    tm = min(PROJ_TM, S)
    nsb = S // tm
    row = lambda n: pl.BlockSpec((tm, n), lambda i: (i, 0))
    tr = lambda n: pl.BlockSpec((1, n, tm), lambda i: (i // nsb, 0, i % nsb))
    full = lambda a: pl.BlockSpec(a.shape, lambda i: (0,) * a.ndim)
    out_shape = (
        jax.ShapeDtypeStruct((T, 512), F32),
        jax.ShapeDtypeStruct((T, 512), BF16),
        jax.ShapeDtypeStruct((T, 512), F32),
        jax.ShapeDtypeStruct((T, 128), F32),
        jax.ShapeDtypeStruct((T, 256), BF16),
        jax.ShapeDtypeStruct((T, 256), BF16),
        jax.ShapeDtypeStruct((T, 128), BF16),
        jax.ShapeDtypeStruct((B, 256, S), BF16),
        jax.ShapeDtypeStruct((B, 256, S), BF16),
        jax.ShapeDtypeStruct((B, 256, S), BF16),
        jax.ShapeDtypeStruct((B, 256, S), BF16),
        jax.ShapeDtypeStruct((B, 256, S), BF16),
        jax.ShapeDtypeStruct((B, 16, S), F32),
    )
    out_specs = (row(512), row(512), row(512), row(128), row(256), row(256), row(128),
                 tr(256), tr(256), tr(256), tr(256), tr(256), tr(16))
    return pl.pallas_call(
        _in_proj_kernel,
        grid=(T // tm,),
        in_specs=[row(D), full(g), full(wn), full(wt)],
        out_specs=out_specs,
        out_shape=out_shape,
        compiler_params=_params("parallel"),
    )(x2, g, wn, wt)


def _mlstm_kernel(qk_ref, v_ref, o_ref, sm_ref, smT_ref, conv_ref, brow_ref, bcol_ref, hn_ref,
                  y_ref, xp_scr, st_scr, m_scr, *, L):
    c = pl.program_id(1)
    halo = SUBLANES

    @pl.when(c == 0)
    def _():
        xp_scr[0:halo, :] = jnp.zeros((halo, 2 * ML_QK_W), F32)
        st_scr[...] = jnp.zeros_like(st_scr)
        m_scr[...] = jnp.zeros_like(m_scr)

    cur = qk_ref[0]
    xp_scr[halo:halo + L, :] = cur
    base = halo - (ML_CONV - 1)
    acc = conv_ref[0:1, :] * xp_scr[base:base + L, :]
    for j in range(1, ML_CONV):
        acc = acc + conv_ref[j:j + 1, :] * xp_scr[base + j:base + j + L, :]
    xp_scr[0:halo, :] = cur[L - halo:L, :]
    qk = acc * jax.nn.sigmoid(acc)
    q_all = (qk[:, :ML_QK_W] * (ML_DQK ** -0.5)).astype(BF16)
    k_all = qk[:, ML_QK_W:]

    gcol = sm_ref[0] + brow_ref[...]
    grow = smT_ref[0][0:SUBLANES, :] + bcol_ref[...]
    ri = lax.broadcasted_iota(I32, (L, L), 0)
    ci = lax.broadcasted_iota(I32, (L, L), 1)
    causal = ci <= ri
    tril = causal.astype(F32)
    triu = (ri <= ci).astype(F32)
    hp = lax.Precision.HIGHEST
    b_col = jnp.dot(tril, _log_sigmoid(gcol), precision=hp, preferred_element_type=F32)
    b_row = jnp.dot(_log_sigmoid(grow), triu, precision=hp, preferred_element_type=F32)

    v_all = v_ref[0]
    o_all = o_ref[0]
    lane = lax.broadcasted_iota(I32, (L, LANES), 1)
    ones_col = jnp.where(lane == 0, 1.0, 0.0).astype(BF16)
    for h in range(ML_HEADS):
        qh = q_all[:, h * ML_DQK:(h + 1) * ML_DQK]
        kh = k_all[:, h * ML_DQK:(h + 1) * ML_DQK]
        vh = v_all[:, h * ML_DV:(h + 1) * ML_DV]
        bj = b_col[:, ML_HEADS + h:ML_HEADS + h + 1]
        icol = gcol[:, h:h + 1]
        r_row = grow[h:h + 1, :] - b_row[ML_HEADS + h:ML_HEADS + h + 1, :]
        dlog = jnp.where(causal, bj + r_row, NEG_INF)
        m0 = m_scr[h:h + 1, 0:1]
        inter = bj + m0
        m_t = jnp.maximum(inter, jnp.max(dlog, axis=-1, keepdims=True))
        w_inter = jnp.exp(inter - m_t)
        qk_s = lax.dot_general(qh, kh.astype(BF16), _NT, preferred_element_type=F32)
        w_intra = jnp.exp(dlog - m_t) * qk_s
        st = st_scr[h]
        cq = jnp.dot(qh, st.astype(BF16), preferred_element_type=F32)
        num = w_inter * cq[:, :ML_DV] + jnp.dot(w_intra.astype(BF16), vh, preferred_element_type=F32)
        den = w_inter * cq[:, ML_DV:ML_DV + 1] + jnp.sum(w_intra, axis=-1, keepdims=True)
        hh = num / jnp.maximum(jnp.abs(den), jnp.exp(-m_t))
        hh = hh * lax.rsqrt(jnp.mean(hh * hh, axis=-1, keepdims=True) + EPS)
        hh = hh * hn_ref[:, h * ML_DV:(h + 1) * ML_DV]
        y_ref[0, :, h * ML_DV:(h + 1) * ML_DV] = (
            jax.nn.sigmoid(o_all[:, h * ML_DV:(h + 1) * ML_DV]) * hh).astype(y_ref.dtype)

        b_last = bj[L - 1:L, :]
        a = b_last - bj + icol
        m_loc = jnp.max(a, axis=0, keepdims=True)
        wa = jnp.exp(a - m_loc)
        kw = (kh * wa).astype(BF16)
        vaug = jnp.concatenate([vh, ones_col], axis=1)
        c_loc = lax.dot_general(kw, vaug, _TN, preferred_element_type=F32)
        m_new = jnp.maximum(b_last + m0, m_loc)
        s_old = jnp.exp(b_last + m0 - m_new)
        s_loc = jnp.exp(m_loc - m_new)
        st_scr[h] = s_old * st + s_loc * c_loc
        m_scr[h:h + 1, :] = jnp.broadcast_to(m_new, (1, LANES))


def _mlstm(qk, v, o, sm, smT, conv_w, i_bias, f_bias, head_norm, B, S):
    L = min(ML_CHUNK, S)
    nc = S // L
    qk = qk.reshape(B, S, 2 * ML_QK_W)
    v = v.reshape(B, S, ML_V_W)
    o = o.reshape(B, S, ML_V_W)
    sm = sm.reshape(B, S, LANES)
    bias = jnp.concatenate([i_bias, f_bias]).astype(F32)
    brow = jnp.zeros((1, LANES), F32).at[0, :2 * ML_HEADS].set(bias)
    bcol = bias.reshape(2 * ML_HEADS, 1)
    hn = head_norm.reshape(1, ML_V_W).astype(F32)
    blk = lambda n: pl.BlockSpec((1, L, n), lambda b, c: (b, c, 0))
    full = lambda a: pl.BlockSpec(a.shape, lambda b, c: (0,) * a.ndim)
    return pl.pallas_call(
        functools.partial(_mlstm_kernel, L=L),
        grid=(B, nc),
        in_specs=[blk(2 * ML_QK_W), blk(ML_V_W), blk(ML_V_W), blk(LANES),
                  pl.BlockSpec((1, 16, L), lambda b, c: (b, 0, c)),
                  full(conv_w), full(brow), full(bcol), full(hn)],
        out_specs=blk(ML_V_W),
        out_shape=jax.ShapeDtypeStruct((B, S, ML_V_W), BF16),
        scratch_shapes=[pltpu.VMEM((L + SUBLANES, 2 * ML_QK_W), F32),
                        pltpu.VMEM((ML_HEADS, ML_DQK, ML_DV + LANES), F32),
                        pltpu.VMEM((SUBLANES, LANES), F32)],
        compiler_params=_params("parallel", "arbitrary"),
    )(qk, v, o, sm, smT, conv_w.astype(F32), brow, bcol, hn)


def _block_diag_T(qT, heads, dh):
    rid = lax.broadcasted_iota(I32, qT.shape, 0)
    zero = jnp.zeros_like(qT)
    return jnp.concatenate(
        [jnp.where((rid >= h * dh) & (rid < (h + 1) * dh), qT, zero) for h in range(heads)], axis=1)


def _scores(k_chunk, qs, h, TQ):
    return jnp.dot(k_chunk, qs[:, h * TQ:(h + 1) * TQ], preferred_element_type=F32)


def _accumulate(h, p, vT_chunk, l, acc_scr, dh):
    rows = slice(h * dh, (h + 1) * dh)
    acc_scr[rows, :] += jnp.dot(vT_chunk[rows, :], p.astype(BF16), preferred_element_type=F32)
    return l + jnp.sum(p, axis=0, keepdims=True)


def _finish_attention(l, acc_scr, y_ref, heads, dh):
    outs = [acc_scr[h * dh:(h + 1) * dh, :] / l[h] for h in range(heads)]
    y_ref[0] = jnp.concatenate(outs, axis=0).T.astype(y_ref.dtype)


def _max_key_norm2(kb, heads, dh, TQ):
    ch = lax.broadcasted_iota(I32, (heads * dh, LANES), 0) // dh
    hd = lax.broadcasted_iota(I32, (heads * dh, LANES), 1)
    n2 = jnp.dot(kb * kb, (ch == hd).astype(F32), precision=lax.Precision.HIGHEST, preferred_element_type=F32)
    n2 = jnp.max(n2, axis=0, keepdims=True)
    return jnp.concatenate([jnp.broadcast_to(n2[:, h:h + 1], (1, TQ)) for h in range(heads)], axis=1)


def _norm_bound(qbd, k2, dh):
    q = qbd.astype(F32)
    q2 = jnp.sum(q * q, axis=0, keepdims=True)
    return jnp.sqrt(q2 * k2) * (dh ** -0.5 * (1.0 + 2.0 ** -8))


def _underflowed(l):
    small = l[0]
    for lh in l[1:]:
        small = jnp.minimum(small, lh)
    return jnp.logical_not(jnp.min(small) >= SOFTMAX_L_MIN)


def _moba_kernel(qT_ref, k_ref, vT_ref, y_ref, km_scr, kn_scr, acc_scr, *, NB, NBP, n_sel):
    H, DH, BS = MOBA_HEADS, MOBA_DH, MOBA_BLOCK
    TQ = BS
    qi = pl.program_id(1)

    @pl.when(qi == 0)
    def _():
        km_scr[...] = jnp.zeros_like(km_scr)
        kn_scr[...] = jnp.zeros_like(kn_scr)

        def body(j, carry):
            kb = k_ref[0, pl.ds(pl.multiple_of(j * BS, BS), BS), :].astype(F32)
            km_scr[pl.ds(j, 1), :] = jnp.mean(kb, axis=0, keepdims=True)
            kn_scr[pl.ds(j, 1), :] = _max_key_norm2(kb, H, DH, TQ)
            return carry
        lax.fori_loop(0, NB, body, 0)

    qT = qT_ref[0]
    qbd = _block_diag_T(qT, H, DH)
    gate = jnp.dot(km_scr[...], qbd.astype(F32), precision=lax.Precision.HIGHEST,
                   preferred_element_type=F32)
    blk = lax.broadcasted_iota(I32, gate.shape, 0)
    g = jnp.where(blk < qi, gate, NEG_INF)
    sels = []
    for _ in range(n_sel):
        mx = jnp.max(g, axis=0, keepdims=True)
        isel = jnp.min(jnp.where(g == mx, blk, NBP), axis=0, keepdims=True)
        sels.append(jnp.where(mx > NEG_INF, isel, -1))
        g = jnp.where(blk == isel, NEG_INF, g)

    qs = (qbd * (DH ** -0.5)).astype(BF16)

    def hit_row(j):
        hit = sels[0] == j
        for s in sels[1:]:
            hit = hit | (s == j)
        return hit

    npair = (qi + 1) // 2
    blk_start = lambda j: pl.multiple_of(j * BS, BS)
    blk_scores = lambda j: jnp.dot(k_ref[0, pl.ds(blk_start(j), BS), :], qs, preferred_element_type=F32)
    hcols = lambda row, h: row[:, h * TQ:(h + 1) * TQ]
    causal = lax.broadcasted_iota(I32, (BS, TQ), 0) <= lax.broadcasted_iota(I32, (BS, TQ), 1)

    def diag_scores():
        s_all = blk_scores(qi)
        return [jnp.where(causal, hcols(s_all, h), NEG_INF) for h in range(H)]

    def max_sweep():
        def max_pair(g, m):
            for u in range(2):
                j = 2 * g + u
                m = jnp.maximum(m, jnp.where(hit_row(j), jnp.max(blk_scores(j), axis=0, keepdims=True), NEG_INF))
            return m
        m_all = lax.fori_loop(0, npair, max_pair, jnp.full((1, H * TQ), NEG_INF, F32))
        s_diag = diag_scores()
        return [jnp.maximum(hcols(m_all, h), jnp.max(s_diag[h], axis=0, keepdims=True)) for h in range(H)]

    def sum_sweep(m):
        acc_scr[...] = jnp.zeros_like(acc_scr)

        def sum_pair(g, l):
            for u in range(2):
                j = 2 * g + u
                hit, s_all = hit_row(j), blk_scores(j)
                vT_blk = vT_ref[0, :, pl.ds(blk_start(j), BS)]
                l = tuple(_accumulate(h, jnp.exp(hcols(s_all, h) - jnp.where(hcols(hit, h), m[h], jnp.inf)),
                                      vT_blk, l[h], acc_scr, DH) for h in range(H))
            return l
        l = lax.fori_loop(0, npair, sum_pair, (jnp.zeros((1, TQ), F32),) * H)
        s_diag = diag_scores()
        vT_diag = vT_ref[0, :, pl.ds(blk_start(qi), BS)]
        return [_accumulate(h, jnp.exp(s_diag[h] - m[h]), vT_diag, l[h], acc_scr, DH) for h in range(H)]

    blk_n = lax.broadcasted_iota(I32, kn_scr.shape, 0)
    k2 = jnp.max(jnp.where(blk_n <= qi, kn_scr[...], 0.0), axis=0, keepdims=True)
    bound = _norm_bound(qbd, k2, DH)
    l = sum_sweep([hcols(bound, h) for h in range(H)])
    _finish_attention(l, acc_scr, y_ref, H, DH)

    @pl.when(_underflowed(l))
    def _():
        _finish_attention(sum_sweep(max_sweep()), acc_scr, y_ref, H, DH)


def _moba(qT, k, vT, B, S):
    BS = MOBA_BLOCK
    assert S % BS == 0
    NB = S // BS
    NBP = -(-NB // SUBLANES) * SUBLANES
    n_sel = max(min(MOBA_TOPK, NB - 1), 1)
    k = k.reshape(B, S, MOBA_W)
    return pl.pallas_call(
        functools.partial(_moba_kernel, NB=NB, NBP=NBP, n_sel=n_sel),
        grid=(B, NB),
        in_specs=[pl.BlockSpec((1, MOBA_W, BS), lambda b, i: (b, 0, i)),
                  pl.BlockSpec((1, S, MOBA_W), lambda b, i: (b, 0, 0)),
                  pl.BlockSpec((1, MOBA_W, S), lambda b, i: (b, 0, 0))],
        out_specs=pl.BlockSpec((1, BS, MOBA_W), lambda b, i: (b, i, 0)),
        out_shape=jax.ShapeDtypeStruct((B, S, MOBA_W), BF16),
        scratch_shapes=[pltpu.VMEM((NBP, MOBA_W), F32),
                        pltpu.VMEM((NBP, MOBA_HEADS * BS), F32),
                        pltpu.VMEM((MOBA_W, BS), F32)],
        compiler_params=_params("parallel", "arbitrary"),
    )(qT, k, vT)


def _dsa_kernel(qT_ref, k_ref, vT_ref, iqT_ref, ik_ref, smT_ref, y_ref, e_scr, gmax_scr, acc_scr, thr_scr, kn_scr,
                *, TQ, topk, spow, n_chunks):
    H, DH = DSA_HEADS, DSA_DH
    KC = TQ
    qi = pl.program_id(1)
    nch = qi + 1
    npair = (nch + 1) // 2
    chunk_start = lambda c: pl.multiple_of(c * KC, KC)

    @pl.when(qi == 0)
    def _():
        kn_scr[...] = jnp.zeros_like(kn_scr)

        def body(c, carry):
            kn_scr[pl.ds(c, 1), :] = _max_key_norm2(k_ref[0, pl.ds(chunk_start(c), KC), :].astype(F32), H, DH, TQ)
            return carry
        lax.fori_loop(0, n_chunks, body, 0)
    sum8 = lambda b: jnp.sum(b.astype(I32).reshape(KC // SUBLANES, SUBLANES, TQ), axis=0)
    any_row = lambda r: jnp.max(r.astype(I32)) > 0

    iqT = iqT_ref[0]
    iq_cat = jnp.concatenate([iqT[h * IDX_DH:(h + 1) * IDX_DH, :] for h in range(IDX_HEADS)], axis=1)
    iw = smT_ref[0][2 * ML_HEADS:2 * ML_HEADS + IDX_HEADS, :] * (IDX_HEADS ** -0.5 * IDX_DH ** -0.5)
    t_pos = qi * TQ + lax.broadcasted_iota(I32, (KC, TQ), 1)
    s_off = lax.broadcasted_iota(I32, (KC, TQ), 0)
    gmax_scr[...] = jnp.full((KC, TQ), NEG_INF, F32)

    def score_pair(g, carry):
        n_pos, n_nonneg = carry
        for u in range(2):
            c = 2 * g + u
            ikc = ik_ref[0, pl.ds(chunk_start(c), KC), :][:, :IDX_DH]
            rel = jnp.dot(ikc, iq_cat, preferred_element_type=F32)
            sc = iw[0:1, :] * jnp.maximum(rel[:, 0:TQ], 0.0)
            for h in range(1, IDX_HEADS):
                sc = sc + iw[h:h + 1, :] * jnp.maximum(rel[:, h * TQ:(h + 1) * TQ], 0.0)
            sc = jnp.where(c * KC + s_off <= t_pos, sc, NEG_INF)
            e_scr[pl.ds(chunk_start(c), KC), :] = sc
            gmax_scr[...] = jnp.maximum(gmax_scr[...], sc)
            n_pos = n_pos + sum8(sc > 0.0)
            n_nonneg = n_nonneg + sum8(sc >= 0.0)
        return n_pos, n_nonneg
    zero8 = jnp.zeros((SUBLANES, TQ), I32)
    n_pos, n_nonneg = lax.fori_loop(0, npair, score_pair, (zero8, zero8))
    n_pos = jnp.sum(n_pos, axis=0, keepdims=True)
    n_nonneg = jnp.sum(n_nonneg, axis=0, keepdims=True)

    def count_ge(cand):
        def body(g, acc):
            for u in range(2):
                acc = acc + sum8(e_scr[pl.ds(chunk_start(2 * g + u), KC), :] >= cand)
            return acc
        return jnp.sum(lax.fori_loop(0, npair, body, zero8), axis=0, keepdims=True)

    def rewrite(rows, thr):
        def body(c, carry):
            x = e_scr[pl.ds(chunk_start(c), KC), :]
            rank = (2 * spow - (c * KC + s_off)).astype(F32)
            new = jnp.where(x > thr, jnp.inf, jnp.where(x == thr, rank, NEG_INF))
            e_scr[pl.ds(chunk_start(c), KC), :] = jnp.where(rows, new, x)
            return carry
        lax.fori_loop(0, nch, body, 0)

    def bisect(lo, hi, act):
        def cond(c):
            return jnp.logical_and(c[0] < BISECT_CAP, c[5] > 0)

        def body(c):
            it, lo, hi, act, tie, _ = c
            for _ in range(BISECT_STEPS_PER_CHECK):
                mid = lo + (hi - lo) * 0.5
                inside = (mid > lo) & (mid < hi)
                cnt = count_ge(mid)
                live = act > 0.0
                run = live & inside
                up = cnt >= topk
                lo = jnp.where(run & up, mid, lo)
                hi = jnp.where(run & jnp.logical_not(up), mid, hi)
                collapsed = live & jnp.logical_not(inside)
                tie = jnp.where(collapsed, 1.0, tie)
                act = jnp.where(collapsed | (run & (cnt == topk)), 0.0, act)
            return it + BISECT_STEPS_PER_CHECK, lo, hi, act, tie, jnp.max(act)
        out = lax.while_loop(cond, body, (jnp.int32(0), lo, hi, act, jnp.zeros_like(act), jnp.max(act)))
        return out[1], out[4]

    fmax = float(jnp.finfo(F32).max)
    gm = gmax_scr[...]
    g_lo = jnp.min(gm, axis=0, keepdims=True)
    g_hi = jnp.max(gm, axis=0, keepdims=True)
    n_vis = qi * TQ + lax.broadcasted_iota(I32, (1, TQ), 1) + 1
    few = n_vis <= topk
    zero_tie = (n_pos < topk) & (n_nonneg >= topk) & jnp.logical_not(few)

    @pl.when(any_row(zero_tie))
    def _():
        rewrite(zero_tie, jnp.zeros((1, TQ), F32))

    rank_lo, rank_hi = float(spow), float(2 * spow + 1)
    lo0 = jnp.where(zero_tie, rank_lo, jnp.maximum(g_lo, -fmax))
    hi0 = jnp.where(zero_tie, rank_hi, g_hi + jnp.abs(g_hi) * (2.0 ** -10) + 1e-30)
    lo, tie = bisect(lo0, hi0, jnp.where(few, 0.0, 1.0))
    thr_scr[0:1, :] = jnp.where(few, -fmax, lo)

    @pl.when(jnp.max(tie) > 0.0)
    def _():
        rows = tie > 0.0
        rewrite(rows, lo)
        lo2, _ = bisect(jnp.full((1, TQ), rank_lo, F32), jnp.full((1, TQ), rank_hi, F32), tie)
        thr_scr[0:1, :] = jnp.where(rows, lo2, thr_scr[0:1, :])

    thr = thr_scr[0:1, :]
    qbd = _block_diag_T(qT_ref[0], H, DH)
    qs = (qbd * (DH ** -0.5)).astype(BF16)
    hcols = lambda row, h: row[:, h * TQ:(h + 1) * TQ]

    def chunk_scores(c):
        start = chunk_start(c)
        bias = jnp.where(e_scr[pl.ds(start, KC), :] >= thr, 0.0, NEG_INF)
        s_all = jnp.dot(k_ref[0, pl.ds(start, KC), :], qs, preferred_element_type=F32)
        return [hcols(s_all, h) + bias for h in range(H)]

    def max_sweep():
        def max_pair(g, m):
            for u in range(2):
                s = chunk_scores(2 * g + u)
                m = tuple(jnp.maximum(m[h], jnp.max(s[h], axis=0, keepdims=True)) for h in range(H))
            return m
        m = lax.fori_loop(0, npair, max_pair, (jnp.full((1, TQ), NEG_INF, F32),) * H)
        return [jnp.where(mh == NEG_INF, 0.0, mh) for mh in m]

    def sum_sweep(m):
        acc_scr[...] = jnp.zeros_like(acc_scr)

        def sum_pair(g, l):
            for u in range(2):
                c = 2 * g + u
                s = chunk_scores(c)
                vT_chunk = vT_ref[0, :, pl.ds(chunk_start(c), KC)]
                l = tuple(_accumulate(h, jnp.exp(s[h] - m[h]), vT_chunk, l[h], acc_scr, DH) for h in range(H))
            return l
        return lax.fori_loop(0, npair, sum_pair, (jnp.zeros((1, TQ), F32),) * H)

    chunk_n = lax.broadcasted_iota(I32, kn_scr.shape, 0)
    k2 = jnp.max(jnp.where(chunk_n < 2 * npair, kn_scr[...], 0.0), axis=0, keepdims=True)
    bound = _norm_bound(qbd, k2, DH)
    l = sum_sweep([hcols(bound, h) for h in range(H)])
    _finish_attention(l, acc_scr, y_ref, H, DH)

    @pl.when(_underflowed(l))
    def _():
        _finish_attention(sum_sweep(max_sweep()), acc_scr, y_ref, H, DH)


def _dsa(qT, k, vT, iqT, ik, smT, B, S):
    TQ = min(ATT_TQ, S)
    topk = min(DSA_TOPK_MAX, S // 4)
    assert S % (2 * TQ) == 0 and topk <= TQ
    spow = 1 << max((S - 1).bit_length(), 1)
    k = k.reshape(B, S, DSA_W)
    ik = ik.reshape(B, S, LANES)
    qblk = lambda n: pl.BlockSpec((1, n, TQ), lambda b, i: (b, 0, i))
    return pl.pallas_call(
        functools.partial(_dsa_kernel, TQ=TQ, topk=topk, spow=spow, n_chunks=S // TQ),
        grid=(B, S // TQ),
        in_specs=[qblk(DSA_W),
                  pl.BlockSpec((1, S, DSA_W), lambda b, i: (b, 0, 0)),
                  pl.BlockSpec((1, DSA_W, S), lambda b, i: (b, 0, 0)),
                  qblk(IDX_HEADS * IDX_DH),
                  pl.BlockSpec((1, S, LANES), lambda b, i: (b, 0, 0)),
                  qblk(16)],
        out_specs=pl.BlockSpec((1, TQ, DSA_W), lambda b, i: (b, i, 0)),
        out_shape=jax.ShapeDtypeStruct((B, S, DSA_W), BF16),
        scratch_shapes=[pltpu.VMEM((S, TQ), F32),
                        pltpu.VMEM((TQ, TQ), F32),
                        pltpu.VMEM((DSA_W, TQ), F32),
                        pltpu.VMEM((SUBLANES, TQ), F32),
                        pltpu.VMEM((-(-(S // TQ) // SUBLANES) * SUBLANES, DSA_HEADS * TQ), F32)],
        compiler_params=_params("parallel", "arbitrary"),
    )(qT, k, vT, iqT, ik, smT)


def _out_mlp_kernel(yml_ref, ymb_ref, yds_ref, x_ref, wout_ref, gpost_ref, gpre_ref,
                    w1_ref, w2_ref, g2_ref, out_ref, x1_scr, h_scr, acc_scr):
    f = pl.program_id(1)

    @pl.when(f == 0)
    def _():
        mix = jnp.dot(yml_ref[...], wout_ref[0:ML_V_W, :], preferred_element_type=F32)
        mix = mix + jnp.dot(ymb_ref[...], wout_ref[ML_V_W:ML_V_W + MOBA_W, :], preferred_element_type=F32)
        mix = mix + jnp.dot(yds_ref[...], wout_ref[ML_V_W + MOBA_W:, :], preferred_element_type=F32)
        x1 = x_ref[...] + _rms(mix, gpost_ref[...])
        x1_scr[...] = x1
        h_scr[...] = _rms(x1, gpre_ref[...]).astype(BF16)
        acc_scr[...] = jnp.zeros_like(acc_scr)

    u = jnp.maximum(jnp.dot(h_scr[...], w1_ref[...], preferred_element_type=F32), 0.0)
    acc_scr[...] += jnp.dot((u * u).astype(BF16), w2_ref[...], preferred_element_type=F32)

    @pl.when(f == pl.num_programs(1) - 1)
    def _():
        out_ref[...] = x1_scr[...] + _rms(acc_scr[...], g2_ref[...])


def _out_mlp(yml, ymb, yds, x2, w_out, g_post, g_pre, w1, w2, g2):
    T, D = x2.shape
    tm = min(MLP_TM, T)
    tf = min(MLP_TF, D_FF)
    row = lambda n: pl.BlockSpec((tm, n), lambda i, f: (i, 0))
    full = lambda a: pl.BlockSpec(a.shape, lambda i, f: (0,) * a.ndim)
    return pl.pallas_call(
        _out_mlp_kernel,
        grid=(T // tm, D_FF // tf),
        in_specs=[row(ML_V_W), row(MOBA_W), row(DSA_W), row(D), full(w_out), full(g_post), full(g_pre),
                  pl.BlockSpec((D, tf), lambda i, f: (0, f)),
                  pl.BlockSpec((tf, D), lambda i, f: (f, 0)),
                  full(g2)],
        out_specs=row(D),
        out_shape=jax.ShapeDtypeStruct((T, D), F32),
        scratch_shapes=[pltpu.VMEM((tm, D), F32), pltpu.VMEM((tm, D), BF16), pltpu.VMEM((tm, D), F32)],
        compiler_params=_params("parallel", "arbitrary"),
    )(yml, ymb, yds, x2, w_out, g_post, g_pre, w1, w2, g2)


def kernel(x, norm_mix_pre, w_in, ml_conv, ml_i_bias, ml_f_bias, ml_head_norm, w_out, norm_mix_post,
           norm_mlp_pre, w_ff1, w_ff2, norm_mlp_post):
    B, S, D = x.shape
    depth = w_in.shape[0]
    x2 = x.reshape(B * S, D)
    gain = lambda g: g.reshape(1, D).astype(F32)
    for l in range(depth):
        wn, wt = _prep_in_weights(w_in[l])
        (qk, v, o, sm, mbk, dsk, ixk, mbqT, mbvT, dsqT, dsvT, ixqT, smT) = _in_proj(
            x2, gain(norm_mix_pre[l]), wn, wt, B, S)
        y_ml = _mlstm(qk, v, o, sm, smT, ml_conv[l], ml_i_bias[l], ml_f_bias[l], ml_head_norm[l], B, S)
        y_mb = _moba(mbqT, mbk, mbvT, B, S)
        y_ds = _dsa(dsqT, dsk, dsvT, ixqT, ixk, smT, B, S)
        x2 = _out_mlp(y_ml.reshape(B * S, ML_V_W), y_mb.reshape(B * S, MOBA_W), y_ds.reshape(B * S, DSA_W),
                      x2, w_out[l].astype(BF16), gain(norm_mix_post[l]), gain(norm_mlp_pre[l]),
                      w_ff1[l].astype(BF16), w_ff2[l].astype(BF16), gain(norm_mlp_post[l]))
    return x2.reshape(B, S, D)
```

```python
import functools

import jax
import jax.numpy as jnp
from jax import lax
from jax.experimental import pallas as pl
from jax.experimental.pallas import tpu as pltpu

F32 = jnp.float32
BF16 = jnp.bfloat16
I32 = jnp.int32

D_MODEL = 1024
ML_HEADS, ML_DQK, ML_DV, ML_CONV = 4, 64, 128, 4
MOBA_HEADS, MOBA_DH, MOBA_BLOCK, MOBA_TOPK = 4, 64, 256, 3
DSA_HEADS, DSA_DH, IDX_HEADS, IDX_DH, DSA_TOPK_MAX = 4, 64, 4, 64, 256
D_FF = 4 * D_MODEL
EPS = 1e-6

ML_QK_W = ML_HEADS * ML_DQK
ML_V_W = ML_HEADS * ML_DV
MOBA_W = MOBA_HEADS * MOBA_DH
DSA_W = DSA_HEADS * DSA_DH
IN_SPLITS = (ML_QK_W, ML_QK_W, ML_V_W, ML_V_W, ML_HEADS, ML_HEADS,
             MOBA_W, MOBA_W, MOBA_W,
             DSA_W, DSA_W, DSA_W, IDX_HEADS * IDX_DH, IDX_DH, IDX_HEADS)

LANES = 128
SUBLANES = 8
VMEM_LIMIT = 52 * 1024 * 1024

ML_CHUNK = 256
ATT_TQ = 256
PROJ_TM = 512
MLP_TM = 1024
MLP_TF = 512

NEG_INF = float("-inf")
BISECT_CAP = 300
BISECT_STEPS_PER_CHECK = 3
SOFTMAX_L_MIN = 1e-30
assert MOBA_DH == DSA_DH
ATT_QSCALE = MOBA_DH ** -0.5 * 1.4426950408889634

_NT = (((1,), (1,)), ((), ()))
_TN = (((0,), (0,)), ((), ()))


def _rms(x, g):
    return x * lax.rsqrt(jnp.mean(x * x, axis=-1, keepdims=True) + EPS) * g


def _log_sigmoid(x):
    return jnp.minimum(x, 0.0) - jnp.log1p(jnp.exp(-jnp.abs(x)))


def _params(*sem):
    return pltpu.CompilerParams(dimension_semantics=sem, vmem_limit_bytes=VMEM_LIMIT)


_N_QK, _N_V, _N_O, _N_MBK, _N_DSK, _N_IXK, _N_SM = 0, 512, 1024, 1536, 1792, 2048, 2176
_N_TOTAL = 2304
_T_MBQ, _T_MBV, _T_DSQ, _T_DSV, _T_IXQ, _T_SM = 0, 256, 512, 768, 1024, 1280
_T_TOTAL = 1296


def _in_proj_kernel(x_ref, g_ref, wn_ref, wt_ref,
                    qk_ref, v_ref, o_ref, sm_ref, mbk_ref, dsk_ref, ixk_ref,
                    mbqT_ref, mbvT_ref, dsqT_ref, dsvT_ref, ixqT_ref, smT_ref):
    h = _rms(x_ref[...], g_ref[...]).astype(BF16)

    def mm(a, n):
        return jnp.dot(h, wn_ref[:, a:a + n], preferred_element_type=F32)

    def mt(a, n):
        return lax.dot_general(wt_ref[a:a + n, :], h, _NT, preferred_element_type=F32)

    qk_ref[...] = mm(_N_QK, 512)
    v_ref[...] = mm(_N_V, 512).astype(BF16)
    o_ref[...] = mm(_N_O, 512)
    mbk_ref[...] = mm(_N_MBK, 256).astype(BF16)
    dsk_ref[...] = mm(_N_DSK, 256).astype(BF16)
    ixk_ref[...] = mm(_N_IXK, 128).astype(BF16)
    sm_ref[...] = mm(_N_SM, 128)
    mbqT_ref[0] = (mt(_T_MBQ, 256) * ATT_QSCALE).astype(BF16)
    mbvT_ref[0] = mt(_T_MBV, 256).astype(BF16)
    dsqT_ref[0] = (mt(_T_DSQ, 256) * ATT_QSCALE).astype(BF16)
    dsvT_ref[0] = mt(_T_DSV, 256).astype(BF16)
    ixqT_ref[0] = mt(_T_IXQ, 256).astype(BF16)
    smT_ref[0] = mt(_T_SM, 16)


def _prep_in_weights(w_in):
    pts, acc = [], 0
    for n in IN_SPLITS:
        pts.append((acc, acc + n))
        acc += n
    col = lambda i: w_in[:, pts[i][0]:pts[i][1]]
    (ml_q, ml_k, ml_v, ml_o, ml_i, ml_f, mb_q, mb_k, mb_v,
     ds_q, ds_k, ds_v, ix_q, ix_k, ix_w) = [col(i) for i in range(len(IN_SPLITS))]
    d = w_in.shape[0]
    small = jnp.concatenate([ml_i, ml_f, ix_w], axis=1)
    wn = jnp.concatenate([
        ml_q, ml_k, ml_v, ml_o, mb_k, ds_k,
        ix_k, jnp.zeros((d, 128 - IDX_DH), w_in.dtype),
        small, jnp.zeros((d, 128 - small.shape[1]), w_in.dtype)], axis=1)
    wt = jnp.concatenate([
        mb_q, mb_v, ds_q, ds_v, ix_q,
        small, jnp.zeros((d, 16 - small.shape[1]), w_in.dtype)], axis=1).T
    assert wn.shape == (d, _N_TOTAL) and wt.shape == (_T_TOTAL, d)
    return wn.astype(BF16), wt.astype(BF16)


def _in_proj(x2, g, wn, wt, B, S):
    T, D = x2.shape
    tm = min(PROJ_TM, S)
    nsb = S // tm
    row = lambda n: pl.BlockSpec((tm, n), lambda i: (i, 0))
    tr = lambda n: pl.BlockSpec((1, n, tm), lambda i: (i // nsb, 0, i % nsb))
    full = lambda a: pl.BlockSpec(a.shape, lambda i: (0,) * a.ndim)
    out_shape = (
        jax.ShapeDtypeStruct((T, 512), F32),
        jax.ShapeDtypeStruct((T, 512), BF16),
        jax.ShapeDtypeStruct((T, 512), F32),
        jax.ShapeDtypeStruct((T, 128), F32),
        jax.ShapeDtypeStruct((T, 256), BF16),
        jax.ShapeDtypeStruct((T, 256), BF16),
        jax.ShapeDtypeStruct((T, 128), BF16),
        jax.ShapeDtypeStruct((B, 256, S), BF16),
        jax.ShapeDtypeStruct((B, 256, S), BF16),
        jax.ShapeDtypeStruct((B, 256, S), BF16),
        jax.ShapeDtypeStruct((B, 256, S), BF16),
        jax.ShapeDtypeStruct((B, 256, S), BF16),
        jax.ShapeDtypeStruct((B, 16, S), F32),
    )
    out_specs = (row(512), row(512), row(512), row(128), row(256), row(256), row(128),
                 tr(256), tr(256), tr(256), tr(256), tr(256), tr(16))
    return pl.pallas_call(
        _in_proj_kernel,
        grid=(T // tm,),
        in_specs=[row(D), full(g), full(wn), full(wt)],
        out_specs=out_specs,
        out_shape=out_shape,
        compiler_params=_params("parallel"),
    )(x2, g, wn, wt)


def _mlstm_kernel(qk_ref, v_ref, o_ref, sm_ref, smT_ref, conv_ref, brow_ref, bcol_ref, hn_ref,
                  y_ref, xp_scr, st_scr, m_scr, *, L):
    c = pl.program_id(1)
    halo = SUBLANES

    @pl.when(c == 0)
    def _():
        xp_scr[0:halo, :] = jnp.zeros((halo, 2 * ML_QK_W), F32)
        st_scr[...] = jnp.zeros_like(st_scr)
        m_scr[...] = jnp.zeros_like(m_scr)

    cur = qk_ref[0]
    xp_scr[halo:halo + L, :] = cur
    base = halo - (ML_CONV - 1)
    acc = conv_ref[0:1, :] * xp_scr[base:base + L, :]
    for j in range(1, ML_CONV):
        acc = acc + conv_ref[j:j + 1, :] * xp_scr[base + j:base + j + L, :]
    xp_scr[0:halo, :] = cur[L - halo:L, :]
    qk = acc * jax.nn.sigmoid(acc)
    q_all = (qk[:, :ML_QK_W] * (ML_DQK ** -0.5)).astype(BF16)
    k_all = qk[:, ML_QK_W:]

    gcol = sm_ref[0] + brow_ref[...]
    grow = smT_ref[0][0:SUBLANES, :] + bcol_ref[...]
    ri = lax.broadcasted_iota(I32, (L, L), 0)
    ci = lax.broadcasted_iota(I32, (L, L), 1)
    causal = ci <= ri
    tril = causal.astype(F32)
    triu = (ri <= ci).astype(F32)
    hp = lax.Precision.HIGHEST
    b_col = jnp.dot(tril, _log_sigmoid(gcol), precision=hp, preferred_element_type=F32)
    b_row = jnp.dot(_log_sigmoid(grow), triu, precision=hp, preferred_element_type=F32)

    v_all = v_ref[0]
    o_all = o_ref[0]
    lane = lax.broadcasted_iota(I32, (L, LANES), 1)
    ones_col = jnp.where(lane == 0, 1.0, 0.0).astype(BF16)
    for h in range(ML_HEADS):
        qh = q_all[:, h * ML_DQK:(h + 1) * ML_DQK]
        kh = k_all[:, h * ML_DQK:(h + 1) * ML_DQK]
        vh = v_all[:, h * ML_DV:(h + 1) * ML_DV]
        bj = b_col[:, ML_HEADS + h:ML_HEADS + h + 1]
        icol = gcol[:, h:h + 1]
        r_row = grow[h:h + 1, :] - b_row[ML_HEADS + h:ML_HEADS + h + 1, :]
        dlog = jnp.where(causal, bj + r_row, NEG_INF)
        m0 = m_scr[h:h + 1, 0:1]
        inter = bj + m0
        m_t = jnp.maximum(inter, jnp.max(dlog, axis=-1, keepdims=True))
        w_inter = jnp.exp(inter - m_t)
        qk_s = lax.dot_general(qh, kh.astype(BF16), _NT, preferred_element_type=F32)
        w_intra = jnp.exp(dlog - m_t) * qk_s
        st = st_scr[h]
        cq = jnp.dot(qh, st.astype(BF16), preferred_element_type=F32)
        num = w_inter * cq[:, :ML_DV] + jnp.dot(w_intra.astype(BF16), vh, preferred_element_type=F32)
        den = w_inter * cq[:, ML_DV:ML_DV + 1] + jnp.sum(w_intra, axis=-1, keepdims=True)
        hh = num / jnp.maximum(jnp.abs(den), jnp.exp(-m_t))
        hh = hh * lax.rsqrt(jnp.mean(hh * hh, axis=-1, keepdims=True) + EPS)
        hh = hh * hn_ref[:, h * ML_DV:(h + 1) * ML_DV]
        y_ref[0, :, h * ML_DV:(h + 1) * ML_DV] = (
            jax.nn.sigmoid(o_all[:, h * ML_DV:(h + 1) * ML_DV]) * hh).astype(y_ref.dtype)

        b_last = bj[L - 1:L, :]
        a = b_last - bj + icol
        m_loc = jnp.max(a, axis=0, keepdims=True)
        wa = jnp.exp(a - m_loc)
        kw = (kh * wa).astype(BF16)
        vaug = jnp.concatenate([vh, ones_col], axis=1)
        c_loc = lax.dot_general(kw, vaug, _TN, preferred_element_type=F32)
        m_new = jnp.maximum(b_last + m0, m_loc)
        s_old = jnp.exp(b_last + m0 - m_new)
        s_loc = jnp.exp(m_loc - m_new)
        st_scr[h] = s_old * st + s_loc * c_loc
        m_scr[h:h + 1, :] = jnp.broadcast_to(m_new, (1, LANES))


def _mlstm(qk, v, o, sm, smT, conv_w, i_bias, f_bias, head_norm, B, S):
    L = min(ML_CHUNK, S)
    nc = S // L
    qk = qk.reshape(B, S, 2 * ML_QK_W)
    v = v.reshape(B, S, ML_V_W)
    o = o.reshape(B, S, ML_V_W)
    sm = sm.reshape(B, S, LANES)
    bias = jnp.concatenate([i_bias, f_bias]).astype(F32)
    brow = jnp.zeros((1, LANES), F32).at[0, :2 * ML_HEADS].set(bias)
    bcol = bias.reshape(2 * ML_HEADS, 1)
    hn = head_norm.reshape(1, ML_V_W).astype(F32)
    blk = lambda n: pl.BlockSpec((1, L, n), lambda b, c: (b, c, 0))
    full = lambda a: pl.BlockSpec(a.shape, lambda b, c: (0,) * a.ndim)
    return pl.pallas_call(
        functools.partial(_mlstm_kernel, L=L),
        grid=(B, nc),
        in_specs=[blk(2 * ML_QK_W), blk(ML_V_W), blk(ML_V_W), blk(LANES),
                  pl.BlockSpec((1, 16, L), lambda b, c: (b, 0, c)),
                  full(conv_w), full(brow), full(bcol), full(hn)],
        out_specs=blk(ML_V_W),
        out_shape=jax.ShapeDtypeStruct((B, S, ML_V_W), BF16),
        scratch_shapes=[pltpu.VMEM((L + SUBLANES, 2 * ML_QK_W), F32),
                        pltpu.VMEM((ML_HEADS, ML_DQK, ML_DV + LANES), F32),
                        pltpu.VMEM((SUBLANES, LANES), F32)],
        compiler_params=_params("parallel", "arbitrary"),
    )(qk, v, o, sm, smT, conv_w.astype(F32), brow, bcol, hn)


def _block_diag_T(qT, heads, dh):
    rid = lax.broadcasted_iota(I32, qT.shape, 0)
    zero = jnp.zeros_like(qT)
    return jnp.concatenate(
        [jnp.where((rid >= h * dh) & (rid < (h + 1) * dh), qT, zero) for h in range(heads)], axis=1)


def _scores(k_chunk, qs, h, TQ):
    return jnp.dot(k_chunk, qs[:, h * TQ:(h + 1) * TQ], preferred_element_type=F32)


def _accumulate(h, p, vT_chunk, l, acc_scr, dh):
    rows = slice(h * dh, (h + 1) * dh)
    acc_scr[rows, :] += jnp.dot(vT_chunk[rows, :], p.astype(BF16), preferred_element_type=F32)
    return l + jnp.sum(p, axis=0, keepdims=True)


def _finish_attention(l, acc_scr, y_ref, heads, dh):
    outs = [acc_scr[h * dh:(h + 1) * dh, :] / l[h] for h in range(heads)]
    y_ref[0] = jnp.concatenate(outs, axis=0).T.astype(y_ref.dtype)


def _max_key_norm2(kb, heads, dh, TQ):
    ch = lax.broadcasted_iota(I32, (heads * dh, LANES), 0) // dh
    hd = lax.broadcasted_iota(I32, (heads * dh, LANES), 1)
    n2 = jnp.dot(kb * kb, (ch == hd).astype(F32), precision=lax.Precision.HIGHEST, preferred_element_type=F32)
    n2 = jnp.max(n2, axis=0, keepdims=True)
    return jnp.concatenate([jnp.broadcast_to(n2[:, h:h + 1], (1, TQ)) for h in range(heads)], axis=1)


def _norm_bound(qTf, k2, heads, dh, TQ):
    out = []
    for h in range(heads):
        qh = qTf[h * dh:(h + 1) * dh, :]
        q2 = jnp.sum(qh * qh, axis=0, keepdims=True)
        out.append(jnp.sqrt(q2 * k2[:, h * TQ:(h + 1) * TQ]) * (1.0 + 2.0 ** -8))
    return out


def _underflowed(l):
    small = l[0]
    for lh in l[1:]:
        small = jnp.minimum(small, lh)
    return jnp.logical_not(jnp.min(small) >= SOFTMAX_L_MIN)


def _moba_kernel(qT_ref, k_ref, vT_ref, y_ref, km_scr, kn_scr, acc_scr, *, NB, NBP, n_sel):
    H, DH, BS = MOBA_HEADS, MOBA_DH, MOBA_BLOCK
    TQ = BS
    qi = pl.program_id(1)

    @pl.when(qi == 0)
    def _():
        km_scr[...] = jnp.zeros_like(km_scr)
        kn_scr[...] = jnp.zeros_like(kn_scr)

        def body(j, carry):
            kb = k_ref[0, pl.ds(pl.multiple_of(j * BS, BS), BS), :].astype(F32)
            km_scr[pl.ds(j, 1), :] = jnp.mean(kb, axis=0, keepdims=True)
            kn_scr[pl.ds(j, 1), :] = _max_key_norm2(kb, H, DH, TQ)
            return carry
        lax.fori_loop(0, NB, body, 0)

    qT = qT_ref[0]
    qTf = qT.astype(F32)
    km = km_scr[...]
    gate = jnp.concatenate(
        [jnp.dot(km[:, h * DH:(h + 1) * DH], qTf[h * DH:(h + 1) * DH, :], precision=lax.Precision.HIGHEST,
                 preferred_element_type=F32) for h in range(H)], axis=1) * (1.0 / ATT_QSCALE)
    blk = lax.broadcasted_iota(I32, gate.shape, 0)
    g = jnp.where(blk < qi, gate, NEG_INF)
    sels = []
    for _ in range(n_sel):
        mx = jnp.max(g, axis=0, keepdims=True)
        isel = jnp.min(jnp.where(g == mx, blk, NBP), axis=0, keepdims=True)
        sels.append(jnp.where(mx > NEG_INF, isel, -1))
        g = jnp.where(blk == isel, NEG_INF, g)

    qs = _block_diag_T(qT, H, DH)

    def hit_row(j):
        hit = sels[0] == j
        for s in sels[1:]:
            hit = hit | (s == j)
        return hit

    npair = (qi + 1) // 2
    blk_start = lambda j: pl.multiple_of(j * BS, BS)
    blk_scores = lambda j: jnp.dot(k_ref[0, pl.ds(blk_start(j), BS), :], qs, preferred_element_type=F32)
    hcols = lambda row, h: row[:, h * TQ:(h + 1) * TQ]
    causal = lax.broadcasted_iota(I32, (BS, TQ), 0) <= lax.broadcasted_iota(I32, (BS, TQ), 1)

    def diag_scores():
        s_all = blk_scores(qi)
        return [jnp.where(causal, hcols(s_all, h), NEG_INF) for h in range(H)]

    def max_sweep():
        def max_pair(g, m):
            for u in range(2):
                j = 2 * g + u
                m = jnp.maximum(m, jnp.where(hit_row(j), jnp.max(blk_scores(j), axis=0, keepdims=True), NEG_INF))
            return m
        m_all = lax.fori_loop(0, npair, max_pair, jnp.full((1, H * TQ), NEG_INF, F32))
        s_diag = diag_scores()
        return [jnp.maximum(hcols(m_all, h), jnp.max(s_diag[h], axis=0, keepdims=True)) for h in range(H)]

    def sum_sweep(m):
        acc_scr[...] = jnp.zeros_like(acc_scr)

        def sum_pair(g, l):
            for u in range(2):
                j = 2 * g + u
                hit, s_all = hit_row(j), blk_scores(j)
                vT_blk = vT_ref[0, :, pl.ds(blk_start(j), BS)]
                l = tuple(_accumulate(h, jnp.exp2(hcols(s_all, h) - jnp.where(hcols(hit, h), m[h], jnp.inf)),
                                      vT_blk, l[h], acc_scr, DH) for h in range(H))
            return l
        l = lax.fori_loop(0, npair, sum_pair, (jnp.zeros((1, TQ), F32),) * H)
        s_diag = diag_scores()
        vT_diag = vT_ref[0, :, pl.ds(blk_start(qi), BS)]
        return [_accumulate(h, jnp.exp2(s_diag[h] - m[h]), vT_diag, l[h], acc_scr, DH) for h in range(H)]

    blk_n = lax.broadcasted_iota(I32, kn_scr.shape, 0)
    k2 = jnp.max(jnp.where(blk_n <= qi, kn_scr[...], 0.0), axis=0, keepdims=True)
    l = sum_sweep(_norm_bound(qTf, k2, H, DH, TQ))
    _finish_attention(l, acc_scr, y_ref, H, DH)

    @pl.when(_underflowed(l))
    def _():
        _finish_attention(sum_sweep(max_sweep()), acc_scr, y_ref, H, DH)


def _moba(qT, k, vT, B, S):
    BS = MOBA_BLOCK
    assert S % BS == 0
    NB = S // BS
    NBP = -(-NB // SUBLANES) * SUBLANES
    n_sel = max(min(MOBA_TOPK, NB - 1), 1)
    k = k.reshape(B, S, MOBA_W)
    return pl.pallas_call(
        functools.partial(_moba_kernel, NB=NB, NBP=NBP, n_sel=n_sel),
        grid=(B, NB),
        in_specs=[pl.BlockSpec((1, MOBA_W, BS), lambda b, i: (b, 0, i)),
                  pl.BlockSpec((1, S, MOBA_W), lambda b, i: (b, 0, 0)),
                  pl.BlockSpec((1, MOBA_W, S), lambda b, i: (b, 0, 0))],
        out_specs=pl.BlockSpec((1, BS, MOBA_W), lambda b, i: (b, i, 0)),
        out_shape=jax.ShapeDtypeStruct((B, S, MOBA_W), BF16),
        scratch_shapes=[pltpu.VMEM((NBP, MOBA_W), F32),
                        pltpu.VMEM((NBP, MOBA_HEADS * BS), F32),
                        pltpu.VMEM((MOBA_W, BS), F32)],
        compiler_params=_params("parallel", "arbitrary"),
    )(qT, k, vT)


def _dsa_kernel(qT_ref, k_ref, vT_ref, iqT_ref, ik_ref, smT_ref, y_ref, e_scr, gmax_scr, acc_scr, thr_scr, kn_scr,
                *, TQ, topk, spow, n_chunks):
    H, DH = DSA_HEADS, DSA_DH
    KC = TQ
    qi = pl.program_id(1)
    nch = qi + 1
    npair = (nch + 1) // 2
    chunk_start = lambda c: pl.multiple_of(c * KC, KC)

    @pl.when(qi == 0)
    def _():
        kn_scr[...] = jnp.zeros_like(kn_scr)

        def body(c, carry):
            kn_scr[pl.ds(c, 1), :] = _max_key_norm2(k_ref[0, pl.ds(chunk_start(c), KC), :].astype(F32), H, DH, TQ)
            return carry
        lax.fori_loop(0, n_chunks, body, 0)
    sum8 = lambda b: jnp.sum(b.astype(I32).reshape(KC // SUBLANES, SUBLANES, TQ), axis=0)
    any_row = lambda r: jnp.max(r.astype(I32)) > 0

    iqT = iqT_ref[0]
    iq_cat = jnp.concatenate([iqT[h * IDX_DH:(h + 1) * IDX_DH, :] for h in range(IDX_HEADS)], axis=1)
    iw = smT_ref[0][2 * ML_HEADS:2 * ML_HEADS + IDX_HEADS, :] * (IDX_HEADS ** -0.5 * IDX_DH ** -0.5)
    t_pos = qi * TQ + lax.broadcasted_iota(I32, (KC, TQ), 1)
    s_off = lax.broadcasted_iota(I32, (KC, TQ), 0)
    gmax_scr[...] = jnp.full((KC, TQ), NEG_INF, F32)

    def score_pair(g, carry):
        n_pos, n_nonneg = carry
        for u in range(2):
            c = 2 * g + u
            ikc = ik_ref[0, pl.ds(chunk_start(c), KC), :][:, :IDX_DH]
            rel = jnp.dot(ikc, iq_cat, preferred_element_type=F32)
            sc = iw[0:1, :] * jnp.maximum(rel[:, 0:TQ], 0.0)
            for h in range(1, IDX_HEADS):
                sc = sc + iw[h:h + 1, :] * jnp.maximum(rel[:, h * TQ:(h + 1) * TQ], 0.0)
            sc = jnp.where(c * KC + s_off <= t_pos, sc, NEG_INF)
            e_scr[pl.ds(chunk_start(c), KC), :] = sc
            gmax_scr[...] = jnp.maximum(gmax_scr[...], sc)
            n_pos = n_pos + sum8(sc > 0.0)
            n_nonneg = n_nonneg + sum8(sc >= 0.0)
        return n_pos, n_nonneg
    zero8 = jnp.zeros((SUBLANES, TQ), I32)
    n_pos, n_nonneg = lax.fori_loop(0, npair, score_pair, (zero8, zero8))
    n_pos = jnp.sum(n_pos, axis=0, keepdims=True)
    n_nonneg = jnp.sum(n_nonneg, axis=0, keepdims=True)

    def count_ge(cand):
        def body(g, acc):
            a, b = acc
            a = a + sum8(e_scr[pl.ds(chunk_start(2 * g), KC), :] >= cand)
            b = b + sum8(e_scr[pl.ds(chunk_start(2 * g + 1), KC), :] >= cand)
            return a, b
        a, b = lax.fori_loop(0, npair, body, (zero8, zero8))
        return jnp.sum(a + b, axis=0, keepdims=True)

    def rewrite(rows, thr):
        def body(c, carry):
            x = e_scr[pl.ds(chunk_start(c), KC), :]
            rank = (2 * spow - (c * KC + s_off)).astype(F32)
            new = jnp.where(x > thr, jnp.inf, jnp.where(x == thr, rank, NEG_INF))
            e_scr[pl.ds(chunk_start(c), KC), :] = jnp.where(rows, new, x)
            return carry
        lax.fori_loop(0, nch, body, 0)

    def bisect(lo, hi, act):
        def cond(c):
            return jnp.logical_and(c[0] < BISECT_CAP, c[5] > 0)

        def body(c):
            it, lo, hi, act, tie, _ = c
            for _ in range(BISECT_STEPS_PER_CHECK):
                mid = lo + (hi - lo) * 0.5
                inside = (mid > lo) & (mid < hi)
                cnt = count_ge(mid)
                live = act > 0.0
                run = live & inside
                up = cnt >= topk
                lo = jnp.where(run & up, mid, lo)
                hi = jnp.where(run & jnp.logical_not(up), mid, hi)
                collapsed = live & jnp.logical_not(inside)
                tie = jnp.where(collapsed, 1.0, tie)
                act = jnp.where(collapsed | (run & (cnt == topk)), 0.0, act)
            return it + BISECT_STEPS_PER_CHECK, lo, hi, act, tie, jnp.max(act)
        out = lax.while_loop(cond, body, (jnp.int32(0), lo, hi, act, jnp.zeros_like(act), jnp.max(act)))
        return out[1], out[4]

    fmax = float(jnp.finfo(F32).max)
    gm = gmax_scr[...]
    g_lo = jnp.min(gm, axis=0, keepdims=True)
    g_hi = jnp.max(gm, axis=0, keepdims=True)
    n_vis = qi * TQ + lax.broadcasted_iota(I32, (1, TQ), 1) + 1
    few = n_vis <= topk
    zero_tie = (n_pos < topk) & (n_nonneg >= topk) & jnp.logical_not(few)

    @pl.when(any_row(zero_tie))
    def _():
        rewrite(zero_tie, jnp.zeros((1, TQ), F32))

    rank_lo, rank_hi = float(spow), float(2 * spow + 1)
    lo0 = jnp.where(zero_tie, rank_lo, jnp.maximum(g_lo, -fmax))
    hi0 = jnp.where(zero_tie, rank_hi, g_hi + jnp.abs(g_hi) * (2.0 ** -10) + 1e-30)
    lo, tie = bisect(lo0, hi0, jnp.where(few, 0.0, 1.0))
    thr_scr[0:1, :] = jnp.where(few, -fmax, lo)

    @pl.when(jnp.max(tie) > 0.0)
    def _():
        rows = tie > 0.0
        rewrite(rows, lo)
        lo2, _ = bisect(jnp.full((1, TQ), rank_lo, F32), jnp.full((1, TQ), rank_hi, F32), tie)
        thr_scr[0:1, :] = jnp.where(rows, lo2, thr_scr[0:1, :])

    thr = thr_scr[0:1, :]
    qT = qT_ref[0]
    qs = _block_diag_T(qT, H, DH)
    hcols = lambda row, h: row[:, h * TQ:(h + 1) * TQ]

    def chunk_scores(c):
        start = chunk_start(c)
        sel = e_scr[pl.ds(start, KC), :] >= thr
        s_all = jnp.dot(k_ref[0, pl.ds(start, KC), :], qs, preferred_element_type=F32)
        return sel, s_all

    def max_sweep():
        def max_pair(g, m):
            for u in range(2):
                sel, s_all = chunk_scores(2 * g + u)
                m = tuple(jnp.maximum(m[h], jnp.max(jnp.where(sel, hcols(s_all, h), NEG_INF), axis=0, keepdims=True))
                          for h in range(H))
            return m
        m = lax.fori_loop(0, npair, max_pair, (jnp.full((1, TQ), NEG_INF, F32),) * H)
        return [jnp.where(mh == NEG_INF, 0.0, mh) for mh in m]

    def sum_sweep(m):
        acc_scr[...] = jnp.zeros_like(acc_scr)

        def sum_pair(g, l):
            for u in range(2):
                c = 2 * g + u
                sel, s_all = chunk_scores(c)
                vT_chunk = vT_ref[0, :, pl.ds(chunk_start(c), KC)]
                l = tuple(_accumulate(h, jnp.exp2(hcols(s_all, h) + jnp.where(sel, -m[h], NEG_INF)),
                                      vT_chunk, l[h], acc_scr, DH) for h in range(H))
            return l
        return lax.fori_loop(0, npair, sum_pair, (jnp.zeros((1, TQ), F32),) * H)

    chunk_n = lax.broadcasted_iota(I32, kn_scr.shape, 0)
    k2 = jnp.max(jnp.where(chunk_n < 2 * npair, kn_scr[...], 0.0), axis=0, keepdims=True)
    l = sum_sweep(_norm_bound(qT.astype(F32), k2, H, DH, TQ))
    _finish_attention(l, acc_scr, y_ref, H, DH)

    @pl.when(_underflowed(l))
    def _():
        _finish_attention(sum_sweep(max_sweep()), acc_scr, y_ref, H, DH)


def _dsa(qT, k, vT, iqT, ik, smT, B, S):
    TQ = min(ATT_TQ, S)
    topk = min(DSA_TOPK_MAX, S // 4)
    assert S % (2 * TQ) == 0 and topk <= TQ
    spow = 1 << max((S - 1).bit_length(), 1)
    k = k.reshape(B, S, DSA_W)
    ik = ik.reshape(B, S, LANES)
    qblk = lambda n: pl.BlockSpec((1, n, TQ), lambda b, i: (b, 0, i))
    return pl.pallas_call(
        functools.partial(_dsa_kernel, TQ=TQ, topk=topk, spow=spow, n_chunks=S // TQ),
        grid=(B, S // TQ),
        in_specs=[qblk(DSA_W),
                  pl.BlockSpec((1, S, DSA_W), lambda b, i: (b, 0, 0)),
                  pl.BlockSpec((1, DSA_W, S), lambda b, i: (b, 0, 0)),
                  qblk(IDX_HEADS * IDX_DH),
                  pl.BlockSpec((1, S, LANES), lambda b, i: (b, 0, 0)),
                  qblk(16)],
        out_specs=pl.BlockSpec((1, TQ, DSA_W), lambda b, i: (b, i, 0)),
        out_shape=jax.ShapeDtypeStruct((B, S, DSA_W), BF16),
        scratch_shapes=[pltpu.VMEM((S, TQ), F32),
                        pltpu.VMEM((TQ, TQ), F32),
                        pltpu.VMEM((DSA_W, TQ), F32),
                        pltpu.VMEM((SUBLANES, TQ), F32),
                        pltpu.VMEM((-(-(S // TQ) // SUBLANES) * SUBLANES, DSA_HEADS * TQ), F32)],
        compiler_params=_params("parallel", "arbitrary"),
    )(qT, k, vT, iqT, ik, smT)


def _out_mlp_kernel(yml_ref, ymb_ref, yds_ref, x_ref, wout_ref, gpost_ref, gpre_ref,
                    w1_ref, w2_ref, g2_ref, out_ref, x1_scr, h_scr, acc_scr):
    f = pl.program_id(1)

    @pl.when(f == 0)
    def _():
        mix = jnp.dot(yml_ref[...], wout_ref[0:ML_V_W, :], preferred_element_type=F32)
        mix = mix + jnp.dot(ymb_ref[...], wout_ref[ML_V_W:ML_V_W + MOBA_W, :], preferred_element_type=F32)
        mix = mix + jnp.dot(yds_ref[...], wout_ref[ML_V_W + MOBA_W:, :], preferred_element_type=F32)
        x1 = x_ref[...] + _rms(mix, gpost_ref[...])
        x1_scr[...] = x1
        h_scr[...] = _rms(x1, gpre_ref[...]).astype(BF16)
        acc_scr[...] = jnp.zeros_like(acc_scr)

    u = jnp.maximum(jnp.dot(h_scr[...], w1_ref[...], preferred_element_type=F32), 0.0)
    acc_scr[...] += jnp.dot((u * u).astype(BF16), w2_ref[...], preferred_element_type=F32)

    @pl.when(f == pl.num_programs(1) - 1)
    def _():
        out_ref[...] = x1_scr[...] + _rms(acc_scr[...], g2_ref[...])


def _out_mlp(yml, ymb, yds, x2, w_out, g_post, g_pre, w1, w2, g2):
    T, D = x2.shape
    tm = min(MLP_TM, T)
    tf = min(MLP_TF, D_FF)
    row = lambda n: pl.BlockSpec((tm, n), lambda i, f: (i, 0))
    full = lambda a: pl.BlockSpec(a.shape, lambda i, f: (0,) * a.ndim)
    return pl.pallas_call(
        _out_mlp_kernel,
        grid=(T // tm, D_FF // tf),
        in_specs=[row(ML_V_W), row(MOBA_W), row(DSA_W), row(D), full(w_out), full(g_post), full(g_pre),
                  pl.BlockSpec((D, tf), lambda i, f: (0, f)),
                  pl.BlockSpec((tf, D), lambda i, f: (f, 0)),
                  full(g2)],
        out_specs=row(D),
        out_shape=jax.ShapeDtypeStruct((T, D), F32),
        scratch_shapes=[pltpu.VMEM((tm, D), F32), pltpu.VMEM((tm, D), BF16), pltpu.VMEM((tm, D), F32)],
        compiler_params=_params("parallel", "arbitrary"),
    )(yml, ymb, yds, x2, w_out, g_post, g_pre, w1, w2, g2)


def kernel(x, norm_mix_pre, w_in, ml_conv, ml_i_bias, ml_f_bias, ml_head_norm, w_out, norm_mix_post,
           norm_mlp_pre, w_ff1, w_ff2, norm_mlp_post):
    B, S, D = x.shape
    depth = w_in.shape[0]
    x2 = x.reshape(B * S, D)
    gain = lambda g: g.reshape(1, D).astype(F32)
    for l in range(depth):
        wn, wt = _prep_in_weights(w_in[l])
        (qk, v, o, sm, mbk, dsk, ixk, mbqT, mbvT, dsqT, dsvT, ixqT, smT) = _in_proj(
            x2, gain(norm_mix_pre[l]), wn, wt, B, S)
        y_ml = _mlstm(qk, v, o, sm, smT, ml_conv[l], ml_i_bias[l], ml_f_bias[l], ml_head_norm[l], B, S)
        y_mb = _moba(mbqT, mbk, mbvT, B, S)
        y_ds = _dsa(dsqT, dsk, dsvT, ixqT, ixk, smT, B, S)
        x2 = _out_mlp(y_ml.reshape(B * S, ML_V_W), y_mb.reshape(B * S, MOBA_W), y_ds.reshape(B * S, DSA_W),
                      x2, w_out[l].astype(BF16), gain(norm_mix_post[l]), gain(norm_mlp_pre[l]),
                      w_ff1[l].astype(BF16), w_ff2[l].astype(BF16), gain(norm_mlp_post[l]))
    return x2.reshape(B, S, D)
```

```python
import functools

import jax
import jax.numpy as jnp
from jax import lax
from jax.experimental import pallas as pl
from jax.experimental.pallas import tpu as pltpu

F32 = jnp.float32
BF16 = jnp.bfloat16
I32 = jnp.int32

D_MODEL = 1024
ML_HEADS, ML_DQK, ML_DV, ML_CONV = 4, 64, 128, 4
MOBA_HEADS, MOBA_DH, MOBA_BLOCK, MOBA_TOPK = 4, 64, 256, 3
DSA_HEADS, DSA_DH, IDX_HEADS, IDX_DH, DSA_TOPK_MAX = 4, 64, 4, 64, 256
D_FF = 4 * D_MODEL
EPS = 1e-6

ML_QK_W = ML_HEADS * ML_DQK
ML_V_W = ML_HEADS * ML_DV
MOBA_W = MOBA_HEADS * MOBA_DH
DSA_W = DSA_HEADS * DSA_DH
IN_SPLITS = (ML_QK_W, ML_QK_W, ML_V_W, ML_V_W, ML_HEADS, ML_HEADS,
             MOBA_W, MOBA_W, MOBA_W,
             DSA_W, DSA_W, DSA_W, IDX_HEADS * IDX_DH, IDX_DH, IDX_HEADS)

LANES = 128
SUBLANES = 8
VMEM_LIMIT = 52 * 1024 * 1024

ML_CHUNK = 256
ATT_TQ = 256
PROJ_TM = 512
MLP_TM = 1024
MLP_TF = 512

NEG_INF = float("-inf")
BISECT_CAP = 300
BISECT_STEPS_PER_CHECK = 3
SOFTMAX_L_MIN = 1e-30
assert MOBA_DH == DSA_DH
ATT_QSCALE = MOBA_DH ** -0.5 * 1.4426950408889634

_NT = (((1,), (1,)), ((), ()))
_TN = (((0,), (0,)), ((), ()))


def _rms(x, g):
    return x * lax.rsqrt(jnp.mean(x * x, axis=-1, keepdims=True) + EPS) * g


def _log_sigmoid(x):
    return jnp.minimum(x, 0.0) - jnp.log1p(jnp.exp(-jnp.abs(x)))


def _params(*sem):
    return pltpu.CompilerParams(dimension_semantics=sem, vmem_limit_bytes=VMEM_LIMIT)


_N_QK, _N_V, _N_O, _N_MBK, _N_DSK, _N_IXK, _N_SM = 0, 512, 1024, 1536, 1792, 2048, 2176
_N_TOTAL = 2304
_T_MBQ, _T_MBV, _T_DSQ, _T_DSV, _T_IXQ, _T_SM = 0, 256, 512, 768, 1024, 1280
_T_TOTAL = 1296


def _in_proj_kernel(x_ref, g_ref, wn_ref, wt_ref,
                    qk_ref, v_ref, o_ref, sm_ref, mbk_ref, dsk_ref, ixk_ref,
                    mbqT_ref, mbvT_ref, dsqT_ref, dsvT_ref, ixqT_ref, smT_ref):
    h = _rms(x_ref[...], g_ref[...]).astype(BF16)

    def mm(a, n):
        return jnp.dot(h, wn_ref[:, a:a + n], preferred_element_type=F32)

    def mt(a, n):
        return lax.dot_general(wt_ref[a:a + n, :], h, _NT, preferred_element_type=F32)

    qk_ref[...] = mm(_N_QK, 512)
    v_ref[...] = mm(_N_V, 512).astype(BF16)
    o_ref[...] = mm(_N_O, 512)
    mbk_ref[...] = mm(_N_MBK, 256).astype(BF16)
    dsk_ref[...] = mm(_N_DSK, 256).astype(BF16)
    ixk_ref[...] = mm(_N_IXK, 128).astype(BF16)
    sm_ref[...] = mm(_N_SM, 128)
    mbqT_ref[0] = (mt(_T_MBQ, 256) * ATT_QSCALE).astype(BF16)
    mbvT_ref[0] = mt(_T_MBV, 256).astype(BF16)
    dsqT_ref[0] = (mt(_T_DSQ, 256) * ATT_QSCALE).astype(BF16)
    dsvT_ref[0] = mt(_T_DSV, 256).astype(BF16)
    ixqT_ref[0] = mt(_T_IXQ, 256).astype(BF16)
    smT_ref[0] = mt(_T_SM, 16)


def _prep_in_weights(w_in):
    pts, acc = [], 0
    for n in IN_SPLITS:
        pts.append((acc, acc + n))
        acc += n
    col = lambda i: w_in[:, pts[i][0]:pts[i][1]]
    (ml_q, ml_k, ml_v, ml_o, ml_i, ml_f, mb_q, mb_k, mb_v,
     ds_q, ds_k, ds_v, ix_q, ix_k, ix_w) = [col(i) for i in range(len(IN_SPLITS))]
    d = w_in.shape[0]
    small = jnp.concatenate([ml_i, ml_f, ix_w], axis=1)
    wn = jnp.concatenate([
        ml_q, ml_k, ml_v, ml_o, mb_k, ds_k,
        ix_k, jnp.zeros((d, 128 - IDX_DH), w_in.dtype),
        small, jnp.zeros((d, 128 - small.shape[1]), w_in.dtype)], axis=1)
    wt = jnp.concatenate([
        mb_q, mb_v, ds_q, ds_v, ix_q,
        small, jnp.zeros((d, 16 - small.shape[1]), w_in.dtype)], axis=1).T
    assert wn.shape == (d, _N_TOTAL) and wt.shape == (_T_TOTAL, d)
    return wn.astype(BF16), wt.astype(BF16)


def _in_proj(x2, g, wn, wt, B, S):
    T, D = x2.shape
    tm = min(PROJ_TM, S)
    nsb = S // tm
    row = lambda n: pl.BlockSpec((tm, n), lambda i: (i, 0))
    tr = lambda n: pl.BlockSpec((1, n, tm), lambda i: (i // nsb, 0, i % nsb))
    full = lambda a: pl.BlockSpec(a.shape, lambda i: (0,) * a.ndim)
    out_shape = (
        jax.ShapeDtypeStruct((T, 512), F32),
        jax.ShapeDtypeStruct((T, 512), BF16),
        jax.ShapeDtypeStruct((T, 512), F32),
        jax.ShapeDtypeStruct((T, 128), F32),
        jax.ShapeDtypeStruct((T, 256), BF16),
        jax.ShapeDtypeStruct((T, 256), BF16),
        jax.ShapeDtypeStruct((T, 128), BF16),
        jax.ShapeDtypeStruct((B, 256, S), BF16),
        jax.ShapeDtypeStruct((B, 256, S), BF16),
        jax.ShapeDtypeStruct((B, 256, S), BF16),
        jax.ShapeDtypeStruct((B, 256, S), BF16),
        jax.ShapeDtypeStruct((B, 256, S), BF16),
        jax.ShapeDtypeStruct((B, 16, S), F32),
    )
    out_specs = (row(512), row(512), row(512), row(128), row(256), row(256), row(128),
                 tr(256), tr(256), tr(256), tr(256), tr(256), tr(16))
    return pl.pallas_call(
        _in_proj_kernel,
        grid=(T // tm,),
        in_specs=[row(D), full(g), full(wn), full(wt)],
        out_specs=out_specs,
        out_shape=out_shape,
        compiler_params=_params("parallel"),
    )(x2, g, wn, wt)


def _mlstm_kernel(qk_ref, v_ref, o_ref, sm_ref, smT_ref, conv_ref, brow_ref, bcol_ref, hn_ref,
                  y_ref, xp_scr, st_scr, m_scr, *, L):
    c = pl.program_id(1)
    halo = SUBLANES

    @pl.when(c == 0)
    def _():
        xp_scr[0:halo, :] = jnp.zeros((halo, 2 * ML_QK_W), F32)
        st_scr[...] = jnp.zeros_like(st_scr)
        m_scr[...] = jnp.zeros_like(m_scr)

    cur = qk_ref[0]
    xp_scr[halo:halo + L, :] = cur
    base = halo - (ML_CONV - 1)
    acc = conv_ref[0:1, :] * xp_scr[base:base + L, :]
    for j in range(1, ML_CONV):
        acc = acc + conv_ref[j:j + 1, :] * xp_scr[base + j:base + j + L, :]
    xp_scr[0:halo, :] = cur[L - halo:L, :]
    qk = acc * jax.nn.sigmoid(acc)
    q_all = (qk[:, :ML_QK_W] * (ML_DQK ** -0.5)).astype(BF16)
    k_all = qk[:, ML_QK_W:]

    gcol = sm_ref[0] + brow_ref[...]
    grow = smT_ref[0][0:SUBLANES, :] + bcol_ref[...]
    ri = lax.broadcasted_iota(I32, (L, L), 0)
    ci = lax.broadcasted_iota(I32, (L, L), 1)
    causal = ci <= ri
    tril = causal.astype(F32)
    triu = (ri <= ci).astype(F32)
    hp = lax.Precision.HIGHEST
    b_col = jnp.dot(tril, _log_sigmoid(gcol), precision=hp, preferred_element_type=F32)
    b_row = jnp.dot(_log_sigmoid(grow), triu, precision=hp, preferred_element_type=F32)

    v_all = v_ref[0]
    o_all = o_ref[0]
    lane = lax.broadcasted_iota(I32, (L, LANES), 1)
    ones_col = jnp.where(lane == 0, 1.0, 0.0).astype(BF16)
    for h in range(ML_HEADS):
        qh = q_all[:, h * ML_DQK:(h + 1) * ML_DQK]
        kh = k_all[:, h * ML_DQK:(h + 1) * ML_DQK]
        vh = v_all[:, h * ML_DV:(h + 1) * ML_DV]
        bj = b_col[:, ML_HEADS + h:ML_HEADS + h + 1]
        icol = gcol[:, h:h + 1]
        r_row = grow[h:h + 1, :] - b_row[ML_HEADS + h:ML_HEADS + h + 1, :]
        dlog = jnp.where(causal, bj + r_row, NEG_INF)
        m0 = m_scr[h:h + 1, 0:1]
        inter = bj + m0
        m_t = jnp.maximum(inter, jnp.max(dlog, axis=-1, keepdims=True))
        w_inter = jnp.exp(inter - m_t)
        qk_s = lax.dot_general(qh, kh.astype(BF16), _NT, preferred_element_type=F32)
        w_intra = jnp.exp(dlog - m_t) * qk_s
        st = st_scr[h]
        cq = jnp.dot(qh, st.astype(BF16), preferred_element_type=F32)
        num = w_inter * cq[:, :ML_DV] + jnp.dot(w_intra.astype(BF16), vh, preferred_element_type=F32)
        den = w_inter * cq[:, ML_DV:ML_DV + 1] + jnp.sum(w_intra, axis=-1, keepdims=True)
        hh = num / jnp.maximum(jnp.abs(den), jnp.exp(-m_t))
        hh = hh * lax.rsqrt(jnp.mean(hh * hh, axis=-1, keepdims=True) + EPS)
        hh = hh * hn_ref[:, h * ML_DV:(h + 1) * ML_DV]
        y_ref[0, :, h * ML_DV:(h + 1) * ML_DV] = (
            jax.nn.sigmoid(o_all[:, h * ML_DV:(h + 1) * ML_DV]) * hh).astype(y_ref.dtype)

        b_last = bj[L - 1:L, :]
        a = b_last - bj + icol
        m_loc = jnp.max(a, axis=0, keepdims=True)
        wa = jnp.exp(a - m_loc)
        kw = (kh * wa).astype(BF16)
        vaug = jnp.concatenate([vh, ones_col], axis=1)
        c_loc = lax.dot_general(kw, vaug, _TN, preferred_element_type=F32)
        m_new = jnp.maximum(b_last + m0, m_loc)
        s_old = jnp.exp(b_last + m0 - m_new)
        s_loc = jnp.exp(m_loc - m_new)
        st_scr[h] = s_old * st + s_loc * c_loc
        m_scr[h:h + 1, :] = jnp.broadcast_to(m_new, (1, LANES))


def _mlstm(qk, v, o, sm, smT, conv_w, i_bias, f_bias, head_norm, B, S):
    L = min(ML_CHUNK, S)
    nc = S // L
    qk = qk.reshape(B, S, 2 * ML_QK_W)
    v = v.reshape(B, S, ML_V_W)
    o = o.reshape(B, S, ML_V_W)
    sm = sm.reshape(B, S, LANES)
    bias = jnp.concatenate([i_bias, f_bias]).astype(F32)
    brow = jnp.zeros((1, LANES), F32).at[0, :2 * ML_HEADS].set(bias)
    bcol = bias.reshape(2 * ML_HEADS, 1)
    hn = head_norm.reshape(1, ML_V_W).astype(F32)
    blk = lambda n: pl.BlockSpec((1, L, n), lambda b, c: (b, c, 0))
    full = lambda a: pl.BlockSpec(a.shape, lambda b, c: (0,) * a.ndim)
    return pl.pallas_call(
        functools.partial(_mlstm_kernel, L=L),
        grid=(B, nc),
        in_specs=[blk(2 * ML_QK_W), blk(ML_V_W), blk(ML_V_W), blk(LANES),
                  pl.BlockSpec((1, 16, L), lambda b, c: (b, 0, c)),
                  full(conv_w), full(brow), full(bcol), full(hn)],
        out_specs=blk(ML_V_W),
        out_shape=jax.ShapeDtypeStruct((B, S, ML_V_W), BF16),
        scratch_shapes=[pltpu.VMEM((L + SUBLANES, 2 * ML_QK_W), F32),
                        pltpu.VMEM((ML_HEADS, ML_DQK, ML_DV + LANES), F32),
                        pltpu.VMEM((SUBLANES, LANES), F32)],
        compiler_params=_params("parallel", "arbitrary"),
    )(qk, v, o, sm, smT, conv_w.astype(F32), brow, bcol, hn)


def _block_diag_T(qT, heads, dh):
    rid = lax.broadcasted_iota(I32, qT.shape, 0)
    zero = jnp.zeros_like(qT)
    return jnp.concatenate(
        [jnp.where((rid >= h * dh) & (rid < (h + 1) * dh), qT, zero) for h in range(heads)], axis=1)


def _scores(k_chunk, qs, h, TQ):
    return jnp.dot(k_chunk, qs[:, h * TQ:(h + 1) * TQ], preferred_element_type=F32)


def _accumulate(h, p, vT_chunk, l, acc_scr, dh):
    rows = slice(h * dh, (h + 1) * dh)
    acc_scr[rows, :] += jnp.dot(vT_chunk[rows, :], p.astype(BF16), preferred_element_type=F32)
    return l + jnp.sum(p, axis=0, keepdims=True)


def _finish_attention(l, acc_scr, y_ref, heads, dh):
    outs = [acc_scr[h * dh:(h + 1) * dh, :] / l[h] for h in range(heads)]
    y_ref[0] = jnp.concatenate(outs, axis=0).T.astype(y_ref.dtype)


def _max_key_norm2(kb, heads, dh, TQ):
    ch = lax.broadcasted_iota(I32, (heads * dh, LANES), 0) // dh
    hd = lax.broadcasted_iota(I32, (heads * dh, LANES), 1)
    n2 = jnp.dot(kb * kb, (ch == hd).astype(F32), precision=lax.Precision.HIGHEST, preferred_element_type=F32)
    n2 = jnp.max(n2, axis=0, keepdims=True)
    return jnp.concatenate([jnp.broadcast_to(n2[:, h:h + 1], (1, TQ)) for h in range(heads)], axis=1)


def _norm_bound(qTf, k2, heads, dh, TQ):
    out = []
    for h in range(heads):
        qh = qTf[h * dh:(h + 1) * dh, :]
        q2 = jnp.sum(qh * qh, axis=0, keepdims=True)
        out.append(jnp.sqrt(q2 * k2[:, h * TQ:(h + 1) * TQ]) * (1.0 + 2.0 ** -8))
    return out


def _underflowed(l):
    small = l[0]
    for lh in l[1:]:
        small = jnp.minimum(small, lh)
    return jnp.logical_not(jnp.min(small) >= SOFTMAX_L_MIN)


def _moba_kernel(qT_ref, k_ref, vT_ref, y_ref, km_scr, kn_scr, acc_scr, s_scr, *, NB, NBP, n_sel):
    H, DH, BS = MOBA_HEADS, MOBA_DH, MOBA_BLOCK
    TQ = BS
    qi = pl.program_id(1)

    @pl.when(qi == 0)
    def _():
        km_scr[...] = jnp.zeros_like(km_scr)
        kn_scr[...] = jnp.zeros_like(kn_scr)

        def body(j, carry):
            kb = k_ref[0, pl.ds(pl.multiple_of(j * BS, BS), BS), :].astype(F32)
            km_scr[pl.ds(j, 1), :] = jnp.mean(kb, axis=0, keepdims=True)
            kn_scr[pl.ds(j, 1), :] = _max_key_norm2(kb, H, DH, TQ)
            return carry
        lax.fori_loop(0, NB, body, 0)

    qT = qT_ref[0]
    qTf = qT.astype(F32)
    km = km_scr[...]
    gate = jnp.concatenate(
        [jnp.dot(km[:, h * DH:(h + 1) * DH], qTf[h * DH:(h + 1) * DH, :], precision=lax.Precision.HIGHEST,
                 preferred_element_type=F32) for h in range(H)], axis=1) * (1.0 / ATT_QSCALE)
    blk = lax.broadcasted_iota(I32, gate.shape, 0)
    g = jnp.where(blk < qi, gate, NEG_INF)
    sels = []
    for _ in range(n_sel):
        mx = jnp.max(g, axis=0, keepdims=True)
        isel = jnp.min(jnp.where(g == mx, blk, NBP), axis=0, keepdims=True)
        sels.append(jnp.where(mx > NEG_INF, isel, -1))
        g = jnp.where(blk == isel, NEG_INF, g)

    qs = _block_diag_T(qT, H, DH)

    def hit_row(j):
        hit = sels[0] == j
        for s in sels[1:]:
            hit = hit | (s == j)
        return hit

    npair = (qi + 1) // 2
    blk_start = lambda j: pl.multiple_of(j * BS, BS)
    blk_scores = lambda j: jnp.dot(k_ref[0, pl.ds(blk_start(j), BS), :], qs, preferred_element_type=F32)
    hcols = lambda row, h: row[:, h * TQ:(h + 1) * TQ]
    causal = lax.broadcasted_iota(I32, (BS, TQ), 0) <= lax.broadcasted_iota(I32, (BS, TQ), 1)

    def diag_scores():
        s_all = blk_scores(qi)
        return [jnp.where(causal, hcols(s_all, h), NEG_INF) for h in range(H)]

    def max_sweep():
        def max_pair(g, m):
            for u in range(2):
                j = 2 * g + u
                m = jnp.maximum(m, jnp.where(hit_row(j), jnp.max(blk_scores(j), axis=0, keepdims=True), NEG_INF))
            return m
        m_all = lax.fori_loop(0, npair, max_pair, jnp.full((1, H * TQ), NEG_INF, F32))
        s_diag = diag_scores()
        return [jnp.maximum(hcols(m_all, h), jnp.max(s_diag[h], axis=0, keepdims=True)) for h in range(H)]

    def sum_sweep(m):
        acc_scr[...] = jnp.zeros_like(acc_scr)

        def sum_group(width):
            def body(g, l):
                for u in range(width):
                    s_scr[u] = blk_scores(width * g + u)
                for u in range(width):
                    j = width * g + u
                    hit = hit_row(j)
                    vT_blk = vT_ref[0, :, pl.ds(blk_start(j), BS)]
                    l = tuple(_accumulate(h, jnp.exp2(s_scr[u, :, h * TQ:(h + 1) * TQ]
                                                      - jnp.where(hcols(hit, h), m[h], jnp.inf)),
                                          vT_blk, l[h], acc_scr, DH) for h in range(H))
                return l
            return body
        nquad = qi // 4
        l = lax.fori_loop(0, nquad, sum_group(4), (jnp.zeros((1, TQ), F32),) * H)
        l = lax.fori_loop(2 * nquad, npair, sum_group(2), l)
        s_diag = diag_scores()
        vT_diag = vT_ref[0, :, pl.ds(blk_start(qi), BS)]
        return [_accumulate(h, jnp.exp2(s_diag[h] - m[h]), vT_diag, l[h], acc_scr, DH) for h in range(H)]

    blk_n = lax.broadcasted_iota(I32, kn_scr.shape, 0)
    k2 = jnp.max(jnp.where(blk_n <= qi, kn_scr[...], 0.0), axis=0, keepdims=True)
    l = sum_sweep(_norm_bound(qTf, k2, H, DH, TQ))
    _finish_attention(l, acc_scr, y_ref, H, DH)

    @pl.when(_underflowed(l))
    def _():
        _finish_attention(sum_sweep(max_sweep()), acc_scr, y_ref, H, DH)


def _moba(qT, k, vT, B, S):
    BS = MOBA_BLOCK
    assert S % BS == 0
    NB = S // BS
    NBP = -(-NB // SUBLANES) * SUBLANES
    n_sel = max(min(MOBA_TOPK, NB - 1), 1)
    k = k.reshape(B, S, MOBA_W)
    return pl.pallas_call(
        functools.partial(_moba_kernel, NB=NB, NBP=NBP, n_sel=n_sel),
        grid=(B, NB),
        in_specs=[pl.BlockSpec((1, MOBA_W, BS), lambda b, i: (b, 0, i)),
                  pl.BlockSpec((1, S, MOBA_W), lambda b, i: (b, 0, 0)),
                  pl.BlockSpec((1, MOBA_W, S), lambda b, i: (b, 0, 0))],
        out_specs=pl.BlockSpec((1, BS, MOBA_W), lambda b, i: (b, i, 0)),
        out_shape=jax.ShapeDtypeStruct((B, S, MOBA_W), BF16),
        scratch_shapes=[pltpu.VMEM((NBP, MOBA_W), F32),
                        pltpu.VMEM((NBP, MOBA_HEADS * BS), F32),
                        pltpu.VMEM((MOBA_W, BS), F32),
                        pltpu.VMEM((4, BS, MOBA_HEADS * BS), F32)],
        compiler_params=_params("parallel", "arbitrary"),
    )(qT, k, vT)


def _dsa_kernel(qT_ref, k_ref, vT_ref, iqT_ref, ik_ref, smT_ref, y_ref,
                e_scr, gmax_scr, acc_scr, thr_scr, kn_scr, s_scr, *, TQ, topk, spow, n_chunks):
    H, DH = DSA_HEADS, DSA_DH
    KC = TQ
    qi = pl.program_id(1)
    nch = qi + 1
    npair = (nch + 1) // 2
    chunk_start = lambda c: pl.multiple_of(c * KC, KC)

    @pl.when(qi == 0)
    def _():
        kn_scr[...] = jnp.zeros_like(kn_scr)

        def body(c, carry):
            kn_scr[pl.ds(c, 1), :] = _max_key_norm2(k_ref[0, pl.ds(chunk_start(c), KC), :].astype(F32), H, DH, TQ)
            return carry
        lax.fori_loop(0, n_chunks, body, 0)
    sum8 = lambda b: jnp.sum(b.astype(I32).reshape(KC // SUBLANES, SUBLANES, TQ), axis=0)
    any_row = lambda r: jnp.max(r.astype(I32)) > 0

    iqT = iqT_ref[0]
    iq_cat = jnp.concatenate([iqT[h * IDX_DH:(h + 1) * IDX_DH, :] for h in range(IDX_HEADS)], axis=1)
    iw = smT_ref[0][2 * ML_HEADS:2 * ML_HEADS + IDX_HEADS, :] * (IDX_HEADS ** -0.5 * IDX_DH ** -0.5)
    t_pos = qi * TQ + lax.broadcasted_iota(I32, (KC, TQ), 1)
    s_off = lax.broadcasted_iota(I32, (KC, TQ), 0)
    gmax_scr[...] = jnp.full((KC, TQ), NEG_INF, F32)

    def score_pair(g, carry):
        n_pos, n_nonneg = carry
        for u in range(2):
            c = 2 * g + u
            ikc = ik_ref[0, pl.ds(chunk_start(c), KC), :][:, :IDX_DH]
            rel = jnp.dot(ikc, iq_cat, preferred_element_type=F32)
            sc = iw[0:1, :] * jnp.maximum(rel[:, 0:TQ], 0.0)
            for h in range(1, IDX_HEADS):
                sc = sc + iw[h:h + 1, :] * jnp.maximum(rel[:, h * TQ:(h + 1) * TQ], 0.0)
            sc = jnp.where(c * KC + s_off <= t_pos, sc, NEG_INF)
            e_scr[pl.ds(chunk_start(c), KC), :] = sc
            gmax_scr[...] = jnp.maximum(gmax_scr[...], sc)
            n_pos = n_pos + sum8(sc > 0.0)
            n_nonneg = n_nonneg + sum8(sc >= 0.0)
        return n_pos, n_nonneg
    zero8 = jnp.zeros((SUBLANES, TQ), I32)
    n_pos, n_nonneg = lax.fori_loop(0, npair, score_pair, (zero8, zero8))
    n_pos = jnp.sum(n_pos, axis=0, keepdims=True)
    n_nonneg = jnp.sum(n_nonneg, axis=0, keepdims=True)

    def count_ge(cand):
        def body(g, acc):
            a, b = acc
            a = a + sum8(e_scr[pl.ds(chunk_start(2 * g), KC), :] >= cand)
            b = b + sum8(e_scr[pl.ds(chunk_start(2 * g + 1), KC), :] >= cand)
            return a, b
        a, b = lax.fori_loop(0, npair, body, (zero8, zero8))
        return jnp.sum(a + b, axis=0, keepdims=True)

    def rewrite(rows, thr):
        def body(c, carry):
            x = e_scr[pl.ds(chunk_start(c), KC), :]
            rank = (2 * spow - (c * KC + s_off)).astype(F32)
            new = jnp.where(x > thr, jnp.inf, jnp.where(x == thr, rank, NEG_INF))
            e_scr[pl.ds(chunk_start(c), KC), :] = jnp.where(rows, new, x)
            return carry
        lax.fori_loop(0, nch, body, 0)

    def bisect(lo, hi, act):
        def cond(c):
            return jnp.logical_and(c[0] < BISECT_CAP, c[5] > 0)

        def body(c):
            it, lo, hi, act, tie, _ = c
            for _ in range(BISECT_STEPS_PER_CHECK):
                mid = lo + (hi - lo) * 0.5
                inside = (mid > lo) & (mid < hi)
                cnt = count_ge(mid)
                live = act > 0.0
                run = live & inside
                up = cnt >= topk
                lo = jnp.where(run & up, mid, lo)
                hi = jnp.where(run & jnp.logical_not(up), mid, hi)
                collapsed = live & jnp.logical_not(inside)
                tie = jnp.where(collapsed, 1.0, tie)
                act = jnp.where(collapsed | (run & (cnt == topk)), 0.0, act)
            return it + BISECT_STEPS_PER_CHECK, lo, hi, act, tie, jnp.max(act)
        out = lax.while_loop(cond, body, (jnp.int32(0), lo, hi, act, jnp.zeros_like(act), jnp.max(act)))
        return out[1], out[4]

    fmax = float(jnp.finfo(F32).max)
    gm = gmax_scr[...]
    g_lo = jnp.min(gm, axis=0, keepdims=True)
    g_hi = jnp.max(gm, axis=0, keepdims=True)
    n_vis = qi * TQ + lax.broadcasted_iota(I32, (1, TQ), 1) + 1
    few = n_vis <= topk
    zero_tie = (n_pos < topk) & (n_nonneg >= topk) & jnp.logical_not(few)

    @pl.when(any_row(zero_tie))
    def _():
        rewrite(zero_tie, jnp.zeros((1, TQ), F32))

    rank_lo, rank_hi = float(spow), float(2 * spow + 1)
    lo0 = jnp.where(zero_tie, rank_lo, jnp.maximum(g_lo, -fmax))
    hi0 = jnp.where(zero_tie, rank_hi, g_hi + jnp.abs(g_hi) * (2.0 ** -10) + 1e-30)
    lo, tie = bisect(lo0, hi0, jnp.where(few, 0.0, 1.0))
    thr_scr[0:1, :] = jnp.where(few, -fmax, lo)

    @pl.when(jnp.max(tie) > 0.0)
    def _():
        rows = tie > 0.0
        rewrite(rows, lo)
        lo2, _ = bisect(jnp.full((1, TQ), rank_lo, F32), jnp.full((1, TQ), rank_hi, F32), tie)
        thr_scr[0:1, :] = jnp.where(rows, lo2, thr_scr[0:1, :])

    thr = thr_scr[0:1, :]
    qT = qT_ref[0]
    qs = _block_diag_T(qT, H, DH)
    hcols = lambda row, h: row[:, h * TQ:(h + 1) * TQ]

    def chunk_scores(c):
        start = chunk_start(c)
        sel = e_scr[pl.ds(start, KC), :] >= thr
        s_all = jnp.dot(k_ref[0, pl.ds(start, KC), :], qs, preferred_element_type=F32)
        return sel, s_all

    def max_sweep():
        def max_pair(g, m):
            for u in range(2):
                sel, s_all = chunk_scores(2 * g + u)
                m = tuple(jnp.maximum(m[h], jnp.max(jnp.where(sel, hcols(s_all, h), NEG_INF), axis=0, keepdims=True))
                          for h in range(H))
            return m
        m = lax.fori_loop(0, npair, max_pair, (jnp.full((1, TQ), NEG_INF, F32),) * H)
        return [jnp.where(mh == NEG_INF, 0.0, mh) for mh in m]

    def sum_sweep(m):
        acc_scr[...] = jnp.zeros_like(acc_scr)

        def sum_group(width):
            def body(g, l):
                for u in range(width):
                    s_scr[u] = jnp.dot(k_ref[0, pl.ds(chunk_start(width * g + u), KC), :], qs,
                                       preferred_element_type=F32)
                for u in range(width):
                    start = chunk_start(width * g + u)
                    sel = e_scr[pl.ds(start, KC), :] >= thr
                    vT_chunk = vT_ref[0, :, pl.ds(start, KC)]
                    l = tuple(_accumulate(h, jnp.exp2(s_scr[u, :, h * TQ:(h + 1) * TQ] + jnp.where(sel, -m[h], NEG_INF)),
                                          vT_chunk, l[h], acc_scr, DH) for h in range(H))
                return l
            return body
        nquad = npair // 2
        l = lax.fori_loop(0, nquad, sum_group(4), (jnp.zeros((1, TQ), F32),) * H)
        return lax.fori_loop(2 * nquad, npair, sum_group(2), l)

    chunk_n = lax.broadcasted_iota(I32, kn_scr.shape, 0)
    k2 = jnp.max(jnp.where(chunk_n < 2 * npair, kn_scr[...], 0.0), axis=0, keepdims=True)
    l = sum_sweep(_norm_bound(qT.astype(F32), k2, H, DH, TQ))
    _finish_attention(l, acc_scr, y_ref, H, DH)

    @pl.when(_underflowed(l))
    def _():
        _finish_attention(sum_sweep(max_sweep()), acc_scr, y_ref, H, DH)


def _dsa(qT, k, vT, iqT, ik, smT, B, S):
    TQ = min(ATT_TQ, S)
    topk = min(DSA_TOPK_MAX, S // 4)
    assert S % (2 * TQ) == 0 and topk <= TQ
    spow = 1 << max((S - 1).bit_length(), 1)
    k = k.reshape(B, S, DSA_W)
    ik = ik.reshape(B, S, LANES)
    qblk = lambda n: pl.BlockSpec((1, n, TQ), lambda b, i: (b, 0, i))
    return pl.pallas_call(
        functools.partial(_dsa_kernel, TQ=TQ, topk=topk, spow=spow, n_chunks=S // TQ),
        grid=(B, S // TQ),
        in_specs=[qblk(DSA_W),
                  pl.BlockSpec((1, S, DSA_W), lambda b, i: (b, 0, 0)),
                  pl.BlockSpec((1, DSA_W, S), lambda b, i: (b, 0, 0)),
                  qblk(IDX_HEADS * IDX_DH),
                  pl.BlockSpec((1, S, LANES), lambda b, i: (b, 0, 0)),
                  qblk(16)],
        out_specs=pl.BlockSpec((1, TQ, DSA_W), lambda b, i: (b, i, 0)),
        out_shape=jax.ShapeDtypeStruct((B, S, DSA_W), BF16),
        scratch_shapes=[pltpu.VMEM((S, TQ), F32),
                        pltpu.VMEM((TQ, TQ), F32),
                        pltpu.VMEM((DSA_W, TQ), F32),
                        pltpu.VMEM((SUBLANES, TQ), F32),
                        pltpu.VMEM((-(-(S // TQ) // SUBLANES) * SUBLANES, DSA_HEADS * TQ), F32),
                        pltpu.VMEM((4, TQ, DSA_HEADS * TQ), F32)],
        compiler_params=_params("parallel", "arbitrary"),
    )(qT, k, vT, iqT, ik, smT)


def _out_mlp_kernel(yml_ref, ymb_ref, yds_ref, x_ref, wout_ref, gpost_ref, gpre_ref,
                    w1_ref, w2_ref, g2_ref, out_ref, x1_scr, h_scr, acc_scr):
    f = pl.program_id(1)

    @pl.when(f == 0)
    def _():
        mix = jnp.dot(yml_ref[...], wout_ref[0:ML_V_W, :], preferred_element_type=F32)
        mix = mix + jnp.dot(ymb_ref[...], wout_ref[ML_V_W:ML_V_W + MOBA_W, :], preferred_element_type=F32)
        mix = mix + jnp.dot(yds_ref[...], wout_ref[ML_V_W + MOBA_W:, :], preferred_element_type=F32)
        x1 = x_ref[...] + _rms(mix, gpost_ref[...])
        x1_scr[...] = x1
        h_scr[...] = _rms(x1, gpre_ref[...]).astype(BF16)
        acc_scr[...] = jnp.zeros_like(acc_scr)

    u = jnp.maximum(jnp.dot(h_scr[...], w1_ref[...], preferred_element_type=F32), 0.0)
    acc_scr[...] += jnp.dot((u * u).astype(BF16), w2_ref[...], preferred_element_type=F32)

    @pl.when(f == pl.num_programs(1) - 1)
    def _():
        out_ref[...] = x1_scr[...] + _rms(acc_scr[...], g2_ref[...])


def _out_mlp(yml, ymb, yds, x2, w_out, g_post, g_pre, w1, w2, g2):
    T, D = x2.shape
    tm = min(MLP_TM, T)
    tf = min(MLP_TF, D_FF)
    row = lambda n: pl.BlockSpec((tm, n), lambda i, f: (i, 0))
    full = lambda a: pl.BlockSpec(a.shape, lambda i, f: (0,) * a.ndim)
    return pl.pallas_call(
        _out_mlp_kernel,
        grid=(T // tm, D_FF // tf),
        in_specs=[row(ML_V_W), row(MOBA_W), row(DSA_W), row(D), full(w_out), full(g_post), full(g_pre),
                  pl.BlockSpec((D, tf), lambda i, f: (0, f)),
                  pl.BlockSpec((tf, D), lambda i, f: (f, 0)),
                  full(g2)],
        out_specs=row(D),
        out_shape=jax.ShapeDtypeStruct((T, D), F32),
        scratch_shapes=[pltpu.VMEM((tm, D), F32), pltpu.VMEM((tm, D), BF16), pltpu.VMEM((tm, D), F32)],
        compiler_params=_params("parallel", "arbitrary"),
    )(yml, ymb, yds, x2, w_out, g_post, g_pre, w1, w2, g2)


def kernel(x, norm_mix_pre, w_in, ml_conv, ml_i_bias, ml_f_bias, ml_head_norm, w_out, norm_mix_post,
           norm_mlp_pre, w_ff1, w_ff2, norm_mlp_post):
    B, S, D = x.shape
    depth = w_in.shape[0]
    x2 = x.reshape(B * S, D)
    gain = lambda g: g.reshape(1, D).astype(F32)
    for l in range(depth):
        wn, wt = _prep_in_weights(w_in[l])
        (qk, v, o, sm, mbk, dsk, ixk, mbqT, mbvT, dsqT, dsvT, ixqT, smT) = _in_proj(
            x2, gain(norm_mix_pre[l]), wn, wt, B, S)
        y_ml = _mlstm(qk, v, o, sm, smT, ml_conv[l], ml_i_bias[l], ml_f_bias[l], ml_head_norm[l], B, S)
        y_mb = _moba(mbqT, mbk, mbvT, B, S)
        y_ds = _dsa(dsqT, dsk, dsvT, ixqT, ixk, smT, B, S)
        x2 = _out_mlp(y_ml.reshape(B * S, ML_V_W), y_mb.reshape(B * S, MOBA_W), y_ds.reshape(B * S, DSA_W),
                      x2, w_out[l].astype(BF16), gain(norm_mix_post[l]), gain(norm_mlp_pre[l]),
                      w_ff1[l].astype(BF16), w_ff2[l].astype(BF16), gain(norm_mlp_post[l]))
    return x2.reshape(B, S, D)
```

```python
import functools

import jax
import jax.numpy as jnp
from jax import lax
from jax.experimental import pallas as pl
from jax.experimental.pallas import tpu as pltpu

F32 = jnp.float32
BF16 = jnp.bfloat16
I32 = jnp.int32

D_MODEL = 1024
ML_HEADS, ML_DQK, ML_DV, ML_CONV = 4, 64, 128, 4
MOBA_HEADS, MOBA_DH, MOBA_BLOCK, MOBA_TOPK = 4, 64, 256, 3
DSA_HEADS, DSA_DH, IDX_HEADS, IDX_DH, DSA_TOPK_MAX = 4, 64, 4, 64, 256
D_FF = 4 * D_MODEL
EPS = 1e-6

ML_QK_W = ML_HEADS * ML_DQK
ML_V_W = ML_HEADS * ML_DV
MOBA_W = MOBA_HEADS * MOBA_DH
DSA_W = DSA_HEADS * DSA_DH
IN_SPLITS = (ML_QK_W, ML_QK_W, ML_V_W, ML_V_W, ML_HEADS, ML_HEADS,
             MOBA_W, MOBA_W, MOBA_W,
             DSA_W, DSA_W, DSA_W, IDX_HEADS * IDX_DH, IDX_DH, IDX_HEADS)

LANES = 128
SUBLANES = 8
VMEM_LIMIT = 52 * 1024 * 1024

ML_CHUNK = 256
ATT_TQ = 256
ATT_STAGE = 8
PROJ_TM = 512
MLP_TM = 1024
MLP_TF = 512

NEG_INF = float("-inf")
BISECT_CAP = 300
BISECT_STEPS_PER_CHECK = 3
SOFTMAX_L_MIN = 1e-30
assert MOBA_DH == DSA_DH
ATT_QSCALE = MOBA_DH ** -0.5 * 1.4426950408889634

_NT = (((1,), (1,)), ((), ()))
_TN = (((0,), (0,)), ((), ()))


def _rms(x, g):
    return x * lax.rsqrt(jnp.mean(x * x, axis=-1, keepdims=True) + EPS) * g


def _log_sigmoid(x):
    return jnp.minimum(x, 0.0) - jnp.log1p(jnp.exp(-jnp.abs(x)))


def _params(*sem):
    return pltpu.CompilerParams(dimension_semantics=sem, vmem_limit_bytes=VMEM_LIMIT)


_N_QK, _N_V, _N_O, _N_MBK, _N_DSK, _N_IXK, _N_SM = 0, 512, 1024, 1536, 1792, 2048, 2176
_N_TOTAL = 2304
_T_MBQ, _T_MBV, _T_DSQ, _T_DSV, _T_IXQ, _T_SM = 0, 256, 512, 768, 1024, 1280
_T_TOTAL = 1296


def _in_proj_kernel(x_ref, g_ref, wn_ref, wt_ref,
                    qk_ref, v_ref, o_ref, sm_ref, mbk_ref, dsk_ref, ixk_ref,
                    mbqT_ref, mbvT_ref, dsqT_ref, dsvT_ref, ixqT_ref, smT_ref):
    h = _rms(x_ref[...], g_ref[...]).astype(BF16)

    def mm(a, n):
        return jnp.dot(h, wn_ref[:, a:a + n], preferred_element_type=F32)

    def mt(a, n):
        return lax.dot_general(wt_ref[a:a + n, :], h, _NT, preferred_element_type=F32)

    qk_ref[...] = mm(_N_QK, 512)
    v_ref[...] = mm(_N_V, 512).astype(BF16)
    o_ref[...] = mm(_N_O, 512)
    mbk_ref[...] = mm(_N_MBK, 256).astype(BF16)
    dsk_ref[...] = mm(_N_DSK, 256).astype(BF16)
    ixk_ref[...] = mm(_N_IXK, 128).astype(BF16)
    sm_ref[...] = mm(_N_SM, 128)
    mbqT_ref[0] = (mt(_T_MBQ, 256) * ATT_QSCALE).astype(BF16)
    mbvT_ref[0] = mt(_T_MBV, 256).astype(BF16)
    dsqT_ref[0] = (mt(_T_DSQ, 256) * ATT_QSCALE).astype(BF16)
    dsvT_ref[0] = mt(_T_DSV, 256).astype(BF16)
    ixqT_ref[0] = mt(_T_IXQ, 256).astype(BF16)
    smT_ref[0] = mt(_T_SM, 16)


def _prep_in_weights(w_in):
    pts, acc = [], 0
    for n in IN_SPLITS:
        pts.append((acc, acc + n))
        acc += n
    col = lambda i: w_in[:, pts[i][0]:pts[i][1]]
    (ml_q, ml_k, ml_v, ml_o, ml_i, ml_f, mb_q, mb_k, mb_v,
     ds_q, ds_k, ds_v, ix_q, ix_k, ix_w) = [col(i) for i in range(len(IN_SPLITS))]
    d = w_in.shape[0]
    small = jnp.concatenate([ml_i, ml_f, ix_w], axis=1)
    wn = jnp.concatenate([
        ml_q, ml_k, ml_v, ml_o, mb_k, ds_k,
        ix_k, jnp.zeros((d, 128 - IDX_DH), w_in.dtype),
        small, jnp.zeros((d, 128 - small.shape[1]), w_in.dtype)], axis=1)
    wt = jnp.concatenate([
        mb_q, mb_v, ds_q, ds_v, ix_q,
        small, jnp.zeros((d, 16 - small.shape[1]), w_in.dtype)], axis=1).T
    assert wn.shape == (d, _N_TOTAL) and wt.shape == (_T_TOTAL, d)
    return wn.astype(BF16), wt.astype(BF16)


def _in_proj(x2, g, wn, wt, B, S):
    T, D = x2.shape
    tm = min(PROJ_TM, S)
    nsb = S // tm
    row = lambda n: pl.BlockSpec((tm, n), lambda i: (i, 0))
    tr = lambda n: pl.BlockSpec((1, n, tm), lambda i: (i // nsb, 0, i % nsb))
    full = lambda a: pl.BlockSpec(a.shape, lambda i: (0,) * a.ndim)
    out_shape = (
        jax.ShapeDtypeStruct((T, 512), F32),
        jax.ShapeDtypeStruct((T, 512), BF16),
        jax.ShapeDtypeStruct((T, 512), F32),
        jax.ShapeDtypeStruct((T, 128), F32),
        jax.ShapeDtypeStruct((T, 256), BF16),
        jax.ShapeDtypeStruct((T, 256), BF16),
        jax.ShapeDtypeStruct((T, 128), BF16),
        jax.ShapeDtypeStruct((B, 256, S), BF16),
        jax.ShapeDtypeStruct((B, 256, S), BF16),
        jax.ShapeDtypeStruct((B, 256, S), BF16),
        jax.ShapeDtypeStruct((B, 256, S), BF16),
        jax.ShapeDtypeStruct((B, 256, S), BF16),
        jax.ShapeDtypeStruct((B, 16, S), F32),
    )
    out_specs = (row(512), row(512), row(512), row(128), row(256), row(256), row(128),
                 tr(256), tr(256), tr(256), tr(256), tr(256), tr(16))
    return pl.pallas_call(
        _in_proj_kernel,
        grid=(T // tm,),
        in_specs=[row(D), full(g), full(wn), full(wt)],
        out_specs=out_specs,
        out_shape=out_shape,
        compiler_params=_params("parallel"),
    )(x2, g, wn, wt)


def _mlstm_kernel(qk_ref, v_ref, o_ref, sm_ref, smT_ref, conv_ref, brow_ref, bcol_ref, hn_ref,
                  y_ref, xp_scr, st_scr, m_scr, *, L):
    c = pl.program_id(1)
    halo = SUBLANES

    @pl.when(c == 0)
    def _():
        xp_scr[0:halo, :] = jnp.zeros((halo, 2 * ML_QK_W), F32)
        st_scr[...] = jnp.zeros_like(st_scr)
        m_scr[...] = jnp.zeros_like(m_scr)

    cur = qk_ref[0]
    xp_scr[halo:halo + L, :] = cur
    base = halo - (ML_CONV - 1)
    acc = conv_ref[0:1, :] * xp_scr[base:base + L, :]
    for j in range(1, ML_CONV):
        acc = acc + conv_ref[j:j + 1, :] * xp_scr[base + j:base + j + L, :]
    xp_scr[0:halo, :] = cur[L - halo:L, :]
    qk = acc * jax.nn.sigmoid(acc)
    q_all = (qk[:, :ML_QK_W] * (ML_DQK ** -0.5)).astype(BF16)
    k_all = qk[:, ML_QK_W:]

    gcol = sm_ref[0] + brow_ref[...]
    grow = smT_ref[0][0:SUBLANES, :] + bcol_ref[...]
    ri = lax.broadcasted_iota(I32, (L, L), 0)
    ci = lax.broadcasted_iota(I32, (L, L), 1)
    causal = ci <= ri
    tril = causal.astype(F32)
    triu = (ri <= ci).astype(F32)
    hp = lax.Precision.HIGHEST
    b_col = jnp.dot(tril, _log_sigmoid(gcol), precision=hp, preferred_element_type=F32)
    b_row = jnp.dot(_log_sigmoid(grow), triu, precision=hp, preferred_element_type=F32)

    v_all = v_ref[0]
    o_all = o_ref[0]
    lane = lax.broadcasted_iota(I32, (L, LANES), 1)
    ones_col = jnp.where(lane == 0, 1.0, 0.0).astype(BF16)
    for h in range(ML_HEADS):
        qh = q_all[:, h * ML_DQK:(h + 1) * ML_DQK]
        kh = k_all[:, h * ML_DQK:(h + 1) * ML_DQK]
        vh = v_all[:, h * ML_DV:(h + 1) * ML_DV]
        bj = b_col[:, ML_HEADS + h:ML_HEADS + h + 1]
        icol = gcol[:, h:h + 1]
        r_row = grow[h:h + 1, :] - b_row[ML_HEADS + h:ML_HEADS + h + 1, :]
        dlog = jnp.where(causal, bj + r_row, NEG_INF)
        m0 = m_scr[h:h + 1, 0:1]
        inter = bj + m0
        m_t = jnp.maximum(inter, jnp.max(dlog, axis=-1, keepdims=True))
        w_inter = jnp.exp(inter - m_t)
        qk_s = lax.dot_general(qh, kh.astype(BF16), _NT, preferred_element_type=F32)
        w_intra = jnp.exp(dlog - m_t) * qk_s
        st = st_scr[h]
        cq = jnp.dot(qh, st.astype(BF16), preferred_element_type=F32)
        num = w_inter * cq[:, :ML_DV] + jnp.dot(w_intra.astype(BF16), vh, preferred_element_type=F32)
        den = w_inter * cq[:, ML_DV:ML_DV + 1] + jnp.sum(w_intra, axis=-1, keepdims=True)
        hh = num / jnp.maximum(jnp.abs(den), jnp.exp(-m_t))
        hh = hh * lax.rsqrt(jnp.mean(hh * hh, axis=-1, keepdims=True) + EPS)
        hh = hh * hn_ref[:, h * ML_DV:(h + 1) * ML_DV]
        y_ref[0, :, h * ML_DV:(h + 1) * ML_DV] = (
            jax.nn.sigmoid(o_all[:, h * ML_DV:(h + 1) * ML_DV]) * hh).astype(y_ref.dtype)

        b_last = bj[L - 1:L, :]
        a = b_last - bj + icol
        m_loc = jnp.max(a, axis=0, keepdims=True)
        wa = jnp.exp(a - m_loc)
        kw = (kh * wa).astype(BF16)
        vaug = jnp.concatenate([vh, ones_col], axis=1)
        c_loc = lax.dot_general(kw, vaug, _TN, preferred_element_type=F32)
        m_new = jnp.maximum(b_last + m0, m_loc)
        s_old = jnp.exp(b_last + m0 - m_new)
        s_loc = jnp.exp(m_loc - m_new)
        st_scr[h] = s_old * st + s_loc * c_loc
        m_scr[h:h + 1, :] = jnp.broadcast_to(m_new, (1, LANES))


def _mlstm(qk, v, o, sm, smT, conv_w, i_bias, f_bias, head_norm, B, S):
    L = min(ML_CHUNK, S)
    nc = S // L
    qk = qk.reshape(B, S, 2 * ML_QK_W)
    v = v.reshape(B, S, ML_V_W)
    o = o.reshape(B, S, ML_V_W)
    sm = sm.reshape(B, S, LANES)
    bias = jnp.concatenate([i_bias, f_bias]).astype(F32)
    brow = jnp.zeros((1, LANES), F32).at[0, :2 * ML_HEADS].set(bias)
    bcol = bias.reshape(2 * ML_HEADS, 1)
    hn = head_norm.reshape(1, ML_V_W).astype(F32)
    blk = lambda n: pl.BlockSpec((1, L, n), lambda b, c: (b, c, 0))
    full = lambda a: pl.BlockSpec(a.shape, lambda b, c: (0,) * a.ndim)
    return pl.pallas_call(
        functools.partial(_mlstm_kernel, L=L),
        grid=(B, nc),
        in_specs=[blk(2 * ML_QK_W), blk(ML_V_W), blk(ML_V_W), blk(LANES),
                  pl.BlockSpec((1, 16, L), lambda b, c: (b, 0, c)),
                  full(conv_w), full(brow), full(bcol), full(hn)],
        out_specs=blk(ML_V_W),
        out_shape=jax.ShapeDtypeStruct((B, S, ML_V_W), BF16),
        scratch_shapes=[pltpu.VMEM((L + SUBLANES, 2 * ML_QK_W), F32),
                        pltpu.VMEM((ML_HEADS, ML_DQK, ML_DV + LANES), F32),
                        pltpu.VMEM((SUBLANES, LANES), F32)],
        compiler_params=_params("parallel", "arbitrary"),
    )(qk, v, o, sm, smT, conv_w.astype(F32), brow, bcol, hn)


def _block_diag_T(qT, heads, dh):
    rid = lax.broadcasted_iota(I32, qT.shape, 0)
    zero = jnp.zeros_like(qT)
    return jnp.concatenate(
        [jnp.where((rid >= h * dh) & (rid < (h + 1) * dh), qT, zero) for h in range(heads)], axis=1)


def _scores(k_chunk, qs, h, TQ):
    return jnp.dot(k_chunk, qs[:, h * TQ:(h + 1) * TQ], preferred_element_type=F32)


def _accumulate(h, p, vT_chunk, l, acc_scr, dh):
    rows = slice(h * dh, (h + 1) * dh)
    acc_scr[rows, :] += jnp.dot(vT_chunk[rows, :], p.astype(BF16), preferred_element_type=F32)
    return l + jnp.sum(p, axis=0, keepdims=True)


def _finish_attention(l, acc_scr, y_ref, heads, dh):
    outs = [acc_scr[h * dh:(h + 1) * dh, :] / l[h] for h in range(heads)]
    y_ref[0] = jnp.concatenate(outs, axis=0).T.astype(y_ref.dtype)


def _max_key_norm2(kb, heads, dh, TQ):
    ch = lax.broadcasted_iota(I32, (heads * dh, LANES), 0) // dh
    hd = lax.broadcasted_iota(I32, (heads * dh, LANES), 1)
    n2 = jnp.dot(kb * kb, (ch == hd).astype(F32), precision=lax.Precision.HIGHEST, preferred_element_type=F32)
    n2 = jnp.max(n2, axis=0, keepdims=True)
    return jnp.concatenate([jnp.broadcast_to(n2[:, h:h + 1], (1, TQ)) for h in range(heads)], axis=1)


def _norm_bound(qTf, k2, heads, dh, TQ):
    out = []
    for h in range(heads):
        qh = qTf[h * dh:(h + 1) * dh, :]
        q2 = jnp.sum(qh * qh, axis=0, keepdims=True)
        out.append(jnp.sqrt(q2 * k2[:, h * TQ:(h + 1) * TQ]) * (1.0 + 2.0 ** -8))
    return out


def _underflowed(l):
    small = l[0]
    for lh in l[1:]:
        small = jnp.minimum(small, lh)
    return jnp.logical_not(jnp.min(small) >= SOFTMAX_L_MIN)


def _moba_kernel(qT_ref, k_ref, vT_ref, y_ref, km_scr, kn_scr, acc_scr, s_scr, *, NB, NBP, n_sel):
    H, DH, BS = MOBA_HEADS, MOBA_DH, MOBA_BLOCK
    TQ = BS
    qi = pl.program_id(1)

    @pl.when(qi == 0)
    def _():
        km_scr[...] = jnp.zeros_like(km_scr)
        kn_scr[...] = jnp.zeros_like(kn_scr)

        def body(j, carry):
            kb = k_ref[0, pl.ds(pl.multiple_of(j * BS, BS), BS), :].astype(F32)
            km_scr[pl.ds(j, 1), :] = jnp.mean(kb, axis=0, keepdims=True)
            kn_scr[pl.ds(j, 1), :] = _max_key_norm2(kb, H, DH, TQ)
            return carry
        lax.fori_loop(0, NB, body, 0)

    qT = qT_ref[0]
    qTf = qT.astype(F32)
    km = km_scr[...]
    gate = jnp.concatenate(
        [jnp.dot(km[:, h * DH:(h + 1) * DH], qTf[h * DH:(h + 1) * DH, :], precision=lax.Precision.HIGHEST,
                 preferred_element_type=F32) for h in range(H)], axis=1) * (1.0 / ATT_QSCALE)
    blk = lax.broadcasted_iota(I32, gate.shape, 0)
    g = jnp.where(blk < qi, gate, NEG_INF)
    sels = []
    for _ in range(n_sel):
        mx = jnp.max(g, axis=0, keepdims=True)
        isel = jnp.min(jnp.where(g == mx, blk, NBP), axis=0, keepdims=True)
        sels.append(jnp.where(mx > NEG_INF, isel, -1))
        g = jnp.where(blk == isel, NEG_INF, g)

    qs = _block_diag_T(qT, H, DH)

    def hit_row(j):
        hit = sels[0] == j
        for s in sels[1:]:
            hit = hit | (s == j)
        return hit

    npair = (qi + 1) // 2
    blk_start = lambda j: pl.multiple_of(j * BS, BS)
    blk_scores = lambda j: jnp.dot(k_ref[0, pl.ds(blk_start(j), BS), :], qs, preferred_element_type=F32)
    hcols = lambda row, h: row[:, h * TQ:(h + 1) * TQ]
    causal = lax.broadcasted_iota(I32, (BS, TQ), 0) <= lax.broadcasted_iota(I32, (BS, TQ), 1)

    def diag_scores():
        s_all = blk_scores(qi)
        return [jnp.where(causal, hcols(s_all, h), NEG_INF) for h in range(H)]

    def max_sweep():
        def max_pair(g, m):
            for u in range(2):
                j = 2 * g + u
                m = jnp.maximum(m, jnp.where(hit_row(j), jnp.max(blk_scores(j), axis=0, keepdims=True), NEG_INF))
            return m
        m_all = lax.fori_loop(0, npair, max_pair, jnp.full((1, H * TQ), NEG_INF, F32))
        s_diag = diag_scores()
        return [jnp.maximum(hcols(m_all, h), jnp.max(s_diag[h], axis=0, keepdims=True)) for h in range(H)]

    def sum_sweep(m):
        acc_scr[...] = jnp.zeros_like(acc_scr)

        def sum_group(width):
            def body(g, l):
                for u in range(width):
                    s_scr[u] = blk_scores(width * g + u)
                for u in range(width):
                    j = width * g + u
                    hit = hit_row(j)
                    vT_blk = vT_ref[0, :, pl.ds(blk_start(j), BS)]
                    l = tuple(_accumulate(h, jnp.exp2(s_scr[u, :, h * TQ:(h + 1) * TQ]
                                                      - jnp.where(hcols(hit, h), m[h], jnp.inf)),
                                          vT_blk, l[h], acc_scr, DH) for h in range(H))
                return l
            return body
        nwide = qi // ATT_STAGE
        l = lax.fori_loop(0, nwide, sum_group(ATT_STAGE), (jnp.zeros((1, TQ), F32),) * H)
        l = lax.fori_loop((ATT_STAGE // 2) * nwide, npair, sum_group(2), l)
        s_diag = diag_scores()
        vT_diag = vT_ref[0, :, pl.ds(blk_start(qi), BS)]
        return [_accumulate(h, jnp.exp2(s_diag[h] - m[h]), vT_diag, l[h], acc_scr, DH) for h in range(H)]

    blk_n = lax.broadcasted_iota(I32, kn_scr.shape, 0)
    k2 = jnp.max(jnp.where(blk_n <= qi, kn_scr[...], 0.0), axis=0, keepdims=True)
    l = sum_sweep(_norm_bound(qTf, k2, H, DH, TQ))
    _finish_attention(l, acc_scr, y_ref, H, DH)

    @pl.when(_underflowed(l))
    def _():
        _finish_attention(sum_sweep(max_sweep()), acc_scr, y_ref, H, DH)


def _moba(qT, k, vT, B, S):
    BS = MOBA_BLOCK
    assert S % BS == 0
    NB = S // BS
    NBP = -(-NB // SUBLANES) * SUBLANES
    n_sel = max(min(MOBA_TOPK, NB - 1), 1)
    k = k.reshape(B, S, MOBA_W)
    return pl.pallas_call(
        functools.partial(_moba_kernel, NB=NB, NBP=NBP, n_sel=n_sel),
        grid=(B, NB),
        in_specs=[pl.BlockSpec((1, MOBA_W, BS), lambda b, i: (b, 0, i)),
                  pl.BlockSpec((1, S, MOBA_W), lambda b, i: (b, 0, 0)),
                  pl.BlockSpec((1, MOBA_W, S), lambda b, i: (b, 0, 0))],
        out_specs=pl.BlockSpec((1, BS, MOBA_W), lambda b, i: (b, i, 0)),
        out_shape=jax.ShapeDtypeStruct((B, S, MOBA_W), BF16),
        scratch_shapes=[pltpu.VMEM((NBP, MOBA_W), F32),
                        pltpu.VMEM((NBP, MOBA_HEADS * BS), F32),
                        pltpu.VMEM((MOBA_W, BS), F32),
                        pltpu.VMEM((ATT_STAGE, BS, MOBA_HEADS * BS), F32)],
        compiler_params=_params("parallel", "arbitrary"),
    )(qT, k, vT)


def _dsa_kernel(qT_ref, k_ref, vT_ref, iqT_ref, ik_ref, smT_ref, y_ref,
                e_scr, gmax_scr, acc_scr, thr_scr, kn_scr, s_scr, *, TQ, topk, spow, n_chunks):
    H, DH = DSA_HEADS, DSA_DH
    KC = TQ
    qi = pl.program_id(1)
    nch = qi + 1
    npair = (nch + 1) // 2
    chunk_start = lambda c: pl.multiple_of(c * KC, KC)

    @pl.when(qi == 0)
    def _():
        kn_scr[...] = jnp.zeros_like(kn_scr)

        def body(c, carry):
            kn_scr[pl.ds(c, 1), :] = _max_key_norm2(k_ref[0, pl.ds(chunk_start(c), KC), :].astype(F32), H, DH, TQ)
            return carry
        lax.fori_loop(0, n_chunks, body, 0)
    sum8 = lambda b: jnp.sum(b.astype(I32).reshape(KC // SUBLANES, SUBLANES, TQ), axis=0)
    any_row = lambda r: jnp.max(r.astype(I32)) > 0

    iqT = iqT_ref[0]
    iq_cat = jnp.concatenate([iqT[h * IDX_DH:(h + 1) * IDX_DH, :] for h in range(IDX_HEADS)], axis=1)
    iw = smT_ref[0][2 * ML_HEADS:2 * ML_HEADS + IDX_HEADS, :] * (IDX_HEADS ** -0.5 * IDX_DH ** -0.5)
    t_pos = qi * TQ + lax.broadcasted_iota(I32, (KC, TQ), 1)
    s_off = lax.broadcasted_iota(I32, (KC, TQ), 0)
    gmax_scr[...] = jnp.full((KC, TQ), NEG_INF, F32)

    def score_group(width):
        def body(g, carry):
            n_pos, n_nonneg = carry
            for u in range(width):
                ikc = ik_ref[0, pl.ds(chunk_start(width * g + u), KC), :][:, :IDX_DH]
                s_scr[u] = jnp.dot(ikc, iq_cat, preferred_element_type=F32)
            for u in range(width):
                c = width * g + u
                sc = iw[0:1, :] * jnp.maximum(s_scr[u, :, 0:TQ], 0.0)
                for h in range(1, IDX_HEADS):
                    sc = sc + iw[h:h + 1, :] * jnp.maximum(s_scr[u, :, h * TQ:(h + 1) * TQ], 0.0)
                sc = jnp.where(c * KC + s_off <= t_pos, sc, NEG_INF)
                e_scr[pl.ds(chunk_start(c), KC), :] = sc
                gmax_scr[...] = jnp.maximum(gmax_scr[...], sc)
                n_pos = n_pos + sum8(sc > 0.0)
                n_nonneg = n_nonneg + sum8(sc >= 0.0)
            return n_pos, n_nonneg
        return body
    zero8 = jnp.zeros((SUBLANES, TQ), I32)
    counts = lax.fori_loop(0, npair // 2, score_group(4), (zero8, zero8))
    n_pos, n_nonneg = lax.fori_loop(2 * (npair // 2), npair, score_group(2), counts)
    n_pos = jnp.sum(n_pos, axis=0, keepdims=True)
    n_nonneg = jnp.sum(n_nonneg, axis=0, keepdims=True)

    def count_ge(cand):
        def body(g, acc):
            a, b = acc
            a = a + sum8(e_scr[pl.ds(chunk_start(2 * g), KC), :] >= cand)
            b = b + sum8(e_scr[pl.ds(chunk_start(2 * g + 1), KC), :] >= cand)
            return a, b
        a, b = lax.fori_loop(0, npair, body, (zero8, zero8))
        return jnp.sum(a + b, axis=0, keepdims=True)

    def rewrite(rows, thr):
        def body(c, carry):
            x = e_scr[pl.ds(chunk_start(c), KC), :]
            rank = (2 * spow - (c * KC + s_off)).astype(F32)
            new = jnp.where(x > thr, jnp.inf, jnp.where(x == thr, rank, NEG_INF))
            e_scr[pl.ds(chunk_start(c), KC), :] = jnp.where(rows, new, x)
            return carry
        lax.fori_loop(0, nch, body, 0)

    def bisect(lo, hi, act):
        def cond(c):
            return jnp.logical_and(c[0] < BISECT_CAP, c[5] > 0)

        def body(c):
            it, lo, hi, act, tie, _ = c
            for _ in range(BISECT_STEPS_PER_CHECK):
                mid = lo + (hi - lo) * 0.5
                inside = (mid > lo) & (mid < hi)
                cnt = count_ge(mid)
                live = act > 0.0
                run = live & inside
                up = cnt >= topk
                lo = jnp.where(run & up, mid, lo)
                hi = jnp.where(run & jnp.logical_not(up), mid, hi)
                collapsed = live & jnp.logical_not(inside)
                tie = jnp.where(collapsed, 1.0, tie)
                act = jnp.where(collapsed | (run & (cnt == topk)), 0.0, act)
            return it + BISECT_STEPS_PER_CHECK, lo, hi, act, tie, jnp.max(act)
        out = lax.while_loop(cond, body, (jnp.int32(0), lo, hi, act, jnp.zeros_like(act), jnp.max(act)))
        return out[1], out[4]

    fmax = float(jnp.finfo(F32).max)
    gm = gmax_scr[...]
    g_lo = jnp.min(gm, axis=0, keepdims=True)
    g_hi = jnp.max(gm, axis=0, keepdims=True)
    n_vis = qi * TQ + lax.broadcasted_iota(I32, (1, TQ), 1) + 1
    few = n_vis <= topk
    zero_tie = (n_pos < topk) & (n_nonneg >= topk) & jnp.logical_not(few)

    @pl.when(any_row(zero_tie))
    def _():
        rewrite(zero_tie, jnp.zeros((1, TQ), F32))

    rank_lo, rank_hi = float(spow), float(2 * spow + 1)
    lo0 = jnp.where(zero_tie, rank_lo, jnp.maximum(g_lo, -fmax))
    hi0 = jnp.where(zero_tie, rank_hi, g_hi + jnp.abs(g_hi) * (2.0 ** -10) + 1e-30)
    lo, tie = bisect(lo0, hi0, jnp.where(few, 0.0, 1.0))
    thr_scr[0:1, :] = jnp.where(few, -fmax, lo)

    @pl.when(jnp.max(tie) > 0.0)
    def _():
        rows = tie > 0.0
        rewrite(rows, lo)
        lo2, _ = bisect(jnp.full((1, TQ), rank_lo, F32), jnp.full((1, TQ), rank_hi, F32), tie)
        thr_scr[0:1, :] = jnp.where(rows, lo2, thr_scr[0:1, :])

    thr = thr_scr[0:1, :]
    qT = qT_ref[0]
    qs = _block_diag_T(qT, H, DH)
    hcols = lambda row, h: row[:, h * TQ:(h + 1) * TQ]

    def chunk_scores(c):
        start = chunk_start(c)
        sel = e_scr[pl.ds(start, KC), :] >= thr
        s_all = jnp.dot(k_ref[0, pl.ds(start, KC), :], qs, preferred_element_type=F32)
        return sel, s_all

    def max_sweep():
        def max_pair(g, m):
            for u in range(2):
                sel, s_all = chunk_scores(2 * g + u)
                m = tuple(jnp.maximum(m[h], jnp.max(jnp.where(sel, hcols(s_all, h), NEG_INF), axis=0, keepdims=True))
                          for h in range(H))
            return m
        m = lax.fori_loop(0, npair, max_pair, (jnp.full((1, TQ), NEG_INF, F32),) * H)
        return [jnp.where(mh == NEG_INF, 0.0, mh) for mh in m]

    def sum_sweep(m):
        acc_scr[...] = jnp.zeros_like(acc_scr)

        def sum_group(width):
            def body(g, l):
                for u in range(width):
                    s_scr[u] = jnp.dot(k_ref[0, pl.ds(chunk_start(width * g + u), KC), :], qs,
                                       preferred_element_type=F32)
                for u in range(width):
                    start = chunk_start(width * g + u)
                    sel = e_scr[pl.ds(start, KC), :] >= thr
                    vT_chunk = vT_ref[0, :, pl.ds(start, KC)]
                    l = tuple(_accumulate(h, jnp.exp2(s_scr[u, :, h * TQ:(h + 1) * TQ] + jnp.where(sel, -m[h], NEG_INF)),
                                          vT_chunk, l[h], acc_scr, DH) for h in range(H))
                return l
            return body
        nquad = npair // 2
        l = lax.fori_loop(0, nquad, sum_group(4), (jnp.zeros((1, TQ), F32),) * H)
        return lax.fori_loop(2 * nquad, npair, sum_group(2), l)

    chunk_n = lax.broadcasted_iota(I32, kn_scr.shape, 0)
    k2 = jnp.max(jnp.where(chunk_n < 2 * npair, kn_scr[...], 0.0), axis=0, keepdims=True)
    l = sum_sweep(_norm_bound(qT.astype(F32), k2, H, DH, TQ))
    _finish_attention(l, acc_scr, y_ref, H, DH)

    @pl.when(_underflowed(l))
    def _():
        _finish_attention(sum_sweep(max_sweep()), acc_scr, y_ref, H, DH)


def _dsa(qT, k, vT, iqT, ik, smT, B, S):
    TQ = min(ATT_TQ, S)
    topk = min(DSA_TOPK_MAX, S // 4)
    assert S % (2 * TQ) == 0 and topk <= TQ
    spow = 1 << max((S - 1).bit_length(), 1)
    k = k.reshape(B, S, DSA_W)
    ik = ik.reshape(B, S, LANES)
    qblk = lambda n: pl.BlockSpec((1, n, TQ), lambda b, i: (b, 0, i))
    return pl.pallas_call(
        functools.partial(_dsa_kernel, TQ=TQ, topk=topk, spow=spow, n_chunks=S // TQ),
        grid=(B, S // TQ),
        in_specs=[qblk(DSA_W),
                  pl.BlockSpec((1, S, DSA_W), lambda b, i: (b, 0, 0)),
                  pl.BlockSpec((1, DSA_W, S), lambda b, i: (b, 0, 0)),
                  qblk(IDX_HEADS * IDX_DH),
                  pl.BlockSpec((1, S, LANES), lambda b, i: (b, 0, 0)),
                  qblk(16)],
        out_specs=pl.BlockSpec((1, TQ, DSA_W), lambda b, i: (b, i, 0)),
        out_shape=jax.ShapeDtypeStruct((B, S, DSA_W), BF16),
        scratch_shapes=[pltpu.VMEM((S, TQ), F32),
                        pltpu.VMEM((TQ, TQ), F32),
                        pltpu.VMEM((DSA_W, TQ), F32),
                        pltpu.VMEM((SUBLANES, TQ), F32),
                        pltpu.VMEM((-(-(S // TQ) // SUBLANES) * SUBLANES, DSA_HEADS * TQ), F32),
                        pltpu.VMEM((4, TQ, DSA_HEADS * TQ), F32)],
        compiler_params=_params("parallel", "arbitrary"),
    )(qT, k, vT, iqT, ik, smT)


def _out_mlp_kernel(yml_ref, ymb_ref, yds_ref, x_ref, wout_ref, gpost_ref, gpre_ref,
                    w1_ref, w2_ref, g2_ref, out_ref, x1_scr, h_scr, acc_scr):
    f = pl.program_id(1)

    @pl.when(f == 0)
    def _():
        mix = jnp.dot(yml_ref[...], wout_ref[0:ML_V_W, :], preferred_element_type=F32)
        mix = mix + jnp.dot(ymb_ref[...], wout_ref[ML_V_W:ML_V_W + MOBA_W, :], preferred_element_type=F32)
        mix = mix + jnp.dot(yds_ref[...], wout_ref[ML_V_W + MOBA_W:, :], preferred_element_type=F32)
        x1 = x_ref[...] + _rms(mix, gpost_ref[...])
        x1_scr[...] = x1
        h_scr[...] = _rms(x1, gpre_ref[...]).astype(BF16)
        acc_scr[...] = jnp.zeros_like(acc_scr)

    u = jnp.maximum(jnp.dot(h_scr[...], w1_ref[...], preferred_element_type=F32), 0.0)
    acc_scr[...] += jnp.dot((u * u).astype(BF16), w2_ref[...], preferred_element_type=F32)

    @pl.when(f == pl.num_programs(1) - 1)
    def _():
        out_ref[...] = x1_scr[...] + _rms(acc_scr[...], g2_ref[...])


def _out_mlp(yml, ymb, yds, x2, w_out, g_post, g_pre, w1, w2, g2):
    T, D = x2.shape
    tm = min(MLP_TM, T)
    tf = min(MLP_TF, D_FF)
    row = lambda n: pl.BlockSpec((tm, n), lambda i, f: (i, 0))
    full = lambda a: pl.BlockSpec(a.shape, lambda i, f: (0,) * a.ndim)
    return pl.pallas_call(
        _out_mlp_kernel,
        grid=(T // tm, D_FF // tf),
        in_specs=[row(ML_V_W), row(MOBA_W), row(DSA_W), row(D), full(w_out), full(g_post), full(g_pre),
                  pl.BlockSpec((D, tf), lambda i, f: (0, f)),
                  pl.BlockSpec((tf, D), lambda i, f: (f, 0)),
                  full(g2)],
        out_specs=row(D),
        out_shape=jax.ShapeDtypeStruct((T, D), F32),
        scratch_shapes=[pltpu.VMEM((tm, D), F32), pltpu.VMEM((tm, D), BF16), pltpu.VMEM((tm, D), F32)],
        compiler_params=_params("parallel", "arbitrary"),
    )(yml, ymb, yds, x2, w_out, g_post, g_pre, w1, w2, g2)


def kernel(x, norm_mix_pre, w_in, ml_conv, ml_i_bias, ml_f_bias, ml_head_norm, w_out, norm_mix_post,
           norm_mlp_pre, w_ff1, w_ff2, norm_mlp_post):
    B, S, D = x.shape
    depth = w_in.shape[0]
    x2 = x.reshape(B * S, D)
    gain = lambda g: g.reshape(1, D).astype(F32)
    for l in range(depth):
        wn, wt = _prep_in_weights(w_in[l])
        (qk, v, o, sm, mbk, dsk, ixk, mbqT, mbvT, dsqT, dsvT, ixqT, smT) = _in_proj(
            x2, gain(norm_mix_pre[l]), wn, wt, B, S)
        y_ml = _mlstm(qk, v, o, sm, smT, ml_conv[l], ml_i_bias[l], ml_f_bias[l], ml_head_norm[l], B, S)
        y_mb = _moba(mbqT, mbk, mbvT, B, S)
        y_ds = _dsa(dsqT, dsk, dsvT, ixqT, ixk, smT, B, S)
        x2 = _out_mlp(y_ml.reshape(B * S, ML_V_W), y_mb.reshape(B * S, MOBA_W), y_ds.reshape(B * S, DSA_W),
                      x2, w_out[l].astype(BF16), gain(norm_mix_post[l]), gain(norm_mlp_pre[l]),
                      w_ff1[l].astype(BF16), w_ff2[l].astype(BF16), gain(norm_mlp_post[l]))
    return x2.reshape(B, S, D)
```

```python
import functools

import jax
import jax.numpy as jnp
from jax import lax
from jax.experimental import pallas as pl
from jax.experimental.pallas import tpu as pltpu

F32 = jnp.float32
BF16 = jnp.bfloat16
I32 = jnp.int32

D_MODEL = 1024
ML_HEADS, ML_DQK, ML_DV, ML_CONV = 4, 64, 128, 4
MOBA_HEADS, MOBA_DH, MOBA_BLOCK, MOBA_TOPK = 4, 64, 256, 3
DSA_HEADS, DSA_DH, IDX_HEADS, IDX_DH, DSA_TOPK_MAX = 4, 64, 4, 64, 256
D_FF = 4 * D_MODEL
EPS = 1e-6

ML_QK_W = ML_HEADS * ML_DQK
ML_V_W = ML_HEADS * ML_DV
MOBA_W = MOBA_HEADS * MOBA_DH
DSA_W = DSA_HEADS * DSA_DH
IN_SPLITS = (ML_QK_W, ML_QK_W, ML_V_W, ML_V_W, ML_HEADS, ML_HEADS,
             MOBA_W, MOBA_W, MOBA_W,
             DSA_W, DSA_W, DSA_W, IDX_HEADS * IDX_DH, IDX_DH, IDX_HEADS)

LANES = 128
SUBLANES = 8
VMEM_LIMIT = 52 * 1024 * 1024

ML_CHUNK = 256
ATT_TQ = 256
ATT_STAGE = 8
PROJ_TM = 512
MLP_TM = 1024
MLP_TF = 512

NEG_INF = float("-inf")
BISECT_CAP = 300
BISECT_STEPS_PER_CHECK = 3
SOFTMAX_L_MIN = 1e-30
assert MOBA_DH == DSA_DH
ATT_QSCALE = MOBA_DH ** -0.5 * 1.4426950408889634

_NT = (((1,), (1,)), ((), ()))
_TN = (((0,), (0,)), ((), ()))


def _rms(x, g):
    return x * lax.rsqrt(jnp.mean(x * x, axis=-1, keepdims=True) + EPS) * g


def _log_sigmoid(x):
    return jnp.minimum(x, 0.0) - jnp.log1p(jnp.exp(-jnp.abs(x)))


def _params(*sem):
    return pltpu.CompilerParams(dimension_semantics=sem, vmem_limit_bytes=VMEM_LIMIT)


_N_QK, _N_MBK, _N_DSK, _N_IXK = 0, 512, 768, 1024
_N_TOTAL = 1152
_T_MBQ, _T_MBV, _T_DSQ, _T_DSV, _T_IXQ, _T_MLV, _T_MLO, _T_SM = 0, 256, 512, 768, 1024, 1280, 1792, 2304
_T_TOTAL = 2320


def _in_proj_kernel(x_ref, g_ref, wn_ref, wt_ref,
                    qk_ref, mbk_ref, dsk_ref, ixk_ref,
                    mbqT_ref, mbvT_ref, dsqT_ref, dsvT_ref, ixqT_ref, mlvT_ref, mloT_ref, smT_ref):
    h = _rms(x_ref[...], g_ref[...]).astype(BF16)

    def mm(a, n):
        return jnp.dot(h, wn_ref[:, a:a + n], preferred_element_type=F32)

    def mt(a, n):
        return lax.dot_general(wt_ref[a:a + n, :], h, _NT, preferred_element_type=F32)

    qk_ref[...] = mm(_N_QK, 512)
    mbk_ref[...] = mm(_N_MBK, 256).astype(BF16)
    dsk_ref[...] = mm(_N_DSK, 256).astype(BF16)
    ixk_ref[...] = mm(_N_IXK, 128).astype(BF16)
    mbqT_ref[0] = (mt(_T_MBQ, 256) * ATT_QSCALE).astype(BF16)
    mbvT_ref[0] = mt(_T_MBV, 256).astype(BF16)
    dsqT_ref[0] = (mt(_T_DSQ, 256) * ATT_QSCALE).astype(BF16)
    dsvT_ref[0] = mt(_T_DSV, 256).astype(BF16)
    ixqT_ref[0] = mt(_T_IXQ, 256).astype(BF16)
    mlvT_ref[0] = mt(_T_MLV, 512).astype(BF16)
    mloT_ref[0] = mt(_T_MLO, 512)
    smT_ref[0] = mt(_T_SM, 16)


def _prep_in_weights(w_in):
    pts, acc = [], 0
    for n in IN_SPLITS:
        pts.append((acc, acc + n))
        acc += n
    col = lambda i: w_in[:, pts[i][0]:pts[i][1]]
    (ml_q, ml_k, ml_v, ml_o, ml_i, ml_f, mb_q, mb_k, mb_v,
     ds_q, ds_k, ds_v, ix_q, ix_k, ix_w) = [col(i) for i in range(len(IN_SPLITS))]
    d = w_in.shape[0]
    small = jnp.concatenate([ml_i, ml_f, ix_w], axis=1)
    wn = jnp.concatenate([
        ml_q, ml_k, mb_k, ds_k,
        ix_k, jnp.zeros((d, 128 - IDX_DH), w_in.dtype)], axis=1)
    wt = jnp.concatenate([
        mb_q, mb_v, ds_q, ds_v, ix_q, ml_v, ml_o,
        small, jnp.zeros((d, 16 - small.shape[1]), w_in.dtype)], axis=1).T
    assert wn.shape == (d, _N_TOTAL) and wt.shape == (_T_TOTAL, d)
    return wn.astype(BF16), wt.astype(BF16)


def _in_proj(x2, g, wn, wt, B, S):
    T, D = x2.shape
    tm = min(PROJ_TM, S)
    nsb = S // tm
    row = lambda n: pl.BlockSpec((tm, n), lambda i: (i, 0))
    tr = lambda n: pl.BlockSpec((1, n, tm), lambda i: (i // nsb, 0, i % nsb))
    full = lambda a: pl.BlockSpec(a.shape, lambda i: (0,) * a.ndim)
    out_shape = (
        jax.ShapeDtypeStruct((T, 512), F32),
        jax.ShapeDtypeStruct((T, 256), BF16),
        jax.ShapeDtypeStruct((T, 256), BF16),
        jax.ShapeDtypeStruct((T, 128), BF16),
        jax.ShapeDtypeStruct((B, 256, S), BF16),
        jax.ShapeDtypeStruct((B, 256, S), BF16),
        jax.ShapeDtypeStruct((B, 256, S), BF16),
        jax.ShapeDtypeStruct((B, 256, S), BF16),
        jax.ShapeDtypeStruct((B, 256, S), BF16),
        jax.ShapeDtypeStruct((B, 512, S), BF16),
        jax.ShapeDtypeStruct((B, 512, S), F32),
        jax.ShapeDtypeStruct((B, 16, S), F32),
    )
    out_specs = (row(512), row(256), row(256), row(128),
                 tr(256), tr(256), tr(256), tr(256), tr(256), tr(512), tr(512), tr(16))
    return pl.pallas_call(
        _in_proj_kernel,
        grid=(T // tm,),
        in_specs=[row(D), full(g), full(wn), full(wt)],
        out_specs=out_specs,
        out_shape=out_shape,
        compiler_params=_params("parallel"),
    )(x2, g, wn, wt)


def _dot3(a_f32, b_exact):
    b = b_exact.astype(BF16)
    hi = a_f32.astype(BF16)
    r1 = a_f32 - hi.astype(F32)
    mid = r1.astype(BF16)
    lo = (r1 - mid.astype(F32)).astype(BF16)
    return (jnp.dot(hi, b, preferred_element_type=F32) + jnp.dot(mid, b, preferred_element_type=F32)
            + jnp.dot(lo, b, preferred_element_type=F32))


def _mlstm_kernel(qk_ref, vT_ref, oT_ref, smT_ref, conv_ref, bcol_ref, hn_ref,
                  y_ref, xp_scr, st_scr, m_scr, *, L):
    c = pl.program_id(1)
    halo = SUBLANES
    DK, DV = ML_DQK, ML_DV

    @pl.when(c == 0)
    def _():
        xp_scr[0:halo, :] = jnp.zeros((halo, 2 * ML_QK_W), F32)
        st_scr[...] = jnp.zeros_like(st_scr)
        m_scr[...] = jnp.zeros_like(m_scr)

    cur = qk_ref[0]
    xp_scr[halo:halo + L, :] = cur
    base = halo - (ML_CONV - 1)
    acc = conv_ref[0:1, :] * xp_scr[base:base + L, :]
    for j in range(1, ML_CONV):
        acc = acc + conv_ref[j:j + 1, :] * xp_scr[base + j:base + j + L, :]
    xp_scr[0:halo, :] = cur[L - halo:L, :]
    qk = acc * jax.nn.sigmoid(acc)
    qT_all = (qk[:, :ML_QK_W] * (DK ** -0.5)).T.astype(BF16)
    k_all = qk[:, ML_QK_W:].astype(BF16)

    grow = smT_ref[0][0:SUBLANES, :] + bcol_ref[...]
    si = lax.broadcasted_iota(I32, (L, L), 0)
    ji = lax.broadcasted_iota(I32, (L, L), 1)
    causal = si <= ji
    b_row = _dot3(_log_sigmoid(grow), causal)

    for h in range(ML_HEADS):
        qT = qT_all[h * DK:(h + 1) * DK, :]
        kh = k_all[:, h * DK:(h + 1) * DK]
        vT = vT_ref[0, h * DV:(h + 1) * DV, :]
        b_j = b_row[ML_HEADS + h:ML_HEADS + h + 1, :]
        c_row = grow[h:h + 1, :] - b_j
        c_col = jnp.broadcast_to(c_row, (LANES, L)).T
        dlog = jnp.where(causal, b_j + jnp.concatenate([c_col] * (L // LANES), axis=1), NEG_INF)
        m0 = m_scr[h:h + 1, 0:1]
        inter = b_j + m0
        m_t = jnp.maximum(inter, jnp.max(dlog, axis=0, keepdims=True))
        w_inter = jnp.exp(inter - m_t)
        w_intra = jnp.exp(dlog - m_t) * jnp.dot(kh, qT, preferred_element_type=F32)
        st = st_scr[h]
        cq = jnp.dot(st.astype(BF16), qT, preferred_element_type=F32)
        num = w_inter * cq[:DV] + jnp.dot(vT, w_intra.astype(BF16), preferred_element_type=F32)
        den = w_inter * cq[DV:DV + 1] + jnp.sum(w_intra, axis=0, keepdims=True)
        hh = num / jnp.maximum(jnp.abs(den), jnp.exp(-m_t))
        hh = hh * lax.rsqrt(jnp.mean(hh * hh, axis=0, keepdims=True) + EPS)
        rows = slice(h * DV, (h + 1) * DV)
        y_ref[0, :, rows] = (jax.nn.sigmoid(oT_ref[0, rows, :]) * hh * hn_ref[rows, :]).T.astype(y_ref.dtype)

        b_last = b_j[:, L - 1:L]
        a = b_last + c_row
        m_loc = jnp.max(a, axis=1, keepdims=True)
        wa = jnp.exp(a - m_loc)
        lhs = jnp.concatenate([(vT.astype(F32) * wa).astype(BF16),
                               jnp.broadcast_to(wa, (SUBLANES, L)).astype(BF16)], axis=0)
        c_loc = jnp.dot(lhs, kh, preferred_element_type=F32)
        m_new = jnp.maximum(b_last + m0, m_loc)
        s_old = jnp.exp(b_last + m0 - m_new)
        s_loc = jnp.exp(m_loc - m_new)
        st_scr[h] = s_old * st + s_loc * c_loc
        m_scr[h:h + 1, :] = jnp.broadcast_to(m_new, (1, LANES))


def _mlstm(qk, vT, oT, smT, conv_w, i_bias, f_bias, head_norm, B, S):
    L = min(ML_CHUNK, S)
    assert L % LANES == 0
    nc = S // L
    qk = qk.reshape(B, S, 2 * ML_QK_W)
    bcol = jnp.concatenate([i_bias, f_bias]).astype(F32).reshape(2 * ML_HEADS, 1)
    hn = jnp.broadcast_to(head_norm.astype(F32)[:, None], (ML_V_W, L))
    blkT = lambda n: pl.BlockSpec((1, n, L), lambda b, c: (b, 0, c))
    full = lambda a: pl.BlockSpec(a.shape, lambda b, c: (0,) * a.ndim)
    return pl.pallas_call(
        functools.partial(_mlstm_kernel, L=L),
        grid=(B, nc),
        in_specs=[pl.BlockSpec((1, L, 2 * ML_QK_W), lambda b, c: (b, c, 0)),
                  blkT(ML_V_W), blkT(ML_V_W), blkT(16), full(conv_w), full(bcol), full(hn)],
        out_specs=pl.BlockSpec((1, L, ML_V_W), lambda b, c: (b, c, 0)),
        out_shape=jax.ShapeDtypeStruct((B, S, ML_V_W), BF16),
        scratch_shapes=[pltpu.VMEM((L + SUBLANES, 2 * ML_QK_W), F32),
                        pltpu.VMEM((ML_HEADS, ML_DV + SUBLANES, ML_DQK), F32),
                        pltpu.VMEM((SUBLANES, LANES), F32)],
        compiler_params=_params("parallel", "arbitrary"),
    )(qk, vT, oT, smT, conv_w.astype(F32), bcol, hn)


def _block_diag_T(qT, heads, dh):
    rid = lax.broadcasted_iota(I32, qT.shape, 0)
    zero = jnp.zeros_like(qT)
    return jnp.concatenate(
        [jnp.where((rid >= h * dh) & (rid < (h + 1) * dh), qT, zero) for h in range(heads)], axis=1)


def _scores(k_chunk, qs, h, TQ):
    return jnp.dot(k_chunk, qs[:, h * TQ:(h + 1) * TQ], preferred_element_type=F32)


def _accumulate(h, p, vT_chunk, l, acc_scr, dh):
    rows = slice(h * dh, (h + 1) * dh)
    acc_scr[rows, :] += jnp.dot(vT_chunk[rows, :], p.astype(BF16), preferred_element_type=F32)
    return l + jnp.sum(p, axis=0, keepdims=True)


def _finish_attention(l, acc_scr, y_ref, heads, dh):
    outs = [acc_scr[h * dh:(h + 1) * dh, :] / l[h] for h in range(heads)]
    y_ref[0] = jnp.concatenate(outs, axis=0).T.astype(y_ref.dtype)


def _max_key_norm2(kb, heads, dh, TQ):
    ch = lax.broadcasted_iota(I32, (heads * dh, LANES), 0) // dh
    hd = lax.broadcasted_iota(I32, (heads * dh, LANES), 1)
    n2 = jnp.dot(kb * kb, (ch == hd).astype(F32), precision=lax.Precision.HIGHEST, preferred_element_type=F32)
    n2 = jnp.max(n2, axis=0, keepdims=True)
    return jnp.concatenate([jnp.broadcast_to(n2[:, h:h + 1], (1, TQ)) for h in range(heads)], axis=1)


def _norm_bound(qTf, k2, heads, dh, TQ):
    out = []
    for h in range(heads):
        qh = qTf[h * dh:(h + 1) * dh, :]
        q2 = jnp.sum(qh * qh, axis=0, keepdims=True)
        out.append(jnp.sqrt(q2 * k2[:, h * TQ:(h + 1) * TQ]) * (1.0 + 2.0 ** -8))
    return out


def _underflowed(l):
    small = l[0]
    for lh in l[1:]:
        small = jnp.minimum(small, lh)
    return jnp.logical_not(jnp.min(small) >= SOFTMAX_L_MIN)


def _moba_kernel(qT_ref, k_ref, vT_ref, y_ref, km_scr, kn_scr, acc_scr, s_scr, *, NB, NBP, n_sel):
    H, DH, BS = MOBA_HEADS, MOBA_DH, MOBA_BLOCK
    TQ = BS
    qi = pl.program_id(1)

    @pl.when(qi == 0)
    def _():
        km_scr[...] = jnp.zeros_like(km_scr)
        kn_scr[...] = jnp.zeros_like(kn_scr)

        def body(j, carry):
            kb = k_ref[0, pl.ds(pl.multiple_of(j * BS, BS), BS), :].astype(F32)
            km_scr[pl.ds(j, 1), :] = jnp.mean(kb, axis=0, keepdims=True)
            kn_scr[pl.ds(j, 1), :] = _max_key_norm2(kb, H, DH, TQ)
            return carry
        lax.fori_loop(0, NB, body, 0)

    qT = qT_ref[0]
    qTf = qT.astype(F32)
    km = km_scr[...]
    gate = jnp.concatenate(
        [jnp.dot(km[:, h * DH:(h + 1) * DH], qTf[h * DH:(h + 1) * DH, :], precision=lax.Precision.HIGHEST,
                 preferred_element_type=F32) for h in range(H)], axis=1) * (1.0 / ATT_QSCALE)
    blk = lax.broadcasted_iota(I32, gate.shape, 0)
    g = jnp.where(blk < qi, gate, NEG_INF)
    sels = []
    for _ in range(n_sel):
        mx = jnp.max(g, axis=0, keepdims=True)
        isel = jnp.min(jnp.where(g == mx, blk, NBP), axis=0, keepdims=True)
        sels.append(jnp.where(mx > NEG_INF, isel, -1))
        g = jnp.where(blk == isel, NEG_INF, g)

    qs = _block_diag_T(qT, H, DH)

    def hit_row(j):
        hit = sels[0] == j
        for s in sels[1:]:
            hit = hit | (s == j)
        return hit

    npair = (qi + 1) // 2
    blk_start = lambda j: pl.multiple_of(j * BS, BS)
    blk_scores = lambda j: jnp.dot(k_ref[0, pl.ds(blk_start(j), BS), :], qs, preferred_element_type=F32)
    hcols = lambda row, h: row[:, h * TQ:(h + 1) * TQ]
    causal = lax.broadcasted_iota(I32, (BS, TQ), 0) <= lax.broadcasted_iota(I32, (BS, TQ), 1)

    def diag_scores():
        s_all = blk_scores(qi)
        return [jnp.where(causal, hcols(s_all, h), NEG_INF) for h in range(H)]

    def max_sweep():
        def max_pair(g, m):
            for u in range(2):
                j = 2 * g + u
                m = jnp.maximum(m, jnp.where(hit_row(j), jnp.max(blk_scores(j), axis=0, keepdims=True), NEG_INF))
            return m
        m_all = lax.fori_loop(0, npair, max_pair, jnp.full((1, H * TQ), NEG_INF, F32))
        s_diag = diag_scores()
        return [jnp.maximum(hcols(m_all, h), jnp.max(s_diag[h], axis=0, keepdims=True)) for h in range(H)]

    def sum_sweep(m):
        acc_scr[...] = jnp.zeros_like(acc_scr)

        def sum_group(width):
            def body(g, l):
                for u in range(width):
                    s_scr[u] = blk_scores(width * g + u)
                for u in range(width):
                    j = width * g + u
                    hit = hit_row(j)
                    vT_blk = vT_ref[0, :, pl.ds(blk_start(j), BS)]
                    l = tuple(_accumulate(h, jnp.exp2(s_scr[u, :, h * TQ:(h + 1) * TQ]
                                                      - jnp.where(hcols(hit, h), m[h], jnp.inf)),
                                          vT_blk, l[h], acc_scr, DH) for h in range(H))
                return l
            return body
        nwide = qi // ATT_STAGE
        l = lax.fori_loop(0, nwide, sum_group(ATT_STAGE), (jnp.zeros((1, TQ), F32),) * H)
        l = lax.fori_loop((ATT_STAGE // 2) * nwide, npair, sum_group(2), l)
        s_diag = diag_scores()
        vT_diag = vT_ref[0, :, pl.ds(blk_start(qi), BS)]
        return [_accumulate(h, jnp.exp2(s_diag[h] - m[h]), vT_diag, l[h], acc_scr, DH) for h in range(H)]

    blk_n = lax.broadcasted_iota(I32, kn_scr.shape, 0)
    k2 = jnp.max(jnp.where(blk_n <= qi, kn_scr[...], 0.0), axis=0, keepdims=True)
    l = sum_sweep(_norm_bound(qTf, k2, H, DH, TQ))
    _finish_attention(l, acc_scr, y_ref, H, DH)

    @pl.when(_underflowed(l))
    def _():
        _finish_attention(sum_sweep(max_sweep()), acc_scr, y_ref, H, DH)


def _moba(qT, k, vT, B, S):
    BS = MOBA_BLOCK
    assert S % BS == 0
    NB = S // BS
    NBP = -(-NB // SUBLANES) * SUBLANES
    n_sel = max(min(MOBA_TOPK, NB - 1), 1)
    k = k.reshape(B, S, MOBA_W)
    return pl.pallas_call(
        functools.partial(_moba_kernel, NB=NB, NBP=NBP, n_sel=n_sel),
        grid=(B, NB),
        in_specs=[pl.BlockSpec((1, MOBA_W, BS), lambda b, i: (b, 0, i)),
                  pl.BlockSpec((1, S, MOBA_W), lambda b, i: (b, 0, 0)),
                  pl.BlockSpec((1, MOBA_W, S), lambda b, i: (b, 0, 0))],
        out_specs=pl.BlockSpec((1, BS, MOBA_W), lambda b, i: (b, i, 0)),
        out_shape=jax.ShapeDtypeStruct((B, S, MOBA_W), BF16),
        scratch_shapes=[pltpu.VMEM((NBP, MOBA_W), F32),
                        pltpu.VMEM((NBP, MOBA_HEADS * BS), F32),
                        pltpu.VMEM((MOBA_W, BS), F32),
                        pltpu.VMEM((ATT_STAGE, BS, MOBA_HEADS * BS), F32)],
        compiler_params=_params("parallel", "arbitrary"),
    )(qT, k, vT)


def _dsa_kernel(qT_ref, k_ref, vT_ref, iqT_ref, ik_ref, smT_ref, y_ref,
                e_scr, gmax_scr, acc_scr, thr_scr, kn_scr, s_scr, *, TQ, topk, spow, n_chunks):
    H, DH = DSA_HEADS, DSA_DH
    KC = TQ
    qi = pl.program_id(1)
    nch = qi + 1
    npair = (nch + 1) // 2
    chunk_start = lambda c: pl.multiple_of(c * KC, KC)

    @pl.when(qi == 0)
    def _():
        kn_scr[...] = jnp.zeros_like(kn_scr)

        def body(c, carry):
            kn_scr[pl.ds(c, 1), :] = _max_key_norm2(k_ref[0, pl.ds(chunk_start(c), KC), :].astype(F32), H, DH, TQ)
            return carry
        lax.fori_loop(0, n_chunks, body, 0)
    sum8 = lambda b: jnp.sum(b.astype(I32).reshape(KC // SUBLANES, SUBLANES, TQ), axis=0)
    any_row = lambda r: jnp.max(r.astype(I32)) > 0

    iqT = iqT_ref[0]
    iq_cat = jnp.concatenate([iqT[h * IDX_DH:(h + 1) * IDX_DH, :] for h in range(IDX_HEADS)], axis=1)
    iw = smT_ref[0][2 * ML_HEADS:2 * ML_HEADS + IDX_HEADS, :] * (IDX_HEADS ** -0.5 * IDX_DH ** -0.5)
    t_pos = qi * TQ + lax.broadcasted_iota(I32, (KC, TQ), 1)
    s_off = lax.broadcasted_iota(I32, (KC, TQ), 0)
    gmax_scr[...] = jnp.full((KC, TQ), NEG_INF, F32)

    def score_group(width):
        def body(g, carry):
            n_pos, n_nonneg = carry
            for u in range(width):
                ikc = ik_ref[0, pl.ds(chunk_start(width * g + u), KC), :][:, :IDX_DH]
                s_scr[u] = jnp.dot(ikc, iq_cat, preferred_element_type=F32)
            for u in range(width):
                c = width * g + u
                sc = iw[0:1, :] * jnp.maximum(s_scr[u, :, 0:TQ], 0.0)
                for h in range(1, IDX_HEADS):
                    sc = sc + iw[h:h + 1, :] * jnp.maximum(s_scr[u, :, h * TQ:(h + 1) * TQ], 0.0)
                sc = jnp.where(c * KC + s_off <= t_pos, sc, NEG_INF)
                e_scr[pl.ds(chunk_start(c), KC), :] = sc
                gmax_scr[...] = jnp.maximum(gmax_scr[...], sc)
                n_pos = n_pos + sum8(sc > 0.0)
                n_nonneg = n_nonneg + sum8(sc >= 0.0)
            return n_pos, n_nonneg
        return body
    zero8 = jnp.zeros((SUBLANES, TQ), I32)
    counts = lax.fori_loop(0, npair // 2, score_group(4), (zero8, zero8))
    n_pos, n_nonneg = lax.fori_loop(2 * (npair // 2), npair, score_group(2), counts)
    n_pos = jnp.sum(n_pos, axis=0, keepdims=True)
    n_nonneg = jnp.sum(n_nonneg, axis=0, keepdims=True)

    def count_ge(cand):
        def body(g, acc):
            a, b = acc
            a = a + sum8(e_scr[pl.ds(chunk_start(2 * g), KC), :] >= cand)
            b = b + sum8(e_scr[pl.ds(chunk_start(2 * g + 1), KC), :] >= cand)
            return a, b
        a, b = lax.fori_loop(0, npair, body, (zero8, zero8))
        return jnp.sum(a + b, axis=0, keepdims=True)

    def rewrite(rows, thr):
        def body(c, carry):
            x = e_scr[pl.ds(chunk_start(c), KC), :]
            rank = (2 * spow - (c * KC + s_off)).astype(F32)
            new = jnp.where(x > thr, jnp.inf, jnp.where(x == thr, rank, NEG_INF))
            e_scr[pl.ds(chunk_start(c), KC), :] = jnp.where(rows, new, x)
            return carry
        lax.fori_loop(0, nch, body, 0)

    def bisect(lo, hi, act):
        def cond(c):
            return jnp.logical_and(c[0] < BISECT_CAP, c[5] > 0)

        def body(c):
            it, lo, hi, act, tie, _ = c
            for _ in range(BISECT_STEPS_PER_CHECK):
                mid = lo + (hi - lo) * 0.5
                inside = (mid > lo) & (mid < hi)
                cnt = count_ge(mid)
                live = act > 0.0
                run = live & inside
                up = cnt >= topk
                lo = jnp.where(run & up, mid, lo)
                hi = jnp.where(run & jnp.logical_not(up), mid, hi)
                collapsed = live & jnp.logical_not(inside)
                tie = jnp.where(collapsed, 1.0, tie)
                act = jnp.where(collapsed | (run & (cnt == topk)), 0.0, act)
            return it + BISECT_STEPS_PER_CHECK, lo, hi, act, tie, jnp.max(act)
        out = lax.while_loop(cond, body, (jnp.int32(0), lo, hi, act, jnp.zeros_like(act), jnp.max(act)))
        return out[1], out[4]

    fmax = float(jnp.finfo(F32).max)
    gm = gmax_scr[...]
    g_lo = jnp.min(gm, axis=0, keepdims=True)
    g_hi = jnp.max(gm, axis=0, keepdims=True)
    n_vis = qi * TQ + lax.broadcasted_iota(I32, (1, TQ), 1) + 1
    few = n_vis <= topk
    zero_tie = (n_pos < topk) & (n_nonneg >= topk) & jnp.logical_not(few)

    @pl.when(any_row(zero_tie))
    def _():
        rewrite(zero_tie, jnp.zeros((1, TQ), F32))

    rank_lo, rank_hi = float(spow), float(2 * spow + 1)
    lo0 = jnp.where(zero_tie, rank_lo, jnp.maximum(g_lo, -fmax))
    hi0 = jnp.where(zero_tie, rank_hi, g_hi + jnp.abs(g_hi) * (2.0 ** -10) + 1e-30)
    lo, tie = bisect(lo0, hi0, jnp.where(few, 0.0, 1.0))
    thr_scr[0:1, :] = jnp.where(few, -fmax, lo)

    @pl.when(jnp.max(tie) > 0.0)
    def _():
        rows = tie > 0.0
        rewrite(rows, lo)
        lo2, _ = bisect(jnp.full((1, TQ), rank_lo, F32), jnp.full((1, TQ), rank_hi, F32), tie)
        thr_scr[0:1, :] = jnp.where(rows, lo2, thr_scr[0:1, :])

    thr = thr_scr[0:1, :]
    qT = qT_ref[0]
    qs = _block_diag_T(qT, H, DH)
    hcols = lambda row, h: row[:, h * TQ:(h + 1) * TQ]

    def chunk_scores(c):
        start = chunk_start(c)
        sel = e_scr[pl.ds(start, KC), :] >= thr
        s_all = jnp.dot(k_ref[0, pl.ds(start, KC), :], qs, preferred_element_type=F32)
        return sel, s_all

    def max_sweep():
        def max_pair(g, m):
            for u in range(2):
                sel, s_all = chunk_scores(2 * g + u)
                m = tuple(jnp.maximum(m[h], jnp.max(jnp.where(sel, hcols(s_all, h), NEG_INF), axis=0, keepdims=True))
                          for h in range(H))
            return m
        m = lax.fori_loop(0, npair, max_pair, (jnp.full((1, TQ), NEG_INF, F32),) * H)
        return [jnp.where(mh == NEG_INF, 0.0, mh) for mh in m]

    def sum_sweep(m):
        acc_scr[...] = jnp.zeros_like(acc_scr)

        def sum_group(width):
            def body(g, l):
                for u in range(width):
                    s_scr[u] = jnp.dot(k_ref[0, pl.ds(chunk_start(width * g + u), KC), :], qs,
                                       preferred_element_type=F32)
                for u in range(width):
                    start = chunk_start(width * g + u)
                    sel = e_scr[pl.ds(start, KC), :] >= thr
                    vT_chunk = vT_ref[0, :, pl.ds(start, KC)]
                    l = tuple(_accumulate(h, jnp.exp2(s_scr[u, :, h * TQ:(h + 1) * TQ] + jnp.where(sel, -m[h], NEG_INF)),
                                          vT_chunk, l[h], acc_scr, DH) for h in range(H))
                return l
            return body
        nquad = npair // 2
        l = lax.fori_loop(0, nquad, sum_group(4), (jnp.zeros((1, TQ), F32),) * H)
        return lax.fori_loop(2 * nquad, npair, sum_group(2), l)

    chunk_n = lax.broadcasted_iota(I32, kn_scr.shape, 0)
    k2 = jnp.max(jnp.where(chunk_n < 2 * npair, kn_scr[...], 0.0), axis=0, keepdims=True)
    l = sum_sweep(_norm_bound(qT.astype(F32), k2, H, DH, TQ))
    _finish_attention(l, acc_scr, y_ref, H, DH)

    @pl.when(_underflowed(l))
    def _():
        _finish_attention(sum_sweep(max_sweep()), acc_scr, y_ref, H, DH)


def _dsa(qT, k, vT, iqT, ik, smT, B, S):
    TQ = min(ATT_TQ, S)
    topk = min(DSA_TOPK_MAX, S // 4)
    assert S % (2 * TQ) == 0 and topk <= TQ
    spow = 1 << max((S - 1).bit_length(), 1)
    k = k.reshape(B, S, DSA_W)
    ik = ik.reshape(B, S, LANES)
    qblk = lambda n: pl.BlockSpec((1, n, TQ), lambda b, i: (b, 0, i))
    return pl.pallas_call(
        functools.partial(_dsa_kernel, TQ=TQ, topk=topk, spow=spow, n_chunks=S // TQ),
        grid=(B, S // TQ),
        in_specs=[qblk(DSA_W),
                  pl.BlockSpec((1, S, DSA_W), lambda b, i: (b, 0, 0)),
                  pl.BlockSpec((1, DSA_W, S), lambda b, i: (b, 0, 0)),
                  qblk(IDX_HEADS * IDX_DH),
                  pl.BlockSpec((1, S, LANES), lambda b, i: (b, 0, 0)),
                  qblk(16)],
        out_specs=pl.BlockSpec((1, TQ, DSA_W), lambda b, i: (b, i, 0)),
        out_shape=jax.ShapeDtypeStruct((B, S, DSA_W), BF16),
        scratch_shapes=[pltpu.VMEM((S, TQ), F32),
                        pltpu.VMEM((TQ, TQ), F32),
                        pltpu.VMEM((DSA_W, TQ), F32),
                        pltpu.VMEM((SUBLANES, TQ), F32),
                        pltpu.VMEM((-(-(S // TQ) // SUBLANES) * SUBLANES, DSA_HEADS * TQ), F32),
                        pltpu.VMEM((4, TQ, DSA_HEADS * TQ), F32)],
        compiler_params=_params("parallel", "arbitrary"),
    )(qT, k, vT, iqT, ik, smT)


def _out_mlp_kernel(yml_ref, ymb_ref, yds_ref, x_ref, wout_ref, gpost_ref, gpre_ref,
                    w1_ref, w2_ref, g2_ref, out_ref, x1_scr, h_scr, acc_scr):
    f = pl.program_id(1)

    @pl.when(f == 0)
    def _():
        mix = jnp.dot(yml_ref[...], wout_ref[0:ML_V_W, :], preferred_element_type=F32)
        mix = mix + jnp.dot(ymb_ref[...], wout_ref[ML_V_W:ML_V_W + MOBA_W, :], preferred_element_type=F32)
        mix = mix + jnp.dot(yds_ref[...], wout_ref[ML_V_W + MOBA_W:, :], preferred_element_type=F32)
        x1 = x_ref[...] + _rms(mix, gpost_ref[...])
        x1_scr[...] = x1
        h_scr[...] = _rms(x1, gpre_ref[...]).astype(BF16)
        acc_scr[...] = jnp.zeros_like(acc_scr)

    u = jnp.maximum(jnp.dot(h_scr[...], w1_ref[...], preferred_element_type=F32), 0.0)
    acc_scr[...] += jnp.dot((u * u).astype(BF16), w2_ref[...], preferred_element_type=F32)

    @pl.when(f == pl.num_programs(1) - 1)
    def _():
        out_ref[...] = x1_scr[...] + _rms(acc_scr[...], g2_ref[...])


def _out_mlp(yml, ymb, yds, x2, w_out, g_post, g_pre, w1, w2, g2):
    T, D = x2.shape
    tm = min(MLP_TM, T)
    tf = min(MLP_TF, D_FF)
    row = lambda n: pl.BlockSpec((tm, n), lambda i, f: (i, 0))
    full = lambda a: pl.BlockSpec(a.shape, lambda i, f: (0,) * a.ndim)
    return pl.pallas_call(
        _out_mlp_kernel,
        grid=(T // tm, D_FF // tf),
        in_specs=[row(ML_V_W), row(MOBA_W), row(DSA_W), row(D), full(w_out), full(g_post), full(g_pre),
                  pl.BlockSpec((D, tf), lambda i, f: (0, f)),
                  pl.BlockSpec((tf, D), lambda i, f: (f, 0)),
                  full(g2)],
        out_specs=row(D),
        out_shape=jax.ShapeDtypeStruct((T, D), F32),
        scratch_shapes=[pltpu.VMEM((tm, D), F32), pltpu.VMEM((tm, D), BF16), pltpu.VMEM((tm, D), F32)],
        compiler_params=_params("parallel", "arbitrary"),
    )(yml, ymb, yds, x2, w_out, g_post, g_pre, w1, w2, g2)


def kernel(x, norm_mix_pre, w_in, ml_conv, ml_i_bias, ml_f_bias, ml_head_norm, w_out, norm_mix_post,
           norm_mlp_pre, w_ff1, w_ff2, norm_mlp_post):
    B, S, D = x.shape
    depth = w_in.shape[0]
    x2 = x.reshape(B * S, D)
    gain = lambda g: g.reshape(1, D).astype(F32)
    for l in range(depth):
        wn, wt = _prep_in_weights(w_in[l])
        (qk, mbk, dsk, ixk, mbqT, mbvT, dsqT, dsvT, ixqT, mlvT, mloT, smT) = _in_proj(
            x2, gain(norm_mix_pre[l]), wn, wt, B, S)
        y_ml = _mlstm(qk, mlvT, mloT, smT, ml_conv[l], ml_i_bias[l], ml_f_bias[l], ml_head_norm[l], B, S)
        y_mb = _moba(mbqT, mbk, mbvT, B, S)
        y_ds = _dsa(dsqT, dsk, dsvT, ixqT, ixk, smT, B, S)
        x2 = _out_mlp(y_ml.reshape(B * S, ML_V_W), y_mb.reshape(B * S, MOBA_W), y_ds.reshape(B * S, DSA_W),
                      x2, w_out[l].astype(BF16), gain(norm_mix_post[l]), gain(norm_mlp_pre[l]),
                      w_ff1[l].astype(BF16), w_ff2[l].astype(BF16), gain(norm_mlp_post[l]))
    return x2.reshape(B, S, D)
```

```python
import functools

import jax
import jax.numpy as jnp
from jax import lax
from jax.experimental import pallas as pl
from jax.experimental.pallas import tpu as pltpu

F32 = jnp.float32
BF16 = jnp.bfloat16
I32 = jnp.int32

D_MODEL = 1024
ML_HEADS, ML_DQK, ML_DV, ML_CONV = 4, 64, 128, 4
MOBA_HEADS, MOBA_DH, MOBA_BLOCK, MOBA_TOPK = 4, 64, 256, 3
DSA_HEADS, DSA_DH, IDX_HEADS, IDX_DH, DSA_TOPK_MAX = 4, 64, 4, 64, 256
D_FF = 4 * D_MODEL
EPS = 1e-6

ML_QK_W = ML_HEADS * ML_DQK
ML_V_W = ML_HEADS * ML_DV
MOBA_W = MOBA_HEADS * MOBA_DH
DSA_W = DSA_HEADS * DSA_DH
IN_SPLITS = (ML_QK_W, ML_QK_W, ML_V_W, ML_V_W, ML_HEADS, ML_HEADS,
             MOBA_W, MOBA_W, MOBA_W,
             DSA_W, DSA_W, DSA_W, IDX_HEADS * IDX_DH, IDX_DH, IDX_HEADS)

LANES = 128
SUBLANES = 8
VMEM_LIMIT = 52 * 1024 * 1024

ML_CHUNK = 256
ATT_TQ = 256
ATT_STAGE = 8
PROJ_TM = 512
MLP_TM = 1024
MLP_TF = 512

NEG_INF = float("-inf")
BISECT_CAP = 300
BISECT_WARMUP = 12
BISECT_STEPS_PER_CHECK = 2
SOFTMAX_L_MIN = 1e-30
assert MOBA_DH == DSA_DH
ATT_QSCALE = MOBA_DH ** -0.5 * 1.4426950408889634

_NT = (((1,), (1,)), ((), ()))
_TN = (((0,), (0,)), ((), ()))


def _rms(x, g):
    return x * lax.rsqrt(jnp.mean(x * x, axis=-1, keepdims=True) + EPS) * g


def _log_sigmoid(x):
    return jnp.minimum(x, 0.0) - jnp.log1p(jnp.exp(-jnp.abs(x)))


def _params(*sem):
    return pltpu.CompilerParams(dimension_semantics=sem, vmem_limit_bytes=VMEM_LIMIT)


_N_QK, _N_MBK, _N_DSK, _N_IXK = 0, 512, 768, 1024
_N_TOTAL = 1152
_T_MBQ, _T_MBV, _T_DSQ, _T_DSV, _T_IXQ, _T_MLV, _T_MLO, _T_SM = 0, 256, 512, 768, 1024, 1280, 1792, 2304
_T_TOTAL = 2320


def _in_proj_kernel(x_ref, g_ref, wn_ref, wt_ref,
                    qk_ref, mbk_ref, dsk_ref, ixk_ref,
                    mbqT_ref, mbvT_ref, dsqT_ref, dsvT_ref, ixqT_ref, mlvT_ref, mloT_ref, smT_ref):
    h = _rms(x_ref[...], g_ref[...]).astype(BF16)

    def mm(a, n):
        return jnp.dot(h, wn_ref[:, a:a + n], preferred_element_type=F32)

    def mt(a, n):
        return lax.dot_general(wt_ref[a:a + n, :], h, _NT, preferred_element_type=F32)

    qk_ref[...] = mm(_N_QK, 512)
    mbk_ref[...] = mm(_N_MBK, 256).astype(BF16)
    dsk_ref[...] = mm(_N_DSK, 256).astype(BF16)
    ixk_ref[...] = mm(_N_IXK, 128).astype(BF16)
    mbqT_ref[0] = (mt(_T_MBQ, 256) * ATT_QSCALE).astype(BF16)
    mbvT_ref[0] = mt(_T_MBV, 256).astype(BF16)
    dsqT_ref[0] = (mt(_T_DSQ, 256) * ATT_QSCALE).astype(BF16)
    dsvT_ref[0] = mt(_T_DSV, 256).astype(BF16)
    ixqT_ref[0] = mt(_T_IXQ, 256).astype(BF16)
    mlvT_ref[0] = mt(_T_MLV, 512).astype(BF16)
    mloT_ref[0] = mt(_T_MLO, 512)
    smT_ref[0] = mt(_T_SM, 16)


def _prep_in_weights(w_in):
    pts, acc = [], 0
    for n in IN_SPLITS:
        pts.append((acc, acc + n))
        acc += n
    col = lambda i: w_in[:, pts[i][0]:pts[i][1]]
    (ml_q, ml_k, ml_v, ml_o, ml_i, ml_f, mb_q, mb_k, mb_v,
     ds_q, ds_k, ds_v, ix_q, ix_k, ix_w) = [col(i) for i in range(len(IN_SPLITS))]
    d = w_in.shape[0]
    small = jnp.concatenate([ml_i, ml_f, ix_w], axis=1)
    wn = jnp.concatenate([
        ml_q, ml_k, mb_k, ds_k,
        ix_k, jnp.zeros((d, 128 - IDX_DH), w_in.dtype)], axis=1)
    wt = jnp.concatenate([
        mb_q, mb_v, ds_q, ds_v, ix_q, ml_v, ml_o,
        small, jnp.zeros((d, 16 - small.shape[1]), w_in.dtype)], axis=1).T
    assert wn.shape == (d, _N_TOTAL) and wt.shape == (_T_TOTAL, d)
    return wn.astype(BF16), wt.astype(BF16)


def _in_proj(x2, g, wn, wt, B, S):
    T, D = x2.shape
    tm = min(PROJ_TM, S)
    nsb = S // tm
    row = lambda n: pl.BlockSpec((tm, n), lambda i: (i, 0))
    tr = lambda n: pl.BlockSpec((1, n, tm), lambda i: (i // nsb, 0, i % nsb))
    full = lambda a: pl.BlockSpec(a.shape, lambda i: (0,) * a.ndim)
    out_shape = (
        jax.ShapeDtypeStruct((T, 512), F32),
        jax.ShapeDtypeStruct((T, 256), BF16),
        jax.ShapeDtypeStruct((T, 256), BF16),
        jax.ShapeDtypeStruct((T, 128), BF16),
        jax.ShapeDtypeStruct((B, 256, S), BF16),
        jax.ShapeDtypeStruct((B, 256, S), BF16),
        jax.ShapeDtypeStruct((B, 256, S), BF16),
        jax.ShapeDtypeStruct((B, 256, S), BF16),
        jax.ShapeDtypeStruct((B, 256, S), BF16),
        jax.ShapeDtypeStruct((B, 512, S), BF16),
        jax.ShapeDtypeStruct((B, 512, S), F32),
        jax.ShapeDtypeStruct((B, 16, S), F32),
    )
    out_specs = (row(512), row(256), row(256), row(128),
                 tr(256), tr(256), tr(256), tr(256), tr(256), tr(512), tr(512), tr(16))
    return pl.pallas_call(
        _in_proj_kernel,
        grid=(T // tm,),
        in_specs=[row(D), full(g), full(wn), full(wt)],
        out_specs=out_specs,
        out_shape=out_shape,
        compiler_params=_params("parallel"),
    )(x2, g, wn, wt)


def _dot3(a_f32, b_exact):
    b = b_exact.astype(BF16)
    hi = a_f32.astype(BF16)
    r1 = a_f32 - hi.astype(F32)
    mid = r1.astype(BF16)
    lo = (r1 - mid.astype(F32)).astype(BF16)
    return (jnp.dot(hi, b, preferred_element_type=F32) + jnp.dot(mid, b, preferred_element_type=F32)
            + jnp.dot(lo, b, preferred_element_type=F32))


def _mlstm_kernel(qk_ref, vT_ref, oT_ref, smT_ref, conv_ref, bcol_ref, hn_ref,
                  y_ref, xp_scr, st_scr, m_scr, *, L):
    c = pl.program_id(1)
    halo = SUBLANES
    DK, DV = ML_DQK, ML_DV

    @pl.when(c == 0)
    def _():
        xp_scr[0:halo, :] = jnp.zeros((halo, 2 * ML_QK_W), F32)
        st_scr[...] = jnp.zeros_like(st_scr)
        m_scr[...] = jnp.zeros_like(m_scr)

    cur = qk_ref[0]
    xp_scr[halo:halo + L, :] = cur
    base = halo - (ML_CONV - 1)
    acc = conv_ref[0:1, :] * xp_scr[base:base + L, :]
    for j in range(1, ML_CONV):
        acc = acc + conv_ref[j:j + 1, :] * xp_scr[base + j:base + j + L, :]
    xp_scr[0:halo, :] = cur[L - halo:L, :]
    qk = acc * jax.nn.sigmoid(acc)
    qT_all = (qk[:, :ML_QK_W] * (DK ** -0.5)).T.astype(BF16)
    k_all = qk[:, ML_QK_W:].astype(BF16)

    grow = smT_ref[0][0:SUBLANES, :] + bcol_ref[...]
    si = lax.broadcasted_iota(I32, (L, L), 0)
    ji = lax.broadcasted_iota(I32, (L, L), 1)
    causal = si <= ji
    b_row = _dot3(_log_sigmoid(grow), causal)

    for h in range(ML_HEADS):
        qT = qT_all[h * DK:(h + 1) * DK, :]
        kh = k_all[:, h * DK:(h + 1) * DK]
        vT = vT_ref[0, h * DV:(h + 1) * DV, :]
        b_j = b_row[ML_HEADS + h:ML_HEADS + h + 1, :]
        c_row = grow[h:h + 1, :] - b_j
        c_col = jnp.broadcast_to(c_row, (LANES, L)).T
        dlog = jnp.where(causal, b_j + jnp.concatenate([c_col] * (L // LANES), axis=1), NEG_INF)
        m0 = m_scr[h:h + 1, 0:1]
        inter = b_j + m0
        m_t = jnp.maximum(inter, jnp.max(dlog, axis=0, keepdims=True))
        w_inter = jnp.exp(inter - m_t)
        w_intra = jnp.exp(dlog - m_t) * jnp.dot(kh, qT, preferred_element_type=F32)
        st = st_scr[h]
        cq = jnp.dot(st.astype(BF16), qT, preferred_element_type=F32)
        num = w_inter * cq[:DV] + jnp.dot(vT, w_intra.astype(BF16), preferred_element_type=F32)
        den = w_inter * cq[DV:DV + 1] + jnp.sum(w_intra, axis=0, keepdims=True)
        hh = num / jnp.maximum(jnp.abs(den), jnp.exp(-m_t))
        hh = hh * lax.rsqrt(jnp.mean(hh * hh, axis=0, keepdims=True) + EPS)
        rows = slice(h * DV, (h + 1) * DV)
        y_ref[0, :, rows] = (jax.nn.sigmoid(oT_ref[0, rows, :]) * hh * hn_ref[rows, :]).T.astype(y_ref.dtype)

        b_last = b_j[:, L - 1:L]
        a = b_last + c_row
        m_loc = jnp.max(a, axis=1, keepdims=True)
        wa = jnp.exp(a - m_loc)
        lhs = jnp.concatenate([(vT.astype(F32) * wa).astype(BF16),
                               jnp.broadcast_to(wa, (SUBLANES, L)).astype(BF16)], axis=0)
        c_loc = jnp.dot(lhs, kh, preferred_element_type=F32)
        m_new = jnp.maximum(b_last + m0, m_loc)
        s_old = jnp.exp(b_last + m0 - m_new)
        s_loc = jnp.exp(m_loc - m_new)
        st_scr[h] = s_old * st + s_loc * c_loc
        m_scr[h:h + 1, :] = jnp.broadcast_to(m_new, (1, LANES))


def _mlstm(qk, vT, oT, smT, conv_w, i_bias, f_bias, head_norm, B, S):
    L = min(ML_CHUNK, S)
    assert L % LANES == 0
    nc = S // L
    qk = qk.reshape(B, S, 2 * ML_QK_W)
    bcol = jnp.concatenate([i_bias, f_bias]).astype(F32).reshape(2 * ML_HEADS, 1)
    hn = jnp.broadcast_to(head_norm.astype(F32)[:, None], (ML_V_W, L))
    blkT = lambda n: pl.BlockSpec((1, n, L), lambda b, c: (b, 0, c))
    full = lambda a: pl.BlockSpec(a.shape, lambda b, c: (0,) * a.ndim)
    return pl.pallas_call(
        functools.partial(_mlstm_kernel, L=L),
        grid=(B, nc),
        in_specs=[pl.BlockSpec((1, L, 2 * ML_QK_W), lambda b, c: (b, c, 0)),
                  blkT(ML_V_W), blkT(ML_V_W), blkT(16), full(conv_w), full(bcol), full(hn)],
        out_specs=pl.BlockSpec((1, L, ML_V_W), lambda b, c: (b, c, 0)),
        out_shape=jax.ShapeDtypeStruct((B, S, ML_V_W), BF16),
        scratch_shapes=[pltpu.VMEM((L + SUBLANES, 2 * ML_QK_W), F32),
                        pltpu.VMEM((ML_HEADS, ML_DV + SUBLANES, ML_DQK), F32),
                        pltpu.VMEM((SUBLANES, LANES), F32)],
        compiler_params=_params("parallel", "arbitrary"),
    )(qk, vT, oT, smT, conv_w.astype(F32), bcol, hn)


def _block_diag_T(qT, heads, dh):
    rid = lax.broadcasted_iota(I32, qT.shape, 0)
    zero = jnp.zeros_like(qT)
    return jnp.concatenate(
        [jnp.where((rid >= h * dh) & (rid < (h + 1) * dh), qT, zero) for h in range(heads)], axis=1)


def _scores(k_chunk, qs, h, TQ):
    return jnp.dot(k_chunk, qs[:, h * TQ:(h + 1) * TQ], preferred_element_type=F32)


def _accumulate(h, p, vT_chunk, l, acc_scr, dh):
    rows = slice(h * dh, (h + 1) * dh)
    acc_scr[rows, :] += jnp.dot(vT_chunk[rows, :], p.astype(BF16), preferred_element_type=F32)
    return l + jnp.sum(p, axis=0, keepdims=True)


def _finish_attention(l, acc_scr, y_ref, heads, dh):
    outs = [acc_scr[h * dh:(h + 1) * dh, :] / l[h] for h in range(heads)]
    y_ref[0] = jnp.concatenate(outs, axis=0).T.astype(y_ref.dtype)


def _max_key_norm2(kb, heads, dh, TQ):
    ch = lax.broadcasted_iota(I32, (heads * dh, LANES), 0) // dh
    hd = lax.broadcasted_iota(I32, (heads * dh, LANES), 1)
    n2 = jnp.dot(kb * kb, (ch == hd).astype(F32), precision=lax.Precision.HIGHEST, preferred_element_type=F32)
    n2 = jnp.max(n2, axis=0, keepdims=True)
    return jnp.concatenate([jnp.broadcast_to(n2[:, h:h + 1], (1, TQ)) for h in range(heads)], axis=1)


def _norm_bound(qTf, k2, heads, dh, TQ):
    out = []
    for h in range(heads):
        qh = qTf[h * dh:(h + 1) * dh, :]
        q2 = jnp.sum(qh * qh, axis=0, keepdims=True)
        out.append(jnp.sqrt(q2 * k2[:, h * TQ:(h + 1) * TQ]) * (1.0 + 2.0 ** -8))
    return out


def _underflowed(l):
    small = l[0]
    for lh in l[1:]:
        small = jnp.minimum(small, lh)
    return jnp.logical_not(jnp.min(small) >= SOFTMAX_L_MIN)


def _moba_kernel(qT_ref, k_ref, vT_ref, y_ref, km_scr, kn_scr, acc_scr, s_scr, *, NB, NBP, n_sel):
    H, DH, BS = MOBA_HEADS, MOBA_DH, MOBA_BLOCK
    TQ = BS
    qi = pl.program_id(1)

    @pl.when(qi == 0)
    def _():
        km_scr[...] = jnp.zeros_like(km_scr)
        kn_scr[...] = jnp.zeros_like(kn_scr)

        def body(j, carry):
            kb = k_ref[0, pl.ds(pl.multiple_of(j * BS, BS), BS), :].astype(F32)
            km_scr[pl.ds(j, 1), :] = jnp.mean(kb, axis=0, keepdims=True)
            kn_scr[pl.ds(j, 1), :] = _max_key_norm2(kb, H, DH, TQ)
            return carry
        lax.fori_loop(0, NB, body, 0)

    qT = qT_ref[0]
    qTf = qT.astype(F32)
    km = km_scr[...]
    gate = jnp.concatenate(
        [jnp.dot(km[:, h * DH:(h + 1) * DH], qTf[h * DH:(h + 1) * DH, :], precision=lax.Precision.HIGHEST,
                 preferred_element_type=F32) for h in range(H)], axis=1) * (1.0 / ATT_QSCALE)
    blk = lax.broadcasted_iota(I32, gate.shape, 0)
    g = jnp.where(blk < qi, gate, NEG_INF)
    sels = []
    for _ in range(n_sel):
        mx = jnp.max(g, axis=0, keepdims=True)
        isel = jnp.min(jnp.where(g == mx, blk, NBP), axis=0, keepdims=True)
        sels.append(jnp.where(mx > NEG_INF, isel, -1))
        g = jnp.where(blk == isel, NEG_INF, g)

    qs = _block_diag_T(qT, H, DH)

    def hit_row(j):
        hit = sels[0] == j
        for s in sels[1:]:
            hit = hit | (s == j)
        return hit

    npair = (qi + 1) // 2
    blk_start = lambda j: pl.multiple_of(j * BS, BS)
    blk_scores = lambda j: jnp.dot(k_ref[0, pl.ds(blk_start(j), BS), :], qs, preferred_element_type=F32)
    hcols = lambda row, h: row[:, h * TQ:(h + 1) * TQ]
    causal = lax.broadcasted_iota(I32, (BS, TQ), 0) <= lax.broadcasted_iota(I32, (BS, TQ), 1)

    def diag_scores():
        s_all = blk_scores(qi)
        return [jnp.where(causal, hcols(s_all, h), NEG_INF) for h in range(H)]

    def max_sweep():
        def max_pair(g, m):
            for u in range(2):
                j = 2 * g + u
                m = jnp.maximum(m, jnp.where(hit_row(j), jnp.max(blk_scores(j), axis=0, keepdims=True), NEG_INF))
            return m
        m_all = lax.fori_loop(0, npair, max_pair, jnp.full((1, H * TQ), NEG_INF, F32))
        s_diag = diag_scores()
        return [jnp.maximum(hcols(m_all, h), jnp.max(s_diag[h], axis=0, keepdims=True)) for h in range(H)]

    def sum_sweep(m):
        acc_scr[...] = jnp.zeros_like(acc_scr)

        def sum_group(width):
            def body(g, l):
                for u in range(width):
                    s_scr[u] = blk_scores(width * g + u)
                for u in range(width):
                    j = width * g + u
                    hit = hit_row(j)
                    vT_blk = vT_ref[0, :, pl.ds(blk_start(j), BS)]
                    l = tuple(_accumulate(h, jnp.exp2(s_scr[u, :, h * TQ:(h + 1) * TQ]
                                                      - jnp.where(hcols(hit, h), m[h], jnp.inf)),
                                          vT_blk, l[h], acc_scr, DH) for h in range(H))
                return l
            return body
        nwide = qi // ATT_STAGE
        l = lax.fori_loop(0, nwide, sum_group(ATT_STAGE), (jnp.zeros((1, TQ), F32),) * H)
        l = lax.fori_loop((ATT_STAGE // 2) * nwide, npair, sum_group(2), l)
        s_diag = diag_scores()
        vT_diag = vT_ref[0, :, pl.ds(blk_start(qi), BS)]
        return [_accumulate(h, jnp.exp2(s_diag[h] - m[h]), vT_diag, l[h], acc_scr, DH) for h in range(H)]

    blk_n = lax.broadcasted_iota(I32, kn_scr.shape, 0)
    k2 = jnp.max(jnp.where(blk_n <= qi, kn_scr[...], 0.0), axis=0, keepdims=True)
    l = sum_sweep(_norm_bound(qTf, k2, H, DH, TQ))
    _finish_attention(l, acc_scr, y_ref, H, DH)

    @pl.when(_underflowed(l))
    def _():
        _finish_attention(sum_sweep(max_sweep()), acc_scr, y_ref, H, DH)


def _moba(qT, k, vT, B, S):
    BS = MOBA_BLOCK
    assert S % BS == 0
    NB = S // BS
    NBP = -(-NB // SUBLANES) * SUBLANES
    n_sel = max(min(MOBA_TOPK, NB - 1), 1)
    k = k.reshape(B, S, MOBA_W)
    return pl.pallas_call(
        functools.partial(_moba_kernel, NB=NB, NBP=NBP, n_sel=n_sel),
        grid=(B, NB),
        in_specs=[pl.BlockSpec((1, MOBA_W, BS), lambda b, i: (b, 0, i)),
                  pl.BlockSpec((1, S, MOBA_W), lambda b, i: (b, 0, 0)),
                  pl.BlockSpec((1, MOBA_W, S), lambda b, i: (b, 0, 0))],
        out_specs=pl.BlockSpec((1, BS, MOBA_W), lambda b, i: (b, i, 0)),
        out_shape=jax.ShapeDtypeStruct((B, S, MOBA_W), BF16),
        scratch_shapes=[pltpu.VMEM((NBP, MOBA_W), F32),
                        pltpu.VMEM((NBP, MOBA_HEADS * BS), F32),
                        pltpu.VMEM((MOBA_W, BS), F32),
                        pltpu.VMEM((ATT_STAGE, BS, MOBA_HEADS * BS), F32)],
        compiler_params=_params("parallel", "arbitrary"),
    )(qT, k, vT)


def _dsa_kernel(qT_ref, k_ref, vT_ref, iqT_ref, ik_ref, smT_ref, y_ref,
                e_scr, gmax_scr, acc_scr, thr_scr, kn_scr, s_scr, *, TQ, topk, spow, n_chunks):
    H, DH = DSA_HEADS, DSA_DH
    KC = TQ
    qi = pl.program_id(1)
    nch = qi + 1
    npair = (nch + 1) // 2
    chunk_start = lambda c: pl.multiple_of(c * KC, KC)

    @pl.when(qi == 0)
    def _():
        kn_scr[...] = jnp.zeros_like(kn_scr)

        def body(c, carry):
            kn_scr[pl.ds(c, 1), :] = _max_key_norm2(k_ref[0, pl.ds(chunk_start(c), KC), :].astype(F32), H, DH, TQ)
            return carry
        lax.fori_loop(0, n_chunks, body, 0)
    sum8 = lambda b: jnp.sum(b.astype(I32).reshape(KC // SUBLANES, SUBLANES, TQ), axis=0)
    any_row = lambda r: jnp.max(r.astype(I32)) > 0

    iqT = iqT_ref[0]
    iq_cat = jnp.concatenate([iqT[h * IDX_DH:(h + 1) * IDX_DH, :] for h in range(IDX_HEADS)], axis=1)
    iw = smT_ref[0][2 * ML_HEADS:2 * ML_HEADS + IDX_HEADS, :] * (IDX_HEADS ** -0.5 * IDX_DH ** -0.5)
    t_pos = qi * TQ + lax.broadcasted_iota(I32, (KC, TQ), 1)
    s_off = lax.broadcasted_iota(I32, (KC, TQ), 0)
    gmax_scr[...] = jnp.full((KC, TQ), NEG_INF, F32)

    def score_group(width):
        def body(g, carry):
            n_pos, n_nonneg = carry
            for u in range(width):
                ikc = ik_ref[0, pl.ds(chunk_start(width * g + u), KC), :][:, :IDX_DH]
                s_scr[u] = jnp.dot(ikc, iq_cat, preferred_element_type=F32)
            for u in range(width):
                c = width * g + u
                sc = iw[0:1, :] * jnp.maximum(s_scr[u, :, 0:TQ], 0.0)
                for h in range(1, IDX_HEADS):
                    sc = sc + iw[h:h + 1, :] * jnp.maximum(s_scr[u, :, h * TQ:(h + 1) * TQ], 0.0)
                sc = jnp.where(c * KC + s_off <= t_pos, sc, NEG_INF)
                e_scr[pl.ds(chunk_start(c), KC), :] = sc
                gmax_scr[...] = jnp.maximum(gmax_scr[...], sc)
                n_pos = n_pos + sum8(sc > 0.0)
                n_nonneg = n_nonneg + sum8(sc >= 0.0)
            return n_pos, n_nonneg
        return body
    zero8 = jnp.zeros((SUBLANES, TQ), I32)
    counts = lax.fori_loop(0, npair // 2, score_group(4), (zero8, zero8))
    n_pos, n_nonneg = lax.fori_loop(2 * (npair // 2), npair, score_group(2), counts)
    n_pos = jnp.sum(n_pos, axis=0, keepdims=True)
    n_nonneg = jnp.sum(n_nonneg, axis=0, keepdims=True)

    def count_ge(cand):
        def body(g, acc):
            a, b = acc
            a = a + sum8(e_scr[pl.ds(chunk_start(2 * g), KC), :] >= cand)
            b = b + sum8(e_scr[pl.ds(chunk_start(2 * g + 1), KC), :] >= cand)
            return a, b
        a, b = lax.fori_loop(0, npair, body, (zero8, zero8))
        return jnp.sum(a + b, axis=0, keepdims=True)

    def rewrite(rows, thr):
        def body(c, carry):
            x = e_scr[pl.ds(chunk_start(c), KC), :]
            rank = (2 * spow - (c * KC + s_off)).astype(F32)
            new = jnp.where(x > thr, jnp.inf, jnp.where(x == thr, rank, NEG_INF))
            e_scr[pl.ds(chunk_start(c), KC), :] = jnp.where(rows, new, x)
            return carry
        lax.fori_loop(0, nch, body, 0)

    def bisect(lo, hi, act, warmup):
        def step(_, c):
            lo, hi, act, tie = c
            mid = lo + (hi - lo) * 0.5
            inside = (mid > lo) & (mid < hi)
            cnt = count_ge(mid)
            live = act > 0.0
            run = live & inside
            up = cnt >= topk
            lo = jnp.where(run & up, mid, lo)
            hi = jnp.where(run & jnp.logical_not(up), mid, hi)
            collapsed = live & jnp.logical_not(inside)
            tie = jnp.where(collapsed, 1.0, tie)
            act = jnp.where(collapsed | (run & (cnt == topk)), 0.0, act)
            return lo, hi, act, tie

        def cond(c):
            return jnp.logical_and(c[0] < BISECT_CAP, c[2] > 0)

        def body(c):
            state = c[1]
            for _ in range(BISECT_STEPS_PER_CHECK):
                state = step(0, state)
            return c[0] + BISECT_STEPS_PER_CHECK, state, jnp.max(state[2])
        state = lax.fori_loop(0, warmup, step, (lo, hi, act, jnp.zeros_like(act)))
        out = lax.while_loop(cond, body, (jnp.int32(warmup), state, jnp.max(state[2])))
        return out[1][0], out[1][3]

    fmax = float(jnp.finfo(F32).max)
    gm = gmax_scr[...]
    g_lo = jnp.min(gm, axis=0, keepdims=True)
    g_hi = jnp.max(gm, axis=0, keepdims=True)
    n_vis = qi * TQ + lax.broadcasted_iota(I32, (1, TQ), 1) + 1
    few = n_vis <= topk
    zero_tie = (n_pos < topk) & (n_nonneg >= topk) & jnp.logical_not(few)

    @pl.when(any_row(zero_tie))
    def _():
        rewrite(zero_tie, jnp.zeros((1, TQ), F32))

    rank_lo, rank_hi = float(spow), float(2 * spow + 1)
    lo0 = jnp.where(zero_tie, rank_lo, jnp.maximum(g_lo, -fmax))
    hi0 = jnp.where(zero_tie, rank_hi, g_hi + jnp.abs(g_hi) * (2.0 ** -10) + 1e-30)
    lo, tie = bisect(lo0, hi0, jnp.where(few, 0.0, 1.0), BISECT_WARMUP)
    thr_scr[0:1, :] = jnp.where(few, -fmax, lo)

    @pl.when(jnp.max(tie) > 0.0)
    def _():
        rows = tie > 0.0
        rewrite(rows, lo)
        lo2, _ = bisect(jnp.full((1, TQ), rank_lo, F32), jnp.full((1, TQ), rank_hi, F32), tie, 0)
        thr_scr[0:1, :] = jnp.where(rows, lo2, thr_scr[0:1, :])

    thr = thr_scr[0:1, :]
    qT = qT_ref[0]
    qs = _block_diag_T(qT, H, DH)
    hcols = lambda row, h: row[:, h * TQ:(h + 1) * TQ]

    def chunk_scores(c):
        start = chunk_start(c)
        sel = e_scr[pl.ds(start, KC), :] >= thr
        s_all = jnp.dot(k_ref[0, pl.ds(start, KC), :], qs, preferred_element_type=F32)
        return sel, s_all

    def max_sweep():
        def max_pair(g, m):
            for u in range(2):
                sel, s_all = chunk_scores(2 * g + u)
                m = tuple(jnp.maximum(m[h], jnp.max(jnp.where(sel, hcols(s_all, h), NEG_INF), axis=0, keepdims=True))
                          for h in range(H))
            return m
        m = lax.fori_loop(0, npair, max_pair, (jnp.full((1, TQ), NEG_INF, F32),) * H)
        return [jnp.where(mh == NEG_INF, 0.0, mh) for mh in m]

    def sum_sweep(m):
        acc_scr[...] = jnp.zeros_like(acc_scr)

        def sum_group(width):
            def body(g, l):
                for u in range(width):
                    s_scr[u] = jnp.dot(k_ref[0, pl.ds(chunk_start(width * g + u), KC), :], qs,
                                       preferred_element_type=F32)
                for u in range(width):
                    start = chunk_start(width * g + u)
                    sel = e_scr[pl.ds(start, KC), :] >= thr
                    vT_chunk = vT_ref[0, :, pl.ds(start, KC)]
                    l = tuple(_accumulate(h, jnp.exp2(s_scr[u, :, h * TQ:(h + 1) * TQ] + jnp.where(sel, -m[h], NEG_INF)),
                                          vT_chunk, l[h], acc_scr, DH) for h in range(H))
                return l
            return body
        nwide = (2 * npair) // ATT_STAGE
        l = lax.fori_loop(0, nwide, sum_group(ATT_STAGE), (jnp.zeros((1, TQ), F32),) * H)
        return lax.fori_loop((ATT_STAGE // 2) * nwide, npair, sum_group(2), l)

    chunk_n = lax.broadcasted_iota(I32, kn_scr.shape, 0)
    k2 = jnp.max(jnp.where(chunk_n < 2 * npair, kn_scr[...], 0.0), axis=0, keepdims=True)
    l = sum_sweep(_norm_bound(qT.astype(F32), k2, H, DH, TQ))
    _finish_attention(l, acc_scr, y_ref, H, DH)

    @pl.when(_underflowed(l))
    def _():
        _finish_attention(sum_sweep(max_sweep()), acc_scr, y_ref, H, DH)


def _dsa(qT, k, vT, iqT, ik, smT, B, S):
    TQ = min(ATT_TQ, S)
    topk = min(DSA_TOPK_MAX, S // 4)
    assert S % (2 * TQ) == 0 and topk <= TQ
    spow = 1 << max((S - 1).bit_length(), 1)
    k = k.reshape(B, S, DSA_W)
    ik = ik.reshape(B, S, LANES)
    qblk = lambda n: pl.BlockSpec((1, n, TQ), lambda b, i: (b, 0, i))
    return pl.pallas_call(
        functools.partial(_dsa_kernel, TQ=TQ, topk=topk, spow=spow, n_chunks=S // TQ),
        grid=(B, S // TQ),
        in_specs=[qblk(DSA_W),
                  pl.BlockSpec((1, S, DSA_W), lambda b, i: (b, 0, 0)),
                  pl.BlockSpec((1, DSA_W, S), lambda b, i: (b, 0, 0)),
                  qblk(IDX_HEADS * IDX_DH),
                  pl.BlockSpec((1, S, LANES), lambda b, i: (b, 0, 0)),
                  qblk(16)],
        out_specs=pl.BlockSpec((1, TQ, DSA_W), lambda b, i: (b, i, 0)),
        out_shape=jax.ShapeDtypeStruct((B, S, DSA_W), BF16),
        scratch_shapes=[pltpu.VMEM((S, TQ), F32),
                        pltpu.VMEM((TQ, TQ), F32),
                        pltpu.VMEM((DSA_W, TQ), F32),
                        pltpu.VMEM((SUBLANES, TQ), F32),
                        pltpu.VMEM((-(-(S // TQ) // SUBLANES) * SUBLANES, DSA_HEADS * TQ), F32),
                        pltpu.VMEM((ATT_STAGE, TQ, DSA_HEADS * TQ), F32)],
        compiler_params=_params("parallel", "arbitrary"),
    )(qT, k, vT, iqT, ik, smT)


def _out_mlp_kernel(yml_ref, ymb_ref, yds_ref, x_ref, wout_ref, gpost_ref, gpre_ref,
                    w1_ref, w2_ref, g2_ref, out_ref, x1_scr, h_scr, acc_scr):
    f = pl.program_id(1)

    @pl.when(f == 0)
    def _():
        mix = jnp.dot(yml_ref[...], wout_ref[0:ML_V_W, :], preferred_element_type=F32)
        mix = mix + jnp.dot(ymb_ref[...], wout_ref[ML_V_W:ML_V_W + MOBA_W, :], preferred_element_type=F32)
        mix = mix + jnp.dot(yds_ref[...], wout_ref[ML_V_W + MOBA_W:, :], preferred_element_type=F32)
        x1 = x_ref[...] + _rms(mix, gpost_ref[...])
        x1_scr[...] = x1
        h_scr[...] = _rms(x1, gpre_ref[...]).astype(BF16)
        acc_scr[...] = jnp.zeros_like(acc_scr)

    u = jnp.maximum(jnp.dot(h_scr[...], w1_ref[...], preferred_element_type=F32), 0.0)
    acc_scr[...] += jnp.dot((u * u).astype(BF16), w2_ref[...], preferred_element_type=F32)

    @pl.when(f == pl.num_programs(1) - 1)
    def _():
        out_ref[...] = x1_scr[...] + _rms(acc_scr[...], g2_ref[...])


def _out_mlp(yml, ymb, yds, x2, w_out, g_post, g_pre, w1, w2, g2):
    T, D = x2.shape
    tm = min(MLP_TM, T)
    tf = min(MLP_TF, D_FF)
    row = lambda n: pl.BlockSpec((tm, n), lambda i, f: (i, 0))
    full = lambda a: pl.BlockSpec(a.shape, lambda i, f: (0,) * a.ndim)
    return pl.pallas_call(
        _out_mlp_kernel,
        grid=(T // tm, D_FF // tf),
        in_specs=[row(ML_V_W), row(MOBA_W), row(DSA_W), row(D), full(w_out), full(g_post), full(g_pre),
                  pl.BlockSpec((D, tf), lambda i, f: (0, f)),
                  pl.BlockSpec((tf, D), lambda i, f: (f, 0)),
                  full(g2)],
        out_specs=row(D),
        out_shape=jax.ShapeDtypeStruct((T, D), F32),
        scratch_shapes=[pltpu.VMEM((tm, D), F32), pltpu.VMEM((tm, D), BF16), pltpu.VMEM((tm, D), F32)],
        compiler_params=_params("parallel", "arbitrary"),
    )(yml, ymb, yds, x2, w_out, g_post, g_pre, w1, w2, g2)


def kernel(x, norm_mix_pre, w_in, ml_conv, ml_i_bias, ml_f_bias, ml_head_norm, w_out, norm_mix_post,
           norm_mlp_pre, w_ff1, w_ff2, norm_mlp_post):
    B, S, D = x.shape
    depth = w_in.shape[0]
    x2 = x.reshape(B * S, D)
    gain = lambda g: g.reshape(1, D).astype(F32)
    for l in range(depth):
        wn, wt = _prep_in_weights(w_in[l])
        (qk, mbk, dsk, ixk, mbqT, mbvT, dsqT, dsvT, ixqT, mlvT, mloT, smT) = _in_proj(
            x2, gain(norm_mix_pre[l]), wn, wt, B, S)
        y_ml = _mlstm(qk, mlvT, mloT, smT, ml_conv[l], ml_i_bias[l], ml_f_bias[l], ml_head_norm[l], B, S)
        y_mb = _moba(mbqT, mbk, mbvT, B, S)
        y_ds = _dsa(dsqT, dsk, dsvT, ixqT, ixk, smT, B, S)
        x2 = _out_mlp(y_ml.reshape(B * S, ML_V_W), y_mb.reshape(B * S, MOBA_W), y_ds.reshape(B * S, DSA_W),
                      x2, w_out[l].astype(BF16), gain(norm_mix_post[l]), gain(norm_mlp_pre[l]),
                      w_ff1[l].astype(BF16), w_ff2[l].astype(BF16), gain(norm_mlp_post[l]))
    return x2.reshape(B, S, D)
```

```python
import functools

import jax
import jax.numpy as jnp
from jax import lax
from jax.experimental import pallas as pl
from jax.experimental.pallas import tpu as pltpu

F32 = jnp.float32
BF16 = jnp.bfloat16
I32 = jnp.int32

D_MODEL = 1024
ML_HEADS, ML_DQK, ML_DV, ML_CONV = 4, 64, 128, 4
MOBA_HEADS, MOBA_DH, MOBA_BLOCK, MOBA_TOPK = 4, 64, 256, 3
DSA_HEADS, DSA_DH, IDX_HEADS, IDX_DH, DSA_TOPK_MAX = 4, 64, 4, 64, 256
D_FF = 4 * D_MODEL
EPS = 1e-6

ML_QK_W = ML_HEADS * ML_DQK
ML_V_W = ML_HEADS * ML_DV
MOBA_W = MOBA_HEADS * MOBA_DH
DSA_W = DSA_HEADS * DSA_DH
IN_SPLITS = (ML_QK_W, ML_QK_W, ML_V_W, ML_V_W, ML_HEADS, ML_HEADS,
             MOBA_W, MOBA_W, MOBA_W,
             DSA_W, DSA_W, DSA_W, IDX_HEADS * IDX_DH, IDX_DH, IDX_HEADS)

LANES = 128
SUBLANES = 8
VMEM_LIMIT = 52 * 1024 * 1024

ML_CHUNK = 256
ATT_TQ = 256
ATT_STAGE = 8
PROJ_TM = 512
MLP_TM = 1024
MLP_TF = 512

NEG_INF = float("-inf")
BISECT_CAP = 300
BISECT_WARMUP = 12
BISECT_DROP = 2
BISECT_STEPS_PER_CHECK = 2
SOFTMAX_L_MIN = 1e-30
assert MOBA_DH == DSA_DH
ATT_QSCALE = MOBA_DH ** -0.5 * 1.4426950408889634

_NT = (((1,), (1,)), ((), ()))
_TN = (((0,), (0,)), ((), ()))


def _rms(x, g):
    return x * lax.rsqrt(jnp.mean(x * x, axis=-1, keepdims=True) + EPS) * g


def _log_sigmoid(x):
    return jnp.minimum(x, 0.0) - jnp.log1p(jnp.exp(-jnp.abs(x)))


def _params(*sem):
    return pltpu.CompilerParams(dimension_semantics=sem, vmem_limit_bytes=VMEM_LIMIT)


_N_QK, _N_MBK, _N_DSK, _N_IXK = 0, 512, 768, 1024
_N_TOTAL = 1152
_T_MBQ, _T_MBV, _T_DSQ, _T_DSV, _T_IXQ, _T_MLV, _T_MLO, _T_SM = 0, 256, 512, 768, 1024, 1280, 1792, 2304
_T_TOTAL = 2320


def _in_proj_kernel(x_ref, g_ref, wn_ref, wt_ref,
                    qk_ref, mbk_ref, dsk_ref, ixk_ref,
                    mbqT_ref, mbvT_ref, dsqT_ref, dsvT_ref, ixqT_ref, mlvT_ref, mloT_ref, smT_ref):
    h = _rms(x_ref[...], g_ref[...]).astype(BF16)

    def mm(a, n):
        return jnp.dot(h, wn_ref[:, a:a + n], preferred_element_type=F32)

    def mt(a, n):
        return lax.dot_general(wt_ref[a:a + n, :], h, _NT, preferred_element_type=F32)

    qk_ref[...] = mm(_N_QK, 512)
    mbk_ref[...] = mm(_N_MBK, 256).astype(BF16)
    dsk_ref[...] = mm(_N_DSK, 256).astype(BF16)
    ixk_ref[...] = mm(_N_IXK, 128).astype(BF16)
    mbqT_ref[0] = (mt(_T_MBQ, 256) * ATT_QSCALE).astype(BF16)
    mbvT_ref[0] = mt(_T_MBV, 256).astype(BF16)
    dsqT_ref[0] = (mt(_T_DSQ, 256) * ATT_QSCALE).astype(BF16)
    dsvT_ref[0] = mt(_T_DSV, 256).astype(BF16)
    ixqT_ref[0] = mt(_T_IXQ, 256).astype(BF16)
    mlvT_ref[0] = mt(_T_MLV, 512).astype(BF16)
    mloT_ref[0] = mt(_T_MLO, 512)
    smT_ref[0] = mt(_T_SM, 16)


def _prep_in_weights(w_in):
    pts, acc = [], 0
    for n in IN_SPLITS:
        pts.append((acc, acc + n))
        acc += n
    col = lambda i: w_in[:, pts[i][0]:pts[i][1]]
    (ml_q, ml_k, ml_v, ml_o, ml_i, ml_f, mb_q, mb_k, mb_v,
     ds_q, ds_k, ds_v, ix_q, ix_k, ix_w) = [col(i) for i in range(len(IN_SPLITS))]
    d = w_in.shape[0]
    small = jnp.concatenate([ml_i, ml_f, ix_w], axis=1)
    wn = jnp.concatenate([
        ml_q, ml_k, mb_k, ds_k,
        ix_k, jnp.zeros((d, 128 - IDX_DH), w_in.dtype)], axis=1)
    wt = jnp.concatenate([
        mb_q, mb_v, ds_q, ds_v, ix_q, ml_v, ml_o,
        small, jnp.zeros((d, 16 - small.shape[1]), w_in.dtype)], axis=1).T
    assert wn.shape == (d, _N_TOTAL) and wt.shape == (_T_TOTAL, d)
    return wn.astype(BF16), wt.astype(BF16)


def _in_proj(x2, g, wn, wt, B, S):
    T, D = x2.shape
    tm = min(PROJ_TM, S)
    nsb = S // tm
    row = lambda n: pl.BlockSpec((tm, n), lambda i: (i, 0))
    tr = lambda n: pl.BlockSpec((1, n, tm), lambda i: (i // nsb, 0, i % nsb))
    full = lambda a: pl.BlockSpec(a.shape, lambda i: (0,) * a.ndim)
    out_shape = (
        jax.ShapeDtypeStruct((T, 512), F32),
        jax.ShapeDtypeStruct((T, 256), BF16),
        jax.ShapeDtypeStruct((T, 256), BF16),
        jax.ShapeDtypeStruct((T, 128), BF16),
        jax.ShapeDtypeStruct((B, 256, S), BF16),
        jax.ShapeDtypeStruct((B, 256, S), BF16),
        jax.ShapeDtypeStruct((B, 256, S), BF16),
        jax.ShapeDtypeStruct((B, 256, S), BF16),
        jax.ShapeDtypeStruct((B, 256, S), BF16),
        jax.ShapeDtypeStruct((B, 512, S), BF16),
        jax.ShapeDtypeStruct((B, 512, S), F32),
        jax.ShapeDtypeStruct((B, 16, S), F32),
    )
    out_specs = (row(512), row(256), row(256), row(128),
                 tr(256), tr(256), tr(256), tr(256), tr(256), tr(512), tr(512), tr(16))
    return pl.pallas_call(
        _in_proj_kernel,
        grid=(T // tm,),
        in_specs=[row(D), full(g), full(wn), full(wt)],
        out_specs=out_specs,
        out_shape=out_shape,
        compiler_params=_params("parallel"),
    )(x2, g, wn, wt)


def _dot3(a_f32, b_exact):
    b = b_exact.astype(BF16)
    hi = a_f32.astype(BF16)
    r1 = a_f32 - hi.astype(F32)
    mid = r1.astype(BF16)
    lo = (r1 - mid.astype(F32)).astype(BF16)
    return (jnp.dot(hi, b, preferred_element_type=F32) + jnp.dot(mid, b, preferred_element_type=F32)
            + jnp.dot(lo, b, preferred_element_type=F32))


def _mlstm_kernel(qk_ref, vT_ref, oT_ref, smT_ref, conv_ref, bcol_ref, hn_ref,
                  y_ref, xp_scr, st_scr, m_scr, *, L):
    c = pl.program_id(1)
    halo = SUBLANES
    DK, DV = ML_DQK, ML_DV

    @pl.when(c == 0)
    def _():
        xp_scr[0:halo, :] = jnp.zeros((halo, 2 * ML_QK_W), F32)
        st_scr[...] = jnp.zeros_like(st_scr)
        m_scr[...] = jnp.zeros_like(m_scr)

    cur = qk_ref[0]
    xp_scr[halo:halo + L, :] = cur
    base = halo - (ML_CONV - 1)
    acc = conv_ref[0:1, :] * xp_scr[base:base + L, :]
    for j in range(1, ML_CONV):
        acc = acc + conv_ref[j:j + 1, :] * xp_scr[base + j:base + j + L, :]
    xp_scr[0:halo, :] = cur[L - halo:L, :]
    qk = acc * jax.nn.sigmoid(acc)
    qT_all = (qk[:, :ML_QK_W] * (DK ** -0.5)).T.astype(BF16)
    k_all = qk[:, ML_QK_W:].astype(BF16)

    grow = smT_ref[0][0:SUBLANES, :] + bcol_ref[...]
    si = lax.broadcasted_iota(I32, (L, L), 0)
    ji = lax.broadcasted_iota(I32, (L, L), 1)
    causal = si <= ji
    b_row = _dot3(_log_sigmoid(grow), causal)

    for h in range(ML_HEADS):
        qT = qT_all[h * DK:(h + 1) * DK, :]
        kh = k_all[:, h * DK:(h + 1) * DK]
        vT = vT_ref[0, h * DV:(h + 1) * DV, :]
        b_j = b_row[ML_HEADS + h:ML_HEADS + h + 1, :]
        c_row = grow[h:h + 1, :] - b_j
        c_col = jnp.broadcast_to(c_row, (LANES, L)).T
        dlog = jnp.where(causal, b_j + jnp.concatenate([c_col] * (L // LANES), axis=1), NEG_INF)
        m0 = m_scr[h:h + 1, 0:1]
        inter = b_j + m0
        m_t = jnp.maximum(inter, jnp.max(dlog, axis=0, keepdims=True))
        w_inter = jnp.exp(inter - m_t)
        w_intra = jnp.exp(dlog - m_t) * jnp.dot(kh, qT, preferred_element_type=F32)
        st = st_scr[h]
        cq = jnp.dot(st.astype(BF16), qT, preferred_element_type=F32)
        num = w_inter * cq[:DV] + jnp.dot(vT, w_intra.astype(BF16), preferred_element_type=F32)
        den = w_inter * cq[DV:DV + 1] + jnp.sum(w_intra, axis=0, keepdims=True)
        hh = num / jnp.maximum(jnp.abs(den), jnp.exp(-m_t))
        hh = hh * lax.rsqrt(jnp.mean(hh * hh, axis=0, keepdims=True) + EPS)
        rows = slice(h * DV, (h + 1) * DV)
        y_ref[0, :, rows] = (jax.nn.sigmoid(oT_ref[0, rows, :]) * hh * hn_ref[rows, :]).T.astype(y_ref.dtype)

        b_last = b_j[:, L - 1:L]
        a = b_last + c_row
        m_loc = jnp.max(a, axis=1, keepdims=True)
        wa = jnp.exp(a - m_loc)
        lhs = jnp.concatenate([(vT.astype(F32) * wa).astype(BF16),
                               jnp.broadcast_to(wa, (SUBLANES, L)).astype(BF16)], axis=0)
        c_loc = jnp.dot(lhs, kh, preferred_element_type=F32)
        m_new = jnp.maximum(b_last + m0, m_loc)
        s_old = jnp.exp(b_last + m0 - m_new)
        s_loc = jnp.exp(m_loc - m_new)
        st_scr[h] = s_old * st + s_loc * c_loc
        m_scr[h:h + 1, :] = jnp.broadcast_to(m_new, (1, LANES))


def _mlstm(qk, vT, oT, smT, conv_w, i_bias, f_bias, head_norm, B, S):
    L = min(ML_CHUNK, S)
    assert L % LANES == 0
    nc = S // L
    qk = qk.reshape(B, S, 2 * ML_QK_W)
    bcol = jnp.concatenate([i_bias, f_bias]).astype(F32).reshape(2 * ML_HEADS, 1)
    hn = jnp.broadcast_to(head_norm.astype(F32)[:, None], (ML_V_W, L))
    blkT = lambda n: pl.BlockSpec((1, n, L), lambda b, c: (b, 0, c))
    full = lambda a: pl.BlockSpec(a.shape, lambda b, c: (0,) * a.ndim)
    return pl.pallas_call(
        functools.partial(_mlstm_kernel, L=L),
        grid=(B, nc),
        in_specs=[pl.BlockSpec((1, L, 2 * ML_QK_W), lambda b, c: (b, c, 0)),
                  blkT(ML_V_W), blkT(ML_V_W), blkT(16), full(conv_w), full(bcol), full(hn)],
        out_specs=pl.BlockSpec((1, L, ML_V_W), lambda b, c: (b, c, 0)),
        out_shape=jax.ShapeDtypeStruct((B, S, ML_V_W), BF16),
        scratch_shapes=[pltpu.VMEM((L + SUBLANES, 2 * ML_QK_W), F32),
                        pltpu.VMEM((ML_HEADS, ML_DV + SUBLANES, ML_DQK), F32),
                        pltpu.VMEM((SUBLANES, LANES), F32)],
        compiler_params=_params("parallel", "arbitrary"),
    )(qk, vT, oT, smT, conv_w.astype(F32), bcol, hn)


def _block_diag_T(qT, heads, dh):
    rid = lax.broadcasted_iota(I32, qT.shape, 0)
    zero = jnp.zeros_like(qT)
    return jnp.concatenate(
        [jnp.where((rid >= h * dh) & (rid < (h + 1) * dh), qT, zero) for h in range(heads)], axis=1)


def _scores(k_chunk, qs, h, TQ):
    return jnp.dot(k_chunk, qs[:, h * TQ:(h + 1) * TQ], preferred_element_type=F32)


def _accumulate(h, p, vT_chunk, l, acc_scr, dh):
    rows = slice(h * dh, (h + 1) * dh)
    acc_scr[rows, :] += jnp.dot(vT_chunk[rows, :], p.astype(BF16), preferred_element_type=F32)
    return l + jnp.sum(p, axis=0, keepdims=True)


def _finish_attention(l, acc_scr, y_ref, heads, dh):
    outs = [acc_scr[h * dh:(h + 1) * dh, :] / l[h] for h in range(heads)]
    y_ref[0] = jnp.concatenate(outs, axis=0).T.astype(y_ref.dtype)


def _max_key_norm2(kb, heads, dh, TQ):
    ch = lax.broadcasted_iota(I32, (heads * dh, LANES), 0) // dh
    hd = lax.broadcasted_iota(I32, (heads * dh, LANES), 1)
    n2 = jnp.dot(kb * kb, (ch == hd).astype(F32), precision=lax.Precision.HIGHEST, preferred_element_type=F32)
    n2 = jnp.max(n2, axis=0, keepdims=True)
    return jnp.concatenate([jnp.broadcast_to(n2[:, h:h + 1], (1, TQ)) for h in range(heads)], axis=1)


def _norm_bound(qTf, k2, heads, dh, TQ):
    out = []
    for h in range(heads):
        qh = qTf[h * dh:(h + 1) * dh, :]
        q2 = jnp.sum(qh * qh, axis=0, keepdims=True)
        out.append(jnp.sqrt(q2 * k2[:, h * TQ:(h + 1) * TQ]) * (1.0 + 2.0 ** -8))
    return out


def _underflowed(l):
    small = l[0]
    for lh in l[1:]:
        small = jnp.minimum(small, lh)
    return jnp.logical_not(jnp.min(small) >= SOFTMAX_L_MIN)


def _moba_kernel(qT_ref, k_ref, vT_ref, y_ref, km_scr, kn_scr, acc_scr, s_scr, *, NB, NBP, n_sel):
    H, DH, BS = MOBA_HEADS, MOBA_DH, MOBA_BLOCK
    TQ = BS
    qi = pl.program_id(1)

    @pl.when(qi == 0)
    def _():
        km_scr[...] = jnp.zeros_like(km_scr)
        kn_scr[...] = jnp.zeros_like(kn_scr)

        def body(j, carry):
            kb = k_ref[0, pl.ds(pl.multiple_of(j * BS, BS), BS), :].astype(F32)
            km_scr[pl.ds(j, 1), :] = jnp.mean(kb, axis=0, keepdims=True)
            kn_scr[pl.ds(j, 1), :] = _max_key_norm2(kb, H, DH, TQ)
            return carry
        lax.fori_loop(0, NB, body, 0)

    qT = qT_ref[0]
    qTf = qT.astype(F32)
    km = km_scr[...]
    gate = jnp.concatenate(
        [jnp.dot(km[:, h * DH:(h + 1) * DH], qTf[h * DH:(h + 1) * DH, :], precision=lax.Precision.HIGHEST,
                 preferred_element_type=F32) for h in range(H)], axis=1) * (1.0 / ATT_QSCALE)
    blk = lax.broadcasted_iota(I32, gate.shape, 0)
    g = jnp.where(blk < qi, gate, NEG_INF)
    sels = []
    for _ in range(n_sel):
        mx = jnp.max(g, axis=0, keepdims=True)
        isel = jnp.min(jnp.where(g == mx, blk, NBP), axis=0, keepdims=True)
        sels.append(jnp.where(mx > NEG_INF, isel, -1))
        g = jnp.where(blk == isel, NEG_INF, g)

    qs = _block_diag_T(qT, H, DH)

    def hit_row(j):
        hit = sels[0] == j
        for s in sels[1:]:
            hit = hit | (s == j)
        return hit

    npair = (qi + 1) // 2
    blk_start = lambda j: pl.multiple_of(j * BS, BS)
    blk_scores = lambda j: jnp.dot(k_ref[0, pl.ds(blk_start(j), BS), :], qs, preferred_element_type=F32)
    hcols = lambda row, h: row[:, h * TQ:(h + 1) * TQ]
    causal = lax.broadcasted_iota(I32, (BS, TQ), 0) <= lax.broadcasted_iota(I32, (BS, TQ), 1)

    def diag_scores():
        s_all = blk_scores(qi)
        return [jnp.where(causal, hcols(s_all, h), NEG_INF) for h in range(H)]

    def max_sweep():
        def max_pair(g, m):
            for u in range(2):
                j = 2 * g + u
                m = jnp.maximum(m, jnp.where(hit_row(j), jnp.max(blk_scores(j), axis=0, keepdims=True), NEG_INF))
            return m
        m_all = lax.fori_loop(0, npair, max_pair, jnp.full((1, H * TQ), NEG_INF, F32))
        s_diag = diag_scores()
        return [jnp.maximum(hcols(m_all, h), jnp.max(s_diag[h], axis=0, keepdims=True)) for h in range(H)]

    def sum_sweep(m):
        acc_scr[...] = jnp.zeros_like(acc_scr)

        def sum_group(width):
            def body(g, l):
                for u in range(width):
                    s_scr[u] = blk_scores(width * g + u)
                for u in range(width):
                    j = width * g + u
                    hit = hit_row(j)
                    vT_blk = vT_ref[0, :, pl.ds(blk_start(j), BS)]
                    l = tuple(_accumulate(h, jnp.exp2(s_scr[u, :, h * TQ:(h + 1) * TQ]
                                                      - jnp.where(hcols(hit, h), m[h], jnp.inf)),
                                          vT_blk, l[h], acc_scr, DH) for h in range(H))
                return l
            return body
        nwide = qi // ATT_STAGE
        l = lax.fori_loop(0, nwide, sum_group(ATT_STAGE), (jnp.zeros((1, TQ), F32),) * H)
        l = lax.fori_loop((ATT_STAGE // 2) * nwide, npair, sum_group(2), l)
        s_diag = diag_scores()
        vT_diag = vT_ref[0, :, pl.ds(blk_start(qi), BS)]
        return [_accumulate(h, jnp.exp2(s_diag[h] - m[h]), vT_diag, l[h], acc_scr, DH) for h in range(H)]

    blk_n = lax.broadcasted_iota(I32, kn_scr.shape, 0)
    k2 = jnp.max(jnp.where(blk_n <= qi, kn_scr[...], 0.0), axis=0, keepdims=True)
    l = sum_sweep(_norm_bound(qTf, k2, H, DH, TQ))
    _finish_attention(l, acc_scr, y_ref, H, DH)

    @pl.when(_underflowed(l))
    def _():
        _finish_attention(sum_sweep(max_sweep()), acc_scr, y_ref, H, DH)


def _moba(qT, k, vT, B, S):
    BS = MOBA_BLOCK
    assert S % BS == 0
    NB = S // BS
    NBP = -(-NB // SUBLANES) * SUBLANES
    n_sel = max(min(MOBA_TOPK, NB - 1), 1)
    k = k.reshape(B, S, MOBA_W)
    return pl.pallas_call(
        functools.partial(_moba_kernel, NB=NB, NBP=NBP, n_sel=n_sel),
        grid=(B, NB),
        in_specs=[pl.BlockSpec((1, MOBA_W, BS), lambda b, i: (b, 0, i)),
                  pl.BlockSpec((1, S, MOBA_W), lambda b, i: (b, 0, 0)),
                  pl.BlockSpec((1, MOBA_W, S), lambda b, i: (b, 0, 0))],
        out_specs=pl.BlockSpec((1, BS, MOBA_W), lambda b, i: (b, i, 0)),
        out_shape=jax.ShapeDtypeStruct((B, S, MOBA_W), BF16),
        scratch_shapes=[pltpu.VMEM((NBP, MOBA_W), F32),
                        pltpu.VMEM((NBP, MOBA_HEADS * BS), F32),
                        pltpu.VMEM((MOBA_W, BS), F32),
                        pltpu.VMEM((ATT_STAGE, BS, MOBA_HEADS * BS), F32)],
        compiler_params=_params("parallel", "arbitrary"),
    )(qT, k, vT)


def _dsa_kernel(qT_ref, k_ref, vT_ref, iqT_ref, ik_ref, smT_ref, y_ref,
                e_scr, gmax_scr, acc_scr, thr_scr, kn_scr, s_scr, *, TQ, topk, spow, n_chunks):
    H, DH = DSA_HEADS, DSA_DH
    KC = TQ
    qi = pl.program_id(1)
    nch = qi + 1
    npair = (nch + 1) // 2
    chunk_start = lambda c: pl.multiple_of(c * KC, KC)

    @pl.when(qi == 0)
    def _():
        kn_scr[...] = jnp.zeros_like(kn_scr)

        def body(c, carry):
            kn_scr[pl.ds(c, 1), :] = _max_key_norm2(k_ref[0, pl.ds(chunk_start(c), KC), :].astype(F32), H, DH, TQ)
            return carry
        lax.fori_loop(0, n_chunks, body, 0)
    sum8 = lambda b: jnp.sum(b.astype(I32).reshape(KC // SUBLANES, SUBLANES, TQ), axis=0)
    any_row = lambda r: jnp.max(r.astype(I32)) > 0

    iqT = iqT_ref[0]
    iq_cat = jnp.concatenate([iqT[h * IDX_DH:(h + 1) * IDX_DH, :] for h in range(IDX_HEADS)], axis=1)
    iw = smT_ref[0][2 * ML_HEADS:2 * ML_HEADS + IDX_HEADS, :] * (IDX_HEADS ** -0.5 * IDX_DH ** -0.5)
    t_pos = qi * TQ + lax.broadcasted_iota(I32, (KC, TQ), 1)
    s_off = lax.broadcasted_iota(I32, (KC, TQ), 0)
    gmax_scr[...] = jnp.full((KC, TQ), NEG_INF, F32)

    def score_group(width, masked):
        def body(g, tally):
            for u in range(width):
                ikc = ik_ref[0, pl.ds(chunk_start(width * g + u), KC), :][:, :IDX_DH]
                s_scr[u] = jnp.dot(ikc, iq_cat, preferred_element_type=F32)
            for u in range(width):
                c = width * g + u
                sc = iw[0:1, :] * jnp.maximum(s_scr[u, :, 0:TQ], 0.0)
                for h in range(1, IDX_HEADS):
                    sc = sc + iw[h:h + 1, :] * jnp.maximum(s_scr[u, :, h * TQ:(h + 1) * TQ], 0.0)
                if masked:
                    sc = jnp.where(c * KC + s_off <= t_pos, sc, NEG_INF)
                e_scr[pl.ds(chunk_start(c), KC), :] = sc
                gmax_scr[...] = jnp.maximum(gmax_scr[...], sc)
                tally = tally + jnp.sum(jnp.where(sc > 0.0, 1, jnp.where(sc == 0.0, 1 << 16, 0))
                                        .reshape(KC // SUBLANES, SUBLANES, TQ), axis=0)
            return tally
        return body
    assert n_chunks * KC < (1 << 16)
    zero8 = jnp.zeros((SUBLANES, TQ), I32)
    tally = lax.fori_loop(0, (npair - 1) // 2, score_group(4, False), zero8)
    tally = lax.fori_loop(2 * ((npair - 1) // 2), npair - 1, score_group(2, False), tally)
    tally = jnp.sum(score_group(2, True)(npair - 1, tally), axis=0, keepdims=True)
    n_pos = tally & 0xFFFF
    n_nonneg = n_pos + (tally >> 16)

    def count_gt(cand):
        def body(g, acc):
            a, b = acc
            a = a + sum8(e_scr[pl.ds(chunk_start(2 * g), KC), :] > cand)
            b = b + sum8(e_scr[pl.ds(chunk_start(2 * g + 1), KC), :] > cand)
            return a, b
        a, b = lax.fori_loop(0, npair, body, (zero8, zero8))
        return jnp.sum(a + b, axis=0, keepdims=True)

    def min_gt(cand):
        fold = lambda x: jnp.min(jnp.where(x > cand, x, jnp.inf).reshape(KC // SUBLANES, SUBLANES, TQ), axis=0)

        def body(g, acc):
            a, b = acc
            a = jnp.minimum(a, fold(e_scr[pl.ds(chunk_start(2 * g), KC), :]))
            b = jnp.minimum(b, fold(e_scr[pl.ds(chunk_start(2 * g + 1), KC), :]))
            return a, b
        inf8 = jnp.full((SUBLANES, TQ), jnp.inf, F32)
        a, b = lax.fori_loop(0, npair, body, (inf8, inf8))
        return jnp.min(jnp.minimum(a, b), axis=0, keepdims=True)

    def rewrite(rows, thr):
        def body(c, carry):
            x = e_scr[pl.ds(chunk_start(c), KC), :]
            rank = (2 * spow - (c * KC + s_off)).astype(F32)
            new = jnp.where(x > thr, jnp.inf, jnp.where(x == thr, rank, NEG_INF))
            e_scr[pl.ds(chunk_start(c), KC), :] = jnp.where(rows, new, x)
            return carry
        lax.fori_loop(0, nch, body, 0)

    def halve(_, c):
        lo, hi, act, tie, n_lo = c
        mid = lo + (hi - lo) * 0.5
        inside = (mid > lo) & (mid < hi)
        cnt = count_gt(mid)
        live = act > 0.0
        run = live & inside
        up = run & (cnt >= topk)
        lo = jnp.where(up, mid, lo)
        n_lo = jnp.where(up, cnt, n_lo)
        hi = jnp.where(run & (cnt < topk), mid, hi)
        collapsed = live & jnp.logical_not(inside)
        tie = jnp.where(collapsed, 1.0, tie)
        act = jnp.where(collapsed | (run & (cnt == topk)), 0.0, act)
        return lo, hi, act, tie, n_lo

    def excess(state):
        return jnp.max(jnp.where(state[2] > 0.0, state[4] - topk, 0).astype(F32))

    def bisect(lo, hi, act, warmup):
        state = (lo, hi, act, jnp.zeros_like(act), jnp.full((1, TQ), 1 << 20, I32))
        state = lax.fori_loop(0, warmup, halve, state)

        def one_more(c):
            state = halve(0, c[1])
            return c[0] + 1, state, excess(state)
        near = lax.while_loop(lambda c: jnp.logical_and(c[0] < BISECT_CAP, c[2] > BISECT_DROP), one_more,
                              (jnp.int32(warmup), state, excess(state)))
        lo, hi, act, tie, n_lo = near[1]
        cur, left = lo, jnp.where(act > 0.0, n_lo - topk, 0)
        for _ in range(BISECT_DROP):
            cur = jnp.where(left > 0, min_gt(cur), cur)
            left = left - 1
        done = (act > 0.0) & (count_gt(cur) == topk)
        lo = jnp.where(done, cur, lo)
        act = jnp.where(done, 0.0, act)

        def body(c):
            state = c[1]
            for _ in range(BISECT_STEPS_PER_CHECK):
                state = halve(0, state)
            return c[0] + BISECT_STEPS_PER_CHECK, state, jnp.max(state[2])
        out = lax.while_loop(lambda c: jnp.logical_and(c[0] < BISECT_CAP, c[2] > 0.0), body,
                             (near[0], (lo, hi, act, tie, n_lo), jnp.max(act)))
        return out[1][0], out[1][1], out[1][3]

    fmax = float(jnp.finfo(F32).max)
    gm = gmax_scr[...]
    g_lo = jnp.min(gm, axis=0, keepdims=True)
    g_hi = jnp.max(gm, axis=0, keepdims=True)
    n_vis = qi * TQ + lax.broadcasted_iota(I32, (1, TQ), 1) + 1
    few = n_vis <= topk
    zero_tie = (n_pos < topk) & (n_nonneg >= topk) & jnp.logical_not(few)

    @pl.when(any_row(zero_tie))
    def _():
        rewrite(zero_tie, jnp.zeros((1, TQ), F32))

    rank_lo, rank_hi = float(spow), float(2 * spow + 1)
    below = jnp.maximum(g_lo, -fmax)
    below = below - jnp.abs(below) * (2.0 ** -10) - 1e-30
    lo0 = jnp.where(zero_tie, rank_lo, jnp.maximum(below, -fmax))
    hi0 = jnp.where(zero_tie, rank_hi, g_hi)
    lo, hi, tie = bisect(lo0, hi0, jnp.where(few, 0.0, 1.0), BISECT_WARMUP)
    thr_scr[0:1, :] = jnp.where(few, NEG_INF, lo)

    @pl.when(jnp.max(tie) > 0.0)
    def _():
        rows = tie > 0.0
        rewrite(rows, hi)
        lo2, _, _ = bisect(jnp.full((1, TQ), rank_lo, F32), jnp.full((1, TQ), rank_hi, F32), tie, 0)
        thr_scr[0:1, :] = jnp.where(rows, lo2, thr_scr[0:1, :])

    thr = thr_scr[0:1, :]
    qT = qT_ref[0]
    qs = _block_diag_T(qT, H, DH)
    hcols = lambda row, h: row[:, h * TQ:(h + 1) * TQ]

    def chunk_scores(c):
        start = chunk_start(c)
        sel = e_scr[pl.ds(start, KC), :] > thr
        s_all = jnp.dot(k_ref[0, pl.ds(start, KC), :], qs, preferred_element_type=F32)
        return sel, s_all

    def max_sweep():
        def max_pair(g, m):
            for u in range(2):
                sel, s_all = chunk_scores(2 * g + u)
                m = tuple(jnp.maximum(m[h], jnp.max(jnp.where(sel, hcols(s_all, h), NEG_INF), axis=0, keepdims=True))
                          for h in range(H))
            return m
        m = lax.fori_loop(0, npair, max_pair, (jnp.full((1, TQ), NEG_INF, F32),) * H)
        return [jnp.where(mh == NEG_INF, 0.0, mh) for mh in m]

    def sum_sweep(m):
        acc_scr[...] = jnp.zeros_like(acc_scr)

        def sum_group(width):
            def body(g, l):
                for u in range(width):
                    s_scr[u] = jnp.dot(k_ref[0, pl.ds(chunk_start(width * g + u), KC), :], qs,
                                       preferred_element_type=F32)
                for u in range(width):
                    start = chunk_start(width * g + u)
                    sel = e_scr[pl.ds(start, KC), :] > thr
                    vT_chunk = vT_ref[0, :, pl.ds(start, KC)]
                    l = tuple(_accumulate(h, jnp.exp2(s_scr[u, :, h * TQ:(h + 1) * TQ] + jnp.where(sel, -m[h], NEG_INF)),
                                          vT_chunk, l[h], acc_scr, DH) for h in range(H))
                return l
            return body
        nwide = (2 * npair) // ATT_STAGE
        l = lax.fori_loop(0, nwide, sum_group(ATT_STAGE), (jnp.zeros((1, TQ), F32),) * H)
        return lax.fori_loop((ATT_STAGE // 2) * nwide, npair, sum_group(2), l)

    chunk_n = lax.broadcasted_iota(I32, kn_scr.shape, 0)
    k2 = jnp.max(jnp.where(chunk_n < 2 * npair, kn_scr[...], 0.0), axis=0, keepdims=True)
    l = sum_sweep(_norm_bound(qT.astype(F32), k2, H, DH, TQ))
    _finish_attention(l, acc_scr, y_ref, H, DH)

    @pl.when(_underflowed(l))
    def _():
        _finish_attention(sum_sweep(max_sweep()), acc_scr, y_ref, H, DH)


def _dsa(qT, k, vT, iqT, ik, smT, B, S):
    TQ = min(ATT_TQ, S)
    topk = min(DSA_TOPK_MAX, S // 4)
    assert S % (2 * TQ) == 0 and topk <= TQ
    spow = 1 << max((S - 1).bit_length(), 1)
    k = k.reshape(B, S, DSA_W)
    ik = ik.reshape(B, S, LANES)
    qblk = lambda n: pl.BlockSpec((1, n, TQ), lambda b, i: (b, 0, i))
    return pl.pallas_call(
        functools.partial(_dsa_kernel, TQ=TQ, topk=topk, spow=spow, n_chunks=S // TQ),
        grid=(B, S // TQ),
        in_specs=[qblk(DSA_W),
                  pl.BlockSpec((1, S, DSA_W), lambda b, i: (b, 0, 0)),
                  pl.BlockSpec((1, DSA_W, S), lambda b, i: (b, 0, 0)),
                  qblk(IDX_HEADS * IDX_DH),
                  pl.BlockSpec((1, S, LANES), lambda b, i: (b, 0, 0)),
                  qblk(16)],
        out_specs=pl.BlockSpec((1, TQ, DSA_W), lambda b, i: (b, i, 0)),
        out_shape=jax.ShapeDtypeStruct((B, S, DSA_W), BF16),
        scratch_shapes=[pltpu.VMEM((S, TQ), F32),
                        pltpu.VMEM((TQ, TQ), F32),
                        pltpu.VMEM((DSA_W, TQ), F32),
                        pltpu.VMEM((SUBLANES, TQ), F32),
                        pltpu.VMEM((-(-(S // TQ) // SUBLANES) * SUBLANES, DSA_HEADS * TQ), F32),
                        pltpu.VMEM((ATT_STAGE, TQ, DSA_HEADS * TQ), F32)],
        compiler_params=_params("parallel", "arbitrary"),
    )(qT, k, vT, iqT, ik, smT)


def _out_mlp_kernel(yml_ref, ymb_ref, yds_ref, x_ref, wout_ref, gpost_ref, gpre_ref,
                    w1_ref, w2_ref, g2_ref, out_ref, x1_scr, h_scr, acc_scr):
    f = pl.program_id(1)

    @pl.when(f == 0)
    def _():
        mix = jnp.dot(yml_ref[...], wout_ref[0:ML_V_W, :], preferred_element_type=F32)
        mix = mix + jnp.dot(ymb_ref[...], wout_ref[ML_V_W:ML_V_W + MOBA_W, :], preferred_element_type=F32)
        mix = mix + jnp.dot(yds_ref[...], wout_ref[ML_V_W + MOBA_W:, :], preferred_element_type=F32)
        x1 = x_ref[...] + _rms(mix, gpost_ref[...])
        x1_scr[...] = x1
        h_scr[...] = _rms(x1, gpre_ref[...]).astype(BF16)
        acc_scr[...] = jnp.zeros_like(acc_scr)

    u = jnp.maximum(jnp.dot(h_scr[...], w1_ref[...], preferred_element_type=F32), 0.0)
    acc_scr[...] += jnp.dot((u * u).astype(BF16), w2_ref[...], preferred_element_type=F32)

    @pl.when(f == pl.num_programs(1) - 1)
    def _():
        out_ref[...] = x1_scr[...] + _rms(acc_scr[...], g2_ref[...])


def _out_mlp(yml, ymb, yds, x2, w_out, g_post, g_pre, w1, w2, g2):
    T, D = x2.shape
    tm = min(MLP_TM, T)
    tf = min(MLP_TF, D_FF)
    row = lambda n: pl.BlockSpec((tm, n), lambda i, f: (i, 0))
    full = lambda a: pl.BlockSpec(a.shape, lambda i, f: (0,) * a.ndim)
    return pl.pallas_call(
        _out_mlp_kernel,
        grid=(T // tm, D_FF // tf),
        in_specs=[row(ML_V_W), row(MOBA_W), row(DSA_W), row(D), full(w_out), full(g_post), full(g_pre),
                  pl.BlockSpec((D, tf), lambda i, f: (0, f)),
                  pl.BlockSpec((tf, D), lambda i, f: (f, 0)),
                  full(g2)],
        out_specs=row(D),
        out_shape=jax.ShapeDtypeStruct((T, D), F32),
        scratch_shapes=[pltpu.VMEM((tm, D), F32), pltpu.VMEM((tm, D), BF16), pltpu.VMEM((tm, D), F32)],
        compiler_params=_params("parallel", "arbitrary"),
    )(yml, ymb, yds, x2, w_out, g_post, g_pre, w1, w2, g2)


def kernel(x, norm_mix_pre, w_in, ml_conv, ml_i_bias, ml_f_bias, ml_head_norm, w_out, norm_mix_post,
           norm_mlp_pre, w_ff1, w_ff2, norm_mlp_post):
    B, S, D = x.shape
    depth = w_in.shape[0]
    x2 = x.reshape(B * S, D)
    gain = lambda g: g.reshape(1, D).astype(F32)
    for l in range(depth):
        wn, wt = _prep_in_weights(w_in[l])
        (qk, mbk, dsk, ixk, mbqT, mbvT, dsqT, dsvT, ixqT, mlvT, mloT, smT) = _in_proj(
            x2, gain(norm_mix_pre[l]), wn, wt, B, S)
        y_ml = _mlstm(qk, mlvT, mloT, smT, ml_conv[l], ml_i_bias[l], ml_f_bias[l], ml_head_norm[l], B, S)
        y_mb = _moba(mbqT, mbk, mbvT, B, S)
        y_ds = _dsa(dsqT, dsk, dsvT, ixqT, ixk, smT, B, S)
        x2 = _out_mlp(y_ml.reshape(B * S, ML_V_W), y_mb.reshape(B * S, MOBA_W), y_ds.reshape(B * S, DSA_W),
                      x2, w_out[l].astype(BF16), gain(norm_mix_post[l]), gain(norm_mlp_pre[l]),
                      w_ff1[l].astype(BF16), w_ff2[l].astype(BF16), gain(norm_mlp_post[l]))
    return x2.reshape(B, S, D)
```

```python
import functools

import jax
import jax.numpy as jnp
from jax import lax
from jax.experimental import pallas as pl
from jax.experimental.pallas import tpu as pltpu

F32 = jnp.float32
BF16 = jnp.bfloat16
I32 = jnp.int32

D_MODEL = 1024
ML_HEADS, ML_DQK, ML_DV, ML_CONV = 4, 64, 128, 4
MOBA_HEADS, MOBA_DH, MOBA_BLOCK, MOBA_TOPK = 4, 64, 256, 3
DSA_HEADS, DSA_DH, IDX_HEADS, IDX_DH, DSA_TOPK_MAX = 4, 64, 4, 64, 256
D_FF = 4 * D_MODEL
EPS = 1e-6

ML_QK_W = ML_HEADS * ML_DQK
ML_V_W = ML_HEADS * ML_DV
MOBA_W = MOBA_HEADS * MOBA_DH
DSA_W = DSA_HEADS * DSA_DH
IN_SPLITS = (ML_QK_W, ML_QK_W, ML_V_W, ML_V_W, ML_HEADS, ML_HEADS,
             MOBA_W, MOBA_W, MOBA_W,
             DSA_W, DSA_W, DSA_W, IDX_HEADS * IDX_DH, IDX_DH, IDX_HEADS)

LANES = 128
SUBLANES = 8
VMEM_LIMIT = 52 * 1024 * 1024

ML_CHUNK = 256
ATT_TQ = 256
ATT_STAGE = 8
PROJ_TM = 512
MLP_TM = 1024
MLP_TF = 512

NEG_INF = float("-inf")
BISECT_CAP = 300
BISECT_WARMUP = 12
BISECT_DROP = 2
BISECT_STEPS_PER_CHECK = 2
SOFTMAX_L_MIN = 1e-30
assert MOBA_DH == DSA_DH
ATT_QSCALE = MOBA_DH ** -0.5 * 1.4426950408889634

_NT = (((1,), (1,)), ((), ()))
_TN = (((0,), (0,)), ((), ()))


def _rms(x, g):
    return x * lax.rsqrt(jnp.mean(x * x, axis=-1, keepdims=True) + EPS) * g


def _log_sigmoid(x):
    return jnp.minimum(x, 0.0) - jnp.log1p(jnp.exp(-jnp.abs(x)))


def _params(*sem):
    return pltpu.CompilerParams(dimension_semantics=sem, vmem_limit_bytes=VMEM_LIMIT)


_N_QK, _N_MBK, _N_DSK, _N_IXK = 0, 512, 768, 1024
_N_TOTAL = 1152
_T_MBQ, _T_MBV, _T_DSQ, _T_DSV, _T_IXQ, _T_MLV, _T_MLO, _T_SM = 0, 256, 512, 768, 1024, 1280, 1792, 2304
_T_TOTAL = 2320


def _in_proj_kernel(x_ref, g_ref, wn_ref, wt_ref,
                    qk_ref, mbk_ref, dsk_ref, ixk_ref,
                    mbqT_ref, mbvT_ref, dsqT_ref, dsvT_ref, ixqT_ref, mlvT_ref, mloT_ref, smT_ref):
    h = _rms(x_ref[...], g_ref[...]).astype(BF16)

    def mm(a, n):
        return jnp.dot(h, wn_ref[:, a:a + n], preferred_element_type=F32)

    def mt(a, n):
        return lax.dot_general(wt_ref[a:a + n, :], h, _NT, preferred_element_type=F32)

    qk_ref[...] = mm(_N_QK, 512)
    mbk_ref[...] = mm(_N_MBK, 256).astype(BF16)
    dsk_ref[...] = mm(_N_DSK, 256).astype(BF16)
    ixk_ref[...] = mm(_N_IXK, 128).astype(BF16)
    mbqT_ref[0] = (mt(_T_MBQ, 256) * ATT_QSCALE).astype(BF16)
    mbvT_ref[0] = mt(_T_MBV, 256).astype(BF16)
    dsqT_ref[0] = (mt(_T_DSQ, 256) * ATT_QSCALE).astype(BF16)
    dsvT_ref[0] = mt(_T_DSV, 256).astype(BF16)
    ixqT_ref[0] = mt(_T_IXQ, 256).astype(BF16)
    mlvT_ref[0] = mt(_T_MLV, 512).astype(BF16)
    mloT_ref[0] = mt(_T_MLO, 512)
    smT_ref[0] = mt(_T_SM, 16)


def _prep_in_weights(w_in):
    pts, acc = [], 0
    for n in IN_SPLITS:
        pts.append((acc, acc + n))
        acc += n
    col = lambda i: w_in[:, pts[i][0]:pts[i][1]]
    (ml_q, ml_k, ml_v, ml_o, ml_i, ml_f, mb_q, mb_k, mb_v,
     ds_q, ds_k, ds_v, ix_q, ix_k, ix_w) = [col(i) for i in range(len(IN_SPLITS))]
    d = w_in.shape[0]
    small = jnp.concatenate([ml_i, ml_f, ix_w], axis=1)
    wn = jnp.concatenate([
        ml_q, ml_k, mb_k, ds_k,
        ix_k, jnp.zeros((d, 128 - IDX_DH), w_in.dtype)], axis=1)
    wt = jnp.concatenate([
        mb_q, mb_v, ds_q, ds_v, ix_q, ml_v, ml_o,
        small, jnp.zeros((d, 16 - small.shape[1]), w_in.dtype)], axis=1).T
    assert wn.shape == (d, _N_TOTAL) and wt.shape == (_T_TOTAL, d)
    return wn.astype(BF16), wt.astype(BF16)


def _in_proj(x2, g, wn, wt, B, S):
    T, D = x2.shape
    tm = min(PROJ_TM, S)
    nsb = S // tm
    row = lambda n: pl.BlockSpec((tm, n), lambda i: (i, 0))
    tr = lambda n: pl.BlockSpec((1, n, tm), lambda i: (i // nsb, 0, i % nsb))
    full = lambda a: pl.BlockSpec(a.shape, lambda i: (0,) * a.ndim)
    out_shape = (
        jax.ShapeDtypeStruct((T, 512), F32),
        jax.ShapeDtypeStruct((T, 256), BF16),
        jax.ShapeDtypeStruct((T, 256), BF16),
        jax.ShapeDtypeStruct((T, 128), BF16),
        jax.ShapeDtypeStruct((B, 256, S), BF16),
        jax.ShapeDtypeStruct((B, 256, S), BF16),
        jax.ShapeDtypeStruct((B, 256, S), BF16),
        jax.ShapeDtypeStruct((B, 256, S), BF16),
        jax.ShapeDtypeStruct((B, 256, S), BF16),
        jax.ShapeDtypeStruct((B, 512, S), BF16),
        jax.ShapeDtypeStruct((B, 512, S), F32),
        jax.ShapeDtypeStruct((B, 16, S), F32),
    )
    out_specs = (row(512), row(256), row(256), row(128),
                 tr(256), tr(256), tr(256), tr(256), tr(256), tr(512), tr(512), tr(16))
    return pl.pallas_call(
        _in_proj_kernel,
        grid=(T // tm,),
        in_specs=[row(D), full(g), full(wn), full(wt)],
        out_specs=out_specs,
        out_shape=out_shape,
        compiler_params=_params("parallel"),
    )(x2, g, wn, wt)


def _dot3(a_f32, b_exact):
    b = b_exact.astype(BF16)
    hi = a_f32.astype(BF16)
    r1 = a_f32 - hi.astype(F32)
    mid = r1.astype(BF16)
    lo = (r1 - mid.astype(F32)).astype(BF16)
    return (jnp.dot(hi, b, preferred_element_type=F32) + jnp.dot(mid, b, preferred_element_type=F32)
            + jnp.dot(lo, b, preferred_element_type=F32))


def _mlstm_kernel(qk_ref, vT_ref, oT_ref, smT_ref, conv_ref, bcol_ref, hn_ref,
                  y_ref, xp_scr, st_scr, m_scr, *, L):
    c = pl.program_id(1)
    halo = SUBLANES
    DK, DV = ML_DQK, ML_DV

    @pl.when(c == 0)
    def _():
        xp_scr[0:halo, :] = jnp.zeros((halo, 2 * ML_QK_W), F32)
        st_scr[...] = jnp.zeros_like(st_scr)
        m_scr[...] = jnp.zeros_like(m_scr)

    cur = qk_ref[0]
    xp_scr[halo:halo + L, :] = cur
    base = halo - (ML_CONV - 1)
    acc = conv_ref[0:1, :] * xp_scr[base:base + L, :]
    for j in range(1, ML_CONV):
        acc = acc + conv_ref[j:j + 1, :] * xp_scr[base + j:base + j + L, :]
    xp_scr[0:halo, :] = cur[L - halo:L, :]
    qk = acc * jax.nn.sigmoid(acc)
    qT_all = (qk[:, :ML_QK_W] * (DK ** -0.5)).T.astype(BF16)
    k_all = qk[:, ML_QK_W:].astype(BF16)

    grow = smT_ref[0][0:SUBLANES, :] + bcol_ref[...]
    si = lax.broadcasted_iota(I32, (L, L), 0)
    ji = lax.broadcasted_iota(I32, (L, L), 1)
    causal = si <= ji
    b_row = _dot3(_log_sigmoid(grow), causal)

    for h in range(ML_HEADS):
        qT = qT_all[h * DK:(h + 1) * DK, :]
        kh = k_all[:, h * DK:(h + 1) * DK]
        vT = vT_ref[0, h * DV:(h + 1) * DV, :]
        b_j = b_row[ML_HEADS + h:ML_HEADS + h + 1, :]
        c_row = grow[h:h + 1, :] - b_j
        c_col = jnp.broadcast_to(c_row, (LANES, L)).T
        dlog = jnp.where(causal, b_j + jnp.concatenate([c_col] * (L // LANES), axis=1), NEG_INF)
        m0 = m_scr[h:h + 1, 0:1]
        inter = b_j + m0
        m_t = jnp.maximum(inter, jnp.max(dlog, axis=0, keepdims=True))
        w_inter = jnp.exp(inter - m_t)
        w_intra = jnp.exp(dlog - m_t) * jnp.dot(kh, qT, preferred_element_type=F32)
        st = st_scr[h]
        cq = jnp.dot(st.astype(BF16), qT, preferred_element_type=F32)
        num = w_inter * cq[:DV] + jnp.dot(vT, w_intra.astype(BF16), preferred_element_type=F32)
        den = w_inter * cq[DV:DV + 1] + jnp.sum(w_intra, axis=0, keepdims=True)
        hh = num / jnp.maximum(jnp.abs(den), jnp.exp(-m_t))
        hh = hh * lax.rsqrt(jnp.mean(hh * hh, axis=0, keepdims=True) + EPS)
        rows = slice(h * DV, (h + 1) * DV)
        y_ref[0, :, rows] = (jax.nn.sigmoid(oT_ref[0, rows, :]) * hh * hn_ref[rows, :]).T.astype(y_ref.dtype)

        b_last = b_j[:, L - 1:L]
        a = b_last + c_row
        m_loc = jnp.max(a, axis=1, keepdims=True)
        wa = jnp.exp(a - m_loc)
        lhs = jnp.concatenate([(vT.astype(F32) * wa).astype(BF16),
                               jnp.broadcast_to(wa, (SUBLANES, L)).astype(BF16)], axis=0)
        c_loc = jnp.dot(lhs, kh, preferred_element_type=F32)
        m_new = jnp.maximum(b_last + m0, m_loc)
        s_old = jnp.exp(b_last + m0 - m_new)
        s_loc = jnp.exp(m_loc - m_new)
        st_scr[h] = s_old * st + s_loc * c_loc
        m_scr[h:h + 1, :] = jnp.broadcast_to(m_new, (1, LANES))


def _mlstm(qk, vT, oT, smT, conv_w, i_bias, f_bias, head_norm, B, S):
    L = min(ML_CHUNK, S)
    assert L % LANES == 0
    nc = S // L
    qk = qk.reshape(B, S, 2 * ML_QK_W)
    bcol = jnp.concatenate([i_bias, f_bias]).astype(F32).reshape(2 * ML_HEADS, 1)
    hn = jnp.broadcast_to(head_norm.astype(F32)[:, None], (ML_V_W, L))
    blkT = lambda n: pl.BlockSpec((1, n, L), lambda b, c: (b, 0, c))
    full = lambda a: pl.BlockSpec(a.shape, lambda b, c: (0,) * a.ndim)
    return pl.pallas_call(
        functools.partial(_mlstm_kernel, L=L),
        grid=(B, nc),
        in_specs=[pl.BlockSpec((1, L, 2 * ML_QK_W), lambda b, c: (b, c, 0)),
                  blkT(ML_V_W), blkT(ML_V_W), blkT(16), full(conv_w), full(bcol), full(hn)],
        out_specs=pl.BlockSpec((1, L, ML_V_W), lambda b, c: (b, c, 0)),
        out_shape=jax.ShapeDtypeStruct((B, S, ML_V_W), BF16),
        scratch_shapes=[pltpu.VMEM((L + SUBLANES, 2 * ML_QK_W), F32),
                        pltpu.VMEM((ML_HEADS, ML_DV + SUBLANES, ML_DQK), F32),
                        pltpu.VMEM((SUBLANES, LANES), F32)],
        compiler_params=_params("parallel", "arbitrary"),
    )(qk, vT, oT, smT, conv_w.astype(F32), bcol, hn)


def _block_diag_T(qT, heads, dh):
    rid = lax.broadcasted_iota(I32, qT.shape, 0)
    zero = jnp.zeros_like(qT)
    return jnp.concatenate(
        [jnp.where((rid >= h * dh) & (rid < (h + 1) * dh), qT, zero) for h in range(heads)], axis=1)


def _scores(k_chunk, qs, h, TQ):
    return jnp.dot(k_chunk, qs[:, h * TQ:(h + 1) * TQ], preferred_element_type=F32)


def _accumulate(h, p, vT_chunk, l, acc_scr, dh):
    rows = slice(h * dh, (h + 1) * dh)
    acc_scr[rows, :] += jnp.dot(vT_chunk[rows, :], p.astype(BF16), preferred_element_type=F32)
    return l + jnp.sum(p, axis=0, keepdims=True)


def _finish_attention(l, acc_scr, y_ref, heads, dh):
    outs = [acc_scr[h * dh:(h + 1) * dh, :] / l[h] for h in range(heads)]
    y_ref[0] = jnp.concatenate(outs, axis=0).T.astype(y_ref.dtype)


def _max_key_norm2(kb, heads, dh, TQ):
    ch = lax.broadcasted_iota(I32, (heads * dh, LANES), 0) // dh
    hd = lax.broadcasted_iota(I32, (heads * dh, LANES), 1)
    n2 = jnp.dot(kb * kb, (ch == hd).astype(F32), precision=lax.Precision.HIGHEST, preferred_element_type=F32)
    n2 = jnp.max(n2, axis=0, keepdims=True)
    return jnp.concatenate([jnp.broadcast_to(n2[:, h:h + 1], (1, TQ)) for h in range(heads)], axis=1)


def _norm_bound(qTf, k2, heads, dh, TQ):
    out = []
    for h in range(heads):
        qh = qTf[h * dh:(h + 1) * dh, :]
        q2 = jnp.sum(qh * qh, axis=0, keepdims=True)
        out.append(jnp.sqrt(q2 * k2[:, h * TQ:(h + 1) * TQ]) * (1.0 + 2.0 ** -8))
    return out


def _underflowed(l):
    small = l[0]
    for lh in l[1:]:
        small = jnp.minimum(small, lh)
    return jnp.logical_not(jnp.min(small) >= SOFTMAX_L_MIN)


def _moba_kernel(qT_ref, k_ref, vT_ref, y_ref, km_scr, kn_scr, acc_scr, s_scr, *, NB, NBP, n_sel):
    H, DH, BS = MOBA_HEADS, MOBA_DH, MOBA_BLOCK
    TQ = BS
    qi = pl.program_id(1)

    @pl.when(qi == 0)
    def _():
        km_scr[...] = jnp.zeros_like(km_scr)
        kn_scr[...] = jnp.zeros_like(kn_scr)

        def body(j, carry):
            kb = k_ref[0, pl.ds(pl.multiple_of(j * BS, BS), BS), :].astype(F32)
            km_scr[pl.ds(j, 1), :] = jnp.mean(kb, axis=0, keepdims=True)
            kn_scr[pl.ds(j, 1), :] = _max_key_norm2(kb, H, DH, TQ)
            return carry
        lax.fori_loop(0, NB, body, 0)

    qT = qT_ref[0]
    qTf = qT.astype(F32)
    km = km_scr[...]
    gate = jnp.concatenate(
        [jnp.dot(km[:, h * DH:(h + 1) * DH], qTf[h * DH:(h + 1) * DH, :], precision=lax.Precision.HIGHEST,
                 preferred_element_type=F32) for h in range(H)], axis=1) * (1.0 / ATT_QSCALE)
    blk = lax.broadcasted_iota(I32, gate.shape, 0)
    g = jnp.where(blk < qi, gate, NEG_INF)
    sels = []
    for _ in range(n_sel):
        mx = jnp.max(g, axis=0, keepdims=True)
        isel = jnp.min(jnp.where(g == mx, blk, NBP), axis=0, keepdims=True)
        sels.append(jnp.where(mx > NEG_INF, isel, -1))
        g = jnp.where(blk == isel, NEG_INF, g)

    qs = _block_diag_T(qT, H, DH)

    def hit_row(j):
        hit = sels[0] == j
        for s in sels[1:]:
            hit = hit | (s == j)
        return hit

    npair = (qi + 1) // 2
    blk_start = lambda j: pl.multiple_of(j * BS, BS)
    blk_scores = lambda j: jnp.dot(k_ref[0, pl.ds(blk_start(j), BS), :], qs, preferred_element_type=F32)
    hcols = lambda row, h: row[:, h * TQ:(h + 1) * TQ]
    causal = lax.broadcasted_iota(I32, (BS, TQ), 0) <= lax.broadcasted_iota(I32, (BS, TQ), 1)

    def diag_scores():
        s_all = blk_scores(qi)
        return [jnp.where(causal, hcols(s_all, h), NEG_INF) for h in range(H)]

    def max_sweep():
        def max_pair(g, m):
            for u in range(2):
                j = 2 * g + u
                m = jnp.maximum(m, jnp.where(hit_row(j), jnp.max(blk_scores(j), axis=0, keepdims=True), NEG_INF))
            return m
        m_all = lax.fori_loop(0, npair, max_pair, jnp.full((1, H * TQ), NEG_INF, F32))
        s_diag = diag_scores()
        return [jnp.maximum(hcols(m_all, h), jnp.max(s_diag[h], axis=0, keepdims=True)) for h in range(H)]

    def sum_sweep(m):
        acc_scr[...] = jnp.zeros_like(acc_scr)

        def sum_group(width):
            def body(g, l):
                for u in range(width):
                    s_scr[u] = blk_scores(width * g + u)
                for u in range(width):
                    j = width * g + u
                    hit = hit_row(j)
                    vT_blk = vT_ref[0, :, pl.ds(blk_start(j), BS)]
                    l = tuple(_accumulate(h, jnp.exp2(s_scr[u, :, h * TQ:(h + 1) * TQ]
                                                      - jnp.where(hcols(hit, h), m[h], jnp.inf)),
                                          vT_blk, l[h], acc_scr, DH) for h in range(H))
                return l
            return body
        nwide = qi // ATT_STAGE
        l = lax.fori_loop(0, nwide, sum_group(ATT_STAGE), (jnp.zeros((1, TQ), F32),) * H)
        l = lax.fori_loop((ATT_STAGE // 2) * nwide, npair, sum_group(2), l)
        s_diag = diag_scores()
        vT_diag = vT_ref[0, :, pl.ds(blk_start(qi), BS)]
        return [_accumulate(h, jnp.exp2(s_diag[h] - m[h]), vT_diag, l[h], acc_scr, DH) for h in range(H)]

    blk_n = lax.broadcasted_iota(I32, kn_scr.shape, 0)
    k2 = jnp.max(jnp.where(blk_n <= qi, kn_scr[...], 0.0), axis=0, keepdims=True)
    l = sum_sweep(_norm_bound(qTf, k2, H, DH, TQ))
    _finish_attention(l, acc_scr, y_ref, H, DH)

    @pl.when(_underflowed(l))
    def _():
        _finish_attention(sum_sweep(max_sweep()), acc_scr, y_ref, H, DH)


def _moba(qT, k, vT, B, S):
    BS = MOBA_BLOCK
    assert S % BS == 0
    NB = S // BS
    NBP = -(-NB // SUBLANES) * SUBLANES
    n_sel = max(min(MOBA_TOPK, NB - 1), 1)
    k = k.reshape(B, S, MOBA_W)
    return pl.pallas_call(
        functools.partial(_moba_kernel, NB=NB, NBP=NBP, n_sel=n_sel),
        grid=(B, NB),
        in_specs=[pl.BlockSpec((1, MOBA_W, BS), lambda b, i: (b, 0, i)),
                  pl.BlockSpec((1, S, MOBA_W), lambda b, i: (b, 0, 0)),
                  pl.BlockSpec((1, MOBA_W, S), lambda b, i: (b, 0, 0))],
        out_specs=pl.BlockSpec((1, BS, MOBA_W), lambda b, i: (b, i, 0)),
        out_shape=jax.ShapeDtypeStruct((B, S, MOBA_W), BF16),
        scratch_shapes=[pltpu.VMEM((NBP, MOBA_W), F32),
                        pltpu.VMEM((NBP, MOBA_HEADS * BS), F32),
                        pltpu.VMEM((MOBA_W, BS), F32),
                        pltpu.VMEM((ATT_STAGE, BS, MOBA_HEADS * BS), F32)],
        compiler_params=_params("parallel", "arbitrary"),
    )(qT, k, vT)


def _dsa_kernel(qT_ref, k_ref, vT_ref, iqT_ref, ik_ref, smT_ref, y_ref,
                e_scr, gmax_scr, acc_scr, thr_scr, kn_scr, s_scr, *, TQ, topk, spow, n_chunks):
    H, DH = DSA_HEADS, DSA_DH
    KC = TQ
    qi = pl.program_id(1)
    nch = qi + 1
    npair = (nch + 1) // 2
    chunk_start = lambda c: pl.multiple_of(c * KC, KC)

    @pl.when(qi == 0)
    def _():
        kn_scr[...] = jnp.zeros_like(kn_scr)

        def body(c, carry):
            kn_scr[pl.ds(c, 1), :] = _max_key_norm2(k_ref[0, pl.ds(chunk_start(c), KC), :].astype(F32), H, DH, TQ)
            return carry
        lax.fori_loop(0, n_chunks, body, 0)
    sum8 = lambda b: jnp.sum(b.astype(I32).reshape(KC // SUBLANES, SUBLANES, TQ), axis=0)
    any_row = lambda r: jnp.max(r.astype(I32)) > 0

    iqT = iqT_ref[0]
    iq_cat = jnp.concatenate([iqT[h * IDX_DH:(h + 1) * IDX_DH, :] for h in range(IDX_HEADS)], axis=1)
    iw = smT_ref[0][2 * ML_HEADS:2 * ML_HEADS + IDX_HEADS, :] * (IDX_HEADS ** -0.5 * IDX_DH ** -0.5)
    t_pos = qi * TQ + lax.broadcasted_iota(I32, (KC, TQ), 1)
    s_off = lax.broadcasted_iota(I32, (KC, TQ), 0)
    gmax_scr[...] = jnp.full((KC, TQ), NEG_INF, F32)

    def score_group(width, masked):
        def body(g, tally):
            for u in range(width):
                ikc = ik_ref[0, pl.ds(chunk_start(width * g + u), KC), :][:, :IDX_DH]
                s_scr[u] = jnp.dot(ikc, iq_cat, preferred_element_type=F32)
            for u in range(width):
                c = width * g + u
                sc = iw[0:1, :] * jnp.maximum(s_scr[u, :, 0:TQ], 0.0)
                for h in range(1, IDX_HEADS):
                    sc = sc + iw[h:h + 1, :] * jnp.maximum(s_scr[u, :, h * TQ:(h + 1) * TQ], 0.0)
                if masked:
                    sc = jnp.where(c * KC + s_off <= t_pos, sc, NEG_INF)
                e_scr[pl.ds(chunk_start(c), KC), :] = sc
                gmax_scr[...] = jnp.maximum(gmax_scr[...], sc)
                tally = tally + jnp.sum(jnp.where(sc > 0.0, 1, jnp.where(sc == 0.0, 1 << 16, 0))
                                        .reshape(KC // SUBLANES, SUBLANES, TQ), axis=0)
            return tally
        return body
    assert n_chunks * KC < (1 << 16)
    zero8 = jnp.zeros((SUBLANES, TQ), I32)
    tally = lax.fori_loop(0, (npair - 1) // 2, score_group(4, False), zero8)
    tally = lax.fori_loop(2 * ((npair - 1) // 2), npair - 1, score_group(2, False), tally)
    tally = jnp.sum(score_group(2, True)(npair - 1, tally), axis=0, keepdims=True)
    n_pos = tally & 0xFFFF
    n_nonneg = n_pos + (tally >> 16)

    def sweep_keys(fold, merge, init):
        def group(width):
            def body(g, acc):
                return tuple(merge(acc[u], fold(e_scr[pl.ds(chunk_start(width * g + u), KC), :]))
                             for u in range(width))
            return body
        acc = lax.fori_loop(0, npair // 2, group(4), (init,) * 4)
        a, b = lax.fori_loop(2 * (npair // 2), npair, group(2), (merge(acc[0], acc[2]), merge(acc[1], acc[3])))
        return merge(a, b)

    def count_gt(cand):
        return jnp.sum(sweep_keys(lambda x: sum8(x > cand), lambda p, q: p + q, zero8), axis=0, keepdims=True)

    def min_gt(cand):
        fold = lambda x: jnp.min(jnp.where(x > cand, x, jnp.inf).reshape(KC // SUBLANES, SUBLANES, TQ), axis=0)
        return jnp.min(sweep_keys(fold, jnp.minimum, jnp.full((SUBLANES, TQ), jnp.inf, F32)), axis=0, keepdims=True)

    def rewrite(rows, thr):
        def body(c, carry):
            x = e_scr[pl.ds(chunk_start(c), KC), :]
            rank = (2 * spow - (c * KC + s_off)).astype(F32)
            new = jnp.where(x > thr, jnp.inf, jnp.where(x == thr, rank, NEG_INF))
            e_scr[pl.ds(chunk_start(c), KC), :] = jnp.where(rows, new, x)
            return carry
        lax.fori_loop(0, nch, body, 0)

    def halve(_, c):
        lo, hi, act, tie, n_lo = c
        mid = lo + (hi - lo) * 0.5
        inside = (mid > lo) & (mid < hi)
        cnt = count_gt(mid)
        live = act > 0.0
        run = live & inside
        up = run & (cnt >= topk)
        lo = jnp.where(up, mid, lo)
        n_lo = jnp.where(up, cnt, n_lo)
        hi = jnp.where(run & (cnt < topk), mid, hi)
        collapsed = live & jnp.logical_not(inside)
        tie = jnp.where(collapsed, 1.0, tie)
        act = jnp.where(collapsed | (run & (cnt == topk)), 0.0, act)
        return lo, hi, act, tie, n_lo

    def excess(state):
        return jnp.max(jnp.where(state[2] > 0.0, state[4] - topk, 0).astype(F32))

    def bisect(lo, hi, act, warmup):
        state = (lo, hi, act, jnp.zeros_like(act), jnp.full((1, TQ), 1 << 20, I32))
        state = lax.fori_loop(0, warmup, halve, state)

        def one_more(c):
            state = halve(0, c[1])
            return c[0] + 1, state, excess(state)
        near = lax.while_loop(lambda c: jnp.logical_and(c[0] < BISECT_CAP, c[2] > BISECT_DROP), one_more,
                              (jnp.int32(warmup), state, excess(state)))
        lo, hi, act, tie, n_lo = near[1]
        cur, left = lo, jnp.where(act > 0.0, n_lo - topk, 0)
        for _ in range(BISECT_DROP):
            cur = jnp.where(left > 0, min_gt(cur), cur)
            left = left - 1
        done = (act > 0.0) & (count_gt(cur) == topk)
        lo = jnp.where(done, cur, lo)
        act = jnp.where(done, 0.0, act)

        def body(c):
            state = c[1]
            for _ in range(BISECT_STEPS_PER_CHECK):
                state = halve(0, state)
            return c[0] + BISECT_STEPS_PER_CHECK, state, jnp.max(state[2])
        out = lax.while_loop(lambda c: jnp.logical_and(c[0] < BISECT_CAP, c[2] > 0.0), body,
                             (near[0], (lo, hi, act, tie, n_lo), jnp.max(act)))
        return out[1][0], out[1][1], out[1][3]

    fmax = float(jnp.finfo(F32).max)
    gm = gmax_scr[...]
    g_lo = jnp.min(gm, axis=0, keepdims=True)
    g_hi = jnp.max(gm, axis=0, keepdims=True)
    n_vis = qi * TQ + lax.broadcasted_iota(I32, (1, TQ), 1) + 1
    few = n_vis <= topk
    zero_tie = (n_pos < topk) & (n_nonneg >= topk) & jnp.logical_not(few)

    @pl.when(any_row(zero_tie))
    def _():
        rewrite(zero_tie, jnp.zeros((1, TQ), F32))

    rank_lo, rank_hi = float(spow), float(2 * spow + 1)
    below = jnp.maximum(g_lo, -fmax)
    below = below - jnp.abs(below) * (2.0 ** -10) - 1e-30
    lo0 = jnp.where(zero_tie, rank_lo, jnp.maximum(below, -fmax))
    hi0 = jnp.where(zero_tie, rank_hi, g_hi)
    lo, hi, tie = bisect(lo0, hi0, jnp.where(few, 0.0, 1.0), BISECT_WARMUP)
    thr_scr[0:1, :] = jnp.where(few, NEG_INF, lo)

    @pl.when(jnp.max(tie) > 0.0)
    def _():
        rows = tie > 0.0
        rewrite(rows, hi)
        lo2, _, _ = bisect(jnp.full((1, TQ), rank_lo, F32), jnp.full((1, TQ), rank_hi, F32), tie, 0)
        thr_scr[0:1, :] = jnp.where(rows, lo2, thr_scr[0:1, :])

    thr = thr_scr[0:1, :]
    qT = qT_ref[0]
    qs = _block_diag_T(qT, H, DH)
    hcols = lambda row, h: row[:, h * TQ:(h + 1) * TQ]

    def chunk_scores(c):
        start = chunk_start(c)
        sel = e_scr[pl.ds(start, KC), :] > thr
        s_all = jnp.dot(k_ref[0, pl.ds(start, KC), :], qs, preferred_element_type=F32)
        return sel, s_all

    def max_sweep():
        def max_pair(g, m):
            for u in range(2):
                sel, s_all = chunk_scores(2 * g + u)
                m = tuple(jnp.maximum(m[h], jnp.max(jnp.where(sel, hcols(s_all, h), NEG_INF), axis=0, keepdims=True))
                          for h in range(H))
            return m
        m = lax.fori_loop(0, npair, max_pair, (jnp.full((1, TQ), NEG_INF, F32),) * H)
        return [jnp.where(mh == NEG_INF, 0.0, mh) for mh in m]

    def sum_sweep(m):
        acc_scr[...] = jnp.zeros_like(acc_scr)

        def sum_group(width):
            def body(g, l):
                for u in range(width):
                    s_scr[u] = jnp.dot(k_ref[0, pl.ds(chunk_start(width * g + u), KC), :], qs,
                                       preferred_element_type=F32)
                for u in range(width):
                    start = chunk_start(width * g + u)
                    sel = e_scr[pl.ds(start, KC), :] > thr
                    vT_chunk = vT_ref[0, :, pl.ds(start, KC)]
                    l = tuple(_accumulate(h, jnp.exp2(s_scr[u, :, h * TQ:(h + 1) * TQ] + jnp.where(sel, -m[h], NEG_INF)),
                                          vT_chunk, l[h], acc_scr, DH) for h in range(H))
                return l
            return body
        nwide = (2 * npair) // ATT_STAGE
        l = lax.fori_loop(0, nwide, sum_group(ATT_STAGE), (jnp.zeros((1, TQ), F32),) * H)
        return lax.fori_loop((ATT_STAGE // 2) * nwide, npair, sum_group(2), l)

    chunk_n = lax.broadcasted_iota(I32, kn_scr.shape, 0)
    k2 = jnp.max(jnp.where(chunk_n < 2 * npair, kn_scr[...], 0.0), axis=0, keepdims=True)
    l = sum_sweep(_norm_bound(qT.astype(F32), k2, H, DH, TQ))
    _finish_attention(l, acc_scr, y_ref, H, DH)

    @pl.when(_underflowed(l))
    def _():
        _finish_attention(sum_sweep(max_sweep()), acc_scr, y_ref, H, DH)


def _dsa(qT, k, vT, iqT, ik, smT, B, S):
    TQ = min(ATT_TQ, S)
    topk = min(DSA_TOPK_MAX, S // 4)
    assert S % (2 * TQ) == 0 and topk <= TQ
    spow = 1 << max((S - 1).bit_length(), 1)
    k = k.reshape(B, S, DSA_W)
    ik = ik.reshape(B, S, LANES)
    qblk = lambda n: pl.BlockSpec((1, n, TQ), lambda b, i: (b, 0, i))
    return pl.pallas_call(
        functools.partial(_dsa_kernel, TQ=TQ, topk=topk, spow=spow, n_chunks=S // TQ),
        grid=(B, S // TQ),
        in_specs=[qblk(DSA_W),
                  pl.BlockSpec((1, S, DSA_W), lambda b, i: (b, 0, 0)),
                  pl.BlockSpec((1, DSA_W, S), lambda b, i: (b, 0, 0)),
                  qblk(IDX_HEADS * IDX_DH),
                  pl.BlockSpec((1, S, LANES), lambda b, i: (b, 0, 0)),
                  qblk(16)],
        out_specs=pl.BlockSpec((1, TQ, DSA_W), lambda b, i: (b, i, 0)),
        out_shape=jax.ShapeDtypeStruct((B, S, DSA_W), BF16),
        scratch_shapes=[pltpu.VMEM((S, TQ), F32),
                        pltpu.VMEM((TQ, TQ), F32),
                        pltpu.VMEM((DSA_W, TQ), F32),
                        pltpu.VMEM((SUBLANES, TQ), F32),
                        pltpu.VMEM((-(-(S // TQ) // SUBLANES) * SUBLANES, DSA_HEADS * TQ), F32),
                        pltpu.VMEM((ATT_STAGE, TQ, DSA_HEADS * TQ), F32)],
        compiler_params=_params("parallel", "arbitrary"),
    )(qT, k, vT, iqT, ik, smT)


def _out_mlp_kernel(yml_ref, ymb_ref, yds_ref, x_ref, wout_ref, gpost_ref, gpre_ref,
                    w1_ref, w2_ref, g2_ref, out_ref, x1_scr, h_scr, acc_scr):
    f = pl.program_id(1)

    @pl.when(f == 0)
    def _():
        mix = jnp.dot(yml_ref[...], wout_ref[0:ML_V_W, :], preferred_element_type=F32)
        mix = mix + jnp.dot(ymb_ref[...], wout_ref[ML_V_W:ML_V_W + MOBA_W, :], preferred_element_type=F32)
        mix = mix + jnp.dot(yds_ref[...], wout_ref[ML_V_W + MOBA_W:, :], preferred_element_type=F32)
        x1 = x_ref[...] + _rms(mix, gpost_ref[...])
        x1_scr[...] = x1
        h_scr[...] = _rms(x1, gpre_ref[...]).astype(BF16)
        acc_scr[...] = jnp.zeros_like(acc_scr)

    u = jnp.maximum(jnp.dot(h_scr[...], w1_ref[...], preferred_element_type=F32), 0.0)
    acc_scr[...] += jnp.dot((u * u).astype(BF16), w2_ref[...], preferred_element_type=F32)

    @pl.when(f == pl.num_programs(1) - 1)
    def _():
        out_ref[...] = x1_scr[...] + _rms(acc_scr[...], g2_ref[...])


def _out_mlp(yml, ymb, yds, x2, w_out, g_post, g_pre, w1, w2, g2):
    T, D = x2.shape
    tm = min(MLP_TM, T)
    tf = min(MLP_TF, D_FF)
    row = lambda n: pl.BlockSpec((tm, n), lambda i, f: (i, 0))
    full = lambda a: pl.BlockSpec(a.shape, lambda i, f: (0,) * a.ndim)
    return pl.pallas_call(
        _out_mlp_kernel,
        grid=(T // tm, D_FF // tf),
        in_specs=[row(ML_V_W), row(MOBA_W), row(DSA_W), row(D), full(w_out), full(g_post), full(g_pre),
                  pl.BlockSpec((D, tf), lambda i, f: (0, f)),
                  pl.BlockSpec((tf, D), lambda i, f: (f, 0)),
                  full(g2)],
        out_specs=row(D),
        out_shape=jax.ShapeDtypeStruct((T, D), F32),
        scratch_shapes=[pltpu.VMEM((tm, D), F32), pltpu.VMEM((tm, D), BF16), pltpu.VMEM((tm, D), F32)],
        compiler_params=_params("parallel", "arbitrary"),
    )(yml, ymb, yds, x2, w_out, g_post, g_pre, w1, w2, g2)


def kernel(x, norm_mix_pre, w_in, ml_conv, ml_i_bias, ml_f_bias, ml_head_norm, w_out, norm_mix_post,
           norm_mlp_pre, w_ff1, w_ff2, norm_mlp_post):
    B, S, D = x.shape
    depth = w_in.shape[0]
    x2 = x.reshape(B * S, D)
    gain = lambda g: g.reshape(1, D).astype(F32)
    for l in range(depth):
        wn, wt = _prep_in_weights(w_in[l])
        (qk, mbk, dsk, ixk, mbqT, mbvT, dsqT, dsvT, ixqT, mlvT, mloT, smT) = _in_proj(
            x2, gain(norm_mix_pre[l]), wn, wt, B, S)
        y_ml = _mlstm(qk, mlvT, mloT, smT, ml_conv[l], ml_i_bias[l], ml_f_bias[l], ml_head_norm[l], B, S)
        y_mb = _moba(mbqT, mbk, mbvT, B, S)
        y_ds = _dsa(dsqT, dsk, dsvT, ixqT, ixk, smT, B, S)
        x2 = _out_mlp(y_ml.reshape(B * S, ML_V_W), y_mb.reshape(B * S, MOBA_W), y_ds.reshape(B * S, DSA_W),
                      x2, w_out[l].astype(BF16), gain(norm_mix_post[l]), gain(norm_mlp_pre[l]),
                      w_ff1[l].astype(BF16), w_ff2[l].astype(BF16), gain(norm_mlp_post[l]))
    return x2.reshape(B, S, D)
```

```python
import functools

import jax
import jax.numpy as jnp
from jax import lax
from jax.experimental import pallas as pl
from jax.experimental.pallas import tpu as pltpu

F32 = jnp.float32
BF16 = jnp.bfloat16
I32 = jnp.int32

D_MODEL = 1024
ML_HEADS, ML_DQK, ML_DV, ML_CONV = 4, 64, 128, 4
MOBA_HEADS, MOBA_DH, MOBA_BLOCK, MOBA_TOPK = 4, 64, 256, 3
DSA_HEADS, DSA_DH, IDX_HEADS, IDX_DH, DSA_TOPK_MAX = 4, 64, 4, 64, 256
D_FF = 4 * D_MODEL
EPS = 1e-6

ML_QK_W = ML_HEADS * ML_DQK
ML_V_W = ML_HEADS * ML_DV
MOBA_W = MOBA_HEADS * MOBA_DH
DSA_W = DSA_HEADS * DSA_DH
IN_SPLITS = (ML_QK_W, ML_QK_W, ML_V_W, ML_V_W, ML_HEADS, ML_HEADS,
             MOBA_W, MOBA_W, MOBA_W,
             DSA_W, DSA_W, DSA_W, IDX_HEADS * IDX_DH, IDX_DH, IDX_HEADS)

LANES = 128
SUBLANES = 8
VMEM_LIMIT = 52 * 1024 * 1024

ML_CHUNK = 256
ATT_TQ = 256
ATT_STAGE = 8
PROJ_TM = 512
MLP_TM = 1024
MLP_TF = 512
MLP_SLABS = 4

NEG_INF = float("-inf")
BISECT_CAP = 300
BISECT_WARMUP = 12
BISECT_DROP = 2
BISECT_STEPS_PER_CHECK = 2
SOFTMAX_L_MIN = 1e-30
assert MOBA_DH == DSA_DH
ATT_QSCALE = MOBA_DH ** -0.5 * 1.4426950408889634

_NT = (((1,), (1,)), ((), ()))
_TN = (((0,), (0,)), ((), ()))


def _rms(x, g):
    return x * lax.rsqrt(jnp.mean(x * x, axis=-1, keepdims=True) + EPS) * g


def _log_sigmoid(x):
    return jnp.minimum(x, 0.0) - jnp.log1p(jnp.exp(-jnp.abs(x)))


def _params(*sem):
    return pltpu.CompilerParams(dimension_semantics=sem, vmem_limit_bytes=VMEM_LIMIT)


_N_QK, _N_MBK, _N_DSK, _N_IXK = 0, 512, 768, 1024
_N_TOTAL = 1152
_T_MBQ, _T_MBV, _T_DSQ, _T_DSV, _T_IXQ, _T_MLV, _T_MLO, _T_SM = 0, 256, 512, 768, 1024, 1280, 1792, 2304
_T_TOTAL = 2320


def _in_proj_kernel(x_ref, g_ref, wn_ref, wt_ref,
                    qk_ref, mbk_ref, dsk_ref, ixk_ref,
                    mbqT_ref, mbvT_ref, dsqT_ref, dsvT_ref, ixqT_ref, mlvT_ref, mloT_ref, smT_ref):
    h = _rms(x_ref[...], g_ref[...]).astype(BF16)

    def mm(a, n):
        return jnp.dot(h, wn_ref[:, a:a + n], preferred_element_type=F32)

    def mt(a, n):
        return lax.dot_general(wt_ref[a:a + n, :], h, _NT, preferred_element_type=F32)

    qk_ref[...] = mm(_N_QK, 512)
    mbk_ref[...] = mm(_N_MBK, 256).astype(BF16)
    dsk_ref[...] = mm(_N_DSK, 256).astype(BF16)
    ixk_ref[...] = mm(_N_IXK, 128).astype(BF16)
    mbqT_ref[0] = (mt(_T_MBQ, 256) * ATT_QSCALE).astype(BF16)
    mbvT_ref[0] = mt(_T_MBV, 256).astype(BF16)
    dsqT_ref[0] = (mt(_T_DSQ, 256) * ATT_QSCALE).astype(BF16)
    dsvT_ref[0] = mt(_T_DSV, 256).astype(BF16)
    ixqT_ref[0] = mt(_T_IXQ, 256).astype(BF16)
    mlvT_ref[0] = mt(_T_MLV, 512).astype(BF16)
    mloT_ref[0] = mt(_T_MLO, 512)
    smT_ref[0] = mt(_T_SM, 16)


def _prep_in_weights(w_in):
    pts, acc = [], 0
    for n in IN_SPLITS:
        pts.append((acc, acc + n))
        acc += n
    col = lambda i: w_in[:, pts[i][0]:pts[i][1]]
    (ml_q, ml_k, ml_v, ml_o, ml_i, ml_f, mb_q, mb_k, mb_v,
     ds_q, ds_k, ds_v, ix_q, ix_k, ix_w) = [col(i) for i in range(len(IN_SPLITS))]
    d = w_in.shape[0]
    small = jnp.concatenate([ml_i, ml_f, ix_w], axis=1)
    wn = jnp.concatenate([
        ml_q, ml_k, mb_k, ds_k,
        ix_k, jnp.zeros((d, 128 - IDX_DH), w_in.dtype)], axis=1)
    wt = jnp.concatenate([
        mb_q, mb_v, ds_q, ds_v, ix_q, ml_v, ml_o,
        small, jnp.zeros((d, 16 - small.shape[1]), w_in.dtype)], axis=1).T
    assert wn.shape == (d, _N_TOTAL) and wt.shape == (_T_TOTAL, d)
    return wn.astype(BF16), wt.astype(BF16)


def _in_proj(x2, g, wn, wt, B, S):
    T, D = x2.shape
    tm = min(PROJ_TM, S)
    nsb = S // tm
    row = lambda n: pl.BlockSpec((tm, n), lambda i: (i, 0))
    tr = lambda n: pl.BlockSpec((1, n, tm), lambda i: (i // nsb, 0, i % nsb))
    full = lambda a: pl.BlockSpec(a.shape, lambda i: (0,) * a.ndim)
    out_shape = (
        jax.ShapeDtypeStruct((T, 512), F32),
        jax.ShapeDtypeStruct((T, 256), BF16),
        jax.ShapeDtypeStruct((T, 256), BF16),
        jax.ShapeDtypeStruct((T, 128), BF16),
        jax.ShapeDtypeStruct((B, 256, S), BF16),
        jax.ShapeDtypeStruct((B, 256, S), BF16),
        jax.ShapeDtypeStruct((B, 256, S), BF16),
        jax.ShapeDtypeStruct((B, 256, S), BF16),
        jax.ShapeDtypeStruct((B, 256, S), BF16),
        jax.ShapeDtypeStruct((B, 512, S), BF16),
        jax.ShapeDtypeStruct((B, 512, S), F32),
        jax.ShapeDtypeStruct((B, 16, S), F32),
    )
    out_specs = (row(512), row(256), row(256), row(128),
                 tr(256), tr(256), tr(256), tr(256), tr(256), tr(512), tr(512), tr(16))
    return pl.pallas_call(
        _in_proj_kernel,
        grid=(T // tm,),
        in_specs=[row(D), full(g), full(wn), full(wt)],
        out_specs=out_specs,
        out_shape=out_shape,
        compiler_params=_params("parallel"),
    )(x2, g, wn, wt)


def _dot3(a_f32, b_exact):
    b = b_exact.astype(BF16)
    hi = a_f32.astype(BF16)
    r1 = a_f32 - hi.astype(F32)
    mid = r1.astype(BF16)
    lo = (r1 - mid.astype(F32)).astype(BF16)
    return (jnp.dot(hi, b, preferred_element_type=F32) + jnp.dot(mid, b, preferred_element_type=F32)
            + jnp.dot(lo, b, preferred_element_type=F32))


def _mlstm_kernel(qk_ref, vT_ref, oT_ref, smT_ref, conv_ref, bcol_ref, hn_ref,
                  y_ref, xp_scr, st_scr, m_scr, *, L):
    c = pl.program_id(1)
    halo = SUBLANES
    DK, DV = ML_DQK, ML_DV

    @pl.when(c == 0)
    def _():
        xp_scr[0:halo, :] = jnp.zeros((halo, 2 * ML_QK_W), F32)
        st_scr[...] = jnp.zeros_like(st_scr)
        m_scr[...] = jnp.zeros_like(m_scr)

    cur = qk_ref[0]
    xp_scr[halo:halo + L, :] = cur
    base = halo - (ML_CONV - 1)
    acc = conv_ref[0:1, :] * xp_scr[base:base + L, :]
    for j in range(1, ML_CONV):
        acc = acc + conv_ref[j:j + 1, :] * xp_scr[base + j:base + j + L, :]
    xp_scr[0:halo, :] = cur[L - halo:L, :]
    qk = acc * jax.nn.sigmoid(acc)
    qT_all = (qk[:, :ML_QK_W] * (DK ** -0.5)).T.astype(BF16)
    k_all = qk[:, ML_QK_W:].astype(BF16)

    grow = smT_ref[0][0:SUBLANES, :] + bcol_ref[...]
    si = lax.broadcasted_iota(I32, (L, L), 0)
    ji = lax.broadcasted_iota(I32, (L, L), 1)
    causal = si <= ji
    b_row = _dot3(_log_sigmoid(grow), causal)

    for h in range(ML_HEADS):
        qT = qT_all[h * DK:(h + 1) * DK, :]
        kh = k_all[:, h * DK:(h + 1) * DK]
        vT = vT_ref[0, h * DV:(h + 1) * DV, :]
        b_j = b_row[ML_HEADS + h:ML_HEADS + h + 1, :]
        c_row = grow[h:h + 1, :] - b_j
        c_col = jnp.broadcast_to(c_row, (LANES, L)).T
        dlog = jnp.where(causal, b_j + jnp.concatenate([c_col] * (L // LANES), axis=1), NEG_INF)
        m0 = m_scr[h:h + 1, 0:1]
        inter = b_j + m0
        m_t = jnp.maximum(inter, jnp.max(dlog, axis=0, keepdims=True))
        w_inter = jnp.exp(inter - m_t)
        w_intra = jnp.exp(dlog - m_t) * jnp.dot(kh, qT, preferred_element_type=F32)
        st = st_scr[h]
        cq = jnp.dot(st.astype(BF16), qT, preferred_element_type=F32)
        num = w_inter * cq[:DV] + jnp.dot(vT, w_intra.astype(BF16), preferred_element_type=F32)
        den = w_inter * cq[DV:DV + 1] + jnp.sum(w_intra, axis=0, keepdims=True)
        hh = num / jnp.maximum(jnp.abs(den), jnp.exp(-m_t))
        hh = hh * lax.rsqrt(jnp.mean(hh * hh, axis=0, keepdims=True) + EPS)
        rows = slice(h * DV, (h + 1) * DV)
        y_ref[0, :, rows] = (jax.nn.sigmoid(oT_ref[0, rows, :]) * hh * hn_ref[rows, :]).T.astype(y_ref.dtype)

        b_last = b_j[:, L - 1:L]
        a = b_last + c_row
        m_loc = jnp.max(a, axis=1, keepdims=True)
        wa = jnp.exp(a - m_loc)
        lhs = jnp.concatenate([(vT.astype(F32) * wa).astype(BF16),
                               jnp.broadcast_to(wa, (SUBLANES, L)).astype(BF16)], axis=0)
        c_loc = jnp.dot(lhs, kh, preferred_element_type=F32)
        m_new = jnp.maximum(b_last + m0, m_loc)
        s_old = jnp.exp(b_last + m0 - m_new)
        s_loc = jnp.exp(m_loc - m_new)
        st_scr[h] = s_old * st + s_loc * c_loc
        m_scr[h:h + 1, :] = jnp.broadcast_to(m_new, (1, LANES))


def _mlstm(qk, vT, oT, smT, conv_w, i_bias, f_bias, head_norm, B, S):
    L = min(ML_CHUNK, S)
    assert L % LANES == 0
    nc = S // L
    qk = qk.reshape(B, S, 2 * ML_QK_W)
    bcol = jnp.concatenate([i_bias, f_bias]).astype(F32).reshape(2 * ML_HEADS, 1)
    hn = jnp.broadcast_to(head_norm.astype(F32)[:, None], (ML_V_W, L))
    blkT = lambda n: pl.BlockSpec((1, n, L), lambda b, c: (b, 0, c))
    full = lambda a: pl.BlockSpec(a.shape, lambda b, c: (0,) * a.ndim)
    return pl.pallas_call(
        functools.partial(_mlstm_kernel, L=L),
        grid=(B, nc),
        in_specs=[pl.BlockSpec((1, L, 2 * ML_QK_W), lambda b, c: (b, c, 0)),
                  blkT(ML_V_W), blkT(ML_V_W), blkT(16), full(conv_w), full(bcol), full(hn)],
        out_specs=pl.BlockSpec((1, L, ML_V_W), lambda b, c: (b, c, 0)),
        out_shape=jax.ShapeDtypeStruct((B, S, ML_V_W), BF16),
        scratch_shapes=[pltpu.VMEM((L + SUBLANES, 2 * ML_QK_W), F32),
                        pltpu.VMEM((ML_HEADS, ML_DV + SUBLANES, ML_DQK), F32),
                        pltpu.VMEM((SUBLANES, LANES), F32)],
        compiler_params=_params("parallel", "arbitrary"),
    )(qk, vT, oT, smT, conv_w.astype(F32), bcol, hn)


def _block_diag_T(qT, heads, dh):
    rid = lax.broadcasted_iota(I32, qT.shape, 0)
    zero = jnp.zeros_like(qT)
    return jnp.concatenate(
        [jnp.where((rid >= h * dh) & (rid < (h + 1) * dh), qT, zero) for h in range(heads)], axis=1)


def _scores(k_chunk, qs, h, TQ):
    return jnp.dot(k_chunk, qs[:, h * TQ:(h + 1) * TQ], preferred_element_type=F32)


def _accumulate(h, p, vT_chunk, l, acc_scr, dh):
    rows = slice(h * dh, (h + 1) * dh)
    acc_scr[rows, :] += jnp.dot(vT_chunk[rows, :], p.astype(BF16), preferred_element_type=F32)
    return l + jnp.sum(p, axis=0, keepdims=True)


def _finish_attention(l, acc_scr, y_ref, heads, dh):
    outs = [acc_scr[h * dh:(h + 1) * dh, :] / l[h] for h in range(heads)]
    y_ref[0] = jnp.concatenate(outs, axis=0).T.astype(y_ref.dtype)


def _max_key_norm2(kb, heads, dh, TQ):
    ch = lax.broadcasted_iota(I32, (heads * dh, LANES), 0) // dh
    hd = lax.broadcasted_iota(I32, (heads * dh, LANES), 1)
    n2 = jnp.dot(kb * kb, (ch == hd).astype(F32), precision=lax.Precision.HIGHEST, preferred_element_type=F32)
    n2 = jnp.max(n2, axis=0, keepdims=True)
    return jnp.concatenate([jnp.broadcast_to(n2[:, h:h + 1], (1, TQ)) for h in range(heads)], axis=1)


def _norm_bound(qTf, k2, heads, dh, TQ):
    out = []
    for h in range(heads):
        qh = qTf[h * dh:(h + 1) * dh, :]
        q2 = jnp.sum(qh * qh, axis=0, keepdims=True)
        out.append(jnp.sqrt(q2 * k2[:, h * TQ:(h + 1) * TQ]) * (1.0 + 2.0 ** -8))
    return out


def _underflowed(l):
    small = l[0]
    for lh in l[1:]:
        small = jnp.minimum(small, lh)
    return jnp.logical_not(jnp.min(small) >= SOFTMAX_L_MIN)


def _moba_kernel(qT_ref, k_ref, vT_ref, y_ref, km_scr, kn_scr, acc_scr, s_scr, *, NB, NBP, n_sel):
    H, DH, BS = MOBA_HEADS, MOBA_DH, MOBA_BLOCK
    TQ = BS
    qi = pl.program_id(1)

    @pl.when(qi == 0)
    def _():
        km_scr[...] = jnp.zeros_like(km_scr)
        kn_scr[...] = jnp.zeros_like(kn_scr)

        def body(j, carry):
            kb = k_ref[0, pl.ds(pl.multiple_of(j * BS, BS), BS), :].astype(F32)
            km_scr[pl.ds(j, 1), :] = jnp.mean(kb, axis=0, keepdims=True)
            kn_scr[pl.ds(j, 1), :] = _max_key_norm2(kb, H, DH, TQ)
            return carry
        lax.fori_loop(0, NB, body, 0)

    qT = qT_ref[0]
    qTf = qT.astype(F32)
    km = km_scr[...]
    gate = jnp.concatenate(
        [jnp.dot(km[:, h * DH:(h + 1) * DH], qTf[h * DH:(h + 1) * DH, :], precision=lax.Precision.HIGHEST,
                 preferred_element_type=F32) for h in range(H)], axis=1) * (1.0 / ATT_QSCALE)
    blk = lax.broadcasted_iota(I32, gate.shape, 0)
    g = jnp.where(blk < qi, gate, NEG_INF)
    sels = []
    for _ in range(n_sel):
        mx = jnp.max(g, axis=0, keepdims=True)
        isel = jnp.min(jnp.where(g == mx, blk, NBP), axis=0, keepdims=True)
        sels.append(jnp.where(mx > NEG_INF, isel, -1))
        g = jnp.where(blk == isel, NEG_INF, g)

    qs = _block_diag_T(qT, H, DH)

    def hit_row(j):
        hit = sels[0] == j
        for s in sels[1:]:
            hit = hit | (s == j)
        return hit

    npair = (qi + 1) // 2
    blk_start = lambda j: pl.multiple_of(j * BS, BS)
    blk_scores = lambda j: jnp.dot(k_ref[0, pl.ds(blk_start(j), BS), :], qs, preferred_element_type=F32)
    hcols = lambda row, h: row[:, h * TQ:(h + 1) * TQ]
    causal = lax.broadcasted_iota(I32, (BS, TQ), 0) <= lax.broadcasted_iota(I32, (BS, TQ), 1)

    def diag_scores():
        s_all = blk_scores(qi)
        return [jnp.where(causal, hcols(s_all, h), NEG_INF) for h in range(H)]

    def max_sweep():
        def max_pair(g, m):
            for u in range(2):
                j = 2 * g + u
                m = jnp.maximum(m, jnp.where(hit_row(j), jnp.max(blk_scores(j), axis=0, keepdims=True), NEG_INF))
            return m
        m_all = lax.fori_loop(0, npair, max_pair, jnp.full((1, H * TQ), NEG_INF, F32))
        s_diag = diag_scores()
        return [jnp.maximum(hcols(m_all, h), jnp.max(s_diag[h], axis=0, keepdims=True)) for h in range(H)]

    def sum_sweep(m):
        acc_scr[...] = jnp.zeros_like(acc_scr)

        def sum_group(width):
            def body(g, l):
                for u in range(width):
                    s_scr[u] = blk_scores(width * g + u)
                for u in range(width):
                    j = width * g + u
                    hit = hit_row(j)
                    vT_blk = vT_ref[0, :, pl.ds(blk_start(j), BS)]
                    l = tuple(_accumulate(h, jnp.exp2(s_scr[u, :, h * TQ:(h + 1) * TQ]
                                                      - jnp.where(hcols(hit, h), m[h], jnp.inf)),
                                          vT_blk, l[h], acc_scr, DH) for h in range(H))
                return l
            return body
        nwide = qi // ATT_STAGE
        l = lax.fori_loop(0, nwide, sum_group(ATT_STAGE), (jnp.zeros((1, TQ), F32),) * H)
        l = lax.fori_loop((ATT_STAGE // 2) * nwide, npair, sum_group(2), l)
        s_diag = diag_scores()
        vT_diag = vT_ref[0, :, pl.ds(blk_start(qi), BS)]
        return [_accumulate(h, jnp.exp2(s_diag[h] - m[h]), vT_diag, l[h], acc_scr, DH) for h in range(H)]

    blk_n = lax.broadcasted_iota(I32, kn_scr.shape, 0)
    k2 = jnp.max(jnp.where(blk_n <= qi, kn_scr[...], 0.0), axis=0, keepdims=True)
    l = sum_sweep(_norm_bound(qTf, k2, H, DH, TQ))
    _finish_attention(l, acc_scr, y_ref, H, DH)

    @pl.when(_underflowed(l))
    def _():
        _finish_attention(sum_sweep(max_sweep()), acc_scr, y_ref, H, DH)


def _moba(qT, k, vT, B, S):
    BS = MOBA_BLOCK
    assert S % BS == 0
    NB = S // BS
    NBP = -(-NB // SUBLANES) * SUBLANES
    n_sel = max(min(MOBA_TOPK, NB - 1), 1)
    k = k.reshape(B, S, MOBA_W)
    return pl.pallas_call(
        functools.partial(_moba_kernel, NB=NB, NBP=NBP, n_sel=n_sel),
        grid=(B, NB),
        in_specs=[pl.BlockSpec((1, MOBA_W, BS), lambda b, i: (b, 0, i)),
                  pl.BlockSpec((1, S, MOBA_W), lambda b, i: (b, 0, 0)),
                  pl.BlockSpec((1, MOBA_W, S), lambda b, i: (b, 0, 0))],
        out_specs=pl.BlockSpec((1, BS, MOBA_W), lambda b, i: (b, i, 0)),
        out_shape=jax.ShapeDtypeStruct((B, S, MOBA_W), BF16),
        scratch_shapes=[pltpu.VMEM((NBP, MOBA_W), F32),
                        pltpu.VMEM((NBP, MOBA_HEADS * BS), F32),
                        pltpu.VMEM((MOBA_W, BS), F32),
                        pltpu.VMEM((ATT_STAGE, BS, MOBA_HEADS * BS), F32)],
        compiler_params=_params("parallel", "arbitrary"),
    )(qT, k, vT)


def _dsa_kernel(qT_ref, k_ref, vT_ref, iqT_ref, ik_ref, smT_ref, y_ref,
                e_scr, gmax_scr, acc_scr, thr_scr, kn_scr, s_scr, *, TQ, topk, spow, n_chunks):
    H, DH = DSA_HEADS, DSA_DH
    KC = TQ
    qi = pl.program_id(1)
    nch = qi + 1
    npair = (nch + 1) // 2
    chunk_start = lambda c: pl.multiple_of(c * KC, KC)

    @pl.when(qi == 0)
    def _():
        kn_scr[...] = jnp.zeros_like(kn_scr)

        def body(c, carry):
            kn_scr[pl.ds(c, 1), :] = _max_key_norm2(k_ref[0, pl.ds(chunk_start(c), KC), :].astype(F32), H, DH, TQ)
            return carry
        lax.fori_loop(0, n_chunks, body, 0)
    sum8 = lambda b: jnp.sum(b.astype(I32).reshape(KC // SUBLANES, SUBLANES, TQ), axis=0)
    any_row = lambda r: jnp.max(r.astype(I32)) > 0

    iqT = iqT_ref[0]
    iq_cat = jnp.concatenate([iqT[h * IDX_DH:(h + 1) * IDX_DH, :] for h in range(IDX_HEADS)], axis=1)
    iw = smT_ref[0][2 * ML_HEADS:2 * ML_HEADS + IDX_HEADS, :] * (IDX_HEADS ** -0.5 * IDX_DH ** -0.5)
    t_pos = qi * TQ + lax.broadcasted_iota(I32, (KC, TQ), 1)
    s_off = lax.broadcasted_iota(I32, (KC, TQ), 0)
    gmax_scr[...] = jnp.full((KC, TQ), NEG_INF, F32)

    def score_group(width, masked):
        def body(g, tally):
            for u in range(width):
                ikc = ik_ref[0, pl.ds(chunk_start(width * g + u), KC), :][:, :IDX_DH]
                s_scr[u] = jnp.dot(ikc, iq_cat, preferred_element_type=F32)
            for u in range(width):
                c = width * g + u
                sc = iw[0:1, :] * jnp.maximum(s_scr[u, :, 0:TQ], 0.0)
                for h in range(1, IDX_HEADS):
                    sc = sc + iw[h:h + 1, :] * jnp.maximum(s_scr[u, :, h * TQ:(h + 1) * TQ], 0.0)
                if masked:
                    sc = jnp.where(c * KC + s_off <= t_pos, sc, NEG_INF)
                e_scr[pl.ds(chunk_start(c), KC), :] = sc
                gmax_scr[...] = jnp.maximum(gmax_scr[...], sc)
                tally = tally + jnp.sum(jnp.where(sc > 0.0, 1, jnp.where(sc == 0.0, 1 << 16, 0))
                                        .reshape(KC // SUBLANES, SUBLANES, TQ), axis=0)
            return tally
        return body
    assert n_chunks * KC < (1 << 16)
    zero8 = jnp.zeros((SUBLANES, TQ), I32)
    tally = lax.fori_loop(0, (npair - 1) // 2, score_group(4, False), zero8)
    tally = lax.fori_loop(2 * ((npair - 1) // 2), npair - 1, score_group(2, False), tally)
    tally = jnp.sum(score_group(2, True)(npair - 1, tally), axis=0, keepdims=True)
    n_pos = tally & 0xFFFF
    n_nonneg = n_pos + (tally >> 16)

    def sweep_keys(fold, merge, init):
        def group(width):
            def body(g, acc):
                return tuple(merge(acc[u], fold(e_scr[pl.ds(chunk_start(width * g + u), KC), :]))
                             for u in range(width))
            return body
        acc = lax.fori_loop(0, npair // 2, group(4), (init,) * 4)
        a, b = lax.fori_loop(2 * (npair // 2), npair, group(2), (merge(acc[0], acc[2]), merge(acc[1], acc[3])))
        return merge(a, b)

    def count_gt(cand):
        return jnp.sum(sweep_keys(lambda x: sum8(x > cand), lambda p, q: p + q, zero8), axis=0, keepdims=True)

    def min_gt(cand):
        fold = lambda x: jnp.min(jnp.where(x > cand, x, jnp.inf).reshape(KC // SUBLANES, SUBLANES, TQ), axis=0)
        return jnp.min(sweep_keys(fold, jnp.minimum, jnp.full((SUBLANES, TQ), jnp.inf, F32)), axis=0, keepdims=True)

    def rewrite(rows, thr):
        def body(c, carry):
            x = e_scr[pl.ds(chunk_start(c), KC), :]
            rank = (2 * spow - (c * KC + s_off)).astype(F32)
            new = jnp.where(x > thr, jnp.inf, jnp.where(x == thr, rank, NEG_INF))
            e_scr[pl.ds(chunk_start(c), KC), :] = jnp.where(rows, new, x)
            return carry
        lax.fori_loop(0, nch, body, 0)

    def halve(_, c):
        lo, hi, act, tie, n_lo = c
        mid = lo + (hi - lo) * 0.5
        inside = (mid > lo) & (mid < hi)
        cnt = count_gt(mid)
        live = act > 0.0
        run = live & inside
        up = run & (cnt >= topk)
        lo = jnp.where(up, mid, lo)
        n_lo = jnp.where(up, cnt, n_lo)
        hi = jnp.where(run & (cnt < topk), mid, hi)
        collapsed = live & jnp.logical_not(inside)
        tie = jnp.where(collapsed, 1.0, tie)
        act = jnp.where(collapsed | (run & (cnt == topk)), 0.0, act)
        return lo, hi, act, tie, n_lo

    def excess(state):
        return jnp.max(jnp.where(state[2] > 0.0, state[4] - topk, 0).astype(F32))

    def bisect(lo, hi, act, warmup):
        state = (lo, hi, act, jnp.zeros_like(act), jnp.full((1, TQ), 1 << 20, I32))
        state = lax.fori_loop(0, warmup, halve, state)

        def one_more(c):
            state = halve(0, c[1])
            return c[0] + 1, state, excess(state)
        near = lax.while_loop(lambda c: jnp.logical_and(c[0] < BISECT_CAP, c[2] > BISECT_DROP), one_more,
                              (jnp.int32(warmup), state, excess(state)))
        lo, hi, act, tie, n_lo = near[1]
        cur, left = lo, jnp.where(act > 0.0, n_lo - topk, 0)
        for _ in range(BISECT_DROP):
            cur = jnp.where(left > 0, min_gt(cur), cur)
            left = left - 1
        done = (act > 0.0) & (count_gt(cur) == topk)
        lo = jnp.where(done, cur, lo)
        act = jnp.where(done, 0.0, act)

        def body(c):
            state = c[1]
            for _ in range(BISECT_STEPS_PER_CHECK):
                state = halve(0, state)
            return c[0] + BISECT_STEPS_PER_CHECK, state, jnp.max(state[2])
        out = lax.while_loop(lambda c: jnp.logical_and(c[0] < BISECT_CAP, c[2] > 0.0), body,
                             (near[0], (lo, hi, act, tie, n_lo), jnp.max(act)))
        return out[1][0], out[1][1], out[1][3]

    fmax = float(jnp.finfo(F32).max)
    gm = gmax_scr[...]
    g_lo = jnp.min(gm, axis=0, keepdims=True)
    g_hi = jnp.max(gm, axis=0, keepdims=True)
    n_vis = qi * TQ + lax.broadcasted_iota(I32, (1, TQ), 1) + 1
    few = n_vis <= topk
    zero_tie = (n_pos < topk) & (n_nonneg >= topk) & jnp.logical_not(few)

    @pl.when(any_row(zero_tie))
    def _():
        rewrite(zero_tie, jnp.zeros((1, TQ), F32))

    rank_lo, rank_hi = float(spow), float(2 * spow + 1)
    below = jnp.maximum(g_lo, -fmax)
    below = below - jnp.abs(below) * (2.0 ** -10) - 1e-30
    lo0 = jnp.where(zero_tie, rank_lo, jnp.maximum(below, -fmax))
    hi0 = jnp.where(zero_tie, rank_hi, g_hi)
    lo, hi, tie = bisect(lo0, hi0, jnp.where(few, 0.0, 1.0), BISECT_WARMUP)
    thr_scr[0:1, :] = jnp.where(few, NEG_INF, lo)

    @pl.when(jnp.max(tie) > 0.0)
    def _():
        rows = tie > 0.0
        rewrite(rows, hi)
        lo2, _, _ = bisect(jnp.full((1, TQ), rank_lo, F32), jnp.full((1, TQ), rank_hi, F32), tie, 0)
        thr_scr[0:1, :] = jnp.where(rows, lo2, thr_scr[0:1, :])

    thr = thr_scr[0:1, :]
    qT = qT_ref[0]
    qs = _block_diag_T(qT, H, DH)
    hcols = lambda row, h: row[:, h * TQ:(h + 1) * TQ]

    def chunk_scores(c):
        start = chunk_start(c)
        sel = e_scr[pl.ds(start, KC), :] > thr
        s_all = jnp.dot(k_ref[0, pl.ds(start, KC), :], qs, preferred_element_type=F32)
        return sel, s_all

    def max_sweep():
        def max_pair(g, m):
            for u in range(2):
                sel, s_all = chunk_scores(2 * g + u)
                m = tuple(jnp.maximum(m[h], jnp.max(jnp.where(sel, hcols(s_all, h), NEG_INF), axis=0, keepdims=True))
                          for h in range(H))
            return m
        m = lax.fori_loop(0, npair, max_pair, (jnp.full((1, TQ), NEG_INF, F32),) * H)
        return [jnp.where(mh == NEG_INF, 0.0, mh) for mh in m]

    def sum_sweep(m):
        acc_scr[...] = jnp.zeros_like(acc_scr)

        def sum_group(width):
            def body(g, l):
                for u in range(width):
                    s_scr[u] = jnp.dot(k_ref[0, pl.ds(chunk_start(width * g + u), KC), :], qs,
                                       preferred_element_type=F32)
                for u in range(width):
                    start = chunk_start(width * g + u)
                    sel = e_scr[pl.ds(start, KC), :] > thr
                    vT_chunk = vT_ref[0, :, pl.ds(start, KC)]
                    l = tuple(_accumulate(h, jnp.exp2(s_scr[u, :, h * TQ:(h + 1) * TQ] + jnp.where(sel, -m[h], NEG_INF)),
                                          vT_chunk, l[h], acc_scr, DH) for h in range(H))
                return l
            return body
        nwide = (2 * npair) // ATT_STAGE
        l = lax.fori_loop(0, nwide, sum_group(ATT_STAGE), (jnp.zeros((1, TQ), F32),) * H)
        return lax.fori_loop((ATT_STAGE // 2) * nwide, npair, sum_group(2), l)

    chunk_n = lax.broadcasted_iota(I32, kn_scr.shape, 0)
    k2 = jnp.max(jnp.where(chunk_n < 2 * npair, kn_scr[...], 0.0), axis=0, keepdims=True)
    l = sum_sweep(_norm_bound(qT.astype(F32), k2, H, DH, TQ))
    _finish_attention(l, acc_scr, y_ref, H, DH)

    @pl.when(_underflowed(l))
    def _():
        _finish_attention(sum_sweep(max_sweep()), acc_scr, y_ref, H, DH)


def _dsa(qT, k, vT, iqT, ik, smT, B, S):
    TQ = min(ATT_TQ, S)
    topk = min(DSA_TOPK_MAX, S // 4)
    assert S % (2 * TQ) == 0 and topk <= TQ
    spow = 1 << max((S - 1).bit_length(), 1)
    k = k.reshape(B, S, DSA_W)
    ik = ik.reshape(B, S, LANES)
    qblk = lambda n: pl.BlockSpec((1, n, TQ), lambda b, i: (b, 0, i))
    return pl.pallas_call(
        functools.partial(_dsa_kernel, TQ=TQ, topk=topk, spow=spow, n_chunks=S // TQ),
        grid=(B, S // TQ),
        in_specs=[qblk(DSA_W),
                  pl.BlockSpec((1, S, DSA_W), lambda b, i: (b, 0, 0)),
                  pl.BlockSpec((1, DSA_W, S), lambda b, i: (b, 0, 0)),
                  qblk(IDX_HEADS * IDX_DH),
                  pl.BlockSpec((1, S, LANES), lambda b, i: (b, 0, 0)),
                  qblk(16)],
        out_specs=pl.BlockSpec((1, TQ, DSA_W), lambda b, i: (b, i, 0)),
        out_shape=jax.ShapeDtypeStruct((B, S, DSA_W), BF16),
        scratch_shapes=[pltpu.VMEM((S, TQ), F32),
                        pltpu.VMEM((TQ, TQ), F32),
                        pltpu.VMEM((DSA_W, TQ), F32),
                        pltpu.VMEM((SUBLANES, TQ), F32),
                        pltpu.VMEM((-(-(S // TQ) // SUBLANES) * SUBLANES, DSA_HEADS * TQ), F32),
                        pltpu.VMEM((ATT_STAGE, TQ, DSA_HEADS * TQ), F32)],
        compiler_params=_params("parallel", "arbitrary"),
    )(qT, k, vT, iqT, ik, smT)


def _out_mlp_kernel(yml_ref, ymb_ref, yds_ref, x_ref, wout_ref, gpost_ref, gpre_ref,
                    w1_ref, w2_ref, g2_ref, out_ref, x1_scr, h_scr, acc_scr):
    f = pl.program_id(1)

    @pl.when(f == 0)
    def _():
        slab = x_ref.shape[0] // MLP_SLABS
        for r in range(MLP_SLABS):
            rows = slice(r * slab, (r + 1) * slab)
            mix = jnp.dot(yml_ref[rows, :], wout_ref[0:ML_V_W, :], preferred_element_type=F32)
            mix = mix + jnp.dot(ymb_ref[rows, :], wout_ref[ML_V_W:ML_V_W + MOBA_W, :], preferred_element_type=F32)
            mix = mix + jnp.dot(yds_ref[rows, :], wout_ref[ML_V_W + MOBA_W:, :], preferred_element_type=F32)
            x1 = x_ref[rows, :] + _rms(mix, gpost_ref[...])
            x1_scr[rows, :] = x1
            h_scr[rows, :] = _rms(x1, gpre_ref[...]).astype(BF16)
        acc_scr[...] = jnp.zeros_like(acc_scr)

    u = jnp.maximum(jnp.dot(h_scr[...], w1_ref[...], preferred_element_type=F32), 0.0)
    acc_scr[...] += jnp.dot((u * u).astype(BF16), w2_ref[...], preferred_element_type=F32)

    @pl.when(f == pl.num_programs(1) - 1)
    def _():
        out_ref[...] = x1_scr[...] + _rms(acc_scr[...], g2_ref[...])


def _out_mlp(yml, ymb, yds, x2, w_out, g_post, g_pre, w1, w2, g2):
    T, D = x2.shape
    tm = min(MLP_TM, T)
    tf = min(MLP_TF, D_FF)
    row = lambda n: pl.BlockSpec((tm, n), lambda i, f: (i, 0))
    full = lambda a: pl.BlockSpec(a.shape, lambda i, f: (0,) * a.ndim)
    return pl.pallas_call(
        _out_mlp_kernel,
        grid=(T // tm, D_FF // tf),
        in_specs=[row(ML_V_W), row(MOBA_W), row(DSA_W), row(D), full(w_out), full(g_post), full(g_pre),
                  pl.BlockSpec((D, tf), lambda i, f: (0, f)),
                  pl.BlockSpec((tf, D), lambda i, f: (f, 0)),
                  full(g2)],
        out_specs=row(D),
        out_shape=jax.ShapeDtypeStruct((T, D), F32),
        scratch_shapes=[pltpu.VMEM((tm, D), F32), pltpu.VMEM((tm, D), BF16), pltpu.VMEM((tm, D), F32)],
        compiler_params=_params("parallel", "arbitrary"),
    )(yml, ymb, yds, x2, w_out, g_post, g_pre, w1, w2, g2)


def kernel(x, norm_mix_pre, w_in, ml_conv, ml_i_bias, ml_f_bias, ml_head_norm, w_out, norm_mix_post,
           norm_mlp_pre, w_ff1, w_ff2, norm_mlp_post):
    B, S, D = x.shape
    depth = w_in.shape[0]
    x2 = x.reshape(B * S, D)
    gain = lambda g: g.reshape(1, D).astype(F32)
    for l in range(depth):
        wn, wt = _prep_in_weights(w_in[l])
        (qk, mbk, dsk, ixk, mbqT, mbvT, dsqT, dsvT, ixqT, mlvT, mloT, smT) = _in_proj(
            x2, gain(norm_mix_pre[l]), wn, wt, B, S)
        y_ml = _mlstm(qk, mlvT, mloT, smT, ml_conv[l], ml_i_bias[l], ml_f_bias[l], ml_head_norm[l], B, S)
        y_mb = _moba(mbqT, mbk, mbvT, B, S)
        y_ds = _dsa(dsqT, dsk, dsvT, ixqT, ixk, smT, B, S)
        x2 = _out_mlp(y_ml.reshape(B * S, ML_V_W), y_mb.reshape(B * S, MOBA_W), y_ds.reshape(B * S, DSA_W),
                      x2, w_out[l].astype(BF16), gain(norm_mix_post[l]), gain(norm_mlp_pre[l]),
                      w_ff1[l].astype(BF16), w_ff2[l].astype(BF16), gain(norm_mlp_post[l]))
    return x2.reshape(B, S, D)
```

```python
import functools

import jax
import jax.numpy as jnp
from jax import lax
from jax.experimental import pallas as pl
from jax.experimental.pallas import tpu as pltpu

F32 = jnp.float32
BF16 = jnp.bfloat16
I32 = jnp.int32

D_MODEL = 1024
ML_HEADS, ML_DQK, ML_DV, ML_CONV = 4, 64, 128, 4
MOBA_HEADS, MOBA_DH, MOBA_BLOCK, MOBA_TOPK = 4, 64, 256, 3
DSA_HEADS, DSA_DH, IDX_HEADS, IDX_DH, DSA_TOPK_MAX = 4, 64, 4, 64, 256
D_FF = 4 * D_MODEL
EPS = 1e-6

ML_QK_W = ML_HEADS * ML_DQK
ML_V_W = ML_HEADS * ML_DV
MOBA_W = MOBA_HEADS * MOBA_DH
DSA_W = DSA_HEADS * DSA_DH
IN_SPLITS = (ML_QK_W, ML_QK_W, ML_V_W, ML_V_W, ML_HEADS, ML_HEADS,
             MOBA_W, MOBA_W, MOBA_W,
             DSA_W, DSA_W, DSA_W, IDX_HEADS * IDX_DH, IDX_DH, IDX_HEADS)

LANES = 128
SUBLANES = 8
VMEM_LIMIT = 52 * 1024 * 1024

ML_CHUNK = 512
ATT_TQ = 256
ATT_STAGE = 8
PROJ_TM = 1024
MLP_TM = 1024
MLP_TF = 512
MLP_SLABS = 4

NEG_INF = float("-inf")
BISECT_CAP = 300
BISECT_WARMUP = 12
BISECT_DROP = 2
BISECT_STEPS_PER_CHECK = 2
SOFTMAX_L_MIN = 1e-30
assert MOBA_DH == DSA_DH
ATT_QSCALE = MOBA_DH ** -0.5 * 1.4426950408889634

_NT = (((1,), (1,)), ((), ()))


def _rms(x, g):
    return x * lax.rsqrt(jnp.mean(x * x, axis=-1, keepdims=True) + EPS) * g


def _log_sigmoid(x):
    return jnp.minimum(x, 0.0) - jnp.log1p(jnp.exp(-jnp.abs(x)))


def _params(*sem):
    return pltpu.CompilerParams(dimension_semantics=sem, vmem_limit_bytes=VMEM_LIMIT)


_N_QK, _N_MBK, _N_DSK, _N_IXK = 0, 512, 768, 1024
_N_TOTAL = 1152
_T_MBQ, _T_MBV, _T_DSQ, _T_DSV, _T_IXQ, _T_MLV, _T_MLO, _T_SM = 0, 256, 512, 768, 1024, 1280, 1792, 2304
_T_TOTAL = 2320


def _in_proj_kernel(x_ref, g_ref, wn_ref, wt_ref,
                    qk_ref, mbk_ref, dsk_ref, ixk_ref,
                    mbqT_ref, mbvT_ref, dsqT_ref, dsvT_ref, ixqT_ref, mlvT_ref, mloT_ref, smT_ref):
    h = _rms(x_ref[...], g_ref[...]).astype(BF16)

    def mm(a, n):
        return jnp.dot(h, wn_ref[:, a:a + n], preferred_element_type=F32)

    def mt(a, n):
        return lax.dot_general(wt_ref[a:a + n, :], h, _NT, preferred_element_type=F32)

    qk_ref[...] = mm(_N_QK, 512)
    mbk_ref[...] = mm(_N_MBK, 256).astype(BF16)
    dsk_ref[...] = mm(_N_DSK, 256).astype(BF16)
    ixk_ref[...] = mm(_N_IXK, 128).astype(BF16)
    mbqT_ref[0] = (mt(_T_MBQ, 256) * ATT_QSCALE).astype(BF16)
    mbvT_ref[0] = mt(_T_MBV, 256).astype(BF16)
    dsqT_ref[0] = (mt(_T_DSQ, 256) * ATT_QSCALE).astype(BF16)
    dsvT_ref[0] = mt(_T_DSV, 256).astype(BF16)
    ixqT_ref[0] = mt(_T_IXQ, 256).astype(BF16)
    mlvT_ref[0] = mt(_T_MLV, 512).astype(BF16)
    mloT_ref[0] = mt(_T_MLO, 512)
    smT_ref[0] = mt(_T_SM, 16)


def _prep_in_weights(w_in):
    pts, acc = [], 0
    for n in IN_SPLITS:
        pts.append((acc, acc + n))
        acc += n
    col = lambda i: w_in[:, pts[i][0]:pts[i][1]]
    (ml_q, ml_k, ml_v, ml_o, ml_i, ml_f, mb_q, mb_k, mb_v,
     ds_q, ds_k, ds_v, ix_q, ix_k, ix_w) = [col(i) for i in range(len(IN_SPLITS))]
    d = w_in.shape[0]
    small = jnp.concatenate([ml_i, ml_f, ix_w], axis=1)
    wn = jnp.concatenate([
        ml_q, ml_k, mb_k, ds_k,
        ix_k, jnp.zeros((d, 128 - IDX_DH), w_in.dtype)], axis=1)
    wt = jnp.concatenate([
        mb_q, mb_v, ds_q, ds_v, ix_q, ml_v, ml_o,
        small, jnp.zeros((d, 16 - small.shape[1]), w_in.dtype)], axis=1).T
    assert wn.shape == (d, _N_TOTAL) and wt.shape == (_T_TOTAL, d)
    return wn.astype(BF16), wt.astype(BF16)


def _in_proj(x2, g, wn, wt, B, S):
    T, D = x2.shape
    tm = min(PROJ_TM, S)
    nsb = S // tm
    row = lambda n: pl.BlockSpec((tm, n), lambda i: (i, 0))
    tr = lambda n: pl.BlockSpec((1, n, tm), lambda i: (i // nsb, 0, i % nsb))
    full = lambda a: pl.BlockSpec(a.shape, lambda i: (0,) * a.ndim)
    out_shape = (
        jax.ShapeDtypeStruct((T, 512), F32),
        jax.ShapeDtypeStruct((T, 256), BF16),
        jax.ShapeDtypeStruct((T, 256), BF16),
        jax.ShapeDtypeStruct((T, 128), BF16),
        jax.ShapeDtypeStruct((B, 256, S), BF16),
        jax.ShapeDtypeStruct((B, 256, S), BF16),
        jax.ShapeDtypeStruct((B, 256, S), BF16),
        jax.ShapeDtypeStruct((B, 256, S), BF16),
        jax.ShapeDtypeStruct((B, 256, S), BF16),
        jax.ShapeDtypeStruct((B, 512, S), BF16),
        jax.ShapeDtypeStruct((B, 512, S), F32),
        jax.ShapeDtypeStruct((B, 16, S), F32),
    )
    out_specs = (row(512), row(256), row(256), row(128),
                 tr(256), tr(256), tr(256), tr(256), tr(256), tr(512), tr(512), tr(16))
    return pl.pallas_call(
        _in_proj_kernel,
        grid=(T // tm,),
        in_specs=[row(D), full(g), full(wn), full(wt)],
        out_specs=out_specs,
        out_shape=out_shape,
        compiler_params=_params("parallel"),
    )(x2, g, wn, wt)


def _dot3(a_f32, b_exact):
    b = b_exact.astype(BF16)
    hi = a_f32.astype(BF16)
    r1 = a_f32 - hi.astype(F32)
    mid = r1.astype(BF16)
    lo = (r1 - mid.astype(F32)).astype(BF16)
    return (jnp.dot(hi, b, preferred_element_type=F32) + jnp.dot(mid, b, preferred_element_type=F32)
            + jnp.dot(lo, b, preferred_element_type=F32))


def _mlstm_kernel(qk_ref, vT_ref, oT_ref, smT_ref, conv_ref, bcol_ref, hn_ref,
                  y_ref, xp_scr, st_scr, m_scr, *, L):
    c = pl.program_id(1)
    halo = SUBLANES
    DK, DV = ML_DQK, ML_DV

    @pl.when(c == 0)
    def _():
        xp_scr[0:halo, :] = jnp.zeros((halo, 2 * ML_QK_W), F32)
        st_scr[...] = jnp.zeros_like(st_scr)
        m_scr[...] = jnp.zeros_like(m_scr)

    cur = qk_ref[0]
    xp_scr[halo:halo + L, :] = cur
    base = halo - (ML_CONV - 1)
    acc = conv_ref[0:1, :] * xp_scr[base:base + L, :]
    for j in range(1, ML_CONV):
        acc = acc + conv_ref[j:j + 1, :] * xp_scr[base + j:base + j + L, :]
    xp_scr[0:halo, :] = cur[L - halo:L, :]
    qk = acc * jax.nn.sigmoid(acc)
    qT_all = (qk[:, :ML_QK_W] * (DK ** -0.5)).T.astype(BF16)
    k_all = qk[:, ML_QK_W:].astype(BF16)

    grow = smT_ref[0][0:SUBLANES, :] + bcol_ref[...]
    si = lax.broadcasted_iota(I32, (L, L), 0)
    ji = lax.broadcasted_iota(I32, (L, L), 1)
    causal = si <= ji
    b_row = _dot3(_log_sigmoid(grow), causal)

    for h in range(ML_HEADS):
        qT = qT_all[h * DK:(h + 1) * DK, :]
        kh = k_all[:, h * DK:(h + 1) * DK]
        vT = vT_ref[0, h * DV:(h + 1) * DV, :]
        b_j = b_row[ML_HEADS + h:ML_HEADS + h + 1, :]
        c_row = grow[h:h + 1, :] - b_j
        c_col = jnp.broadcast_to(c_row, (LANES, L)).T
        dlog = jnp.where(causal, b_j + jnp.concatenate([c_col] * (L // LANES), axis=1), NEG_INF)
        m0 = m_scr[h:h + 1, 0:1]
        inter = b_j + m0
        m_t = jnp.maximum(inter, jnp.max(dlog, axis=0, keepdims=True))
        w_inter = jnp.exp(inter - m_t)
        w_intra = jnp.exp(dlog - m_t) * jnp.dot(kh, qT, preferred_element_type=F32)
        st = st_scr[h]
        cq = jnp.dot(st.astype(BF16), qT, preferred_element_type=F32)
        num = w_inter * cq[:DV] + jnp.dot(vT, w_intra.astype(BF16), preferred_element_type=F32)
        den = w_inter * cq[DV:DV + 1] + jnp.sum(w_intra, axis=0, keepdims=True)
        hh = num / jnp.maximum(jnp.abs(den), jnp.exp(-m_t))
        hh = hh * lax.rsqrt(jnp.mean(hh * hh, axis=0, keepdims=True) + EPS)
        rows = slice(h * DV, (h + 1) * DV)
        y_ref[0, :, rows] = (jax.nn.sigmoid(oT_ref[0, rows, :]) * hh * hn_ref[rows, :]).T.astype(y_ref.dtype)

        b_last = b_j[:, L - 1:L]
        a = b_last + c_row
        m_loc = jnp.max(a, axis=1, keepdims=True)
        wa = jnp.exp(a - m_loc)
        lhs = jnp.concatenate([(vT.astype(F32) * wa).astype(BF16),
                               jnp.broadcast_to(wa, (SUBLANES, L)).astype(BF16)], axis=0)
        c_loc = jnp.dot(lhs, kh, preferred_element_type=F32)
        m_new = jnp.maximum(b_last + m0, m_loc)
        s_old = jnp.exp(b_last + m0 - m_new)
        s_loc = jnp.exp(m_loc - m_new)
        st_scr[h] = s_old * st + s_loc * c_loc
        m_scr[h:h + 1, :] = jnp.broadcast_to(m_new, (1, LANES))


def _mlstm(qk, vT, oT, smT, conv_w, i_bias, f_bias, head_norm, B, S):
    L = min(ML_CHUNK, S)
    assert L % LANES == 0
    nc = S // L
    qk = qk.reshape(B, S, 2 * ML_QK_W)
    bcol = jnp.concatenate([i_bias, f_bias]).astype(F32).reshape(2 * ML_HEADS, 1)
    hn = jnp.broadcast_to(head_norm.astype(F32)[:, None], (ML_V_W, L))
    blkT = lambda n: pl.BlockSpec((1, n, L), lambda b, c: (b, 0, c))
    full = lambda a: pl.BlockSpec(a.shape, lambda b, c: (0,) * a.ndim)
    return pl.pallas_call(
        functools.partial(_mlstm_kernel, L=L),
        grid=(B, nc),
        in_specs=[pl.BlockSpec((1, L, 2 * ML_QK_W), lambda b, c: (b, c, 0)),
                  blkT(ML_V_W), blkT(ML_V_W), blkT(16), full(conv_w), full(bcol), full(hn)],
        out_specs=pl.BlockSpec((1, L, ML_V_W), lambda b, c: (b, c, 0)),
        out_shape=jax.ShapeDtypeStruct((B, S, ML_V_W), BF16),
        scratch_shapes=[pltpu.VMEM((L + SUBLANES, 2 * ML_QK_W), F32),
                        pltpu.VMEM((ML_HEADS, ML_DV + SUBLANES, ML_DQK), F32),
                        pltpu.VMEM((SUBLANES, LANES), F32)],
        compiler_params=_params("parallel", "arbitrary"),
    )(qk, vT, oT, smT, conv_w.astype(F32), bcol, hn)


def _block_diag_T(qT, heads, dh):
    rid = lax.broadcasted_iota(I32, qT.shape, 0)
    zero = jnp.zeros_like(qT)
    return jnp.concatenate(
        [jnp.where((rid >= h * dh) & (rid < (h + 1) * dh), qT, zero) for h in range(heads)], axis=1)


def _accumulate(h, p, vT_chunk, l, acc_scr, dh):
    rows = slice(h * dh, (h + 1) * dh)
    acc_scr[rows, :] += jnp.dot(vT_chunk[rows, :], p.astype(BF16), preferred_element_type=F32)
    return l + jnp.sum(p, axis=0, keepdims=True)


def _finish_attention(l, acc_scr, y_ref, heads, dh):
    outs = [acc_scr[h * dh:(h + 1) * dh, :] / l[h] for h in range(heads)]
    y_ref[0] = jnp.concatenate(outs, axis=0).T.astype(y_ref.dtype)


def _max_key_norm2(kb, heads, dh, TQ):
    ch = lax.broadcasted_iota(I32, (heads * dh, LANES), 0) // dh
    hd = lax.broadcasted_iota(I32, (heads * dh, LANES), 1)
    n2 = jnp.dot(kb * kb, (ch == hd).astype(F32), precision=lax.Precision.HIGHEST, preferred_element_type=F32)
    n2 = jnp.max(n2, axis=0, keepdims=True)
    return jnp.concatenate([jnp.broadcast_to(n2[:, h:h + 1], (1, TQ)) for h in range(heads)], axis=1)


def _norm_bound(qTf, k2, heads, dh, TQ):
    out = []
    for h in range(heads):
        qh = qTf[h * dh:(h + 1) * dh, :]
        q2 = jnp.sum(qh * qh, axis=0, keepdims=True)
        out.append(jnp.sqrt(q2 * k2[:, h * TQ:(h + 1) * TQ]) * (1.0 + 2.0 ** -8))
    return out


def _underflowed(l):
    small = l[0]
    for lh in l[1:]:
        small = jnp.minimum(small, lh)
    return jnp.logical_not(jnp.min(small) >= SOFTMAX_L_MIN)


def _moba_kernel(qT_ref, k_ref, vT_ref, y_ref, km_scr, kn_scr, acc_scr, s_scr, *, NB, NBP, n_sel):
    H, DH, BS = MOBA_HEADS, MOBA_DH, MOBA_BLOCK
    TQ = BS
    qi = pl.program_id(1)

    @pl.when(qi == 0)
    def _():
        km_scr[...] = jnp.zeros_like(km_scr)
        kn_scr[...] = jnp.zeros_like(kn_scr)

        def body(j, carry):
            kb = k_ref[0, pl.ds(pl.multiple_of(j * BS, BS), BS), :].astype(F32)
            km_scr[pl.ds(j, 1), :] = jnp.mean(kb, axis=0, keepdims=True)
            kn_scr[pl.ds(j, 1), :] = _max_key_norm2(kb, H, DH, TQ)
            return carry
        lax.fori_loop(0, NB, body, 0)

    qT = qT_ref[0]
    qTf = qT.astype(F32)
    km = km_scr[...]
    gate = jnp.concatenate(
        [jnp.dot(km[:, h * DH:(h + 1) * DH], qTf[h * DH:(h + 1) * DH, :], precision=lax.Precision.HIGHEST,
                 preferred_element_type=F32) for h in range(H)], axis=1) * (1.0 / ATT_QSCALE)
    blk = lax.broadcasted_iota(I32, gate.shape, 0)
    g = jnp.where(blk < qi, gate, NEG_INF)
    sels = []
    for _ in range(n_sel):
        mx = jnp.max(g, axis=0, keepdims=True)
        isel = jnp.min(jnp.where(g == mx, blk, NBP), axis=0, keepdims=True)
        sels.append(jnp.where(mx > NEG_INF, isel, -1))
        g = jnp.where(blk == isel, NEG_INF, g)

    qs = _block_diag_T(qT, H, DH)

    def hit_row(j):
        hit = sels[0] == j
        for s in sels[1:]:
            hit = hit | (s == j)
        return hit

    npair = (qi + 1) // 2
    blk_start = lambda j: pl.multiple_of(j * BS, BS)
    blk_scores = lambda j: jnp.dot(k_ref[0, pl.ds(blk_start(j), BS), :], qs, preferred_element_type=F32)
    hcols = lambda row, h: row[:, h * TQ:(h + 1) * TQ]
    causal = lax.broadcasted_iota(I32, (BS, TQ), 0) <= lax.broadcasted_iota(I32, (BS, TQ), 1)

    def diag_scores():
        s_all = blk_scores(qi)
        return [jnp.where(causal, hcols(s_all, h), NEG_INF) for h in range(H)]

    def max_sweep():
        def max_pair(g, m):
            for u in range(2):
                j = 2 * g + u
                m = jnp.maximum(m, jnp.where(hit_row(j), jnp.max(blk_scores(j), axis=0, keepdims=True), NEG_INF))
            return m
        m_all = lax.fori_loop(0, npair, max_pair, jnp.full((1, H * TQ), NEG_INF, F32))
        s_diag = diag_scores()
        return [jnp.maximum(hcols(m_all, h), jnp.max(s_diag[h], axis=0, keepdims=True)) for h in range(H)]

    def sum_sweep(m):
        acc_scr[...] = jnp.zeros_like(acc_scr)

        def sum_group(width):
            def body(g, l):
                for u in range(width):
                    s_scr[u] = blk_scores(width * g + u)
                for u in range(width):
                    j = width * g + u
                    hit = hit_row(j)
                    vT_blk = vT_ref[0, :, pl.ds(blk_start(j), BS)]
                    l = tuple(_accumulate(h, jnp.exp2(s_scr[u, :, h * TQ:(h + 1) * TQ]
                                                      - jnp.where(hcols(hit, h), m[h], jnp.inf)),
                                          vT_blk, l[h], acc_scr, DH) for h in range(H))
                return l
            return body
        nwide = qi // ATT_STAGE
        l = lax.fori_loop(0, nwide, sum_group(ATT_STAGE), (jnp.zeros((1, TQ), F32),) * H)
        l = lax.fori_loop((ATT_STAGE // 2) * nwide, npair, sum_group(2), l)
        s_diag = diag_scores()
        vT_diag = vT_ref[0, :, pl.ds(blk_start(qi), BS)]
        return [_accumulate(h, jnp.exp2(s_diag[h] - m[h]), vT_diag, l[h], acc_scr, DH) for h in range(H)]

    blk_n = lax.broadcasted_iota(I32, kn_scr.shape, 0)
    k2 = jnp.max(jnp.where(blk_n <= qi, kn_scr[...], 0.0), axis=0, keepdims=True)
    l = sum_sweep(_norm_bound(qTf, k2, H, DH, TQ))
    _finish_attention(l, acc_scr, y_ref, H, DH)

    @pl.when(_underflowed(l))
    def _():
        _finish_attention(sum_sweep(max_sweep()), acc_scr, y_ref, H, DH)


def _moba(qT, k, vT, B, S):
    BS = MOBA_BLOCK
    assert S % BS == 0
    NB = S // BS
    NBP = -(-NB // SUBLANES) * SUBLANES
    n_sel = max(min(MOBA_TOPK, NB - 1), 1)
    k = k.reshape(B, S, MOBA_W)
    return pl.pallas_call(
        functools.partial(_moba_kernel, NB=NB, NBP=NBP, n_sel=n_sel),
        grid=(B, NB),
        in_specs=[pl.BlockSpec((1, MOBA_W, BS), lambda b, i: (b, 0, i)),
                  pl.BlockSpec((1, S, MOBA_W), lambda b, i: (b, 0, 0)),
                  pl.BlockSpec((1, MOBA_W, S), lambda b, i: (b, 0, 0))],
        out_specs=pl.BlockSpec((1, BS, MOBA_W), lambda b, i: (b, i, 0)),
        out_shape=jax.ShapeDtypeStruct((B, S, MOBA_W), BF16),
        scratch_shapes=[pltpu.VMEM((NBP, MOBA_W), F32),
                        pltpu.VMEM((NBP, MOBA_HEADS * BS), F32),
                        pltpu.VMEM((MOBA_W, BS), F32),
                        pltpu.VMEM((ATT_STAGE, BS, MOBA_HEADS * BS), F32)],
        compiler_params=_params("parallel", "arbitrary"),
    )(qT, k, vT)


def _dsa_kernel(qT_ref, k_ref, vT_ref, iqT_ref, ik_ref, smT_ref, y_ref,
                e_scr, gmax_scr, acc_scr, thr_scr, kn_scr, s_scr, *, TQ, topk, spow, n_chunks):
    H, DH = DSA_HEADS, DSA_DH
    KC = TQ
    qi = pl.program_id(1)
    nch = qi + 1
    npair = (nch + 1) // 2
    chunk_start = lambda c: pl.multiple_of(c * KC, KC)

    @pl.when(qi == 0)
    def _():
        kn_scr[...] = jnp.zeros_like(kn_scr)

        def body(c, carry):
            kn_scr[pl.ds(c, 1), :] = _max_key_norm2(k_ref[0, pl.ds(chunk_start(c), KC), :].astype(F32), H, DH, TQ)
            return carry
        lax.fori_loop(0, n_chunks, body, 0)
    sum8 = lambda b: jnp.sum(b.astype(I32).reshape(KC // SUBLANES, SUBLANES, TQ), axis=0)
    any_row = lambda r: jnp.max(r.astype(I32)) > 0

    iqT = iqT_ref[0]
    iq_cat = jnp.concatenate([iqT[h * IDX_DH:(h + 1) * IDX_DH, :] for h in range(IDX_HEADS)], axis=1)
    iw = smT_ref[0][2 * ML_HEADS:2 * ML_HEADS + IDX_HEADS, :] * (IDX_HEADS ** -0.5 * IDX_DH ** -0.5)
    t_pos = qi * TQ + lax.broadcasted_iota(I32, (KC, TQ), 1)
    s_off = lax.broadcasted_iota(I32, (KC, TQ), 0)
    gmax_scr[...] = jnp.full((KC, TQ), NEG_INF, F32)

    def score_group(width, masked):
        def body(g, tally):
            for u in range(width):
                ikc = ik_ref[0, pl.ds(chunk_start(width * g + u), KC), :][:, :IDX_DH]
                s_scr[u] = jnp.dot(ikc, iq_cat, preferred_element_type=F32)
            for u in range(width):
                c = width * g + u
                sc = iw[0:1, :] * jnp.maximum(s_scr[u, :, 0:TQ], 0.0)
                for h in range(1, IDX_HEADS):
                    sc = sc + iw[h:h + 1, :] * jnp.maximum(s_scr[u, :, h * TQ:(h + 1) * TQ], 0.0)
                if masked:
                    sc = jnp.where(c * KC + s_off <= t_pos, sc, NEG_INF)
                e_scr[pl.ds(chunk_start(c), KC), :] = sc
                gmax_scr[...] = jnp.maximum(gmax_scr[...], sc)
                tally = tally + jnp.sum(jnp.where(sc > 0.0, 1, jnp.where(sc == 0.0, 1 << 16, 0))
                                        .reshape(KC // SUBLANES, SUBLANES, TQ), axis=0)
            return tally
        return body
    assert n_chunks * KC < (1 << 16)
    zero8 = jnp.zeros((SUBLANES, TQ), I32)
    tally = lax.fori_loop(0, (npair - 1) // 2, score_group(4, False), zero8)
    tally = lax.fori_loop(2 * ((npair - 1) // 2), npair - 1, score_group(2, False), tally)
    tally = jnp.sum(score_group(2, True)(npair - 1, tally), axis=0, keepdims=True)
    n_pos = tally & 0xFFFF
    n_nonneg = n_pos + (tally >> 16)

    def sweep_keys(fold, merge, init):
        def group(width):
            def body(g, acc):
                return tuple(merge(acc[u], fold(e_scr[pl.ds(chunk_start(width * g + u), KC), :]))
                             for u in range(width))
            return body
        acc = lax.fori_loop(0, npair // 2, group(4), (init,) * 4)
        a, b = lax.fori_loop(2 * (npair // 2), npair, group(2), (merge(acc[0], acc[2]), merge(acc[1], acc[3])))
        return merge(a, b)

    def count_gt(cand):
        return jnp.sum(sweep_keys(lambda x: sum8(x > cand), lambda p, q: p + q, zero8), axis=0, keepdims=True)

    def min_gt(cand):
        fold = lambda x: jnp.min(jnp.where(x > cand, x, jnp.inf).reshape(KC // SUBLANES, SUBLANES, TQ), axis=0)
        return jnp.min(sweep_keys(fold, jnp.minimum, jnp.full((SUBLANES, TQ), jnp.inf, F32)), axis=0, keepdims=True)

    def rewrite(rows, thr):
        def body(c, carry):
            x = e_scr[pl.ds(chunk_start(c), KC), :]
            rank = (2 * spow - (c * KC + s_off)).astype(F32)
            new = jnp.where(x > thr, jnp.inf, jnp.where(x == thr, rank, NEG_INF))
            e_scr[pl.ds(chunk_start(c), KC), :] = jnp.where(rows, new, x)
            return carry
        lax.fori_loop(0, nch, body, 0)

    def halve(_, c):
        lo, hi, act, tie, n_lo = c
        mid = lo + (hi - lo) * 0.5
        inside = (mid > lo) & (mid < hi)
        cnt = count_gt(mid)
        live = act > 0.0
        run = live & inside
        up = run & (cnt >= topk)
        lo = jnp.where(up, mid, lo)
        n_lo = jnp.where(up, cnt, n_lo)
        hi = jnp.where(run & (cnt < topk), mid, hi)
        collapsed = live & jnp.logical_not(inside)
        tie = jnp.where(collapsed, 1.0, tie)
        act = jnp.where(collapsed | (run & (cnt == topk)), 0.0, act)
        return lo, hi, act, tie, n_lo

    def excess(state):
        return jnp.max(jnp.where(state[2] > 0.0, state[4] - topk, 0).astype(F32))

    def bisect(lo, hi, act, warmup):
        state = (lo, hi, act, jnp.zeros_like(act), jnp.full((1, TQ), 1 << 20, I32))
        state = lax.fori_loop(0, warmup, halve, state)

        def one_more(c):
            state = halve(0, c[1])
            return c[0] + 1, state, excess(state)
        near = lax.while_loop(lambda c: jnp.logical_and(c[0] < BISECT_CAP, c[2] > BISECT_DROP), one_more,
                              (jnp.int32(warmup), state, excess(state)))
        lo, hi, act, tie, n_lo = near[1]
        cur, left = lo, jnp.where(act > 0.0, n_lo - topk, 0)
        for _ in range(BISECT_DROP):
            cur = jnp.where(left > 0, min_gt(cur), cur)
            left = left - 1
        done = (act > 0.0) & (count_gt(cur) == topk)
        lo = jnp.where(done, cur, lo)
        act = jnp.where(done, 0.0, act)

        def body(c):
            state = c[1]
            for _ in range(BISECT_STEPS_PER_CHECK):
                state = halve(0, state)
            return c[0] + BISECT_STEPS_PER_CHECK, state, jnp.max(state[2])
        out = lax.while_loop(lambda c: jnp.logical_and(c[0] < BISECT_CAP, c[2] > 0.0), body,
                             (near[0], (lo, hi, act, tie, n_lo), jnp.max(act)))
        return out[1][0], out[1][1], out[1][3]

    fmax = float(jnp.finfo(F32).max)
    gm = gmax_scr[...]
    g_lo = jnp.min(gm, axis=0, keepdims=True)
    g_hi = jnp.max(gm, axis=0, keepdims=True)
    n_vis = qi * TQ + lax.broadcasted_iota(I32, (1, TQ), 1) + 1
    few = n_vis <= topk
    zero_tie = (n_pos < topk) & (n_nonneg >= topk) & jnp.logical_not(few)

    @pl.when(any_row(zero_tie))
    def _():
        rewrite(zero_tie, jnp.zeros((1, TQ), F32))

    rank_lo, rank_hi = float(spow), float(2 * spow + 1)
    below = jnp.maximum(g_lo, -fmax)
    below = below - jnp.abs(below) * (2.0 ** -10) - 1e-30
    lo0 = jnp.where(zero_tie, rank_lo, jnp.maximum(below, -fmax))
    hi0 = jnp.where(zero_tie, rank_hi, g_hi)
    lo, hi, tie = bisect(lo0, hi0, jnp.where(few, 0.0, 1.0), BISECT_WARMUP)
    thr_scr[0:1, :] = jnp.where(few, NEG_INF, lo)

    @pl.when(jnp.max(tie) > 0.0)
    def _():
        rows = tie > 0.0
        rewrite(rows, hi)
        lo2, _, _ = bisect(jnp.full((1, TQ), rank_lo, F32), jnp.full((1, TQ), rank_hi, F32), tie, 0)
        thr_scr[0:1, :] = jnp.where(rows, lo2, thr_scr[0:1, :])

    thr = thr_scr[0:1, :]
    qT = qT_ref[0]
    qs = _block_diag_T(qT, H, DH)
    hcols = lambda row, h: row[:, h * TQ:(h + 1) * TQ]

    def chunk_scores(c):
        start = chunk_start(c)
        sel = e_scr[pl.ds(start, KC), :] > thr
        s_all = jnp.dot(k_ref[0, pl.ds(start, KC), :], qs, preferred_element_type=F32)
        return sel, s_all

    def max_sweep():
        def max_pair(g, m):
            for u in range(2):
                sel, s_all = chunk_scores(2 * g + u)
                m = tuple(jnp.maximum(m[h], jnp.max(jnp.where(sel, hcols(s_all, h), NEG_INF), axis=0, keepdims=True))
                          for h in range(H))
            return m
        m = lax.fori_loop(0, npair, max_pair, (jnp.full((1, TQ), NEG_INF, F32),) * H)
        return [jnp.where(mh == NEG_INF, 0.0, mh) for mh in m]

    def sum_sweep(m):
        acc_scr[...] = jnp.zeros_like(acc_scr)

        def sum_group(width):
            def body(g, l):
                for u in range(width):
                    s_scr[u] = jnp.dot(k_ref[0, pl.ds(chunk_start(width * g + u), KC), :], qs,
                                       preferred_element_type=F32)
                for u in range(width):
                    start = chunk_start(width * g + u)
                    sel = e_scr[pl.ds(start, KC), :] > thr
                    vT_chunk = vT_ref[0, :, pl.ds(start, KC)]
                    l = tuple(_accumulate(h, jnp.exp2(s_scr[u, :, h * TQ:(h + 1) * TQ] + jnp.where(sel, -m[h], NEG_INF)),
                                          vT_chunk, l[h], acc_scr, DH) for h in range(H))
                return l
            return body
        nwide = (2 * npair) // ATT_STAGE
        l = lax.fori_loop(0, nwide, sum_group(ATT_STAGE), (jnp.zeros((1, TQ), F32),) * H)
        return lax.fori_loop((ATT_STAGE // 2) * nwide, npair, sum_group(2), l)

    chunk_n = lax.broadcasted_iota(I32, kn_scr.shape, 0)
    k2 = jnp.max(jnp.where(chunk_n < 2 * npair, kn_scr[...], 0.0), axis=0, keepdims=True)
    l = sum_sweep(_norm_bound(qT.astype(F32), k2, H, DH, TQ))
    _finish_attention(l, acc_scr, y_ref, H, DH)

    @pl.when(_underflowed(l))
    def _():
        _finish_attention(sum_sweep(max_sweep()), acc_scr, y_ref, H, DH)


def _dsa(qT, k, vT, iqT, ik, smT, B, S):
    TQ = min(ATT_TQ, S)
    topk = min(DSA_TOPK_MAX, S // 4)
    assert S % (2 * TQ) == 0 and topk <= TQ
    spow = 1 << max((S - 1).bit_length(), 1)
    k = k.reshape(B, S, DSA_W)
    ik = ik.reshape(B, S, LANES)
    qblk = lambda n: pl.BlockSpec((1, n, TQ), lambda b, i: (b, 0, i))
    return pl.pallas_call(
        functools.partial(_dsa_kernel, TQ=TQ, topk=topk, spow=spow, n_chunks=S // TQ),
        grid=(B, S // TQ),
        in_specs=[qblk(DSA_W),
                  pl.BlockSpec((1, S, DSA_W), lambda b, i: (b, 0, 0)),
                  pl.BlockSpec((1, DSA_W, S), lambda b, i: (b, 0, 0)),
                  qblk(IDX_HEADS * IDX_DH),
                  pl.BlockSpec((1, S, LANES), lambda b, i: (b, 0, 0)),
                  qblk(16)],
        out_specs=pl.BlockSpec((1, TQ, DSA_W), lambda b, i: (b, i, 0)),
        out_shape=jax.ShapeDtypeStruct((B, S, DSA_W), BF16),
        scratch_shapes=[pltpu.VMEM((S, TQ), F32),
                        pltpu.VMEM((TQ, TQ), F32),
                        pltpu.VMEM((DSA_W, TQ), F32),
                        pltpu.VMEM((SUBLANES, TQ), F32),
                        pltpu.VMEM((-(-(S // TQ) // SUBLANES) * SUBLANES, DSA_HEADS * TQ), F32),
                        pltpu.VMEM((ATT_STAGE, TQ, DSA_HEADS * TQ), F32)],
        compiler_params=_params("parallel", "arbitrary"),
    )(qT, k, vT, iqT, ik, smT)


def _out_mlp_kernel(yml_ref, ymb_ref, yds_ref, x_ref, wout_ref, gpost_ref, gpre_ref,
                    w1_ref, w2_ref, g2_ref, out_ref, x1_scr, h_scr, acc_scr):
    f = pl.program_id(1)

    @pl.when(f == 0)
    def _():
        slab = x_ref.shape[0] // MLP_SLABS
        for r in range(MLP_SLABS):
            rows = slice(r * slab, (r + 1) * slab)
            mix = jnp.dot(yml_ref[rows, :], wout_ref[0:ML_V_W, :], preferred_element_type=F32)
            mix = mix + jnp.dot(ymb_ref[rows, :], wout_ref[ML_V_W:ML_V_W + MOBA_W, :], preferred_element_type=F32)
            mix = mix + jnp.dot(yds_ref[rows, :], wout_ref[ML_V_W + MOBA_W:, :], preferred_element_type=F32)
            x1 = x_ref[rows, :] + _rms(mix, gpost_ref[...])
            x1_scr[rows, :] = x1
            h_scr[rows, :] = _rms(x1, gpre_ref[...]).astype(BF16)
        acc_scr[...] = jnp.zeros_like(acc_scr)

    u = jnp.maximum(jnp.dot(h_scr[...], w1_ref[...], preferred_element_type=F32), 0.0)
    acc_scr[...] += jnp.dot((u * u).astype(BF16), w2_ref[...], preferred_element_type=F32)

    @pl.when(f == pl.num_programs(1) - 1)
    def _():
        out_ref[...] = x1_scr[...] + _rms(acc_scr[...], g2_ref[...])


def _out_mlp(yml, ymb, yds, x2, w_out, g_post, g_pre, w1, w2, g2):
    T, D = x2.shape
    tm = min(MLP_TM, T)
    tf = min(MLP_TF, D_FF)
    row = lambda n: pl.BlockSpec((tm, n), lambda i, f: (i, 0))
    full = lambda a: pl.BlockSpec(a.shape, lambda i, f: (0,) * a.ndim)
    return pl.pallas_call(
        _out_mlp_kernel,
        grid=(T // tm, D_FF // tf),
        in_specs=[row(ML_V_W), row(MOBA_W), row(DSA_W), row(D), full(w_out), full(g_post), full(g_pre),
                  pl.BlockSpec((D, tf), lambda i, f: (0, f)),
                  pl.BlockSpec((tf, D), lambda i, f: (f, 0)),
                  full(g2)],
        out_specs=row(D),
        out_shape=jax.ShapeDtypeStruct((T, D), F32),
        scratch_shapes=[pltpu.VMEM((tm, D), F32), pltpu.VMEM((tm, D), BF16), pltpu.VMEM((tm, D), F32)],
        compiler_params=_params("parallel", "arbitrary"),
    )(yml, ymb, yds, x2, w_out, g_post, g_pre, w1, w2, g2)


def kernel(x, norm_mix_pre, w_in, ml_conv, ml_i_bias, ml_f_bias, ml_head_norm, w_out, norm_mix_post,
           norm_mlp_pre, w_ff1, w_ff2, norm_mlp_post):
    B, S, D = x.shape
    depth = w_in.shape[0]
    x2 = x.reshape(B * S, D)
    gain = lambda g: g.reshape(1, D).astype(F32)
    for l in range(depth):
        wn, wt = _prep_in_weights(w_in[l])
        (qk, mbk, dsk, ixk, mbqT, mbvT, dsqT, dsvT, ixqT, mlvT, mloT, smT) = _in_proj(
            x2, gain(norm_mix_pre[l]), wn, wt, B, S)
        y_ml = _mlstm(qk, mlvT, mloT, smT, ml_conv[l], ml_i_bias[l], ml_f_bias[l], ml_head_norm[l], B, S)
        y_mb = _moba(mbqT, mbk, mbvT, B, S)
        y_ds = _dsa(dsqT, dsk, dsvT, ixqT, ixk, smT, B, S)
        x2 = _out_mlp(y_ml.reshape(B * S, ML_V_W), y_mb.reshape(B * S, MOBA_W), y_ds.reshape(B * S, DSA_W),
                      x2, w_out[l].astype(BF16), gain(norm_mix_post[l]), gain(norm_mlp_pre[l]),
                      w_ff1[l].astype(BF16), w_ff2[l].astype(BF16), gain(norm_mlp_post[l]))
    return x2.reshape(B, S, D)
```

```python
import functools

import jax
import jax.numpy as jnp
from jax import lax
from jax.experimental import pallas as pl
from jax.experimental.pallas import tpu as pltpu

F32 = jnp.float32
BF16 = jnp.bfloat16
I32 = jnp.int32

D_MODEL = 1024
ML_HEADS, ML_DQK, ML_DV, ML_CONV = 4, 64, 128, 4
MOBA_HEADS, MOBA_DH, MOBA_BLOCK, MOBA_TOPK = 4, 64, 256, 3
DSA_HEADS, DSA_DH, IDX_HEADS, IDX_DH, DSA_TOPK_MAX = 4, 64, 4, 64, 256
D_FF = 4 * D_MODEL
EPS = 1e-6

ML_QK_W = ML_HEADS * ML_DQK
ML_V_W = ML_HEADS * ML_DV
MOBA_W = MOBA_HEADS * MOBA_DH
DSA_W = DSA_HEADS * DSA_DH
IN_SPLITS = (ML_QK_W, ML_QK_W, ML_V_W, ML_V_W, ML_HEADS, ML_HEADS,
             MOBA_W, MOBA_W, MOBA_W,
             DSA_W, DSA_W, DSA_W, IDX_HEADS * IDX_DH, IDX_DH, IDX_HEADS)

LANES = 128
SUBLANES = 8
VMEM_LIMIT = 52 * 1024 * 1024

ML_CHUNK = 512
ATT_TQ = 256
ATT_STAGE = 8
PROJ_TM = 1024
MLP_TM = 1024
MLP_TF = 512
MLP_SLABS = 4

NEG_INF = float("-inf")
BISECT_CAP = 300
BISECT_WARMUP = 12
BISECT_DROP = 2
BISECT_STEPS_PER_CHECK = 2
SOFTMAX_L_MIN = 1e-30
assert MOBA_DH == DSA_DH
ATT_QSCALE = MOBA_DH ** -0.5 * 1.4426950408889634

_NT = (((1,), (1,)), ((), ()))


def _rms(x, g):
    return x * lax.rsqrt(jnp.mean(x * x, axis=-1, keepdims=True) + EPS) * g


def _log_sigmoid(x):
    return jnp.minimum(x, 0.0) - jnp.log1p(jnp.exp(-jnp.abs(x)))


def _params(*sem):
    return pltpu.CompilerParams(dimension_semantics=sem, vmem_limit_bytes=VMEM_LIMIT)


_N_QK, _N_MBK, _N_DSK, _N_IXK = 0, 512, 768, 1024
_N_TOTAL = 1152
_T_MBQ, _T_MBV, _T_DSQ, _T_DSV, _T_IXQ, _T_MLV, _T_MLO, _T_SM = 0, 256, 512, 768, 1024, 1280, 1792, 2304
_T_TOTAL = 2320


def _in_proj_kernel(x_ref, g_ref, wn_ref, wt_ref,
                    qk_ref, mbk_ref, dsk_ref, ixk_ref,
                    mbqT_ref, mbvT_ref, dsqT_ref, dsvT_ref, ixqT_ref, mlvT_ref, mloT_ref, smT_ref):
    h = _rms(x_ref[...], g_ref[...]).astype(BF16)

    def mm(a, n):
        return jnp.dot(h, wn_ref[:, a:a + n], preferred_element_type=F32)

    def mt(a, n):
        return lax.dot_general(wt_ref[a:a + n, :], h, _NT, preferred_element_type=F32)

    qk_ref[...] = mm(_N_QK, 512)
    mbk_ref[...] = mm(_N_MBK, 256).astype(BF16)
    dsk_ref[...] = mm(_N_DSK, 256).astype(BF16)
    ixk_ref[...] = mm(_N_IXK, 128).astype(BF16)
    mbqT_ref[0] = (mt(_T_MBQ, 256) * ATT_QSCALE).astype(BF16)
    mbvT_ref[0] = mt(_T_MBV, 256).astype(BF16)
    dsqT_ref[0] = (mt(_T_DSQ, 256) * ATT_QSCALE).astype(BF16)
    dsvT_ref[0] = mt(_T_DSV, 256).astype(BF16)
    ixqT_ref[0] = mt(_T_IXQ, 256).astype(BF16)
    mlvT_ref[0] = mt(_T_MLV, 512).astype(BF16)
    mloT_ref[0] = mt(_T_MLO, 512)
    smT_ref[0] = mt(_T_SM, 16)


def _prep_in_weights(w_in):
    pts, acc = [], 0
    for n in IN_SPLITS:
        pts.append((acc, acc + n))
        acc += n
    col = lambda i: w_in[:, pts[i][0]:pts[i][1]]
    (ml_q, ml_k, ml_v, ml_o, ml_i, ml_f, mb_q, mb_k, mb_v,
     ds_q, ds_k, ds_v, ix_q, ix_k, ix_w) = [col(i) for i in range(len(IN_SPLITS))]
    d = w_in.shape[0]
    small = jnp.concatenate([ml_i, ml_f, ix_w], axis=1)
    wn = jnp.concatenate([
        ml_q, ml_k, mb_k, ds_k,
        ix_k, jnp.zeros((d, 128 - IDX_DH), w_in.dtype)], axis=1)
    wt = jnp.concatenate([
        mb_q, mb_v, ds_q, ds_v, ix_q, ml_v, ml_o,
        small, jnp.zeros((d, 16 - small.shape[1]), w_in.dtype)], axis=1).T
    assert wn.shape == (d, _N_TOTAL) and wt.shape == (_T_TOTAL, d)
    return wn.astype(BF16), wt.astype(BF16)


def _in_proj(x2, g, wn, wt, B, S):
    T, D = x2.shape
    tm = min(PROJ_TM, S)
    nsb = S // tm
    row = lambda n: pl.BlockSpec((tm, n), lambda i: (i, 0))
    tr = lambda n: pl.BlockSpec((1, n, tm), lambda i: (i // nsb, 0, i % nsb))
    full = lambda a: pl.BlockSpec(a.shape, lambda i: (0,) * a.ndim)
    out_shape = (
        jax.ShapeDtypeStruct((T, 512), F32),
        jax.ShapeDtypeStruct((T, 256), BF16),
        jax.ShapeDtypeStruct((T, 256), BF16),
        jax.ShapeDtypeStruct((T, 128), BF16),
        jax.ShapeDtypeStruct((B, 256, S), BF16),
        jax.ShapeDtypeStruct((B, 256, S), BF16),
        jax.ShapeDtypeStruct((B, 256, S), BF16),
        jax.ShapeDtypeStruct((B, 256, S), BF16),
        jax.ShapeDtypeStruct((B, 256, S), BF16),
        jax.ShapeDtypeStruct((B, 512, S), BF16),
        jax.ShapeDtypeStruct((B, 512, S), F32),
        jax.ShapeDtypeStruct((B, 16, S), F32),
    )
    out_specs = (row(512), row(256), row(256), row(128),
                 tr(256), tr(256), tr(256), tr(256), tr(256), tr(512), tr(512), tr(16))
    return pl.pallas_call(
        _in_proj_kernel,
        grid=(T // tm,),
        in_specs=[row(D), full(g), full(wn), full(wt)],
        out_specs=out_specs,
        out_shape=out_shape,
        compiler_params=_params("parallel"),
    )(x2, g, wn, wt)


def _dot3(a_f32, b_exact):
    b = b_exact.astype(BF16)
    hi = a_f32.astype(BF16)
    r1 = a_f32 - hi.astype(F32)
    mid = r1.astype(BF16)
    lo = (r1 - mid.astype(F32)).astype(BF16)
    return (jnp.dot(hi, b, preferred_element_type=F32) + jnp.dot(mid, b, preferred_element_type=F32)
            + jnp.dot(lo, b, preferred_element_type=F32))


def _mlstm_kernel(qk_ref, vT_ref, oT_ref, smT_ref, conv_ref, bcol_ref, hn_ref,
                  y_ref, xp_scr, st_scr, m_scr, *, L):
    c = pl.program_id(1)
    halo = SUBLANES
    DK, DV = ML_DQK, ML_DV

    @pl.when(c == 0)
    def _():
        xp_scr[0:halo, :] = jnp.zeros((halo, 2 * ML_QK_W), F32)
        st_scr[...] = jnp.zeros_like(st_scr)
        m_scr[...] = jnp.zeros_like(m_scr)

    cur = qk_ref[0]
    xp_scr[halo:halo + L, :] = cur
    base = halo - (ML_CONV - 1)
    acc = conv_ref[0:1, :] * xp_scr[base:base + L, :]
    for j in range(1, ML_CONV):
        acc = acc + conv_ref[j:j + 1, :] * xp_scr[base + j:base + j + L, :]
    xp_scr[0:halo, :] = cur[L - halo:L, :]
    qk = acc * jax.nn.sigmoid(acc)
    qT_all = (qk[:, :ML_QK_W] * (DK ** -0.5)).T.astype(BF16)
    k_all = qk[:, ML_QK_W:].astype(BF16)

    grow = smT_ref[0][0:SUBLANES, :] + bcol_ref[...]
    si = lax.broadcasted_iota(I32, (L, L), 0)
    ji = lax.broadcasted_iota(I32, (L, L), 1)
    causal = si <= ji
    b_row = _dot3(_log_sigmoid(grow), causal)

    for h in range(ML_HEADS):
        qT = qT_all[h * DK:(h + 1) * DK, :]
        kh = k_all[:, h * DK:(h + 1) * DK]
        vT = vT_ref[0, h * DV:(h + 1) * DV, :]
        b_j = b_row[ML_HEADS + h:ML_HEADS + h + 1, :]
        c_row = grow[h:h + 1, :] - b_j
        c_col = jnp.broadcast_to(c_row, (LANES, L)).T
        dlog = jnp.where(causal, b_j + jnp.concatenate([c_col] * (L // LANES), axis=1), NEG_INF)
        m0 = m_scr[h:h + 1, 0:1]
        inter = b_j + m0
        m_t = jnp.maximum(inter, jnp.max(dlog, axis=0, keepdims=True))
        w_inter = jnp.exp(inter - m_t)
        w_intra = jnp.exp(dlog - m_t) * jnp.dot(kh, qT, preferred_element_type=F32)
        st = st_scr[h]
        cq = jnp.dot(st.astype(BF16), qT, preferred_element_type=F32)
        num = w_inter * cq[:DV] + jnp.dot(vT, w_intra.astype(BF16), preferred_element_type=F32)
        den = w_inter * cq[DV:DV + 1] + jnp.sum(w_intra, axis=0, keepdims=True)
        hh = num / jnp.maximum(jnp.abs(den), jnp.exp(-m_t))
        hh = hh * lax.rsqrt(jnp.mean(hh * hh, axis=0, keepdims=True) + EPS)
        rows = slice(h * DV, (h + 1) * DV)
        y_ref[0, :, rows] = (jax.nn.sigmoid(oT_ref[0, rows, :]) * hh * hn_ref[rows, :]).T.astype(y_ref.dtype)

        b_last = b_j[:, L - 1:L]
        a = b_last + c_row
        m_loc = jnp.max(a, axis=1, keepdims=True)
        wa = jnp.exp(a - m_loc)
        lhs = jnp.concatenate([(vT.astype(F32) * wa).astype(BF16),
                               jnp.broadcast_to(wa, (SUBLANES, L)).astype(BF16)], axis=0)
        c_loc = jnp.dot(lhs, kh, preferred_element_type=F32)
        m_new = jnp.maximum(b_last + m0, m_loc)
        s_old = jnp.exp(b_last + m0 - m_new)
        s_loc = jnp.exp(m_loc - m_new)
        st_scr[h] = s_old * st + s_loc * c_loc
        m_scr[h:h + 1, :] = jnp.broadcast_to(m_new, (1, LANES))


def _mlstm(qk, vT, oT, smT, conv_w, i_bias, f_bias, head_norm, B, S):
    L = min(ML_CHUNK, S)
    assert L % LANES == 0
    nc = S // L
    qk = qk.reshape(B, S, 2 * ML_QK_W)
    bcol = jnp.concatenate([i_bias, f_bias]).astype(F32).reshape(2 * ML_HEADS, 1)
    hn = jnp.broadcast_to(head_norm.astype(F32)[:, None], (ML_V_W, L))
    blkT = lambda n: pl.BlockSpec((1, n, L), lambda b, c: (b, 0, c))
    full = lambda a: pl.BlockSpec(a.shape, lambda b, c: (0,) * a.ndim)
    return pl.pallas_call(
        functools.partial(_mlstm_kernel, L=L),
        grid=(B, nc),
        in_specs=[pl.BlockSpec((1, L, 2 * ML_QK_W), lambda b, c: (b, c, 0)),
                  blkT(ML_V_W), blkT(ML_V_W), blkT(16), full(conv_w), full(bcol), full(hn)],
        out_specs=pl.BlockSpec((1, L, ML_V_W), lambda b, c: (b, c, 0)),
        out_shape=jax.ShapeDtypeStruct((B, S, ML_V_W), BF16),
        scratch_shapes=[pltpu.VMEM((L + SUBLANES, 2 * ML_QK_W), F32),
                        pltpu.VMEM((ML_HEADS, ML_DV + SUBLANES, ML_DQK), F32),
                        pltpu.VMEM((SUBLANES, LANES), F32)],
        compiler_params=_params("parallel", "arbitrary"),
    )(qk, vT, oT, smT, conv_w.astype(F32), bcol, hn)


def _block_diag_T(qT, heads, dh):
    rid = lax.broadcasted_iota(I32, qT.shape, 0)
    zero = jnp.zeros_like(qT)
    return jnp.concatenate(
        [jnp.where((rid >= h * dh) & (rid < (h + 1) * dh), qT, zero) for h in range(heads)], axis=1)


def _accumulate(h, p, vT_chunk, l, acc_scr, dh):
    rows = slice(h * dh, (h + 1) * dh)
    acc_scr[rows, :] += jnp.dot(vT_chunk[rows, :], p.astype(BF16), preferred_element_type=F32)
    return l + jnp.sum(p, axis=0, keepdims=True)


def _finish_attention(l, acc_scr, y_ref, heads, dh):
    outs = [acc_scr[h * dh:(h + 1) * dh, :] / l[h] for h in range(heads)]
    y_ref[0] = jnp.concatenate(outs, axis=0).T.astype(y_ref.dtype)


def _max_key_norm2(kb, heads, dh, TQ):
    ch = lax.broadcasted_iota(I32, (heads * dh, LANES), 0) // dh
    hd = lax.broadcasted_iota(I32, (heads * dh, LANES), 1)
    n2 = jnp.dot((kb * kb).astype(BF16), (ch == hd).astype(BF16), preferred_element_type=F32)
    n2 = jnp.max(n2, axis=0, keepdims=True) * (1.0 + 2.0 ** -7)
    return jnp.concatenate([jnp.broadcast_to(n2[:, h:h + 1], (1, TQ)) for h in range(heads)], axis=1)


def _norm_bound(qTf, k2, heads, dh, TQ):
    out = []
    for h in range(heads):
        qh = qTf[h * dh:(h + 1) * dh, :]
        q2 = jnp.sum(qh * qh, axis=0, keepdims=True)
        out.append(jnp.sqrt(q2 * k2[:, h * TQ:(h + 1) * TQ]) * (1.0 + 2.0 ** -8))
    return out


def _underflowed(l):
    small = l[0]
    for lh in l[1:]:
        small = jnp.minimum(small, lh)
    return jnp.logical_not(jnp.min(small) >= SOFTMAX_L_MIN)


def _moba_kernel(qT_ref, k_ref, vT_ref, y_ref, km_scr, kn_scr, acc_scr, s_scr, *, NB, NBP, n_sel):
    H, DH, BS = MOBA_HEADS, MOBA_DH, MOBA_BLOCK
    TQ = BS
    qi = pl.program_id(1)

    @pl.when(qi == 0)
    def _():
        km_scr[...] = jnp.zeros_like(km_scr)
        kn_scr[...] = jnp.zeros_like(kn_scr)

        def body(j, carry):
            kb = k_ref[0, pl.ds(pl.multiple_of(j * BS, BS), BS), :].astype(F32)
            km_scr[pl.ds(j, 1), :] = jnp.mean(kb, axis=0, keepdims=True)
            kn_scr[pl.ds(j, 1), :] = _max_key_norm2(kb, H, DH, TQ)
            return carry
        lax.fori_loop(0, NB, body, 0)

    qT = qT_ref[0]
    qTf = qT.astype(F32)
    km = km_scr[...]
    gate = jnp.concatenate(
        [jnp.dot(km[:, h * DH:(h + 1) * DH], qTf[h * DH:(h + 1) * DH, :], precision=lax.Precision.HIGHEST,
                 preferred_element_type=F32) for h in range(H)], axis=1) * (1.0 / ATT_QSCALE)
    blk = lax.broadcasted_iota(I32, gate.shape, 0)
    g = jnp.where(blk < qi, gate, NEG_INF)
    sels = []
    for _ in range(n_sel):
        mx = jnp.max(g, axis=0, keepdims=True)
        isel = jnp.min(jnp.where(g == mx, blk, NBP), axis=0, keepdims=True)
        sels.append(jnp.where(mx > NEG_INF, isel, -1))
        g = jnp.where(blk == isel, NEG_INF, g)

    qs = _block_diag_T(qT, H, DH)

    def hit_row(j):
        hit = sels[0] == j
        for s in sels[1:]:
            hit = hit | (s == j)
        return hit

    npair = (qi + 1) // 2
    blk_start = lambda j: pl.multiple_of(j * BS, BS)
    blk_scores = lambda j: jnp.dot(k_ref[0, pl.ds(blk_start(j), BS), :], qs, preferred_element_type=F32)
    hcols = lambda row, h: row[:, h * TQ:(h + 1) * TQ]
    causal = lax.broadcasted_iota(I32, (BS, TQ), 0) <= lax.broadcasted_iota(I32, (BS, TQ), 1)

    def diag_scores():
        s_all = blk_scores(qi)
        return [jnp.where(causal, hcols(s_all, h), NEG_INF) for h in range(H)]

    def max_sweep():
        def max_pair(g, m):
            for u in range(2):
                j = 2 * g + u
                m = jnp.maximum(m, jnp.where(hit_row(j), jnp.max(blk_scores(j), axis=0, keepdims=True), NEG_INF))
            return m
        m_all = lax.fori_loop(0, npair, max_pair, jnp.full((1, H * TQ), NEG_INF, F32))
        s_diag = diag_scores()
        return [jnp.maximum(hcols(m_all, h), jnp.max(s_diag[h], axis=0, keepdims=True)) for h in range(H)]

    def sum_sweep(m):
        acc_scr[...] = jnp.zeros_like(acc_scr)

        def sum_group(width):
            def body(g, l):
                for u in range(width):
                    s_scr[u] = blk_scores(width * g + u)
                for u in range(width):
                    j = width * g + u
                    hit = hit_row(j)
                    vT_blk = vT_ref[0, :, pl.ds(blk_start(j), BS)]
                    l = tuple(_accumulate(h, jnp.exp2(s_scr[u, :, h * TQ:(h + 1) * TQ]
                                                      - jnp.where(hcols(hit, h), m[h], jnp.inf)),
                                          vT_blk, l[h], acc_scr, DH) for h in range(H))
                return l
            return body
        nwide = qi // ATT_STAGE
        l = lax.fori_loop(0, nwide, sum_group(ATT_STAGE), (jnp.zeros((1, TQ), F32),) * H)
        l = lax.fori_loop((ATT_STAGE // 2) * nwide, npair, sum_group(2), l)
        s_diag = diag_scores()
        vT_diag = vT_ref[0, :, pl.ds(blk_start(qi), BS)]
        return [_accumulate(h, jnp.exp2(s_diag[h] - m[h]), vT_diag, l[h], acc_scr, DH) for h in range(H)]

    blk_n = lax.broadcasted_iota(I32, kn_scr.shape, 0)
    k2 = jnp.max(jnp.where(blk_n <= qi, kn_scr[...], 0.0), axis=0, keepdims=True)
    l = sum_sweep(_norm_bound(qTf, k2, H, DH, TQ))
    _finish_attention(l, acc_scr, y_ref, H, DH)

    @pl.when(_underflowed(l))
    def _():
        _finish_attention(sum_sweep(max_sweep()), acc_scr, y_ref, H, DH)


def _moba(qT, k, vT, B, S):
    BS = MOBA_BLOCK
    assert S % BS == 0
    NB = S // BS
    NBP = -(-NB // SUBLANES) * SUBLANES
    n_sel = max(min(MOBA_TOPK, NB - 1), 1)
    k = k.reshape(B, S, MOBA_W)
    return pl.pallas_call(
        functools.partial(_moba_kernel, NB=NB, NBP=NBP, n_sel=n_sel),
        grid=(B, NB),
        in_specs=[pl.BlockSpec((1, MOBA_W, BS), lambda b, i: (b, 0, i)),
                  pl.BlockSpec((1, S, MOBA_W), lambda b, i: (b, 0, 0)),
                  pl.BlockSpec((1, MOBA_W, S), lambda b, i: (b, 0, 0))],
        out_specs=pl.BlockSpec((1, BS, MOBA_W), lambda b, i: (b, i, 0)),
        out_shape=jax.ShapeDtypeStruct((B, S, MOBA_W), BF16),
        scratch_shapes=[pltpu.VMEM((NBP, MOBA_W), F32),
                        pltpu.VMEM((NBP, MOBA_HEADS * BS), F32),
                        pltpu.VMEM((MOBA_W, BS), F32),
                        pltpu.VMEM((ATT_STAGE, BS, MOBA_HEADS * BS), F32)],
        compiler_params=_params("parallel", "arbitrary"),
    )(qT, k, vT)


def _dsa_kernel(qT_ref, k_ref, vT_ref, iqT_ref, ik_ref, smT_ref, y_ref,
                e_scr, gmax_scr, acc_scr, thr_scr, kn_scr, s_scr, *, TQ, topk, spow, n_chunks):
    H, DH = DSA_HEADS, DSA_DH
    KC = TQ
    qi = pl.program_id(1)
    nch = qi + 1
    npair = (nch + 1) // 2
    chunk_start = lambda c: pl.multiple_of(c * KC, KC)

    @pl.when(qi == 0)
    def _():
        kn_scr[...] = jnp.zeros_like(kn_scr)

        def body(c, carry):
            kn_scr[pl.ds(c, 1), :] = _max_key_norm2(k_ref[0, pl.ds(chunk_start(c), KC), :].astype(F32), H, DH, TQ)
            return carry
        lax.fori_loop(0, n_chunks, body, 0)
    sum8 = lambda b: jnp.sum(b.astype(I32).reshape(KC // SUBLANES, SUBLANES, TQ), axis=0)
    any_row = lambda r: jnp.max(r.astype(I32)) > 0

    iqT = iqT_ref[0]
    iq_cat = jnp.concatenate([iqT[h * IDX_DH:(h + 1) * IDX_DH, :] for h in range(IDX_HEADS)], axis=1)
    iw = smT_ref[0][2 * ML_HEADS:2 * ML_HEADS + IDX_HEADS, :] * (IDX_HEADS ** -0.5 * IDX_DH ** -0.5)
    t_pos = qi * TQ + lax.broadcasted_iota(I32, (KC, TQ), 1)
    s_off = lax.broadcasted_iota(I32, (KC, TQ), 0)
    gmax_scr[...] = jnp.full((KC, TQ), NEG_INF, F32)

    def score_group(width, masked):
        def body(g, tally):
            for u in range(width):
                ikc = ik_ref[0, pl.ds(chunk_start(width * g + u), KC), :][:, :IDX_DH]
                s_scr[u] = jnp.dot(ikc, iq_cat, preferred_element_type=F32)
            for u in range(width):
                c = width * g + u
                sc = iw[0:1, :] * jnp.maximum(s_scr[u, :, 0:TQ], 0.0)
                for h in range(1, IDX_HEADS):
                    sc = sc + iw[h:h + 1, :] * jnp.maximum(s_scr[u, :, h * TQ:(h + 1) * TQ], 0.0)
                if masked:
                    sc = jnp.where(c * KC + s_off <= t_pos, sc, NEG_INF)
                e_scr[pl.ds(chunk_start(c), KC), :] = sc
                gmax_scr[...] = jnp.maximum(gmax_scr[...], sc)
                tally = tally + jnp.sum(jnp.where(sc > 0.0, 1, jnp.where(sc == 0.0, 1 << 16, 0))
                                        .reshape(KC // SUBLANES, SUBLANES, TQ), axis=0)
            return tally
        return body
    assert n_chunks * KC < (1 << 16)
    zero8 = jnp.zeros((SUBLANES, TQ), I32)
    tally = lax.fori_loop(0, (npair - 1) // 2, score_group(4, False), zero8)
    tally = lax.fori_loop(2 * ((npair - 1) // 2), npair - 1, score_group(2, False), tally)
    tally = jnp.sum(score_group(2, True)(npair - 1, tally), axis=0, keepdims=True)
    n_pos = tally & 0xFFFF
    n_nonneg = n_pos + (tally >> 16)

    def sweep_keys(fold, merge, init):
        def group(width):
            def body(g, acc):
                return tuple(merge(acc[u], fold(e_scr[pl.ds(chunk_start(width * g + u), KC), :]))
                             for u in range(width))
            return body
        acc = lax.fori_loop(0, npair // 2, group(4), (init,) * 4)
        a, b = lax.fori_loop(2 * (npair // 2), npair, group(2), (merge(acc[0], acc[2]), merge(acc[1], acc[3])))
        return merge(a, b)

    def count_gt(cand):
        return jnp.sum(sweep_keys(lambda x: sum8(x > cand), lambda p, q: p + q, zero8), axis=0, keepdims=True)

    def min_gt(cand):
        fold = lambda x: jnp.min(jnp.where(x > cand, x, jnp.inf).reshape(KC // SUBLANES, SUBLANES, TQ), axis=0)
        return jnp.min(sweep_keys(fold, jnp.minimum, jnp.full((SUBLANES, TQ), jnp.inf, F32)), axis=0, keepdims=True)

    def rewrite(rows, thr):
        def body(c, carry):
            x = e_scr[pl.ds(chunk_start(c), KC), :]
            rank = (2 * spow - (c * KC + s_off)).astype(F32)
            new = jnp.where(x > thr, jnp.inf, jnp.where(x == thr, rank, NEG_INF))
            e_scr[pl.ds(chunk_start(c), KC), :] = jnp.where(rows, new, x)
            return carry
        lax.fori_loop(0, nch, body, 0)

    def halve(_, c):
        lo, hi, act, tie, n_lo = c
        mid = lo + (hi - lo) * 0.5
        inside = (mid > lo) & (mid < hi)
        cnt = count_gt(mid)
        live = act > 0.0
        run = live & inside
        up = run & (cnt >= topk)
        lo = jnp.where(up, mid, lo)
        n_lo = jnp.where(up, cnt, n_lo)
        hi = jnp.where(run & (cnt < topk), mid, hi)
        collapsed = live & jnp.logical_not(inside)
        tie = jnp.where(collapsed, 1.0, tie)
        act = jnp.where(collapsed | (run & (cnt == topk)), 0.0, act)
        return lo, hi, act, tie, n_lo

    def excess(state):
        return jnp.max(jnp.where(state[2] > 0.0, state[4] - topk, 0).astype(F32))

    def bisect(lo, hi, act, warmup):
        state = (lo, hi, act, jnp.zeros_like(act), jnp.full((1, TQ), 1 << 20, I32))
        state = lax.fori_loop(0, warmup, halve, state)

        def one_more(c):
            state = halve(0, c[1])
            return c[0] + 1, state, excess(state)
        near = lax.while_loop(lambda c: jnp.logical_and(c[0] < BISECT_CAP, c[2] > BISECT_DROP), one_more,
                              (jnp.int32(warmup), state, excess(state)))
        lo, hi, act, tie, n_lo = near[1]
        cur, left = lo, jnp.where(act > 0.0, n_lo - topk, 0)
        for _ in range(BISECT_DROP):
            cur = jnp.where(left > 0, min_gt(cur), cur)
            left = left - 1
        done = (act > 0.0) & (count_gt(cur) == topk)
        lo = jnp.where(done, cur, lo)
        act = jnp.where(done, 0.0, act)

        def body(c):
            state = c[1]
            for _ in range(BISECT_STEPS_PER_CHECK):
                state = halve(0, state)
            return c[0] + BISECT_STEPS_PER_CHECK, state, jnp.max(state[2])
        out = lax.while_loop(lambda c: jnp.logical_and(c[0] < BISECT_CAP, c[2] > 0.0), body,
                             (near[0], (lo, hi, act, tie, n_lo), jnp.max(act)))
        return out[1][0], out[1][1], out[1][3]

    fmax = float(jnp.finfo(F32).max)
    gm = gmax_scr[...]
    g_lo = jnp.min(gm, axis=0, keepdims=True)
    g_hi = jnp.max(gm, axis=0, keepdims=True)
    n_vis = qi * TQ + lax.broadcasted_iota(I32, (1, TQ), 1) + 1
    few = n_vis <= topk
    zero_tie = (n_pos < topk) & (n_nonneg >= topk) & jnp.logical_not(few)

    @pl.when(any_row(zero_tie))
    def _():
        rewrite(zero_tie, jnp.zeros((1, TQ), F32))

    rank_lo, rank_hi = float(spow), float(2 * spow + 1)
    below = jnp.maximum(g_lo, -fmax)
    below = below - jnp.abs(below) * (2.0 ** -10) - 1e-30
    lo0 = jnp.where(zero_tie, rank_lo, jnp.maximum(below, -fmax))
    hi0 = jnp.where(zero_tie, rank_hi, g_hi)
    lo, hi, tie = bisect(lo0, hi0, jnp.where(few, 0.0, 1.0), BISECT_WARMUP)
    thr_scr[0:1, :] = jnp.where(few, NEG_INF, lo)

    @pl.when(jnp.max(tie) > 0.0)
    def _():
        rows = tie > 0.0
        rewrite(rows, hi)
        lo2, _, _ = bisect(jnp.full((1, TQ), rank_lo, F32), jnp.full((1, TQ), rank_hi, F32), tie, 0)
        thr_scr[0:1, :] = jnp.where(rows, lo2, thr_scr[0:1, :])

    thr = thr_scr[0:1, :]
    qT = qT_ref[0]
    qs = _block_diag_T(qT, H, DH)
    hcols = lambda row, h: row[:, h * TQ:(h + 1) * TQ]

    def chunk_scores(c):
        start = chunk_start(c)
        sel = e_scr[pl.ds(start, KC), :] > thr
        s_all = jnp.dot(k_ref[0, pl.ds(start, KC), :], qs, preferred_element_type=F32)
        return sel, s_all

    def max_sweep():
        def max_pair(g, m):
            for u in range(2):
                sel, s_all = chunk_scores(2 * g + u)
                m = tuple(jnp.maximum(m[h], jnp.max(jnp.where(sel, hcols(s_all, h), NEG_INF), axis=0, keepdims=True))
                          for h in range(H))
            return m
        m = lax.fori_loop(0, npair, max_pair, (jnp.full((1, TQ), NEG_INF, F32),) * H)
        return [jnp.where(mh == NEG_INF, 0.0, mh) for mh in m]

    def sum_sweep(m):
        acc_scr[...] = jnp.zeros_like(acc_scr)

        def sum_group(width):
            def body(g, l):
                for u in range(width):
                    s_scr[u] = jnp.dot(k_ref[0, pl.ds(chunk_start(width * g + u), KC), :], qs,
                                       preferred_element_type=F32)
                for u in range(width):
                    start = chunk_start(width * g + u)
                    sel = e_scr[pl.ds(start, KC), :] > thr
                    vT_chunk = vT_ref[0, :, pl.ds(start, KC)]
                    l = tuple(_accumulate(h, jnp.exp2(s_scr[u, :, h * TQ:(h + 1) * TQ] + jnp.where(sel, -m[h], NEG_INF)),
                                          vT_chunk, l[h], acc_scr, DH) for h in range(H))
                return l
            return body
        nwide = (2 * npair) // ATT_STAGE
        l = lax.fori_loop(0, nwide, sum_group(ATT_STAGE), (jnp.zeros((1, TQ), F32),) * H)
        return lax.fori_loop((ATT_STAGE // 2) * nwide, npair, sum_group(2), l)

    chunk_n = lax.broadcasted_iota(I32, kn_scr.shape, 0)
    k2 = jnp.max(jnp.where(chunk_n < 2 * npair, kn_scr[...], 0.0), axis=0, keepdims=True)
    l = sum_sweep(_norm_bound(qT.astype(F32), k2, H, DH, TQ))
    _finish_attention(l, acc_scr, y_ref, H, DH)

    @pl.when(_underflowed(l))
    def _():
        _finish_attention(sum_sweep(max_sweep()), acc_scr, y_ref, H, DH)


def _dsa(qT, k, vT, iqT, ik, smT, B, S):
    TQ = min(ATT_TQ, S)
    topk = min(DSA_TOPK_MAX, S // 4)
    assert S % (2 * TQ) == 0 and topk <= TQ
    spow = 1 << max((S - 1).bit_length(), 1)
    k = k.reshape(B, S, DSA_W)
    ik = ik.reshape(B, S, LANES)
    qblk = lambda n: pl.BlockSpec((1, n, TQ), lambda b, i: (b, 0, i))
    return pl.pallas_call(
        functools.partial(_dsa_kernel, TQ=TQ, topk=topk, spow=spow, n_chunks=S // TQ),
        grid=(B, S // TQ),
        in_specs=[qblk(DSA_W),
                  pl.BlockSpec((1, S, DSA_W), lambda b, i: (b, 0, 0)),
                  pl.BlockSpec((1, DSA_W, S), lambda b, i: (b, 0, 0)),
                  qblk(IDX_HEADS * IDX_DH),
                  pl.BlockSpec((1, S, LANES), lambda b, i: (b, 0, 0)),
                  qblk(16)],
        out_specs=pl.BlockSpec((1, TQ, DSA_W), lambda b, i: (b, i, 0)),
        out_shape=jax.ShapeDtypeStruct((B, S, DSA_W), BF16),
        scratch_shapes=[pltpu.VMEM((S, TQ), F32),
                        pltpu.VMEM((TQ, TQ), F32),
                        pltpu.VMEM((DSA_W, TQ), F32),
                        pltpu.VMEM((SUBLANES, TQ), F32),
                        pltpu.VMEM((-(-(S // TQ) // SUBLANES) * SUBLANES, DSA_HEADS * TQ), F32),
                        pltpu.VMEM((ATT_STAGE, TQ, DSA_HEADS * TQ), F32)],
        compiler_params=_params("parallel", "arbitrary"),
    )(qT, k, vT, iqT, ik, smT)


def _out_mlp_kernel(yml_ref, ymb_ref, yds_ref, x_ref, wout_ref, gpost_ref, gpre_ref,
                    w1_ref, w2_ref, g2_ref, out_ref, x1_scr, h_scr, acc_scr):
    f = pl.program_id(1)

    @pl.when(f == 0)
    def _():
        slab = x_ref.shape[0] // MLP_SLABS
        for r in range(MLP_SLABS):
            rows = slice(r * slab, (r + 1) * slab)
            mix = jnp.dot(yml_ref[rows, :], wout_ref[0:ML_V_W, :], preferred_element_type=F32)
            mix = mix + jnp.dot(ymb_ref[rows, :], wout_ref[ML_V_W:ML_V_W + MOBA_W, :], preferred_element_type=F32)
            mix = mix + jnp.dot(yds_ref[rows, :], wout_ref[ML_V_W + MOBA_W:, :], preferred_element_type=F32)
            x1 = x_ref[rows, :] + _rms(mix, gpost_ref[...])
            x1_scr[rows, :] = x1
            h_scr[rows, :] = _rms(x1, gpre_ref[...]).astype(BF16)
        acc_scr[...] = jnp.zeros_like(acc_scr)

    u = jnp.maximum(jnp.dot(h_scr[...], w1_ref[...], preferred_element_type=F32), 0.0)
    acc_scr[...] += jnp.dot((u * u).astype(BF16), w2_ref[...], preferred_element_type=F32)

    @pl.when(f == pl.num_programs(1) - 1)
    def _():
        out_ref[...] = x1_scr[...] + _rms(acc_scr[...], g2_ref[...])


def _out_mlp(yml, ymb, yds, x2, w_out, g_post, g_pre, w1, w2, g2):
    T, D = x2.shape
    tm = min(MLP_TM, T)
    tf = min(MLP_TF, D_FF)
    row = lambda n: pl.BlockSpec((tm, n), lambda i, f: (i, 0))
    full = lambda a: pl.BlockSpec(a.shape, lambda i, f: (0,) * a.ndim)
    return pl.pallas_call(
        _out_mlp_kernel,
        grid=(T // tm, D_FF // tf),
        in_specs=[row(ML_V_W), row(MOBA_W), row(DSA_W), row(D), full(w_out), full(g_post), full(g_pre),
                  pl.BlockSpec((D, tf), lambda i, f: (0, f)),
                  pl.BlockSpec((tf, D), lambda i, f: (f, 0)),
                  full(g2)],
        out_specs=row(D),
        out_shape=jax.ShapeDtypeStruct((T, D), F32),
        scratch_shapes=[pltpu.VMEM((tm, D), F32), pltpu.VMEM((tm, D), BF16), pltpu.VMEM((tm, D), F32)],
        compiler_params=_params("parallel", "arbitrary"),
    )(yml, ymb, yds, x2, w_out, g_post, g_pre, w1, w2, g2)


def kernel(x, norm_mix_pre, w_in, ml_conv, ml_i_bias, ml_f_bias, ml_head_norm, w_out, norm_mix_post,
           norm_mlp_pre, w_ff1, w_ff2, norm_mlp_post):
    B, S, D = x.shape
    depth = w_in.shape[0]
    x2 = x.reshape(B * S, D)
    gain = lambda g: g.reshape(1, D).astype(F32)
    for l in range(depth):
        wn, wt = _prep_in_weights(w_in[l])
        (qk, mbk, dsk, ixk, mbqT, mbvT, dsqT, dsvT, ixqT, mlvT, mloT, smT) = _in_proj(
            x2, gain(norm_mix_pre[l]), wn, wt, B, S)
        y_ml = _mlstm(qk, mlvT, mloT, smT, ml_conv[l], ml_i_bias[l], ml_f_bias[l], ml_head_norm[l], B, S)
        y_mb = _moba(mbqT, mbk, mbvT, B, S)
        y_ds = _dsa(dsqT, dsk, dsvT, ixqT, ixk, smT, B, S)
        x2 = _out_mlp(y_ml.reshape(B * S, ML_V_W), y_mb.reshape(B * S, MOBA_W), y_ds.reshape(B * S, DSA_W),
                      x2, w_out[l].astype(BF16), gain(norm_mix_post[l]), gain(norm_mlp_pre[l]),
                      w_ff1[l].astype(BF16), w_ff2[l].astype(BF16), gain(norm_mlp_post[l]))
    return x2.reshape(B, S, D)
```

```python
import functools

import jax
import jax.numpy as jnp
from jax import lax
from jax.experimental import pallas as pl
from jax.experimental.pallas import tpu as pltpu

F32 = jnp.float32
BF16 = jnp.bfloat16
I32 = jnp.int32

D_MODEL = 1024
ML_HEADS, ML_DQK, ML_DV, ML_CONV = 4, 64, 128, 4
MOBA_HEADS, MOBA_DH, MOBA_BLOCK, MOBA_TOPK = 4, 64, 256, 3
DSA_HEADS, DSA_DH, IDX_HEADS, IDX_DH, DSA_TOPK_MAX = 4, 64, 4, 64, 256
D_FF = 4 * D_MODEL
EPS = 1e-6

ML_QK_W = ML_HEADS * ML_DQK
ML_V_W = ML_HEADS * ML_DV
MOBA_W = MOBA_HEADS * MOBA_DH
DSA_W = DSA_HEADS * DSA_DH
IN_SPLITS = (ML_QK_W, ML_QK_W, ML_V_W, ML_V_W, ML_HEADS, ML_HEADS,
             MOBA_W, MOBA_W, MOBA_W,
             DSA_W, DSA_W, DSA_W, IDX_HEADS * IDX_DH, IDX_DH, IDX_HEADS)

LANES = 128
SUBLANES = 8
VMEM_LIMIT = 52 * 1024 * 1024

ML_CHUNK = 512
ATT_TQ = 256
ATT_STAGE = 8
PROJ_TM = 1024
MLP_TM = 1024
MLP_TF = 512
MLP_SLABS = 4

NEG_INF = float("-inf")
BISECT_CAP = 300
BISECT_WARMUP = 11
BISECT_DROP = 2
BISECT_STEPS_PER_CHECK = 2
SOFTMAX_L_MIN = 1e-30
assert MOBA_DH == DSA_DH
ATT_QSCALE = MOBA_DH ** -0.5 * 1.4426950408889634

_NT = (((1,), (1,)), ((), ()))


def _rms(x, g):
    return x * lax.rsqrt(jnp.mean(x * x, axis=-1, keepdims=True) + EPS) * g


def _log_sigmoid(x):
    return jnp.minimum(x, 0.0) - jnp.log1p(jnp.exp(-jnp.abs(x)))


def _params(*sem):
    return pltpu.CompilerParams(dimension_semantics=sem, vmem_limit_bytes=VMEM_LIMIT)


_N_QK, _N_MBK, _N_DSK, _N_IXK = 0, 512, 768, 1024
_N_TOTAL = 1152
_T_MBQ, _T_MBV, _T_DSQ, _T_DSV, _T_IXQ, _T_MLV, _T_MLO, _T_SM = 0, 256, 512, 768, 1024, 1280, 1792, 2304
_T_TOTAL = 2320


def _in_proj_kernel(x_ref, g_ref, wn_ref, wt_ref,
                    qk_ref, mbk_ref, dsk_ref, ixk_ref,
                    mbqT_ref, mbvT_ref, dsqT_ref, dsvT_ref, ixqT_ref, mlvT_ref, mloT_ref, smT_ref):
    h = _rms(x_ref[...], g_ref[...]).astype(BF16)

    def mm(a, n):
        return jnp.dot(h, wn_ref[:, a:a + n], preferred_element_type=F32)

    def mt(a, n):
        return lax.dot_general(wt_ref[a:a + n, :], h, _NT, preferred_element_type=F32)

    qk_ref[...] = mm(_N_QK, 512)
    mbk_ref[...] = mm(_N_MBK, 256).astype(BF16)
    dsk_ref[...] = mm(_N_DSK, 256).astype(BF16)
    ixk_ref[...] = mm(_N_IXK, 128).astype(BF16)
    mbqT_ref[0] = (mt(_T_MBQ, 256) * ATT_QSCALE).astype(BF16)
    mbvT_ref[0] = mt(_T_MBV, 256).astype(BF16)
    dsqT_ref[0] = (mt(_T_DSQ, 256) * ATT_QSCALE).astype(BF16)
    dsvT_ref[0] = mt(_T_DSV, 256).astype(BF16)
    ixqT_ref[0] = mt(_T_IXQ, 256).astype(BF16)
    mlvT_ref[0] = mt(_T_MLV, 512).astype(BF16)
    mloT_ref[0] = mt(_T_MLO, 512)
    smT_ref[0] = mt(_T_SM, 16)


def _prep_in_weights(w_in):
    pts, acc = [], 0
    for n in IN_SPLITS:
        pts.append((acc, acc + n))
        acc += n
    col = lambda i: w_in[:, pts[i][0]:pts[i][1]]
    (ml_q, ml_k, ml_v, ml_o, ml_i, ml_f, mb_q, mb_k, mb_v,
     ds_q, ds_k, ds_v, ix_q, ix_k, ix_w) = [col(i) for i in range(len(IN_SPLITS))]
    d = w_in.shape[0]
    small = jnp.concatenate([ml_i, ml_f, ix_w], axis=1)
    wn = jnp.concatenate([
        ml_q, ml_k, mb_k, ds_k,
        ix_k, jnp.zeros((d, 128 - IDX_DH), w_in.dtype)], axis=1)
    wt = jnp.concatenate([
        mb_q, mb_v, ds_q, ds_v, ix_q, ml_v, ml_o,
        small, jnp.zeros((d, 16 - small.shape[1]), w_in.dtype)], axis=1).T
    assert wn.shape == (d, _N_TOTAL) and wt.shape == (_T_TOTAL, d)
    return wn.astype(BF16), wt.astype(BF16)


def _in_proj(x2, g, wn, wt, B, S):
    T, D = x2.shape
    tm = min(PROJ_TM, S)
    nsb = S // tm
    row = lambda n: pl.BlockSpec((tm, n), lambda i: (i, 0))
    tr = lambda n: pl.BlockSpec((1, n, tm), lambda i: (i // nsb, 0, i % nsb))
    full = lambda a: pl.BlockSpec(a.shape, lambda i: (0,) * a.ndim)
    out_shape = (
        jax.ShapeDtypeStruct((T, 512), F32),
        jax.ShapeDtypeStruct((T, 256), BF16),
        jax.ShapeDtypeStruct((T, 256), BF16),
        jax.ShapeDtypeStruct((T, 128), BF16),
        jax.ShapeDtypeStruct((B, 256, S), BF16),
        jax.ShapeDtypeStruct((B, 256, S), BF16),
        jax.ShapeDtypeStruct((B, 256, S), BF16),
        jax.ShapeDtypeStruct((B, 256, S), BF16),
        jax.ShapeDtypeStruct((B, 256, S), BF16),
        jax.ShapeDtypeStruct((B, 512, S), BF16),
        jax.ShapeDtypeStruct((B, 512, S), F32),
        jax.ShapeDtypeStruct((B, 16, S), F32),
    )
    out_specs = (row(512), row(256), row(256), row(128),
                 tr(256), tr(256), tr(256), tr(256), tr(256), tr(512), tr(512), tr(16))
    return pl.pallas_call(
        _in_proj_kernel,
        grid=(T // tm,),
        in_specs=[row(D), full(g), full(wn), full(wt)],
        out_specs=out_specs,
        out_shape=out_shape,
        compiler_params=_params("parallel"),
    )(x2, g, wn, wt)


def _dot3(a_f32, b_exact):
    b = b_exact.astype(BF16)
    hi = a_f32.astype(BF16)
    r1 = a_f32 - hi.astype(F32)
    mid = r1.astype(BF16)
    lo = (r1 - mid.astype(F32)).astype(BF16)
    return (jnp.dot(hi, b, preferred_element_type=F32) + jnp.dot(mid, b, preferred_element_type=F32)
            + jnp.dot(lo, b, preferred_element_type=F32))


def _mlstm_kernel(qk_ref, vT_ref, oT_ref, smT_ref, conv_ref, bcol_ref, hn_ref,
                  y_ref, xp_scr, st_scr, m_scr, *, L):
    c = pl.program_id(1)
    halo = SUBLANES
    DK, DV = ML_DQK, ML_DV

    @pl.when(c == 0)
    def _():
        xp_scr[0:halo, :] = jnp.zeros((halo, 2 * ML_QK_W), F32)
        st_scr[...] = jnp.zeros_like(st_scr)
        m_scr[...] = jnp.zeros_like(m_scr)

    cur = qk_ref[0]
    xp_scr[halo:halo + L, :] = cur
    base = halo - (ML_CONV - 1)
    acc = conv_ref[0:1, :] * xp_scr[base:base + L, :]
    for j in range(1, ML_CONV):
        acc = acc + conv_ref[j:j + 1, :] * xp_scr[base + j:base + j + L, :]
    xp_scr[0:halo, :] = cur[L - halo:L, :]
    qk = acc * jax.nn.sigmoid(acc)
    qT_all = (qk[:, :ML_QK_W] * (DK ** -0.5)).T.astype(BF16)
    k_all = qk[:, ML_QK_W:].astype(BF16)

    grow = smT_ref[0][0:SUBLANES, :] + bcol_ref[...]
    si = lax.broadcasted_iota(I32, (L, L), 0)
    ji = lax.broadcasted_iota(I32, (L, L), 1)
    causal = si <= ji
    b_row = _dot3(_log_sigmoid(grow), causal)

    for h in range(ML_HEADS):
        qT = qT_all[h * DK:(h + 1) * DK, :]
        kh = k_all[:, h * DK:(h + 1) * DK]
        vT = vT_ref[0, h * DV:(h + 1) * DV, :]
        b_j = b_row[ML_HEADS + h:ML_HEADS + h + 1, :]
        c_row = grow[h:h + 1, :] - b_j
        c_col = jnp.broadcast_to(c_row, (LANES, L)).T
        dlog = jnp.where(causal, b_j + jnp.concatenate([c_col] * (L // LANES), axis=1), NEG_INF)
        m0 = m_scr[h:h + 1, 0:1]
        inter = b_j + m0
        m_t = jnp.maximum(inter, jnp.max(dlog, axis=0, keepdims=True))
        w_inter = jnp.exp(inter - m_t)
        w_intra = jnp.exp(dlog - m_t) * jnp.dot(kh, qT, preferred_element_type=F32)
        st = st_scr[h]
        cq = jnp.dot(st.astype(BF16), qT, preferred_element_type=F32)
        num = w_inter * cq[:DV] + jnp.dot(vT, w_intra.astype(BF16), preferred_element_type=F32)
        den = w_inter * cq[DV:DV + 1] + jnp.sum(w_intra, axis=0, keepdims=True)
        hh = num / jnp.maximum(jnp.abs(den), jnp.exp(-m_t))
        hh = hh * lax.rsqrt(jnp.mean(hh * hh, axis=0, keepdims=True) + EPS)
        rows = slice(h * DV, (h + 1) * DV)
        y_ref[0, :, rows] = (jax.nn.sigmoid(oT_ref[0, rows, :]) * hh * hn_ref[rows, :]).T.astype(y_ref.dtype)

        b_last = b_j[:, L - 1:L]
        a = b_last + c_row
        m_loc = jnp.max(a, axis=1, keepdims=True)
        wa = jnp.exp(a - m_loc)
        lhs = jnp.concatenate([(vT.astype(F32) * wa).astype(BF16),
                               jnp.broadcast_to(wa, (SUBLANES, L)).astype(BF16)], axis=0)
        c_loc = jnp.dot(lhs, kh, preferred_element_type=F32)
        m_new = jnp.maximum(b_last + m0, m_loc)
        s_old = jnp.exp(b_last + m0 - m_new)
        s_loc = jnp.exp(m_loc - m_new)
        st_scr[h] = s_old * st + s_loc * c_loc
        m_scr[h:h + 1, :] = jnp.broadcast_to(m_new, (1, LANES))


def _mlstm(qk, vT, oT, smT, conv_w, i_bias, f_bias, head_norm, B, S):
    L = min(ML_CHUNK, S)
    assert L % LANES == 0
    nc = S // L
    qk = qk.reshape(B, S, 2 * ML_QK_W)
    bcol = jnp.concatenate([i_bias, f_bias]).astype(F32).reshape(2 * ML_HEADS, 1)
    hn = jnp.broadcast_to(head_norm.astype(F32)[:, None], (ML_V_W, L))
    blkT = lambda n: pl.BlockSpec((1, n, L), lambda b, c: (b, 0, c))
    full = lambda a: pl.BlockSpec(a.shape, lambda b, c: (0,) * a.ndim)
    return pl.pallas_call(
        functools.partial(_mlstm_kernel, L=L),
        grid=(B, nc),
        in_specs=[pl.BlockSpec((1, L, 2 * ML_QK_W), lambda b, c: (b, c, 0)),
                  blkT(ML_V_W), blkT(ML_V_W), blkT(16), full(conv_w), full(bcol), full(hn)],
        out_specs=pl.BlockSpec((1, L, ML_V_W), lambda b, c: (b, c, 0)),
        out_shape=jax.ShapeDtypeStruct((B, S, ML_V_W), BF16),
        scratch_shapes=[pltpu.VMEM((L + SUBLANES, 2 * ML_QK_W), F32),
                        pltpu.VMEM((ML_HEADS, ML_DV + SUBLANES, ML_DQK), F32),
                        pltpu.VMEM((SUBLANES, LANES), F32)],
        compiler_params=_params("parallel", "arbitrary"),
    )(qk, vT, oT, smT, conv_w.astype(F32), bcol, hn)


def _block_diag_T(qT, heads, dh):
    rid = lax.broadcasted_iota(I32, qT.shape, 0)
    zero = jnp.zeros_like(qT)
    return jnp.concatenate(
        [jnp.where((rid >= h * dh) & (rid < (h + 1) * dh), qT, zero) for h in range(heads)], axis=1)


def _accumulate(h, p, vT_chunk, l, acc_scr, dh):
    rows = slice(h * dh, (h + 1) * dh)
    acc_scr[rows, :] += jnp.dot(vT_chunk[rows, :], p.astype(BF16), preferred_element_type=F32)
    return l + jnp.sum(p, axis=0, keepdims=True)


def _finish_attention(l, acc_scr, y_ref, heads, dh):
    outs = [acc_scr[h * dh:(h + 1) * dh, :] / l[h] for h in range(heads)]
    y_ref[0] = jnp.concatenate(outs, axis=0).T.astype(y_ref.dtype)


def _max_key_norm2(kb, heads, dh, TQ):
    ch = lax.broadcasted_iota(I32, (heads * dh, LANES), 0) // dh
    hd = lax.broadcasted_iota(I32, (heads * dh, LANES), 1)
    n2 = jnp.dot((kb * kb).astype(BF16), (ch == hd).astype(BF16), preferred_element_type=F32)
    n2 = jnp.max(n2, axis=0, keepdims=True) * (1.0 + 2.0 ** -7)
    return jnp.concatenate([jnp.broadcast_to(n2[:, h:h + 1], (1, TQ)) for h in range(heads)], axis=1)


def _norm_bound(qTf, k2, heads, dh, TQ):
    out = []
    for h in range(heads):
        qh = qTf[h * dh:(h + 1) * dh, :]
        q2 = jnp.sum(qh * qh, axis=0, keepdims=True)
        out.append(jnp.sqrt(q2 * k2[:, h * TQ:(h + 1) * TQ]) * (1.0 + 2.0 ** -8))
    return out


def _underflowed(l):
    small = l[0]
    for lh in l[1:]:
        small = jnp.minimum(small, lh)
    return jnp.logical_not(jnp.min(small) >= SOFTMAX_L_MIN)


def _moba_kernel(qT_ref, k_ref, vT_ref, y_ref, km_scr, kn_scr, acc_scr, s_scr, *, NB, NBP, n_sel):
    H, DH, BS = MOBA_HEADS, MOBA_DH, MOBA_BLOCK
    TQ = BS
    qi = pl.program_id(1)

    @pl.when(qi == 0)
    def _():
        km_scr[...] = jnp.zeros_like(km_scr)
        kn_scr[...] = jnp.zeros_like(kn_scr)

        def body(j, carry):
            kb = k_ref[0, pl.ds(pl.multiple_of(j * BS, BS), BS), :].astype(F32)
            km_scr[pl.ds(j, 1), :] = jnp.mean(kb, axis=0, keepdims=True)
            kn_scr[pl.ds(j, 1), :] = _max_key_norm2(kb, H, DH, TQ)
            return carry
        lax.fori_loop(0, NB, body, 0)

    qT = qT_ref[0]
    qTf = qT.astype(F32)
    km = km_scr[...]
    gate = jnp.concatenate(
        [jnp.dot(km[:, h * DH:(h + 1) * DH], qTf[h * DH:(h + 1) * DH, :], precision=lax.Precision.HIGHEST,
                 preferred_element_type=F32) for h in range(H)], axis=1) * (1.0 / ATT_QSCALE)
    blk = lax.broadcasted_iota(I32, gate.shape, 0)
    g = jnp.where(blk < qi, gate, NEG_INF)
    sels = []
    for _ in range(n_sel):
        mx = jnp.max(g, axis=0, keepdims=True)
        isel = jnp.min(jnp.where(g == mx, blk, NBP), axis=0, keepdims=True)
        sels.append(jnp.where(mx > NEG_INF, isel, -1))
        g = jnp.where(blk == isel, NEG_INF, g)

    qs = _block_diag_T(qT, H, DH)

    def hit_row(j):
        hit = sels[0] == j
        for s in sels[1:]:
            hit = hit | (s == j)
        return hit

    npair = (qi + 1) // 2
    blk_start = lambda j: pl.multiple_of(j * BS, BS)
    blk_scores = lambda j: jnp.dot(k_ref[0, pl.ds(blk_start(j), BS), :], qs, preferred_element_type=F32)
    hcols = lambda row, h: row[:, h * TQ:(h + 1) * TQ]
    causal = lax.broadcasted_iota(I32, (BS, TQ), 0) <= lax.broadcasted_iota(I32, (BS, TQ), 1)

    def diag_scores():
        s_all = blk_scores(qi)
        return [jnp.where(causal, hcols(s_all, h), NEG_INF) for h in range(H)]

    def max_sweep():
        def max_pair(g, m):
            for u in range(2):
                j = 2 * g + u
                m = jnp.maximum(m, jnp.where(hit_row(j), jnp.max(blk_scores(j), axis=0, keepdims=True), NEG_INF))
            return m
        m_all = lax.fori_loop(0, npair, max_pair, jnp.full((1, H * TQ), NEG_INF, F32))
        s_diag = diag_scores()
        return [jnp.maximum(hcols(m_all, h), jnp.max(s_diag[h], axis=0, keepdims=True)) for h in range(H)]

    def sum_sweep(m):
        acc_scr[...] = jnp.zeros_like(acc_scr)

        def sum_group(width):
            def body(g, l):
                for u in range(width):
                    s_scr[u] = blk_scores(width * g + u)
                for u in range(width):
                    j = width * g + u
                    hit = hit_row(j)
                    vT_blk = vT_ref[0, :, pl.ds(blk_start(j), BS)]
                    l = tuple(_accumulate(h, jnp.exp2(s_scr[u, :, h * TQ:(h + 1) * TQ]
                                                      - jnp.where(hcols(hit, h), m[h], jnp.inf)),
                                          vT_blk, l[h], acc_scr, DH) for h in range(H))
                return l
            return body
        nwide = qi // ATT_STAGE
        l = lax.fori_loop(0, nwide, sum_group(ATT_STAGE), (jnp.zeros((1, TQ), F32),) * H)
        l = lax.fori_loop((ATT_STAGE // 2) * nwide, npair, sum_group(2), l)
        s_diag = diag_scores()
        vT_diag = vT_ref[0, :, pl.ds(blk_start(qi), BS)]
        return [_accumulate(h, jnp.exp2(s_diag[h] - m[h]), vT_diag, l[h], acc_scr, DH) for h in range(H)]

    blk_n = lax.broadcasted_iota(I32, kn_scr.shape, 0)
    k2 = jnp.max(jnp.where(blk_n <= qi, kn_scr[...], 0.0), axis=0, keepdims=True)
    l = sum_sweep(_norm_bound(qTf, k2, H, DH, TQ))
    _finish_attention(l, acc_scr, y_ref, H, DH)

    @pl.when(_underflowed(l))
    def _():
        _finish_attention(sum_sweep(max_sweep()), acc_scr, y_ref, H, DH)


def _moba(qT, k, vT, B, S):
    BS = MOBA_BLOCK
    assert S % BS == 0
    NB = S // BS
    NBP = -(-NB // SUBLANES) * SUBLANES
    n_sel = max(min(MOBA_TOPK, NB - 1), 1)
    k = k.reshape(B, S, MOBA_W)
    return pl.pallas_call(
        functools.partial(_moba_kernel, NB=NB, NBP=NBP, n_sel=n_sel),
        grid=(B, NB),
        in_specs=[pl.BlockSpec((1, MOBA_W, BS), lambda b, i: (b, 0, i)),
                  pl.BlockSpec((1, S, MOBA_W), lambda b, i: (b, 0, 0)),
                  pl.BlockSpec((1, MOBA_W, S), lambda b, i: (b, 0, 0))],
        out_specs=pl.BlockSpec((1, BS, MOBA_W), lambda b, i: (b, i, 0)),
        out_shape=jax.ShapeDtypeStruct((B, S, MOBA_W), BF16),
        scratch_shapes=[pltpu.VMEM((NBP, MOBA_W), F32),
                        pltpu.VMEM((NBP, MOBA_HEADS * BS), F32),
                        pltpu.VMEM((MOBA_W, BS), F32),
                        pltpu.VMEM((ATT_STAGE, BS, MOBA_HEADS * BS), F32)],
        compiler_params=_params("parallel", "arbitrary"),
    )(qT, k, vT)


def _dsa_kernel(qT_ref, k_ref, vT_ref, iqT_ref, ik_ref, smT_ref, y_ref,
                e_scr, gmax_scr, acc_scr, thr_scr, kn_scr, s_scr, *, TQ, topk, spow, n_chunks):
    H, DH = DSA_HEADS, DSA_DH
    KC = TQ
    qi = pl.program_id(1)
    nch = qi + 1
    npair = (nch + 1) // 2
    chunk_start = lambda c: pl.multiple_of(c * KC, KC)

    @pl.when(qi == 0)
    def _():
        kn_scr[...] = jnp.zeros_like(kn_scr)

        def body(c, carry):
            kn_scr[pl.ds(c, 1), :] = _max_key_norm2(k_ref[0, pl.ds(chunk_start(c), KC), :].astype(F32), H, DH, TQ)
            return carry
        lax.fori_loop(0, n_chunks, body, 0)
    sum8 = lambda b: jnp.sum(b.astype(I32).reshape(KC // SUBLANES, SUBLANES, TQ), axis=0)
    any_row = lambda r: jnp.max(r.astype(I32)) > 0

    iqT = iqT_ref[0]
    iq_cat = jnp.concatenate([iqT[h * IDX_DH:(h + 1) * IDX_DH, :] for h in range(IDX_HEADS)], axis=1)
    iw = smT_ref[0][2 * ML_HEADS:2 * ML_HEADS + IDX_HEADS, :] * (IDX_HEADS ** -0.5 * IDX_DH ** -0.5)
    t_pos = qi * TQ + lax.broadcasted_iota(I32, (KC, TQ), 1)
    s_off = lax.broadcasted_iota(I32, (KC, TQ), 0)
    gmax_scr[...] = jnp.full((KC, TQ), NEG_INF, F32)

    def score_group(width, masked):
        def body(g, tally):
            for u in range(width):
                ikc = ik_ref[0, pl.ds(chunk_start(width * g + u), KC), :][:, :IDX_DH]
                s_scr[u] = jnp.dot(ikc, iq_cat, preferred_element_type=F32)
            for u in range(width):
                c = width * g + u
                sc = iw[0:1, :] * jnp.maximum(s_scr[u, :, 0:TQ], 0.0)
                for h in range(1, IDX_HEADS):
                    sc = sc + iw[h:h + 1, :] * jnp.maximum(s_scr[u, :, h * TQ:(h + 1) * TQ], 0.0)
                if masked:
                    sc = jnp.where(c * KC + s_off <= t_pos, sc, NEG_INF)
                e_scr[pl.ds(chunk_start(c), KC), :] = sc
                gmax_scr[...] = jnp.maximum(gmax_scr[...], sc)
                tally = tally + jnp.sum(jnp.where(sc > 0.0, 1, jnp.where(sc == 0.0, 1 << 16, 0))
                                        .reshape(KC // SUBLANES, SUBLANES, TQ), axis=0)
            return tally
        return body
    assert n_chunks * KC < (1 << 16)
    zero8 = jnp.zeros((SUBLANES, TQ), I32)
    tally = lax.fori_loop(0, (npair - 1) // 2, score_group(4, False), zero8)
    tally = lax.fori_loop(2 * ((npair - 1) // 2), npair - 1, score_group(2, False), tally)
    tally = jnp.sum(score_group(2, True)(npair - 1, tally), axis=0, keepdims=True)
    n_pos = tally & 0xFFFF
    n_nonneg = n_pos + (tally >> 16)

    def sweep_keys(fold, merge, init):
        def group(width):
            def body(g, acc):
                return tuple(merge(acc[u], fold(e_scr[pl.ds(chunk_start(width * g + u), KC), :]))
                             for u in range(width))
            return body
        acc = lax.fori_loop(0, npair // 2, group(4), (init,) * 4)
        a, b = lax.fori_loop(2 * (npair // 2), npair, group(2), (merge(acc[0], acc[2]), merge(acc[1], acc[3])))
        return merge(a, b)

    def count_gt(cand):
        return jnp.sum(sweep_keys(lambda x: sum8(x > cand), lambda p, q: p + q, zero8), axis=0, keepdims=True)

    def min_gt(cand):
        fold = lambda x: jnp.min(jnp.where(x > cand, x, jnp.inf).reshape(KC // SUBLANES, SUBLANES, TQ), axis=0)
        return jnp.min(sweep_keys(fold, jnp.minimum, jnp.full((SUBLANES, TQ), jnp.inf, F32)), axis=0, keepdims=True)

    def rewrite(rows, thr):
        def body(c, carry):
            x = e_scr[pl.ds(chunk_start(c), KC), :]
            rank = (2 * spow - (c * KC + s_off)).astype(F32)
            new = jnp.where(x > thr, jnp.inf, jnp.where(x == thr, rank, NEG_INF))
            e_scr[pl.ds(chunk_start(c), KC), :] = jnp.where(rows, new, x)
            return carry
        lax.fori_loop(0, nch, body, 0)

    def halve(_, c):
        lo, hi, act, tie, n_lo = c
        mid = lo + (hi - lo) * 0.5
        inside = (mid > lo) & (mid < hi)
        cnt = count_gt(mid)
        live = act > 0.0
        run = live & inside
        up = run & (cnt >= topk)
        lo = jnp.where(up, mid, lo)
        n_lo = jnp.where(up, cnt, n_lo)
        hi = jnp.where(run & (cnt < topk), mid, hi)
        collapsed = live & jnp.logical_not(inside)
        tie = jnp.where(collapsed, 1.0, tie)
        act = jnp.where(collapsed | (run & (cnt == topk)), 0.0, act)
        return lo, hi, act, tie, n_lo

    def excess(state):
        return jnp.max(jnp.where(state[2] > 0.0, state[4] - topk, 0).astype(F32))

    def bisect(lo, hi, act, warmup):
        state = (lo, hi, act, jnp.zeros_like(act), jnp.full((1, TQ), 1 << 20, I32))
        state = lax.fori_loop(0, warmup, halve, state)

        def one_more(c):
            state = halve(0, c[1])
            return c[0] + 1, state, excess(state)
        near = lax.while_loop(lambda c: jnp.logical_and(c[0] < BISECT_CAP, c[2] > BISECT_DROP), one_more,
                              (jnp.int32(warmup), state, excess(state)))
        lo, hi, act, tie, n_lo = near[1]
        cur, left = lo, jnp.where(act > 0.0, n_lo - topk, 0)
        for _ in range(BISECT_DROP):
            cur = jnp.where(left > 0, min_gt(cur), cur)
            left = left - 1
        done = (act > 0.0) & (count_gt(cur) == topk)
        lo = jnp.where(done, cur, lo)
        act = jnp.where(done, 0.0, act)

        def body(c):
            state = c[1]
            for _ in range(BISECT_STEPS_PER_CHECK):
                state = halve(0, state)
            return c[0] + BISECT_STEPS_PER_CHECK, state, jnp.max(state[2])
        out = lax.while_loop(lambda c: jnp.logical_and(c[0] < BISECT_CAP, c[2] > 0.0), body,
                             (near[0], (lo, hi, act, tie, n_lo), jnp.max(act)))
        return out[1][0], out[1][1], out[1][3]

    fmax = float(jnp.finfo(F32).max)
    gm = gmax_scr[...]
    g_lo = jnp.min(gm, axis=0, keepdims=True)
    g_hi = jnp.max(gm, axis=0, keepdims=True)
    n_vis = qi * TQ + lax.broadcasted_iota(I32, (1, TQ), 1) + 1
    few = n_vis <= topk
    zero_tie = (n_pos < topk) & (n_nonneg >= topk) & jnp.logical_not(few)

    @pl.when(any_row(zero_tie))
    def _():
        rewrite(zero_tie, jnp.zeros((1, TQ), F32))

    rank_lo, rank_hi = float(spow), float(2 * spow + 1)
    below = jnp.maximum(g_lo, -fmax)
    below = below - jnp.abs(below) * (2.0 ** -10) - 1e-30
    lo0 = jnp.where(zero_tie, rank_lo, jnp.maximum(below, -fmax))
    hi0 = jnp.where(zero_tie, rank_hi, g_hi)
    lo, hi, tie = bisect(lo0, hi0, jnp.where(few, 0.0, 1.0), BISECT_WARMUP)
    thr_scr[0:1, :] = jnp.where(few, NEG_INF, lo)

    @pl.when(jnp.max(tie) > 0.0)
    def _():
        rows = tie > 0.0
        rewrite(rows, hi)
        lo2, _, _ = bisect(jnp.full((1, TQ), rank_lo, F32), jnp.full((1, TQ), rank_hi, F32), tie, 0)
        thr_scr[0:1, :] = jnp.where(rows, lo2, thr_scr[0:1, :])

    thr = thr_scr[0:1, :]
    qT = qT_ref[0]
    qs = _block_diag_T(qT, H, DH)
    hcols = lambda row, h: row[:, h * TQ:(h + 1) * TQ]

    def chunk_scores(c):
        start = chunk_start(c)
        sel = e_scr[pl.ds(start, KC), :] > thr
        s_all = jnp.dot(k_ref[0, pl.ds(start, KC), :], qs, preferred_element_type=F32)
        return sel, s_all

    def max_sweep():
        def max_pair(g, m):
            for u in range(2):
                sel, s_all = chunk_scores(2 * g + u)
                m = tuple(jnp.maximum(m[h], jnp.max(jnp.where(sel, hcols(s_all, h), NEG_INF), axis=0, keepdims=True))
                          for h in range(H))
            return m
        m = lax.fori_loop(0, npair, max_pair, (jnp.full((1, TQ), NEG_INF, F32),) * H)
        return [jnp.where(mh == NEG_INF, 0.0, mh) for mh in m]

    def sum_sweep(m):
        acc_scr[...] = jnp.zeros_like(acc_scr)

        def sum_group(width):
            def body(g, l):
                for u in range(width):
                    s_scr[u] = jnp.dot(k_ref[0, pl.ds(chunk_start(width * g + u), KC), :], qs,
                                       preferred_element_type=F32)
                for u in range(width):
                    start = chunk_start(width * g + u)
                    sel = e_scr[pl.ds(start, KC), :] > thr
                    vT_chunk = vT_ref[0, :, pl.ds(start, KC)]
                    l = tuple(_accumulate(h, jnp.exp2(s_scr[u, :, h * TQ:(h + 1) * TQ] + jnp.where(sel, -m[h], NEG_INF)),
                                          vT_chunk, l[h], acc_scr, DH) for h in range(H))
                return l
            return body
        nwide = (2 * npair) // ATT_STAGE
        l = lax.fori_loop(0, nwide, sum_group(ATT_STAGE), (jnp.zeros((1, TQ), F32),) * H)
        return lax.fori_loop((ATT_STAGE // 2) * nwide, npair, sum_group(2), l)

    chunk_n = lax.broadcasted_iota(I32, kn_scr.shape, 0)
    k2 = jnp.max(jnp.where(chunk_n < 2 * npair, kn_scr[...], 0.0), axis=0, keepdims=True)
    l = sum_sweep(_norm_bound(qT.astype(F32), k2, H, DH, TQ))
    _finish_attention(l, acc_scr, y_ref, H, DH)

    @pl.when(_underflowed(l))
    def _():
        _finish_attention(sum_sweep(max_sweep()), acc_scr, y_ref, H, DH)


def _dsa(qT, k, vT, iqT, ik, smT, B, S):
    TQ = min(ATT_TQ, S)
    topk = min(DSA_TOPK_MAX, S // 4)
    assert S % (2 * TQ) == 0 and topk <= TQ
    spow = 1 << max((S - 1).bit_length(), 1)
    k = k.reshape(B, S, DSA_W)
    ik = ik.reshape(B, S, LANES)
    qblk = lambda n: pl.BlockSpec((1, n, TQ), lambda b, i: (b, 0, i))
    return pl.pallas_call(
        functools.partial(_dsa_kernel, TQ=TQ, topk=topk, spow=spow, n_chunks=S // TQ),
        grid=(B, S // TQ),
        in_specs=[qblk(DSA_W),
                  pl.BlockSpec((1, S, DSA_W), lambda b, i: (b, 0, 0)),
                  pl.BlockSpec((1, DSA_W, S), lambda b, i: (b, 0, 0)),
                  qblk(IDX_HEADS * IDX_DH),
                  pl.BlockSpec((1, S, LANES), lambda b, i: (b, 0, 0)),
                  qblk(16)],
        out_specs=pl.BlockSpec((1, TQ, DSA_W), lambda b, i: (b, i, 0)),
        out_shape=jax.ShapeDtypeStruct((B, S, DSA_W), BF16),
        scratch_shapes=[pltpu.VMEM((S, TQ), F32),
                        pltpu.VMEM((TQ, TQ), F32),
                        pltpu.VMEM((DSA_W, TQ), F32),
                        pltpu.VMEM((SUBLANES, TQ), F32),
                        pltpu.VMEM((-(-(S // TQ) // SUBLANES) * SUBLANES, DSA_HEADS * TQ), F32),
                        pltpu.VMEM((ATT_STAGE, TQ, DSA_HEADS * TQ), F32)],
        compiler_params=_params("parallel", "arbitrary"),
    )(qT, k, vT, iqT, ik, smT)


def _out_mlp_kernel(yml_ref, ymb_ref, yds_ref, x_ref, wout_ref, gpost_ref, gpre_ref,
                    w1_ref, w2_ref, g2_ref, out_ref, x1_scr, h_scr, acc_scr):
    f = pl.program_id(1)

    @pl.when(f == 0)
    def _():
        slab = x_ref.shape[0] // MLP_SLABS
        for r in range(MLP_SLABS):
            rows = slice(r * slab, (r + 1) * slab)
            mix = jnp.dot(yml_ref[rows, :], wout_ref[0:ML_V_W, :], preferred_element_type=F32)
            mix = mix + jnp.dot(ymb_ref[rows, :], wout_ref[ML_V_W:ML_V_W + MOBA_W, :], preferred_element_type=F32)
            mix = mix + jnp.dot(yds_ref[rows, :], wout_ref[ML_V_W + MOBA_W:, :], preferred_element_type=F32)
            x1 = x_ref[rows, :] + _rms(mix, gpost_ref[...])
            x1_scr[rows, :] = x1
            h_scr[rows, :] = _rms(x1, gpre_ref[...]).astype(BF16)
        acc_scr[...] = jnp.zeros_like(acc_scr)

    u = jnp.maximum(jnp.dot(h_scr[...], w1_ref[...], preferred_element_type=F32), 0.0)
    acc_scr[...] += jnp.dot((u * u).astype(BF16), w2_ref[...], preferred_element_type=F32)

    @pl.when(f == pl.num_programs(1) - 1)
    def _():
        out_ref[...] = x1_scr[...] + _rms(acc_scr[...], g2_ref[...])


def _out_mlp(yml, ymb, yds, x2, w_out, g_post, g_pre, w1, w2, g2):
    T, D = x2.shape
    tm = min(MLP_TM, T)
    tf = min(MLP_TF, D_FF)
    row = lambda n: pl.BlockSpec((tm, n), lambda i, f: (i, 0))
    full = lambda a: pl.BlockSpec(a.shape, lambda i, f: (0,) * a.ndim)
    return pl.pallas_call(
        _out_mlp_kernel,
        grid=(T // tm, D_FF // tf),
        in_specs=[row(ML_V_W), row(MOBA_W), row(DSA_W), row(D), full(w_out), full(g_post), full(g_pre),
                  pl.BlockSpec((D, tf), lambda i, f: (0, f)),
                  pl.BlockSpec((tf, D), lambda i, f: (f, 0)),
                  full(g2)],
        out_specs=row(D),
        out_shape=jax.ShapeDtypeStruct((T, D), F32),
        scratch_shapes=[pltpu.VMEM((tm, D), F32), pltpu.VMEM((tm, D), BF16), pltpu.VMEM((tm, D), F32)],
        compiler_params=_params("parallel", "arbitrary"),
    )(yml, ymb, yds, x2, w_out, g_post, g_pre, w1, w2, g2)


def kernel(x, norm_mix_pre, w_in, ml_conv, ml_i_bias, ml_f_bias, ml_head_norm, w_out, norm_mix_post,
           norm_mlp_pre, w_ff1, w_ff2, norm_mlp_post):
    B, S, D = x.shape
    depth = w_in.shape[0]
    x2 = x.reshape(B * S, D)
    gain = lambda g: g.reshape(1, D).astype(F32)
    for l in range(depth):
        wn, wt = _prep_in_weights(w_in[l])
        (qk, mbk, dsk, ixk, mbqT, mbvT, dsqT, dsvT, ixqT, mlvT, mloT, smT) = _in_proj(
            x2, gain(norm_mix_pre[l]), wn, wt, B, S)
        y_ml = _mlstm(qk, mlvT, mloT, smT, ml_conv[l], ml_i_bias[l], ml_f_bias[l], ml_head_norm[l], B, S)
        y_mb = _moba(mbqT, mbk, mbvT, B, S)
        y_ds = _dsa(dsqT, dsk, dsvT, ixqT, ixk, smT, B, S)
        x2 = _out_mlp(y_ml.reshape(B * S, ML_V_W), y_mb.reshape(B * S, MOBA_W), y_ds.reshape(B * S, DSA_W),
                      x2, w_out[l].astype(BF16), gain(norm_mix_post[l]), gain(norm_mlp_pre[l]),
                      w_ff1[l].astype(BF16), w_ff2[l].astype(BF16), gain(norm_mlp_post[l]))
    return x2.reshape(B, S, D)
```

```python
import functools

import jax
import jax.numpy as jnp
from jax import lax
from jax.experimental import pallas as pl
from jax.experimental.pallas import tpu as pltpu

F32 = jnp.float32
BF16 = jnp.bfloat16
I32 = jnp.int32

D_MODEL = 1024
ML_HEADS, ML_DQK, ML_DV, ML_CONV = 4, 64, 128, 4
MOBA_HEADS, MOBA_DH, MOBA_BLOCK, MOBA_TOPK = 4, 64, 256, 3
DSA_HEADS, DSA_DH, IDX_HEADS, IDX_DH, DSA_TOPK_MAX = 4, 64, 4, 64, 256
D_FF = 4 * D_MODEL
EPS = 1e-6

ML_QK_W = ML_HEADS * ML_DQK
ML_V_W = ML_HEADS * ML_DV
MOBA_W = MOBA_HEADS * MOBA_DH
DSA_W = DSA_HEADS * DSA_DH
IN_SPLITS = (ML_QK_W, ML_QK_W, ML_V_W, ML_V_W, ML_HEADS, ML_HEADS,
             MOBA_W, MOBA_W, MOBA_W,
             DSA_W, DSA_W, DSA_W, IDX_HEADS * IDX_DH, IDX_DH, IDX_HEADS)

LANES = 128
SUBLANES = 8
VMEM_LIMIT = 52 * 1024 * 1024

ML_CHUNK = 512
ATT_TQ = 256
ATT_STAGE = 8
PROJ_TM = 1024
MLP_TM = 1024
MLP_TF = 512
MLP_SLABS = 4

NEG_INF = float("-inf")
BISECT_CAP = 300
BISECT_WARMUP = 13
BISECT_DROP = 2
BISECT_STEPS_PER_CHECK = 2
SOFTMAX_L_MIN = 1e-30
assert MOBA_DH == DSA_DH
ATT_QSCALE = MOBA_DH ** -0.5 * 1.4426950408889634

_NT = (((1,), (1,)), ((), ()))


def _rms(x, g):
    return x * lax.rsqrt(jnp.mean(x * x, axis=-1, keepdims=True) + EPS) * g


def _log_sigmoid(x):
    return jnp.minimum(x, 0.0) - jnp.log1p(jnp.exp(-jnp.abs(x)))


def _params(*sem):
    return pltpu.CompilerParams(dimension_semantics=sem, vmem_limit_bytes=VMEM_LIMIT)


_N_QK, _N_MBK, _N_DSK, _N_IXK = 0, 512, 768, 1024
_N_TOTAL = 1152
_T_MBQ, _T_MBV, _T_DSQ, _T_DSV, _T_IXQ, _T_MLV, _T_MLO, _T_SM = 0, 256, 512, 768, 1024, 1280, 1792, 2304
_T_TOTAL = 2320


def _in_proj_kernel(x_ref, g_ref, wn_ref, wt_ref,
                    qk_ref, mbk_ref, dsk_ref, ixk_ref,
                    mbqT_ref, mbvT_ref, dsqT_ref, dsvT_ref, ixqT_ref, mlvT_ref, mloT_ref, smT_ref):
    h = _rms(x_ref[...], g_ref[...]).astype(BF16)

    def mm(a, n):
        return jnp.dot(h, wn_ref[:, a:a + n], preferred_element_type=F32)

    def mt(a, n):
        return lax.dot_general(wt_ref[a:a + n, :], h, _NT, preferred_element_type=F32)

    qk_ref[...] = mm(_N_QK, 512)
    mbk_ref[...] = mm(_N_MBK, 256).astype(BF16)
    dsk_ref[...] = mm(_N_DSK, 256).astype(BF16)
    ixk_ref[...] = mm(_N_IXK, 128).astype(BF16)
    mbqT_ref[0] = (mt(_T_MBQ, 256) * ATT_QSCALE).astype(BF16)
    mbvT_ref[0] = mt(_T_MBV, 256).astype(BF16)
    dsqT_ref[0] = (mt(_T_DSQ, 256) * ATT_QSCALE).astype(BF16)
    dsvT_ref[0] = mt(_T_DSV, 256).astype(BF16)
    ixqT_ref[0] = mt(_T_IXQ, 256).astype(BF16)
    mlvT_ref[0] = mt(_T_MLV, 512).astype(BF16)
    mloT_ref[0] = mt(_T_MLO, 512)
    smT_ref[0] = mt(_T_SM, 16)


def _prep_in_weights(w_in):
    pts, acc = [], 0
    for n in IN_SPLITS:
        pts.append((acc, acc + n))
        acc += n
    col = lambda i: w_in[:, pts[i][0]:pts[i][1]]
    (ml_q, ml_k, ml_v, ml_o, ml_i, ml_f, mb_q, mb_k, mb_v,
     ds_q, ds_k, ds_v, ix_q, ix_k, ix_w) = [col(i) for i in range(len(IN_SPLITS))]
    d = w_in.shape[0]
    small = jnp.concatenate([ml_i, ml_f, ix_w], axis=1)
    wn = jnp.concatenate([
        ml_q, ml_k, mb_k, ds_k,
        ix_k, jnp.zeros((d, 128 - IDX_DH), w_in.dtype)], axis=1)
    wt = jnp.concatenate([
        mb_q, mb_v, ds_q, ds_v, ix_q, ml_v, ml_o,
        small, jnp.zeros((d, 16 - small.shape[1]), w_in.dtype)], axis=1).T
    assert wn.shape == (d, _N_TOTAL) and wt.shape == (_T_TOTAL, d)
    return wn.astype(BF16), wt.astype(BF16)


def _in_proj(x2, g, wn, wt, B, S):
    T, D = x2.shape
    tm = min(PROJ_TM, S)
    nsb = S // tm
    row = lambda n: pl.BlockSpec((tm, n), lambda i: (i, 0))
    tr = lambda n: pl.BlockSpec((1, n, tm), lambda i: (i // nsb, 0, i % nsb))
    full = lambda a: pl.BlockSpec(a.shape, lambda i: (0,) * a.ndim)
    out_shape = (
        jax.ShapeDtypeStruct((T, 512), F32),
        jax.ShapeDtypeStruct((T, 256), BF16),
        jax.ShapeDtypeStruct((T, 256), BF16),
        jax.ShapeDtypeStruct((T, 128), BF16),
        jax.ShapeDtypeStruct((B, 256, S), BF16),
        jax.ShapeDtypeStruct((B, 256, S), BF16),
        jax.ShapeDtypeStruct((B, 256, S), BF16),
        jax.ShapeDtypeStruct((B, 256, S), BF16),
        jax.ShapeDtypeStruct((B, 256, S), BF16),
        jax.ShapeDtypeStruct((B, 512, S), BF16),
        jax.ShapeDtypeStruct((B, 512, S), F32),
        jax.ShapeDtypeStruct((B, 16, S), F32),
    )
    out_specs = (row(512), row(256), row(256), row(128),
                 tr(256), tr(256), tr(256), tr(256), tr(256), tr(512), tr(512), tr(16))
    return pl.pallas_call(
        _in_proj_kernel,
        grid=(T // tm,),
        in_specs=[row(D), full(g), full(wn), full(wt)],
        out_specs=out_specs,
        out_shape=out_shape,
        compiler_params=_params("parallel"),
    )(x2, g, wn, wt)


def _dot3(a_f32, b_exact):
    b = b_exact.astype(BF16)
    hi = a_f32.astype(BF16)
    r1 = a_f32 - hi.astype(F32)
    mid = r1.astype(BF16)
    lo = (r1 - mid.astype(F32)).astype(BF16)
    return (jnp.dot(hi, b, preferred_element_type=F32) + jnp.dot(mid, b, preferred_element_type=F32)
            + jnp.dot(lo, b, preferred_element_type=F32))


def _mlstm_kernel(qk_ref, vT_ref, oT_ref, smT_ref, conv_ref, bcol_ref, hn_ref,
                  y_ref, xp_scr, st_scr, m_scr, *, L):
    c = pl.program_id(1)
    halo = SUBLANES
    DK, DV = ML_DQK, ML_DV

    @pl.when(c == 0)
    def _():
        xp_scr[0:halo, :] = jnp.zeros((halo, 2 * ML_QK_W), F32)
        st_scr[...] = jnp.zeros_like(st_scr)
        m_scr[...] = jnp.zeros_like(m_scr)

    cur = qk_ref[0]
    xp_scr[halo:halo + L, :] = cur
    base = halo - (ML_CONV - 1)
    acc = conv_ref[0:1, :] * xp_scr[base:base + L, :]
    for j in range(1, ML_CONV):
        acc = acc + conv_ref[j:j + 1, :] * xp_scr[base + j:base + j + L, :]
    xp_scr[0:halo, :] = cur[L - halo:L, :]
    qk = acc * jax.nn.sigmoid(acc)
    qT_all = (qk[:, :ML_QK_W] * (DK ** -0.5)).T.astype(BF16)
    k_all = qk[:, ML_QK_W:].astype(BF16)

    grow = smT_ref[0][0:SUBLANES, :] + bcol_ref[...]
    si = lax.broadcasted_iota(I32, (L, L), 0)
    ji = lax.broadcasted_iota(I32, (L, L), 1)
    causal = si <= ji
    b_row = _dot3(_log_sigmoid(grow), causal)

    for h in range(ML_HEADS):
        qT = qT_all[h * DK:(h + 1) * DK, :]
        kh = k_all[:, h * DK:(h + 1) * DK]
        vT = vT_ref[0, h * DV:(h + 1) * DV, :]
        b_j = b_row[ML_HEADS + h:ML_HEADS + h + 1, :]
        c_row = grow[h:h + 1, :] - b_j
        c_col = jnp.broadcast_to(c_row, (LANES, L)).T
        dlog = jnp.where(causal, b_j + jnp.concatenate([c_col] * (L // LANES), axis=1), NEG_INF)
        m0 = m_scr[h:h + 1, 0:1]
        inter = b_j + m0
        m_t = jnp.maximum(inter, jnp.max(dlog, axis=0, keepdims=True))
        w_inter = jnp.exp(inter - m_t)
        w_intra = jnp.exp(dlog - m_t) * jnp.dot(kh, qT, preferred_element_type=F32)
        st = st_scr[h]
        cq = jnp.dot(st.astype(BF16), qT, preferred_element_type=F32)
        num = w_inter * cq[:DV] + jnp.dot(vT, w_intra.astype(BF16), preferred_element_type=F32)
        den = w_inter * cq[DV:DV + 1] + jnp.sum(w_intra, axis=0, keepdims=True)
        hh = num / jnp.maximum(jnp.abs(den), jnp.exp(-m_t))
        hh = hh * lax.rsqrt(jnp.mean(hh * hh, axis=0, keepdims=True) + EPS)
        rows = slice(h * DV, (h + 1) * DV)
        y_ref[0, :, rows] = (jax.nn.sigmoid(oT_ref[0, rows, :]) * hh * hn_ref[rows, :]).T.astype(y_ref.dtype)

        b_last = b_j[:, L - 1:L]
        a = b_last + c_row
        m_loc = jnp.max(a, axis=1, keepdims=True)
        wa = jnp.exp(a - m_loc)
        lhs = jnp.concatenate([(vT.astype(F32) * wa).astype(BF16),
                               jnp.broadcast_to(wa, (SUBLANES, L)).astype(BF16)], axis=0)
        c_loc = jnp.dot(lhs, kh, preferred_element_type=F32)
        m_new = jnp.maximum(b_last + m0, m_loc)
        s_old = jnp.exp(b_last + m0 - m_new)
        s_loc = jnp.exp(m_loc - m_new)
        st_scr[h] = s_old * st + s_loc * c_loc
        m_scr[h:h + 1, :] = jnp.broadcast_to(m_new, (1, LANES))


def _mlstm(qk, vT, oT, smT, conv_w, i_bias, f_bias, head_norm, B, S):
    L = min(ML_CHUNK, S)
    assert L % LANES == 0
    nc = S // L
    qk = qk.reshape(B, S, 2 * ML_QK_W)
    bcol = jnp.concatenate([i_bias, f_bias]).astype(F32).reshape(2 * ML_HEADS, 1)
    hn = jnp.broadcast_to(head_norm.astype(F32)[:, None], (ML_V_W, L))
    blkT = lambda n: pl.BlockSpec((1, n, L), lambda b, c: (b, 0, c))
    full = lambda a: pl.BlockSpec(a.shape, lambda b, c: (0,) * a.ndim)
    return pl.pallas_call(
        functools.partial(_mlstm_kernel, L=L),
        grid=(B, nc),
        in_specs=[pl.BlockSpec((1, L, 2 * ML_QK_W), lambda b, c: (b, c, 0)),
                  blkT(ML_V_W), blkT(ML_V_W), blkT(16), full(conv_w), full(bcol), full(hn)],
        out_specs=pl.BlockSpec((1, L, ML_V_W), lambda b, c: (b, c, 0)),
        out_shape=jax.ShapeDtypeStruct((B, S, ML_V_W), BF16),
        scratch_shapes=[pltpu.VMEM((L + SUBLANES, 2 * ML_QK_W), F32),
                        pltpu.VMEM((ML_HEADS, ML_DV + SUBLANES, ML_DQK), F32),
                        pltpu.VMEM((SUBLANES, LANES), F32)],
        compiler_params=_params("parallel", "arbitrary"),
    )(qk, vT, oT, smT, conv_w.astype(F32), bcol, hn)


def _block_diag_T(qT, heads, dh):
    rid = lax.broadcasted_iota(I32, qT.shape, 0)
    zero = jnp.zeros_like(qT)
    return jnp.concatenate(
        [jnp.where((rid >= h * dh) & (rid < (h + 1) * dh), qT, zero) for h in range(heads)], axis=1)


def _accumulate(h, p, vT_chunk, l, acc_scr, dh):
    rows = slice(h * dh, (h + 1) * dh)
    acc_scr[rows, :] += jnp.dot(vT_chunk[rows, :], p.astype(BF16), preferred_element_type=F32)
    return l + jnp.sum(p, axis=0, keepdims=True)


def _finish_attention(l, acc_scr, y_ref, heads, dh):
    outs = [acc_scr[h * dh:(h + 1) * dh, :] / l[h] for h in range(heads)]
    y_ref[0] = jnp.concatenate(outs, axis=0).T.astype(y_ref.dtype)


def _max_key_norm2(kb, heads, dh, TQ):
    ch = lax.broadcasted_iota(I32, (heads * dh, LANES), 0) // dh
    hd = lax.broadcasted_iota(I32, (heads * dh, LANES), 1)
    n2 = jnp.dot((kb * kb).astype(BF16), (ch == hd).astype(BF16), preferred_element_type=F32)
    n2 = jnp.max(n2, axis=0, keepdims=True) * (1.0 + 2.0 ** -7)
    return jnp.concatenate([jnp.broadcast_to(n2[:, h:h + 1], (1, TQ)) for h in range(heads)], axis=1)


def _norm_bound(qTf, k2, heads, dh, TQ):
    out = []
    for h in range(heads):
        qh = qTf[h * dh:(h + 1) * dh, :]
        q2 = jnp.sum(qh * qh, axis=0, keepdims=True)
        out.append(jnp.sqrt(q2 * k2[:, h * TQ:(h + 1) * TQ]) * (1.0 + 2.0 ** -8))
    return out


def _underflowed(l):
    small = l[0]
    for lh in l[1:]:
        small = jnp.minimum(small, lh)
    return jnp.logical_not(jnp.min(small) >= SOFTMAX_L_MIN)


def _moba_kernel(qT_ref, k_ref, vT_ref, y_ref, km_scr, kn_scr, acc_scr, s_scr, *, NB, NBP, n_sel):
    H, DH, BS = MOBA_HEADS, MOBA_DH, MOBA_BLOCK
    TQ = BS
    qi = pl.program_id(1)

    @pl.when(qi == 0)
    def _():
        km_scr[...] = jnp.zeros_like(km_scr)
        kn_scr[...] = jnp.zeros_like(kn_scr)

        def body(j, carry):
            kb = k_ref[0, pl.ds(pl.multiple_of(j * BS, BS), BS), :].astype(F32)
            km_scr[pl.ds(j, 1), :] = jnp.mean(kb, axis=0, keepdims=True)
            kn_scr[pl.ds(j, 1), :] = _max_key_norm2(kb, H, DH, TQ)
            return carry
        lax.fori_loop(0, NB, body, 0)

    qT = qT_ref[0]
    qTf = qT.astype(F32)
    km = km_scr[...]
    gate = jnp.concatenate(
        [jnp.dot(km[:, h * DH:(h + 1) * DH], qTf[h * DH:(h + 1) * DH, :], precision=lax.Precision.HIGHEST,
                 preferred_element_type=F32) for h in range(H)], axis=1) * (1.0 / ATT_QSCALE)
    blk = lax.broadcasted_iota(I32, gate.shape, 0)
    g = jnp.where(blk < qi, gate, NEG_INF)
    sels = []
    for _ in range(n_sel):
        mx = jnp.max(g, axis=0, keepdims=True)
        isel = jnp.min(jnp.where(g == mx, blk, NBP), axis=0, keepdims=True)
        sels.append(jnp.where(mx > NEG_INF, isel, -1))
        g = jnp.where(blk == isel, NEG_INF, g)

    qs = _block_diag_T(qT, H, DH)

    def hit_row(j):
        hit = sels[0] == j
        for s in sels[1:]:
            hit = hit | (s == j)
        return hit

    npair = (qi + 1) // 2
    blk_start = lambda j: pl.multiple_of(j * BS, BS)
    blk_scores = lambda j: jnp.dot(k_ref[0, pl.ds(blk_start(j), BS), :], qs, preferred_element_type=F32)
    hcols = lambda row, h: row[:, h * TQ:(h + 1) * TQ]
    causal = lax.broadcasted_iota(I32, (BS, TQ), 0) <= lax.broadcasted_iota(I32, (BS, TQ), 1)

    def diag_scores():
        s_all = blk_scores(qi)
        return [jnp.where(causal, hcols(s_all, h), NEG_INF) for h in range(H)]

    def max_sweep():
        def max_pair(g, m):
            for u in range(2):
                j = 2 * g + u
                m = jnp.maximum(m, jnp.where(hit_row(j), jnp.max(blk_scores(j), axis=0, keepdims=True), NEG_INF))
            return m
        m_all = lax.fori_loop(0, npair, max_pair, jnp.full((1, H * TQ), NEG_INF, F32))
        s_diag = diag_scores()
        return [jnp.maximum(hcols(m_all, h), jnp.max(s_diag[h], axis=0, keepdims=True)) for h in range(H)]

    def sum_sweep(m):
        acc_scr[...] = jnp.zeros_like(acc_scr)

        def sum_group(width):
            def body(g, l):
                for u in range(width):
                    s_scr[u] = blk_scores(width * g + u)
                for u in range(width):
                    j = width * g + u
                    hit = hit_row(j)
                    vT_blk = vT_ref[0, :, pl.ds(blk_start(j), BS)]
                    l = tuple(_accumulate(h, jnp.exp2(s_scr[u, :, h * TQ:(h + 1) * TQ]
                                                      - jnp.where(hcols(hit, h), m[h], jnp.inf)),
                                          vT_blk, l[h], acc_scr, DH) for h in range(H))
                return l
            return body
        nwide = qi // ATT_STAGE
        l = lax.fori_loop(0, nwide, sum_group(ATT_STAGE), (jnp.zeros((1, TQ), F32),) * H)
        l = lax.fori_loop((ATT_STAGE // 2) * nwide, npair, sum_group(2), l)
        s_diag = diag_scores()
        vT_diag = vT_ref[0, :, pl.ds(blk_start(qi), BS)]
        return [_accumulate(h, jnp.exp2(s_diag[h] - m[h]), vT_diag, l[h], acc_scr, DH) for h in range(H)]

    blk_n = lax.broadcasted_iota(I32, kn_scr.shape, 0)
    k2 = jnp.max(jnp.where(blk_n <= qi, kn_scr[...], 0.0), axis=0, keepdims=True)
    l = sum_sweep(_norm_bound(qTf, k2, H, DH, TQ))
    _finish_attention(l, acc_scr, y_ref, H, DH)

    @pl.when(_underflowed(l))
    def _():
        _finish_attention(sum_sweep(max_sweep()), acc_scr, y_ref, H, DH)


def _moba(qT, k, vT, B, S):
    BS = MOBA_BLOCK
    assert S % BS == 0
    NB = S // BS
    NBP = -(-NB // SUBLANES) * SUBLANES
    n_sel = max(min(MOBA_TOPK, NB - 1), 1)
    k = k.reshape(B, S, MOBA_W)
    return pl.pallas_call(
        functools.partial(_moba_kernel, NB=NB, NBP=NBP, n_sel=n_sel),
        grid=(B, NB),
        in_specs=[pl.BlockSpec((1, MOBA_W, BS), lambda b, i: (b, 0, i)),
                  pl.BlockSpec((1, S, MOBA_W), lambda b, i: (b, 0, 0)),
                  pl.BlockSpec((1, MOBA_W, S), lambda b, i: (b, 0, 0))],
        out_specs=pl.BlockSpec((1, BS, MOBA_W), lambda b, i: (b, i, 0)),
        out_shape=jax.ShapeDtypeStruct((B, S, MOBA_W), BF16),
        scratch_shapes=[pltpu.VMEM((NBP, MOBA_W), F32),
                        pltpu.VMEM((NBP, MOBA_HEADS * BS), F32),
                        pltpu.VMEM((MOBA_W, BS), F32),
                        pltpu.VMEM((ATT_STAGE, BS, MOBA_HEADS * BS), F32)],
        compiler_params=_params("parallel", "arbitrary"),
    )(qT, k, vT)


def _dsa_kernel(qT_ref, k_ref, vT_ref, iqT_ref, ik_ref, smT_ref, y_ref,
                e_scr, gmax_scr, acc_scr, thr_scr, kn_scr, s_scr, *, TQ, topk, spow, n_chunks):
    H, DH = DSA_HEADS, DSA_DH
    KC = TQ
    qi = pl.program_id(1)
    nch = qi + 1
    npair = (nch + 1) // 2
    chunk_start = lambda c: pl.multiple_of(c * KC, KC)

    @pl.when(qi == 0)
    def _():
        kn_scr[...] = jnp.zeros_like(kn_scr)

        def body(c, carry):
            kn_scr[pl.ds(c, 1), :] = _max_key_norm2(k_ref[0, pl.ds(chunk_start(c), KC), :].astype(F32), H, DH, TQ)
            return carry
        lax.fori_loop(0, n_chunks, body, 0)
    sum8 = lambda b: jnp.sum(b.astype(I32).reshape(KC // SUBLANES, SUBLANES, TQ), axis=0)
    any_row = lambda r: jnp.max(r.astype(I32)) > 0

    iqT = iqT_ref[0]
    iq_cat = jnp.concatenate([iqT[h * IDX_DH:(h + 1) * IDX_DH, :] for h in range(IDX_HEADS)], axis=1)
    iw = smT_ref[0][2 * ML_HEADS:2 * ML_HEADS + IDX_HEADS, :] * (IDX_HEADS ** -0.5 * IDX_DH ** -0.5)
    t_pos = qi * TQ + lax.broadcasted_iota(I32, (KC, TQ), 1)
    s_off = lax.broadcasted_iota(I32, (KC, TQ), 0)
    gmax_scr[...] = jnp.full((KC, TQ), NEG_INF, F32)

    def score_group(width, masked):
        def body(g, tally):
            for u in range(width):
                ikc = ik_ref[0, pl.ds(chunk_start(width * g + u), KC), :][:, :IDX_DH]
                s_scr[u] = jnp.dot(ikc, iq_cat, preferred_element_type=F32)
            for u in range(width):
                c = width * g + u
                sc = iw[0:1, :] * jnp.maximum(s_scr[u, :, 0:TQ], 0.0)
                for h in range(1, IDX_HEADS):
                    sc = sc + iw[h:h + 1, :] * jnp.maximum(s_scr[u, :, h * TQ:(h + 1) * TQ], 0.0)
                if masked:
                    sc = jnp.where(c * KC + s_off <= t_pos, sc, NEG_INF)
                e_scr[pl.ds(chunk_start(c), KC), :] = sc
                gmax_scr[...] = jnp.maximum(gmax_scr[...], sc)
                tally = tally + jnp.sum(jnp.where(sc > 0.0, 1, jnp.where(sc == 0.0, 1 << 16, 0))
                                        .reshape(KC // SUBLANES, SUBLANES, TQ), axis=0)
            return tally
        return body
    assert n_chunks * KC < (1 << 16)
    zero8 = jnp.zeros((SUBLANES, TQ), I32)
    tally = lax.fori_loop(0, (npair - 1) // 2, score_group(4, False), zero8)
    tally = lax.fori_loop(2 * ((npair - 1) // 2), npair - 1, score_group(2, False), tally)
    tally = jnp.sum(score_group(2, True)(npair - 1, tally), axis=0, keepdims=True)
    n_pos = tally & 0xFFFF
    n_nonneg = n_pos + (tally >> 16)

    def sweep_keys(fold, merge, init):
        def group(width):
            def body(g, acc):
                return tuple(merge(acc[u], fold(e_scr[pl.ds(chunk_start(width * g + u), KC), :]))
                             for u in range(width))
            return body
        acc = lax.fori_loop(0, npair // 2, group(4), (init,) * 4)
        a, b = lax.fori_loop(2 * (npair // 2), npair, group(2), (merge(acc[0], acc[2]), merge(acc[1], acc[3])))
        return merge(a, b)

    def count_gt(cand):
        return jnp.sum(sweep_keys(lambda x: sum8(x > cand), lambda p, q: p + q, zero8), axis=0, keepdims=True)

    def min_gt(cand):
        fold = lambda x: jnp.min(jnp.where(x > cand, x, jnp.inf).reshape(KC // SUBLANES, SUBLANES, TQ), axis=0)
        return jnp.min(sweep_keys(fold, jnp.minimum, jnp.full((SUBLANES, TQ), jnp.inf, F32)), axis=0, keepdims=True)

    def rewrite(rows, thr):
        def body(c, carry):
            x = e_scr[pl.ds(chunk_start(c), KC), :]
            rank = (2 * spow - (c * KC + s_off)).astype(F32)
            new = jnp.where(x > thr, jnp.inf, jnp.where(x == thr, rank, NEG_INF))
            e_scr[pl.ds(chunk_start(c), KC), :] = jnp.where(rows, new, x)
            return carry
        lax.fori_loop(0, nch, body, 0)

    def halve(_, c):
        lo, hi, act, tie, n_lo = c
        mid = lo + (hi - lo) * 0.5
        inside = (mid > lo) & (mid < hi)
        cnt = count_gt(mid)
        live = act > 0.0
        run = live & inside
        up = run & (cnt >= topk)
        lo = jnp.where(up, mid, lo)
        n_lo = jnp.where(up, cnt, n_lo)
        hi = jnp.where(run & (cnt < topk), mid, hi)
        collapsed = live & jnp.logical_not(inside)
        tie = jnp.where(collapsed, 1.0, tie)
        act = jnp.where(collapsed | (run & (cnt == topk)), 0.0, act)
        return lo, hi, act, tie, n_lo

    def excess(state):
        return jnp.max(jnp.where(state[2] > 0.0, state[4] - topk, 0).astype(F32))

    def bisect(lo, hi, act, warmup):
        state = (lo, hi, act, jnp.zeros_like(act), jnp.full((1, TQ), 1 << 20, I32))
        state = lax.fori_loop(0, warmup, halve, state)

        def one_more(c):
            state = halve(0, c[1])
            return c[0] + 1, state, excess(state)
        near = lax.while_loop(lambda c: jnp.logical_and(c[0] < BISECT_CAP, c[2] > BISECT_DROP), one_more,
                              (jnp.int32(warmup), state, excess(state)))
        lo, hi, act, tie, n_lo = near[1]
        cur, left = lo, jnp.where(act > 0.0, n_lo - topk, 0)
        for _ in range(BISECT_DROP):
            cur = jnp.where(left > 0, min_gt(cur), cur)
            left = left - 1
        done = (act > 0.0) & (count_gt(cur) == topk)
        lo = jnp.where(done, cur, lo)
        act = jnp.where(done, 0.0, act)

        def body(c):
            state = c[1]
            for _ in range(BISECT_STEPS_PER_CHECK):
                state = halve(0, state)
            return c[0] + BISECT_STEPS_PER_CHECK, state, jnp.max(state[2])
        out = lax.while_loop(lambda c: jnp.logical_and(c[0] < BISECT_CAP, c[2] > 0.0), body,
                             (near[0], (lo, hi, act, tie, n_lo), jnp.max(act)))
        return out[1][0], out[1][1], out[1][3]

    fmax = float(jnp.finfo(F32).max)
    gm = gmax_scr[...]
    g_lo = jnp.min(gm, axis=0, keepdims=True)
    g_hi = jnp.max(gm, axis=0, keepdims=True)
    n_vis = qi * TQ + lax.broadcasted_iota(I32, (1, TQ), 1) + 1
    few = n_vis <= topk
    zero_tie = (n_pos < topk) & (n_nonneg >= topk) & jnp.logical_not(few)

    @pl.when(any_row(zero_tie))
    def _():
        rewrite(zero_tie, jnp.zeros((1, TQ), F32))

    rank_lo, rank_hi = float(spow), float(2 * spow + 1)
    below = jnp.maximum(g_lo, -fmax)
    below = below - jnp.abs(below) * (2.0 ** -10) - 1e-30
    lo0 = jnp.where(zero_tie, rank_lo, jnp.maximum(below, -fmax))
    hi0 = jnp.where(zero_tie, rank_hi, g_hi)
    lo, hi, tie = bisect(lo0, hi0, jnp.where(few, 0.0, 1.0), BISECT_WARMUP)
    thr_scr[0:1, :] = jnp.where(few, NEG_INF, lo)

    @pl.when(jnp.max(tie) > 0.0)
    def _():
        rows = tie > 0.0
        rewrite(rows, hi)
        lo2, _, _ = bisect(jnp.full((1, TQ), rank_lo, F32), jnp.full((1, TQ), rank_hi, F32), tie, 0)
        thr_scr[0:1, :] = jnp.where(rows, lo2, thr_scr[0:1, :])

    thr = thr_scr[0:1, :]
    qT = qT_ref[0]
    qs = _block_diag_T(qT, H, DH)
    hcols = lambda row, h: row[:, h * TQ:(h + 1) * TQ]

    def chunk_scores(c):
        start = chunk_start(c)
        sel = e_scr[pl.ds(start, KC), :] > thr
        s_all = jnp.dot(k_ref[0, pl.ds(start, KC), :], qs, preferred_element_type=F32)
        return sel, s_all

    def max_sweep():
        def max_pair(g, m):
            for u in range(2):
                sel, s_all = chunk_scores(2 * g + u)
                m = tuple(jnp.maximum(m[h], jnp.max(jnp.where(sel, hcols(s_all, h), NEG_INF), axis=0, keepdims=True))
                          for h in range(H))
            return m
        m = lax.fori_loop(0, npair, max_pair, (jnp.full((1, TQ), NEG_INF, F32),) * H)
        return [jnp.where(mh == NEG_INF, 0.0, mh) for mh in m]

    def sum_sweep(m):
        acc_scr[...] = jnp.zeros_like(acc_scr)

        def sum_group(width):
            def body(g, l):
                for u in range(width):
                    s_scr[u] = jnp.dot(k_ref[0, pl.ds(chunk_start(width * g + u), KC), :], qs,
                                       preferred_element_type=F32)
                for u in range(width):
                    start = chunk_start(width * g + u)
                    sel = e_scr[pl.ds(start, KC), :] > thr
                    vT_chunk = vT_ref[0, :, pl.ds(start, KC)]
                    l = tuple(_accumulate(h, jnp.exp2(s_scr[u, :, h * TQ:(h + 1) * TQ] + jnp.where(sel, -m[h], NEG_INF)),
                                          vT_chunk, l[h], acc_scr, DH) for h in range(H))
                return l
            return body
        nwide = (2 * npair) // ATT_STAGE
        l = lax.fori_loop(0, nwide, sum_group(ATT_STAGE), (jnp.zeros((1, TQ), F32),) * H)
        return lax.fori_loop((ATT_STAGE // 2) * nwide, npair, sum_group(2), l)

    chunk_n = lax.broadcasted_iota(I32, kn_scr.shape, 0)
    k2 = jnp.max(jnp.where(chunk_n < 2 * npair, kn_scr[...], 0.0), axis=0, keepdims=True)
    l = sum_sweep(_norm_bound(qT.astype(F32), k2, H, DH, TQ))
    _finish_attention(l, acc_scr, y_ref, H, DH)

    @pl.when(_underflowed(l))
    def _():
        _finish_attention(sum_sweep(max_sweep()), acc_scr, y_ref, H, DH)


def _dsa(qT, k, vT, iqT, ik, smT, B, S):
    TQ = min(ATT_TQ, S)
    topk = min(DSA_TOPK_MAX, S // 4)
    assert S % (2 * TQ) == 0 and topk <= TQ
    spow = 1 << max((S - 1).bit_length(), 1)
    k = k.reshape(B, S, DSA_W)
    ik = ik.reshape(B, S, LANES)
    qblk = lambda n: pl.BlockSpec((1, n, TQ), lambda b, i: (b, 0, i))
    return pl.pallas_call(
        functools.partial(_dsa_kernel, TQ=TQ, topk=topk, spow=spow, n_chunks=S // TQ),
        grid=(B, S // TQ),
        in_specs=[qblk(DSA_W),
                  pl.BlockSpec((1, S, DSA_W), lambda b, i: (b, 0, 0)),
                  pl.BlockSpec((1, DSA_W, S), lambda b, i: (b, 0, 0)),
                  qblk(IDX_HEADS * IDX_DH),
                  pl.BlockSpec((1, S, LANES), lambda b, i: (b, 0, 0)),
                  qblk(16)],
        out_specs=pl.BlockSpec((1, TQ, DSA_W), lambda b, i: (b, i, 0)),
        out_shape=jax.ShapeDtypeStruct((B, S, DSA_W), BF16),
        scratch_shapes=[pltpu.VMEM((S, TQ), F32),
                        pltpu.VMEM((TQ, TQ), F32),
                        pltpu.VMEM((DSA_W, TQ), F32),
                        pltpu.VMEM((SUBLANES, TQ), F32),
                        pltpu.VMEM((-(-(S // TQ) // SUBLANES) * SUBLANES, DSA_HEADS * TQ), F32),
                        pltpu.VMEM((ATT_STAGE, TQ, DSA_HEADS * TQ), F32)],
        compiler_params=_params("parallel", "arbitrary"),
    )(qT, k, vT, iqT, ik, smT)


def _out_mlp_kernel(yml_ref, ymb_ref, yds_ref, x_ref, wout_ref, gpost_ref, gpre_ref,
                    w1_ref, w2_ref, g2_ref, out_ref, x1_scr, h_scr, acc_scr):
    f = pl.program_id(1)

    @pl.when(f == 0)
    def _():
        slab = x_ref.shape[0] // MLP_SLABS
        for r in range(MLP_SLABS):
            rows = slice(r * slab, (r + 1) * slab)
            mix = jnp.dot(yml_ref[rows, :], wout_ref[0:ML_V_W, :], preferred_element_type=F32)
            mix = mix + jnp.dot(ymb_ref[rows, :], wout_ref[ML_V_W:ML_V_W + MOBA_W, :], preferred_element_type=F32)
            mix = mix + jnp.dot(yds_ref[rows, :], wout_ref[ML_V_W + MOBA_W:, :], preferred_element_type=F32)
            x1 = x_ref[rows, :] + _rms(mix, gpost_ref[...])
            x1_scr[rows, :] = x1
            h_scr[rows, :] = _rms(x1, gpre_ref[...]).astype(BF16)
        acc_scr[...] = jnp.zeros_like(acc_scr)

    u = jnp.maximum(jnp.dot(h_scr[...], w1_ref[...], preferred_element_type=F32), 0.0)
    acc_scr[...] += jnp.dot((u * u).astype(BF16), w2_ref[...], preferred_element_type=F32)

    @pl.when(f == pl.num_programs(1) - 1)
    def _():
        out_ref[...] = x1_scr[...] + _rms(acc_scr[...], g2_ref[...])


def _out_mlp(yml, ymb, yds, x2, w_out, g_post, g_pre, w1, w2, g2):
    T, D = x2.shape
    tm = min(MLP_TM, T)
    tf = min(MLP_TF, D_FF)
    row = lambda n: pl.BlockSpec((tm, n), lambda i, f: (i, 0))
    full = lambda a: pl.BlockSpec(a.shape, lambda i, f: (0,) * a.ndim)
    return pl.pallas_call(
        _out_mlp_kernel,
        grid=(T // tm, D_FF // tf),
        in_specs=[row(ML_V_W), row(MOBA_W), row(DSA_W), row(D), full(w_out), full(g_post), full(g_pre),
                  pl.BlockSpec((D, tf), lambda i, f: (0, f)),
                  pl.BlockSpec((tf, D), lambda i, f: (f, 0)),
                  full(g2)],
        out_specs=row(D),
        out_shape=jax.ShapeDtypeStruct((T, D), F32),
        scratch_shapes=[pltpu.VMEM((tm, D), F32), pltpu.VMEM((tm, D), BF16), pltpu.VMEM((tm, D), F32)],
        compiler_params=_params("parallel", "arbitrary"),
    )(yml, ymb, yds, x2, w_out, g_post, g_pre, w1, w2, g2)


def kernel(x, norm_mix_pre, w_in, ml_conv, ml_i_bias, ml_f_bias, ml_head_norm, w_out, norm_mix_post,
           norm_mlp_pre, w_ff1, w_ff2, norm_mlp_post):
    B, S, D = x.shape
    depth = w_in.shape[0]
    x2 = x.reshape(B * S, D)
    gain = lambda g: g.reshape(1, D).astype(F32)
    for l in range(depth):
        wn, wt = _prep_in_weights(w_in[l])
        (qk, mbk, dsk, ixk, mbqT, mbvT, dsqT, dsvT, ixqT, mlvT, mloT, smT) = _in_proj(
            x2, gain(norm_mix_pre[l]), wn, wt, B, S)
        y_ml = _mlstm(qk, mlvT, mloT, smT, ml_conv[l], ml_i_bias[l], ml_f_bias[l], ml_head_norm[l], B, S)
        y_mb = _moba(mbqT, mbk, mbvT, B, S)
        y_ds = _dsa(dsqT, dsk, dsvT, ixqT, ixk, smT, B, S)
        x2 = _out_mlp(y_ml.reshape(B * S, ML_V_W), y_mb.reshape(B * S, MOBA_W), y_ds.reshape(B * S, DSA_W),
                      x2, w_out[l].astype(BF16), gain(norm_mix_post[l]), gain(norm_mlp_pre[l]),
                      w_ff1[l].astype(BF16), w_ff2[l].astype(BF16), gain(norm_mlp_post[l]))
    return x2.reshape(B, S, D)
```

```python
import functools

import jax
import jax.numpy as jnp
from jax import lax
from jax.experimental import pallas as pl
from jax.experimental.pallas import tpu as pltpu

F32 = jnp.float32
BF16 = jnp.bfloat16
I32 = jnp.int32

D_MODEL = 1024
ML_HEADS, ML_DQK, ML_DV, ML_CONV = 4, 64, 128, 4
MOBA_HEADS, MOBA_DH, MOBA_BLOCK, MOBA_TOPK = 4, 64, 256, 3
DSA_HEADS, DSA_DH, IDX_HEADS, IDX_DH, DSA_TOPK_MAX = 4, 64, 4, 64, 256
D_FF = 4 * D_MODEL
EPS = 1e-6

ML_QK_W = ML_HEADS * ML_DQK
ML_V_W = ML_HEADS * ML_DV
MOBA_W = MOBA_HEADS * MOBA_DH
DSA_W = DSA_HEADS * DSA_DH
IN_SPLITS = (ML_QK_W, ML_QK_W, ML_V_W, ML_V_W, ML_HEADS, ML_HEADS,
             MOBA_W, MOBA_W, MOBA_W,
             DSA_W, DSA_W, DSA_W, IDX_HEADS * IDX_DH, IDX_DH, IDX_HEADS)

LANES = 128
SUBLANES = 8
VMEM_LIMIT = 52 * 1024 * 1024

ML_CHUNK = 512
ATT_TQ = 256
ATT_STAGE = 8
PROJ_TM = 1024
MLP_TM = 1024
MLP_TF = 512
MLP_SLABS = 4

NEG_INF = float("-inf")
BISECT_CAP = 300
BISECT_WARMUP = 12
BISECT_DROP = 2
BISECT_STEPS_PER_CHECK = 2
SOFTMAX_L_MIN = 1e-30
assert MOBA_DH == DSA_DH
ATT_QSCALE = MOBA_DH ** -0.5 * 1.4426950408889634

_NT = (((1,), (1,)), ((), ()))


def _rms(x, g):
    return x * lax.rsqrt(jnp.mean(x * x, axis=-1, keepdims=True) + EPS) * g


def _log_sigmoid(x):
    return jnp.minimum(x, 0.0) - jnp.log1p(jnp.exp(-jnp.abs(x)))


def _params(*sem):
    return pltpu.CompilerParams(dimension_semantics=sem, vmem_limit_bytes=VMEM_LIMIT)


_N_QK, _N_MBK, _N_DSK, _N_IXK = 0, 512, 768, 1024
_N_TOTAL = 1152
_T_MBQ, _T_MBV, _T_DSQ, _T_DSV, _T_IXQ, _T_MLV, _T_MLO, _T_SM = 0, 256, 512, 768, 1024, 1280, 1792, 2304
_T_TOTAL = 2320


def _in_proj_kernel(x_ref, g_ref, wn_ref, wt_ref,
                    qk_ref, mbk_ref, dsk_ref, ixk_ref,
                    mbqT_ref, mbvT_ref, dsqT_ref, dsvT_ref, ixqT_ref, mlvT_ref, mloT_ref, smT_ref):
    h = _rms(x_ref[...], g_ref[...]).astype(BF16)

    def mm(a, n):
        return jnp.dot(h, wn_ref[:, a:a + n], preferred_element_type=F32)

    def mt(a, n):
        return lax.dot_general(wt_ref[a:a + n, :], h, _NT, preferred_element_type=F32)

    qk_ref[...] = mm(_N_QK, 512)
    mbk_ref[...] = mm(_N_MBK, 256).astype(BF16)
    dsk_ref[...] = mm(_N_DSK, 256).astype(BF16)
    ixk_ref[...] = mm(_N_IXK, 128).astype(BF16)
    mbqT_ref[0] = (mt(_T_MBQ, 256) * ATT_QSCALE).astype(BF16)
    mbvT_ref[0] = mt(_T_MBV, 256).astype(BF16)
    dsqT_ref[0] = (mt(_T_DSQ, 256) * ATT_QSCALE).astype(BF16)
    dsvT_ref[0] = mt(_T_DSV, 256).astype(BF16)
    ixqT_ref[0] = mt(_T_IXQ, 256).astype(BF16)
    mlvT_ref[0] = mt(_T_MLV, 512).astype(BF16)
    mloT_ref[0] = mt(_T_MLO, 512)
    smT_ref[0] = mt(_T_SM, 16)


def _prep_in_weights(w_in):
    pts, acc = [], 0
    for n in IN_SPLITS:
        pts.append((acc, acc + n))
        acc += n
    col = lambda i: w_in[:, pts[i][0]:pts[i][1]]
    (ml_q, ml_k, ml_v, ml_o, ml_i, ml_f, mb_q, mb_k, mb_v,
     ds_q, ds_k, ds_v, ix_q, ix_k, ix_w) = [col(i) for i in range(len(IN_SPLITS))]
    d = w_in.shape[0]
    small = jnp.concatenate([ml_i, ml_f, ix_w], axis=1)
    wn = jnp.concatenate([
        ml_q, ml_k, mb_k, ds_k,
        ix_k, jnp.zeros((d, 128 - IDX_DH), w_in.dtype)], axis=1)
    wt = jnp.concatenate([
        mb_q, mb_v, ds_q, ds_v, ix_q, ml_v, ml_o,
        small, jnp.zeros((d, 16 - small.shape[1]), w_in.dtype)], axis=1).T
    assert wn.shape == (d, _N_TOTAL) and wt.shape == (_T_TOTAL, d)
    return wn.astype(BF16), wt.astype(BF16)


def _in_proj(x2, g, wn, wt, B, S):
    T, D = x2.shape
    tm = min(PROJ_TM, S)
    nsb = S // tm
    row = lambda n: pl.BlockSpec((tm, n), lambda i: (i, 0))
    tr = lambda n: pl.BlockSpec((1, n, tm), lambda i: (i // nsb, 0, i % nsb))
    full = lambda a: pl.BlockSpec(a.shape, lambda i: (0,) * a.ndim)
    out_shape = (
        jax.ShapeDtypeStruct((T, 512), F32),
        jax.ShapeDtypeStruct((T, 256), BF16),
        jax.ShapeDtypeStruct((T, 256), BF16),
        jax.ShapeDtypeStruct((T, 128), BF16),
        jax.ShapeDtypeStruct((B, 256, S), BF16),
        jax.ShapeDtypeStruct((B, 256, S), BF16),
        jax.ShapeDtypeStruct((B, 256, S), BF16),
        jax.ShapeDtypeStruct((B, 256, S), BF16),
        jax.ShapeDtypeStruct((B, 256, S), BF16),
        jax.ShapeDtypeStruct((B, 512, S), BF16),
        jax.ShapeDtypeStruct((B, 512, S), F32),
        jax.ShapeDtypeStruct((B, 16, S), F32),
    )
    out_specs = (row(512), row(256), row(256), row(128),
                 tr(256), tr(256), tr(256), tr(256), tr(256), tr(512), tr(512), tr(16))
    return pl.pallas_call(
        _in_proj_kernel,
        grid=(T // tm,),
        in_specs=[row(D), full(g), full(wn), full(wt)],
        out_specs=out_specs,
        out_shape=out_shape,
        compiler_params=_params("parallel"),
    )(x2, g, wn, wt)


def _dot3(a_f32, b_exact):
    b = b_exact.astype(BF16)
    hi = a_f32.astype(BF16)
    r1 = a_f32 - hi.astype(F32)
    mid = r1.astype(BF16)
    lo = (r1 - mid.astype(F32)).astype(BF16)
    return (jnp.dot(hi, b, preferred_element_type=F32) + jnp.dot(mid, b, preferred_element_type=F32)
            + jnp.dot(lo, b, preferred_element_type=F32))


def _mlstm_kernel(qk_ref, vT_ref, oT_ref, smT_ref, conv_ref, bcol_ref, hn_ref,
                  y_ref, xp_scr, st_scr, m_scr, *, L):
    c = pl.program_id(1)
    halo = SUBLANES
    DK, DV = ML_DQK, ML_DV

    @pl.when(c == 0)
    def _():
        xp_scr[0:halo, :] = jnp.zeros((halo, 2 * ML_QK_W), F32)
        st_scr[...] = jnp.zeros_like(st_scr)
        m_scr[...] = jnp.zeros_like(m_scr)

    cur = qk_ref[0]
    xp_scr[halo:halo + L, :] = cur
    base = halo - (ML_CONV - 1)
    acc = conv_ref[0:1, :] * xp_scr[base:base + L, :]
    for j in range(1, ML_CONV):
        acc = acc + conv_ref[j:j + 1, :] * xp_scr[base + j:base + j + L, :]
    xp_scr[0:halo, :] = cur[L - halo:L, :]
    qk = acc * jax.nn.sigmoid(acc)
    qT_all = (qk[:, :ML_QK_W] * (DK ** -0.5)).T.astype(BF16)
    k_all = qk[:, ML_QK_W:].astype(BF16)

    grow = smT_ref[0][0:SUBLANES, :] + bcol_ref[...]
    si = lax.broadcasted_iota(I32, (L, L), 0)
    ji = lax.broadcasted_iota(I32, (L, L), 1)
    causal = si <= ji
    b_row = _dot3(_log_sigmoid(grow), causal)

    for h in range(ML_HEADS):
        qT = qT_all[h * DK:(h + 1) * DK, :]
        kh = k_all[:, h * DK:(h + 1) * DK]
        vT = vT_ref[0, h * DV:(h + 1) * DV, :]
        b_j = b_row[ML_HEADS + h:ML_HEADS + h + 1, :]
        c_row = grow[h:h + 1, :] - b_j
        c_col = jnp.broadcast_to(c_row, (LANES, L)).T
        dlog = jnp.where(causal, b_j + jnp.concatenate([c_col] * (L // LANES), axis=1), NEG_INF)
        m0 = m_scr[h:h + 1, 0:1]
        inter = b_j + m0
        m_t = jnp.maximum(inter, jnp.max(dlog, axis=0, keepdims=True))
        w_inter = jnp.exp(inter - m_t)
        w_intra = jnp.exp(dlog - m_t) * jnp.dot(kh, qT, preferred_element_type=F32)
        st = st_scr[h]
        cq = jnp.dot(st.astype(BF16), qT, preferred_element_type=F32)
        num = w_inter * cq[:DV] + jnp.dot(vT, w_intra.astype(BF16), preferred_element_type=F32)
        den = w_inter * cq[DV:DV + 1] + jnp.sum(w_intra, axis=0, keepdims=True)
        hh = num / jnp.maximum(jnp.abs(den), jnp.exp(-m_t))
        hh = hh * lax.rsqrt(jnp.mean(hh * hh, axis=0, keepdims=True) + EPS)
        rows = slice(h * DV, (h + 1) * DV)
        y_ref[0, :, rows] = (jax.nn.sigmoid(oT_ref[0, rows, :]) * hh * hn_ref[rows, :]).T.astype(y_ref.dtype)

        b_last = b_j[:, L - 1:L]
        a = b_last + c_row
        m_loc = jnp.max(a, axis=1, keepdims=True)
        wa = jnp.exp(a - m_loc)
        lhs = jnp.concatenate([(vT.astype(F32) * wa).astype(BF16),
                               jnp.broadcast_to(wa, (SUBLANES, L)).astype(BF16)], axis=0)
        c_loc = jnp.dot(lhs, kh, preferred_element_type=F32)
        m_new = jnp.maximum(b_last + m0, m_loc)
        s_old = jnp.exp(b_last + m0 - m_new)
        s_loc = jnp.exp(m_loc - m_new)
        st_scr[h] = s_old * st + s_loc * c_loc
        m_scr[h:h + 1, :] = jnp.broadcast_to(m_new, (1, LANES))


def _mlstm(qk, vT, oT, smT, conv_w, i_bias, f_bias, head_norm, B, S):
    L = min(ML_CHUNK, S)
    assert L % LANES == 0
    nc = S // L
    qk = qk.reshape(B, S, 2 * ML_QK_W)
    bcol = jnp.concatenate([i_bias, f_bias]).astype(F32).reshape(2 * ML_HEADS, 1)
    hn = jnp.broadcast_to(head_norm.astype(F32)[:, None], (ML_V_W, L))
    blkT = lambda n: pl.BlockSpec((1, n, L), lambda b, c: (b, 0, c))
    full = lambda a: pl.BlockSpec(a.shape, lambda b, c: (0,) * a.ndim)
    return pl.pallas_call(
        functools.partial(_mlstm_kernel, L=L),
        grid=(B, nc),
        in_specs=[pl.BlockSpec((1, L, 2 * ML_QK_W), lambda b, c: (b, c, 0)),
                  blkT(ML_V_W), blkT(ML_V_W), blkT(16), full(conv_w), full(bcol), full(hn)],
        out_specs=pl.BlockSpec((1, L, ML_V_W), lambda b, c: (b, c, 0)),
        out_shape=jax.ShapeDtypeStruct((B, S, ML_V_W), BF16),
        scratch_shapes=[pltpu.VMEM((L + SUBLANES, 2 * ML_QK_W), F32),
                        pltpu.VMEM((ML_HEADS, ML_DV + SUBLANES, ML_DQK), F32),
                        pltpu.VMEM((SUBLANES, LANES), F32)],
        compiler_params=_params("parallel", "arbitrary"),
    )(qk, vT, oT, smT, conv_w.astype(F32), bcol, hn)


def _block_diag_T(qT, heads, dh):
    rid = lax.broadcasted_iota(I32, qT.shape, 0)
    zero = jnp.zeros_like(qT)
    return jnp.concatenate(
        [jnp.where((rid >= h * dh) & (rid < (h + 1) * dh), qT, zero) for h in range(heads)], axis=1)


def _accumulate(h, p, vT_chunk, l, acc_scr, dh):
    rows = slice(h * dh, (h + 1) * dh)
    acc_scr[rows, :] += jnp.dot(vT_chunk[rows, :], p.astype(BF16), preferred_element_type=F32)
    return l + jnp.sum(p, axis=0, keepdims=True)


def _finish_attention(l, acc_scr, y_ref, heads, dh):
    outs = [acc_scr[h * dh:(h + 1) * dh, :] / l[h] for h in range(heads)]
    y_ref[0] = jnp.concatenate(outs, axis=0).T.astype(y_ref.dtype)


def _max_key_norm2(kb, heads, dh, TQ):
    ch = lax.broadcasted_iota(I32, (heads * dh, LANES), 0) // dh
    hd = lax.broadcasted_iota(I32, (heads * dh, LANES), 1)
    n2 = jnp.dot((kb * kb).astype(BF16), (ch == hd).astype(BF16), preferred_element_type=F32)
    n2 = jnp.max(n2, axis=0, keepdims=True) * (1.0 + 2.0 ** -7)
    return jnp.concatenate([jnp.broadcast_to(n2[:, h:h + 1], (1, TQ)) for h in range(heads)], axis=1)


def _norm_bound(qTf, k2, heads, dh, TQ):
    out = []
    for h in range(heads):
        qh = qTf[h * dh:(h + 1) * dh, :]
        q2 = jnp.sum(qh * qh, axis=0, keepdims=True)
        out.append(jnp.sqrt(q2 * k2[:, h * TQ:(h + 1) * TQ]) * (1.0 + 2.0 ** -8))
    return out


def _underflowed(l):
    small = l[0]
    for lh in l[1:]:
        small = jnp.minimum(small, lh)
    return jnp.logical_not(jnp.min(small) >= SOFTMAX_L_MIN)


def _moba_kernel(qT_ref, k_ref, vT_ref, y_ref, km_scr, kn_scr, acc_scr, s_scr, *, NB, NBP, n_sel):
    H, DH, BS = MOBA_HEADS, MOBA_DH, MOBA_BLOCK
    TQ = BS
    qi = pl.program_id(1)

    @pl.when(qi == 0)
    def _():
        km_scr[...] = jnp.zeros_like(km_scr)
        kn_scr[...] = jnp.zeros_like(kn_scr)

        def body(j, carry):
            kb = k_ref[0, pl.ds(pl.multiple_of(j * BS, BS), BS), :].astype(F32)
            km_scr[pl.ds(j, 1), :] = jnp.mean(kb, axis=0, keepdims=True)
            kn_scr[pl.ds(j, 1), :] = _max_key_norm2(kb, H, DH, TQ)
            return carry
        lax.fori_loop(0, NB, body, 0)

    qT = qT_ref[0]
    qTf = qT.astype(F32)
    km = km_scr[...]
    gate = jnp.concatenate(
        [jnp.dot(km[:, h * DH:(h + 1) * DH], qTf[h * DH:(h + 1) * DH, :], precision=lax.Precision.HIGHEST,
                 preferred_element_type=F32) for h in range(H)], axis=1) * (1.0 / ATT_QSCALE)
    blk = lax.broadcasted_iota(I32, gate.shape, 0)
    g = jnp.where(blk < qi, gate, NEG_INF)
    sels = []
    for _ in range(n_sel):
        mx = jnp.max(g, axis=0, keepdims=True)
        isel = jnp.min(jnp.where(g == mx, blk, NBP), axis=0, keepdims=True)
        sels.append(jnp.where(mx > NEG_INF, isel, -1))
        g = jnp.where(blk == isel, NEG_INF, g)

    qs = _block_diag_T(qT, H, DH)

    def hit_row(j):
        hit = sels[0] == j
        for s in sels[1:]:
            hit = hit | (s == j)
        return hit

    npair = (qi + 1) // 2
    blk_start = lambda j: pl.multiple_of(j * BS, BS)
    blk_scores = lambda j: jnp.dot(k_ref[0, pl.ds(blk_start(j), BS), :], qs, preferred_element_type=F32)
    hcols = lambda row, h: row[:, h * TQ:(h + 1) * TQ]
    causal = lax.broadcasted_iota(I32, (BS, TQ), 0) <= lax.broadcasted_iota(I32, (BS, TQ), 1)

    def diag_scores():
        s_all = blk_scores(qi)
        return [jnp.where(causal, hcols(s_all, h), NEG_INF) for h in range(H)]

    def max_sweep():
        def max_pair(g, m):
            for u in range(2):
                j = 2 * g + u
                m = jnp.maximum(m, jnp.where(hit_row(j), jnp.max(blk_scores(j), axis=0, keepdims=True), NEG_INF))
            return m
        m_all = lax.fori_loop(0, npair, max_pair, jnp.full((1, H * TQ), NEG_INF, F32))
        s_diag = diag_scores()
        return [jnp.maximum(hcols(m_all, h), jnp.max(s_diag[h], axis=0, keepdims=True)) for h in range(H)]

    def sum_sweep(m):
        acc_scr[...] = jnp.zeros_like(acc_scr)

        def sum_group(width):
            def body(g, l):
                for u in range(width):
                    s_scr[u] = blk_scores(width * g + u)
                for u in range(width):
                    j = width * g + u
                    hit = hit_row(j)
                    vT_blk = vT_ref[0, :, pl.ds(blk_start(j), BS)]
                    l = tuple(_accumulate(h, jnp.exp2(s_scr[u, :, h * TQ:(h + 1) * TQ]
                                                      - jnp.where(hcols(hit, h), m[h], jnp.inf)),
                                          vT_blk, l[h], acc_scr, DH) for h in range(H))
                return l
            return body
        nwide = qi // ATT_STAGE
        l = lax.fori_loop(0, nwide, sum_group(ATT_STAGE), (jnp.zeros((1, TQ), F32),) * H)
        l = lax.fori_loop((ATT_STAGE // 2) * nwide, npair, sum_group(2), l)
        s_diag = diag_scores()
        vT_diag = vT_ref[0, :, pl.ds(blk_start(qi), BS)]
        return [_accumulate(h, jnp.exp2(s_diag[h] - m[h]), vT_diag, l[h], acc_scr, DH) for h in range(H)]

    blk_n = lax.broadcasted_iota(I32, kn_scr.shape, 0)
    k2 = jnp.max(jnp.where(blk_n <= qi, kn_scr[...], 0.0), axis=0, keepdims=True)
    l = sum_sweep(_norm_bound(qTf, k2, H, DH, TQ))
    _finish_attention(l, acc_scr, y_ref, H, DH)

    @pl.when(_underflowed(l))
    def _():
        _finish_attention(sum_sweep(max_sweep()), acc_scr, y_ref, H, DH)


def _moba(qT, k, vT, B, S):
    BS = MOBA_BLOCK
    assert S % BS == 0
    NB = S // BS
    NBP = -(-NB // SUBLANES) * SUBLANES
    n_sel = max(min(MOBA_TOPK, NB - 1), 1)
    k = k.reshape(B, S, MOBA_W)
    return pl.pallas_call(
        functools.partial(_moba_kernel, NB=NB, NBP=NBP, n_sel=n_sel),
        grid=(B, NB),
        in_specs=[pl.BlockSpec((1, MOBA_W, BS), lambda b, i: (b, 0, i)),
                  pl.BlockSpec((1, S, MOBA_W), lambda b, i: (b, 0, 0)),
                  pl.BlockSpec((1, MOBA_W, S), lambda b, i: (b, 0, 0))],
        out_specs=pl.BlockSpec((1, BS, MOBA_W), lambda b, i: (b, i, 0)),
        out_shape=jax.ShapeDtypeStruct((B, S, MOBA_W), BF16),
        scratch_shapes=[pltpu.VMEM((NBP, MOBA_W), F32),
                        pltpu.VMEM((NBP, MOBA_HEADS * BS), F32),
                        pltpu.VMEM((MOBA_W, BS), F32),
                        pltpu.VMEM((ATT_STAGE, BS, MOBA_HEADS * BS), F32)],
        compiler_params=_params("parallel", "arbitrary"),
    )(qT, k, vT)


def _dsa_kernel(qT_ref, k_ref, vT_ref, iqT_ref, ik_ref, smT_ref, y_ref,
                e_scr, gmax_scr, acc_scr, thr_scr, kn_scr, s_scr, *, TQ, topk, spow, n_chunks):
    H, DH = DSA_HEADS, DSA_DH
    KC = TQ
    qi = pl.program_id(1)
    nch = qi + 1
    npair = (nch + 1) // 2
    chunk_start = lambda c: pl.multiple_of(c * KC, KC)

    @pl.when(qi == 0)
    def _():
        kn_scr[...] = jnp.zeros_like(kn_scr)

        def body(c, carry):
            kn_scr[pl.ds(c, 1), :] = _max_key_norm2(k_ref[0, pl.ds(chunk_start(c), KC), :].astype(F32), H, DH, TQ)
            return carry
        lax.fori_loop(0, n_chunks, body, 0)
    sum8 = lambda b: jnp.sum(b.astype(I32).reshape(KC // SUBLANES, SUBLANES, TQ), axis=0)
    any_row = lambda r: jnp.max(r.astype(I32)) > 0

    iqT = iqT_ref[0]
    iq_cat = jnp.concatenate([iqT[h * IDX_DH:(h + 1) * IDX_DH, :] for h in range(IDX_HEADS)], axis=1)
    iw = smT_ref[0][2 * ML_HEADS:2 * ML_HEADS + IDX_HEADS, :] * (IDX_HEADS ** -0.5 * IDX_DH ** -0.5)
    t_pos = qi * TQ + lax.broadcasted_iota(I32, (KC, TQ), 1)
    s_off = lax.broadcasted_iota(I32, (KC, TQ), 0)
    gmax_scr[...] = jnp.full((KC, TQ), NEG_INF, F32)

    def score_group(width, masked):
        def body(g, tally):
            for u in range(width):
                ikc = ik_ref[0, pl.ds(chunk_start(width * g + u), KC), :][:, :IDX_DH]
                s_scr[u] = jnp.dot(ikc, iq_cat, preferred_element_type=F32)
            for u in range(width):
                c = width * g + u
                sc = iw[0:1, :] * jnp.maximum(s_scr[u, :, 0:TQ], 0.0)
                for h in range(1, IDX_HEADS):
                    sc = sc + iw[h:h + 1, :] * jnp.maximum(s_scr[u, :, h * TQ:(h + 1) * TQ], 0.0)
                if masked:
                    sc = jnp.where(c * KC + s_off <= t_pos, sc, NEG_INF)
                e_scr[pl.ds(chunk_start(c), KC), :] = sc
                gmax_scr[...] = jnp.maximum(gmax_scr[...], sc)
                tally = tally + jnp.sum(jnp.where(sc > 0.0, 1, jnp.where(sc == 0.0, 1 << 16, 0))
                                        .reshape(KC // SUBLANES, SUBLANES, TQ), axis=0)
            return tally
        return body
    assert n_chunks * KC < (1 << 16)
    zero8 = jnp.zeros((SUBLANES, TQ), I32)
    tally = lax.fori_loop(0, (npair - 1) // 2, score_group(4, False), zero8)
    tally = lax.fori_loop(2 * ((npair - 1) // 2), npair - 1, score_group(2, False), tally)
    tally = jnp.sum(score_group(2, True)(npair - 1, tally), axis=0, keepdims=True)
    n_pos = tally & 0xFFFF
    n_nonneg = n_pos + (tally >> 16)

    def sweep_keys(fold, merge, init):
        def group(width):
            def body(g, acc):
                return tuple(merge(acc[u], fold(e_scr[pl.ds(chunk_start(width * g + u), KC), :]))
                             for u in range(width))
            return body
        acc = lax.fori_loop(0, npair // 2, group(4), (init,) * 4)
        a, b = lax.fori_loop(2 * (npair // 2), npair, group(2), (merge(acc[0], acc[2]), merge(acc[1], acc[3])))
        return merge(a, b)

    def count_gt(cand):
        return jnp.sum(sweep_keys(lambda x: sum8(x > cand), lambda p, q: p + q, zero8), axis=0, keepdims=True)

    def min_gt(cand):
        fold = lambda x: jnp.min(jnp.where(x > cand, x, jnp.inf).reshape(KC // SUBLANES, SUBLANES, TQ), axis=0)
        return jnp.min(sweep_keys(fold, jnp.minimum, jnp.full((SUBLANES, TQ), jnp.inf, F32)), axis=0, keepdims=True)

    def rewrite(rows, thr):
        def body(c, carry):
            x = e_scr[pl.ds(chunk_start(c), KC), :]
            rank = (2 * spow - (c * KC + s_off)).astype(F32)
            new = jnp.where(x > thr, jnp.inf, jnp.where(x == thr, rank, NEG_INF))
            e_scr[pl.ds(chunk_start(c), KC), :] = jnp.where(rows, new, x)
            return carry
        lax.fori_loop(0, nch, body, 0)

    def halve(_, c):
        lo, hi, act, tie, n_lo = c
        mid = lo + (hi - lo) * 0.5
        inside = (mid > lo) & (mid < hi)
        cnt = count_gt(mid)
        live = act > 0.0
        run = live & inside
        up = run & (cnt >= topk)
        lo = jnp.where(up, mid, lo)
        n_lo = jnp.where(up, cnt, n_lo)
        hi = jnp.where(run & (cnt < topk), mid, hi)
        collapsed = live & jnp.logical_not(inside)
        tie = jnp.where(collapsed, 1.0, tie)
        act = jnp.where(collapsed | (run & (cnt == topk)), 0.0, act)
        return lo, hi, act, tie, n_lo

    def excess(state):
        return jnp.max(jnp.where(state[2] > 0.0, state[4] - topk, 0).astype(F32))

    def bisect(lo, hi, act, warmup):
        state = (lo, hi, act, jnp.zeros_like(act), jnp.full((1, TQ), 1 << 20, I32))
        state = lax.fori_loop(0, warmup, halve, state)

        def one_more(c):
            state = halve(0, c[1])
            return c[0] + 1, state, excess(state)
        near = lax.while_loop(lambda c: jnp.logical_and(c[0] < BISECT_CAP, c[2] > BISECT_DROP), one_more,
                              (jnp.int32(warmup), state, excess(state)))
        lo, hi, act, tie, n_lo = near[1]
        cur, left = lo, jnp.where(act > 0.0, n_lo - topk, 0)
        for _ in range(BISECT_DROP):
            cur = jnp.where(left > 0, min_gt(cur), cur)
            left = left - 1
        done = (act > 0.0) & (count_gt(cur) == topk)
        lo = jnp.where(done, cur, lo)
        act = jnp.where(done, 0.0, act)

        def body(c):
            state = c[1]
            for _ in range(BISECT_STEPS_PER_CHECK):
                state = halve(0, state)
            return c[0] + BISECT_STEPS_PER_CHECK, state, jnp.max(state[2])
        out = lax.while_loop(lambda c: jnp.logical_and(c[0] < BISECT_CAP, c[2] > 0.0), body,
                             (near[0], (lo, hi, act, tie, n_lo), jnp.max(act)))
        return out[1][0], out[1][1], out[1][3]

    fmax = float(jnp.finfo(F32).max)
    gm = gmax_scr[...]
    g_lo = jnp.min(gm, axis=0, keepdims=True)
    g_hi = jnp.max(gm, axis=0, keepdims=True)
    n_vis = qi * TQ + lax.broadcasted_iota(I32, (1, TQ), 1) + 1
    few = n_vis <= topk
    zero_tie = (n_pos < topk) & (n_nonneg >= topk) & jnp.logical_not(few)

    @pl.when(any_row(zero_tie))
    def _():
        rewrite(zero_tie, jnp.zeros((1, TQ), F32))

    rank_lo, rank_hi = float(spow), float(2 * spow + 1)
    below = jnp.maximum(g_lo, -fmax)
    below = below - jnp.abs(below) * (2.0 ** -10) - 1e-30
    lo0 = jnp.where(zero_tie, rank_lo, jnp.maximum(below, -fmax))
    hi0 = jnp.where(zero_tie, rank_hi, g_hi)
    lo, hi, tie = bisect(lo0, hi0, jnp.where(few, 0.0, 1.0), BISECT_WARMUP)
    thr_scr[0:1, :] = jnp.where(few, NEG_INF, lo)

    @pl.when(jnp.max(tie) > 0.0)
    def _():
        rows = tie > 0.0
        rewrite(rows, hi)
        lo2, _, _ = bisect(jnp.full((1, TQ), rank_lo, F32), jnp.full((1, TQ), rank_hi, F32), tie, 0)
        thr_scr[0:1, :] = jnp.where(rows, lo2, thr_scr[0:1, :])

    thr = thr_scr[0:1, :]
    qT = qT_ref[0]
    qs = _block_diag_T(qT, H, DH)
    hcols = lambda row, h: row[:, h * TQ:(h + 1) * TQ]

    def chunk_scores(c):
        start = chunk_start(c)
        sel = e_scr[pl.ds(start, KC), :] > thr
        s_all = jnp.dot(k_ref[0, pl.ds(start, KC), :], qs, preferred_element_type=F32)
        return sel, s_all

    def max_sweep():
        def max_pair(g, m):
            for u in range(2):
                sel, s_all = chunk_scores(2 * g + u)
                m = tuple(jnp.maximum(m[h], jnp.max(jnp.where(sel, hcols(s_all, h), NEG_INF), axis=0, keepdims=True))
                          for h in range(H))
            return m
        m = lax.fori_loop(0, npair, max_pair, (jnp.full((1, TQ), NEG_INF, F32),) * H)
        return [jnp.where(mh == NEG_INF, 0.0, mh) for mh in m]

    def sum_sweep(m):
        acc_scr[...] = jnp.zeros_like(acc_scr)

        def sum_group(width):
            def body(g, l):
                for u in range(width):
                    s_scr[u] = jnp.dot(k_ref[0, pl.ds(chunk_start(width * g + u), KC), :], qs,
                                       preferred_element_type=F32)
                for u in range(width):
                    start = chunk_start(width * g + u)
                    sel = e_scr[pl.ds(start, KC), :] > thr
                    vT_chunk = vT_ref[0, :, pl.ds(start, KC)]
                    l = tuple(_accumulate(h, jnp.exp2(s_scr[u, :, h * TQ:(h + 1) * TQ] + jnp.where(sel, -m[h], NEG_INF)),
                                          vT_chunk, l[h], acc_scr, DH) for h in range(H))
                return l
            return body
        nwide = (2 * npair) // ATT_STAGE
        l = lax.fori_loop(0, nwide, sum_group(ATT_STAGE), (jnp.zeros((1, TQ), F32),) * H)
        return lax.fori_loop((ATT_STAGE // 2) * nwide, npair, sum_group(2), l)

    chunk_n = lax.broadcasted_iota(I32, kn_scr.shape, 0)
    k2 = jnp.max(jnp.where(chunk_n < 2 * npair, kn_scr[...], 0.0), axis=0, keepdims=True)
    l = sum_sweep(_norm_bound(qT.astype(F32), k2, H, DH, TQ))
    _finish_attention(l, acc_scr, y_ref, H, DH)

    @pl.when(_underflowed(l))
    def _():
        _finish_attention(sum_sweep(max_sweep()), acc_scr, y_ref, H, DH)


def _dsa(qT, k, vT, iqT, ik, smT, B, S):
    TQ = min(ATT_TQ, S)
    topk = min(DSA_TOPK_MAX, S // 4)
    assert S % (2 * TQ) == 0 and topk <= TQ
    spow = 1 << max((S - 1).bit_length(), 1)
    k = k.reshape(B, S, DSA_W)
    ik = ik.reshape(B, S, LANES)
    qblk = lambda n: pl.BlockSpec((1, n, TQ), lambda b, i: (b, 0, i))
    return pl.pallas_call(
        functools.partial(_dsa_kernel, TQ=TQ, topk=topk, spow=spow, n_chunks=S // TQ),
        grid=(B, S // TQ),
        in_specs=[qblk(DSA_W),
                  pl.BlockSpec((1, S, DSA_W), lambda b, i: (b, 0, 0)),
                  pl.BlockSpec((1, DSA_W, S), lambda b, i: (b, 0, 0)),
                  qblk(IDX_HEADS * IDX_DH),
                  pl.BlockSpec((1, S, LANES), lambda b, i: (b, 0, 0)),
                  qblk(16)],
        out_specs=pl.BlockSpec((1, TQ, DSA_W), lambda b, i: (b, i, 0)),
        out_shape=jax.ShapeDtypeStruct((B, S, DSA_W), BF16),
        scratch_shapes=[pltpu.VMEM((S, TQ), F32),
                        pltpu.VMEM((TQ, TQ), F32),
                        pltpu.VMEM((DSA_W, TQ), F32),
                        pltpu.VMEM((SUBLANES, TQ), F32),
                        pltpu.VMEM((-(-(S // TQ) // SUBLANES) * SUBLANES, DSA_HEADS * TQ), F32),
                        pltpu.VMEM((ATT_STAGE, TQ, DSA_HEADS * TQ), F32)],
        compiler_params=_params("parallel", "arbitrary"),
    )(qT, k, vT, iqT, ik, smT)


def _out_mlp_kernel(yml_ref, ymb_ref, yds_ref, x_ref, wout_ref, gpost_ref, gpre_ref,
                    w1_ref, w2_ref, g2_ref, out_ref, x1_scr, h_scr, acc_scr):
    f = pl.program_id(1)

    @pl.when(f == 0)
    def _():
        slab = x_ref.shape[0] // MLP_SLABS
        for r in range(MLP_SLABS):
            rows = slice(r * slab, (r + 1) * slab)
            mix = jnp.dot(yml_ref[rows, :], wout_ref[0:ML_V_W, :], preferred_element_type=F32)
            mix = mix + jnp.dot(ymb_ref[rows, :], wout_ref[ML_V_W:ML_V_W + MOBA_W, :], preferred_element_type=F32)
            mix = mix + jnp.dot(yds_ref[rows, :], wout_ref[ML_V_W + MOBA_W:, :], preferred_element_type=F32)
            x1 = x_ref[rows, :] + _rms(mix, gpost_ref[...])
            x1_scr[rows, :] = x1
            h_scr[rows, :] = _rms(x1, gpre_ref[...]).astype(BF16)
        acc_scr[...] = jnp.zeros_like(acc_scr)

    u = jnp.maximum(jnp.dot(h_scr[...], w1_ref[...].astype(BF16), preferred_element_type=F32), 0.0)
    acc_scr[...] += jnp.dot((u * u).astype(BF16), w2_ref[...].astype(BF16), preferred_element_type=F32)

    @pl.when(f == pl.num_programs(1) - 1)
    def _():
        out_ref[...] = x1_scr[...] + _rms(acc_scr[...], g2_ref[...])


def _out_mlp(yml, ymb, yds, x2, w_out, g_post, g_pre, w1, w2, g2):
    T, D = x2.shape
    tm = min(MLP_TM, T)
    tf = min(MLP_TF, D_FF)
    row = lambda n: pl.BlockSpec((tm, n), lambda i, f: (i, 0))
    full = lambda a: pl.BlockSpec(a.shape, lambda i, f: (0,) * a.ndim)
    return pl.pallas_call(
        _out_mlp_kernel,
        grid=(T // tm, D_FF // tf),
        in_specs=[row(ML_V_W), row(MOBA_W), row(DSA_W), row(D), full(w_out), full(g_post), full(g_pre),
                  pl.BlockSpec((D, tf), lambda i, f: (0, f)),
                  pl.BlockSpec((tf, D), lambda i, f: (f, 0)),
                  full(g2)],
        out_specs=row(D),
        out_shape=jax.ShapeDtypeStruct((T, D), F32),
        scratch_shapes=[pltpu.VMEM((tm, D), F32), pltpu.VMEM((tm, D), BF16), pltpu.VMEM((tm, D), F32)],
        compiler_params=_params("parallel", "arbitrary"),
    )(yml, ymb, yds, x2, w_out, g_post, g_pre, w1, w2, g2)


def kernel(x, norm_mix_pre, w_in, ml_conv, ml_i_bias, ml_f_bias, ml_head_norm, w_out, norm_mix_post,
           norm_mlp_pre, w_ff1, w_ff2, norm_mlp_post):
    B, S, D = x.shape
    depth = w_in.shape[0]
    x2 = x.reshape(B * S, D)
    gain = lambda g: g.reshape(1, D).astype(F32)
    for l in range(depth):
        wn, wt = _prep_in_weights(w_in[l])
        (qk, mbk, dsk, ixk, mbqT, mbvT, dsqT, dsvT, ixqT, mlvT, mloT, smT) = _in_proj(
            x2, gain(norm_mix_pre[l]), wn, wt, B, S)
        y_ml = _mlstm(qk, mlvT, mloT, smT, ml_conv[l], ml_i_bias[l], ml_f_bias[l], ml_head_norm[l], B, S)
        y_mb = _moba(mbqT, mbk, mbvT, B, S)
        y_ds = _dsa(dsqT, dsk, dsvT, ixqT, ixk, smT, B, S)
        x2 = _out_mlp(y_ml.reshape(B * S, ML_V_W), y_mb.reshape(B * S, MOBA_W), y_ds.reshape(B * S, DSA_W),
                      x2, w_out[l].astype(BF16), gain(norm_mix_post[l]), gain(norm_mlp_pre[l]),
                      w_ff1[l], w_ff2[l], gain(norm_mlp_post[l]))
    return x2.reshape(B, S, D)
```

```python
import functools

import jax
import jax.numpy as jnp
from jax import lax
from jax.experimental import pallas as pl
from jax.experimental.pallas import tpu as pltpu

F32 = jnp.float32
BF16 = jnp.bfloat16
I32 = jnp.int32

D_MODEL = 1024
ML_HEADS, ML_DQK, ML_DV, ML_CONV = 4, 64, 128, 4
MOBA_HEADS, MOBA_DH, MOBA_BLOCK, MOBA_TOPK = 4, 64, 256, 3
DSA_HEADS, DSA_DH, IDX_HEADS, IDX_DH, DSA_TOPK_MAX = 4, 64, 4, 64, 256
D_FF = 4 * D_MODEL
EPS = 1e-6

ML_QK_W = ML_HEADS * ML_DQK
ML_V_W = ML_HEADS * ML_DV
MOBA_W = MOBA_HEADS * MOBA_DH
DSA_W = DSA_HEADS * DSA_DH
IN_SPLITS = (ML_QK_W, ML_QK_W, ML_V_W, ML_V_W, ML_HEADS, ML_HEADS,
             MOBA_W, MOBA_W, MOBA_W,
             DSA_W, DSA_W, DSA_W, IDX_HEADS * IDX_DH, IDX_DH, IDX_HEADS)

LANES = 128
SUBLANES = 8
VMEM_LIMIT = 52 * 1024 * 1024

ML_CHUNK = 512
ATT_TQ = 256
ATT_STAGE = 8
PROJ_TM = 1024
MLP_TM = 1024
MLP_TF = 512
MLP_SLABS = 4

NEG_INF = float("-inf")
BISECT_CAP = 300
BISECT_WARMUP = 12
BISECT_DROP = 2
BISECT_STEPS_PER_CHECK = 2
SOFTMAX_L_MIN = 1e-30
assert MOBA_DH == DSA_DH
ATT_QSCALE = MOBA_DH ** -0.5 * 1.4426950408889634

_NT = (((1,), (1,)), ((), ()))


def _rms(x, g):
    return x * lax.rsqrt(jnp.mean(x * x, axis=-1, keepdims=True) + EPS) * g


def _log_sigmoid(x):
    return jnp.minimum(x, 0.0) - jnp.log1p(jnp.exp(-jnp.abs(x)))


def _params(*sem):
    return pltpu.CompilerParams(dimension_semantics=sem, vmem_limit_bytes=VMEM_LIMIT)


_N_QK, _N_MBK, _N_DSK, _N_IXK = 0, 512, 768, 1024
_N_TOTAL = 1152
_T_MBQ, _T_MBV, _T_DSQ, _T_DSV, _T_IXQ, _T_MLV, _T_MLO, _T_SM = 0, 256, 512, 768, 1024, 1280, 1792, 2304
_T_TOTAL = 2320


def _in_proj_kernel(x_ref, g_ref, wn_ref, wt_ref,
                    qk_ref, mbk_ref, dsk_ref, ixk_ref,
                    mbqT_ref, mbvT_ref, dsqT_ref, dsvT_ref, ixqT_ref, mlvT_ref, mloT_ref, smT_ref):
    h = _rms(x_ref[...], g_ref[...]).astype(BF16)

    def mm(a, n):
        return jnp.dot(h, wn_ref[:, a:a + n], preferred_element_type=F32)

    def mt(a, n):
        return lax.dot_general(wt_ref[a:a + n, :], h, _NT, preferred_element_type=F32)

    qk_ref[...] = mm(_N_QK, 512)
    mbk_ref[...] = mm(_N_MBK, 256).astype(BF16)
    dsk_ref[...] = mm(_N_DSK, 256).astype(BF16)
    ixk_ref[...] = mm(_N_IXK, 128).astype(BF16)
    mbqT_ref[0] = (mt(_T_MBQ, 256) * ATT_QSCALE).astype(BF16)
    mbvT_ref[0] = mt(_T_MBV, 256).astype(BF16)
    dsqT_ref[0] = (mt(_T_DSQ, 256) * ATT_QSCALE).astype(BF16)
    dsvT_ref[0] = mt(_T_DSV, 256).astype(BF16)
    ixqT_ref[0] = mt(_T_IXQ, 256).astype(BF16)
    mlvT_ref[0] = mt(_T_MLV, 512).astype(BF16)
    mloT_ref[0] = mt(_T_MLO, 512)
    smT_ref[0] = mt(_T_SM, 16)


def _prep_in_weights(w_in):
    pts, acc = [], 0
    for n in IN_SPLITS:
        pts.append((acc, acc + n))
        acc += n
    col = lambda i: w_in[:, pts[i][0]:pts[i][1]]
    (ml_q, ml_k, ml_v, ml_o, ml_i, ml_f, mb_q, mb_k, mb_v,
     ds_q, ds_k, ds_v, ix_q, ix_k, ix_w) = [col(i) for i in range(len(IN_SPLITS))]
    d = w_in.shape[0]
    small = jnp.concatenate([ml_i, ml_f, ix_w], axis=1)
    wn = jnp.concatenate([
        ml_q, ml_k, mb_k, ds_k,
        ix_k, jnp.zeros((d, 128 - IDX_DH), w_in.dtype)], axis=1)
    wt = jnp.concatenate([
        mb_q, mb_v, ds_q, ds_v, ix_q, ml_v, ml_o,
        small, jnp.zeros((d, 16 - small.shape[1]), w_in.dtype)], axis=1).T
    assert wn.shape == (d, _N_TOTAL) and wt.shape == (_T_TOTAL, d)
    return wn.astype(BF16), wt.astype(BF16)


def _in_proj(x2, g, wn, wt, B, S):
    T, D = x2.shape
    tm = min(PROJ_TM, S)
    nsb = S // tm
    row = lambda n: pl.BlockSpec((tm, n), lambda i: (i, 0))
    tr = lambda n: pl.BlockSpec((1, n, tm), lambda i: (i // nsb, 0, i % nsb))
    full = lambda a: pl.BlockSpec(a.shape, lambda i: (0,) * a.ndim)
    out_shape = (
        jax.ShapeDtypeStruct((T, 512), F32),
        jax.ShapeDtypeStruct((T, 256), BF16),
        jax.ShapeDtypeStruct((T, 256), BF16),
        jax.ShapeDtypeStruct((T, 128), BF16),
        jax.ShapeDtypeStruct((B, 256, S), BF16),
        jax.ShapeDtypeStruct((B, 256, S), BF16),
        jax.ShapeDtypeStruct((B, 256, S), BF16),
        jax.ShapeDtypeStruct((B, 256, S), BF16),
        jax.ShapeDtypeStruct((B, 256, S), BF16),
        jax.ShapeDtypeStruct((B, 512, S), BF16),
        jax.ShapeDtypeStruct((B, 512, S), F32),
        jax.ShapeDtypeStruct((B, 16, S), F32),
    )
    out_specs = (row(512), row(256), row(256), row(128),
                 tr(256), tr(256), tr(256), tr(256), tr(256), tr(512), tr(512), tr(16))
    return pl.pallas_call(
        _in_proj_kernel,
        grid=(T // tm,),
        in_specs=[row(D), full(g), full(wn), full(wt)],
        out_specs=out_specs,
        out_shape=out_shape,
        compiler_params=_params("parallel"),
    )(x2, g, wn, wt)


def _dot3(a_f32, b_exact):
    b = b_exact.astype(BF16)
    hi = a_f32.astype(BF16)
    r1 = a_f32 - hi.astype(F32)
    mid = r1.astype(BF16)
    lo = (r1 - mid.astype(F32)).astype(BF16)
    return (jnp.dot(hi, b, preferred_element_type=F32) + jnp.dot(mid, b, preferred_element_type=F32)
            + jnp.dot(lo, b, preferred_element_type=F32))


def _mlstm_kernel(qk_ref, vT_ref, oT_ref, smT_ref, conv_ref, bcol_ref, hn_ref,
                  y_ref, xp_scr, st_scr, m_scr, *, L):
    c = pl.program_id(1)
    halo = SUBLANES
    DK, DV = ML_DQK, ML_DV

    @pl.when(c == 0)
    def _():
        xp_scr[0:halo, :] = jnp.zeros((halo, 2 * ML_QK_W), F32)
        st_scr[...] = jnp.zeros_like(st_scr)
        m_scr[...] = jnp.zeros_like(m_scr)

    cur = qk_ref[0]
    xp_scr[halo:halo + L, :] = cur
    base = halo - (ML_CONV - 1)
    acc = conv_ref[0:1, :] * xp_scr[base:base + L, :]
    for j in range(1, ML_CONV):
        acc = acc + conv_ref[j:j + 1, :] * xp_scr[base + j:base + j + L, :]
    xp_scr[0:halo, :] = cur[L - halo:L, :]
    qk = acc * jax.nn.sigmoid(acc)
    qT_all = (qk[:, :ML_QK_W] * (DK ** -0.5)).T.astype(BF16)
    k_all = qk[:, ML_QK_W:].astype(BF16)

    grow = smT_ref[0][0:SUBLANES, :] + bcol_ref[...]
    si = lax.broadcasted_iota(I32, (L, L), 0)
    ji = lax.broadcasted_iota(I32, (L, L), 1)
    causal = si <= ji
    b_row = _dot3(_log_sigmoid(grow), causal)

    for h in range(ML_HEADS):
        qT = qT_all[h * DK:(h + 1) * DK, :]
        kh = k_all[:, h * DK:(h + 1) * DK]
        vT = vT_ref[0, h * DV:(h + 1) * DV, :]
        b_j = b_row[ML_HEADS + h:ML_HEADS + h + 1, :]
        c_row = grow[h:h + 1, :] - b_j
        c_col = jnp.broadcast_to(c_row, (LANES, L)).T
        dlog = jnp.where(causal, b_j + jnp.concatenate([c_col] * (L // LANES), axis=1), NEG_INF)
        m0 = m_scr[h:h + 1, 0:1]
        inter = b_j + m0
        m_t = jnp.maximum(inter, jnp.max(dlog, axis=0, keepdims=True))
        w_inter = jnp.exp(inter - m_t)
        w_intra = jnp.exp(dlog - m_t) * jnp.dot(kh, qT, preferred_element_type=F32)
        st = st_scr[h]
        cq = jnp.dot(st.astype(BF16), qT, preferred_element_type=F32)
        num = w_inter * cq[:DV] + jnp.dot(vT, w_intra.astype(BF16), preferred_element_type=F32)
        den = w_inter * cq[DV:DV + 1] + jnp.sum(w_intra, axis=0, keepdims=True)
        hh = num / jnp.maximum(jnp.abs(den), jnp.exp(-m_t))
        hh = hh * lax.rsqrt(jnp.mean(hh * hh, axis=0, keepdims=True) + EPS)
        rows = slice(h * DV, (h + 1) * DV)
        y_ref[0, :, rows] = (jax.nn.sigmoid(oT_ref[0, rows, :]) * hh * hn_ref[rows, :]).T.astype(y_ref.dtype)

        b_last = b_j[:, L - 1:L]
        a = b_last + c_row
        m_loc = jnp.max(a, axis=1, keepdims=True)
        wa = jnp.exp(a - m_loc)
        lhs = jnp.concatenate([(vT.astype(F32) * wa).astype(BF16),
                               jnp.broadcast_to(wa, (SUBLANES, L)).astype(BF16)], axis=0)
        c_loc = jnp.dot(lhs, kh, preferred_element_type=F32)
        m_new = jnp.maximum(b_last + m0, m_loc)
        s_old = jnp.exp(b_last + m0 - m_new)
        s_loc = jnp.exp(m_loc - m_new)
        st_scr[h] = s_old * st + s_loc * c_loc
        m_scr[h:h + 1, :] = jnp.broadcast_to(m_new, (1, LANES))


def _mlstm(qk, vT, oT, smT, conv_w, i_bias, f_bias, head_norm, B, S):
    L = min(ML_CHUNK, S)
    assert L % LANES == 0
    nc = S // L
    qk = qk.reshape(B, S, 2 * ML_QK_W)
    bcol = jnp.concatenate([i_bias, f_bias]).astype(F32).reshape(2 * ML_HEADS, 1)
    hn = jnp.broadcast_to(head_norm.astype(F32)[:, None], (ML_V_W, L))
    blkT = lambda n: pl.BlockSpec((1, n, L), lambda b, c: (b, 0, c))
    full = lambda a: pl.BlockSpec(a.shape, lambda b, c: (0,) * a.ndim)
    return pl.pallas_call(
        functools.partial(_mlstm_kernel, L=L),
        grid=(B, nc),
        in_specs=[pl.BlockSpec((1, L, 2 * ML_QK_W), lambda b, c: (b, c, 0)),
                  blkT(ML_V_W), blkT(ML_V_W), blkT(16), full(conv_w), full(bcol), full(hn)],
        out_specs=pl.BlockSpec((1, L, ML_V_W), lambda b, c: (b, c, 0)),
        out_shape=jax.ShapeDtypeStruct((B, S, ML_V_W), BF16),
        scratch_shapes=[pltpu.VMEM((L + SUBLANES, 2 * ML_QK_W), F32),
                        pltpu.VMEM((ML_HEADS, ML_DV + SUBLANES, ML_DQK), F32),
                        pltpu.VMEM((SUBLANES, LANES), F32)],
        compiler_params=_params("parallel", "arbitrary"),
    )(qk, vT, oT, smT, conv_w.astype(F32), bcol, hn)


def _block_diag_T(qT, heads, dh):
    rid = lax.broadcasted_iota(I32, qT.shape, 0)
    zero = jnp.zeros_like(qT)
    return jnp.concatenate(
        [jnp.where((rid >= h * dh) & (rid < (h + 1) * dh), qT, zero) for h in range(heads)], axis=1)


def _accumulate(h, p, vT_chunk, l, acc_scr, dh):
    rows = slice(h * dh, (h + 1) * dh)
    acc_scr[rows, :] += jnp.dot(vT_chunk[rows, :], p.astype(BF16), preferred_element_type=F32)
    return l + jnp.sum(p, axis=0, keepdims=True)


def _finish_attention(l, acc_scr, y_ref, heads, dh):
    outs = [acc_scr[h * dh:(h + 1) * dh, :] / l[h] for h in range(heads)]
    y_ref[0] = jnp.concatenate(outs, axis=0).T.astype(y_ref.dtype)


def _max_key_norm2(kb, heads, dh, TQ):
    ch = lax.broadcasted_iota(I32, (heads * dh, LANES), 0) // dh
    hd = lax.broadcasted_iota(I32, (heads * dh, LANES), 1)
    n2 = jnp.dot((kb * kb).astype(BF16), (ch == hd).astype(BF16), preferred_element_type=F32)
    n2 = jnp.max(n2, axis=0, keepdims=True) * (1.0 + 2.0 ** -7)
    return jnp.concatenate([jnp.broadcast_to(n2[:, h:h + 1], (1, TQ)) for h in range(heads)], axis=1)


def _norm_bound(qTf, k2, heads, dh, TQ):
    out = []
    for h in range(heads):
        qh = qTf[h * dh:(h + 1) * dh, :]
        q2 = jnp.sum(qh * qh, axis=0, keepdims=True)
        out.append(jnp.sqrt(q2 * k2[:, h * TQ:(h + 1) * TQ]) * (1.0 + 2.0 ** -8))
    return out


def _underflowed(l):
    small = l[0]
    for lh in l[1:]:
        small = jnp.minimum(small, lh)
    return jnp.logical_not(jnp.min(small) >= SOFTMAX_L_MIN)


def _moba_kernel(qT_ref, k_ref, vT_ref, y_ref, km_scr, kn_scr, acc_scr, s_scr, *, NB, NBP, n_sel):
    H, DH, BS = MOBA_HEADS, MOBA_DH, MOBA_BLOCK
    TQ = BS
    qi = pl.program_id(1)

    @pl.when(qi == 0)
    def _():
        km_scr[...] = jnp.zeros_like(km_scr)
        kn_scr[...] = jnp.zeros_like(kn_scr)

        def body(j, carry):
            kb = k_ref[0, pl.ds(pl.multiple_of(j * BS, BS), BS), :].astype(F32)
            km_scr[pl.ds(j, 1), :] = jnp.mean(kb, axis=0, keepdims=True)
            kn_scr[pl.ds(j, 1), :] = _max_key_norm2(kb, H, DH, TQ)
            return carry
        lax.fori_loop(0, NB, body, 0)

    qT = qT_ref[0]
    qTf = qT.astype(F32)
    km = km_scr[...]
    gate = jnp.concatenate(
        [jnp.dot(km[:, h * DH:(h + 1) * DH], qTf[h * DH:(h + 1) * DH, :], precision=lax.Precision.HIGHEST,
                 preferred_element_type=F32) for h in range(H)], axis=1) * (1.0 / ATT_QSCALE)
    blk = lax.broadcasted_iota(I32, gate.shape, 0)
    g = jnp.where(blk < qi, gate, NEG_INF)
    sels = []
    for _ in range(n_sel):
        mx = jnp.max(g, axis=0, keepdims=True)
        isel = jnp.min(jnp.where(g == mx, blk, NBP), axis=0, keepdims=True)
        sels.append(jnp.where(mx > NEG_INF, isel, -1))
        g = jnp.where(blk == isel, NEG_INF, g)

    qs = _block_diag_T(qT, H, DH)

    def hit_row(j):
        hit = sels[0] == j
        for s in sels[1:]:
            hit = hit | (s == j)
        return hit

    npair = (qi + 1) // 2
    blk_start = lambda j: pl.multiple_of(j * BS, BS)
    blk_scores = lambda j: jnp.dot(k_ref[0, pl.ds(blk_start(j), BS), :], qs, preferred_element_type=F32)
    hcols = lambda row, h: row[:, h * TQ:(h + 1) * TQ]
    causal = lax.broadcasted_iota(I32, (BS, TQ), 0) <= lax.broadcasted_iota(I32, (BS, TQ), 1)

    def diag_scores():
        s_all = blk_scores(qi)
        return [jnp.where(causal, hcols(s_all, h), NEG_INF) for h in range(H)]

    def max_sweep():
        def max_pair(g, m):
            for u in range(2):
                j = 2 * g + u
                m = jnp.maximum(m, jnp.where(hit_row(j), jnp.max(blk_scores(j), axis=0, keepdims=True), NEG_INF))
            return m
        m_all = lax.fori_loop(0, npair, max_pair, jnp.full((1, H * TQ), NEG_INF, F32))
        s_diag = diag_scores()
        return [jnp.maximum(hcols(m_all, h), jnp.max(s_diag[h], axis=0, keepdims=True)) for h in range(H)]

    def sum_sweep(m):
        acc_scr[...] = jnp.zeros_like(acc_scr)

        def sum_group(width):
            def body(g, l):
                for u in range(width):
                    s_scr[u] = blk_scores(width * g + u)
                for u in range(width):
                    j = width * g + u
                    hit = hit_row(j)
                    vT_blk = vT_ref[0, :, pl.ds(blk_start(j), BS)]
                    l = tuple(_accumulate(h, jnp.exp2(s_scr[u, :, h * TQ:(h + 1) * TQ]
                                                      - jnp.where(hcols(hit, h), m[h], jnp.inf)),
                                          vT_blk, l[h], acc_scr, DH) for h in range(H))
                return l
            return body
        nwide = qi // ATT_STAGE
        l = lax.fori_loop(0, nwide, sum_group(ATT_STAGE), (jnp.zeros((1, TQ), F32),) * H)
        l = lax.fori_loop((ATT_STAGE // 2) * nwide, npair, sum_group(2), l)
        s_diag = diag_scores()
        vT_diag = vT_ref[0, :, pl.ds(blk_start(qi), BS)]
        return [_accumulate(h, jnp.exp2(s_diag[h] - m[h]), vT_diag, l[h], acc_scr, DH) for h in range(H)]

    blk_n = lax.broadcasted_iota(I32, kn_scr.shape, 0)
    k2 = jnp.max(jnp.where(blk_n <= qi, kn_scr[...], 0.0), axis=0, keepdims=True)
    l = sum_sweep(_norm_bound(qTf, k2, H, DH, TQ))
    _finish_attention(l, acc_scr, y_ref, H, DH)

    @pl.when(_underflowed(l))
    def _():
        _finish_attention(sum_sweep(max_sweep()), acc_scr, y_ref, H, DH)


def _moba(qT, k, vT, B, S):
    BS = MOBA_BLOCK
    assert S % BS == 0
    NB = S // BS
    NBP = -(-NB // SUBLANES) * SUBLANES
    n_sel = max(min(MOBA_TOPK, NB - 1), 1)
    k = k.reshape(B, S, MOBA_W)
    return pl.pallas_call(
        functools.partial(_moba_kernel, NB=NB, NBP=NBP, n_sel=n_sel),
        grid=(B, NB),
        in_specs=[pl.BlockSpec((1, MOBA_W, BS), lambda b, i: (b, 0, i)),
                  pl.BlockSpec((1, S, MOBA_W), lambda b, i: (b, 0, 0)),
                  pl.BlockSpec((1, MOBA_W, S), lambda b, i: (b, 0, 0))],
        out_specs=pl.BlockSpec((1, BS, MOBA_W), lambda b, i: (b, i, 0)),
        out_shape=jax.ShapeDtypeStruct((B, S, MOBA_W), BF16),
        scratch_shapes=[pltpu.VMEM((NBP, MOBA_W), F32),
                        pltpu.VMEM((NBP, MOBA_HEADS * BS), F32),
                        pltpu.VMEM((MOBA_W, BS), F32),
                        pltpu.VMEM((ATT_STAGE, BS, MOBA_HEADS * BS), F32)],
        compiler_params=_params("parallel", "arbitrary"),
    )(qT, k, vT)


def _dsa_kernel(qT_ref, k_ref, vT_ref, iqT_ref, ik_ref, smT_ref, y_ref,
                e_scr, gmax_scr, acc_scr, thr_scr, kn_scr, s_scr, *, TQ, topk, spow, n_chunks):
    H, DH = DSA_HEADS, DSA_DH
    KC = TQ
    qi = pl.program_id(1)
    nch = qi + 1
    chunk_start = lambda c: pl.multiple_of(c * KC, KC)

    def chunk_loops(n, wide, make_body, carry, regroup=lambda c, w: c):
        carry = lax.fori_loop(0, n // wide, make_body(wide), carry)
        carry = lax.fori_loop((wide // 2) * (n // wide), n // 2, make_body(2), regroup(carry, 2))
        return lax.fori_loop(2 * (n // 2), n, make_body(1), regroup(carry, 1))

    @pl.when(qi == 0)
    def _():
        kn_scr[...] = jnp.zeros_like(kn_scr)

        def body(c, carry):
            kn_scr[pl.ds(c, 1), :] = _max_key_norm2(k_ref[0, pl.ds(chunk_start(c), KC), :].astype(F32), H, DH, TQ)
            return carry
        lax.fori_loop(0, n_chunks, body, 0)
    sum8 = lambda b: jnp.sum(b.astype(I32).reshape(KC // SUBLANES, SUBLANES, TQ), axis=0)
    any_row = lambda r: jnp.max(r.astype(I32)) > 0

    iqT = iqT_ref[0]
    iq_cat = jnp.concatenate([iqT[h * IDX_DH:(h + 1) * IDX_DH, :] for h in range(IDX_HEADS)], axis=1)
    iw = smT_ref[0][2 * ML_HEADS:2 * ML_HEADS + IDX_HEADS, :] * (IDX_HEADS ** -0.5 * IDX_DH ** -0.5)
    t_pos = qi * TQ + lax.broadcasted_iota(I32, (KC, TQ), 1)
    s_off = lax.broadcasted_iota(I32, (KC, TQ), 0)
    gmax_scr[...] = jnp.full((KC, TQ), NEG_INF, F32)

    def score_group(width, masked=False):
        def body(g, tally):
            for u in range(width):
                ikc = ik_ref[0, pl.ds(chunk_start(width * g + u), KC), :][:, :IDX_DH]
                s_scr[u] = jnp.dot(ikc, iq_cat, preferred_element_type=F32)
            for u in range(width):
                c = width * g + u
                sc = iw[0:1, :] * jnp.maximum(s_scr[u, :, 0:TQ], 0.0)
                for h in range(1, IDX_HEADS):
                    sc = sc + iw[h:h + 1, :] * jnp.maximum(s_scr[u, :, h * TQ:(h + 1) * TQ], 0.0)
                if masked:
                    sc = jnp.where(c * KC + s_off <= t_pos, sc, NEG_INF)
                e_scr[pl.ds(chunk_start(c), KC), :] = sc
                gmax_scr[...] = jnp.maximum(gmax_scr[...], sc)
                tally = tally + jnp.sum(jnp.where(sc > 0.0, 1, jnp.where(sc == 0.0, 1 << 16, 0))
                                        .reshape(KC // SUBLANES, SUBLANES, TQ), axis=0)
            return tally
        return body
    assert n_chunks * KC < (1 << 16)
    zero8 = jnp.zeros((SUBLANES, TQ), I32)
    tally = chunk_loops(qi, 4, score_group, zero8)
    tally = jnp.sum(score_group(1, True)(qi, tally), axis=0, keepdims=True)
    n_pos = tally & 0xFFFF
    n_nonneg = n_pos + (tally >> 16)

    def sweep_keys(fold, merge, init):
        def group(width):
            def body(g, acc):
                return tuple(merge(acc[u], fold(e_scr[pl.ds(chunk_start(width * g + u), KC), :]))
                             for u in range(width))
            return body

        def regroup(acc, width):
            half = len(acc) // 2
            acc = tuple(merge(acc[u], acc[u + half]) for u in range(half))
            return acc if half == width else regroup(acc, width)
        return chunk_loops(nch, 4, group, (init,) * 4, regroup)[0]

    def count_gt(cand):
        return jnp.sum(sweep_keys(lambda x: sum8(x > cand), lambda p, q: p + q, zero8), axis=0, keepdims=True)

    def min_gt(cand):
        fold = lambda x: jnp.min(jnp.where(x > cand, x, jnp.inf).reshape(KC // SUBLANES, SUBLANES, TQ), axis=0)
        return jnp.min(sweep_keys(fold, jnp.minimum, jnp.full((SUBLANES, TQ), jnp.inf, F32)), axis=0, keepdims=True)

    def rewrite(rows, thr):
        def body(c, carry):
            x = e_scr[pl.ds(chunk_start(c), KC), :]
            rank = (2 * spow - (c * KC + s_off)).astype(F32)
            new = jnp.where(x > thr, jnp.inf, jnp.where(x == thr, rank, NEG_INF))
            e_scr[pl.ds(chunk_start(c), KC), :] = jnp.where(rows, new, x)
            return carry
        lax.fori_loop(0, nch, body, 0)

    def halve(_, c):
        lo, hi, act, tie, n_lo = c
        mid = lo + (hi - lo) * 0.5
        inside = (mid > lo) & (mid < hi)
        cnt = count_gt(mid)
        live = act > 0.0
        run = live & inside
        up = run & (cnt >= topk)
        lo = jnp.where(up, mid, lo)
        n_lo = jnp.where(up, cnt, n_lo)
        hi = jnp.where(run & (cnt < topk), mid, hi)
        collapsed = live & jnp.logical_not(inside)
        tie = jnp.where(collapsed, 1.0, tie)
        act = jnp.where(collapsed | (run & (cnt == topk)), 0.0, act)
        return lo, hi, act, tie, n_lo

    def excess(state):
        return jnp.max(jnp.where(state[2] > 0.0, state[4] - topk, 0).astype(F32))

    def bisect(lo, hi, act, warmup):
        state = (lo, hi, act, jnp.zeros_like(act), jnp.full((1, TQ), 1 << 20, I32))
        state = lax.fori_loop(0, warmup, halve, state)

        def one_more(c):
            state = halve(0, c[1])
            return c[0] + 1, state, excess(state)
        near = lax.while_loop(lambda c: jnp.logical_and(c[0] < BISECT_CAP, c[2] > BISECT_DROP), one_more,
                              (jnp.int32(warmup), state, excess(state)))
        lo, hi, act, tie, n_lo = near[1]
        cur, left = lo, jnp.where(act > 0.0, n_lo - topk, 0)
        for _ in range(BISECT_DROP):
            cur = jnp.where(left > 0, min_gt(cur), cur)
            left = left - 1
        done = (act > 0.0) & (count_gt(cur) == topk)
        lo = jnp.where(done, cur, lo)
        act = jnp.where(done, 0.0, act)

        def body(c):
            state = c[1]
            for _ in range(BISECT_STEPS_PER_CHECK):
                state = halve(0, state)
            return c[0] + BISECT_STEPS_PER_CHECK, state, jnp.max(state[2])
        out = lax.while_loop(lambda c: jnp.logical_and(c[0] < BISECT_CAP, c[2] > 0.0), body,
                             (near[0], (lo, hi, act, tie, n_lo), jnp.max(act)))
        return out[1][0], out[1][1], out[1][3]

    fmax = float(jnp.finfo(F32).max)
    gm = gmax_scr[...]
    g_lo = jnp.min(gm, axis=0, keepdims=True)
    g_hi = jnp.max(gm, axis=0, keepdims=True)
    n_vis = qi * TQ + lax.broadcasted_iota(I32, (1, TQ), 1) + 1
    few = n_vis <= topk
    zero_tie = (n_pos < topk) & (n_nonneg >= topk) & jnp.logical_not(few)

    @pl.when(any_row(zero_tie))
    def _():
        rewrite(zero_tie, jnp.zeros((1, TQ), F32))

    rank_lo, rank_hi = float(spow), float(2 * spow + 1)
    below = jnp.maximum(g_lo, -fmax)
    below = below - jnp.abs(below) * (2.0 ** -10) - 1e-30
    lo0 = jnp.where(zero_tie, rank_lo, jnp.maximum(below, -fmax))
    hi0 = jnp.where(zero_tie, rank_hi, g_hi)
    lo, hi, tie = bisect(lo0, hi0, jnp.where(few, 0.0, 1.0), BISECT_WARMUP)
    thr_scr[0:1, :] = jnp.where(few, NEG_INF, lo)

    @pl.when(jnp.max(tie) > 0.0)
    def _():
        rows = tie > 0.0
        rewrite(rows, hi)
        lo2, _, _ = bisect(jnp.full((1, TQ), rank_lo, F32), jnp.full((1, TQ), rank_hi, F32), tie, 0)
        thr_scr[0:1, :] = jnp.where(rows, lo2, thr_scr[0:1, :])

    thr = thr_scr[0:1, :]
    qT = qT_ref[0]
    qs = _block_diag_T(qT, H, DH)
    hcols = lambda row, h: row[:, h * TQ:(h + 1) * TQ]

    def chunk_scores(c):
        start = chunk_start(c)
        sel = e_scr[pl.ds(start, KC), :] > thr
        s_all = jnp.dot(k_ref[0, pl.ds(start, KC), :], qs, preferred_element_type=F32)
        return sel, s_all

    def max_sweep():
        def max_chunk(c, m):
            sel, s_all = chunk_scores(c)
            return tuple(jnp.maximum(m[h], jnp.max(jnp.where(sel, hcols(s_all, h), NEG_INF), axis=0, keepdims=True))
                         for h in range(H))
        m = lax.fori_loop(0, nch, max_chunk, (jnp.full((1, TQ), NEG_INF, F32),) * H)
        return [jnp.where(mh == NEG_INF, 0.0, mh) for mh in m]

    def sum_sweep(m):
        acc_scr[...] = jnp.zeros_like(acc_scr)

        def sum_group(width):
            def body(g, l):
                for u in range(width):
                    s_scr[u] = jnp.dot(k_ref[0, pl.ds(chunk_start(width * g + u), KC), :], qs,
                                       preferred_element_type=F32)
                for u in range(width):
                    start = chunk_start(width * g + u)
                    sel = e_scr[pl.ds(start, KC), :] > thr
                    vT_chunk = vT_ref[0, :, pl.ds(start, KC)]
                    l = tuple(_accumulate(h, jnp.exp2(s_scr[u, :, h * TQ:(h + 1) * TQ] + jnp.where(sel, -m[h], NEG_INF)),
                                          vT_chunk, l[h], acc_scr, DH) for h in range(H))
                return l
            return body
        return chunk_loops(nch, ATT_STAGE, sum_group, (jnp.zeros((1, TQ), F32),) * H)

    chunk_n = lax.broadcasted_iota(I32, kn_scr.shape, 0)
    k2 = jnp.max(jnp.where(chunk_n < nch, kn_scr[...], 0.0), axis=0, keepdims=True)
    l = sum_sweep(_norm_bound(qT.astype(F32), k2, H, DH, TQ))
    _finish_attention(l, acc_scr, y_ref, H, DH)

    @pl.when(_underflowed(l))
    def _():
        _finish_attention(sum_sweep(max_sweep()), acc_scr, y_ref, H, DH)


def _dsa(qT, k, vT, iqT, ik, smT, B, S):
    TQ = min(ATT_TQ, S)
    topk = min(DSA_TOPK_MAX, S // 4)
    assert S % TQ == 0 and topk <= TQ
    spow = 1 << max((S - 1).bit_length(), 1)
    k = k.reshape(B, S, DSA_W)
    ik = ik.reshape(B, S, LANES)
    qblk = lambda n: pl.BlockSpec((1, n, TQ), lambda b, i: (b, 0, i))
    return pl.pallas_call(
        functools.partial(_dsa_kernel, TQ=TQ, topk=topk, spow=spow, n_chunks=S // TQ),
        grid=(B, S // TQ),
        in_specs=[qblk(DSA_W),
                  pl.BlockSpec((1, S, DSA_W), lambda b, i: (b, 0, 0)),
                  pl.BlockSpec((1, DSA_W, S), lambda b, i: (b, 0, 0)),
                  qblk(IDX_HEADS * IDX_DH),
                  pl.BlockSpec((1, S, LANES), lambda b, i: (b, 0, 0)),
                  qblk(16)],
        out_specs=pl.BlockSpec((1, TQ, DSA_W), lambda b, i: (b, i, 0)),
        out_shape=jax.ShapeDtypeStruct((B, S, DSA_W), BF16),
        scratch_shapes=[pltpu.VMEM((S, TQ), F32),
                        pltpu.VMEM((TQ, TQ), F32),
                        pltpu.VMEM((DSA_W, TQ), F32),
                        pltpu.VMEM((SUBLANES, TQ), F32),
                        pltpu.VMEM((-(-(S // TQ) // SUBLANES) * SUBLANES, DSA_HEADS * TQ), F32),
                        pltpu.VMEM((ATT_STAGE, TQ, DSA_HEADS * TQ), F32)],
        compiler_params=_params("parallel", "arbitrary"),
    )(qT, k, vT, iqT, ik, smT)


def _out_mlp_kernel(yml_ref, ymb_ref, yds_ref, x_ref, wout_ref, gpost_ref, gpre_ref,
                    w1_ref, w2_ref, g2_ref, out_ref, x1_scr, h_scr, acc_scr):
    f = pl.program_id(1)

    @pl.when(f == 0)
    def _():
        slab = x_ref.shape[0] // MLP_SLABS
        for r in range(MLP_SLABS):
            rows = slice(r * slab, (r + 1) * slab)
            mix = jnp.dot(yml_ref[rows, :], wout_ref[0:ML_V_W, :], preferred_element_type=F32)
            mix = mix + jnp.dot(ymb_ref[rows, :], wout_ref[ML_V_W:ML_V_W + MOBA_W, :], preferred_element_type=F32)
            mix = mix + jnp.dot(yds_ref[rows, :], wout_ref[ML_V_W + MOBA_W:, :], preferred_element_type=F32)
            x1 = x_ref[rows, :] + _rms(mix, gpost_ref[...])
            x1_scr[rows, :] = x1
            h_scr[rows, :] = _rms(x1, gpre_ref[...]).astype(BF16)
        acc_scr[...] = jnp.zeros_like(acc_scr)

    u = jnp.maximum(jnp.dot(h_scr[...], w1_ref[...], preferred_element_type=F32), 0.0)
    acc_scr[...] += jnp.dot((u * u).astype(BF16), w2_ref[...], preferred_element_type=F32)

    @pl.when(f == pl.num_programs(1) - 1)
    def _():
        out_ref[...] = x1_scr[...] + _rms(acc_scr[...], g2_ref[...])


def _out_mlp(yml, ymb, yds, x2, w_out, g_post, g_pre, w1, w2, g2):
    T, D = x2.shape
    tm = min(MLP_TM, T)
    tf = min(MLP_TF, D_FF)
    row = lambda n: pl.BlockSpec((tm, n), lambda i, f: (i, 0))
    full = lambda a: pl.BlockSpec(a.shape, lambda i, f: (0,) * a.ndim)
    return pl.pallas_call(
        _out_mlp_kernel,
        grid=(T // tm, D_FF // tf),
        in_specs=[row(ML_V_W), row(MOBA_W), row(DSA_W), row(D), full(w_out), full(g_post), full(g_pre),
                  pl.BlockSpec((D, tf), lambda i, f: (0, f)),
                  pl.BlockSpec((tf, D), lambda i, f: (f, 0)),
                  full(g2)],
        out_specs=row(D),
        out_shape=jax.ShapeDtypeStruct((T, D), F32),
        scratch_shapes=[pltpu.VMEM((tm, D), F32), pltpu.VMEM((tm, D), BF16), pltpu.VMEM((tm, D), F32)],
        compiler_params=_params("parallel", "arbitrary"),
    )(yml, ymb, yds, x2, w_out, g_post, g_pre, w1, w2, g2)


def kernel(x, norm_mix_pre, w_in, ml_conv, ml_i_bias, ml_f_bias, ml_head_norm, w_out, norm_mix_post,
           norm_mlp_pre, w_ff1, w_ff2, norm_mlp_post):
    B, S, D = x.shape
    depth = w_in.shape[0]
    x2 = x.reshape(B * S, D)
    gain = lambda g: g.reshape(1, D).astype(F32)
    for l in range(depth):
        wn, wt = _prep_in_weights(w_in[l])
        (qk, mbk, dsk, ixk, mbqT, mbvT, dsqT, dsvT, ixqT, mlvT, mloT, smT) = _in_proj(
            x2, gain(norm_mix_pre[l]), wn, wt, B, S)
        y_ml = _mlstm(qk, mlvT, mloT, smT, ml_conv[l], ml_i_bias[l], ml_f_bias[l], ml_head_norm[l], B, S)
        y_mb = _moba(mbqT, mbk, mbvT, B, S)
        y_ds = _dsa(dsqT, dsk, dsvT, ixqT, ixk, smT, B, S)
        x2 = _out_mlp(y_ml.reshape(B * S, ML_V_W), y_mb.reshape(B * S, MOBA_W), y_ds.reshape(B * S, DSA_W),
                      x2, w_out[l].astype(BF16), gain(norm_mix_post[l]), gain(norm_mlp_pre[l]),
                      w_ff1[l].astype(BF16), w_ff2[l].astype(BF16), gain(norm_mlp_post[l]))
    return x2.reshape(B, S, D)
```

```python
import functools

import jax
import jax.numpy as jnp
from jax import lax
from jax.experimental import pallas as pl
from jax.experimental.pallas import tpu as pltpu

F32 = jnp.float32
BF16 = jnp.bfloat16
I32 = jnp.int32

D_MODEL = 1024
ML_HEADS, ML_DQK, ML_DV, ML_CONV = 4, 64, 128, 4
MOBA_HEADS, MOBA_DH, MOBA_BLOCK, MOBA_TOPK = 4, 64, 256, 3
DSA_HEADS, DSA_DH, IDX_HEADS, IDX_DH, DSA_TOPK_MAX = 4, 64, 4, 64, 256
D_FF = 4 * D_MODEL
EPS = 1e-6

ML_QK_W = ML_HEADS * ML_DQK
ML_V_W = ML_HEADS * ML_DV
MOBA_W = MOBA_HEADS * MOBA_DH
DSA_W = DSA_HEADS * DSA_DH
IN_SPLITS = (ML_QK_W, ML_QK_W, ML_V_W, ML_V_W, ML_HEADS, ML_HEADS,
             MOBA_W, MOBA_W, MOBA_W,
             DSA_W, DSA_W, DSA_W, IDX_HEADS * IDX_DH, IDX_DH, IDX_HEADS)

LANES = 128
SUBLANES = 8
VMEM_LIMIT = 52 * 1024 * 1024

ML_CHUNK = 512
ATT_TQ = 256
ATT_STAGE = 8
PROJ_TM = 1024
MLP_TM = 1024
MLP_TF = 512
MLP_SLABS = 4

NEG_INF = float("-inf")
BISECT_CAP = 300
BISECT_WARMUP = 12
BISECT_DROP = 2
BISECT_STEPS_PER_CHECK = 2
SOFTMAX_L_MIN = 1e-30
assert MOBA_DH == DSA_DH
ATT_QSCALE = MOBA_DH ** -0.5 * 1.4426950408889634

_NT = (((1,), (1,)), ((), ()))


def _rms(x, g):
    return x * lax.rsqrt(jnp.mean(x * x, axis=-1, keepdims=True) + EPS) * g


def _log_sigmoid(x):
    return jnp.minimum(x, 0.0) - jnp.log1p(jnp.exp(-jnp.abs(x)))


def _params(*sem):
    return pltpu.CompilerParams(dimension_semantics=sem, vmem_limit_bytes=VMEM_LIMIT)


_N_QK, _N_MBK, _N_DSK, _N_IXK = 0, 512, 768, 1024
_N_TOTAL = 1152
_T_MBQ, _T_MBV, _T_DSQ, _T_DSV, _T_IXQ, _T_MLV, _T_MLO, _T_SM = 0, 256, 512, 768, 1024, 1280, 1792, 2304
_T_TOTAL = 2320


def _in_proj_kernel(x_ref, g_ref, wn_ref, wt_ref,
                    qk_ref, mbk_ref, dsk_ref, ixk_ref,
                    mbqT_ref, mbvT_ref, dsqT_ref, dsvT_ref, ixqT_ref, mlvT_ref, mloT_ref, smT_ref):
    h = _rms(x_ref[...], g_ref[...]).astype(BF16)

    def mm(a, n):
        return jnp.dot(h, wn_ref[:, a:a + n], preferred_element_type=F32)

    def mt(a, n):
        return lax.dot_general(wt_ref[a:a + n, :], h, _NT, preferred_element_type=F32)

    qk_ref[...] = mm(_N_QK, 512)
    mbk_ref[...] = mm(_N_MBK, 256).astype(BF16)
    dsk_ref[...] = mm(_N_DSK, 256).astype(BF16)
    ixk_ref[...] = mm(_N_IXK, 128).astype(BF16)
    mbqT_ref[0] = (mt(_T_MBQ, 256) * ATT_QSCALE).astype(BF16)
    mbvT_ref[0] = mt(_T_MBV, 256).astype(BF16)
    dsqT_ref[0] = (mt(_T_DSQ, 256) * ATT_QSCALE).astype(BF16)
    dsvT_ref[0] = mt(_T_DSV, 256).astype(BF16)
    ixqT_ref[0] = mt(_T_IXQ, 256).astype(BF16)
    mlvT_ref[0] = mt(_T_MLV, 512).astype(BF16)
    mloT_ref[0] = mt(_T_MLO, 512)
    smT_ref[0] = mt(_T_SM, 16)


def _prep_in_weights(w_in):
    pts, acc = [], 0
    for n in IN_SPLITS:
        pts.append((acc, acc + n))
        acc += n
    col = lambda i: w_in[:, pts[i][0]:pts[i][1]]
    (ml_q, ml_k, ml_v, ml_o, ml_i, ml_f, mb_q, mb_k, mb_v,
     ds_q, ds_k, ds_v, ix_q, ix_k, ix_w) = [col(i) for i in range(len(IN_SPLITS))]
    d = w_in.shape[0]
    small = jnp.concatenate([ml_i, ml_f, ix_w], axis=1)
    wn = jnp.concatenate([
        ml_q, ml_k, mb_k, ds_k,
        ix_k, jnp.zeros((d, 128 - IDX_DH), w_in.dtype)], axis=1)
    wt = jnp.concatenate([
        mb_q, mb_v, ds_q, ds_v, ix_q, ml_v, ml_o,
        small, jnp.zeros((d, 16 - small.shape[1]), w_in.dtype)], axis=1).T
    assert wn.shape == (d, _N_TOTAL) and wt.shape == (_T_TOTAL, d)
    return wn.astype(BF16), wt.astype(BF16)


def _in_proj(x2, g, wn, wt, B, S):
    T, D = x2.shape
    tm = min(PROJ_TM, S)
    nsb = S // tm
    row = lambda n: pl.BlockSpec((tm, n), lambda i: (i, 0))
    tr = lambda n: pl.BlockSpec((1, n, tm), lambda i: (i // nsb, 0, i % nsb))
    full = lambda a: pl.BlockSpec(a.shape, lambda i: (0,) * a.ndim)
    out_shape = (
        jax.ShapeDtypeStruct((T, 512), F32),
        jax.ShapeDtypeStruct((T, 256), BF16),
        jax.ShapeDtypeStruct((T, 256), BF16),
        jax.ShapeDtypeStruct((T, 128), BF16),
        jax.ShapeDtypeStruct((B, 256, S), BF16),
        jax.ShapeDtypeStruct((B, 256, S), BF16),
        jax.ShapeDtypeStruct((B, 256, S), BF16),
        jax.ShapeDtypeStruct((B, 256, S), BF16),
        jax.ShapeDtypeStruct((B, 256, S), BF16),
        jax.ShapeDtypeStruct((B, 512, S), BF16),
        jax.ShapeDtypeStruct((B, 512, S), F32),
        jax.ShapeDtypeStruct((B, 16, S), F32),
    )
    out_specs = (row(512), row(256), row(256), row(128),
                 tr(256), tr(256), tr(256), tr(256), tr(256), tr(512), tr(512), tr(16))
    return pl.pallas_call(
        _in_proj_kernel,
        grid=(T // tm,),
        in_specs=[row(D), full(g), full(wn), full(wt)],
        out_specs=out_specs,
        out_shape=out_shape,
        compiler_params=_params("parallel"),
    )(x2, g, wn, wt)


def _dot3(a_f32, b_exact):
    b = b_exact.astype(BF16)
    hi = a_f32.astype(BF16)
    r1 = a_f32 - hi.astype(F32)
    mid = r1.astype(BF16)
    lo = (r1 - mid.astype(F32)).astype(BF16)
    return (jnp.dot(hi, b, preferred_element_type=F32) + jnp.dot(mid, b, preferred_element_type=F32)
            + jnp.dot(lo, b, preferred_element_type=F32))


def _mlstm_kernel(qk_ref, vT_ref, oT_ref, smT_ref, conv_ref, bcol_ref, hn_ref,
                  y_ref, xp_scr, st_scr, m_scr, *, L):
    c = pl.program_id(1)
    halo = SUBLANES
    DK, DV = ML_DQK, ML_DV

    @pl.when(c == 0)
    def _():
        xp_scr[0:halo, :] = jnp.zeros((halo, 2 * ML_QK_W), F32)
        st_scr[...] = jnp.zeros_like(st_scr)
        m_scr[...] = jnp.zeros_like(m_scr)

    cur = qk_ref[0]
    xp_scr[halo:halo + L, :] = cur
    base = halo - (ML_CONV - 1)
    acc = conv_ref[0:1, :] * xp_scr[base:base + L, :]
    for j in range(1, ML_CONV):
        acc = acc + conv_ref[j:j + 1, :] * xp_scr[base + j:base + j + L, :]
    xp_scr[0:halo, :] = cur[L - halo:L, :]
    qk = acc * jax.nn.sigmoid(acc)
    qT_all = (qk[:, :ML_QK_W] * (DK ** -0.5)).T.astype(BF16)
    k_all = qk[:, ML_QK_W:].astype(BF16)

    grow = smT_ref[0][0:SUBLANES, :] + bcol_ref[...]
    si = lax.broadcasted_iota(I32, (L, L), 0)
    ji = lax.broadcasted_iota(I32, (L, L), 1)
    causal = si <= ji
    b_row = _dot3(_log_sigmoid(grow), causal)

    for h in range(ML_HEADS):
        qT = qT_all[h * DK:(h + 1) * DK, :]
        kh = k_all[:, h * DK:(h + 1) * DK]
        vT = vT_ref[0, h * DV:(h + 1) * DV, :]
        b_j = b_row[ML_HEADS + h:ML_HEADS + h + 1, :]
        c_row = grow[h:h + 1, :] - b_j
        c_col = jnp.broadcast_to(c_row, (LANES, L)).T
        dlog = jnp.where(causal, b_j + jnp.concatenate([c_col] * (L // LANES), axis=1), NEG_INF)
        m0 = m_scr[h:h + 1, 0:1]
        inter = b_j + m0
        m_t = jnp.maximum(inter, jnp.max(dlog, axis=0, keepdims=True))
        w_inter = jnp.exp(inter - m_t)
        w_intra = jnp.exp(dlog - m_t) * jnp.dot(kh, qT, preferred_element_type=F32)
        st = st_scr[h]
        cq = jnp.dot(st.astype(BF16), qT, preferred_element_type=F32)
        num = w_inter * cq[:DV] + jnp.dot(vT, w_intra.astype(BF16), preferred_element_type=F32)
        den = w_inter * cq[DV:DV + 1] + jnp.sum(w_intra, axis=0, keepdims=True)
        hh = num / jnp.maximum(jnp.abs(den), jnp.exp(-m_t))
        hh = hh * lax.rsqrt(jnp.mean(hh * hh, axis=0, keepdims=True) + EPS)
        rows = slice(h * DV, (h + 1) * DV)
        y_ref[0, :, rows] = (jax.nn.sigmoid(oT_ref[0, rows, :]) * hh * hn_ref[rows, :]).T.astype(y_ref.dtype)

        b_last = b_j[:, L - 1:L]
        a = b_last + c_row
        m_loc = jnp.max(a, axis=1, keepdims=True)
        wa = jnp.exp(a - m_loc)
        lhs = jnp.concatenate([(vT.astype(F32) * wa).astype(BF16),
                               jnp.broadcast_to(wa, (SUBLANES, L)).astype(BF16)], axis=0)
        c_loc = jnp.dot(lhs, kh, preferred_element_type=F32)
        m_new = jnp.maximum(b_last + m0, m_loc)
        s_old = jnp.exp(b_last + m0 - m_new)
        s_loc = jnp.exp(m_loc - m_new)
        st_scr[h] = s_old * st + s_loc * c_loc
        m_scr[h:h + 1, :] = jnp.broadcast_to(m_new, (1, LANES))


def _mlstm(qk, vT, oT, smT, conv_w, i_bias, f_bias, head_norm, B, S):
    L = min(ML_CHUNK, S)
    assert L % LANES == 0
    nc = S // L
    qk = qk.reshape(B, S, 2 * ML_QK_W)
    bcol = jnp.concatenate([i_bias, f_bias]).astype(F32).reshape(2 * ML_HEADS, 1)
    hn = jnp.broadcast_to(head_norm.astype(F32)[:, None], (ML_V_W, L))
    blkT = lambda n: pl.BlockSpec((1, n, L), lambda b, c: (b, 0, c))
    full = lambda a: pl.BlockSpec(a.shape, lambda b, c: (0,) * a.ndim)
    return pl.pallas_call(
        functools.partial(_mlstm_kernel, L=L),
        grid=(B, nc),
        in_specs=[pl.BlockSpec((1, L, 2 * ML_QK_W), lambda b, c: (b, c, 0)),
                  blkT(ML_V_W), blkT(ML_V_W), blkT(16), full(conv_w), full(bcol), full(hn)],
        out_specs=pl.BlockSpec((1, L, ML_V_W), lambda b, c: (b, c, 0)),
        out_shape=jax.ShapeDtypeStruct((B, S, ML_V_W), BF16),
        scratch_shapes=[pltpu.VMEM((L + SUBLANES, 2 * ML_QK_W), F32),
                        pltpu.VMEM((ML_HEADS, ML_DV + SUBLANES, ML_DQK), F32),
                        pltpu.VMEM((SUBLANES, LANES), F32)],
        compiler_params=_params("parallel", "arbitrary"),
    )(qk, vT, oT, smT, conv_w.astype(F32), bcol, hn)


def _block_diag_T(qT, heads, dh):
    rid = lax.broadcasted_iota(I32, qT.shape, 0)
    zero = jnp.zeros_like(qT)
    return jnp.concatenate(
        [jnp.where((rid >= h * dh) & (rid < (h + 1) * dh), qT, zero) for h in range(heads)], axis=1)


def _accumulate(h, p, vT_chunk, l, acc_scr, dh):
    rows = slice(h * dh, (h + 1) * dh)
    acc_scr[rows, :] += jnp.dot(vT_chunk[rows, :], p.astype(BF16), preferred_element_type=F32)
    return l + jnp.sum(p, axis=0, keepdims=True)


def _finish_attention(l, acc_scr, y_ref, heads, dh):
    outs = [acc_scr[h * dh:(h + 1) * dh, :] / l[h] for h in range(heads)]
    y_ref[0] = jnp.concatenate(outs, axis=0).T.astype(y_ref.dtype)


def _max_key_norm2(kb, heads, dh, TQ):
    ch = lax.broadcasted_iota(I32, (heads * dh, LANES), 0) // dh
    hd = lax.broadcasted_iota(I32, (heads * dh, LANES), 1)
    n2 = jnp.dot((kb * kb).astype(BF16), (ch == hd).astype(BF16), preferred_element_type=F32)
    n2 = jnp.max(n2, axis=0, keepdims=True) * (1.0 + 2.0 ** -7)
    return jnp.concatenate([jnp.broadcast_to(n2[:, h:h + 1], (1, TQ)) for h in range(heads)], axis=1)


def _norm_bound(qTf, k2, heads, dh, TQ):
    out = []
    for h in range(heads):
        qh = qTf[h * dh:(h + 1) * dh, :]
        q2 = jnp.sum(qh * qh, axis=0, keepdims=True)
        out.append(jnp.sqrt(q2 * k2[:, h * TQ:(h + 1) * TQ]) * (1.0 + 2.0 ** -8))
    return out


def _underflowed(l):
    small = l[0]
    for lh in l[1:]:
        small = jnp.minimum(small, lh)
    return jnp.logical_not(jnp.min(small) >= SOFTMAX_L_MIN)


def _moba_kernel(qT_ref, k_ref, vT_ref, y_ref, km_scr, kn_scr, acc_scr, s_scr, *, NB, NBP, n_sel):
    H, DH, BS = MOBA_HEADS, MOBA_DH, MOBA_BLOCK
    TQ = BS
    qi = pl.program_id(1)

    @pl.when(qi == 0)
    def _():
        km_scr[...] = jnp.zeros_like(km_scr)
        kn_scr[...] = jnp.zeros_like(kn_scr)

        def body(j, carry):
            kb = k_ref[0, pl.ds(pl.multiple_of(j * BS, BS), BS), :].astype(F32)
            km_scr[pl.ds(j, 1), :] = jnp.mean(kb, axis=0, keepdims=True)
            kn_scr[pl.ds(j, 1), :] = _max_key_norm2(kb, H, DH, TQ)
            return carry
        lax.fori_loop(0, NB, body, 0)

    qT = qT_ref[0]
    qTf = qT.astype(F32)
    km = km_scr[...]
    gate = jnp.concatenate(
        [jnp.dot(km[:, h * DH:(h + 1) * DH], qTf[h * DH:(h + 1) * DH, :], precision=lax.Precision.HIGHEST,
                 preferred_element_type=F32) for h in range(H)], axis=1) * (1.0 / ATT_QSCALE)
    blk = lax.broadcasted_iota(I32, gate.shape, 0)
    g = jnp.where(blk < qi, gate, NEG_INF)
    sels = []
    for _ in range(n_sel):
        mx = jnp.max(g, axis=0, keepdims=True)
        isel = jnp.min(jnp.where(g == mx, blk, NBP), axis=0, keepdims=True)
        sels.append(jnp.where(mx > NEG_INF, isel, -1))
        g = jnp.where(blk == isel, NEG_INF, g)

    qs = _block_diag_T(qT, H, DH)

    def hit_row(j):
        hit = sels[0] == j
        for s in sels[1:]:
            hit = hit | (s == j)
        return hit

    blk_start = lambda j: pl.multiple_of(j * BS, BS)
    blk_scores = lambda j: jnp.dot(k_ref[0, pl.ds(blk_start(j), BS), :], qs, preferred_element_type=F32)
    hcols = lambda row, h: row[:, h * TQ:(h + 1) * TQ]
    causal = lax.broadcasted_iota(I32, (BS, TQ), 0) <= lax.broadcasted_iota(I32, (BS, TQ), 1)

    def diag_scores():
        s_all = blk_scores(qi)
        return [jnp.where(causal, hcols(s_all, h), NEG_INF) for h in range(H)]

    def max_sweep():
        def max_block(j, m):
            return jnp.maximum(m, jnp.where(hit_row(j), jnp.max(blk_scores(j), axis=0, keepdims=True), NEG_INF))
        m_all = lax.fori_loop(0, qi, max_block, jnp.full((1, H * TQ), NEG_INF, F32))
        s_diag = diag_scores()
        return [jnp.maximum(hcols(m_all, h), jnp.max(s_diag[h], axis=0, keepdims=True)) for h in range(H)]

    def sum_sweep(m):
        acc_scr[...] = jnp.zeros_like(acc_scr)

        def sum_group(width):
            def body(g, l):
                for u in range(width):
                    s_scr[u] = blk_scores(width * g + u)
                for u in range(width):
                    j = width * g + u
                    hit = hit_row(j)
                    vT_blk = vT_ref[0, :, pl.ds(blk_start(j), BS)]
                    l = tuple(_accumulate(h, jnp.exp2(s_scr[u, :, h * TQ:(h + 1) * TQ]
                                                      - jnp.where(hcols(hit, h), m[h], jnp.inf)),
                                          vT_blk, l[h], acc_scr, DH) for h in range(H))
                return l
            return body
        nwide = qi // ATT_STAGE
        l = lax.fori_loop(0, nwide, sum_group(ATT_STAGE), (jnp.zeros((1, TQ), F32),) * H)
        l = lax.fori_loop((ATT_STAGE // 2) * nwide, qi // 2, sum_group(2), l)
        l = lax.fori_loop(2 * (qi // 2), qi, sum_group(1), l)
        s_diag = diag_scores()
        vT_diag = vT_ref[0, :, pl.ds(blk_start(qi), BS)]
        return [_accumulate(h, jnp.exp2(s_diag[h] - m[h]), vT_diag, l[h], acc_scr, DH) for h in range(H)]

    blk_n = lax.broadcasted_iota(I32, kn_scr.shape, 0)
    k2 = jnp.max(jnp.where(blk_n <= qi, kn_scr[...], 0.0), axis=0, keepdims=True)
    l = sum_sweep(_norm_bound(qTf, k2, H, DH, TQ))
    _finish_attention(l, acc_scr, y_ref, H, DH)

    @pl.when(_underflowed(l))
    def _():
        _finish_attention(sum_sweep(max_sweep()), acc_scr, y_ref, H, DH)


def _moba(qT, k, vT, B, S):
    BS = MOBA_BLOCK
    assert S % BS == 0
    NB = S // BS
    NBP = -(-NB // SUBLANES) * SUBLANES
    n_sel = max(min(MOBA_TOPK, NB - 1), 1)
    k = k.reshape(B, S, MOBA_W)
    return pl.pallas_call(
        functools.partial(_moba_kernel, NB=NB, NBP=NBP, n_sel=n_sel),
        grid=(B, NB),
        in_specs=[pl.BlockSpec((1, MOBA_W, BS), lambda b, i: (b, 0, i)),
                  pl.BlockSpec((1, S, MOBA_W), lambda b, i: (b, 0, 0)),
                  pl.BlockSpec((1, MOBA_W, S), lambda b, i: (b, 0, 0))],
        out_specs=pl.BlockSpec((1, BS, MOBA_W), lambda b, i: (b, i, 0)),
        out_shape=jax.ShapeDtypeStruct((B, S, MOBA_W), BF16),
        scratch_shapes=[pltpu.VMEM((NBP, MOBA_W), F32),
                        pltpu.VMEM((NBP, MOBA_HEADS * BS), F32),
                        pltpu.VMEM((MOBA_W, BS), F32),
                        pltpu.VMEM((ATT_STAGE, BS, MOBA_HEADS * BS), F32)],
        compiler_params=_params("parallel", "arbitrary"),
    )(qT, k, vT)


def _dsa_kernel(qT_ref, k_ref, vT_ref, iqT_ref, ik_ref, smT_ref, y_ref,
                e_scr, gmax_scr, acc_scr, thr_scr, kn_scr, s_scr, *, TQ, topk, spow, n_chunks):
    H, DH = DSA_HEADS, DSA_DH
    KC = TQ
    qi = pl.program_id(1)
    nch = qi + 1
    chunk_start = lambda c: pl.multiple_of(c * KC, KC)

    def chunk_loops(n, wide, make_body, carry, regroup=lambda c, w: c):
        carry = lax.fori_loop(0, n // wide, make_body(wide), carry)
        carry = lax.fori_loop((wide // 2) * (n // wide), n // 2, make_body(2), regroup(carry, 2))
        return lax.fori_loop(2 * (n // 2), n, make_body(1), regroup(carry, 1))

    @pl.when(qi == 0)
    def _():
        kn_scr[...] = jnp.zeros_like(kn_scr)

        def body(c, carry):
            kn_scr[pl.ds(c, 1), :] = _max_key_norm2(k_ref[0, pl.ds(chunk_start(c), KC), :].astype(F32), H, DH, TQ)
            return carry
        lax.fori_loop(0, n_chunks, body, 0)
    sum8 = lambda b: jnp.sum(b.astype(I32).reshape(KC // SUBLANES, SUBLANES, TQ), axis=0)
    any_row = lambda r: jnp.max(r.astype(I32)) > 0

    iqT = iqT_ref[0]
    iq_cat = jnp.concatenate([iqT[h * IDX_DH:(h + 1) * IDX_DH, :] for h in range(IDX_HEADS)], axis=1)
    iw = smT_ref[0][2 * ML_HEADS:2 * ML_HEADS + IDX_HEADS, :] * (IDX_HEADS ** -0.5 * IDX_DH ** -0.5)
    t_pos = qi * TQ + lax.broadcasted_iota(I32, (KC, TQ), 1)
    s_off = lax.broadcasted_iota(I32, (KC, TQ), 0)
    gmax_scr[...] = jnp.full((KC, TQ), NEG_INF, F32)

    def score_group(width, masked=False):
        def body(g, tally):
            for u in range(width):
                ikc = ik_ref[0, pl.ds(chunk_start(width * g + u), KC), :][:, :IDX_DH]
                s_scr[u] = jnp.dot(ikc, iq_cat, preferred_element_type=F32)
            for u in range(width):
                c = width * g + u
                sc = iw[0:1, :] * jnp.maximum(s_scr[u, :, 0:TQ], 0.0)
                for h in range(1, IDX_HEADS):
                    sc = sc + iw[h:h + 1, :] * jnp.maximum(s_scr[u, :, h * TQ:(h + 1) * TQ], 0.0)
                if masked:
                    sc = jnp.where(c * KC + s_off <= t_pos, sc, NEG_INF)
                e_scr[pl.ds(chunk_start(c), KC), :] = sc
                gmax_scr[...] = jnp.maximum(gmax_scr[...], sc)
                tally = tally + jnp.sum(jnp.where(sc > 0.0, 1, jnp.where(sc == 0.0, 1 << 16, 0))
                                        .reshape(KC // SUBLANES, SUBLANES, TQ), axis=0)
            return tally
        return body
    assert n_chunks * KC < (1 << 16)
    zero8 = jnp.zeros((SUBLANES, TQ), I32)
    tally = chunk_loops(qi, 4, score_group, zero8)
    tally = jnp.sum(score_group(1, True)(qi, tally), axis=0, keepdims=True)
    n_pos = tally & 0xFFFF
    n_nonneg = n_pos + (tally >> 16)

    def sweep_keys(fold, merge, init):
        def group(width):
            def body(g, acc):
                return tuple(merge(acc[u], fold(e_scr[pl.ds(chunk_start(width * g + u), KC), :]))
                             for u in range(width))
            return body

        def regroup(acc, width):
            half = len(acc) // 2
            acc = tuple(merge(acc[u], acc[u + half]) for u in range(half))
            return acc if half == width else regroup(acc, width)
        return chunk_loops(nch, 4, group, (init,) * 4, regroup)[0]

    def count_gt(cand):
        return jnp.sum(sweep_keys(lambda x: sum8(x > cand), lambda p, q: p + q, zero8), axis=0, keepdims=True)

    def min_gt(cand):
        fold = lambda x: jnp.min(jnp.where(x > cand, x, jnp.inf).reshape(KC // SUBLANES, SUBLANES, TQ), axis=0)
        return jnp.min(sweep_keys(fold, jnp.minimum, jnp.full((SUBLANES, TQ), jnp.inf, F32)), axis=0, keepdims=True)

    def rewrite(rows, thr):
        def body(c, carry):
            x = e_scr[pl.ds(chunk_start(c), KC), :]
            rank = (2 * spow - (c * KC + s_off)).astype(F32)
            new = jnp.where(x > thr, jnp.inf, jnp.where(x == thr, rank, NEG_INF))
            e_scr[pl.ds(chunk_start(c), KC), :] = jnp.where(rows, new, x)
            return carry
        lax.fori_loop(0, nch, body, 0)

    def halve(_, c):
        lo, hi, act, tie, n_lo = c
        mid = lo + (hi - lo) * 0.5
        inside = (mid > lo) & (mid < hi)
        cnt = count_gt(mid)
        live = act > 0.0
        run = live & inside
        up = run & (cnt >= topk)
        lo = jnp.where(up, mid, lo)
        n_lo = jnp.where(up, cnt, n_lo)
        hi = jnp.where(run & (cnt < topk), mid, hi)
        collapsed = live & jnp.logical_not(inside)
        tie = jnp.where(collapsed, 1.0, tie)
        act = jnp.where(collapsed | (run & (cnt == topk)), 0.0, act)
        return lo, hi, act, tie, n_lo

    def excess(state):
        return jnp.max(jnp.where(state[2] > 0.0, state[4] - topk, 0).astype(F32))

    def bisect(lo, hi, act, warmup):
        state = (lo, hi, act, jnp.zeros_like(act), jnp.full((1, TQ), 1 << 20, I32))
        state = lax.fori_loop(0, warmup, halve, state)

        def one_more(c):
            state = halve(0, c[1])
            return c[0] + 1, state, excess(state)
        near = lax.while_loop(lambda c: jnp.logical_and(c[0] < BISECT_CAP, c[2] > BISECT_DROP), one_more,
                              (jnp.int32(warmup), state, excess(state)))
        lo, hi, act, tie, n_lo = near[1]
        cur, left = lo, jnp.where(act > 0.0, n_lo - topk, 0)
        for _ in range(BISECT_DROP):
            cur = jnp.where(left > 0, min_gt(cur), cur)
            left = left - 1
        done = (act > 0.0) & (count_gt(cur) == topk)
        lo = jnp.where(done, cur, lo)
        act = jnp.where(done, 0.0, act)

        def body(c):
            state = c[1]
            for _ in range(BISECT_STEPS_PER_CHECK):
                state = halve(0, state)
            return c[0] + BISECT_STEPS_PER_CHECK, state, jnp.max(state[2])
        out = lax.while_loop(lambda c: jnp.logical_and(c[0] < BISECT_CAP, c[2] > 0.0), body,
                             (near[0], (lo, hi, act, tie, n_lo), jnp.max(act)))
        return out[1][0], out[1][1], out[1][3]

    fmax = float(jnp.finfo(F32).max)
    gm = gmax_scr[...]
    g_lo = jnp.min(gm, axis=0, keepdims=True)
    g_hi = jnp.max(gm, axis=0, keepdims=True)
    n_vis = qi * TQ + lax.broadcasted_iota(I32, (1, TQ), 1) + 1
    few = n_vis <= topk
    zero_tie = (n_pos < topk) & (n_nonneg >= topk) & jnp.logical_not(few)

    @pl.when(any_row(zero_tie))
    def _():
        rewrite(zero_tie, jnp.zeros((1, TQ), F32))

    rank_lo, rank_hi = float(spow), float(2 * spow + 1)
    below = jnp.maximum(g_lo, -fmax)
    below = below - jnp.abs(below) * (2.0 ** -10) - 1e-30
    lo0 = jnp.where(zero_tie, rank_lo, jnp.maximum(below, -fmax))
    hi0 = jnp.where(zero_tie, rank_hi, g_hi)
    lo, hi, tie = bisect(lo0, hi0, jnp.where(few, 0.0, 1.0), BISECT_WARMUP)
    thr_scr[0:1, :] = jnp.where(few, NEG_INF, lo)

    @pl.when(jnp.max(tie) > 0.0)
    def _():
        rows = tie > 0.0
        rewrite(rows, hi)
        lo2, _, _ = bisect(jnp.full((1, TQ), rank_lo, F32), jnp.full((1, TQ), rank_hi, F32), tie, 0)
        thr_scr[0:1, :] = jnp.where(rows, lo2, thr_scr[0:1, :])

    thr = thr_scr[0:1, :]
    qT = qT_ref[0]
    qs = _block_diag_T(qT, H, DH)
    hcols = lambda row, h: row[:, h * TQ:(h + 1) * TQ]

    def chunk_scores(c):
        start = chunk_start(c)
        sel = e_scr[pl.ds(start, KC), :] > thr
        s_all = jnp.dot(k_ref[0, pl.ds(start, KC), :], qs, preferred_element_type=F32)
        return sel, s_all

    def max_sweep():
        def max_chunk(c, m):
            sel, s_all = chunk_scores(c)
            return tuple(jnp.maximum(m[h], jnp.max(jnp.where(sel, hcols(s_all, h), NEG_INF), axis=0, keepdims=True))
                         for h in range(H))
        m = lax.fori_loop(0, nch, max_chunk, (jnp.full((1, TQ), NEG_INF, F32),) * H)
        return [jnp.where(mh == NEG_INF, 0.0, mh) for mh in m]

    def sum_sweep(m):
        acc_scr[...] = jnp.zeros_like(acc_scr)

        def sum_group(width):
            def body(g, l):
                for u in range(width):
                    s_scr[u] = jnp.dot(k_ref[0, pl.ds(chunk_start(width * g + u), KC), :], qs,
                                       preferred_element_type=F32)
                for u in range(width):
                    start = chunk_start(width * g + u)
                    sel = e_scr[pl.ds(start, KC), :] > thr
                    vT_chunk = vT_ref[0, :, pl.ds(start, KC)]
                    l = tuple(_accumulate(h, jnp.exp2(s_scr[u, :, h * TQ:(h + 1) * TQ] + jnp.where(sel, -m[h], NEG_INF)),
                                          vT_chunk, l[h], acc_scr, DH) for h in range(H))
                return l
            return body
        return chunk_loops(nch, ATT_STAGE, sum_group, (jnp.zeros((1, TQ), F32),) * H)

    chunk_n = lax.broadcasted_iota(I32, kn_scr.shape, 0)
    k2 = jnp.max(jnp.where(chunk_n < nch, kn_scr[...], 0.0), axis=0, keepdims=True)
    l = sum_sweep(_norm_bound(qT.astype(F32), k2, H, DH, TQ))
    _finish_attention(l, acc_scr, y_ref, H, DH)

    @pl.when(_underflowed(l))
    def _():
        _finish_attention(sum_sweep(max_sweep()), acc_scr, y_ref, H, DH)


def _dsa(qT, k, vT, iqT, ik, smT, B, S):
    TQ = min(ATT_TQ, S)
    topk = min(DSA_TOPK_MAX, S // 4)
    assert S % TQ == 0 and topk <= TQ
    spow = 1 << max((S - 1).bit_length(), 1)
    k = k.reshape(B, S, DSA_W)
    ik = ik.reshape(B, S, LANES)
    qblk = lambda n: pl.BlockSpec((1, n, TQ), lambda b, i: (b, 0, i))
    return pl.pallas_call(
        functools.partial(_dsa_kernel, TQ=TQ, topk=topk, spow=spow, n_chunks=S // TQ),
        grid=(B, S // TQ),
        in_specs=[qblk(DSA_W),
                  pl.BlockSpec((1, S, DSA_W), lambda b, i: (b, 0, 0)),
                  pl.BlockSpec((1, DSA_W, S), lambda b, i: (b, 0, 0)),
                  qblk(IDX_HEADS * IDX_DH),
                  pl.BlockSpec((1, S, LANES), lambda b, i: (b, 0, 0)),
                  qblk(16)],
        out_specs=pl.BlockSpec((1, TQ, DSA_W), lambda b, i: (b, i, 0)),
        out_shape=jax.ShapeDtypeStruct((B, S, DSA_W), BF16),
        scratch_shapes=[pltpu.VMEM((S, TQ), F32),
                        pltpu.VMEM((TQ, TQ), F32),
                        pltpu.VMEM((DSA_W, TQ), F32),
                        pltpu.VMEM((SUBLANES, TQ), F32),
                        pltpu.VMEM((-(-(S // TQ) // SUBLANES) * SUBLANES, DSA_HEADS * TQ), F32),
                        pltpu.VMEM((ATT_STAGE, TQ, DSA_HEADS * TQ), F32)],
        compiler_params=_params("parallel", "arbitrary"),
    )(qT, k, vT, iqT, ik, smT)


def _out_mlp_kernel(yml_ref, ymb_ref, yds_ref, x_ref, wout_ref, gpost_ref, gpre_ref,
                    w1_ref, w2_ref, g2_ref, out_ref, x1_scr, h_scr, acc_scr):
    f = pl.program_id(1)

    @pl.when(f == 0)
    def _():
        slab = x_ref.shape[0] // MLP_SLABS
        for r in range(MLP_SLABS):
            rows = slice(r * slab, (r + 1) * slab)
            mix = jnp.dot(yml_ref[rows, :], wout_ref[0:ML_V_W, :], preferred_element_type=F32)
            mix = mix + jnp.dot(ymb_ref[rows, :], wout_ref[ML_V_W:ML_V_W + MOBA_W, :], preferred_element_type=F32)
            mix = mix + jnp.dot(yds_ref[rows, :], wout_ref[ML_V_W + MOBA_W:, :], preferred_element_type=F32)
            x1 = x_ref[rows, :] + _rms(mix, gpost_ref[...])
            x1_scr[rows, :] = x1
            h_scr[rows, :] = _rms(x1, gpre_ref[...]).astype(BF16)
        acc_scr[...] = jnp.zeros_like(acc_scr)

    u = jnp.maximum(jnp.dot(h_scr[...], w1_ref[...], preferred_element_type=F32), 0.0)
    acc_scr[...] += jnp.dot((u * u).astype(BF16), w2_ref[...], preferred_element_type=F32)

    @pl.when(f == pl.num_programs(1) - 1)
    def _():
        out_ref[...] = x1_scr[...] + _rms(acc_scr[...], g2_ref[...])


def _out_mlp(yml, ymb, yds, x2, w_out, g_post, g_pre, w1, w2, g2):
    T, D = x2.shape
    tm = min(MLP_TM, T)
    tf = min(MLP_TF, D_FF)
    row = lambda n: pl.BlockSpec((tm, n), lambda i, f: (i, 0))
    full = lambda a: pl.BlockSpec(a.shape, lambda i, f: (0,) * a.ndim)
    return pl.pallas_call(
        _out_mlp_kernel,
        grid=(T // tm, D_FF // tf),
        in_specs=[row(ML_V_W), row(MOBA_W), row(DSA_W), row(D), full(w_out), full(g_post), full(g_pre),
                  pl.BlockSpec((D, tf), lambda i, f: (0, f)),
                  pl.BlockSpec((tf, D), lambda i, f: (f, 0)),
                  full(g2)],
        out_specs=row(D),
        out_shape=jax.ShapeDtypeStruct((T, D), F32),
        scratch_shapes=[pltpu.VMEM((tm, D), F32), pltpu.VMEM((tm, D), BF16), pltpu.VMEM((tm, D), F32)],
        compiler_params=_params("parallel", "arbitrary"),
    )(yml, ymb, yds, x2, w_out, g_post, g_pre, w1, w2, g2)


def kernel(x, norm_mix_pre, w_in, ml_conv, ml_i_bias, ml_f_bias, ml_head_norm, w_out, norm_mix_post,
           norm_mlp_pre, w_ff1, w_ff2, norm_mlp_post):
    B, S, D = x.shape
    depth = w_in.shape[0]
    x2 = x.reshape(B * S, D)
    gain = lambda g: g.reshape(1, D).astype(F32)
    for l in range(depth):
        wn, wt = _prep_in_weights(w_in[l])
        (qk, mbk, dsk, ixk, mbqT, mbvT, dsqT, dsvT, ixqT, mlvT, mloT, smT) = _in_proj(
            x2, gain(norm_mix_pre[l]), wn, wt, B, S)
        y_ml = _mlstm(qk, mlvT, mloT, smT, ml_conv[l], ml_i_bias[l], ml_f_bias[l], ml_head_norm[l], B, S)
        y_mb = _moba(mbqT, mbk, mbvT, B, S)
        y_ds = _dsa(dsqT, dsk, dsvT, ixqT, ixk, smT, B, S)
        x2 = _out_mlp(y_ml.reshape(B * S, ML_V_W), y_mb.reshape(B * S, MOBA_W), y_ds.reshape(B * S, DSA_W),
                      x2, w_out[l].astype(BF16), gain(norm_mix_post[l]), gain(norm_mlp_pre[l]),
                      w_ff1[l].astype(BF16), w_ff2[l].astype(BF16), gain(norm_mlp_post[l]))
    return x2.reshape(B, S, D)
```

```python
import functools

import jax
import jax.numpy as jnp
from jax import lax
from jax.experimental import pallas as pl
from jax.experimental.pallas import tpu as pltpu

F32 = jnp.float32
BF16 = jnp.bfloat16
I32 = jnp.int32

D_MODEL = 1024
ML_HEADS, ML_DQK, ML_DV, ML_CONV = 4, 64, 128, 4
MOBA_HEADS, MOBA_DH, MOBA_BLOCK, MOBA_TOPK = 4, 64, 256, 3
DSA_HEADS, DSA_DH, IDX_HEADS, IDX_DH, DSA_TOPK_MAX = 4, 64, 4, 64, 256
D_FF = 4 * D_MODEL
EPS = 1e-6

ML_QK_W = ML_HEADS * ML_DQK
ML_V_W = ML_HEADS * ML_DV
MOBA_W = MOBA_HEADS * MOBA_DH
DSA_W = DSA_HEADS * DSA_DH
IN_SPLITS = (ML_QK_W, ML_QK_W, ML_V_W, ML_V_W, ML_HEADS, ML_HEADS,
             MOBA_W, MOBA_W, MOBA_W,
             DSA_W, DSA_W, DSA_W, IDX_HEADS * IDX_DH, IDX_DH, IDX_HEADS)

LANES = 128
SUBLANES = 8
VMEM_LIMIT = 52 * 1024 * 1024

ML_CHUNK = 512
ATT_TQ = 256
ATT_STAGES = (8, 4, 2, 1)
ATT_STAGE = ATT_STAGES[0]
PROJ_TM = 1024
MLP_TM = 1024
MLP_TF = 512
MLP_SLABS = 4

NEG_INF = float("-inf")
BISECT_CAP = 300
BISECT_WARMUP = 12
BISECT_DROP = 2
BISECT_STEPS_PER_CHECK = 2
SOFTMAX_L_MIN = 1e-30
assert MOBA_DH == DSA_DH
ATT_QSCALE = MOBA_DH ** -0.5 * 1.4426950408889634

_NT = (((1,), (1,)), ((), ()))


def _chunk_loops(n, widths, make_body, carry, regroup=lambda c, w: c):
    covered = 0
    for i, w in enumerate(widths):
        if i:
            carry = regroup(carry, w)
        carry = lax.fori_loop(covered // w, n // w, make_body(w), carry)
        covered = w * (n // w)
    return carry


def _rms(x, g):
    return x * lax.rsqrt(jnp.mean(x * x, axis=-1, keepdims=True) + EPS) * g


def _log_sigmoid(x):
    return jnp.minimum(x, 0.0) - jnp.log1p(jnp.exp(-jnp.abs(x)))


def _params(*sem):
    return pltpu.CompilerParams(dimension_semantics=sem, vmem_limit_bytes=VMEM_LIMIT)


_N_QK, _N_MBK, _N_DSK, _N_IXK = 0, 512, 768, 1024
_N_TOTAL = 1152
_T_MBQ, _T_MBV, _T_DSQ, _T_DSV, _T_IXQ, _T_MLV, _T_MLO, _T_SM = 0, 256, 512, 768, 1024, 1280, 1792, 2304
_T_TOTAL = 2320


def _in_proj_kernel(x_ref, g_ref, wn_ref, wt_ref,
                    qk_ref, mbk_ref, dsk_ref, ixk_ref,
                    mbqT_ref, mbvT_ref, dsqT_ref, dsvT_ref, ixqT_ref, mlvT_ref, mloT_ref, smT_ref):
    h = _rms(x_ref[...], g_ref[...]).astype(BF16)

    def mm(a, n):
        return jnp.dot(h, wn_ref[:, a:a + n], preferred_element_type=F32)

    def mt(a, n):
        return lax.dot_general(wt_ref[a:a + n, :], h, _NT, preferred_element_type=F32)

    qk_ref[...] = mm(_N_QK, 512)
    mbk_ref[...] = mm(_N_MBK, 256).astype(BF16)
    dsk_ref[...] = mm(_N_DSK, 256).astype(BF16)
    ixk_ref[...] = mm(_N_IXK, 128).astype(BF16)
    mbqT_ref[0] = (mt(_T_MBQ, 256) * ATT_QSCALE).astype(BF16)
    mbvT_ref[0] = mt(_T_MBV, 256).astype(BF16)
    dsqT_ref[0] = (mt(_T_DSQ, 256) * ATT_QSCALE).astype(BF16)
    dsvT_ref[0] = mt(_T_DSV, 256).astype(BF16)
    ixqT_ref[0] = mt(_T_IXQ, 256).astype(BF16)
    mlvT_ref[0] = mt(_T_MLV, 512).astype(BF16)
    mloT_ref[0] = mt(_T_MLO, 512)
    smT_ref[0] = mt(_T_SM, 16)


def _prep_in_weights(w_in):
    pts, acc = [], 0
    for n in IN_SPLITS:
        pts.append((acc, acc + n))
        acc += n
    col = lambda i: w_in[:, pts[i][0]:pts[i][1]]
    (ml_q, ml_k, ml_v, ml_o, ml_i, ml_f, mb_q, mb_k, mb_v,
     ds_q, ds_k, ds_v, ix_q, ix_k, ix_w) = [col(i) for i in range(len(IN_SPLITS))]
    d = w_in.shape[0]
    small = jnp.concatenate([ml_i, ml_f, ix_w], axis=1)
    wn = jnp.concatenate([
        ml_q, ml_k, mb_k, ds_k,
        ix_k, jnp.zeros((d, 128 - IDX_DH), w_in.dtype)], axis=1)
    wt = jnp.concatenate([
        mb_q, mb_v, ds_q, ds_v, ix_q, ml_v, ml_o,
        small, jnp.zeros((d, 16 - small.shape[1]), w_in.dtype)], axis=1).T
    assert wn.shape == (d, _N_TOTAL) and wt.shape == (_T_TOTAL, d)
    return wn.astype(BF16), wt.astype(BF16)


def _in_proj(x2, g, wn, wt, B, S):
    T, D = x2.shape
    tm = min(PROJ_TM, S)
    nsb = S // tm
    row = lambda n: pl.BlockSpec((tm, n), lambda i: (i, 0))
    tr = lambda n: pl.BlockSpec((1, n, tm), lambda i: (i // nsb, 0, i % nsb))
    full = lambda a: pl.BlockSpec(a.shape, lambda i: (0,) * a.ndim)
    out_shape = (
        jax.ShapeDtypeStruct((T, 512), F32),
        jax.ShapeDtypeStruct((T, 256), BF16),
        jax.ShapeDtypeStruct((T, 256), BF16),
        jax.ShapeDtypeStruct((T, 128), BF16),
        jax.ShapeDtypeStruct((B, 256, S), BF16),
        jax.ShapeDtypeStruct((B, 256, S), BF16),
        jax.ShapeDtypeStruct((B, 256, S), BF16),
        jax.ShapeDtypeStruct((B, 256, S), BF16),
        jax.ShapeDtypeStruct((B, 256, S), BF16),
        jax.ShapeDtypeStruct((B, 512, S), BF16),
        jax.ShapeDtypeStruct((B, 512, S), F32),
        jax.ShapeDtypeStruct((B, 16, S), F32),
    )
    out_specs = (row(512), row(256), row(256), row(128),
                 tr(256), tr(256), tr(256), tr(256), tr(256), tr(512), tr(512), tr(16))
    return pl.pallas_call(
        _in_proj_kernel,
        grid=(T // tm,),
        in_specs=[row(D), full(g), full(wn), full(wt)],
        out_specs=out_specs,
        out_shape=out_shape,
        compiler_params=_params("parallel"),
    )(x2, g, wn, wt)


def _dot3(a_f32, b_exact):
    b = b_exact.astype(BF16)
    hi = a_f32.astype(BF16)
    r1 = a_f32 - hi.astype(F32)
    mid = r1.astype(BF16)
    lo = (r1 - mid.astype(F32)).astype(BF16)
    return (jnp.dot(hi, b, preferred_element_type=F32) + jnp.dot(mid, b, preferred_element_type=F32)
            + jnp.dot(lo, b, preferred_element_type=F32))


def _mlstm_kernel(qk_ref, vT_ref, oT_ref, smT_ref, conv_ref, bcol_ref, hn_ref,
                  y_ref, xp_scr, st_scr, m_scr, *, L):
    c = pl.program_id(1)
    halo = SUBLANES
    DK, DV = ML_DQK, ML_DV

    @pl.when(c == 0)
    def _():
        xp_scr[0:halo, :] = jnp.zeros((halo, 2 * ML_QK_W), F32)
        st_scr[...] = jnp.zeros_like(st_scr)
        m_scr[...] = jnp.zeros_like(m_scr)

    cur = qk_ref[0]
    xp_scr[halo:halo + L, :] = cur
    base = halo - (ML_CONV - 1)
    acc = conv_ref[0:1, :] * xp_scr[base:base + L, :]
    for j in range(1, ML_CONV):
        acc = acc + conv_ref[j:j + 1, :] * xp_scr[base + j:base + j + L, :]
    xp_scr[0:halo, :] = cur[L - halo:L, :]
    qk = acc * jax.nn.sigmoid(acc)
    qT_all = (qk[:, :ML_QK_W] * (DK ** -0.5)).T.astype(BF16)
    k_all = qk[:, ML_QK_W:].astype(BF16)

    grow = smT_ref[0][0:SUBLANES, :] + bcol_ref[...]
    si = lax.broadcasted_iota(I32, (L, L), 0)
    ji = lax.broadcasted_iota(I32, (L, L), 1)
    causal = si <= ji
    b_row = _dot3(_log_sigmoid(grow), causal)

    for h in range(ML_HEADS):
        qT = qT_all[h * DK:(h + 1) * DK, :]
        kh = k_all[:, h * DK:(h + 1) * DK]
        vT = vT_ref[0, h * DV:(h + 1) * DV, :]
        b_j = b_row[ML_HEADS + h:ML_HEADS + h + 1, :]
        c_row = grow[h:h + 1, :] - b_j
        c_col = jnp.broadcast_to(c_row, (LANES, L)).T
        dlog = jnp.where(causal, b_j + jnp.concatenate([c_col] * (L // LANES), axis=1), NEG_INF)
        m0 = m_scr[h:h + 1, 0:1]
        inter = b_j + m0
        m_t = jnp.maximum(inter, jnp.max(dlog, axis=0, keepdims=True))
        w_inter = jnp.exp(inter - m_t)
        w_intra = jnp.exp(dlog - m_t) * jnp.dot(kh, qT, preferred_element_type=F32)
        st = st_scr[h]
        cq = jnp.dot(st.astype(BF16), qT, preferred_element_type=F32)
        num = w_inter * cq[:DV] + jnp.dot(vT, w_intra.astype(BF16), preferred_element_type=F32)
        den = w_inter * cq[DV:DV + 1] + jnp.sum(w_intra, axis=0, keepdims=True)
        hh = num / jnp.maximum(jnp.abs(den), jnp.exp(-m_t))
        hh = hh * lax.rsqrt(jnp.mean(hh * hh, axis=0, keepdims=True) + EPS)
        rows = slice(h * DV, (h + 1) * DV)
        y_ref[0, :, rows] = (jax.nn.sigmoid(oT_ref[0, rows, :]) * hh * hn_ref[rows, :]).T.astype(y_ref.dtype)

        b_last = b_j[:, L - 1:L]
        a = b_last + c_row
        m_loc = jnp.max(a, axis=1, keepdims=True)
        wa = jnp.exp(a - m_loc)
        lhs = jnp.concatenate([(vT.astype(F32) * wa).astype(BF16),
                               jnp.broadcast_to(wa, (SUBLANES, L)).astype(BF16)], axis=0)
        c_loc = jnp.dot(lhs, kh, preferred_element_type=F32)
        m_new = jnp.maximum(b_last + m0, m_loc)
        s_old = jnp.exp(b_last + m0 - m_new)
        s_loc = jnp.exp(m_loc - m_new)
        st_scr[h] = s_old * st + s_loc * c_loc
        m_scr[h:h + 1, :] = jnp.broadcast_to(m_new, (1, LANES))


def _mlstm(qk, vT, oT, smT, conv_w, i_bias, f_bias, head_norm, B, S):
    L = min(ML_CHUNK, S)
    assert L % LANES == 0
    nc = S // L
    qk = qk.reshape(B, S, 2 * ML_QK_W)
    bcol = jnp.concatenate([i_bias, f_bias]).astype(F32).reshape(2 * ML_HEADS, 1)
    hn = jnp.broadcast_to(head_norm.astype(F32)[:, None], (ML_V_W, L))
    blkT = lambda n: pl.BlockSpec((1, n, L), lambda b, c: (b, 0, c))
    full = lambda a: pl.BlockSpec(a.shape, lambda b, c: (0,) * a.ndim)
    return pl.pallas_call(
        functools.partial(_mlstm_kernel, L=L),
        grid=(B, nc),
        in_specs=[pl.BlockSpec((1, L, 2 * ML_QK_W), lambda b, c: (b, c, 0)),
                  blkT(ML_V_W), blkT(ML_V_W), blkT(16), full(conv_w), full(bcol), full(hn)],
        out_specs=pl.BlockSpec((1, L, ML_V_W), lambda b, c: (b, c, 0)),
        out_shape=jax.ShapeDtypeStruct((B, S, ML_V_W), BF16),
        scratch_shapes=[pltpu.VMEM((L + SUBLANES, 2 * ML_QK_W), F32),
                        pltpu.VMEM((ML_HEADS, ML_DV + SUBLANES, ML_DQK), F32),
                        pltpu.VMEM((SUBLANES, LANES), F32)],
        compiler_params=_params("parallel", "arbitrary"),
    )(qk, vT, oT, smT, conv_w.astype(F32), bcol, hn)


def _block_diag_T(qT, heads, dh):
    rid = lax.broadcasted_iota(I32, qT.shape, 0)
    zero = jnp.zeros_like(qT)
    return jnp.concatenate(
        [jnp.where((rid >= h * dh) & (rid < (h + 1) * dh), qT, zero) for h in range(heads)], axis=1)


def _accumulate(h, p, vT_chunk, l, acc_scr, dh):
    rows = slice(h * dh, (h + 1) * dh)
    acc_scr[rows, :] += jnp.dot(vT_chunk[rows, :], p.astype(BF16), preferred_element_type=F32)
    return l + jnp.sum(p, axis=0, keepdims=True)


def _finish_attention(l, acc_scr, y_ref, heads, dh):
    outs = [acc_scr[h * dh:(h + 1) * dh, :] / l[h] for h in range(heads)]
    y_ref[0] = jnp.concatenate(outs, axis=0).T.astype(y_ref.dtype)


def _max_key_norm2(kb, heads, dh, TQ):
    ch = lax.broadcasted_iota(I32, (heads * dh, LANES), 0) // dh
    hd = lax.broadcasted_iota(I32, (heads * dh, LANES), 1)
    n2 = jnp.dot((kb * kb).astype(BF16), (ch == hd).astype(BF16), preferred_element_type=F32)
    n2 = jnp.max(n2, axis=0, keepdims=True) * (1.0 + 2.0 ** -7)
    return jnp.concatenate([jnp.broadcast_to(n2[:, h:h + 1], (1, TQ)) for h in range(heads)], axis=1)


def _norm_bound(qTf, k2, heads, dh, TQ):
    out = []
    for h in range(heads):
        qh = qTf[h * dh:(h + 1) * dh, :]
        q2 = jnp.sum(qh * qh, axis=0, keepdims=True)
        out.append(jnp.sqrt(q2 * k2[:, h * TQ:(h + 1) * TQ]) * (1.0 + 2.0 ** -8))
    return out


def _underflowed(l):
    small = l[0]
    for lh in l[1:]:
        small = jnp.minimum(small, lh)
    return jnp.logical_not(jnp.min(small) >= SOFTMAX_L_MIN)


def _moba_kernel(qT_ref, k_ref, vT_ref, y_ref, km_scr, kn_scr, acc_scr, s_scr, *, NB, NBP, n_sel):
    H, DH, BS = MOBA_HEADS, MOBA_DH, MOBA_BLOCK
    TQ = BS
    qi = pl.program_id(1)

    @pl.when(qi == 0)
    def _():
        km_scr[...] = jnp.zeros_like(km_scr)
        kn_scr[...] = jnp.zeros_like(kn_scr)

        def body(j, carry):
            kb = k_ref[0, pl.ds(pl.multiple_of(j * BS, BS), BS), :].astype(F32)
            km_scr[pl.ds(j, 1), :] = jnp.mean(kb, axis=0, keepdims=True)
            kn_scr[pl.ds(j, 1), :] = _max_key_norm2(kb, H, DH, TQ)
            return carry
        lax.fori_loop(0, NB, body, 0)

    qT = qT_ref[0]
    qTf = qT.astype(F32)
    km = km_scr[...]
    gate = jnp.concatenate(
        [jnp.dot(km[:, h * DH:(h + 1) * DH], qTf[h * DH:(h + 1) * DH, :], precision=lax.Precision.HIGHEST,
                 preferred_element_type=F32) for h in range(H)], axis=1) * (1.0 / ATT_QSCALE)
    blk = lax.broadcasted_iota(I32, gate.shape, 0)
    g = jnp.where(blk < qi, gate, NEG_INF)
    sels = []
    for _ in range(n_sel):
        mx = jnp.max(g, axis=0, keepdims=True)
        isel = jnp.min(jnp.where(g == mx, blk, NBP), axis=0, keepdims=True)
        sels.append(jnp.where(mx > NEG_INF, isel, -1))
        g = jnp.where(blk == isel, NEG_INF, g)

    qs = _block_diag_T(qT, H, DH)

    def hit_row(j):
        hit = sels[0] == j
        for s in sels[1:]:
            hit = hit | (s == j)
        return hit

    blk_start = lambda j: pl.multiple_of(j * BS, BS)
    blk_scores = lambda j: jnp.dot(k_ref[0, pl.ds(blk_start(j), BS), :], qs, preferred_element_type=F32)
    hcols = lambda row, h: row[:, h * TQ:(h + 1) * TQ]
    causal = lax.broadcasted_iota(I32, (BS, TQ), 0) <= lax.broadcasted_iota(I32, (BS, TQ), 1)

    def diag_scores():
        s_all = blk_scores(qi)
        return [jnp.where(causal, hcols(s_all, h), NEG_INF) for h in range(H)]

    def max_sweep():
        def max_block(j, m):
            return jnp.maximum(m, jnp.where(hit_row(j), jnp.max(blk_scores(j), axis=0, keepdims=True), NEG_INF))
        m_all = lax.fori_loop(0, qi, max_block, jnp.full((1, H * TQ), NEG_INF, F32))
        s_diag = diag_scores()
        return [jnp.maximum(hcols(m_all, h), jnp.max(s_diag[h], axis=0, keepdims=True)) for h in range(H)]

    def sum_sweep(m):
        acc_scr[...] = jnp.zeros_like(acc_scr)

        def sum_group(width):
            def body(g, l):
                for u in range(width):
                    s_scr[u] = blk_scores(width * g + u)
                for u in range(width):
                    j = width * g + u
                    hit = hit_row(j)
                    vT_blk = vT_ref[0, :, pl.ds(blk_start(j), BS)]
                    l = tuple(_accumulate(h, jnp.exp2(s_scr[u, :, h * TQ:(h + 1) * TQ]
                                                      - jnp.where(hcols(hit, h), m[h], jnp.inf)),
                                          vT_blk, l[h], acc_scr, DH) for h in range(H))
                return l
            return body
        l = _chunk_loops(qi, ATT_STAGES, sum_group, (jnp.zeros((1, TQ), F32),) * H)
        s_diag = diag_scores()
        vT_diag = vT_ref[0, :, pl.ds(blk_start(qi), BS)]
        return [_accumulate(h, jnp.exp2(s_diag[h] - m[h]), vT_diag, l[h], acc_scr, DH) for h in range(H)]

    blk_n = lax.broadcasted_iota(I32, kn_scr.shape, 0)
    k2 = jnp.max(jnp.where(blk_n <= qi, kn_scr[...], 0.0), axis=0, keepdims=True)
    l = sum_sweep(_norm_bound(qTf, k2, H, DH, TQ))
    _finish_attention(l, acc_scr, y_ref, H, DH)

    @pl.when(_underflowed(l))
    def _():
        _finish_attention(sum_sweep(max_sweep()), acc_scr, y_ref, H, DH)


def _moba(qT, k, vT, B, S):
    BS = MOBA_BLOCK
    assert S % BS == 0
    NB = S // BS
    NBP = -(-NB // SUBLANES) * SUBLANES
    n_sel = max(min(MOBA_TOPK, NB - 1), 1)
    k = k.reshape(B, S, MOBA_W)
    return pl.pallas_call(
        functools.partial(_moba_kernel, NB=NB, NBP=NBP, n_sel=n_sel),
        grid=(B, NB),
        in_specs=[pl.BlockSpec((1, MOBA_W, BS), lambda b, i: (b, 0, i)),
                  pl.BlockSpec((1, S, MOBA_W), lambda b, i: (b, 0, 0)),
                  pl.BlockSpec((1, MOBA_W, S), lambda b, i: (b, 0, 0))],
        out_specs=pl.BlockSpec((1, BS, MOBA_W), lambda b, i: (b, i, 0)),
        out_shape=jax.ShapeDtypeStruct((B, S, MOBA_W), BF16),
        scratch_shapes=[pltpu.VMEM((NBP, MOBA_W), F32),
                        pltpu.VMEM((NBP, MOBA_HEADS * BS), F32),
                        pltpu.VMEM((MOBA_W, BS), F32),
                        pltpu.VMEM((ATT_STAGE, BS, MOBA_HEADS * BS), F32)],
        compiler_params=_params("parallel", "arbitrary"),
    )(qT, k, vT)


def _dsa_kernel(qT_ref, k_ref, vT_ref, iqT_ref, ik_ref, smT_ref, y_ref,
                e_scr, gmax_scr, acc_scr, thr_scr, kn_scr, s_scr, *, TQ, topk, spow, n_chunks):
    H, DH = DSA_HEADS, DSA_DH
    KC = TQ
    qi = pl.program_id(1)
    nch = qi + 1
    chunk_start = lambda c: pl.multiple_of(c * KC, KC)

    @pl.when(qi == 0)
    def _():
        kn_scr[...] = jnp.zeros_like(kn_scr)

        def body(c, carry):
            kn_scr[pl.ds(c, 1), :] = _max_key_norm2(k_ref[0, pl.ds(chunk_start(c), KC), :].astype(F32), H, DH, TQ)
            return carry
        lax.fori_loop(0, n_chunks, body, 0)
    sum8 = lambda b: jnp.sum(b.astype(I32).reshape(KC // SUBLANES, SUBLANES, TQ), axis=0)
    any_row = lambda r: jnp.max(r.astype(I32)) > 0

    iqT = iqT_ref[0]
    iq_cat = jnp.concatenate([iqT[h * IDX_DH:(h + 1) * IDX_DH, :] for h in range(IDX_HEADS)], axis=1)
    iw = smT_ref[0][2 * ML_HEADS:2 * ML_HEADS + IDX_HEADS, :] * (IDX_HEADS ** -0.5 * IDX_DH ** -0.5)
    t_pos = qi * TQ + lax.broadcasted_iota(I32, (KC, TQ), 1)
    s_off = lax.broadcasted_iota(I32, (KC, TQ), 0)
    gmax_scr[...] = jnp.full((KC, TQ), NEG_INF, F32)

    def score_group(width, masked=False):
        def body(g, tally):
            for u in range(width):
                ikc = ik_ref[0, pl.ds(chunk_start(width * g + u), KC), :][:, :IDX_DH]
                s_scr[u] = jnp.dot(ikc, iq_cat, preferred_element_type=F32)
            for u in range(width):
                c = width * g + u
                sc = iw[0:1, :] * jnp.maximum(s_scr[u, :, 0:TQ], 0.0)
                for h in range(1, IDX_HEADS):
                    sc = sc + iw[h:h + 1, :] * jnp.maximum(s_scr[u, :, h * TQ:(h + 1) * TQ], 0.0)
                if masked:
                    sc = jnp.where(c * KC + s_off <= t_pos, sc, NEG_INF)
                e_scr[pl.ds(chunk_start(c), KC), :] = sc
                gmax_scr[...] = jnp.maximum(gmax_scr[...], sc)
                tally = tally + jnp.sum(jnp.where(sc > 0.0, 1, jnp.where(sc == 0.0, 1 << 16, 0))
                                        .reshape(KC // SUBLANES, SUBLANES, TQ), axis=0)
            return tally
        return body
    assert n_chunks * KC < (1 << 16)
    zero8 = jnp.zeros((SUBLANES, TQ), I32)
    tally = _chunk_loops(qi, (4, 2, 1), score_group, zero8)
    tally = jnp.sum(score_group(1, True)(qi, tally), axis=0, keepdims=True)
    n_pos = tally & 0xFFFF
    n_nonneg = n_pos + (tally >> 16)

    def sweep_keys(fold, merge, init):
        def group(width):
            def body(g, acc):
                return tuple(merge(acc[u], fold(e_scr[pl.ds(chunk_start(width * g + u), KC), :]))
                             for u in range(width))
            return body

        def regroup(acc, width):
            half = len(acc) // 2
            acc = tuple(merge(acc[u], acc[u + half]) for u in range(half))
            return acc if half == width else regroup(acc, width)
        return _chunk_loops(nch, (4, 2, 1), group, (init,) * 4, regroup)[0]

    def count_gt(cand):
        return jnp.sum(sweep_keys(lambda x: sum8(x > cand), lambda p, q: p + q, zero8), axis=0, keepdims=True)

    def min_gt(cand):
        fold = lambda x: jnp.min(jnp.where(x > cand, x, jnp.inf).reshape(KC // SUBLANES, SUBLANES, TQ), axis=0)
        return jnp.min(sweep_keys(fold, jnp.minimum, jnp.full((SUBLANES, TQ), jnp.inf, F32)), axis=0, keepdims=True)

    def rewrite(rows, thr):
        def body(c, carry):
            x = e_scr[pl.ds(chunk_start(c), KC), :]
            rank = (2 * spow - (c * KC + s_off)).astype(F32)
            new = jnp.where(x > thr, jnp.inf, jnp.where(x == thr, rank, NEG_INF))
            e_scr[pl.ds(chunk_start(c), KC), :] = jnp.where(rows, new, x)
            return carry
        lax.fori_loop(0, nch, body, 0)

    def halve(_, c):
        lo, hi, act, tie, n_lo = c
        mid = lo + (hi - lo) * 0.5
        inside = (mid > lo) & (mid < hi)
        cnt = count_gt(mid)
        live = act > 0.0
        run = live & inside
        up = run & (cnt >= topk)
        lo = jnp.where(up, mid, lo)
        n_lo = jnp.where(up, cnt, n_lo)
        hi = jnp.where(run & (cnt < topk), mid, hi)
        collapsed = live & jnp.logical_not(inside)
        tie = jnp.where(collapsed, 1.0, tie)
        act = jnp.where(collapsed | (run & (cnt == topk)), 0.0, act)
        return lo, hi, act, tie, n_lo

    def excess(state):
        return jnp.max(jnp.where(state[2] > 0.0, state[4] - topk, 0).astype(F32))

    def bisect(lo, hi, act, warmup):
        state = (lo, hi, act, jnp.zeros_like(act), jnp.full((1, TQ), 1 << 20, I32))
        state = lax.fori_loop(0, warmup, halve, state)

        def one_more(c):
            state = halve(0, c[1])
            return c[0] + 1, state, excess(state)
        near = lax.while_loop(lambda c: jnp.logical_and(c[0] < BISECT_CAP, c[2] > BISECT_DROP), one_more,
                              (jnp.int32(warmup), state, excess(state)))
        lo, hi, act, tie, n_lo = near[1]
        cur, left = lo, jnp.where(act > 0.0, n_lo - topk, 0)
        for _ in range(BISECT_DROP):
            cur = jnp.where(left > 0, min_gt(cur), cur)
            left = left - 1
        done = (act > 0.0) & (count_gt(cur) == topk)
        lo = jnp.where(done, cur, lo)
        act = jnp.where(done, 0.0, act)

        def body(c):
            state = c[1]
            for _ in range(BISECT_STEPS_PER_CHECK):
                state = halve(0, state)
            return c[0] + BISECT_STEPS_PER_CHECK, state, jnp.max(state[2])
        out = lax.while_loop(lambda c: jnp.logical_and(c[0] < BISECT_CAP, c[2] > 0.0), body,
                             (near[0], (lo, hi, act, tie, n_lo), jnp.max(act)))
        return out[1][0], out[1][1], out[1][3]

    fmax = float(jnp.finfo(F32).max)
    gm = gmax_scr[...]
    g_lo = jnp.min(gm, axis=0, keepdims=True)
    g_hi = jnp.max(gm, axis=0, keepdims=True)
    n_vis = qi * TQ + lax.broadcasted_iota(I32, (1, TQ), 1) + 1
    few = n_vis <= topk
    zero_tie = (n_pos < topk) & (n_nonneg >= topk) & jnp.logical_not(few)

    @pl.when(any_row(zero_tie))
    def _():
        rewrite(zero_tie, jnp.zeros((1, TQ), F32))

    rank_lo, rank_hi = float(spow), float(2 * spow + 1)
    below = jnp.maximum(g_lo, -fmax)
    below = below - jnp.abs(below) * (2.0 ** -10) - 1e-30
    lo0 = jnp.where(zero_tie, rank_lo, jnp.maximum(below, -fmax))
    hi0 = jnp.where(zero_tie, rank_hi, g_hi)
    lo, hi, tie = bisect(lo0, hi0, jnp.where(few, 0.0, 1.0), BISECT_WARMUP)
    thr_scr[0:1, :] = jnp.where(few, NEG_INF, lo)

    @pl.when(jnp.max(tie) > 0.0)
    def _():
        rows = tie > 0.0
        rewrite(rows, hi)
        lo2, _, _ = bisect(jnp.full((1, TQ), rank_lo, F32), jnp.full((1, TQ), rank_hi, F32), tie, 0)
        thr_scr[0:1, :] = jnp.where(rows, lo2, thr_scr[0:1, :])

    thr = thr_scr[0:1, :]
    qT = qT_ref[0]
    qs = _block_diag_T(qT, H, DH)
    hcols = lambda row, h: row[:, h * TQ:(h + 1) * TQ]

    def chunk_scores(c):
        start = chunk_start(c)
        sel = e_scr[pl.ds(start, KC), :] > thr
        s_all = jnp.dot(k_ref[0, pl.ds(start, KC), :], qs, preferred_element_type=F32)
        return sel, s_all

    def max_sweep():
        def max_chunk(c, m):
            sel, s_all = chunk_scores(c)
            return tuple(jnp.maximum(m[h], jnp.max(jnp.where(sel, hcols(s_all, h), NEG_INF), axis=0, keepdims=True))
                         for h in range(H))
        m = lax.fori_loop(0, nch, max_chunk, (jnp.full((1, TQ), NEG_INF, F32),) * H)
        return [jnp.where(mh == NEG_INF, 0.0, mh) for mh in m]

    def sum_sweep(m):
        acc_scr[...] = jnp.zeros_like(acc_scr)

        def sum_group(width):
            def body(g, l):
                for u in range(width):
                    s_scr[u] = jnp.dot(k_ref[0, pl.ds(chunk_start(width * g + u), KC), :], qs,
                                       preferred_element_type=F32)
                for u in range(width):
                    start = chunk_start(width * g + u)
                    sel = e_scr[pl.ds(start, KC), :] > thr
                    vT_chunk = vT_ref[0, :, pl.ds(start, KC)]
                    l = tuple(_accumulate(h, jnp.exp2(s_scr[u, :, h * TQ:(h + 1) * TQ] + jnp.where(sel, -m[h], NEG_INF)),
                                          vT_chunk, l[h], acc_scr, DH) for h in range(H))
                return l
            return body
        return _chunk_loops(nch, ATT_STAGES, sum_group, (jnp.zeros((1, TQ), F32),) * H)

    chunk_n = lax.broadcasted_iota(I32, kn_scr.shape, 0)
    k2 = jnp.max(jnp.where(chunk_n < nch, kn_scr[...], 0.0), axis=0, keepdims=True)
    l = sum_sweep(_norm_bound(qT.astype(F32), k2, H, DH, TQ))
    _finish_attention(l, acc_scr, y_ref, H, DH)

    @pl.when(_underflowed(l))
    def _():
        _finish_attention(sum_sweep(max_sweep()), acc_scr, y_ref, H, DH)


def _dsa(qT, k, vT, iqT, ik, smT, B, S):
    TQ = min(ATT_TQ, S)
    topk = min(DSA_TOPK_MAX, S // 4)
    assert S % TQ == 0 and topk <= TQ
    spow = 1 << max((S - 1).bit_length(), 1)
    k = k.reshape(B, S, DSA_W)
    ik = ik.reshape(B, S, LANES)
    qblk = lambda n: pl.BlockSpec((1, n, TQ), lambda b, i: (b, 0, i))
    return pl.pallas_call(
        functools.partial(_dsa_kernel, TQ=TQ, topk=topk, spow=spow, n_chunks=S // TQ),
        grid=(B, S // TQ),
        in_specs=[qblk(DSA_W),
                  pl.BlockSpec((1, S, DSA_W), lambda b, i: (b, 0, 0)),
                  pl.BlockSpec((1, DSA_W, S), lambda b, i: (b, 0, 0)),
                  qblk(IDX_HEADS * IDX_DH),
                  pl.BlockSpec((1, S, LANES), lambda b, i: (b, 0, 0)),
                  qblk(16)],
        out_specs=pl.BlockSpec((1, TQ, DSA_W), lambda b, i: (b, i, 0)),
        out_shape=jax.ShapeDtypeStruct((B, S, DSA_W), BF16),
        scratch_shapes=[pltpu.VMEM((S, TQ), F32),
                        pltpu.VMEM((TQ, TQ), F32),
                        pltpu.VMEM((DSA_W, TQ), F32),
                        pltpu.VMEM((SUBLANES, TQ), F32),
                        pltpu.VMEM((-(-(S // TQ) // SUBLANES) * SUBLANES, DSA_HEADS * TQ), F32),
                        pltpu.VMEM((ATT_STAGE, TQ, DSA_HEADS * TQ), F32)],
        compiler_params=_params("parallel", "arbitrary"),
    )(qT, k, vT, iqT, ik, smT)


def _out_mlp_kernel(yml_ref, ymb_ref, yds_ref, x_ref, wout_ref, gpost_ref, gpre_ref,
                    w1_ref, w2_ref, g2_ref, out_ref, x1_scr, h_scr, acc_scr):
    f = pl.program_id(1)

    @pl.when(f == 0)
    def _():
        slab = x_ref.shape[0] // MLP_SLABS
        for r in range(MLP_SLABS):
            rows = slice(r * slab, (r + 1) * slab)
            mix = jnp.dot(yml_ref[rows, :], wout_ref[0:ML_V_W, :], preferred_element_type=F32)
            mix = mix + jnp.dot(ymb_ref[rows, :], wout_ref[ML_V_W:ML_V_W + MOBA_W, :], preferred_element_type=F32)
            mix = mix + jnp.dot(yds_ref[rows, :], wout_ref[ML_V_W + MOBA_W:, :], preferred_element_type=F32)
            x1 = x_ref[rows, :] + _rms(mix, gpost_ref[...])
            x1_scr[rows, :] = x1
            h_scr[rows, :] = _rms(x1, gpre_ref[...]).astype(BF16)
        acc_scr[...] = jnp.zeros_like(acc_scr)

    u = jnp.maximum(jnp.dot(h_scr[...], w1_ref[...], preferred_element_type=F32), 0.0)
    acc_scr[...] += jnp.dot((u * u).astype(BF16), w2_ref[...], preferred_element_type=F32)

    @pl.when(f == pl.num_programs(1) - 1)
    def _():
        out_ref[...] = x1_scr[...] + _rms(acc_scr[...], g2_ref[...])


def _out_mlp(yml, ymb, yds, x2, w_out, g_post, g_pre, w1, w2, g2):
    T, D = x2.shape
    tm = min(MLP_TM, T)
    tf = min(MLP_TF, D_FF)
    row = lambda n: pl.BlockSpec((tm, n), lambda i, f: (i, 0))
    full = lambda a: pl.BlockSpec(a.shape, lambda i, f: (0,) * a.ndim)
    return pl.pallas_call(
        _out_mlp_kernel,
        grid=(T // tm, D_FF // tf),
        in_specs=[row(ML_V_W), row(MOBA_W), row(DSA_W), row(D), full(w_out), full(g_post), full(g_pre),
                  pl.BlockSpec((D, tf), lambda i, f: (0, f)),
                  pl.BlockSpec((tf, D), lambda i, f: (f, 0)),
                  full(g2)],
        out_specs=row(D),
        out_shape=jax.ShapeDtypeStruct((T, D), F32),
        scratch_shapes=[pltpu.VMEM((tm, D), F32), pltpu.VMEM((tm, D), BF16), pltpu.VMEM((tm, D), F32)],
        compiler_params=_params("parallel", "arbitrary"),
    )(yml, ymb, yds, x2, w_out, g_post, g_pre, w1, w2, g2)


def kernel(x, norm_mix_pre, w_in, ml_conv, ml_i_bias, ml_f_bias, ml_head_norm, w_out, norm_mix_post,
           norm_mlp_pre, w_ff1, w_ff2, norm_mlp_post):
    B, S, D = x.shape
    depth = w_in.shape[0]
    x2 = x.reshape(B * S, D)
    gain = lambda g: g.reshape(1, D).astype(F32)
    for l in range(depth):
        wn, wt = _prep_in_weights(w_in[l])
        (qk, mbk, dsk, ixk, mbqT, mbvT, dsqT, dsvT, ixqT, mlvT, mloT, smT) = _in_proj(
            x2, gain(norm_mix_pre[l]), wn, wt, B, S)
        y_ml = _mlstm(qk, mlvT, mloT, smT, ml_conv[l], ml_i_bias[l], ml_f_bias[l], ml_head_norm[l], B, S)
        y_mb = _moba(mbqT, mbk, mbvT, B, S)
        y_ds = _dsa(dsqT, dsk, dsvT, ixqT, ixk, smT, B, S)
        x2 = _out_mlp(y_ml.reshape(B * S, ML_V_W), y_mb.reshape(B * S, MOBA_W), y_ds.reshape(B * S, DSA_W),
                      x2, w_out[l].astype(BF16), gain(norm_mix_post[l]), gain(norm_mlp_pre[l]),
                      w_ff1[l].astype(BF16), w_ff2[l].astype(BF16), gain(norm_mlp_post[l]))
    return x2.reshape(B, S, D)
```

```python
import functools

import jax
import jax.numpy as jnp
from jax import lax
from jax.experimental import pallas as pl
from jax.experimental.pallas import tpu as pltpu

F32 = jnp.float32
BF16 = jnp.bfloat16
I32 = jnp.int32

D_MODEL = 1024
ML_HEADS, ML_DQK, ML_DV, ML_CONV = 4, 64, 128, 4
MOBA_HEADS, MOBA_DH, MOBA_BLOCK, MOBA_TOPK = 4, 64, 256, 3
DSA_HEADS, DSA_DH, IDX_HEADS, IDX_DH, DSA_TOPK_MAX = 4, 64, 4, 64, 256
D_FF = 4 * D_MODEL
EPS = 1e-6

ML_QK_W = ML_HEADS * ML_DQK
ML_V_W = ML_HEADS * ML_DV
MOBA_W = MOBA_HEADS * MOBA_DH
DSA_W = DSA_HEADS * DSA_DH
IN_SPLITS = (ML_QK_W, ML_QK_W, ML_V_W, ML_V_W, ML_HEADS, ML_HEADS,
             MOBA_W, MOBA_W, MOBA_W,
             DSA_W, DSA_W, DSA_W, IDX_HEADS * IDX_DH, IDX_DH, IDX_HEADS)

LANES = 128
SUBLANES = 8
VMEM_LIMIT = 52 * 1024 * 1024

ML_CHUNK = 512
ATT_TQ = 256
ATT_STAGES = (8, 4, 2, 1)
ATT_STAGE = ATT_STAGES[0]
PROJ_TM = 1024
MLP_TM = 1024
MLP_TF = 512
MLP_SLABS = 4

NEG_INF = float("-inf")
BISECT_CAP = 300
BISECT_WARMUP = 12
BISECT_DROP = 2
BISECT_STEPS_PER_CHECK = 2
SOFTMAX_L_MIN = 1e-30
assert MOBA_DH == DSA_DH
ATT_QSCALE = MOBA_DH ** -0.5 * 1.4426950408889634

_NT = (((1,), (1,)), ((), ()))


def _chunk_loops(n, widths, make_body, carry, regroup=lambda c, w: c):
    covered = 0
    for i, w in enumerate(widths):
        if i:
            carry = regroup(carry, w)
        carry = lax.fori_loop(covered // w, n // w, make_body(w), carry)
        covered = w * (n // w)
    return carry


def _rms(x, g):
    return x * lax.rsqrt(jnp.mean(x * x, axis=-1, keepdims=True) + EPS) * g


def _log_sigmoid(x):
    return jnp.minimum(x, 0.0) - jnp.log1p(jnp.exp(-jnp.abs(x)))


def _params(*sem):
    return pltpu.CompilerParams(dimension_semantics=sem, vmem_limit_bytes=VMEM_LIMIT)


_N_QK, _N_MBK, _N_DSK, _N_IXK = 0, 512, 768, 1024
_N_TOTAL = 1152
_T_MBQ, _T_MBV, _T_DSQ, _T_DSV, _T_IXQ, _T_MLV, _T_MLO, _T_SM = 0, 256, 512, 768, 1024, 1280, 1792, 2304
_T_TOTAL = 2320


def _in_proj_kernel(x_ref, g_ref, wn_ref, wt_ref,
                    qk_ref, mbk_ref, dsk_ref, ixk_ref,
                    mbqT_ref, mbvT_ref, dsqT_ref, dsvT_ref, ixqT_ref, mlvT_ref, mloT_ref, smT_ref):
    h = _rms(x_ref[...], g_ref[...]).astype(BF16)

    def mm(a, n):
        return jnp.dot(h, wn_ref[:, a:a + n], preferred_element_type=F32)

    def mt(a, n):
        return lax.dot_general(wt_ref[a:a + n, :], h, _NT, preferred_element_type=F32)

    qk_ref[...] = mm(_N_QK, 512)
    mbk_ref[...] = mm(_N_MBK, 256).astype(BF16)
    dsk_ref[...] = mm(_N_DSK, 256).astype(BF16)
    ixk_ref[...] = mm(_N_IXK, 128).astype(BF16)
    mbqT_ref[0] = (mt(_T_MBQ, 256) * ATT_QSCALE).astype(BF16)
    mbvT_ref[0] = mt(_T_MBV, 256).astype(BF16)
    dsqT_ref[0] = (mt(_T_DSQ, 256) * ATT_QSCALE).astype(BF16)
    dsvT_ref[0] = mt(_T_DSV, 256).astype(BF16)
    ixqT_ref[0] = mt(_T_IXQ, 256).astype(BF16)
    mlvT_ref[0] = mt(_T_MLV, 512).astype(BF16)
    mloT_ref[0] = mt(_T_MLO, 512)
    smT_ref[0] = mt(_T_SM, 16)


def _prep_in_weights(w_in):
    pts, acc = [], 0
    for n in IN_SPLITS:
        pts.append((acc, acc + n))
        acc += n
    col = lambda i: w_in[:, pts[i][0]:pts[i][1]]
    (ml_q, ml_k, ml_v, ml_o, ml_i, ml_f, mb_q, mb_k, mb_v,
     ds_q, ds_k, ds_v, ix_q, ix_k, ix_w) = [col(i) for i in range(len(IN_SPLITS))]
    d = w_in.shape[0]
    small = jnp.concatenate([ml_i, ml_f, ix_w], axis=1)
    wn = jnp.concatenate([
        ml_q, ml_k, mb_k, ds_k,
        ix_k, jnp.zeros((d, 128 - IDX_DH), w_in.dtype)], axis=1)
    wt = jnp.concatenate([
        mb_q, mb_v, ds_q, ds_v, ix_q, ml_v, ml_o,
        small, jnp.zeros((d, 16 - small.shape[1]), w_in.dtype)], axis=1).T
    assert wn.shape == (d, _N_TOTAL) and wt.shape == (_T_TOTAL, d)
    return wn.astype(BF16), wt.astype(BF16)


def _in_proj(x2, g, wn, wt, B, S):
    T, D = x2.shape
    tm = min(PROJ_TM, S)
    nsb = S // tm
    row = lambda n: pl.BlockSpec((tm, n), lambda i: (i, 0))
    tr = lambda n: pl.BlockSpec((1, n, tm), lambda i: (i // nsb, 0, i % nsb))
    full = lambda a: pl.BlockSpec(a.shape, lambda i: (0,) * a.ndim)
    out_shape = (
        jax.ShapeDtypeStruct((T, 512), F32),
        jax.ShapeDtypeStruct((T, 256), BF16),
        jax.ShapeDtypeStruct((T, 256), BF16),
        jax.ShapeDtypeStruct((T, 128), BF16),
        jax.ShapeDtypeStruct((B, 256, S), BF16),
        jax.ShapeDtypeStruct((B, 256, S), BF16),
        jax.ShapeDtypeStruct((B, 256, S), BF16),
        jax.ShapeDtypeStruct((B, 256, S), BF16),
        jax.ShapeDtypeStruct((B, 256, S), BF16),
        jax.ShapeDtypeStruct((B, 512, S), BF16),
        jax.ShapeDtypeStruct((B, 512, S), F32),
        jax.ShapeDtypeStruct((B, 16, S), F32),
    )
    out_specs = (row(512), row(256), row(256), row(128),
                 tr(256), tr(256), tr(256), tr(256), tr(256), tr(512), tr(512), tr(16))
    return pl.pallas_call(
        _in_proj_kernel,
        grid=(T // tm,),
        in_specs=[row(D), full(g), full(wn), full(wt)],
        out_specs=out_specs,
        out_shape=out_shape,
        compiler_params=_params("parallel"),
    )(x2, g, wn, wt)


def _dot3(a_f32, b_exact):
    b = b_exact.astype(BF16)
    hi = a_f32.astype(BF16)
    r1 = a_f32 - hi.astype(F32)
    mid = r1.astype(BF16)
    lo = (r1 - mid.astype(F32)).astype(BF16)
    return (jnp.dot(hi, b, preferred_element_type=F32) + jnp.dot(mid, b, preferred_element_type=F32)
            + jnp.dot(lo, b, preferred_element_type=F32))


def _mlstm_kernel(qk_ref, vT_ref, oT_ref, smT_ref, conv_ref, bcol_ref, hn_ref,
                  y_ref, xp_scr, st_scr, m_scr, *, L):
    c = pl.program_id(1)
    halo = SUBLANES
    DK, DV = ML_DQK, ML_DV

    @pl.when(c == 0)
    def _():
        xp_scr[0:halo, :] = jnp.zeros((halo, 2 * ML_QK_W), F32)
        st_scr[...] = jnp.zeros_like(st_scr)
        m_scr[...] = jnp.zeros_like(m_scr)

    cur = qk_ref[0]
    xp_scr[halo:halo + L, :] = cur
    base = halo - (ML_CONV - 1)
    acc = conv_ref[0:1, :] * xp_scr[base:base + L, :]
    for j in range(1, ML_CONV):
        acc = acc + conv_ref[j:j + 1, :] * xp_scr[base + j:base + j + L, :]
    xp_scr[0:halo, :] = cur[L - halo:L, :]
    qk = acc * jax.nn.sigmoid(acc)
    qT_all = (qk[:, :ML_QK_W] * (DK ** -0.5)).T.astype(BF16)
    k_all = qk[:, ML_QK_W:].astype(BF16)

    grow = smT_ref[0][0:SUBLANES, :] + bcol_ref[...]
    si = lax.broadcasted_iota(I32, (L, L), 0)
    ji = lax.broadcasted_iota(I32, (L, L), 1)
    causal = si <= ji
    b_row = _dot3(_log_sigmoid(grow), causal)

    for h in range(ML_HEADS):
        qT = qT_all[h * DK:(h + 1) * DK, :]
        kh = k_all[:, h * DK:(h + 1) * DK]
        vT = vT_ref[0, h * DV:(h + 1) * DV, :]
        b_j = b_row[ML_HEADS + h:ML_HEADS + h + 1, :]
        c_row = grow[h:h + 1, :] - b_j
        c_col = jnp.broadcast_to(c_row, (LANES, L)).T
        dlog = jnp.where(causal, b_j + jnp.concatenate([c_col] * (L // LANES), axis=1), NEG_INF)
        m0 = m_scr[h:h + 1, 0:1]
        inter = b_j + m0
        m_t = jnp.maximum(inter, jnp.max(dlog, axis=0, keepdims=True))
        w_inter = jnp.exp(inter - m_t)
        w_intra = jnp.exp(dlog - m_t) * jnp.dot(kh, qT, preferred_element_type=F32)
        st = st_scr[h]
        cq = jnp.dot(st.astype(BF16), qT, preferred_element_type=F32)
        num = w_inter * cq[:DV] + jnp.dot(vT, w_intra.astype(BF16), preferred_element_type=F32)
        den = w_inter * cq[DV:DV + 1] + jnp.sum(w_intra, axis=0, keepdims=True)
        hh = num / jnp.maximum(jnp.abs(den), jnp.exp(-m_t))
        hh = hh * lax.rsqrt(jnp.mean(hh * hh, axis=0, keepdims=True) + EPS)
        rows = slice(h * DV, (h + 1) * DV)
        y_ref[0, :, rows] = (jax.nn.sigmoid(oT_ref[0, rows, :]) * hh * hn_ref[rows, :]).T.astype(y_ref.dtype)

        b_last = b_j[:, L - 1:L]
        a = b_last + c_row
        m_loc = jnp.max(a, axis=1, keepdims=True)
        wa = jnp.exp(a - m_loc)
        lhs = jnp.concatenate([(vT.astype(F32) * wa).astype(BF16),
                               jnp.broadcast_to(wa, (SUBLANES, L)).astype(BF16)], axis=0)
        c_loc = jnp.dot(lhs, kh, preferred_element_type=F32)
        m_new = jnp.maximum(b_last + m0, m_loc)
        s_old = jnp.exp(b_last + m0 - m_new)
        s_loc = jnp.exp(m_loc - m_new)
        st_scr[h] = s_old * st + s_loc * c_loc
        m_scr[h:h + 1, :] = jnp.broadcast_to(m_new, (1, LANES))


def _mlstm(qk, vT, oT, smT, conv_w, i_bias, f_bias, head_norm, B, S):
    L = min(ML_CHUNK, S)
    assert L % LANES == 0
    nc = S // L
    qk = qk.reshape(B, S, 2 * ML_QK_W)
    bcol = jnp.concatenate([i_bias, f_bias]).astype(F32).reshape(2 * ML_HEADS, 1)
    hn = jnp.broadcast_to(head_norm.astype(F32)[:, None], (ML_V_W, L))
    blkT = lambda n: pl.BlockSpec((1, n, L), lambda b, c: (b, 0, c))
    full = lambda a: pl.BlockSpec(a.shape, lambda b, c: (0,) * a.ndim)
    return pl.pallas_call(
        functools.partial(_mlstm_kernel, L=L),
        grid=(B, nc),
        in_specs=[pl.BlockSpec((1, L, 2 * ML_QK_W), lambda b, c: (b, c, 0)),
                  blkT(ML_V_W), blkT(ML_V_W), blkT(16), full(conv_w), full(bcol), full(hn)],
        out_specs=pl.BlockSpec((1, L, ML_V_W), lambda b, c: (b, c, 0)),
        out_shape=jax.ShapeDtypeStruct((B, S, ML_V_W), BF16),
        scratch_shapes=[pltpu.VMEM((L + SUBLANES, 2 * ML_QK_W), F32),
                        pltpu.VMEM((ML_HEADS, ML_DV + SUBLANES, ML_DQK), F32),
                        pltpu.VMEM((SUBLANES, LANES), F32)],
        compiler_params=_params("parallel", "arbitrary"),
    )(qk, vT, oT, smT, conv_w.astype(F32), bcol, hn)


def _block_diag_T(qT, heads, dh):
    rid = lax.broadcasted_iota(I32, qT.shape, 0)
    zero = jnp.zeros_like(qT)
    return jnp.concatenate(
        [jnp.where((rid >= h * dh) & (rid < (h + 1) * dh), qT, zero) for h in range(heads)], axis=1)


def _accumulate(h, p, vT_chunk, l, acc_scr, dh):
    rows = slice(h * dh, (h + 1) * dh)
    acc_scr[rows, :] += jnp.dot(vT_chunk[rows, :], p.astype(BF16), preferred_element_type=F32)
    return l + jnp.sum(p, axis=0, keepdims=True)


def _finish_attention(l, acc_scr, y_ref, heads, dh):
    outs = [acc_scr[h * dh:(h + 1) * dh, :] / l[h] for h in range(heads)]
    y_ref[0] = jnp.concatenate(outs, axis=0).T.astype(y_ref.dtype)


def _max_key_norm2(kb, heads, dh, TQ):
    ch = lax.broadcasted_iota(I32, (heads * dh, LANES), 0) // dh
    hd = lax.broadcasted_iota(I32, (heads * dh, LANES), 1)
    n2 = jnp.dot((kb * kb).astype(BF16), (ch == hd).astype(BF16), preferred_element_type=F32)
    n2 = jnp.max(n2, axis=0, keepdims=True) * (1.0 + 2.0 ** -7)
    return jnp.concatenate([jnp.broadcast_to(n2[:, h:h + 1], (1, TQ)) for h in range(heads)], axis=1)


def _norm_bound(qTf, k2, heads, dh, TQ):
    out = []
    for h in range(heads):
        qh = qTf[h * dh:(h + 1) * dh, :]
        q2 = jnp.sum(qh * qh, axis=0, keepdims=True)
        out.append(jnp.sqrt(q2 * k2[:, h * TQ:(h + 1) * TQ]) * (1.0 + 2.0 ** -8))
    return out


def _underflowed(l):
    small = l[0]
    for lh in l[1:]:
        small = jnp.minimum(small, lh)
    return jnp.logical_not(jnp.min(small) >= SOFTMAX_L_MIN)


def _moba_kernel(qT_ref, k_ref, vT_ref, y_ref, km_scr, kn_scr, acc_scr, s_scr, *, NB, NBP, n_sel):
    H, DH, BS = MOBA_HEADS, MOBA_DH, MOBA_BLOCK
    TQ = BS
    qi = pl.program_id(1)

    @pl.when(qi == 0)
    def _():
        km_scr[...] = jnp.zeros_like(km_scr)
        kn_scr[...] = jnp.zeros_like(kn_scr)

        def body(j, carry):
            kb = k_ref[0, pl.ds(pl.multiple_of(j * BS, BS), BS), :].astype(F32)
            km_scr[pl.ds(j, 1), :] = jnp.mean(kb, axis=0, keepdims=True)
            kn_scr[pl.ds(j, 1), :] = _max_key_norm2(kb, H, DH, TQ)
            return carry
        lax.fori_loop(0, NB, body, 0)

    qT = qT_ref[0]
    qTf = qT.astype(F32)
    km = km_scr[...]
    gate = jnp.concatenate(
        [jnp.dot(km[:, h * DH:(h + 1) * DH], qTf[h * DH:(h + 1) * DH, :], precision=lax.Precision.HIGHEST,
                 preferred_element_type=F32) for h in range(H)], axis=1) * (1.0 / ATT_QSCALE)
    blk = lax.broadcasted_iota(I32, gate.shape, 0)
    g = jnp.where(blk < qi, gate, NEG_INF)
    sels = []
    for _ in range(n_sel):
        mx = jnp.max(g, axis=0, keepdims=True)
        isel = jnp.min(jnp.where(g == mx, blk, NBP), axis=0, keepdims=True)
        sels.append(jnp.where(mx > NEG_INF, isel, -1))
        g = jnp.where(blk == isel, NEG_INF, g)

    qs = _block_diag_T(qT, H, DH)

    def hit_row(j):
        hit = sels[0] == j
        for s in sels[1:]:
            hit = hit | (s == j)
        return hit

    blk_start = lambda j: pl.multiple_of(j * BS, BS)
    blk_scores = lambda j: jnp.dot(k_ref[0, pl.ds(blk_start(j), BS), :], qs, preferred_element_type=F32)
    hcols = lambda row, h: row[:, h * TQ:(h + 1) * TQ]
    causal = lax.broadcasted_iota(I32, (BS, TQ), 0) <= lax.broadcasted_iota(I32, (BS, TQ), 1)

    def diag_scores():
        s_all = blk_scores(qi)
        return [jnp.where(causal, hcols(s_all, h), NEG_INF) for h in range(H)]

    def max_sweep():
        def max_block(j, m):
            return jnp.maximum(m, jnp.where(hit_row(j), jnp.max(blk_scores(j), axis=0, keepdims=True), NEG_INF))
        m_all = lax.fori_loop(0, qi, max_block, jnp.full((1, H * TQ), NEG_INF, F32))
        s_diag = diag_scores()
        return [jnp.maximum(hcols(m_all, h), jnp.max(s_diag[h], axis=0, keepdims=True)) for h in range(H)]

    def sum_sweep(m):
        acc_scr[...] = jnp.zeros_like(acc_scr)

        def sum_group(width):
            def body(g, l):
                for u in range(width):
                    s_scr[u] = blk_scores(width * g + u)
                for u in range(width):
                    j = width * g + u
                    hit = hit_row(j)
                    vT_blk = vT_ref[0, :, pl.ds(blk_start(j), BS)]
                    l = tuple(_accumulate(h, jnp.exp2(s_scr[u, :, h * TQ:(h + 1) * TQ]
                                                      - jnp.where(hcols(hit, h), m[h], jnp.inf)),
                                          vT_blk, l[h], acc_scr, DH) for h in range(H))
                return l
            return body
        l = _chunk_loops(qi, ATT_STAGES, sum_group, (jnp.zeros((1, TQ), F32),) * H)
        s_diag = diag_scores()
        vT_diag = vT_ref[0, :, pl.ds(blk_start(qi), BS)]
        return [_accumulate(h, jnp.exp2(s_diag[h] - m[h]), vT_diag, l[h], acc_scr, DH) for h in range(H)]

    blk_n = lax.broadcasted_iota(I32, kn_scr.shape, 0)
    k2 = jnp.max(jnp.where(blk_n <= qi, kn_scr[...], 0.0), axis=0, keepdims=True)
    l = sum_sweep(_norm_bound(qTf, k2, H, DH, TQ))
    _finish_attention(l, acc_scr, y_ref, H, DH)

    @pl.when(_underflowed(l))
    def _():
        _finish_attention(sum_sweep(max_sweep()), acc_scr, y_ref, H, DH)


def _moba(qT, k, vT, B, S):
    BS = MOBA_BLOCK
    assert S % BS == 0
    NB = S // BS
    NBP = -(-NB // SUBLANES) * SUBLANES
    n_sel = max(min(MOBA_TOPK, NB - 1), 1)
    k = k.reshape(B, S, MOBA_W)
    return pl.pallas_call(
        functools.partial(_moba_kernel, NB=NB, NBP=NBP, n_sel=n_sel),
        grid=(B, NB),
        in_specs=[pl.BlockSpec((1, MOBA_W, BS), lambda b, i: (b, 0, i)),
                  pl.BlockSpec((1, S, MOBA_W), lambda b, i: (b, 0, 0)),
                  pl.BlockSpec((1, MOBA_W, S), lambda b, i: (b, 0, 0))],
        out_specs=pl.BlockSpec((1, BS, MOBA_W), lambda b, i: (b, i, 0)),
        out_shape=jax.ShapeDtypeStruct((B, S, MOBA_W), BF16),
        scratch_shapes=[pltpu.VMEM((NBP, MOBA_W), F32),
                        pltpu.VMEM((NBP, MOBA_HEADS * BS), F32),
                        pltpu.VMEM((MOBA_W, BS), F32),
                        pltpu.VMEM((ATT_STAGE, BS, MOBA_HEADS * BS), F32)],
        compiler_params=_params("parallel", "arbitrary"),
    )(qT, k, vT)


def _dsa_kernel(qT_ref, k_ref, vT_ref, iqT_ref, ik_ref, smT_ref, y_ref,
                e_scr, gmax_scr, acc_scr, thr_scr, kn_scr, s_scr, *, TQ, topk, spow, n_chunks):
    H, DH = DSA_HEADS, DSA_DH
    KC = TQ
    qi = pl.program_id(1)
    nch = qi + 1
    chunk_start = lambda c: pl.multiple_of(c * KC, KC)

    @pl.when(qi == 0)
    def _():
        kn_scr[...] = jnp.zeros_like(kn_scr)

        def body(c, carry):
            kn_scr[pl.ds(c, 1), :] = _max_key_norm2(k_ref[0, pl.ds(chunk_start(c), KC), :].astype(F32), H, DH, TQ)
            return carry
        lax.fori_loop(0, n_chunks, body, 0)
    sum8 = lambda b: jnp.sum(b.astype(I32).reshape(KC // SUBLANES, SUBLANES, TQ), axis=0)
    any_row = lambda r: jnp.max(r.astype(I32)) > 0

    iqT = iqT_ref[0]
    iq_cat = jnp.concatenate([iqT[h * IDX_DH:(h + 1) * IDX_DH, :] for h in range(IDX_HEADS)], axis=1)
    iw = smT_ref[0][2 * ML_HEADS:2 * ML_HEADS + IDX_HEADS, :] * (IDX_HEADS ** -0.5 * IDX_DH ** -0.5)
    t_pos = qi * TQ + lax.broadcasted_iota(I32, (KC, TQ), 1)
    s_off = lax.broadcasted_iota(I32, (KC, TQ), 0)
    gmax_scr[...] = jnp.full((KC, TQ), NEG_INF, F32)

    def score_group(width, masked=False):
        def body(g, tally):
            for u in range(width):
                ikc = ik_ref[0, pl.ds(chunk_start(width * g + u), KC), :][:, :IDX_DH]
                s_scr[u] = jnp.dot(ikc, iq_cat, preferred_element_type=F32)
            for u in range(width):
                c = width * g + u
                sc = iw[0:1, :] * jnp.maximum(s_scr[u, :, 0:TQ], 0.0)
                for h in range(1, IDX_HEADS):
                    sc = sc + iw[h:h + 1, :] * jnp.maximum(s_scr[u, :, h * TQ:(h + 1) * TQ], 0.0)
                if masked:
                    sc = jnp.where(c * KC + s_off <= t_pos, sc, NEG_INF)
                e_scr[pl.ds(chunk_start(c), KC), :] = sc
                gmax_scr[...] = jnp.maximum(gmax_scr[...], sc)
                tally = tally + jnp.sum(jnp.where(sc > 0.0, 1, jnp.where(sc == 0.0, 1 << 16, 0))
                                        .reshape(KC // SUBLANES, SUBLANES, TQ), axis=0)
            return tally
        return body
    assert n_chunks * KC < (1 << 16)
    zero8 = jnp.zeros((SUBLANES, TQ), I32)
    tally = _chunk_loops(qi, ATT_STAGES, score_group, zero8)
    tally = jnp.sum(score_group(1, True)(qi, tally), axis=0, keepdims=True)
    n_pos = tally & 0xFFFF
    n_nonneg = n_pos + (tally >> 16)

    def sweep_keys(fold, merge, init):
        def group(width):
            def body(g, acc):
                return tuple(merge(acc[u], fold(e_scr[pl.ds(chunk_start(width * g + u), KC), :]))
                             for u in range(width))
            return body

        def regroup(acc, width):
            half = len(acc) // 2
            acc = tuple(merge(acc[u], acc[u + half]) for u in range(half))
            return acc if half == width else regroup(acc, width)
        return _chunk_loops(nch, (4, 2, 1), group, (init,) * 4, regroup)[0]

    def count_gt(cand):
        return jnp.sum(sweep_keys(lambda x: sum8(x > cand), lambda p, q: p + q, zero8), axis=0, keepdims=True)

    def min_gt(cand):
        fold = lambda x: jnp.min(jnp.where(x > cand, x, jnp.inf).reshape(KC // SUBLANES, SUBLANES, TQ), axis=0)
        return jnp.min(sweep_keys(fold, jnp.minimum, jnp.full((SUBLANES, TQ), jnp.inf, F32)), axis=0, keepdims=True)

    def rewrite(rows, thr):
        def body(c, carry):
            x = e_scr[pl.ds(chunk_start(c), KC), :]
            rank = (2 * spow - (c * KC + s_off)).astype(F32)
            new = jnp.where(x > thr, jnp.inf, jnp.where(x == thr, rank, NEG_INF))
            e_scr[pl.ds(chunk_start(c), KC), :] = jnp.where(rows, new, x)
            return carry
        lax.fori_loop(0, nch, body, 0)

    def halve(_, c):
        lo, hi, act, tie, n_lo = c
        mid = lo + (hi - lo) * 0.5
        inside = (mid > lo) & (mid < hi)
        cnt = count_gt(mid)
        live = act > 0.0
        run = live & inside
        up = run & (cnt >= topk)
        lo = jnp.where(up, mid, lo)
        n_lo = jnp.where(up, cnt, n_lo)
        hi = jnp.where(run & (cnt < topk), mid, hi)
        collapsed = live & jnp.logical_not(inside)
        tie = jnp.where(collapsed, 1.0, tie)
        act = jnp.where(collapsed | (run & (cnt == topk)), 0.0, act)
        return lo, hi, act, tie, n_lo

    def excess(state):
        return jnp.max(jnp.where(state[2] > 0.0, state[4] - topk, 0).astype(F32))

    def bisect(lo, hi, act, warmup):
        state = (lo, hi, act, jnp.zeros_like(act), jnp.full((1, TQ), 1 << 20, I32))
        state = lax.fori_loop(0, warmup, halve, state)

        def one_more(c):
            state = halve(0, c[1])
            return c[0] + 1, state, excess(state)
        near = lax.while_loop(lambda c: jnp.logical_and(c[0] < BISECT_CAP, c[2] > BISECT_DROP), one_more,
                              (jnp.int32(warmup), state, excess(state)))
        lo, hi, act, tie, n_lo = near[1]
        cur, left = lo, jnp.where(act > 0.0, n_lo - topk, 0)
        for _ in range(BISECT_DROP):
            cur = jnp.where(left > 0, min_gt(cur), cur)
            left = left - 1
        done = (act > 0.0) & (count_gt(cur) == topk)
        lo = jnp.where(done, cur, lo)
        act = jnp.where(done, 0.0, act)

        def body(c):
            state = c[1]
            for _ in range(BISECT_STEPS_PER_CHECK):
                state = halve(0, state)
            return c[0] + BISECT_STEPS_PER_CHECK, state, jnp.max(state[2])
        out = lax.while_loop(lambda c: jnp.logical_and(c[0] < BISECT_CAP, c[2] > 0.0), body,
                             (near[0], (lo, hi, act, tie, n_lo), jnp.max(act)))
        return out[1][0], out[1][1], out[1][3]

    fmax = float(jnp.finfo(F32).max)
    gm = gmax_scr[...]
    g_lo = jnp.min(gm, axis=0, keepdims=True)
    g_hi = jnp.max(gm, axis=0, keepdims=True)
    n_vis = qi * TQ + lax.broadcasted_iota(I32, (1, TQ), 1) + 1
    few = n_vis <= topk
    zero_tie = (n_pos < topk) & (n_nonneg >= topk) & jnp.logical_not(few)

    @pl.when(any_row(zero_tie))
    def _():
        rewrite(zero_tie, jnp.zeros((1, TQ), F32))

    rank_lo, rank_hi = float(spow), float(2 * spow + 1)
    below = jnp.maximum(g_lo, -fmax)
    below = below - jnp.abs(below) * (2.0 ** -10) - 1e-30
    lo0 = jnp.where(zero_tie, rank_lo, jnp.maximum(below, -fmax))
    hi0 = jnp.where(zero_tie, rank_hi, g_hi)
    lo, hi, tie = bisect(lo0, hi0, jnp.where(few, 0.0, 1.0), BISECT_WARMUP)
    thr_scr[0:1, :] = jnp.where(few, NEG_INF, lo)

    @pl.when(jnp.max(tie) > 0.0)
    def _():
        rows = tie > 0.0
        rewrite(rows, hi)
        lo2, _, _ = bisect(jnp.full((1, TQ), rank_lo, F32), jnp.full((1, TQ), rank_hi, F32), tie, 0)
        thr_scr[0:1, :] = jnp.where(rows, lo2, thr_scr[0:1, :])

    thr = thr_scr[0:1, :]
    qT = qT_ref[0]
    qs = _block_diag_T(qT, H, DH)
    hcols = lambda row, h: row[:, h * TQ:(h + 1) * TQ]

    def chunk_scores(c):
        start = chunk_start(c)
        sel = e_scr[pl.ds(start, KC), :] > thr
        s_all = jnp.dot(k_ref[0, pl.ds(start, KC), :], qs, preferred_element_type=F32)
        return sel, s_all

    def max_sweep():
        def max_chunk(c, m):
            sel, s_all = chunk_scores(c)
            return tuple(jnp.maximum(m[h], jnp.max(jnp.where(sel, hcols(s_all, h), NEG_INF), axis=0, keepdims=True))
                         for h in range(H))
        m = lax.fori_loop(0, nch, max_chunk, (jnp.full((1, TQ), NEG_INF, F32),) * H)
        return [jnp.where(mh == NEG_INF, 0.0, mh) for mh in m]

    def sum_sweep(m):
        acc_scr[...] = jnp.zeros_like(acc_scr)

        def sum_group(width):
            def body(g, l):
                for u in range(width):
                    s_scr[u] = jnp.dot(k_ref[0, pl.ds(chunk_start(width * g + u), KC), :], qs,
                                       preferred_element_type=F32)
                for u in range(width):
                    start = chunk_start(width * g + u)
                    sel = e_scr[pl.ds(start, KC), :] > thr
                    vT_chunk = vT_ref[0, :, pl.ds(start, KC)]
                    l = tuple(_accumulate(h, jnp.exp2(s_scr[u, :, h * TQ:(h + 1) * TQ] + jnp.where(sel, -m[h], NEG_INF)),
                                          vT_chunk, l[h], acc_scr, DH) for h in range(H))
                return l
            return body
        return _chunk_loops(nch, ATT_STAGES, sum_group, (jnp.zeros((1, TQ), F32),) * H)

    chunk_n = lax.broadcasted_iota(I32, kn_scr.shape, 0)
    k2 = jnp.max(jnp.where(chunk_n < nch, kn_scr[...], 0.0), axis=0, keepdims=True)
    l = sum_sweep(_norm_bound(qT.astype(F32), k2, H, DH, TQ))
    _finish_attention(l, acc_scr, y_ref, H, DH)

    @pl.when(_underflowed(l))
    def _():
        _finish_attention(sum_sweep(max_sweep()), acc_scr, y_ref, H, DH)


def _dsa(qT, k, vT, iqT, ik, smT, B, S):
    TQ = min(ATT_TQ, S)
    topk = min(DSA_TOPK_MAX, S // 4)
    assert S % TQ == 0 and topk <= TQ
    spow = 1 << max((S - 1).bit_length(), 1)
    k = k.reshape(B, S, DSA_W)
    ik = ik.reshape(B, S, LANES)
    qblk = lambda n: pl.BlockSpec((1, n, TQ), lambda b, i: (b, 0, i))
    return pl.pallas_call(
        functools.partial(_dsa_kernel, TQ=TQ, topk=topk, spow=spow, n_chunks=S // TQ),
        grid=(B, S // TQ),
        in_specs=[qblk(DSA_W),
                  pl.BlockSpec((1, S, DSA_W), lambda b, i: (b, 0, 0)),
                  pl.BlockSpec((1, DSA_W, S), lambda b, i: (b, 0, 0)),
                  qblk(IDX_HEADS * IDX_DH),
                  pl.BlockSpec((1, S, LANES), lambda b, i: (b, 0, 0)),
                  qblk(16)],
        out_specs=pl.BlockSpec((1, TQ, DSA_W), lambda b, i: (b, i, 0)),
        out_shape=jax.ShapeDtypeStruct((B, S, DSA_W), BF16),
        scratch_shapes=[pltpu.VMEM((S, TQ), F32),
                        pltpu.VMEM((TQ, TQ), F32),
                        pltpu.VMEM((DSA_W, TQ), F32),
                        pltpu.VMEM((SUBLANES, TQ), F32),
                        pltpu.VMEM((-(-(S // TQ) // SUBLANES) * SUBLANES, DSA_HEADS * TQ), F32),
                        pltpu.VMEM((ATT_STAGE, TQ, DSA_HEADS * TQ), F32)],
        compiler_params=_params("parallel", "arbitrary"),
    )(qT, k, vT, iqT, ik, smT)


def _out_mlp_kernel(yml_ref, ymb_ref, yds_ref, x_ref, wout_ref, gpost_ref, gpre_ref,
                    w1_ref, w2_ref, g2_ref, out_ref, x1_scr, h_scr, acc_scr):
    f = pl.program_id(1)

    @pl.when(f == 0)
    def _():
        slab = x_ref.shape[0] // MLP_SLABS
        for r in range(MLP_SLABS):
            rows = slice(r * slab, (r + 1) * slab)
            mix = jnp.dot(yml_ref[rows, :], wout_ref[0:ML_V_W, :], preferred_element_type=F32)
            mix = mix + jnp.dot(ymb_ref[rows, :], wout_ref[ML_V_W:ML_V_W + MOBA_W, :], preferred_element_type=F32)
            mix = mix + jnp.dot(yds_ref[rows, :], wout_ref[ML_V_W + MOBA_W:, :], preferred_element_type=F32)
            x1 = x_ref[rows, :] + _rms(mix, gpost_ref[...])
            x1_scr[rows, :] = x1
            h_scr[rows, :] = _rms(x1, gpre_ref[...]).astype(BF16)
        acc_scr[...] = jnp.zeros_like(acc_scr)

    u = jnp.maximum(jnp.dot(h_scr[...], w1_ref[...], preferred_element_type=F32), 0.0)
    acc_scr[...] += jnp.dot((u * u).astype(BF16), w2_ref[...], preferred_element_type=F32)

    @pl.when(f == pl.num_programs(1) - 1)
    def _():
        out_ref[...] = x1_scr[...] + _rms(acc_scr[...], g2_ref[...])


def _out_mlp(yml, ymb, yds, x2, w_out, g_post, g_pre, w1, w2, g2):
    T, D = x2.shape
    tm = min(MLP_TM, T)
    tf = min(MLP_TF, D_FF)
    row = lambda n: pl.BlockSpec((tm, n), lambda i, f: (i, 0))
    full = lambda a: pl.BlockSpec(a.shape, lambda i, f: (0,) * a.ndim)
    return pl.pallas_call(
        _out_mlp_kernel,
        grid=(T // tm, D_FF // tf),
        in_specs=[row(ML_V_W), row(MOBA_W), row(DSA_W), row(D), full(w_out), full(g_post), full(g_pre),
                  pl.BlockSpec((D, tf), lambda i, f: (0, f)),
                  pl.BlockSpec((tf, D), lambda i, f: (f, 0)),
                  full(g2)],
        out_specs=row(D),
        out_shape=jax.ShapeDtypeStruct((T, D), F32),
        scratch_shapes=[pltpu.VMEM((tm, D), F32), pltpu.VMEM((tm, D), BF16), pltpu.VMEM((tm, D), F32)],
        compiler_params=_params("parallel", "arbitrary"),
    )(yml, ymb, yds, x2, w_out, g_post, g_pre, w1, w2, g2)


def kernel(x, norm_mix_pre, w_in, ml_conv, ml_i_bias, ml_f_bias, ml_head_norm, w_out, norm_mix_post,
           norm_mlp_pre, w_ff1, w_ff2, norm_mlp_post):
    B, S, D = x.shape
    depth = w_in.shape[0]
    x2 = x.reshape(B * S, D)
    gain = lambda g: g.reshape(1, D).astype(F32)
    for l in range(depth):
        wn, wt = _prep_in_weights(w_in[l])
        (qk, mbk, dsk, ixk, mbqT, mbvT, dsqT, dsvT, ixqT, mlvT, mloT, smT) = _in_proj(
            x2, gain(norm_mix_pre[l]), wn, wt, B, S)
        y_ml = _mlstm(qk, mlvT, mloT, smT, ml_conv[l], ml_i_bias[l], ml_f_bias[l], ml_head_norm[l], B, S)
        y_mb = _moba(mbqT, mbk, mbvT, B, S)
        y_ds = _dsa(dsqT, dsk, dsvT, ixqT, ixk, smT, B, S)
        x2 = _out_mlp(y_ml.reshape(B * S, ML_V_W), y_mb.reshape(B * S, MOBA_W), y_ds.reshape(B * S, DSA_W),
                      x2, w_out[l].astype(BF16), gain(norm_mix_post[l]), gain(norm_mlp_pre[l]),
                      w_ff1[l].astype(BF16), w_ff2[l].astype(BF16), gain(norm_mlp_post[l]))
    return x2.reshape(B, S, D)
```

```python
import functools

import jax
import jax.numpy as jnp
from jax import lax
from jax.experimental import pallas as pl
from jax.experimental.pallas import tpu as pltpu

F32 = jnp.float32
BF16 = jnp.bfloat16
I32 = jnp.int32

D_MODEL = 1024
ML_HEADS, ML_DQK, ML_DV, ML_CONV = 4, 64, 128, 4
MOBA_HEADS, MOBA_DH, MOBA_BLOCK, MOBA_TOPK = 4, 64, 256, 3
DSA_HEADS, DSA_DH, IDX_HEADS, IDX_DH, DSA_TOPK_MAX = 4, 64, 4, 64, 256
D_FF = 4 * D_MODEL
EPS = 1e-6

ML_QK_W = ML_HEADS * ML_DQK
ML_V_W = ML_HEADS * ML_DV
MOBA_W = MOBA_HEADS * MOBA_DH
DSA_W = DSA_HEADS * DSA_DH
IN_SPLITS = (ML_QK_W, ML_QK_W, ML_V_W, ML_V_W, ML_HEADS, ML_HEADS,
             MOBA_W, MOBA_W, MOBA_W,
             DSA_W, DSA_W, DSA_W, IDX_HEADS * IDX_DH, IDX_DH, IDX_HEADS)

LANES = 128
SUBLANES = 8
VMEM_LIMIT = 52 * 1024 * 1024

ML_CHUNK = 512
ATT_TQ = 256
ATT_STAGES = (8, 4, 2, 1)
ATT_STAGE = ATT_STAGES[0]
PROJ_TM = 1024
MLP_TM = 1024
MLP_TF = 512
MLP_SLABS = 4

NEG_INF = float("-inf")
BISECT_CAP = 300
BISECT_WARMUP = 12
BISECT_DROP = 2
BISECT_STEPS_PER_CHECK = 2
SOFTMAX_L_MIN = 1e-30
assert MOBA_DH == DSA_DH
ATT_QSCALE = MOBA_DH ** -0.5 * 1.4426950408889634

_NT = (((1,), (1,)), ((), ()))


def _chunk_loops(n, widths, make_body, carry, regroup=lambda c, w: c):
    covered = 0
    for i, w in enumerate(widths):
        if i:
            carry = regroup(carry, w)
        carry = lax.fori_loop(covered // w, n // w, make_body(w), carry)
        covered = w * (n // w)
    return carry


def _rms(x, g):
    return x * lax.rsqrt(jnp.mean(x * x, axis=-1, keepdims=True) + EPS) * g


def _log_sigmoid(x):
    return jnp.minimum(x, 0.0) - jnp.log1p(jnp.exp(-jnp.abs(x)))


def _params(*sem):
    return pltpu.CompilerParams(dimension_semantics=sem, vmem_limit_bytes=VMEM_LIMIT)


_N_QK, _N_MBK, _N_DSK, _N_IXK = 0, 512, 768, 1024
_N_TOTAL = 1152
_T_MBQ, _T_MBV, _T_DSQ, _T_DSV, _T_IXQ, _T_MLV, _T_MLO, _T_SM = 0, 256, 512, 768, 1024, 1280, 1792, 2304
_T_TOTAL = 2320


def _in_proj_kernel(x_ref, g_ref, wn_ref, wt_ref,
                    qk_ref, mbk_ref, dsk_ref, ixk_ref,
                    mbqT_ref, mbvT_ref, dsqT_ref, dsvT_ref, ixqT_ref, mlvT_ref, mloT_ref, smT_ref):
    h = _rms(x_ref[...], g_ref[...]).astype(BF16)

    def mm(a, n):
        return jnp.dot(h, wn_ref[:, a:a + n], preferred_element_type=F32)

    def mt(a, n):
        return lax.dot_general(wt_ref[a:a + n, :], h, _NT, preferred_element_type=F32)

    qk_ref[...] = mm(_N_QK, 512)
    mbk_ref[...] = mm(_N_MBK, 256).astype(BF16)
    dsk_ref[...] = mm(_N_DSK, 256).astype(BF16)
    ixk_ref[...] = mm(_N_IXK, 128).astype(BF16)
    mbqT_ref[0] = (mt(_T_MBQ, 256) * ATT_QSCALE).astype(BF16)
    mbvT_ref[0] = mt(_T_MBV, 256).astype(BF16)
    dsqT_ref[0] = (mt(_T_DSQ, 256) * ATT_QSCALE).astype(BF16)
    dsvT_ref[0] = mt(_T_DSV, 256).astype(BF16)
    ixqT_ref[0] = mt(_T_IXQ, 256).astype(BF16)
    mlvT_ref[0] = mt(_T_MLV, 512).astype(BF16)
    mloT_ref[0] = mt(_T_MLO, 512)
    smT_ref[0] = mt(_T_SM, 16)


def _prep_in_weights(w_in):
    pts, acc = [], 0
    for n in IN_SPLITS:
        pts.append((acc, acc + n))
        acc += n
    col = lambda i: w_in[:, pts[i][0]:pts[i][1]]
    (ml_q, ml_k, ml_v, ml_o, ml_i, ml_f, mb_q, mb_k, mb_v,
     ds_q, ds_k, ds_v, ix_q, ix_k, ix_w) = [col(i) for i in range(len(IN_SPLITS))]
    d = w_in.shape[0]
    small = jnp.concatenate([ml_i, ml_f, ix_w], axis=1)
    wn = jnp.concatenate([
        ml_q, ml_k, mb_k, ds_k,
        ix_k, jnp.zeros((d, 128 - IDX_DH), w_in.dtype)], axis=1)
    wt = jnp.concatenate([
        mb_q, mb_v, ds_q, ds_v, ix_q, ml_v, ml_o,
        small, jnp.zeros((d, 16 - small.shape[1]), w_in.dtype)], axis=1).T
    assert wn.shape == (d, _N_TOTAL) and wt.shape == (_T_TOTAL, d)
    return wn.astype(BF16), wt.astype(BF16)


def _in_proj(x2, g, wn, wt, B, S):
    T, D = x2.shape
    tm = min(PROJ_TM, S)
    nsb = S // tm
    row = lambda n: pl.BlockSpec((tm, n), lambda i: (i, 0))
    tr = lambda n: pl.BlockSpec((1, n, tm), lambda i: (i // nsb, 0, i % nsb))
    full = lambda a: pl.BlockSpec(a.shape, lambda i: (0,) * a.ndim)
    out_shape = (
        jax.ShapeDtypeStruct((T, 512), F32),
        jax.ShapeDtypeStruct((T, 256), BF16),
        jax.ShapeDtypeStruct((T, 256), BF16),
        jax.ShapeDtypeStruct((T, 128), BF16),
        jax.ShapeDtypeStruct((B, 256, S), BF16),
        jax.ShapeDtypeStruct((B, 256, S), BF16),
        jax.ShapeDtypeStruct((B, 256, S), BF16),
        jax.ShapeDtypeStruct((B, 256, S), BF16),
        jax.ShapeDtypeStruct((B, 256, S), BF16),
        jax.ShapeDtypeStruct((B, 512, S), BF16),
        jax.ShapeDtypeStruct((B, 512, S), F32),
        jax.ShapeDtypeStruct((B, 16, S), F32),
    )
    out_specs = (row(512), row(256), row(256), row(128),
                 tr(256), tr(256), tr(256), tr(256), tr(256), tr(512), tr(512), tr(16))
    return pl.pallas_call(
        _in_proj_kernel,
        grid=(T // tm,),
        in_specs=[row(D), full(g), full(wn), full(wt)],
        out_specs=out_specs,
        out_shape=out_shape,
        compiler_params=_params("parallel"),
    )(x2, g, wn, wt)


def _dot3(a_f32, b_exact):
    b = b_exact.astype(BF16)
    hi = a_f32.astype(BF16)
    r1 = a_f32 - hi.astype(F32)
    mid = r1.astype(BF16)
    lo = (r1 - mid.astype(F32)).astype(BF16)
    return (jnp.dot(hi, b, preferred_element_type=F32) + jnp.dot(mid, b, preferred_element_type=F32)
            + jnp.dot(lo, b, preferred_element_type=F32))


def _mlstm_kernel(qk_ref, vT_ref, oT_ref, smT_ref, conv_ref, bcol_ref, hn_ref,
                  y_ref, xp_scr, st_scr, m_scr, *, L):
    c = pl.program_id(1)
    halo = SUBLANES
    DK, DV = ML_DQK, ML_DV

    @pl.when(c == 0)
    def _():
        xp_scr[0:halo, :] = jnp.zeros((halo, 2 * ML_QK_W), F32)
        st_scr[...] = jnp.zeros_like(st_scr)
        m_scr[...] = jnp.zeros_like(m_scr)

    cur = qk_ref[0]
    xp_scr[halo:halo + L, :] = cur
    base = halo - (ML_CONV - 1)
    acc = conv_ref[0:1, :] * xp_scr[base:base + L, :]
    for j in range(1, ML_CONV):
        acc = acc + conv_ref[j:j + 1, :] * xp_scr[base + j:base + j + L, :]
    xp_scr[0:halo, :] = cur[L - halo:L, :]
    qk = acc * jax.nn.sigmoid(acc)
    qT_all = (qk[:, :ML_QK_W] * (DK ** -0.5)).T.astype(BF16)
    k_all = qk[:, ML_QK_W:].astype(BF16)

    grow = smT_ref[0][0:SUBLANES, :] + bcol_ref[...]
    si = lax.broadcasted_iota(I32, (L, L), 0)
    ji = lax.broadcasted_iota(I32, (L, L), 1)
    causal = si <= ji
    b_row = _dot3(_log_sigmoid(grow), causal)

    for h in range(ML_HEADS):
        qT = qT_all[h * DK:(h + 1) * DK, :]
        kh = k_all[:, h * DK:(h + 1) * DK]
        vT = vT_ref[0, h * DV:(h + 1) * DV, :]
        b_j = b_row[ML_HEADS + h:ML_HEADS + h + 1, :]
        c_row = grow[h:h + 1, :] - b_j
        c_col = jnp.broadcast_to(c_row, (LANES, L)).T
        dlog = jnp.where(causal, b_j + jnp.concatenate([c_col] * (L // LANES), axis=1), NEG_INF)
        m0 = m_scr[h:h + 1, 0:1]
        inter = b_j + m0
        m_t = jnp.maximum(inter, jnp.max(dlog, axis=0, keepdims=True))
        w_inter = jnp.exp(inter - m_t)
        w_intra = jnp.exp(dlog - m_t) * jnp.dot(kh, qT, preferred_element_type=F32)
        st = st_scr[h]
        cq = jnp.dot(st.astype(BF16), qT, preferred_element_type=F32)
        num = w_inter * cq[:DV] + jnp.dot(vT, w_intra.astype(BF16), preferred_element_type=F32)
        den = w_inter * cq[DV:DV + 1] + jnp.sum(w_intra, axis=0, keepdims=True)
        hh = num / jnp.maximum(jnp.abs(den), jnp.exp(-m_t))
        hh = hh * lax.rsqrt(jnp.mean(hh * hh, axis=0, keepdims=True) + EPS)
        rows = slice(h * DV, (h + 1) * DV)
        y_ref[0, :, rows] = (jax.nn.sigmoid(oT_ref[0, rows, :]) * hh * hn_ref[rows, :]).T.astype(y_ref.dtype)

        b_last = b_j[:, L - 1:L]
        a = b_last + c_row
        m_loc = jnp.max(a, axis=1, keepdims=True)
        wa = jnp.exp(a - m_loc)
        lhs = jnp.concatenate([(vT.astype(F32) * wa).astype(BF16),
                               jnp.broadcast_to(wa, (SUBLANES, L)).astype(BF16)], axis=0)
        c_loc = jnp.dot(lhs, kh, preferred_element_type=F32)
        m_new = jnp.maximum(b_last + m0, m_loc)
        s_old = jnp.exp(b_last + m0 - m_new)
        s_loc = jnp.exp(m_loc - m_new)
        st_scr[h] = s_old * st + s_loc * c_loc
        m_scr[h:h + 1, :] = jnp.broadcast_to(m_new, (1, LANES))


def _mlstm(qk, vT, oT, smT, conv_w, i_bias, f_bias, head_norm, B, S):
    L = min(ML_CHUNK, S)
    assert L % LANES == 0
    nc = S // L
    qk = qk.reshape(B, S, 2 * ML_QK_W)
    bcol = jnp.concatenate([i_bias, f_bias]).astype(F32).reshape(2 * ML_HEADS, 1)
    hn = jnp.broadcast_to(head_norm.astype(F32)[:, None], (ML_V_W, L))
    blkT = lambda n: pl.BlockSpec((1, n, L), lambda b, c: (b, 0, c))
    full = lambda a: pl.BlockSpec(a.shape, lambda b, c: (0,) * a.ndim)
    return pl.pallas_call(
        functools.partial(_mlstm_kernel, L=L),
        grid=(B, nc),
        in_specs=[pl.BlockSpec((1, L, 2 * ML_QK_W), lambda b, c: (b, c, 0)),
                  blkT(ML_V_W), blkT(ML_V_W), blkT(16), full(conv_w), full(bcol), full(hn)],
        out_specs=pl.BlockSpec((1, L, ML_V_W), lambda b, c: (b, c, 0)),
        out_shape=jax.ShapeDtypeStruct((B, S, ML_V_W), BF16),
        scratch_shapes=[pltpu.VMEM((L + SUBLANES, 2 * ML_QK_W), F32),
                        pltpu.VMEM((ML_HEADS, ML_DV + SUBLANES, ML_DQK), F32),
                        pltpu.VMEM((SUBLANES, LANES), F32)],
        compiler_params=_params("parallel", "arbitrary"),
    )(qk, vT, oT, smT, conv_w.astype(F32), bcol, hn)


def _block_diag_T(qT, heads, dh):
    rid = lax.broadcasted_iota(I32, qT.shape, 0)
    zero = jnp.zeros_like(qT)
    return jnp.concatenate(
        [jnp.where((rid >= h * dh) & (rid < (h + 1) * dh), qT, zero) for h in range(heads)], axis=1)


def _accumulate(h, p, vT_chunk, l, acc_scr, dh):
    rows = slice(h * dh, (h + 1) * dh)
    acc_scr[rows, :] += jnp.dot(vT_chunk[rows, :], p.astype(BF16), preferred_element_type=F32)
    return l + jnp.sum(p, axis=0, keepdims=True)


def _finish_attention(l, acc_scr, y_ref, heads, dh):
    outs = [acc_scr[h * dh:(h + 1) * dh, :] / l[h] for h in range(heads)]
    y_ref[0] = jnp.concatenate(outs, axis=0).T.astype(y_ref.dtype)


def _max_key_norm2(kb, heads, dh, TQ):
    ch = lax.broadcasted_iota(I32, (heads * dh, LANES), 0) // dh
    hd = lax.broadcasted_iota(I32, (heads * dh, LANES), 1)
    n2 = jnp.dot((kb * kb).astype(BF16), (ch == hd).astype(BF16), preferred_element_type=F32)
    n2 = jnp.max(n2, axis=0, keepdims=True) * (1.0 + 2.0 ** -7)
    return jnp.concatenate([jnp.broadcast_to(n2[:, h:h + 1], (1, TQ)) for h in range(heads)], axis=1)


def _norm_bound(qTf, k2, heads, dh, TQ):
    out = []
    for h in range(heads):
        qh = qTf[h * dh:(h + 1) * dh, :]
        q2 = jnp.sum(qh * qh, axis=0, keepdims=True)
        out.append(jnp.sqrt(q2 * k2[:, h * TQ:(h + 1) * TQ]) * (1.0 + 2.0 ** -8))
    return out


def _underflowed(l):
    small = l[0]
    for lh in l[1:]:
        small = jnp.minimum(small, lh)
    return jnp.logical_not(jnp.min(small) >= SOFTMAX_L_MIN)


def _moba_kernel(qT_ref, k_ref, vT_ref, y_ref, km_scr, kn_scr, acc_scr, s_scr, *, NB, NBP, n_sel):
    H, DH, BS = MOBA_HEADS, MOBA_DH, MOBA_BLOCK
    TQ = BS
    qi = pl.program_id(1)

    @pl.when(qi == 0)
    def _():
        km_scr[...] = jnp.zeros_like(km_scr)
        kn_scr[...] = jnp.zeros_like(kn_scr)

        def body(j, carry):
            kb = k_ref[0, pl.ds(pl.multiple_of(j * BS, BS), BS), :].astype(F32)
            km_scr[pl.ds(j, 1), :] = jnp.mean(kb, axis=0, keepdims=True)
            kn_scr[pl.ds(j, 1), :] = _max_key_norm2(kb, H, DH, TQ)
            return carry
        lax.fori_loop(0, NB, body, 0)

    qT = qT_ref[0]
    qTf = qT.astype(F32)
    km = km_scr[...]
    gate = jnp.concatenate(
        [jnp.dot(km[:, h * DH:(h + 1) * DH], qTf[h * DH:(h + 1) * DH, :], precision=lax.Precision.HIGHEST,
                 preferred_element_type=F32) for h in range(H)], axis=1) * (1.0 / ATT_QSCALE)
    blk = lax.broadcasted_iota(I32, gate.shape, 0)
    g = jnp.where(blk < qi, gate, NEG_INF)
    sels = []
    for _ in range(n_sel):
        mx = jnp.max(g, axis=0, keepdims=True)
        isel = jnp.min(jnp.where(g == mx, blk, NBP), axis=0, keepdims=True)
        sels.append(jnp.where(mx > NEG_INF, isel, -1))
        g = jnp.where(blk == isel, NEG_INF, g)

    qs = _block_diag_T(qT, H, DH)

    def hit_row(j):
        hit = sels[0] == j
        for s in sels[1:]:
            hit = hit | (s == j)
        return hit

    blk_start = lambda j: pl.multiple_of(j * BS, BS)
    blk_scores = lambda j: jnp.dot(k_ref[0, pl.ds(blk_start(j), BS), :], qs, preferred_element_type=F32)
    hcols = lambda row, h: row[:, h * TQ:(h + 1) * TQ]
    causal = lax.broadcasted_iota(I32, (BS, TQ), 0) <= lax.broadcasted_iota(I32, (BS, TQ), 1)

    def diag_scores():
        s_all = blk_scores(qi)
        return [jnp.where(causal, hcols(s_all, h), NEG_INF) for h in range(H)]

    def max_sweep():
        def max_block(j, m):
            return jnp.maximum(m, jnp.where(hit_row(j), jnp.max(blk_scores(j), axis=0, keepdims=True), NEG_INF))
        m_all = lax.fori_loop(0, qi, max_block, jnp.full((1, H * TQ), NEG_INF, F32))
        s_diag = diag_scores()
        return [jnp.maximum(hcols(m_all, h), jnp.max(s_diag[h], axis=0, keepdims=True)) for h in range(H)]

    def sum_sweep(m):
        acc_scr[...] = jnp.zeros_like(acc_scr)

        def sum_group(width):
            def body(g, l):
                for u in range(width):
                    s_scr[u] = blk_scores(width * g + u)
                for u in range(width):
                    j = width * g + u
                    hit = hit_row(j)
                    vT_blk = vT_ref[0, :, pl.ds(blk_start(j), BS)]
                    l = tuple(_accumulate(h, jnp.exp2(s_scr[u, :, h * TQ:(h + 1) * TQ]
                                                      - jnp.where(hcols(hit, h), m[h], jnp.inf)),
                                          vT_blk, l[h], acc_scr, DH) for h in range(H))
                return l
            return body
        l = _chunk_loops(qi, ATT_STAGES, sum_group, (jnp.zeros((1, TQ), F32),) * H)
        s_diag = diag_scores()
        vT_diag = vT_ref[0, :, pl.ds(blk_start(qi), BS)]
        return [_accumulate(h, jnp.exp2(s_diag[h] - m[h]), vT_diag, l[h], acc_scr, DH) for h in range(H)]

    blk_n = lax.broadcasted_iota(I32, kn_scr.shape, 0)
    k2 = jnp.max(jnp.where(blk_n <= qi, kn_scr[...], 0.0), axis=0, keepdims=True)
    l = sum_sweep(_norm_bound(qTf, k2, H, DH, TQ))
    _finish_attention(l, acc_scr, y_ref, H, DH)

    @pl.when(_underflowed(l))
    def _():
        _finish_attention(sum_sweep(max_sweep()), acc_scr, y_ref, H, DH)


def _moba(qT, k, vT, B, S):
    BS = MOBA_BLOCK
    assert S % BS == 0
    NB = S // BS
    NBP = -(-NB // SUBLANES) * SUBLANES
    n_sel = max(min(MOBA_TOPK, NB - 1), 1)
    k = k.reshape(B, S, MOBA_W)
    return pl.pallas_call(
        functools.partial(_moba_kernel, NB=NB, NBP=NBP, n_sel=n_sel),
        grid=(B, NB),
        in_specs=[pl.BlockSpec((1, MOBA_W, BS), lambda b, i: (b, 0, i)),
                  pl.BlockSpec((1, S, MOBA_W), lambda b, i: (b, 0, 0)),
                  pl.BlockSpec((1, MOBA_W, S), lambda b, i: (b, 0, 0))],
        out_specs=pl.BlockSpec((1, BS, MOBA_W), lambda b, i: (b, i, 0)),
        out_shape=jax.ShapeDtypeStruct((B, S, MOBA_W), BF16),
        scratch_shapes=[pltpu.VMEM((NBP, MOBA_W), F32),
                        pltpu.VMEM((NBP, MOBA_HEADS * BS), F32),
                        pltpu.VMEM((MOBA_W, BS), F32),
                        pltpu.VMEM((ATT_STAGE, BS, MOBA_HEADS * BS), F32)],
        compiler_params=_params("parallel", "arbitrary"),
    )(qT, k, vT)


def _dsa_kernel(qT_ref, k_ref, vT_ref, iqT_ref, ik_ref, smT_ref, y_ref,
                e_scr, gmax_scr, acc_scr, thr_scr, kn_scr, s_scr, *, TQ, topk, spow, n_chunks):
    H, DH = DSA_HEADS, DSA_DH
    KC = TQ
    qi = pl.program_id(1)
    nch = qi + 1
    chunk_start = lambda c: pl.multiple_of(c * KC, KC)

    @pl.when(qi == 0)
    def _():
        kn_scr[...] = jnp.zeros_like(kn_scr)

        def body(c, carry):
            kn_scr[pl.ds(c, 1), :] = _max_key_norm2(k_ref[0, pl.ds(chunk_start(c), KC), :].astype(F32), H, DH, TQ)
            return carry
        lax.fori_loop(0, n_chunks, body, 0)
    sum8 = lambda b: jnp.sum(b.astype(I32).reshape(KC // SUBLANES, SUBLANES, TQ), axis=0)
    any_row = lambda r: jnp.max(r.astype(I32)) > 0

    iqT = iqT_ref[0]
    iq_cat = jnp.concatenate([iqT[h * IDX_DH:(h + 1) * IDX_DH, :] for h in range(IDX_HEADS)], axis=1)
    iw = smT_ref[0][2 * ML_HEADS:2 * ML_HEADS + IDX_HEADS, :] * (IDX_HEADS ** -0.5 * IDX_DH ** -0.5)
    t_pos = qi * TQ + lax.broadcasted_iota(I32, (KC, TQ), 1)
    s_off = lax.broadcasted_iota(I32, (KC, TQ), 0)
    gmax_scr[...] = jnp.full((KC, TQ), NEG_INF, F32)

    def score_group(width, masked=False):
        def body(g, tally):
            for u in range(width):
                ikc = ik_ref[0, pl.ds(chunk_start(width * g + u), KC), :][:, :IDX_DH]
                s_scr[u] = jnp.dot(ikc, iq_cat, preferred_element_type=F32)
            for u in range(width):
                c = width * g + u
                sc = iw[0:1, :] * jnp.maximum(s_scr[u, :, 0:TQ], 0.0)
                for h in range(1, IDX_HEADS):
                    sc = sc + iw[h:h + 1, :] * jnp.maximum(s_scr[u, :, h * TQ:(h + 1) * TQ], 0.0)
                if masked:
                    sc = jnp.where(c * KC + s_off <= t_pos, sc, NEG_INF)
                e_scr[pl.ds(chunk_start(c), KC), :] = sc
                gmax_scr[...] = jnp.maximum(gmax_scr[...], sc)
                tally = tally + jnp.sum(jnp.where(sc > 0.0, 1, jnp.where(sc == 0.0, 1 << 16, 0))
                                        .reshape(KC // SUBLANES, SUBLANES, TQ), axis=0)
            return tally
        return body
    assert n_chunks * KC < (1 << 16)
    zero8 = jnp.zeros((SUBLANES, TQ), I32)
    tally = _chunk_loops(qi, ATT_STAGES, score_group, zero8)
    tally = jnp.sum(score_group(1, True)(qi, tally), axis=0, keepdims=True)
    n_pos = tally & 0xFFFF
    n_nonneg = n_pos + (tally >> 16)

    def sweep_keys(fold, merge, init):
        def group(width):
            def body(g, acc):
                return tuple(merge(acc[u], fold(e_scr[pl.ds(chunk_start(width * g + u), KC), :]))
                             for u in range(width))
            return body

        def regroup(acc, width):
            half = len(acc) // 2
            acc = tuple(merge(acc[u], acc[u + half]) for u in range(half))
            return acc if half == width else regroup(acc, width)
        return _chunk_loops(nch, ATT_STAGES, group, (init,) * ATT_STAGES[0], regroup)[0]

    def count_gt(cand):
        return jnp.sum(sweep_keys(lambda x: sum8(x > cand), lambda p, q: p + q, zero8), axis=0, keepdims=True)

    def min_gt(cand):
        fold = lambda x: jnp.min(jnp.where(x > cand, x, jnp.inf).reshape(KC // SUBLANES, SUBLANES, TQ), axis=0)
        return jnp.min(sweep_keys(fold, jnp.minimum, jnp.full((SUBLANES, TQ), jnp.inf, F32)), axis=0, keepdims=True)

    def rewrite(rows, thr):
        def body(c, carry):
            x = e_scr[pl.ds(chunk_start(c), KC), :]
            rank = (2 * spow - (c * KC + s_off)).astype(F32)
            new = jnp.where(x > thr, jnp.inf, jnp.where(x == thr, rank, NEG_INF))
            e_scr[pl.ds(chunk_start(c), KC), :] = jnp.where(rows, new, x)
            return carry
        lax.fori_loop(0, nch, body, 0)

    def halve(_, c):
        lo, hi, act, tie, n_lo = c
        mid = lo + (hi - lo) * 0.5
        inside = (mid > lo) & (mid < hi)
        cnt = count_gt(mid)
        live = act > 0.0
        run = live & inside
        up = run & (cnt >= topk)
        lo = jnp.where(up, mid, lo)
        n_lo = jnp.where(up, cnt, n_lo)
        hi = jnp.where(run & (cnt < topk), mid, hi)
        collapsed = live & jnp.logical_not(inside)
        tie = jnp.where(collapsed, 1.0, tie)
        act = jnp.where(collapsed | (run & (cnt == topk)), 0.0, act)
        return lo, hi, act, tie, n_lo

    def excess(state):
        return jnp.max(jnp.where(state[2] > 0.0, state[4] - topk, 0).astype(F32))

    def bisect(lo, hi, act, warmup):
        state = (lo, hi, act, jnp.zeros_like(act), jnp.full((1, TQ), 1 << 20, I32))
        state = lax.fori_loop(0, warmup, halve, state)

        def one_more(c):
            state = halve(0, c[1])
            return c[0] + 1, state, excess(state)
        near = lax.while_loop(lambda c: jnp.logical_and(c[0] < BISECT_CAP, c[2] > BISECT_DROP), one_more,
                              (jnp.int32(warmup), state, excess(state)))
        lo, hi, act, tie, n_lo = near[1]
        cur, left = lo, jnp.where(act > 0.0, n_lo - topk, 0)
        for _ in range(BISECT_DROP):
            cur = jnp.where(left > 0, min_gt(cur), cur)
            left = left - 1
        done = (act > 0.0) & (count_gt(cur) == topk)
        lo = jnp.where(done, cur, lo)
        act = jnp.where(done, 0.0, act)

        def body(c):
            state = c[1]
            for _ in range(BISECT_STEPS_PER_CHECK):
                state = halve(0, state)
            return c[0] + BISECT_STEPS_PER_CHECK, state, jnp.max(state[2])
        out = lax.while_loop(lambda c: jnp.logical_and(c[0] < BISECT_CAP, c[2] > 0.0), body,
                             (near[0], (lo, hi, act, tie, n_lo), jnp.max(act)))
        return out[1][0], out[1][1], out[1][3]

    fmax = float(jnp.finfo(F32).max)
    gm = gmax_scr[...]
    g_lo = jnp.min(gm, axis=0, keepdims=True)
    g_hi = jnp.max(gm, axis=0, keepdims=True)
    n_vis = qi * TQ + lax.broadcasted_iota(I32, (1, TQ), 1) + 1
    few = n_vis <= topk
    zero_tie = (n_pos < topk) & (n_nonneg >= topk) & jnp.logical_not(few)

    @pl.when(any_row(zero_tie))
    def _():
        rewrite(zero_tie, jnp.zeros((1, TQ), F32))

    rank_lo, rank_hi = float(spow), float(2 * spow + 1)
    below = jnp.maximum(g_lo, -fmax)
    below = below - jnp.abs(below) * (2.0 ** -10) - 1e-30
    lo0 = jnp.where(zero_tie, rank_lo, jnp.maximum(below, -fmax))
    hi0 = jnp.where(zero_tie, rank_hi, g_hi)
    lo, hi, tie = bisect(lo0, hi0, jnp.where(few, 0.0, 1.0), BISECT_WARMUP)
    thr_scr[0:1, :] = jnp.where(few, NEG_INF, lo)

    @pl.when(jnp.max(tie) > 0.0)
    def _():
        rows = tie > 0.0
        rewrite(rows, hi)
        lo2, _, _ = bisect(jnp.full((1, TQ), rank_lo, F32), jnp.full((1, TQ), rank_hi, F32), tie, 0)
        thr_scr[0:1, :] = jnp.where(rows, lo2, thr_scr[0:1, :])

    thr = thr_scr[0:1, :]
    qT = qT_ref[0]
    qs = _block_diag_T(qT, H, DH)
    hcols = lambda row, h: row[:, h * TQ:(h + 1) * TQ]

    def chunk_scores(c):
        start = chunk_start(c)
        sel = e_scr[pl.ds(start, KC), :] > thr
        s_all = jnp.dot(k_ref[0, pl.ds(start, KC), :], qs, preferred_element_type=F32)
        return sel, s_all

    def max_sweep():
        def max_chunk(c, m):
            sel, s_all = chunk_scores(c)
            return tuple(jnp.maximum(m[h], jnp.max(jnp.where(sel, hcols(s_all, h), NEG_INF), axis=0, keepdims=True))
                         for h in range(H))
        m = lax.fori_loop(0, nch, max_chunk, (jnp.full((1, TQ), NEG_INF, F32),) * H)
        return [jnp.where(mh == NEG_INF, 0.0, mh) for mh in m]

    def sum_sweep(m):
        acc_scr[...] = jnp.zeros_like(acc_scr)

        def sum_group(width):
            def body(g, l):
                for u in range(width):
                    s_scr[u] = jnp.dot(k_ref[0, pl.ds(chunk_start(width * g + u), KC), :], qs,
                                       preferred_element_type=F32)
                for u in range(width):
                    start = chunk_start(width * g + u)
                    sel = e_scr[pl.ds(start, KC), :] > thr
                    vT_chunk = vT_ref[0, :, pl.ds(start, KC)]
                    l = tuple(_accumulate(h, jnp.exp2(s_scr[u, :, h * TQ:(h + 1) * TQ] + jnp.where(sel, -m[h], NEG_INF)),
                                          vT_chunk, l[h], acc_scr, DH) for h in range(H))
                return l
            return body
        return _chunk_loops(nch, ATT_STAGES, sum_group, (jnp.zeros((1, TQ), F32),) * H)

    chunk_n = lax.broadcasted_iota(I32, kn_scr.shape, 0)
    k2 = jnp.max(jnp.where(chunk_n < nch, kn_scr[...], 0.0), axis=0, keepdims=True)
    l = sum_sweep(_norm_bound(qT.astype(F32), k2, H, DH, TQ))
    _finish_attention(l, acc_scr, y_ref, H, DH)

    @pl.when(_underflowed(l))
    def _():
        _finish_attention(sum_sweep(max_sweep()), acc_scr, y_ref, H, DH)


def _dsa(qT, k, vT, iqT, ik, smT, B, S):
    TQ = min(ATT_TQ, S)
    topk = min(DSA_TOPK_MAX, S // 4)
    assert S % TQ == 0 and topk <= TQ
    spow = 1 << max((S - 1).bit_length(), 1)
    k = k.reshape(B, S, DSA_W)
    ik = ik.reshape(B, S, LANES)
    qblk = lambda n: pl.BlockSpec((1, n, TQ), lambda b, i: (b, 0, i))
    return pl.pallas_call(
        functools.partial(_dsa_kernel, TQ=TQ, topk=topk, spow=spow, n_chunks=S // TQ),
        grid=(B, S // TQ),
        in_specs=[qblk(DSA_W),
                  pl.BlockSpec((1, S, DSA_W), lambda b, i: (b, 0, 0)),
                  pl.BlockSpec((1, DSA_W, S), lambda b, i: (b, 0, 0)),
                  qblk(IDX_HEADS * IDX_DH),
                  pl.BlockSpec((1, S, LANES), lambda b, i: (b, 0, 0)),
                  qblk(16)],
        out_specs=pl.BlockSpec((1, TQ, DSA_W), lambda b, i: (b, i, 0)),
        out_shape=jax.ShapeDtypeStruct((B, S, DSA_W), BF16),
        scratch_shapes=[pltpu.VMEM((S, TQ), F32),
                        pltpu.VMEM((TQ, TQ), F32),
                        pltpu.VMEM((DSA_W, TQ), F32),
                        pltpu.VMEM((SUBLANES, TQ), F32),
                        pltpu.VMEM((-(-(S // TQ) // SUBLANES) * SUBLANES, DSA_HEADS * TQ), F32),
                        pltpu.VMEM((ATT_STAGE, TQ, DSA_HEADS * TQ), F32)],
        compiler_params=_params("parallel", "arbitrary"),
    )(qT, k, vT, iqT, ik, smT)


def _out_mlp_kernel(yml_ref, ymb_ref, yds_ref, x_ref, wout_ref, gpost_ref, gpre_ref,
                    w1_ref, w2_ref, g2_ref, out_ref, x1_scr, h_scr, acc_scr):
    f = pl.program_id(1)

    @pl.when(f == 0)
    def _():
        slab = x_ref.shape[0] // MLP_SLABS
        for r in range(MLP_SLABS):
            rows = slice(r * slab, (r + 1) * slab)
            mix = jnp.dot(yml_ref[rows, :], wout_ref[0:ML_V_W, :], preferred_element_type=F32)
            mix = mix + jnp.dot(ymb_ref[rows, :], wout_ref[ML_V_W:ML_V_W + MOBA_W, :], preferred_element_type=F32)
            mix = mix + jnp.dot(yds_ref[rows, :], wout_ref[ML_V_W + MOBA_W:, :], preferred_element_type=F32)
            x1 = x_ref[rows, :] + _rms(mix, gpost_ref[...])
            x1_scr[rows, :] = x1
            h_scr[rows, :] = _rms(x1, gpre_ref[...]).astype(BF16)
        acc_scr[...] = jnp.zeros_like(acc_scr)

    u = jnp.maximum(jnp.dot(h_scr[...], w1_ref[...], preferred_element_type=F32), 0.0)
    acc_scr[...] += jnp.dot((u * u).astype(BF16), w2_ref[...], preferred_element_type=F32)

    @pl.when(f == pl.num_programs(1) - 1)
    def _():
        out_ref[...] = x1_scr[...] + _rms(acc_scr[...], g2_ref[...])


def _out_mlp(yml, ymb, yds, x2, w_out, g_post, g_pre, w1, w2, g2):
    T, D = x2.shape
    tm = min(MLP_TM, T)
    tf = min(MLP_TF, D_FF)
    row = lambda n: pl.BlockSpec((tm, n), lambda i, f: (i, 0))
    full = lambda a: pl.BlockSpec(a.shape, lambda i, f: (0,) * a.ndim)
    return pl.pallas_call(
        _out_mlp_kernel,
        grid=(T // tm, D_FF // tf),
        in_specs=[row(ML_V_W), row(MOBA_W), row(DSA_W), row(D), full(w_out), full(g_post), full(g_pre),
                  pl.BlockSpec((D, tf), lambda i, f: (0, f)),
                  pl.BlockSpec((tf, D), lambda i, f: (f, 0)),
                  full(g2)],
        out_specs=row(D),
        out_shape=jax.ShapeDtypeStruct((T, D), F32),
        scratch_shapes=[pltpu.VMEM((tm, D), F32), pltpu.VMEM((tm, D), BF16), pltpu.VMEM((tm, D), F32)],
        compiler_params=_params("parallel", "arbitrary"),
    )(yml, ymb, yds, x2, w_out, g_post, g_pre, w1, w2, g2)


def kernel(x, norm_mix_pre, w_in, ml_conv, ml_i_bias, ml_f_bias, ml_head_norm, w_out, norm_mix_post,
           norm_mlp_pre, w_ff1, w_ff2, norm_mlp_post):
    B, S, D = x.shape
    depth = w_in.shape[0]
    x2 = x.reshape(B * S, D)
    gain = lambda g: g.reshape(1, D).astype(F32)
    for l in range(depth):
        wn, wt = _prep_in_weights(w_in[l])
        (qk, mbk, dsk, ixk, mbqT, mbvT, dsqT, dsvT, ixqT, mlvT, mloT, smT) = _in_proj(
            x2, gain(norm_mix_pre[l]), wn, wt, B, S)
        y_ml = _mlstm(qk, mlvT, mloT, smT, ml_conv[l], ml_i_bias[l], ml_f_bias[l], ml_head_norm[l], B, S)
        y_mb = _moba(mbqT, mbk, mbvT, B, S)
        y_ds = _dsa(dsqT, dsk, dsvT, ixqT, ixk, smT, B, S)
        x2 = _out_mlp(y_ml.reshape(B * S, ML_V_W), y_mb.reshape(B * S, MOBA_W), y_ds.reshape(B * S, DSA_W),
                      x2, w_out[l].astype(BF16), gain(norm_mix_post[l]), gain(norm_mlp_pre[l]),
                      w_ff1[l].astype(BF16), w_ff2[l].astype(BF16), gain(norm_mlp_post[l]))
    return x2.reshape(B, S, D)
```

```python
import functools

import jax
import jax.numpy as jnp
from jax import lax
from jax.experimental import pallas as pl
from jax.experimental.pallas import tpu as pltpu

F32 = jnp.float32
BF16 = jnp.bfloat16
I32 = jnp.int32

D_MODEL = 1024
ML_HEADS, ML_DQK, ML_DV, ML_CONV = 4, 64, 128, 4
MOBA_HEADS, MOBA_DH, MOBA_BLOCK, MOBA_TOPK = 4, 64, 256, 3
DSA_HEADS, DSA_DH, IDX_HEADS, IDX_DH, DSA_TOPK_MAX = 4, 64, 4, 64, 256
D_FF = 4 * D_MODEL
EPS = 1e-6

ML_QK_W = ML_HEADS * ML_DQK
ML_V_W = ML_HEADS * ML_DV
MOBA_W = MOBA_HEADS * MOBA_DH
DSA_W = DSA_HEADS * DSA_DH
IN_SPLITS = (ML_QK_W, ML_QK_W, ML_V_W, ML_V_W, ML_HEADS, ML_HEADS,
             MOBA_W, MOBA_W, MOBA_W,
             DSA_W, DSA_W, DSA_W, IDX_HEADS * IDX_DH, IDX_DH, IDX_HEADS)

LANES = 128
SUBLANES = 8
VMEM_LIMIT = 52 * 1024 * 1024

ML_CHUNK = 512
ATT_TQ = 256
ATT_STAGES = (8, 4, 2, 1)
ATT_STAGE = ATT_STAGES[0]
PROJ_TM = 1024
MLP_TM = 1024
MLP_TF = 1024
MLP_SLABS = 4

NEG_INF = float("-inf")
BISECT_CAP = 300
BISECT_WARMUP = 12
BISECT_DROP = 2
BISECT_STEPS_PER_CHECK = 2
SOFTMAX_L_MIN = 1e-30
assert MOBA_DH == DSA_DH
ATT_QSCALE = MOBA_DH ** -0.5 * 1.4426950408889634

_NT = (((1,), (1,)), ((), ()))


def _chunk_loops(n, widths, make_body, carry, regroup=lambda c, w: c):
    covered = 0
    for i, w in enumerate(widths):
        if i:
            carry = regroup(carry, w)
        carry = lax.fori_loop(covered // w, n // w, make_body(w), carry)
        covered = w * (n // w)
    return carry


def _rms(x, g):
    return x * lax.rsqrt(jnp.mean(x * x, axis=-1, keepdims=True) + EPS) * g


def _log_sigmoid(x):
    return jnp.minimum(x, 0.0) - jnp.log1p(jnp.exp(-jnp.abs(x)))


def _params(*sem):
    return pltpu.CompilerParams(dimension_semantics=sem, vmem_limit_bytes=VMEM_LIMIT)


_N_QK, _N_MBK, _N_DSK, _N_IXK = 0, 512, 768, 1024
_N_TOTAL = 1152
_T_MBQ, _T_MBV, _T_DSQ, _T_DSV, _T_IXQ, _T_MLV, _T_MLO, _T_SM = 0, 256, 512, 768, 1024, 1280, 1792, 2304
_T_TOTAL = 2320


def _in_proj_kernel(x_ref, g_ref, wn_ref, wt_ref,
                    qk_ref, mbk_ref, dsk_ref, ixk_ref,
                    mbqT_ref, mbvT_ref, dsqT_ref, dsvT_ref, ixqT_ref, mlvT_ref, mloT_ref, smT_ref):
    h = _rms(x_ref[...], g_ref[...]).astype(BF16)

    def mm(a, n):
        return jnp.dot(h, wn_ref[:, a:a + n], preferred_element_type=F32)

    def mt(a, n):
        return lax.dot_general(wt_ref[a:a + n, :], h, _NT, preferred_element_type=F32)

    qk_ref[...] = mm(_N_QK, 512)
    mbk_ref[...] = mm(_N_MBK, 256).astype(BF16)
    dsk_ref[...] = mm(_N_DSK, 256).astype(BF16)
    ixk_ref[...] = mm(_N_IXK, 128).astype(BF16)
    mbqT_ref[0] = (mt(_T_MBQ, 256) * ATT_QSCALE).astype(BF16)
    mbvT_ref[0] = mt(_T_MBV, 256).astype(BF16)
    dsqT_ref[0] = (mt(_T_DSQ, 256) * ATT_QSCALE).astype(BF16)
    dsvT_ref[0] = mt(_T_DSV, 256).astype(BF16)
    ixqT_ref[0] = mt(_T_IXQ, 256).astype(BF16)
    mlvT_ref[0] = mt(_T_MLV, 512).astype(BF16)
    mloT_ref[0] = mt(_T_MLO, 512)
    smT_ref[0] = mt(_T_SM, 16)


def _prep_in_weights(w_in):
    pts, acc = [], 0
    for n in IN_SPLITS:
        pts.append((acc, acc + n))
        acc += n
    col = lambda i: w_in[:, pts[i][0]:pts[i][1]]
    (ml_q, ml_k, ml_v, ml_o, ml_i, ml_f, mb_q, mb_k, mb_v,
     ds_q, ds_k, ds_v, ix_q, ix_k, ix_w) = [col(i) for i in range(len(IN_SPLITS))]
    d = w_in.shape[0]
    small = jnp.concatenate([ml_i, ml_f, ix_w], axis=1)
    wn = jnp.concatenate([
        ml_q, ml_k, mb_k, ds_k,
        ix_k, jnp.zeros((d, 128 - IDX_DH), w_in.dtype)], axis=1)
    wt = jnp.concatenate([
        mb_q, mb_v, ds_q, ds_v, ix_q, ml_v, ml_o,
        small, jnp.zeros((d, 16 - small.shape[1]), w_in.dtype)], axis=1).T
    assert wn.shape == (d, _N_TOTAL) and wt.shape == (_T_TOTAL, d)
    return wn.astype(BF16), wt.astype(BF16)


def _in_proj(x2, g, wn, wt, B, S):
    T, D = x2.shape
    tm = min(PROJ_TM, S)
    nsb = S // tm
    row = lambda n: pl.BlockSpec((tm, n), lambda i: (i, 0))
    tr = lambda n: pl.BlockSpec((1, n, tm), lambda i: (i // nsb, 0, i % nsb))
    full = lambda a: pl.BlockSpec(a.shape, lambda i: (0,) * a.ndim)
    out_shape = (
        jax.ShapeDtypeStruct((T, 512), F32),
        jax.ShapeDtypeStruct((T, 256), BF16),
        jax.ShapeDtypeStruct((T, 256), BF16),
        jax.ShapeDtypeStruct((T, 128), BF16),
        jax.ShapeDtypeStruct((B, 256, S), BF16),
        jax.ShapeDtypeStruct((B, 256, S), BF16),
        jax.ShapeDtypeStruct((B, 256, S), BF16),
        jax.ShapeDtypeStruct((B, 256, S), BF16),
        jax.ShapeDtypeStruct((B, 256, S), BF16),
        jax.ShapeDtypeStruct((B, 512, S), BF16),
        jax.ShapeDtypeStruct((B, 512, S), F32),
        jax.ShapeDtypeStruct((B, 16, S), F32),
    )
    out_specs = (row(512), row(256), row(256), row(128),
                 tr(256), tr(256), tr(256), tr(256), tr(256), tr(512), tr(512), tr(16))
    return pl.pallas_call(
        _in_proj_kernel,
        grid=(T // tm,),
        in_specs=[row(D), full(g), full(wn), full(wt)],
        out_specs=out_specs,
        out_shape=out_shape,
        compiler_params=_params("parallel"),
    )(x2, g, wn, wt)


def _dot3(a_f32, b_exact):
    b = b_exact.astype(BF16)
    hi = a_f32.astype(BF16)
    r1 = a_f32 - hi.astype(F32)
    mid = r1.astype(BF16)
    lo = (r1 - mid.astype(F32)).astype(BF16)
    return (jnp.dot(hi, b, preferred_element_type=F32) + jnp.dot(mid, b, preferred_element_type=F32)
            + jnp.dot(lo, b, preferred_element_type=F32))


def _mlstm_kernel(qk_ref, vT_ref, oT_ref, smT_ref, conv_ref, bcol_ref, hn_ref,
                  y_ref, xp_scr, st_scr, m_scr, *, L):
    c = pl.program_id(1)
    halo = SUBLANES
    DK, DV = ML_DQK, ML_DV

    @pl.when(c == 0)
    def _():
        xp_scr[0:halo, :] = jnp.zeros((halo, 2 * ML_QK_W), F32)
        st_scr[...] = jnp.zeros_like(st_scr)
        m_scr[...] = jnp.zeros_like(m_scr)

    cur = qk_ref[0]
    xp_scr[halo:halo + L, :] = cur
    base = halo - (ML_CONV - 1)
    acc = conv_ref[0:1, :] * xp_scr[base:base + L, :]
    for j in range(1, ML_CONV):
        acc = acc + conv_ref[j:j + 1, :] * xp_scr[base + j:base + j + L, :]
    xp_scr[0:halo, :] = cur[L - halo:L, :]
    qk = acc * jax.nn.sigmoid(acc)
    qT_all = (qk[:, :ML_QK_W] * (DK ** -0.5)).T.astype(BF16)
    k_all = qk[:, ML_QK_W:].astype(BF16)

    grow = smT_ref[0][0:SUBLANES, :] + bcol_ref[...]
    si = lax.broadcasted_iota(I32, (L, L), 0)
    ji = lax.broadcasted_iota(I32, (L, L), 1)
    causal = si <= ji
    b_row = _dot3(_log_sigmoid(grow), causal)

    for h in range(ML_HEADS):
        qT = qT_all[h * DK:(h + 1) * DK, :]
        kh = k_all[:, h * DK:(h + 1) * DK]
        vT = vT_ref[0, h * DV:(h + 1) * DV, :]
        b_j = b_row[ML_HEADS + h:ML_HEADS + h + 1, :]
        c_row = grow[h:h + 1, :] - b_j
        c_col = jnp.broadcast_to(c_row, (LANES, L)).T
        dlog = jnp.where(causal, b_j + jnp.concatenate([c_col] * (L // LANES), axis=1), NEG_INF)
        m0 = m_scr[h:h + 1, 0:1]
        inter = b_j + m0
        m_t = jnp.maximum(inter, jnp.max(dlog, axis=0, keepdims=True))
        w_inter = jnp.exp(inter - m_t)
        w_intra = jnp.exp(dlog - m_t) * jnp.dot(kh, qT, preferred_element_type=F32)
        st = st_scr[h]
        cq = jnp.dot(st.astype(BF16), qT, preferred_element_type=F32)
        num = w_inter * cq[:DV] + jnp.dot(vT, w_intra.astype(BF16), preferred_element_type=F32)
        den = w_inter * cq[DV:DV + 1] + jnp.sum(w_intra, axis=0, keepdims=True)
        hh = num / jnp.maximum(jnp.abs(den), jnp.exp(-m_t))
        hh = hh * lax.rsqrt(jnp.mean(hh * hh, axis=0, keepdims=True) + EPS)
        rows = slice(h * DV, (h + 1) * DV)
        y_ref[0, :, rows] = (jax.nn.sigmoid(oT_ref[0, rows, :]) * hh * hn_ref[rows, :]).T.astype(y_ref.dtype)

        b_last = b_j[:, L - 1:L]
        a = b_last + c_row
        m_loc = jnp.max(a, axis=1, keepdims=True)
        wa = jnp.exp(a - m_loc)
        lhs = jnp.concatenate([(vT.astype(F32) * wa).astype(BF16),
                               jnp.broadcast_to(wa, (SUBLANES, L)).astype(BF16)], axis=0)
        c_loc = jnp.dot(lhs, kh, preferred_element_type=F32)
        m_new = jnp.maximum(b_last + m0, m_loc)
        s_old = jnp.exp(b_last + m0 - m_new)
        s_loc = jnp.exp(m_loc - m_new)
        st_scr[h] = s_old * st + s_loc * c_loc
        m_scr[h:h + 1, :] = jnp.broadcast_to(m_new, (1, LANES))


def _mlstm(qk, vT, oT, smT, conv_w, i_bias, f_bias, head_norm, B, S):
    L = min(ML_CHUNK, S)
    assert L % LANES == 0
    nc = S // L
    qk = qk.reshape(B, S, 2 * ML_QK_W)
    bcol = jnp.concatenate([i_bias, f_bias]).astype(F32).reshape(2 * ML_HEADS, 1)
    hn = jnp.broadcast_to(head_norm.astype(F32)[:, None], (ML_V_W, L))
    blkT = lambda n: pl.BlockSpec((1, n, L), lambda b, c: (b, 0, c))
    full = lambda a: pl.BlockSpec(a.shape, lambda b, c: (0,) * a.ndim)
    return pl.pallas_call(
        functools.partial(_mlstm_kernel, L=L),
        grid=(B, nc),
        in_specs=[pl.BlockSpec((1, L, 2 * ML_QK_W), lambda b, c: (b, c, 0)),
                  blkT(ML_V_W), blkT(ML_V_W), blkT(16), full(conv_w), full(bcol), full(hn)],
        out_specs=pl.BlockSpec((1, L, ML_V_W), lambda b, c: (b, c, 0)),
        out_shape=jax.ShapeDtypeStruct((B, S, ML_V_W), BF16),
        scratch_shapes=[pltpu.VMEM((L + SUBLANES, 2 * ML_QK_W), F32),
                        pltpu.VMEM((ML_HEADS, ML_DV + SUBLANES, ML_DQK), F32),
                        pltpu.VMEM((SUBLANES, LANES), F32)],
        compiler_params=_params("parallel", "arbitrary"),
    )(qk, vT, oT, smT, conv_w.astype(F32), bcol, hn)


def _block_diag_T(qT, heads, dh):
    rid = lax.broadcasted_iota(I32, qT.shape, 0)
    zero = jnp.zeros_like(qT)
    return jnp.concatenate(
        [jnp.where((rid >= h * dh) & (rid < (h + 1) * dh), qT, zero) for h in range(heads)], axis=1)


def _accumulate(h, p, vT_chunk, l, acc_scr, dh):
    rows = slice(h * dh, (h + 1) * dh)
    acc_scr[rows, :] += jnp.dot(vT_chunk[rows, :], p.astype(BF16), preferred_element_type=F32)
    return l + jnp.sum(p, axis=0, keepdims=True)


def _finish_attention(l, acc_scr, y_ref, heads, dh):
    outs = [acc_scr[h * dh:(h + 1) * dh, :] / l[h] for h in range(heads)]
    y_ref[0] = jnp.concatenate(outs, axis=0).T.astype(y_ref.dtype)


def _max_key_norm2(kb, heads, dh, TQ):
    ch = lax.broadcasted_iota(I32, (heads * dh, LANES), 0) // dh
    hd = lax.broadcasted_iota(I32, (heads * dh, LANES), 1)
    n2 = jnp.dot((kb * kb).astype(BF16), (ch == hd).astype(BF16), preferred_element_type=F32)
    n2 = jnp.max(n2, axis=0, keepdims=True) * (1.0 + 2.0 ** -7)
    return jnp.concatenate([jnp.broadcast_to(n2[:, h:h + 1], (1, TQ)) for h in range(heads)], axis=1)


def _norm_bound(qTf, k2, heads, dh, TQ):
    out = []
    for h in range(heads):
        qh = qTf[h * dh:(h + 1) * dh, :]
        q2 = jnp.sum(qh * qh, axis=0, keepdims=True)
        out.append(jnp.sqrt(q2 * k2[:, h * TQ:(h + 1) * TQ]) * (1.0 + 2.0 ** -8))
    return out


def _underflowed(l):
    small = l[0]
    for lh in l[1:]:
        small = jnp.minimum(small, lh)
    return jnp.logical_not(jnp.min(small) >= SOFTMAX_L_MIN)


def _moba_kernel(qT_ref, k_ref, vT_ref, y_ref, km_scr, kn_scr, acc_scr, s_scr, *, NB, NBP, n_sel):
    H, DH, BS = MOBA_HEADS, MOBA_DH, MOBA_BLOCK
    TQ = BS
    qi = pl.program_id(1)

    @pl.when(qi == 0)
    def _():
        km_scr[...] = jnp.zeros_like(km_scr)
        kn_scr[...] = jnp.zeros_like(kn_scr)

        def body(j, carry):
            kb = k_ref[0, pl.ds(pl.multiple_of(j * BS, BS), BS), :].astype(F32)
            km_scr[pl.ds(j, 1), :] = jnp.mean(kb, axis=0, keepdims=True)
            kn_scr[pl.ds(j, 1), :] = _max_key_norm2(kb, H, DH, TQ)
            return carry
        lax.fori_loop(0, NB, body, 0)

    qT = qT_ref[0]
    qTf = qT.astype(F32)
    km = km_scr[...]
    gate = jnp.concatenate(
        [jnp.dot(km[:, h * DH:(h + 1) * DH], qTf[h * DH:(h + 1) * DH, :], precision=lax.Precision.HIGHEST,
                 preferred_element_type=F32) for h in range(H)], axis=1) * (1.0 / ATT_QSCALE)
    blk = lax.broadcasted_iota(I32, gate.shape, 0)
    g = jnp.where(blk < qi, gate, NEG_INF)
    sels = []
    for _ in range(n_sel):
        mx = jnp.max(g, axis=0, keepdims=True)
        isel = jnp.min(jnp.where(g == mx, blk, NBP), axis=0, keepdims=True)
        sels.append(jnp.where(mx > NEG_INF, isel, -1))
        g = jnp.where(blk == isel, NEG_INF, g)

    qs = _block_diag_T(qT, H, DH)

    def hit_row(j):
        hit = sels[0] == j
        for s in sels[1:]:
            hit = hit | (s == j)
        return hit

    blk_start = lambda j: pl.multiple_of(j * BS, BS)
    blk_scores = lambda j: jnp.dot(k_ref[0, pl.ds(blk_start(j), BS), :], qs, preferred_element_type=F32)
    hcols = lambda row, h: row[:, h * TQ:(h + 1) * TQ]
    causal = lax.broadcasted_iota(I32, (BS, TQ), 0) <= lax.broadcasted_iota(I32, (BS, TQ), 1)

    def diag_scores():
        s_all = blk_scores(qi)
        return [jnp.where(causal, hcols(s_all, h), NEG_INF) for h in range(H)]

    def max_sweep():
        def max_block(j, m):
            return jnp.maximum(m, jnp.where(hit_row(j), jnp.max(blk_scores(j), axis=0, keepdims=True), NEG_INF))
        m_all = lax.fori_loop(0, qi, max_block, jnp.full((1, H * TQ), NEG_INF, F32))
        s_diag = diag_scores()
        return [jnp.maximum(hcols(m_all, h), jnp.max(s_diag[h], axis=0, keepdims=True)) for h in range(H)]

    def sum_sweep(m):
        acc_scr[...] = jnp.zeros_like(acc_scr)

        def sum_group(width):
            def body(g, l):
                for u in range(width):
                    s_scr[u] = blk_scores(width * g + u)
                for u in range(width):
                    j = width * g + u
                    hit = hit_row(j)
                    vT_blk = vT_ref[0, :, pl.ds(blk_start(j), BS)]
                    l = tuple(_accumulate(h, jnp.exp2(s_scr[u, :, h * TQ:(h + 1) * TQ]
                                                      - jnp.where(hcols(hit, h), m[h], jnp.inf)),
                                          vT_blk, l[h], acc_scr, DH) for h in range(H))
                return l
            return body
        l = _chunk_loops(qi, ATT_STAGES, sum_group, (jnp.zeros((1, TQ), F32),) * H)
        s_diag = diag_scores()
        vT_diag = vT_ref[0, :, pl.ds(blk_start(qi), BS)]
        return [_accumulate(h, jnp.exp2(s_diag[h] - m[h]), vT_diag, l[h], acc_scr, DH) for h in range(H)]

    blk_n = lax.broadcasted_iota(I32, kn_scr.shape, 0)
    k2 = jnp.max(jnp.where(blk_n <= qi, kn_scr[...], 0.0), axis=0, keepdims=True)
    l = sum_sweep(_norm_bound(qTf, k2, H, DH, TQ))
    _finish_attention(l, acc_scr, y_ref, H, DH)

    @pl.when(_underflowed(l))
    def _():
        _finish_attention(sum_sweep(max_sweep()), acc_scr, y_ref, H, DH)


def _moba(qT, k, vT, B, S):
    BS = MOBA_BLOCK
    assert S % BS == 0
    NB = S // BS
    NBP = -(-NB // SUBLANES) * SUBLANES
    n_sel = max(min(MOBA_TOPK, NB - 1), 1)
    k = k.reshape(B, S, MOBA_W)
    return pl.pallas_call(
        functools.partial(_moba_kernel, NB=NB, NBP=NBP, n_sel=n_sel),
        grid=(B, NB),
        in_specs=[pl.BlockSpec((1, MOBA_W, BS), lambda b, i: (b, 0, i)),
                  pl.BlockSpec((1, S, MOBA_W), lambda b, i: (b, 0, 0)),
                  pl.BlockSpec((1, MOBA_W, S), lambda b, i: (b, 0, 0))],
        out_specs=pl.BlockSpec((1, BS, MOBA_W), lambda b, i: (b, i, 0)),
        out_shape=jax.ShapeDtypeStruct((B, S, MOBA_W), BF16),
        scratch_shapes=[pltpu.VMEM((NBP, MOBA_W), F32),
                        pltpu.VMEM((NBP, MOBA_HEADS * BS), F32),
                        pltpu.VMEM((MOBA_W, BS), F32),
                        pltpu.VMEM((ATT_STAGE, BS, MOBA_HEADS * BS), F32)],
        compiler_params=_params("parallel", "arbitrary"),
    )(qT, k, vT)


def _dsa_kernel(qT_ref, k_ref, vT_ref, iqT_ref, ik_ref, smT_ref, y_ref,
                e_scr, gmax_scr, acc_scr, thr_scr, kn_scr, s_scr, *, TQ, topk, spow, n_chunks):
    H, DH = DSA_HEADS, DSA_DH
    KC = TQ
    qi = pl.program_id(1)
    nch = qi + 1
    chunk_start = lambda c: pl.multiple_of(c * KC, KC)

    @pl.when(qi == 0)
    def _():
        kn_scr[...] = jnp.zeros_like(kn_scr)

        def body(c, carry):
            kn_scr[pl.ds(c, 1), :] = _max_key_norm2(k_ref[0, pl.ds(chunk_start(c), KC), :].astype(F32), H, DH, TQ)
            return carry
        lax.fori_loop(0, n_chunks, body, 0)
    sum8 = lambda b: jnp.sum(b.astype(I32).reshape(KC // SUBLANES, SUBLANES, TQ), axis=0)
    any_row = lambda r: jnp.max(r.astype(I32)) > 0

    iqT = iqT_ref[0]
    iq_cat = jnp.concatenate([iqT[h * IDX_DH:(h + 1) * IDX_DH, :] for h in range(IDX_HEADS)], axis=1)
    iw = smT_ref[0][2 * ML_HEADS:2 * ML_HEADS + IDX_HEADS, :] * (IDX_HEADS ** -0.5 * IDX_DH ** -0.5)
    t_pos = qi * TQ + lax.broadcasted_iota(I32, (KC, TQ), 1)
    s_off = lax.broadcasted_iota(I32, (KC, TQ), 0)
    gmax_scr[...] = jnp.full((KC, TQ), NEG_INF, F32)

    def score_group(width, masked=False):
        def body(g, tally):
            for u in range(width):
                ikc = ik_ref[0, pl.ds(chunk_start(width * g + u), KC), :][:, :IDX_DH]
                s_scr[u] = jnp.dot(ikc, iq_cat, preferred_element_type=F32)
            for u in range(width):
                c = width * g + u
                sc = iw[0:1, :] * jnp.maximum(s_scr[u, :, 0:TQ], 0.0)
                for h in range(1, IDX_HEADS):
                    sc = sc + iw[h:h + 1, :] * jnp.maximum(s_scr[u, :, h * TQ:(h + 1) * TQ], 0.0)
                if masked:
                    sc = jnp.where(c * KC + s_off <= t_pos, sc, NEG_INF)
                e_scr[pl.ds(chunk_start(c), KC), :] = sc
                gmax_scr[...] = jnp.maximum(gmax_scr[...], sc)
                tally = tally + jnp.sum(jnp.where(sc > 0.0, 1, jnp.where(sc == 0.0, 1 << 16, 0))
                                        .reshape(KC // SUBLANES, SUBLANES, TQ), axis=0)
            return tally
        return body
    assert n_chunks * KC < (1 << 16)
    zero8 = jnp.zeros((SUBLANES, TQ), I32)
    tally = _chunk_loops(qi, ATT_STAGES, score_group, zero8)
    tally = jnp.sum(score_group(1, True)(qi, tally), axis=0, keepdims=True)
    n_pos = tally & 0xFFFF
    n_nonneg = n_pos + (tally >> 16)

    def sweep_keys(fold, merge, init):
        def group(width):
            def body(g, acc):
                return tuple(merge(acc[u], fold(e_scr[pl.ds(chunk_start(width * g + u), KC), :]))
                             for u in range(width))
            return body

        def regroup(acc, width):
            half = len(acc) // 2
            acc = tuple(merge(acc[u], acc[u + half]) for u in range(half))
            return acc if half == width else regroup(acc, width)
        return _chunk_loops(nch, ATT_STAGES, group, (init,) * ATT_STAGES[0], regroup)[0]

    def count_gt(cand):
        return jnp.sum(sweep_keys(lambda x: sum8(x > cand), lambda p, q: p + q, zero8), axis=0, keepdims=True)

    def min_gt(cand):
        fold = lambda x: jnp.min(jnp.where(x > cand, x, jnp.inf).reshape(KC // SUBLANES, SUBLANES, TQ), axis=0)
        return jnp.min(sweep_keys(fold, jnp.minimum, jnp.full((SUBLANES, TQ), jnp.inf, F32)), axis=0, keepdims=True)

    def rewrite(rows, thr):
        def body(c, carry):
            x = e_scr[pl.ds(chunk_start(c), KC), :]
            rank = (2 * spow - (c * KC + s_off)).astype(F32)
            new = jnp.where(x > thr, jnp.inf, jnp.where(x == thr, rank, NEG_INF))
            e_scr[pl.ds(chunk_start(c), KC), :] = jnp.where(rows, new, x)
            return carry
        lax.fori_loop(0, nch, body, 0)

    def halve(_, c):
        lo, hi, act, tie, n_lo = c
        mid = lo + (hi - lo) * 0.5
        inside = (mid > lo) & (mid < hi)
        cnt = count_gt(mid)
        live = act > 0.0
        run = live & inside
        up = run & (cnt >= topk)
        lo = jnp.where(up, mid, lo)
        n_lo = jnp.where(up, cnt, n_lo)
        hi = jnp.where(run & (cnt < topk), mid, hi)
        collapsed = live & jnp.logical_not(inside)
        tie = jnp.where(collapsed, 1.0, tie)
        act = jnp.where(collapsed | (run & (cnt == topk)), 0.0, act)
        return lo, hi, act, tie, n_lo

    def excess(state):
        return jnp.max(jnp.where(state[2] > 0.0, state[4] - topk, 0).astype(F32))

    def bisect(lo, hi, act, warmup):
        state = (lo, hi, act, jnp.zeros_like(act), jnp.full((1, TQ), 1 << 20, I32))
        state = lax.fori_loop(0, warmup, halve, state)

        def one_more(c):
            state = halve(0, c[1])
            return c[0] + 1, state, excess(state)
        near = lax.while_loop(lambda c: jnp.logical_and(c[0] < BISECT_CAP, c[2] > BISECT_DROP), one_more,
                              (jnp.int32(warmup), state, excess(state)))
        lo, hi, act, tie, n_lo = near[1]
        cur, left = lo, jnp.where(act > 0.0, n_lo - topk, 0)
        for _ in range(BISECT_DROP):
            cur = jnp.where(left > 0, min_gt(cur), cur)
            left = left - 1
        done = (act > 0.0) & (count_gt(cur) == topk)
        lo = jnp.where(done, cur, lo)
        act = jnp.where(done, 0.0, act)

        def body(c):
            state = c[1]
            for _ in range(BISECT_STEPS_PER_CHECK):
                state = halve(0, state)
            return c[0] + BISECT_STEPS_PER_CHECK, state, jnp.max(state[2])
        out = lax.while_loop(lambda c: jnp.logical_and(c[0] < BISECT_CAP, c[2] > 0.0), body,
                             (near[0], (lo, hi, act, tie, n_lo), jnp.max(act)))
        return out[1][0], out[1][1], out[1][3]

    fmax = float(jnp.finfo(F32).max)
    gm = gmax_scr[...]
    g_lo = jnp.min(gm, axis=0, keepdims=True)
    g_hi = jnp.max(gm, axis=0, keepdims=True)
    n_vis = qi * TQ + lax.broadcasted_iota(I32, (1, TQ), 1) + 1
    few = n_vis <= topk
    zero_tie = (n_pos < topk) & (n_nonneg >= topk) & jnp.logical_not(few)

    @pl.when(any_row(zero_tie))
    def _():
        rewrite(zero_tie, jnp.zeros((1, TQ), F32))

    rank_lo, rank_hi = float(spow), float(2 * spow + 1)
    below = jnp.maximum(g_lo, -fmax)
    below = below - jnp.abs(below) * (2.0 ** -10) - 1e-30
    lo0 = jnp.where(zero_tie, rank_lo, jnp.maximum(below, -fmax))
    hi0 = jnp.where(zero_tie, rank_hi, g_hi)
    lo, hi, tie = bisect(lo0, hi0, jnp.where(few, 0.0, 1.0), BISECT_WARMUP)
    thr_scr[0:1, :] = jnp.where(few, NEG_INF, lo)

    @pl.when(jnp.max(tie) > 0.0)
    def _():
        rows = tie > 0.0
        rewrite(rows, hi)
        lo2, _, _ = bisect(jnp.full((1, TQ), rank_lo, F32), jnp.full((1, TQ), rank_hi, F32), tie, 0)
        thr_scr[0:1, :] = jnp.where(rows, lo2, thr_scr[0:1, :])

    thr = thr_scr[0:1, :]
    qT = qT_ref[0]
    qs = _block_diag_T(qT, H, DH)
    hcols = lambda row, h: row[:, h * TQ:(h + 1) * TQ]

    def chunk_scores(c):
        start = chunk_start(c)
        sel = e_scr[pl.ds(start, KC), :] > thr
        s_all = jnp.dot(k_ref[0, pl.ds(start, KC), :], qs, preferred_element_type=F32)
        return sel, s_all

    def max_sweep():
        def max_chunk(c, m):
            sel, s_all = chunk_scores(c)
            return tuple(jnp.maximum(m[h], jnp.max(jnp.where(sel, hcols(s_all, h), NEG_INF), axis=0, keepdims=True))
                         for h in range(H))
        m = lax.fori_loop(0, nch, max_chunk, (jnp.full((1, TQ), NEG_INF, F32),) * H)
        return [jnp.where(mh == NEG_INF, 0.0, mh) for mh in m]

    def sum_sweep(m):
        acc_scr[...] = jnp.zeros_like(acc_scr)

        def sum_group(width):
            def body(g, l):
                for u in range(width):
                    s_scr[u] = jnp.dot(k_ref[0, pl.ds(chunk_start(width * g + u), KC), :], qs,
                                       preferred_element_type=F32)
                for u in range(width):
                    start = chunk_start(width * g + u)
                    sel = e_scr[pl.ds(start, KC), :] > thr
                    vT_chunk = vT_ref[0, :, pl.ds(start, KC)]
                    l = tuple(_accumulate(h, jnp.exp2(s_scr[u, :, h * TQ:(h + 1) * TQ] + jnp.where(sel, -m[h], NEG_INF)),
                                          vT_chunk, l[h], acc_scr, DH) for h in range(H))
                return l
            return body
        return _chunk_loops(nch, ATT_STAGES, sum_group, (jnp.zeros((1, TQ), F32),) * H)

    chunk_n = lax.broadcasted_iota(I32, kn_scr.shape, 0)
    k2 = jnp.max(jnp.where(chunk_n < nch, kn_scr[...], 0.0), axis=0, keepdims=True)
    l = sum_sweep(_norm_bound(qT.astype(F32), k2, H, DH, TQ))
    _finish_attention(l, acc_scr, y_ref, H, DH)

    @pl.when(_underflowed(l))
    def _():
        _finish_attention(sum_sweep(max_sweep()), acc_scr, y_ref, H, DH)


def _dsa(qT, k, vT, iqT, ik, smT, B, S):
    TQ = min(ATT_TQ, S)
    topk = min(DSA_TOPK_MAX, S // 4)
    assert S % TQ == 0 and topk <= TQ
    spow = 1 << max((S - 1).bit_length(), 1)
    k = k.reshape(B, S, DSA_W)
    ik = ik.reshape(B, S, LANES)
    qblk = lambda n: pl.BlockSpec((1, n, TQ), lambda b, i: (b, 0, i))
    return pl.pallas_call(
        functools.partial(_dsa_kernel, TQ=TQ, topk=topk, spow=spow, n_chunks=S // TQ),
        grid=(B, S // TQ),
        in_specs=[qblk(DSA_W),
                  pl.BlockSpec((1, S, DSA_W), lambda b, i: (b, 0, 0)),
                  pl.BlockSpec((1, DSA_W, S), lambda b, i: (b, 0, 0)),
                  qblk(IDX_HEADS * IDX_DH),
                  pl.BlockSpec((1, S, LANES), lambda b, i: (b, 0, 0)),
                  qblk(16)],
        out_specs=pl.BlockSpec((1, TQ, DSA_W), lambda b, i: (b, i, 0)),
        out_shape=jax.ShapeDtypeStruct((B, S, DSA_W), BF16),
        scratch_shapes=[pltpu.VMEM((S, TQ), F32),
                        pltpu.VMEM((TQ, TQ), F32),
                        pltpu.VMEM((DSA_W, TQ), F32),
                        pltpu.VMEM((SUBLANES, TQ), F32),
                        pltpu.VMEM((-(-(S // TQ) // SUBLANES) * SUBLANES, DSA_HEADS * TQ), F32),
                        pltpu.VMEM((ATT_STAGE, TQ, DSA_HEADS * TQ), F32)],
        compiler_params=_params("parallel", "arbitrary"),
    )(qT, k, vT, iqT, ik, smT)


def _out_mlp_kernel(yml_ref, ymb_ref, yds_ref, x_ref, wout_ref, gpost_ref, gpre_ref,
                    w1_ref, w2_ref, g2_ref, out_ref, x1_scr, h_scr, acc_scr):
    f = pl.program_id(1)

    @pl.when(f == 0)
    def _():
        slab = x_ref.shape[0] // MLP_SLABS
        for r in range(MLP_SLABS):
            rows = slice(r * slab, (r + 1) * slab)
            mix = jnp.dot(yml_ref[rows, :], wout_ref[0:ML_V_W, :], preferred_element_type=F32)
            mix = mix + jnp.dot(ymb_ref[rows, :], wout_ref[ML_V_W:ML_V_W + MOBA_W, :], preferred_element_type=F32)
            mix = mix + jnp.dot(yds_ref[rows, :], wout_ref[ML_V_W + MOBA_W:, :], preferred_element_type=F32)
            x1 = x_ref[rows, :] + _rms(mix, gpost_ref[...])
            x1_scr[rows, :] = x1
            h_scr[rows, :] = _rms(x1, gpre_ref[...]).astype(BF16)
        acc_scr[...] = jnp.zeros_like(acc_scr)

    u = jnp.maximum(jnp.dot(h_scr[...], w1_ref[...], preferred_element_type=F32), 0.0)
    acc_scr[...] += jnp.dot((u * u).astype(BF16), w2_ref[...], preferred_element_type=F32)

    @pl.when(f == pl.num_programs(1) - 1)
    def _():
        out_ref[...] = x1_scr[...] + _rms(acc_scr[...], g2_ref[...])


def _out_mlp(yml, ymb, yds, x2, w_out, g_post, g_pre, w1, w2, g2):
    T, D = x2.shape
    tm = min(MLP_TM, T)
    tf = min(MLP_TF, D_FF)
    row = lambda n: pl.BlockSpec((tm, n), lambda i, f: (i, 0))
    full = lambda a: pl.BlockSpec(a.shape, lambda i, f: (0,) * a.ndim)
    return pl.pallas_call(
        _out_mlp_kernel,
        grid=(T // tm, D_FF // tf),
        in_specs=[row(ML_V_W), row(MOBA_W), row(DSA_W), row(D), full(w_out), full(g_post), full(g_pre),
                  pl.BlockSpec((D, tf), lambda i, f: (0, f)),
                  pl.BlockSpec((tf, D), lambda i, f: (f, 0)),
                  full(g2)],
        out_specs=row(D),
        out_shape=jax.ShapeDtypeStruct((T, D), F32),
        scratch_shapes=[pltpu.VMEM((tm, D), F32), pltpu.VMEM((tm, D), BF16), pltpu.VMEM((tm, D), F32)],
        compiler_params=_params("parallel", "arbitrary"),
    )(yml, ymb, yds, x2, w_out, g_post, g_pre, w1, w2, g2)


def kernel(x, norm_mix_pre, w_in, ml_conv, ml_i_bias, ml_f_bias, ml_head_norm, w_out, norm_mix_post,
           norm_mlp_pre, w_ff1, w_ff2, norm_mlp_post):
    B, S, D = x.shape
    depth = w_in.shape[0]
    x2 = x.reshape(B * S, D)
    gain = lambda g: g.reshape(1, D).astype(F32)
    for l in range(depth):
        wn, wt = _prep_in_weights(w_in[l])
        (qk, mbk, dsk, ixk, mbqT, mbvT, dsqT, dsvT, ixqT, mlvT, mloT, smT) = _in_proj(
            x2, gain(norm_mix_pre[l]), wn, wt, B, S)
        y_ml = _mlstm(qk, mlvT, mloT, smT, ml_conv[l], ml_i_bias[l], ml_f_bias[l], ml_head_norm[l], B, S)
        y_mb = _moba(mbqT, mbk, mbvT, B, S)
        y_ds = _dsa(dsqT, dsk, dsvT, ixqT, ixk, smT, B, S)
        x2 = _out_mlp(y_ml.reshape(B * S, ML_V_W), y_mb.reshape(B * S, MOBA_W), y_ds.reshape(B * S, DSA_W),
                      x2, w_out[l].astype(BF16), gain(norm_mix_post[l]), gain(norm_mlp_pre[l]),
                      w_ff1[l].astype(BF16), w_ff2[l].astype(BF16), gain(norm_mlp_post[l]))
    return x2.reshape(B, S, D)
```

```python
import functools

import jax
import jax.numpy as jnp
from jax import lax
from jax.experimental import pallas as pl
from jax.experimental.pallas import tpu as pltpu

F32 = jnp.float32
BF16 = jnp.bfloat16
I32 = jnp.int32

D_MODEL = 1024
ML_HEADS, ML_DQK, ML_DV, ML_CONV = 4, 64, 128, 4
MOBA_HEADS, MOBA_DH, MOBA_BLOCK, MOBA_TOPK = 4, 64, 256, 3
DSA_HEADS, DSA_DH, IDX_HEADS, IDX_DH, DSA_TOPK_MAX = 4, 64, 4, 64, 256
D_FF = 4 * D_MODEL
EPS = 1e-6

ML_QK_W = ML_HEADS * ML_DQK
ML_V_W = ML_HEADS * ML_DV
MOBA_W = MOBA_HEADS * MOBA_DH
DSA_W = DSA_HEADS * DSA_DH
IN_SPLITS = (ML_QK_W, ML_QK_W, ML_V_W, ML_V_W, ML_HEADS, ML_HEADS,
             MOBA_W, MOBA_W, MOBA_W,
             DSA_W, DSA_W, DSA_W, IDX_HEADS * IDX_DH, IDX_DH, IDX_HEADS)

LANES = 128
SUBLANES = 8
VMEM_LIMIT = 52 * 1024 * 1024

ML_CHUNK = 512
ATT_TQ = 256
ATT_STAGES = (8, 4, 2, 1)
ATT_STAGE = ATT_STAGES[0]
PROJ_TM = 1024
MLP_TM = 1024
MLP_TF = 2048
MLP_SLABS = 4

NEG_INF = float("-inf")
BISECT_CAP = 300
BISECT_WARMUP = 12
BISECT_DROP = 2
BISECT_STEPS_PER_CHECK = 2
SOFTMAX_L_MIN = 1e-30
assert MOBA_DH == DSA_DH
ATT_QSCALE = MOBA_DH ** -0.5 * 1.4426950408889634

_NT = (((1,), (1,)), ((), ()))


def _chunk_loops(n, widths, make_body, carry, regroup=lambda c, w: c):
    covered = 0
    for i, w in enumerate(widths):
        if i:
            carry = regroup(carry, w)
        carry = lax.fori_loop(covered // w, n // w, make_body(w), carry)
        covered = w * (n // w)
    return carry


def _rms(x, g):
    return x * lax.rsqrt(jnp.mean(x * x, axis=-1, keepdims=True) + EPS) * g


def _log_sigmoid(x):
    return jnp.minimum(x, 0.0) - jnp.log1p(jnp.exp(-jnp.abs(x)))


def _params(*sem):
    return pltpu.CompilerParams(dimension_semantics=sem, vmem_limit_bytes=VMEM_LIMIT)


_N_QK, _N_MBK, _N_DSK, _N_IXK = 0, 512, 768, 1024
_N_TOTAL = 1152
_T_MBQ, _T_MBV, _T_DSQ, _T_DSV, _T_IXQ, _T_MLV, _T_MLO, _T_SM = 0, 256, 512, 768, 1024, 1280, 1792, 2304
_T_TOTAL = 2320


def _in_proj_kernel(x_ref, g_ref, wn_ref, wt_ref,
                    qk_ref, mbk_ref, dsk_ref, ixk_ref,
                    mbqT_ref, mbvT_ref, dsqT_ref, dsvT_ref, ixqT_ref, mlvT_ref, mloT_ref, smT_ref):
    h = _rms(x_ref[...], g_ref[...]).astype(BF16)

    def mm(a, n):
        return jnp.dot(h, wn_ref[:, a:a + n], preferred_element_type=F32)

    def mt(a, n):
        return lax.dot_general(wt_ref[a:a + n, :], h, _NT, preferred_element_type=F32)

    qk_ref[...] = mm(_N_QK, 512)
    mbk_ref[...] = mm(_N_MBK, 256).astype(BF16)
    dsk_ref[...] = mm(_N_DSK, 256).astype(BF16)
    ixk_ref[...] = mm(_N_IXK, 128).astype(BF16)
    mbqT_ref[0] = (mt(_T_MBQ, 256) * ATT_QSCALE).astype(BF16)
    mbvT_ref[0] = mt(_T_MBV, 256).astype(BF16)
    dsqT_ref[0] = (mt(_T_DSQ, 256) * ATT_QSCALE).astype(BF16)
    dsvT_ref[0] = mt(_T_DSV, 256).astype(BF16)
    ixqT_ref[0] = mt(_T_IXQ, 256).astype(BF16)
    mlvT_ref[0] = mt(_T_MLV, 512).astype(BF16)
    mloT_ref[0] = mt(_T_MLO, 512)
    smT_ref[0] = mt(_T_SM, 16)


def _prep_in_weights(w_in):
    pts, acc = [], 0
    for n in IN_SPLITS:
        pts.append((acc, acc + n))
        acc += n
    col = lambda i: w_in[:, pts[i][0]:pts[i][1]]
    (ml_q, ml_k, ml_v, ml_o, ml_i, ml_f, mb_q, mb_k, mb_v,
     ds_q, ds_k, ds_v, ix_q, ix_k, ix_w) = [col(i) for i in range(len(IN_SPLITS))]
    d = w_in.shape[0]
    small = jnp.concatenate([ml_i, ml_f, ix_w], axis=1)
    wn = jnp.concatenate([
        ml_q, ml_k, mb_k, ds_k,
        ix_k, jnp.zeros((d, 128 - IDX_DH), w_in.dtype)], axis=1)
    wt = jnp.concatenate([
        mb_q, mb_v, ds_q, ds_v, ix_q, ml_v, ml_o,
        small, jnp.zeros((d, 16 - small.shape[1]), w_in.dtype)], axis=1).T
    assert wn.shape == (d, _N_TOTAL) and wt.shape == (_T_TOTAL, d)
    return wn.astype(BF16), wt.astype(BF16)


def _in_proj(x2, g, wn, wt, B, S):
    T, D = x2.shape
    tm = min(PROJ_TM, S)
    nsb = S // tm
    row = lambda n: pl.BlockSpec((tm, n), lambda i: (i, 0))
    tr = lambda n: pl.BlockSpec((1, n, tm), lambda i: (i // nsb, 0, i % nsb))
    full = lambda a: pl.BlockSpec(a.shape, lambda i: (0,) * a.ndim)
    out_shape = (
        jax.ShapeDtypeStruct((T, 512), F32),
        jax.ShapeDtypeStruct((T, 256), BF16),
        jax.ShapeDtypeStruct((T, 256), BF16),
        jax.ShapeDtypeStruct((T, 128), BF16),
        jax.ShapeDtypeStruct((B, 256, S), BF16),
        jax.ShapeDtypeStruct((B, 256, S), BF16),
        jax.ShapeDtypeStruct((B, 256, S), BF16),
        jax.ShapeDtypeStruct((B, 256, S), BF16),
        jax.ShapeDtypeStruct((B, 256, S), BF16),
        jax.ShapeDtypeStruct((B, 512, S), BF16),
        jax.ShapeDtypeStruct((B, 512, S), F32),
        jax.ShapeDtypeStruct((B, 16, S), F32),
    )
    out_specs = (row(512), row(256), row(256), row(128),
                 tr(256), tr(256), tr(256), tr(256), tr(256), tr(512), tr(512), tr(16))
    return pl.pallas_call(
        _in_proj_kernel,
        grid=(T // tm,),
        in_specs=[row(D), full(g), full(wn), full(wt)],
        out_specs=out_specs,
        out_shape=out_shape,
        compiler_params=_params("parallel"),
    )(x2, g, wn, wt)


def _dot3(a_f32, b_exact):
    b = b_exact.astype(BF16)
    hi = a_f32.astype(BF16)
    r1 = a_f32 - hi.astype(F32)
    mid = r1.astype(BF16)
    lo = (r1 - mid.astype(F32)).astype(BF16)
    return (jnp.dot(hi, b, preferred_element_type=F32) + jnp.dot(mid, b, preferred_element_type=F32)
            + jnp.dot(lo, b, preferred_element_type=F32))


def _mlstm_kernel(qk_ref, vT_ref, oT_ref, smT_ref, conv_ref, bcol_ref, hn_ref,
                  y_ref, xp_scr, st_scr, m_scr, *, L):
    c = pl.program_id(1)
    halo = SUBLANES
    DK, DV = ML_DQK, ML_DV

    @pl.when(c == 0)
    def _():
        xp_scr[0:halo, :] = jnp.zeros((halo, 2 * ML_QK_W), F32)
        st_scr[...] = jnp.zeros_like(st_scr)
        m_scr[...] = jnp.zeros_like(m_scr)

    cur = qk_ref[0]
    xp_scr[halo:halo + L, :] = cur
    base = halo - (ML_CONV - 1)
    acc = conv_ref[0:1, :] * xp_scr[base:base + L, :]
    for j in range(1, ML_CONV):
        acc = acc + conv_ref[j:j + 1, :] * xp_scr[base + j:base + j + L, :]
    xp_scr[0:halo, :] = cur[L - halo:L, :]
    qk = acc * jax.nn.sigmoid(acc)
    qT_all = (qk[:, :ML_QK_W] * (DK ** -0.5)).T.astype(BF16)
    k_all = qk[:, ML_QK_W:].astype(BF16)

    grow = smT_ref[0][0:SUBLANES, :] + bcol_ref[...]
    si = lax.broadcasted_iota(I32, (L, L), 0)
    ji = lax.broadcasted_iota(I32, (L, L), 1)
    causal = si <= ji
    b_row = _dot3(_log_sigmoid(grow), causal)

    for h in range(ML_HEADS):
        qT = qT_all[h * DK:(h + 1) * DK, :]
        kh = k_all[:, h * DK:(h + 1) * DK]
        vT = vT_ref[0, h * DV:(h + 1) * DV, :]
        b_j = b_row[ML_HEADS + h:ML_HEADS + h + 1, :]
        c_row = grow[h:h + 1, :] - b_j
        c_col = jnp.broadcast_to(c_row, (LANES, L)).T
        dlog = jnp.where(causal, b_j + jnp.concatenate([c_col] * (L // LANES), axis=1), NEG_INF)
        m0 = m_scr[h:h + 1, 0:1]
        inter = b_j + m0
        m_t = jnp.maximum(inter, jnp.max(dlog, axis=0, keepdims=True))
        w_inter = jnp.exp(inter - m_t)
        w_intra = jnp.exp(dlog - m_t) * jnp.dot(kh, qT, preferred_element_type=F32)
        st = st_scr[h]
        cq = jnp.dot(st.astype(BF16), qT, preferred_element_type=F32)
        num = w_inter * cq[:DV] + jnp.dot(vT, w_intra.astype(BF16), preferred_element_type=F32)
        den = w_inter * cq[DV:DV + 1] + jnp.sum(w_intra, axis=0, keepdims=True)
        hh = num / jnp.maximum(jnp.abs(den), jnp.exp(-m_t))
        hh = hh * lax.rsqrt(jnp.mean(hh * hh, axis=0, keepdims=True) + EPS)
        rows = slice(h * DV, (h + 1) * DV)
        y_ref[0, :, rows] = (jax.nn.sigmoid(oT_ref[0, rows, :]) * hh * hn_ref[rows, :]).T.astype(y_ref.dtype)

        b_last = b_j[:, L - 1:L]
        a = b_last + c_row
        m_loc = jnp.max(a, axis=1, keepdims=True)
        wa = jnp.exp(a - m_loc)
        lhs = jnp.concatenate([(vT.astype(F32) * wa).astype(BF16),
                               jnp.broadcast_to(wa, (SUBLANES, L)).astype(BF16)], axis=0)
        c_loc = jnp.dot(lhs, kh, preferred_element_type=F32)
        m_new = jnp.maximum(b_last + m0, m_loc)
        s_old = jnp.exp(b_last + m0 - m_new)
        s_loc = jnp.exp(m_loc - m_new)
        st_scr[h] = s_old * st + s_loc * c_loc
        m_scr[h:h + 1, :] = jnp.broadcast_to(m_new, (1, LANES))


def _mlstm(qk, vT, oT, smT, conv_w, i_bias, f_bias, head_norm, B, S):
    L = min(ML_CHUNK, S)
    assert L % LANES == 0
    nc = S // L
    qk = qk.reshape(B, S, 2 * ML_QK_W)
    bcol = jnp.concatenate([i_bias, f_bias]).astype(F32).reshape(2 * ML_HEADS, 1)
    hn = jnp.broadcast_to(head_norm.astype(F32)[:, None], (ML_V_W, L))
    blkT = lambda n: pl.BlockSpec((1, n, L), lambda b, c: (b, 0, c))
    full = lambda a: pl.BlockSpec(a.shape, lambda b, c: (0,) * a.ndim)
    return pl.pallas_call(
        functools.partial(_mlstm_kernel, L=L),
        grid=(B, nc),
        in_specs=[pl.BlockSpec((1, L, 2 * ML_QK_W), lambda b, c: (b, c, 0)),
                  blkT(ML_V_W), blkT(ML_V_W), blkT(16), full(conv_w), full(bcol), full(hn)],
        out_specs=pl.BlockSpec((1, L, ML_V_W), lambda b, c: (b, c, 0)),
        out_shape=jax.ShapeDtypeStruct((B, S, ML_V_W), BF16),
        scratch_shapes=[pltpu.VMEM((L + SUBLANES, 2 * ML_QK_W), F32),
                        pltpu.VMEM((ML_HEADS, ML_DV + SUBLANES, ML_DQK), F32),
                        pltpu.VMEM((SUBLANES, LANES), F32)],
        compiler_params=_params("parallel", "arbitrary"),
    )(qk, vT, oT, smT, conv_w.astype(F32), bcol, hn)


def _block_diag_T(qT, heads, dh):
    rid = lax.broadcasted_iota(I32, qT.shape, 0)
    zero = jnp.zeros_like(qT)
    return jnp.concatenate(
        [jnp.where((rid >= h * dh) & (rid < (h + 1) * dh), qT, zero) for h in range(heads)], axis=1)


def _accumulate(h, p, vT_chunk, l, acc_scr, dh):
    rows = slice(h * dh, (h + 1) * dh)
    acc_scr[rows, :] += jnp.dot(vT_chunk[rows, :], p.astype(BF16), preferred_element_type=F32)
    return l + jnp.sum(p, axis=0, keepdims=True)


def _finish_attention(l, acc_scr, y_ref, heads, dh):
    outs = [acc_scr[h * dh:(h + 1) * dh, :] / l[h] for h in range(heads)]
    y_ref[0] = jnp.concatenate(outs, axis=0).T.astype(y_ref.dtype)


def _max_key_norm2(kb, heads, dh, TQ):
    ch = lax.broadcasted_iota(I32, (heads * dh, LANES), 0) // dh
    hd = lax.broadcasted_iota(I32, (heads * dh, LANES), 1)
    n2 = jnp.dot((kb * kb).astype(BF16), (ch == hd).astype(BF16), preferred_element_type=F32)
    n2 = jnp.max(n2, axis=0, keepdims=True) * (1.0 + 2.0 ** -7)
    return jnp.concatenate([jnp.broadcast_to(n2[:, h:h + 1], (1, TQ)) for h in range(heads)], axis=1)


def _norm_bound(qTf, k2, heads, dh, TQ):
    out = []
    for h in range(heads):
        qh = qTf[h * dh:(h + 1) * dh, :]
        q2 = jnp.sum(qh * qh, axis=0, keepdims=True)
        out.append(jnp.sqrt(q2 * k2[:, h * TQ:(h + 1) * TQ]) * (1.0 + 2.0 ** -8))
    return out


def _underflowed(l):
    small = l[0]
    for lh in l[1:]:
        small = jnp.minimum(small, lh)
    return jnp.logical_not(jnp.min(small) >= SOFTMAX_L_MIN)


def _moba_kernel(qT_ref, k_ref, vT_ref, y_ref, km_scr, kn_scr, acc_scr, s_scr, *, NB, NBP, n_sel):
    H, DH, BS = MOBA_HEADS, MOBA_DH, MOBA_BLOCK
    TQ = BS
    qi = pl.program_id(1)

    @pl.when(qi == 0)
    def _():
        km_scr[...] = jnp.zeros_like(km_scr)
        kn_scr[...] = jnp.zeros_like(kn_scr)

        def body(j, carry):
            kb = k_ref[0, pl.ds(pl.multiple_of(j * BS, BS), BS), :].astype(F32)
            km_scr[pl.ds(j, 1), :] = jnp.mean(kb, axis=0, keepdims=True)
            kn_scr[pl.ds(j, 1), :] = _max_key_norm2(kb, H, DH, TQ)
            return carry
        lax.fori_loop(0, NB, body, 0)

    qT = qT_ref[0]
    qTf = qT.astype(F32)
    km = km_scr[...]
    gate = jnp.concatenate(
        [jnp.dot(km[:, h * DH:(h + 1) * DH], qTf[h * DH:(h + 1) * DH, :], precision=lax.Precision.HIGHEST,
                 preferred_element_type=F32) for h in range(H)], axis=1) * (1.0 / ATT_QSCALE)
    blk = lax.broadcasted_iota(I32, gate.shape, 0)
    g = jnp.where(blk < qi, gate, NEG_INF)
    sels = []
    for _ in range(n_sel):
        mx = jnp.max(g, axis=0, keepdims=True)
        isel = jnp.min(jnp.where(g == mx, blk, NBP), axis=0, keepdims=True)
        sels.append(jnp.where(mx > NEG_INF, isel, -1))
        g = jnp.where(blk == isel, NEG_INF, g)

    qs = _block_diag_T(qT, H, DH)

    def hit_row(j):
        hit = sels[0] == j
        for s in sels[1:]:
            hit = hit | (s == j)
        return hit

    blk_start = lambda j: pl.multiple_of(j * BS, BS)
    blk_scores = lambda j: jnp.dot(k_ref[0, pl.ds(blk_start(j), BS), :], qs, preferred_element_type=F32)
    hcols = lambda row, h: row[:, h * TQ:(h + 1) * TQ]
    causal = lax.broadcasted_iota(I32, (BS, TQ), 0) <= lax.broadcasted_iota(I32, (BS, TQ), 1)

    def diag_scores():
        s_all = blk_scores(qi)
        return [jnp.where(causal, hcols(s_all, h), NEG_INF) for h in range(H)]

    def max_sweep():
        def max_block(j, m):
            return jnp.maximum(m, jnp.where(hit_row(j), jnp.max(blk_scores(j), axis=0, keepdims=True), NEG_INF))
        m_all = lax.fori_loop(0, qi, max_block, jnp.full((1, H * TQ), NEG_INF, F32))
        s_diag = diag_scores()
        return [jnp.maximum(hcols(m_all, h), jnp.max(s_diag[h], axis=0, keepdims=True)) for h in range(H)]

    def sum_sweep(m):
        acc_scr[...] = jnp.zeros_like(acc_scr)

        def sum_group(width):
            def body(g, l):
                for u in range(width):
                    s_scr[u] = blk_scores(width * g + u)
                for u in range(width):
                    j = width * g + u
                    hit = hit_row(j)
                    vT_blk = vT_ref[0, :, pl.ds(blk_start(j), BS)]
                    l = tuple(_accumulate(h, jnp.exp2(s_scr[u, :, h * TQ:(h + 1) * TQ]
                                                      - jnp.where(hcols(hit, h), m[h], jnp.inf)),
                                          vT_blk, l[h], acc_scr, DH) for h in range(H))
                return l
            return body
        l = _chunk_loops(qi, ATT_STAGES, sum_group, (jnp.zeros((1, TQ), F32),) * H)
        s_diag = diag_scores()
        vT_diag = vT_ref[0, :, pl.ds(blk_start(qi), BS)]
        return [_accumulate(h, jnp.exp2(s_diag[h] - m[h]), vT_diag, l[h], acc_scr, DH) for h in range(H)]

    blk_n = lax.broadcasted_iota(I32, kn_scr.shape, 0)
    k2 = jnp.max(jnp.where(blk_n <= qi, kn_scr[...], 0.0), axis=0, keepdims=True)
    l = sum_sweep(_norm_bound(qTf, k2, H, DH, TQ))
    _finish_attention(l, acc_scr, y_ref, H, DH)

    @pl.when(_underflowed(l))
    def _():
        _finish_attention(sum_sweep(max_sweep()), acc_scr, y_ref, H, DH)


def _moba(qT, k, vT, B, S):
    BS = MOBA_BLOCK
    assert S % BS == 0
    NB = S // BS
    NBP = -(-NB // SUBLANES) * SUBLANES
    n_sel = max(min(MOBA_TOPK, NB - 1), 1)
    k = k.reshape(B, S, MOBA_W)
    return pl.pallas_call(
        functools.partial(_moba_kernel, NB=NB, NBP=NBP, n_sel=n_sel),
        grid=(B, NB),
        in_specs=[pl.BlockSpec((1, MOBA_W, BS), lambda b, i: (b, 0, i)),
                  pl.BlockSpec((1, S, MOBA_W), lambda b, i: (b, 0, 0)),
                  pl.BlockSpec((1, MOBA_W, S), lambda b, i: (b, 0, 0))],
        out_specs=pl.BlockSpec((1, BS, MOBA_W), lambda b, i: (b, i, 0)),
        out_shape=jax.ShapeDtypeStruct((B, S, MOBA_W), BF16),
        scratch_shapes=[pltpu.VMEM((NBP, MOBA_W), F32),
                        pltpu.VMEM((NBP, MOBA_HEADS * BS), F32),
                        pltpu.VMEM((MOBA_W, BS), F32),
                        pltpu.VMEM((ATT_STAGE, BS, MOBA_HEADS * BS), F32)],
        compiler_params=_params("parallel", "arbitrary"),
    )(qT, k, vT)


def _dsa_kernel(qT_ref, k_ref, vT_ref, iqT_ref, ik_ref, smT_ref, y_ref,
                e_scr, gmax_scr, acc_scr, thr_scr, kn_scr, s_scr, *, TQ, topk, spow, n_chunks):
    H, DH = DSA_HEADS, DSA_DH
    KC = TQ
    qi = pl.program_id(1)
    nch = qi + 1
    chunk_start = lambda c: pl.multiple_of(c * KC, KC)

    @pl.when(qi == 0)
    def _():
        kn_scr[...] = jnp.zeros_like(kn_scr)

        def body(c, carry):
            kn_scr[pl.ds(c, 1), :] = _max_key_norm2(k_ref[0, pl.ds(chunk_start(c), KC), :].astype(F32), H, DH, TQ)
            return carry
        lax.fori_loop(0, n_chunks, body, 0)
    sum8 = lambda b: jnp.sum(b.astype(I32).reshape(KC // SUBLANES, SUBLANES, TQ), axis=0)
    any_row = lambda r: jnp.max(r.astype(I32)) > 0

    iqT = iqT_ref[0]
    iq_cat = jnp.concatenate([iqT[h * IDX_DH:(h + 1) * IDX_DH, :] for h in range(IDX_HEADS)], axis=1)
    iw = smT_ref[0][2 * ML_HEADS:2 * ML_HEADS + IDX_HEADS, :] * (IDX_HEADS ** -0.5 * IDX_DH ** -0.5)
    t_pos = qi * TQ + lax.broadcasted_iota(I32, (KC, TQ), 1)
    s_off = lax.broadcasted_iota(I32, (KC, TQ), 0)
    gmax_scr[...] = jnp.full((KC, TQ), NEG_INF, F32)

    def score_group(width, masked=False):
        def body(g, tally):
            for u in range(width):
                ikc = ik_ref[0, pl.ds(chunk_start(width * g + u), KC), :][:, :IDX_DH]
                s_scr[u] = jnp.dot(ikc, iq_cat, preferred_element_type=F32)
            for u in range(width):
                c = width * g + u
                sc = iw[0:1, :] * jnp.maximum(s_scr[u, :, 0:TQ], 0.0)
                for h in range(1, IDX_HEADS):
                    sc = sc + iw[h:h + 1, :] * jnp.maximum(s_scr[u, :, h * TQ:(h + 1) * TQ], 0.0)
                if masked:
                    sc = jnp.where(c * KC + s_off <= t_pos, sc, NEG_INF)
                e_scr[pl.ds(chunk_start(c), KC), :] = sc
                gmax_scr[...] = jnp.maximum(gmax_scr[...], sc)
                tally = tally + jnp.sum(jnp.where(sc > 0.0, 1, jnp.where(sc == 0.0, 1 << 16, 0))
                                        .reshape(KC // SUBLANES, SUBLANES, TQ), axis=0)
            return tally
        return body
    assert n_chunks * KC < (1 << 16)
    zero8 = jnp.zeros((SUBLANES, TQ), I32)
    tally = _chunk_loops(qi, ATT_STAGES, score_group, zero8)
    tally = jnp.sum(score_group(1, True)(qi, tally), axis=0, keepdims=True)
    n_pos = tally & 0xFFFF
    n_nonneg = n_pos + (tally >> 16)

    def sweep_keys(fold, merge, init):
        def group(width):
            def body(g, acc):
                return tuple(merge(acc[u], fold(e_scr[pl.ds(chunk_start(width * g + u), KC), :]))
                             for u in range(width))
            return body

        def regroup(acc, width):
            half = len(acc) // 2
            acc = tuple(merge(acc[u], acc[u + half]) for u in range(half))
            return acc if half == width else regroup(acc, width)
        return _chunk_loops(nch, ATT_STAGES, group, (init,) * ATT_STAGES[0], regroup)[0]

    def count_gt(cand):
        return jnp.sum(sweep_keys(lambda x: sum8(x > cand), lambda p, q: p + q, zero8), axis=0, keepdims=True)

    def min_gt(cand):
        fold = lambda x: jnp.min(jnp.where(x > cand, x, jnp.inf).reshape(KC // SUBLANES, SUBLANES, TQ), axis=0)
        return jnp.min(sweep_keys(fold, jnp.minimum, jnp.full((SUBLANES, TQ), jnp.inf, F32)), axis=0, keepdims=True)

    def rewrite(rows, thr):
        def body(c, carry):
            x = e_scr[pl.ds(chunk_start(c), KC), :]
            rank = (2 * spow - (c * KC + s_off)).astype(F32)
            new = jnp.where(x > thr, jnp.inf, jnp.where(x == thr, rank, NEG_INF))
            e_scr[pl.ds(chunk_start(c), KC), :] = jnp.where(rows, new, x)
            return carry
        lax.fori_loop(0, nch, body, 0)

    def halve(_, c):
        lo, hi, act, tie, n_lo = c
        mid = lo + (hi - lo) * 0.5
        inside = (mid > lo) & (mid < hi)
        cnt = count_gt(mid)
        live = act > 0.0
        run = live & inside
        up = run & (cnt >= topk)
        lo = jnp.where(up, mid, lo)
        n_lo = jnp.where(up, cnt, n_lo)
        hi = jnp.where(run & (cnt < topk), mid, hi)
        collapsed = live & jnp.logical_not(inside)
        tie = jnp.where(collapsed, 1.0, tie)
        act = jnp.where(collapsed | (run & (cnt == topk)), 0.0, act)
        return lo, hi, act, tie, n_lo

    def excess(state):
        return jnp.max(jnp.where(state[2] > 0.0, state[4] - topk, 0).astype(F32))

    def bisect(lo, hi, act, warmup):
        state = (lo, hi, act, jnp.zeros_like(act), jnp.full((1, TQ), 1 << 20, I32))
        state = lax.fori_loop(0, warmup, halve, state)

        def one_more(c):
            state = halve(0, c[1])
            return c[0] + 1, state, excess(state)
        near = lax.while_loop(lambda c: jnp.logical_and(c[0] < BISECT_CAP, c[2] > BISECT_DROP), one_more,
                              (jnp.int32(warmup), state, excess(state)))
        lo, hi, act, tie, n_lo = near[1]
        cur, left = lo, jnp.where(act > 0.0, n_lo - topk, 0)
        for _ in range(BISECT_DROP):
            cur = jnp.where(left > 0, min_gt(cur), cur)
            left = left - 1
        done = (act > 0.0) & (count_gt(cur) == topk)
        lo = jnp.where(done, cur, lo)
        act = jnp.where(done, 0.0, act)

        def body(c):
            state = c[1]
            for _ in range(BISECT_STEPS_PER_CHECK):
                state = halve(0, state)
            return c[0] + BISECT_STEPS_PER_CHECK, state, jnp.max(state[2])
        out = lax.while_loop(lambda c: jnp.logical_and(c[0] < BISECT_CAP, c[2] > 0.0), body,
                             (near[0], (lo, hi, act, tie, n_lo), jnp.max(act)))
        return out[1][0], out[1][1], out[1][3]

    fmax = float(jnp.finfo(F32).max)
    gm = gmax_scr[...]
    g_lo = jnp.min(gm, axis=0, keepdims=True)
    g_hi = jnp.max(gm, axis=0, keepdims=True)
    n_vis = qi * TQ + lax.broadcasted_iota(I32, (1, TQ), 1) + 1
    few = n_vis <= topk
    zero_tie = (n_pos < topk) & (n_nonneg >= topk) & jnp.logical_not(few)

    @pl.when(any_row(zero_tie))
    def _():
        rewrite(zero_tie, jnp.zeros((1, TQ), F32))

    rank_lo, rank_hi = float(spow), float(2 * spow + 1)
    below = jnp.maximum(g_lo, -fmax)
    below = below - jnp.abs(below) * (2.0 ** -10) - 1e-30
    lo0 = jnp.where(zero_tie, rank_lo, jnp.maximum(below, -fmax))
    hi0 = jnp.where(zero_tie, rank_hi, g_hi)
    lo, hi, tie = bisect(lo0, hi0, jnp.where(few, 0.0, 1.0), BISECT_WARMUP)
    thr_scr[0:1, :] = jnp.where(few, NEG_INF, lo)

    @pl.when(jnp.max(tie) > 0.0)
    def _():
        rows = tie > 0.0
        rewrite(rows, hi)
        lo2, _, _ = bisect(jnp.full((1, TQ), rank_lo, F32), jnp.full((1, TQ), rank_hi, F32), tie, 0)
        thr_scr[0:1, :] = jnp.where(rows, lo2, thr_scr[0:1, :])

    thr = thr_scr[0:1, :]
    qT = qT_ref[0]
    qs = _block_diag_T(qT, H, DH)
    hcols = lambda row, h: row[:, h * TQ:(h + 1) * TQ]

    def chunk_scores(c):
        start = chunk_start(c)
        sel = e_scr[pl.ds(start, KC), :] > thr
        s_all = jnp.dot(k_ref[0, pl.ds(start, KC), :], qs, preferred_element_type=F32)
        return sel, s_all

    def max_sweep():
        def max_chunk(c, m):
            sel, s_all = chunk_scores(c)
            return tuple(jnp.maximum(m[h], jnp.max(jnp.where(sel, hcols(s_all, h), NEG_INF), axis=0, keepdims=True))
                         for h in range(H))
        m = lax.fori_loop(0, nch, max_chunk, (jnp.full((1, TQ), NEG_INF, F32),) * H)
        return [jnp.where(mh == NEG_INF, 0.0, mh) for mh in m]

    def sum_sweep(m):
        acc_scr[...] = jnp.zeros_like(acc_scr)

        def sum_group(width):
            def body(g, l):
                for u in range(width):
                    s_scr[u] = jnp.dot(k_ref[0, pl.ds(chunk_start(width * g + u), KC), :], qs,
                                       preferred_element_type=F32)
                for u in range(width):
                    start = chunk_start(width * g + u)
                    sel = e_scr[pl.ds(start, KC), :] > thr
                    vT_chunk = vT_ref[0, :, pl.ds(start, KC)]
                    l = tuple(_accumulate(h, jnp.exp2(s_scr[u, :, h * TQ:(h + 1) * TQ] + jnp.where(sel, -m[h], NEG_INF)),
                                          vT_chunk, l[h], acc_scr, DH) for h in range(H))
                return l
            return body
        return _chunk_loops(nch, ATT_STAGES, sum_group, (jnp.zeros((1, TQ), F32),) * H)

    chunk_n = lax.broadcasted_iota(I32, kn_scr.shape, 0)
    k2 = jnp.max(jnp.where(chunk_n < nch, kn_scr[...], 0.0), axis=0, keepdims=True)
    l = sum_sweep(_norm_bound(qT.astype(F32), k2, H, DH, TQ))
    _finish_attention(l, acc_scr, y_ref, H, DH)

    @pl.when(_underflowed(l))
    def _():
        _finish_attention(sum_sweep(max_sweep()), acc_scr, y_ref, H, DH)


def _dsa(qT, k, vT, iqT, ik, smT, B, S):
    TQ = min(ATT_TQ, S)
    topk = min(DSA_TOPK_MAX, S // 4)
    assert S % TQ == 0 and topk <= TQ
    spow = 1 << max((S - 1).bit_length(), 1)
    k = k.reshape(B, S, DSA_W)
    ik = ik.reshape(B, S, LANES)
    qblk = lambda n: pl.BlockSpec((1, n, TQ), lambda b, i: (b, 0, i))
    return pl.pallas_call(
        functools.partial(_dsa_kernel, TQ=TQ, topk=topk, spow=spow, n_chunks=S // TQ),
        grid=(B, S // TQ),
        in_specs=[qblk(DSA_W),
                  pl.BlockSpec((1, S, DSA_W), lambda b, i: (b, 0, 0)),
                  pl.BlockSpec((1, DSA_W, S), lambda b, i: (b, 0, 0)),
                  qblk(IDX_HEADS * IDX_DH),
                  pl.BlockSpec((1, S, LANES), lambda b, i: (b, 0, 0)),
                  qblk(16)],
        out_specs=pl.BlockSpec((1, TQ, DSA_W), lambda b, i: (b, i, 0)),
        out_shape=jax.ShapeDtypeStruct((B, S, DSA_W), BF16),
        scratch_shapes=[pltpu.VMEM((S, TQ), F32),
                        pltpu.VMEM((TQ, TQ), F32),
                        pltpu.VMEM((DSA_W, TQ), F32),
                        pltpu.VMEM((SUBLANES, TQ), F32),
                        pltpu.VMEM((-(-(S // TQ) // SUBLANES) * SUBLANES, DSA_HEADS * TQ), F32),
                        pltpu.VMEM((ATT_STAGE, TQ, DSA_HEADS * TQ), F32)],
        compiler_params=_params("parallel", "arbitrary"),
    )(qT, k, vT, iqT, ik, smT)


def _out_mlp_kernel(yml_ref, ymb_ref, yds_ref, x_ref, wout_ref, gpost_ref, gpre_ref,
                    w1_ref, w2_ref, g2_ref, out_ref, h_scr, acc_scr):
    f = pl.program_id(1)

    @pl.when(f == 0)
    def _():
        slab = x_ref.shape[0] // MLP_SLABS
        for r in range(MLP_SLABS):
            rows = slice(r * slab, (r + 1) * slab)
            mix = jnp.dot(yml_ref[rows, :], wout_ref[0:ML_V_W, :], preferred_element_type=F32)
            mix = mix + jnp.dot(ymb_ref[rows, :], wout_ref[ML_V_W:ML_V_W + MOBA_W, :], preferred_element_type=F32)
            mix = mix + jnp.dot(yds_ref[rows, :], wout_ref[ML_V_W + MOBA_W:, :], preferred_element_type=F32)
            x1 = x_ref[rows, :] + _rms(mix, gpost_ref[...])
            out_ref[rows, :] = x1
            h_scr[rows, :] = _rms(x1, gpre_ref[...]).astype(BF16)
        acc_scr[...] = jnp.zeros_like(acc_scr)

    u = jnp.maximum(jnp.dot(h_scr[...], w1_ref[...], preferred_element_type=F32), 0.0)
    acc_scr[...] += jnp.dot((u * u).astype(BF16), w2_ref[...], preferred_element_type=F32)

    @pl.when(f == pl.num_programs(1) - 1)
    def _():
        out_ref[...] = out_ref[...] + _rms(acc_scr[...], g2_ref[...])


def _out_mlp(yml, ymb, yds, x2, w_out, g_post, g_pre, w1, w2, g2):
    T, D = x2.shape
    tm = min(MLP_TM, T)
    tf = min(MLP_TF, D_FF)
    row = lambda n: pl.BlockSpec((tm, n), lambda i, f: (i, 0))
    full = lambda a: pl.BlockSpec(a.shape, lambda i, f: (0,) * a.ndim)
    return pl.pallas_call(
        _out_mlp_kernel,
        grid=(T // tm, D_FF // tf),
        in_specs=[row(ML_V_W), row(MOBA_W), row(DSA_W), row(D), full(w_out), full(g_post), full(g_pre),
                  pl.BlockSpec((D, tf), lambda i, f: (0, f)),
                  pl.BlockSpec((tf, D), lambda i, f: (f, 0)),
                  full(g2)],
        out_specs=row(D),
        out_shape=jax.ShapeDtypeStruct((T, D), F32),
        scratch_shapes=[pltpu.VMEM((tm, D), BF16), pltpu.VMEM((tm, D), F32)],
        compiler_params=_params("parallel", "arbitrary"),
    )(yml, ymb, yds, x2, w_out, g_post, g_pre, w1, w2, g2)


def kernel(x, norm_mix_pre, w_in, ml_conv, ml_i_bias, ml_f_bias, ml_head_norm, w_out, norm_mix_post,
           norm_mlp_pre, w_ff1, w_ff2, norm_mlp_post):
    B, S, D = x.shape
    depth = w_in.shape[0]
    x2 = x.reshape(B * S, D)
    gain = lambda g: g.reshape(1, D).astype(F32)
    for l in range(depth):
        wn, wt = _prep_in_weights(w_in[l])
        (qk, mbk, dsk, ixk, mbqT, mbvT, dsqT, dsvT, ixqT, mlvT, mloT, smT) = _in_proj(
            x2, gain(norm_mix_pre[l]), wn, wt, B, S)
        y_ml = _mlstm(qk, mlvT, mloT, smT, ml_conv[l], ml_i_bias[l], ml_f_bias[l], ml_head_norm[l], B, S)
        y_mb = _moba(mbqT, mbk, mbvT, B, S)
        y_ds = _dsa(dsqT, dsk, dsvT, ixqT, ixk, smT, B, S)
        x2 = _out_mlp(y_ml.reshape(B * S, ML_V_W), y_mb.reshape(B * S, MOBA_W), y_ds.reshape(B * S, DSA_W),
                      x2, w_out[l].astype(BF16), gain(norm_mix_post[l]), gain(norm_mlp_pre[l]),
                      w_ff1[l].astype(BF16), w_ff2[l].astype(BF16), gain(norm_mlp_post[l]))
    return x2.reshape(B, S, D)
```

```python
import functools

import jax
import jax.numpy as jnp
from jax import lax
from jax.experimental import pallas as pl
from jax.experimental.pallas import tpu as pltpu

F32 = jnp.float32
BF16 = jnp.bfloat16
I32 = jnp.int32

D_MODEL = 1024
ML_HEADS, ML_DQK, ML_DV, ML_CONV = 4, 64, 128, 4
MOBA_HEADS, MOBA_DH, MOBA_BLOCK, MOBA_TOPK = 4, 64, 256, 3
DSA_HEADS, DSA_DH, IDX_HEADS, IDX_DH, DSA_TOPK_MAX = 4, 64, 4, 64, 256
D_FF = 4 * D_MODEL
EPS = 1e-6

ML_QK_W = ML_HEADS * ML_DQK
ML_V_W = ML_HEADS * ML_DV
MOBA_W = MOBA_HEADS * MOBA_DH
DSA_W = DSA_HEADS * DSA_DH
IN_SPLITS = (ML_QK_W, ML_QK_W, ML_V_W, ML_V_W, ML_HEADS, ML_HEADS,
             MOBA_W, MOBA_W, MOBA_W,
             DSA_W, DSA_W, DSA_W, IDX_HEADS * IDX_DH, IDX_DH, IDX_HEADS)

LANES = 128
SUBLANES = 8
VMEM_LIMIT = 52 * 1024 * 1024

ML_CHUNK = 512
ATT_TQ = 256
ATT_STAGES = (8, 4, 2, 1)
ATT_STAGE = ATT_STAGES[0]
PROJ_TM = 1024
MLP_TM = 1024
MLP_TF = 2048
MLP_SLABS = 4

NEG_INF = float("-inf")
BISECT_CAP = 300
BISECT_WARMUP = 12
BISECT_DROP = 2
BISECT_STEPS_PER_CHECK = 2
SOFTMAX_L_MIN = 1e-30
assert MOBA_DH == DSA_DH
ATT_QSCALE = MOBA_DH ** -0.5 * 1.4426950408889634

_NT = (((1,), (1,)), ((), ()))


def _chunk_loops(n, widths, make_body, carry, regroup=lambda c, w: c):
    covered = 0
    for i, w in enumerate(widths):
        if i:
            carry = regroup(carry, w)
        carry = lax.fori_loop(covered // w, n // w, make_body(w), carry)
        covered = w * (n // w)
    return carry


def _rms(x, g):
    return x * lax.rsqrt(jnp.mean(x * x, axis=-1, keepdims=True) + EPS) * g


def _log_sigmoid(x):
    return jnp.minimum(x, 0.0) - jnp.log1p(jnp.exp(-jnp.abs(x)))


def _params(*sem):
    return pltpu.CompilerParams(dimension_semantics=sem, vmem_limit_bytes=VMEM_LIMIT)


_N_QK, _N_MBK, _N_DSK, _N_IXK = 0, 512, 768, 1024
_N_TOTAL = 1152
_T_MBQ, _T_MBV, _T_DSQ, _T_DSV, _T_IXQ, _T_MLV, _T_MLO, _T_SM = 0, 256, 512, 768, 1024, 1280, 1792, 2304
_T_TOTAL = 2320


def _in_proj_kernel(x_ref, g_ref, wn_ref, wt_ref,
                    qk_ref, mbk_ref, dsk_ref, ixk_ref,
                    mbqT_ref, mbvT_ref, dsqT_ref, dsvT_ref, ixqT_ref, mlvT_ref, mloT_ref, smT_ref):
    h = _rms(x_ref[...], g_ref[...]).astype(BF16)

    def mm(a, n):
        return jnp.dot(h, wn_ref[:, a:a + n], preferred_element_type=F32)

    def mt(a, n):
        return lax.dot_general(wt_ref[a:a + n, :], h, _NT, preferred_element_type=F32)

    qk_ref[...] = mm(_N_QK, 512)
    mbk_ref[...] = mm(_N_MBK, 256).astype(BF16)
    dsk_ref[...] = mm(_N_DSK, 256).astype(BF16)
    ixk_ref[...] = mm(_N_IXK, 128).astype(BF16)
    mbqT_ref[0] = (mt(_T_MBQ, 256) * ATT_QSCALE).astype(BF16)
    mbvT_ref[0] = mt(_T_MBV, 256).astype(BF16)
    dsqT_ref[0] = (mt(_T_DSQ, 256) * ATT_QSCALE).astype(BF16)
    dsvT_ref[0] = mt(_T_DSV, 256).astype(BF16)
    ixqT_ref[0] = mt(_T_IXQ, 256).astype(BF16)
    mlvT_ref[0] = mt(_T_MLV, 512).astype(BF16)
    mloT_ref[0] = mt(_T_MLO, 512)
    smT_ref[0] = mt(_T_SM, 16)


def _prep_in_weights(w_in):
    pts, acc = [], 0
    for n in IN_SPLITS:
        pts.append((acc, acc + n))
        acc += n
    col = lambda i: w_in[:, pts[i][0]:pts[i][1]]
    (ml_q, ml_k, ml_v, ml_o, ml_i, ml_f, mb_q, mb_k, mb_v,
     ds_q, ds_k, ds_v, ix_q, ix_k, ix_w) = [col(i) for i in range(len(IN_SPLITS))]
    d = w_in.shape[0]
    small = jnp.concatenate([ml_i, ml_f, ix_w], axis=1)
    wn = jnp.concatenate([
        ml_q, ml_k, mb_k, ds_k,
        ix_k, jnp.zeros((d, 128 - IDX_DH), w_in.dtype)], axis=1)
    wt = jnp.concatenate([
        mb_q, mb_v, ds_q, ds_v, ix_q, ml_v, ml_o,
        small, jnp.zeros((d, 16 - small.shape[1]), w_in.dtype)], axis=1).T
    assert wn.shape == (d, _N_TOTAL) and wt.shape == (_T_TOTAL, d)
    return wn.astype(BF16), wt.astype(BF16)


def _in_proj(x2, g, wn, wt, B, S):
    T, D = x2.shape
    tm = min(PROJ_TM, S)
    nsb = S // tm
    row = lambda n: pl.BlockSpec((tm, n), lambda i: (i, 0))
    tr = lambda n: pl.BlockSpec((1, n, tm), lambda i: (i // nsb, 0, i % nsb))
    full = lambda a: pl.BlockSpec(a.shape, lambda i: (0,) * a.ndim)
    out_shape = (
        jax.ShapeDtypeStruct((T, 512), F32),
        jax.ShapeDtypeStruct((T, 256), BF16),
        jax.ShapeDtypeStruct((T, 256), BF16),
        jax.ShapeDtypeStruct((T, 128), BF16),
        jax.ShapeDtypeStruct((B, 256, S), BF16),
        jax.ShapeDtypeStruct((B, 256, S), BF16),
        jax.ShapeDtypeStruct((B, 256, S), BF16),
        jax.ShapeDtypeStruct((B, 256, S), BF16),
        jax.ShapeDtypeStruct((B, 256, S), BF16),
        jax.ShapeDtypeStruct((B, 512, S), BF16),
        jax.ShapeDtypeStruct((B, 512, S), F32),
        jax.ShapeDtypeStruct((B, 16, S), F32),
    )
    out_specs = (row(512), row(256), row(256), row(128),
                 tr(256), tr(256), tr(256), tr(256), tr(256), tr(512), tr(512), tr(16))
    return pl.pallas_call(
        _in_proj_kernel,
        grid=(T // tm,),
        in_specs=[row(D), full(g), full(wn), full(wt)],
        out_specs=out_specs,
        out_shape=out_shape,
        compiler_params=_params("parallel"),
    )(x2, g, wn, wt)


def _dot3(a_f32, b_exact):
    b = b_exact.astype(BF16)
    hi = a_f32.astype(BF16)
    r1 = a_f32 - hi.astype(F32)
    mid = r1.astype(BF16)
    lo = (r1 - mid.astype(F32)).astype(BF16)
    return (jnp.dot(hi, b, preferred_element_type=F32) + jnp.dot(mid, b, preferred_element_type=F32)
            + jnp.dot(lo, b, preferred_element_type=F32))


def _mlstm_kernel(qk_ref, vT_ref, oT_ref, smT_ref, conv_ref, bcol_ref, hn_ref,
                  y_ref, xp_scr, st_scr, m_scr, *, L):
    c = pl.program_id(1)
    halo = SUBLANES
    DK, DV = ML_DQK, ML_DV

    @pl.when(c == 0)
    def _():
        xp_scr[0:halo, :] = jnp.zeros((halo, 2 * ML_QK_W), F32)
        st_scr[...] = jnp.zeros_like(st_scr)
        m_scr[...] = jnp.zeros_like(m_scr)

    cur = qk_ref[0]
    xp_scr[halo:halo + L, :] = cur
    base = halo - (ML_CONV - 1)
    acc = conv_ref[0:1, :] * xp_scr[base:base + L, :]
    for j in range(1, ML_CONV):
        acc = acc + conv_ref[j:j + 1, :] * xp_scr[base + j:base + j + L, :]
    xp_scr[0:halo, :] = cur[L - halo:L, :]
    qk = acc * jax.nn.sigmoid(acc)
    qT_all = (qk[:, :ML_QK_W] * (DK ** -0.5)).T.astype(BF16)
    k_all = qk[:, ML_QK_W:].astype(BF16)

    grow = smT_ref[0][0:SUBLANES, :] + bcol_ref[...]
    si = lax.broadcasted_iota(I32, (L, L), 0)
    ji = lax.broadcasted_iota(I32, (L, L), 1)
    causal = si <= ji
    b_row = _dot3(_log_sigmoid(grow), causal)

    for h in range(ML_HEADS):
        qT = qT_all[h * DK:(h + 1) * DK, :]
        kh = k_all[:, h * DK:(h + 1) * DK]
        vT = vT_ref[0, h * DV:(h + 1) * DV, :]
        b_j = b_row[ML_HEADS + h:ML_HEADS + h + 1, :]
        c_row = grow[h:h + 1, :] - b_j
        c_col = jnp.broadcast_to(c_row, (LANES, L)).T
        dlog = jnp.where(causal, b_j + jnp.concatenate([c_col] * (L // LANES), axis=1), NEG_INF)
        m0 = m_scr[h:h + 1, 0:1]
        inter = b_j + m0
        m_t = jnp.maximum(inter, jnp.max(dlog, axis=0, keepdims=True))
        w_inter = jnp.exp(inter - m_t)
        w_intra = jnp.exp(dlog - m_t) * jnp.dot(kh, qT, preferred_element_type=F32)
        st = st_scr[h]
        cq = jnp.dot(st.astype(BF16), qT, preferred_element_type=F32)
        num = w_inter * cq[:DV] + jnp.dot(vT, w_intra.astype(BF16), preferred_element_type=F32)
        den = w_inter * cq[DV:DV + 1] + jnp.sum(w_intra, axis=0, keepdims=True)
        hh = num / jnp.maximum(jnp.abs(den), jnp.exp(-m_t))
        hh = hh * lax.rsqrt(jnp.mean(hh * hh, axis=0, keepdims=True) + EPS)
        rows = slice(h * DV, (h + 1) * DV)
        y_ref[0, :, rows] = (jax.nn.sigmoid(oT_ref[0, rows, :]) * hh * hn_ref[rows, :]).T.astype(y_ref.dtype)

        b_last = b_j[:, L - 1:L]
        a = b_last + c_row
        m_loc = jnp.max(a, axis=1, keepdims=True)
        wa = jnp.exp(a - m_loc)
        lhs = jnp.concatenate([(vT.astype(F32) * wa).astype(BF16),
                               jnp.broadcast_to(wa, (SUBLANES, L)).astype(BF16)], axis=0)
        c_loc = jnp.dot(lhs, kh, preferred_element_type=F32)
        m_new = jnp.maximum(b_last + m0, m_loc)
        s_old = jnp.exp(b_last + m0 - m_new)
        s_loc = jnp.exp(m_loc - m_new)
        st_scr[h] = s_old * st + s_loc * c_loc
        m_scr[h:h + 1, :] = jnp.broadcast_to(m_new, (1, LANES))


def _mlstm(qk, vT, oT, smT, conv_w, i_bias, f_bias, head_norm, B, S):
    L = min(ML_CHUNK, S)
    assert L % LANES == 0
    nc = S // L
    qk = qk.reshape(B, S, 2 * ML_QK_W)
    bcol = jnp.concatenate([i_bias, f_bias]).astype(F32).reshape(2 * ML_HEADS, 1)
    hn = jnp.broadcast_to(head_norm.astype(F32)[:, None], (ML_V_W, L))
    blkT = lambda n: pl.BlockSpec((1, n, L), lambda b, c: (b, 0, c))
    full = lambda a: pl.BlockSpec(a.shape, lambda b, c: (0,) * a.ndim)
    return pl.pallas_call(
        functools.partial(_mlstm_kernel, L=L),
        grid=(B, nc),
        in_specs=[pl.BlockSpec((1, L, 2 * ML_QK_W), lambda b, c: (b, c, 0)),
                  blkT(ML_V_W), blkT(ML_V_W), blkT(16), full(conv_w), full(bcol), full(hn)],
        out_specs=pl.BlockSpec((1, L, ML_V_W), lambda b, c: (b, c, 0)),
        out_shape=jax.ShapeDtypeStruct((B, S, ML_V_W), BF16),
        scratch_shapes=[pltpu.VMEM((L + SUBLANES, 2 * ML_QK_W), F32),
                        pltpu.VMEM((ML_HEADS, ML_DV + SUBLANES, ML_DQK), F32),
                        pltpu.VMEM((SUBLANES, LANES), F32)],
        compiler_params=_params("parallel", "arbitrary"),
    )(qk, vT, oT, smT, conv_w.astype(F32), bcol, hn)


def _block_diag_T(qT, heads, dh):
    rid = lax.broadcasted_iota(I32, qT.shape, 0)
    zero = jnp.zeros_like(qT)
    return jnp.concatenate(
        [jnp.where((rid >= h * dh) & (rid < (h + 1) * dh), qT, zero) for h in range(heads)], axis=1)


def _accumulate(h, p, vT_chunk, l, acc_scr, dh):
    rows = slice(h * dh, (h + 1) * dh)
    acc_scr[rows, :] += jnp.dot(vT_chunk[rows, :], p.astype(BF16), preferred_element_type=F32)
    return l + jnp.sum(p, axis=0, keepdims=True)


def _finish_attention(l, acc_scr, y_ref, heads, dh):
    outs = [acc_scr[h * dh:(h + 1) * dh, :] / l[h] for h in range(heads)]
    y_ref[0] = jnp.concatenate(outs, axis=0).T.astype(y_ref.dtype)


def _max_key_norm2(kb, heads, dh, TQ):
    ch = lax.broadcasted_iota(I32, (heads * dh, LANES), 0) // dh
    hd = lax.broadcasted_iota(I32, (heads * dh, LANES), 1)
    n2 = jnp.dot((kb * kb).astype(BF16), (ch == hd).astype(BF16), preferred_element_type=F32)
    n2 = jnp.max(n2, axis=0, keepdims=True) * (1.0 + 2.0 ** -7)
    return jnp.concatenate([jnp.broadcast_to(n2[:, h:h + 1], (1, TQ)) for h in range(heads)], axis=1)


def _norm_bound(qTf, k2, heads, dh, TQ):
    out = []
    for h in range(heads):
        qh = qTf[h * dh:(h + 1) * dh, :]
        q2 = jnp.sum(qh * qh, axis=0, keepdims=True)
        out.append(jnp.sqrt(q2 * k2[:, h * TQ:(h + 1) * TQ]) * (1.0 + 2.0 ** -8))
    return out


def _underflowed(l):
    small = l[0]
    for lh in l[1:]:
        small = jnp.minimum(small, lh)
    return jnp.logical_not(jnp.min(small) >= SOFTMAX_L_MIN)


def _moba_kernel(qT_ref, k_ref, vT_ref, y_ref, km_scr, kn_scr, acc_scr, s_scr, *, NB, NBP, n_sel):
    H, DH, BS = MOBA_HEADS, MOBA_DH, MOBA_BLOCK
    TQ = BS
    qi = pl.program_id(1)

    @pl.when(qi == 0)
    def _():
        km_scr[...] = jnp.zeros_like(km_scr)
        kn_scr[...] = jnp.zeros_like(kn_scr)

        def body(j, carry):
            kb = k_ref[0, pl.ds(pl.multiple_of(j * BS, BS), BS), :].astype(F32)
            km_scr[pl.ds(j, 1), :] = jnp.mean(kb, axis=0, keepdims=True)
            kn_scr[pl.ds(j, 1), :] = _max_key_norm2(kb, H, DH, TQ)
            return carry
        lax.fori_loop(0, NB, body, 0)

    qT = qT_ref[0]
    qTf = qT.astype(F32)
    km = km_scr[...]
    gate = jnp.concatenate([_dot3(km[:, h * DH:(h + 1) * DH], qT[h * DH:(h + 1) * DH, :]) for h in range(H)],
                           axis=1) * (1.0 / ATT_QSCALE)
    blk = lax.broadcasted_iota(I32, gate.shape, 0)
    g = jnp.where(blk < qi, gate, NEG_INF)
    sels = []
    for _ in range(n_sel):
        mx = jnp.max(g, axis=0, keepdims=True)
        isel = jnp.min(jnp.where(g == mx, blk, NBP), axis=0, keepdims=True)
        sels.append(jnp.where(mx > NEG_INF, isel, -1))
        g = jnp.where(blk == isel, NEG_INF, g)

    qs = _block_diag_T(qT, H, DH)

    def hit_row(j):
        hit = sels[0] == j
        for s in sels[1:]:
            hit = hit | (s == j)
        return hit

    blk_start = lambda j: pl.multiple_of(j * BS, BS)
    blk_scores = lambda j: jnp.dot(k_ref[0, pl.ds(blk_start(j), BS), :], qs, preferred_element_type=F32)
    hcols = lambda row, h: row[:, h * TQ:(h + 1) * TQ]
    causal = lax.broadcasted_iota(I32, (BS, TQ), 0) <= lax.broadcasted_iota(I32, (BS, TQ), 1)

    def diag_scores():
        s_all = blk_scores(qi)
        return [jnp.where(causal, hcols(s_all, h), NEG_INF) for h in range(H)]

    def max_sweep():
        def max_block(j, m):
            return jnp.maximum(m, jnp.where(hit_row(j), jnp.max(blk_scores(j), axis=0, keepdims=True), NEG_INF))
        m_all = lax.fori_loop(0, qi, max_block, jnp.full((1, H * TQ), NEG_INF, F32))
        s_diag = diag_scores()
        return [jnp.maximum(hcols(m_all, h), jnp.max(s_diag[h], axis=0, keepdims=True)) for h in range(H)]

    def sum_sweep(m):
        acc_scr[...] = jnp.zeros_like(acc_scr)

        def sum_group(width):
            def body(g, l):
                for u in range(width):
                    s_scr[u] = blk_scores(width * g + u)
                for u in range(width):
                    j = width * g + u
                    hit = hit_row(j)
                    vT_blk = vT_ref[0, :, pl.ds(blk_start(j), BS)]
                    l = tuple(_accumulate(h, jnp.exp2(s_scr[u, :, h * TQ:(h + 1) * TQ]
                                                      - jnp.where(hcols(hit, h), m[h], jnp.inf)),
                                          vT_blk, l[h], acc_scr, DH) for h in range(H))
                return l
            return body
        l = _chunk_loops(qi, ATT_STAGES, sum_group, (jnp.zeros((1, TQ), F32),) * H)
        s_diag = diag_scores()
        vT_diag = vT_ref[0, :, pl.ds(blk_start(qi), BS)]
        return [_accumulate(h, jnp.exp2(s_diag[h] - m[h]), vT_diag, l[h], acc_scr, DH) for h in range(H)]

    blk_n = lax.broadcasted_iota(I32, kn_scr.shape, 0)
    k2 = jnp.max(jnp.where(blk_n <= qi, kn_scr[...], 0.0), axis=0, keepdims=True)
    l = sum_sweep(_norm_bound(qTf, k2, H, DH, TQ))
    _finish_attention(l, acc_scr, y_ref, H, DH)

    @pl.when(_underflowed(l))
    def _():
        _finish_attention(sum_sweep(max_sweep()), acc_scr, y_ref, H, DH)


def _moba(qT, k, vT, B, S):
    BS = MOBA_BLOCK
    assert S % BS == 0
    NB = S // BS
    NBP = -(-NB // SUBLANES) * SUBLANES
    n_sel = max(min(MOBA_TOPK, NB - 1), 1)
    k = k.reshape(B, S, MOBA_W)
    return pl.pallas_call(
        functools.partial(_moba_kernel, NB=NB, NBP=NBP, n_sel=n_sel),
        grid=(B, NB),
        in_specs=[pl.BlockSpec((1, MOBA_W, BS), lambda b, i: (b, 0, i)),
                  pl.BlockSpec((1, S, MOBA_W), lambda b, i: (b, 0, 0)),
                  pl.BlockSpec((1, MOBA_W, S), lambda b, i: (b, 0, 0))],
        out_specs=pl.BlockSpec((1, BS, MOBA_W), lambda b, i: (b, i, 0)),
        out_shape=jax.ShapeDtypeStruct((B, S, MOBA_W), BF16),
        scratch_shapes=[pltpu.VMEM((NBP, MOBA_W), F32),
                        pltpu.VMEM((NBP, MOBA_HEADS * BS), F32),
                        pltpu.VMEM((MOBA_W, BS), F32),
                        pltpu.VMEM((ATT_STAGE, BS, MOBA_HEADS * BS), F32)],
        compiler_params=_params("parallel", "arbitrary"),
    )(qT, k, vT)


def _dsa_kernel(qT_ref, k_ref, vT_ref, iqT_ref, ik_ref, smT_ref, y_ref,
                e_scr, gmax_scr, acc_scr, thr_scr, kn_scr, s_scr, *, TQ, topk, spow, n_chunks):
    H, DH = DSA_HEADS, DSA_DH
    KC = TQ
    qi = pl.program_id(1)
    nch = qi + 1
    chunk_start = lambda c: pl.multiple_of(c * KC, KC)

    @pl.when(qi == 0)
    def _():
        kn_scr[...] = jnp.zeros_like(kn_scr)

        def body(c, carry):
            kn_scr[pl.ds(c, 1), :] = _max_key_norm2(k_ref[0, pl.ds(chunk_start(c), KC), :].astype(F32), H, DH, TQ)
            return carry
        lax.fori_loop(0, n_chunks, body, 0)
    sum8 = lambda b: jnp.sum(b.astype(I32).reshape(KC // SUBLANES, SUBLANES, TQ), axis=0)
    any_row = lambda r: jnp.max(r.astype(I32)) > 0

    iqT = iqT_ref[0]
    iq_cat = jnp.concatenate([iqT[h * IDX_DH:(h + 1) * IDX_DH, :] for h in range(IDX_HEADS)], axis=1)
    iw = smT_ref[0][2 * ML_HEADS:2 * ML_HEADS + IDX_HEADS, :] * (IDX_HEADS ** -0.5 * IDX_DH ** -0.5)
    t_pos = qi * TQ + lax.broadcasted_iota(I32, (KC, TQ), 1)
    s_off = lax.broadcasted_iota(I32, (KC, TQ), 0)
    gmax_scr[...] = jnp.full((KC, TQ), NEG_INF, F32)

    def score_group(width, masked=False):
        def body(g, tally):
            for u in range(width):
                ikc = ik_ref[0, pl.ds(chunk_start(width * g + u), KC), :][:, :IDX_DH]
                s_scr[u] = jnp.dot(ikc, iq_cat, preferred_element_type=F32)
            for u in range(width):
                c = width * g + u
                sc = iw[0:1, :] * jnp.maximum(s_scr[u, :, 0:TQ], 0.0)
                for h in range(1, IDX_HEADS):
                    sc = sc + iw[h:h + 1, :] * jnp.maximum(s_scr[u, :, h * TQ:(h + 1) * TQ], 0.0)
                if masked:
                    sc = jnp.where(c * KC + s_off <= t_pos, sc, NEG_INF)
                e_scr[pl.ds(chunk_start(c), KC), :] = sc
                gmax_scr[...] = jnp.maximum(gmax_scr[...], sc)
                tally = tally + jnp.sum(jnp.where(sc > 0.0, 1, jnp.where(sc == 0.0, 1 << 16, 0))
                                        .reshape(KC // SUBLANES, SUBLANES, TQ), axis=0)
            return tally
        return body
    assert n_chunks * KC < (1 << 16)
    zero8 = jnp.zeros((SUBLANES, TQ), I32)
    tally = _chunk_loops(qi, ATT_STAGES, score_group, zero8)
    tally = jnp.sum(score_group(1, True)(qi, tally), axis=0, keepdims=True)
    n_pos = tally & 0xFFFF
    n_nonneg = n_pos + (tally >> 16)

    def sweep_keys(fold, merge, init):
        def group(width):
            def body(g, acc):
                return tuple(merge(acc[u], fold(e_scr[pl.ds(chunk_start(width * g + u), KC), :]))
                             for u in range(width))
            return body

        def regroup(acc, width):
            half = len(acc) // 2
            acc = tuple(merge(acc[u], acc[u + half]) for u in range(half))
            return acc if half == width else regroup(acc, width)
        return _chunk_loops(nch, ATT_STAGES, group, (init,) * ATT_STAGES[0], regroup)[0]

    def count_gt(cand):
        return jnp.sum(sweep_keys(lambda x: sum8(x > cand), lambda p, q: p + q, zero8), axis=0, keepdims=True)

    def min_gt(cand):
        fold = lambda x: jnp.min(jnp.where(x > cand, x, jnp.inf).reshape(KC // SUBLANES, SUBLANES, TQ), axis=0)
        return jnp.min(sweep_keys(fold, jnp.minimum, jnp.full((SUBLANES, TQ), jnp.inf, F32)), axis=0, keepdims=True)

    def rewrite(rows, thr):
        def body(c, carry):
            x = e_scr[pl.ds(chunk_start(c), KC), :]
            rank = (2 * spow - (c * KC + s_off)).astype(F32)
            new = jnp.where(x > thr, jnp.inf, jnp.where(x == thr, rank, NEG_INF))
            e_scr[pl.ds(chunk_start(c), KC), :] = jnp.where(rows, new, x)
            return carry
        lax.fori_loop(0, nch, body, 0)

    def halve(_, c):
        lo, hi, act, tie, n_lo = c
        mid = lo + (hi - lo) * 0.5
        inside = (mid > lo) & (mid < hi)
        cnt = count_gt(mid)
        live = act > 0.0
        run = live & inside
        up = run & (cnt >= topk)
        lo = jnp.where(up, mid, lo)
        n_lo = jnp.where(up, cnt, n_lo)
        hi = jnp.where(run & (cnt < topk), mid, hi)
        collapsed = live & jnp.logical_not(inside)
        tie = jnp.where(collapsed, 1.0, tie)
        act = jnp.where(collapsed | (run & (cnt == topk)), 0.0, act)
        return lo, hi, act, tie, n_lo

    def excess(state):
        return jnp.max(jnp.where(state[2] > 0.0, state[4] - topk, 0).astype(F32))

    def bisect(lo, hi, act, warmup):
        state = (lo, hi, act, jnp.zeros_like(act), jnp.full((1, TQ), 1 << 20, I32))
        state = lax.fori_loop(0, warmup, halve, state)

        def one_more(c):
            state = halve(0, c[1])
            return c[0] + 1, state, excess(state)
        near = lax.while_loop(lambda c: jnp.logical_and(c[0] < BISECT_CAP, c[2] > BISECT_DROP), one_more,
                              (jnp.int32(warmup), state, excess(state)))
        lo, hi, act, tie, n_lo = near[1]
        cur, left = lo, jnp.where(act > 0.0, n_lo - topk, 0)
        for _ in range(BISECT_DROP):
            cur = jnp.where(left > 0, min_gt(cur), cur)
            left = left - 1
        done = (act > 0.0) & (count_gt(cur) == topk)
        lo = jnp.where(done, cur, lo)
        act = jnp.where(done, 0.0, act)

        def body(c):
            state = c[1]
            for _ in range(BISECT_STEPS_PER_CHECK):
                state = halve(0, state)
            return c[0] + BISECT_STEPS_PER_CHECK, state, jnp.max(state[2])
        out = lax.while_loop(lambda c: jnp.logical_and(c[0] < BISECT_CAP, c[2] > 0.0), body,
                             (near[0], (lo, hi, act, tie, n_lo), jnp.max(act)))
        return out[1][0], out[1][1], out[1][3]

    fmax = float(jnp.finfo(F32).max)
    gm = gmax_scr[...]
    g_lo = jnp.min(gm, axis=0, keepdims=True)
    g_hi = jnp.max(gm, axis=0, keepdims=True)
    n_vis = qi * TQ + lax.broadcasted_iota(I32, (1, TQ), 1) + 1
    few = n_vis <= topk
    zero_tie = (n_pos < topk) & (n_nonneg >= topk) & jnp.logical_not(few)

    @pl.when(any_row(zero_tie))
    def _():
        rewrite(zero_tie, jnp.zeros((1, TQ), F32))

    rank_lo, rank_hi = float(spow), float(2 * spow + 1)
    below = jnp.maximum(g_lo, -fmax)
    below = below - jnp.abs(below) * (2.0 ** -10) - 1e-30
    lo0 = jnp.where(zero_tie, rank_lo, jnp.maximum(below, -fmax))
    hi0 = jnp.where(zero_tie, rank_hi, g_hi)
    lo, hi, tie = bisect(lo0, hi0, jnp.where(few, 0.0, 1.0), BISECT_WARMUP)
    thr_scr[0:1, :] = jnp.where(few, NEG_INF, lo)

    @pl.when(jnp.max(tie) > 0.0)
    def _():
        rows = tie > 0.0
        rewrite(rows, hi)
        lo2, _, _ = bisect(jnp.full((1, TQ), rank_lo, F32), jnp.full((1, TQ), rank_hi, F32), tie, 0)
        thr_scr[0:1, :] = jnp.where(rows, lo2, thr_scr[0:1, :])

    thr = thr_scr[0:1, :]
    qT = qT_ref[0]
    qs = _block_diag_T(qT, H, DH)
    hcols = lambda row, h: row[:, h * TQ:(h + 1) * TQ]

    def chunk_scores(c):
        start = chunk_start(c)
        sel = e_scr[pl.ds(start, KC), :] > thr
        s_all = jnp.dot(k_ref[0, pl.ds(start, KC), :], qs, preferred_element_type=F32)
        return sel, s_all

    def max_sweep():
        def max_chunk(c, m):
            sel, s_all = chunk_scores(c)
            return tuple(jnp.maximum(m[h], jnp.max(jnp.where(sel, hcols(s_all, h), NEG_INF), axis=0, keepdims=True))
                         for h in range(H))
        m = lax.fori_loop(0, nch, max_chunk, (jnp.full((1, TQ), NEG_INF, F32),) * H)
        return [jnp.where(mh == NEG_INF, 0.0, mh) for mh in m]

    def sum_sweep(m):
        acc_scr[...] = jnp.zeros_like(acc_scr)

        def sum_group(width):
            def body(g, l):
                for u in range(width):
                    s_scr[u] = jnp.dot(k_ref[0, pl.ds(chunk_start(width * g + u), KC), :], qs,
                                       preferred_element_type=F32)
                for u in range(width):
                    start = chunk_start(width * g + u)
                    sel = e_scr[pl.ds(start, KC), :] > thr
                    vT_chunk = vT_ref[0, :, pl.ds(start, KC)]
                    l = tuple(_accumulate(h, jnp.exp2(s_scr[u, :, h * TQ:(h + 1) * TQ] + jnp.where(sel, -m[h], NEG_INF)),
                                          vT_chunk, l[h], acc_scr, DH) for h in range(H))
                return l
            return body
        return _chunk_loops(nch, ATT_STAGES, sum_group, (jnp.zeros((1, TQ), F32),) * H)

    chunk_n = lax.broadcasted_iota(I32, kn_scr.shape, 0)
    k2 = jnp.max(jnp.where(chunk_n < nch, kn_scr[...], 0.0), axis=0, keepdims=True)
    l = sum_sweep(_norm_bound(qT.astype(F32), k2, H, DH, TQ))
    _finish_attention(l, acc_scr, y_ref, H, DH)

    @pl.when(_underflowed(l))
    def _():
        _finish_attention(sum_sweep(max_sweep()), acc_scr, y_ref, H, DH)


def _dsa(qT, k, vT, iqT, ik, smT, B, S):
    TQ = min(ATT_TQ, S)
    topk = min(DSA_TOPK_MAX, S // 4)
    assert S % TQ == 0 and topk <= TQ
    spow = 1 << max((S - 1).bit_length(), 1)
    k = k.reshape(B, S, DSA_W)
    ik = ik.reshape(B, S, LANES)
    qblk = lambda n: pl.BlockSpec((1, n, TQ), lambda b, i: (b, 0, i))
    return pl.pallas_call(
        functools.partial(_dsa_kernel, TQ=TQ, topk=topk, spow=spow, n_chunks=S // TQ),
        grid=(B, S // TQ),
        in_specs=[qblk(DSA_W),
                  pl.BlockSpec((1, S, DSA_W), lambda b, i: (b, 0, 0)),
                  pl.BlockSpec((1, DSA_W, S), lambda b, i: (b, 0, 0)),
                  qblk(IDX_HEADS * IDX_DH),
                  pl.BlockSpec((1, S, LANES), lambda b, i: (b, 0, 0)),
                  qblk(16)],
        out_specs=pl.BlockSpec((1, TQ, DSA_W), lambda b, i: (b, i, 0)),
        out_shape=jax.ShapeDtypeStruct((B, S, DSA_W), BF16),
        scratch_shapes=[pltpu.VMEM((S, TQ), F32),
                        pltpu.VMEM((TQ, TQ), F32),
                        pltpu.VMEM((DSA_W, TQ), F32),
                        pltpu.VMEM((SUBLANES, TQ), F32),
                        pltpu.VMEM((-(-(S // TQ) // SUBLANES) * SUBLANES, DSA_HEADS * TQ), F32),
                        pltpu.VMEM((ATT_STAGE, TQ, DSA_HEADS * TQ), F32)],
        compiler_params=_params("parallel", "arbitrary"),
    )(qT, k, vT, iqT, ik, smT)


def _out_mlp_kernel(yml_ref, ymb_ref, yds_ref, x_ref, wout_ref, gpost_ref, gpre_ref,
                    w1_ref, w2_ref, g2_ref, out_ref, h_scr, acc_scr):
    f = pl.program_id(1)

    @pl.when(f == 0)
    def _():
        slab = x_ref.shape[0] // MLP_SLABS
        for r in range(MLP_SLABS):
            rows = slice(r * slab, (r + 1) * slab)
            mix = jnp.dot(yml_ref[rows, :], wout_ref[0:ML_V_W, :], preferred_element_type=F32)
            mix = mix + jnp.dot(ymb_ref[rows, :], wout_ref[ML_V_W:ML_V_W + MOBA_W, :], preferred_element_type=F32)
            mix = mix + jnp.dot(yds_ref[rows, :], wout_ref[ML_V_W + MOBA_W:, :], preferred_element_type=F32)
            x1 = x_ref[rows, :] + _rms(mix, gpost_ref[...])
            out_ref[rows, :] = x1
            h_scr[rows, :] = _rms(x1, gpre_ref[...]).astype(BF16)
        acc_scr[...] = jnp.zeros_like(acc_scr)

    u = jnp.maximum(jnp.dot(h_scr[...], w1_ref[...], preferred_element_type=F32), 0.0)
    acc_scr[...] += jnp.dot((u * u).astype(BF16), w2_ref[...], preferred_element_type=F32)

    @pl.when(f == pl.num_programs(1) - 1)
    def _():
        out_ref[...] = out_ref[...] + _rms(acc_scr[...], g2_ref[...])


def _out_mlp(yml, ymb, yds, x2, w_out, g_post, g_pre, w1, w2, g2):
    T, D = x2.shape
    tm = min(MLP_TM, T)
    tf = min(MLP_TF, D_FF)
    row = lambda n: pl.BlockSpec((tm, n), lambda i, f: (i, 0))
    full = lambda a: pl.BlockSpec(a.shape, lambda i, f: (0,) * a.ndim)
    return pl.pallas_call(
        _out_mlp_kernel,
        grid=(T // tm, D_FF // tf),
        in_specs=[row(ML_V_W), row(MOBA_W), row(DSA_W), row(D), full(w_out), full(g_post), full(g_pre),
                  pl.BlockSpec((D, tf), lambda i, f: (0, f)),
                  pl.BlockSpec((tf, D), lambda i, f: (f, 0)),
                  full(g2)],
        out_specs=row(D),
        out_shape=jax.ShapeDtypeStruct((T, D), F32),
        scratch_shapes=[pltpu.VMEM((tm, D), BF16), pltpu.VMEM((tm, D), F32)],
        compiler_params=_params("parallel", "arbitrary"),
    )(yml, ymb, yds, x2, w_out, g_post, g_pre, w1, w2, g2)


def kernel(x, norm_mix_pre, w_in, ml_conv, ml_i_bias, ml_f_bias, ml_head_norm, w_out, norm_mix_post,
           norm_mlp_pre, w_ff1, w_ff2, norm_mlp_post):
    B, S, D = x.shape
    depth = w_in.shape[0]
    x2 = x.reshape(B * S, D)
    gain = lambda g: g.reshape(1, D).astype(F32)
    for l in range(depth):
        wn, wt = _prep_in_weights(w_in[l])
        (qk, mbk, dsk, ixk, mbqT, mbvT, dsqT, dsvT, ixqT, mlvT, mloT, smT) = _in_proj(
            x2, gain(norm_mix_pre[l]), wn, wt, B, S)
        y_ml = _mlstm(qk, mlvT, mloT, smT, ml_conv[l], ml_i_bias[l], ml_f_bias[l], ml_head_norm[l], B, S)
        y_mb = _moba(mbqT, mbk, mbvT, B, S)
        y_ds = _dsa(dsqT, dsk, dsvT, ixqT, ixk, smT, B, S)
        x2 = _out_mlp(y_ml.reshape(B * S, ML_V_W), y_mb.reshape(B * S, MOBA_W), y_ds.reshape(B * S, DSA_W),
                      x2, w_out[l].astype(BF16), gain(norm_mix_post[l]), gain(norm_mlp_pre[l]),
                      w_ff1[l].astype(BF16), w_ff2[l].astype(BF16), gain(norm_mlp_post[l]))
    return x2.reshape(B, S, D)
```

```python
import functools

import jax
import jax.numpy as jnp
from jax import lax
from jax.experimental import pallas as pl
from jax.experimental.pallas import tpu as pltpu

F32 = jnp.float32
BF16 = jnp.bfloat16
I32 = jnp.int32

D_MODEL = 1024
ML_HEADS, ML_DQK, ML_DV, ML_CONV = 4, 64, 128, 4
MOBA_HEADS, MOBA_DH, MOBA_BLOCK, MOBA_TOPK = 4, 64, 256, 3
DSA_HEADS, DSA_DH, IDX_HEADS, IDX_DH, DSA_TOPK_MAX = 4, 64, 4, 64, 256
D_FF = 4 * D_MODEL
EPS = 1e-6

ML_QK_W = ML_HEADS * ML_DQK
ML_V_W = ML_HEADS * ML_DV
MOBA_W = MOBA_HEADS * MOBA_DH
DSA_W = DSA_HEADS * DSA_DH
IN_SPLITS = (ML_QK_W, ML_QK_W, ML_V_W, ML_V_W, ML_HEADS, ML_HEADS,
             MOBA_W, MOBA_W, MOBA_W,
             DSA_W, DSA_W, DSA_W, IDX_HEADS * IDX_DH, IDX_DH, IDX_HEADS)

LANES = 128
SUBLANES = 8
VMEM_LIMIT = 52 * 1024 * 1024

ML_CHUNK = 512
ATT_TQ = 256
ATT_STAGES = (8, 4, 2, 1)
ATT_STAGE = ATT_STAGES[0]
PROJ_TM = 1024
MLP_TM = 1024
MLP_TF = 2048
MLP_SLABS = 4

NEG_INF = float("-inf")
BISECT_CAP = 300
BISECT_WARMUP = 12
BISECT_DROP = 2
BISECT_STEPS_PER_CHECK = 2
SOFTMAX_L_MIN = 1e-30
assert MOBA_DH == DSA_DH
ATT_QSCALE = MOBA_DH ** -0.5 * 1.4426950408889634

_NT = (((1,), (1,)), ((), ()))


def _chunk_loops(n, widths, make_body, carry, regroup=lambda c, w: c):
    covered = 0
    for i, w in enumerate(widths):
        if i:
            carry = regroup(carry, w)
        carry = lax.fori_loop(covered // w, n // w, make_body(w), carry)
        covered = w * (n // w)
    return carry


def _rms(x, g):
    return x * lax.rsqrt(jnp.mean(x * x, axis=-1, keepdims=True) + EPS) * g


def _log_sigmoid(x):
    return jnp.minimum(x, 0.0) - jnp.log1p(jnp.exp(-jnp.abs(x)))


def _params(*sem):
    return pltpu.CompilerParams(dimension_semantics=sem, vmem_limit_bytes=VMEM_LIMIT)


_N_QK, _N_MBK, _N_DSK, _N_IXK = 0, 512, 768, 1024
_N_TOTAL = 1152
_T_MBQ, _T_MBV, _T_DSQ, _T_DSV, _T_IXQ, _T_MLV, _T_MLO, _T_SM = 0, 256, 512, 768, 1024, 1280, 1792, 2304
_T_TOTAL = 2320


def _in_proj_kernel(x_ref, g_ref, wn_ref, wt_ref,
                    qk_ref, mbk_ref, dsk_ref, ixk_ref,
                    mbqT_ref, mbvT_ref, dsqT_ref, dsvT_ref, ixqT_ref, mlvT_ref, mloT_ref, smT_ref):
    h = _rms(x_ref[...], g_ref[...]).astype(BF16)

    def mm(a, n):
        return jnp.dot(h, wn_ref[:, a:a + n], preferred_element_type=F32)

    def mt(a, n):
        return lax.dot_general(wt_ref[a:a + n, :], h, _NT, preferred_element_type=F32)

    qk_ref[...] = mm(_N_QK, 512)
    mbk_ref[...] = mm(_N_MBK, 256).astype(BF16)
    dsk_ref[...] = mm(_N_DSK, 256).astype(BF16)
    ixk_ref[...] = mm(_N_IXK, 128).astype(BF16)
    mbqT_ref[0] = (mt(_T_MBQ, 256) * ATT_QSCALE).astype(BF16)
    mbvT_ref[0] = mt(_T_MBV, 256).astype(BF16)
    dsqT_ref[0] = (mt(_T_DSQ, 256) * ATT_QSCALE).astype(BF16)
    dsvT_ref[0] = mt(_T_DSV, 256).astype(BF16)
    ixqT_ref[0] = mt(_T_IXQ, 256).astype(BF16)
    mlvT_ref[0] = mt(_T_MLV, 512).astype(BF16)
    mloT_ref[0] = mt(_T_MLO, 512)
    smT_ref[0] = mt(_T_SM, 16)


def _prep_in_weights(w_in):
    pts, acc = [], 0
    for n in IN_SPLITS:
        pts.append((acc, acc + n))
        acc += n
    col = lambda i: w_in[:, pts[i][0]:pts[i][1]]
    (ml_q, ml_k, ml_v, ml_o, ml_i, ml_f, mb_q, mb_k, mb_v,
     ds_q, ds_k, ds_v, ix_q, ix_k, ix_w) = [col(i) for i in range(len(IN_SPLITS))]
    d = w_in.shape[0]
    small = jnp.concatenate([ml_i, ml_f, ix_w], axis=1)
    wn = jnp.concatenate([
        ml_q, ml_k, mb_k, ds_k,
        ix_k, jnp.zeros((d, 128 - IDX_DH), w_in.dtype)], axis=1)
    wt = jnp.concatenate([
        mb_q, mb_v, ds_q, ds_v, ix_q, ml_v, ml_o,
        small, jnp.zeros((d, 16 - small.shape[1]), w_in.dtype)], axis=1).T
    assert wn.shape == (d, _N_TOTAL) and wt.shape == (_T_TOTAL, d)
    return wn.astype(BF16), wt.astype(BF16)


def _in_proj(x2, g, wn, wt, B, S):
    T, D = x2.shape
    tm = min(PROJ_TM, S)
    nsb = S // tm
    row = lambda n: pl.BlockSpec((tm, n), lambda i: (i, 0))
    tr = lambda n: pl.BlockSpec((1, n, tm), lambda i: (i // nsb, 0, i % nsb))
    full = lambda a: pl.BlockSpec(a.shape, lambda i: (0,) * a.ndim)
    out_shape = (
        jax.ShapeDtypeStruct((T, 512), F32),
        jax.ShapeDtypeStruct((T, 256), BF16),
        jax.ShapeDtypeStruct((T, 256), BF16),
        jax.ShapeDtypeStruct((T, 128), BF16),
        jax.ShapeDtypeStruct((B, 256, S), BF16),
        jax.ShapeDtypeStruct((B, 256, S), BF16),
        jax.ShapeDtypeStruct((B, 256, S), BF16),
        jax.ShapeDtypeStruct((B, 256, S), BF16),
        jax.ShapeDtypeStruct((B, 256, S), BF16),
        jax.ShapeDtypeStruct((B, 512, S), BF16),
        jax.ShapeDtypeStruct((B, 512, S), F32),
        jax.ShapeDtypeStruct((B, 16, S), F32),
    )
    out_specs = (row(512), row(256), row(256), row(128),
                 tr(256), tr(256), tr(256), tr(256), tr(256), tr(512), tr(512), tr(16))
    return pl.pallas_call(
        _in_proj_kernel,
        grid=(T // tm,),
        in_specs=[row(D), full(g), full(wn), full(wt)],
        out_specs=out_specs,
        out_shape=out_shape,
        compiler_params=_params("parallel"),
    )(x2, g, wn, wt)


def _dot3(a_f32, b_exact):
    b = b_exact.astype(BF16)
    hi = a_f32.astype(BF16)
    r1 = a_f32 - hi.astype(F32)
    mid = r1.astype(BF16)
    lo = (r1 - mid.astype(F32)).astype(BF16)
    return (jnp.dot(hi, b, preferred_element_type=F32) + jnp.dot(mid, b, preferred_element_type=F32)
            + jnp.dot(lo, b, preferred_element_type=F32))


def _mlstm_kernel(qk_ref, vT_ref, oT_ref, smT_ref, conv_ref, bcol_ref, hn_ref,
                  y_ref, xp_scr, st_scr, m_scr, *, L):
    c = pl.program_id(1)
    halo = SUBLANES
    DK, DV = ML_DQK, ML_DV

    @pl.when(c == 0)
    def _():
        xp_scr[0:halo, :] = jnp.zeros((halo, 2 * ML_QK_W), F32)
        st_scr[...] = jnp.zeros_like(st_scr)
        m_scr[...] = jnp.zeros_like(m_scr)

    cur = qk_ref[0]
    xp_scr[halo:halo + L, :] = cur
    base = halo - (ML_CONV - 1)
    acc = conv_ref[0:1, :] * xp_scr[base:base + L, :]
    for j in range(1, ML_CONV):
        acc = acc + conv_ref[j:j + 1, :] * xp_scr[base + j:base + j + L, :]
    xp_scr[0:halo, :] = cur[L - halo:L, :]
    qk = acc * jax.nn.sigmoid(acc)
    qT_all = (qk[:, :ML_QK_W] * (DK ** -0.5)).T.astype(BF16)
    k_all = qk[:, ML_QK_W:].astype(BF16)

    grow = smT_ref[0][0:SUBLANES, :] + bcol_ref[...]
    si = lax.broadcasted_iota(I32, (L, L), 0)
    ji = lax.broadcasted_iota(I32, (L, L), 1)
    causal = si <= ji
    b_row = _dot3(_log_sigmoid(grow), causal)

    for h in range(ML_HEADS):
        qT = qT_all[h * DK:(h + 1) * DK, :]
        kh = k_all[:, h * DK:(h + 1) * DK]
        vT = vT_ref[0, h * DV:(h + 1) * DV, :]
        b_j = b_row[ML_HEADS + h:ML_HEADS + h + 1, :]
        c_row = grow[h:h + 1, :] - b_j
        c_col = jnp.broadcast_to(c_row, (LANES, L)).T
        dlog = jnp.where(causal, b_j + jnp.concatenate([c_col] * (L // LANES), axis=1), NEG_INF)
        m0 = m_scr[h:h + 1, 0:1]
        inter = b_j + m0
        m_t = jnp.maximum(inter, jnp.max(dlog, axis=0, keepdims=True))
        w_inter = jnp.exp(inter - m_t)
        w_intra = jnp.exp(dlog - m_t) * jnp.dot(kh, qT, preferred_element_type=F32)
        st = st_scr[h]
        cq = jnp.dot(st.astype(BF16), qT, preferred_element_type=F32)
        num = w_inter * cq[:DV] + jnp.dot(vT, w_intra.astype(BF16), preferred_element_type=F32)
        den = w_inter * cq[DV:DV + 1] + jnp.sum(w_intra, axis=0, keepdims=True)
        hh = num / jnp.maximum(jnp.abs(den), jnp.exp(-m_t))
        hh = hh * lax.rsqrt(jnp.mean(hh * hh, axis=0, keepdims=True) + EPS)
        rows = slice(h * DV, (h + 1) * DV)
        y_ref[0, :, rows] = (jax.nn.sigmoid(oT_ref[0, rows, :]) * hh * hn_ref[rows, :]).T.astype(y_ref.dtype)

        b_last = b_j[:, L - 1:L]
        a = b_last + c_row
        m_loc = jnp.max(a, axis=1, keepdims=True)
        wa = jnp.exp(a - m_loc)
        lhs = jnp.concatenate([(vT.astype(F32) * wa).astype(BF16),
                               jnp.broadcast_to(wa, (SUBLANES, L)).astype(BF16)], axis=0)
        c_loc = jnp.dot(lhs, kh, preferred_element_type=F32)
        m_new = jnp.maximum(b_last + m0, m_loc)
        s_old = jnp.exp(b_last + m0 - m_new)
        s_loc = jnp.exp(m_loc - m_new)
        st_scr[h] = s_old * st + s_loc * c_loc
        m_scr[h:h + 1, :] = jnp.broadcast_to(m_new, (1, LANES))


def _mlstm(qk, vT, oT, smT, conv_w, i_bias, f_bias, head_norm, B, S):
    L = min(ML_CHUNK, S)
    assert L % LANES == 0
    nc = S // L
    qk = qk.reshape(B, S, 2 * ML_QK_W)
    bcol = jnp.concatenate([i_bias, f_bias]).astype(F32).reshape(2 * ML_HEADS, 1)
    hn = jnp.broadcast_to(head_norm.astype(F32)[:, None], (ML_V_W, L))
    blkT = lambda n: pl.BlockSpec((1, n, L), lambda b, c: (b, 0, c))
    full = lambda a: pl.BlockSpec(a.shape, lambda b, c: (0,) * a.ndim)
    return pl.pallas_call(
        functools.partial(_mlstm_kernel, L=L),
        grid=(B, nc),
        in_specs=[pl.BlockSpec((1, L, 2 * ML_QK_W), lambda b, c: (b, c, 0)),
                  blkT(ML_V_W), blkT(ML_V_W), blkT(16), full(conv_w), full(bcol), full(hn)],
        out_specs=pl.BlockSpec((1, L, ML_V_W), lambda b, c: (b, c, 0)),
        out_shape=jax.ShapeDtypeStruct((B, S, ML_V_W), BF16),
        scratch_shapes=[pltpu.VMEM((L + SUBLANES, 2 * ML_QK_W), F32),
                        pltpu.VMEM((ML_HEADS, ML_DV + SUBLANES, ML_DQK), F32),
                        pltpu.VMEM((SUBLANES, LANES), F32)],
        compiler_params=_params("parallel", "arbitrary"),
    )(qk, vT, oT, smT, conv_w.astype(F32), bcol, hn)


def _block_diag_T(qT, heads, dh):
    rid = lax.broadcasted_iota(I32, qT.shape, 0)
    zero = jnp.zeros_like(qT)
    return jnp.concatenate(
        [jnp.where((rid >= h * dh) & (rid < (h + 1) * dh), qT, zero) for h in range(heads)], axis=1)


def _accumulate(h, p, vT_chunk, l, acc_scr, dh):
    rows = slice(h * dh, (h + 1) * dh)
    acc_scr[rows, :] += jnp.dot(vT_chunk[rows, :], p.astype(BF16), preferred_element_type=F32)
    return l + jnp.sum(p, axis=0, keepdims=True)


def _finish_attention(l, acc_scr, y_ref, heads, dh):
    outs = [acc_scr[h * dh:(h + 1) * dh, :] / l[h] for h in range(heads)]
    y_ref[0] = jnp.concatenate(outs, axis=0).T.astype(y_ref.dtype)


def _max_key_norm2(kb, heads, dh, TQ):
    ch = lax.broadcasted_iota(I32, (heads * dh, LANES), 0) // dh
    hd = lax.broadcasted_iota(I32, (heads * dh, LANES), 1)
    n2 = jnp.dot((kb * kb).astype(BF16), (ch == hd).astype(BF16), preferred_element_type=F32)
    n2 = jnp.max(n2, axis=0, keepdims=True) * (1.0 + 2.0 ** -7)
    return jnp.concatenate([jnp.broadcast_to(n2[:, h:h + 1], (1, TQ)) for h in range(heads)], axis=1)


def _norm_bound(qTf, k2, heads, dh, TQ):
    out = []
    for h in range(heads):
        qh = qTf[h * dh:(h + 1) * dh, :]
        q2 = jnp.sum(qh * qh, axis=0, keepdims=True)
        out.append(jnp.sqrt(q2 * k2[:, h * TQ:(h + 1) * TQ]) * (1.0 + 2.0 ** -8))
    return out


def _underflowed(l):
    small = l[0]
    for lh in l[1:]:
        small = jnp.minimum(small, lh)
    return jnp.logical_not(jnp.min(small) >= SOFTMAX_L_MIN)


def _moba_kernel(qT_ref, k_ref, vT_ref, y_ref, km_scr, kn_scr, acc_scr, s_scr, *, NB, NBP, n_sel):
    H, DH, BS = MOBA_HEADS, MOBA_DH, MOBA_BLOCK
    TQ = BS
    qi = pl.program_id(1)

    @pl.when(qi == 0)
    def _():
        km_scr[...] = jnp.zeros_like(km_scr)
        kn_scr[...] = jnp.zeros_like(kn_scr)

        def body(j, carry):
            kb = k_ref[0, pl.ds(pl.multiple_of(j * BS, BS), BS), :].astype(F32)
            km_scr[pl.ds(j, 1), :] = jnp.mean(kb, axis=0, keepdims=True)
            kn_scr[pl.ds(j, 1), :] = _max_key_norm2(kb, H, DH, TQ)
            return carry
        lax.fori_loop(0, NB, body, 0)

    qT = qT_ref[0]
    qTf = qT.astype(F32)
    km = km_scr[...]
    gate = jnp.concatenate([_dot3(km[:, h * DH:(h + 1) * DH], qT[h * DH:(h + 1) * DH, :]) for h in range(H)],
                           axis=1) * (1.0 / ATT_QSCALE)
    blk = lax.broadcasted_iota(I32, gate.shape, 0)
    g = jnp.where(blk < qi, gate, NEG_INF)
    sels = []
    for _ in range(n_sel):
        mx = jnp.max(g, axis=0, keepdims=True)
        isel = jnp.min(jnp.where(g == mx, blk, NBP), axis=0, keepdims=True)
        sels.append(jnp.where(mx > NEG_INF, isel, -1))
        g = jnp.where(blk == isel, NEG_INF, g)

    qs = _block_diag_T(qT, H, DH)

    def hit_row(j):
        hit = sels[0] == j
        for s in sels[1:]:
            hit = hit | (s == j)
        return hit

    blk_start = lambda j: pl.multiple_of(j * BS, BS)
    blk_scores = lambda j: jnp.dot(k_ref[0, pl.ds(blk_start(j), BS), :], qs, preferred_element_type=F32)
    hcols = lambda row, h: row[:, h * TQ:(h + 1) * TQ]
    causal = lax.broadcasted_iota(I32, (BS, TQ), 0) <= lax.broadcasted_iota(I32, (BS, TQ), 1)

    def diag_scores():
        s_all = blk_scores(qi)
        return [jnp.where(causal, hcols(s_all, h), NEG_INF) for h in range(H)]

    def max_sweep():
        def max_block(j, m):
            return jnp.maximum(m, jnp.where(hit_row(j), jnp.max(blk_scores(j), axis=0, keepdims=True), NEG_INF))
        m_all = lax.fori_loop(0, qi, max_block, jnp.full((1, H * TQ), NEG_INF, F32))
        s_diag = diag_scores()
        return [jnp.maximum(hcols(m_all, h), jnp.max(s_diag[h], axis=0, keepdims=True)) for h in range(H)]

    def sum_sweep(m):
        acc_scr[...] = jnp.zeros_like(acc_scr)

        def sum_group(width):
            def body(g, l):
                for u in range(width):
                    s_scr[u] = blk_scores(width * g + u)
                for u in range(width):
                    j = width * g + u
                    hit = hit_row(j)
                    vT_blk = vT_ref[0, :, pl.ds(blk_start(j), BS)]
                    l = tuple(_accumulate(h, jnp.exp2(s_scr[u, :, h * TQ:(h + 1) * TQ]
                                                      - jnp.where(hcols(hit, h), m[h], jnp.inf)),
                                          vT_blk, l[h], acc_scr, DH) for h in range(H))
                return l
            return body
        l = _chunk_loops(qi, ATT_STAGES, sum_group, (jnp.zeros((1, TQ), F32),) * H)
        s_diag = diag_scores()
        vT_diag = vT_ref[0, :, pl.ds(blk_start(qi), BS)]
        return [_accumulate(h, jnp.exp2(s_diag[h] - m[h]), vT_diag, l[h], acc_scr, DH) for h in range(H)]

    blk_n = lax.broadcasted_iota(I32, kn_scr.shape, 0)
    k2 = jnp.max(jnp.where(blk_n <= qi, kn_scr[...], 0.0), axis=0, keepdims=True)
    l = sum_sweep(_norm_bound(qTf, k2, H, DH, TQ))
    _finish_attention(l, acc_scr, y_ref, H, DH)

    @pl.when(_underflowed(l))
    def _():
        _finish_attention(sum_sweep(max_sweep()), acc_scr, y_ref, H, DH)


def _moba(qT, k, vT, B, S):
    BS = MOBA_BLOCK
    assert S % BS == 0
    NB = S // BS
    NBP = -(-NB // SUBLANES) * SUBLANES
    n_sel = max(min(MOBA_TOPK, NB - 1), 1)
    k = k.reshape(B, S, MOBA_W)
    return pl.pallas_call(
        functools.partial(_moba_kernel, NB=NB, NBP=NBP, n_sel=n_sel),
        grid=(B, NB),
        in_specs=[pl.BlockSpec((1, MOBA_W, BS), lambda b, i: (b, 0, i)),
                  pl.BlockSpec((1, S, MOBA_W), lambda b, i: (b, 0, 0)),
                  pl.BlockSpec((1, MOBA_W, S), lambda b, i: (b, 0, 0))],
        out_specs=pl.BlockSpec((1, BS, MOBA_W), lambda b, i: (b, i, 0)),
        out_shape=jax.ShapeDtypeStruct((B, S, MOBA_W), BF16),
        scratch_shapes=[pltpu.VMEM((NBP, MOBA_W), F32),
                        pltpu.VMEM((NBP, MOBA_HEADS * BS), F32),
                        pltpu.VMEM((MOBA_W, BS), F32),
                        pltpu.VMEM((ATT_STAGE, BS, MOBA_HEADS * BS), F32)],
        compiler_params=_params("parallel", "arbitrary"),
    )(qT, k, vT)


def _dsa_kernel(qT_ref, k_ref, vT_ref, iqT_ref, ik_ref, smT_ref, y_ref,
                e_scr, gmax_scr, acc_scr, thr_scr, kn_scr, s_scr, *, TQ, topk, spow, n_chunks):
    H, DH = DSA_HEADS, DSA_DH
    KC = TQ
    qi = pl.program_id(1)
    nch = qi + 1
    chunk_start = lambda c: pl.multiple_of(c * KC, KC)

    @pl.when(qi == 0)
    def _():
        kn_scr[...] = jnp.zeros_like(kn_scr)

        def body(c, carry):
            kn_scr[pl.ds(c, 1), :] = _max_key_norm2(k_ref[0, pl.ds(chunk_start(c), KC), :].astype(F32), H, DH, TQ)
            return carry
        lax.fori_loop(0, n_chunks, body, 0)
    sum8 = lambda b: jnp.sum(b.astype(I32).reshape(KC // SUBLANES, SUBLANES, TQ), axis=0)
    any_row = lambda r: jnp.max(r.astype(I32)) > 0

    iqT = iqT_ref[0]
    iq_cat = jnp.concatenate([iqT[h * IDX_DH:(h + 1) * IDX_DH, :] for h in range(IDX_HEADS)], axis=1)
    iw = smT_ref[0][2 * ML_HEADS:2 * ML_HEADS + IDX_HEADS, :] * (IDX_HEADS ** -0.5 * IDX_DH ** -0.5)
    t_pos = qi * TQ + lax.broadcasted_iota(I32, (KC, TQ), 1)
    s_off = lax.broadcasted_iota(I32, (KC, TQ), 0)
    gmax_scr[...] = jnp.full((KC, TQ), NEG_INF, F32)

    def score_group(width, masked=False):
        def body(g, tally):
            for u in range(width):
                ikc = ik_ref[0, pl.ds(chunk_start(width * g + u), KC), :][:, :IDX_DH]
                s_scr[u] = jnp.dot(ikc, iq_cat, preferred_element_type=F32)
            for u in range(width):
                c = width * g + u
                sc = iw[0:1, :] * jnp.maximum(s_scr[u, :, 0:TQ], 0.0)
                for h in range(1, IDX_HEADS):
                    sc = sc + iw[h:h + 1, :] * jnp.maximum(s_scr[u, :, h * TQ:(h + 1) * TQ], 0.0)
                if masked:
                    sc = jnp.where(c * KC + s_off <= t_pos, sc, NEG_INF)
                e_scr[pl.ds(chunk_start(c), KC), :] = sc
                gmax_scr[...] = jnp.maximum(gmax_scr[...], sc)
                tally = tally + jnp.sum(jnp.where(sc > 0.0, 1, jnp.where(sc == 0.0, 1 << 16, 0))
                                        .reshape(KC // SUBLANES, SUBLANES, TQ), axis=0)
            return tally
        return body
    assert n_chunks * KC < (1 << 16)
    zero8 = jnp.zeros((SUBLANES, TQ), I32)
    tally = _chunk_loops(qi, ATT_STAGES, score_group, zero8)
    tally = jnp.sum(score_group(1, True)(qi, tally), axis=0, keepdims=True)
    n_pos = tally & 0xFFFF
    n_nonneg = n_pos + (tally >> 16)

    def sweep_keys(fold, merge, init):
        def group(width):
            def body(g, acc):
                return tuple(merge(acc[u], fold(e_scr[pl.ds(chunk_start(width * g + u), KC), :]))
                             for u in range(width))
            return body

        def regroup(acc, width):
            half = len(acc) // 2
            acc = tuple(merge(acc[u], acc[u + half]) for u in range(half))
            return acc if half == width else regroup(acc, width)
        return _chunk_loops(nch, ATT_STAGES, group, (init,) * ATT_STAGES[0], regroup)[0]

    def count_gt(cand):
        return jnp.sum(sweep_keys(lambda x: sum8(x > cand), lambda p, q: p + q, zero8), axis=0, keepdims=True)

    def min_gt(cand):
        fold = lambda x: jnp.min(jnp.where(x > cand, x, jnp.inf).reshape(KC // SUBLANES, SUBLANES, TQ), axis=0)
        return jnp.min(sweep_keys(fold, jnp.minimum, jnp.full((SUBLANES, TQ), jnp.inf, F32)), axis=0, keepdims=True)

    def rewrite(rows, thr):
        def body(c, carry):
            x = e_scr[pl.ds(chunk_start(c), KC), :]
            rank = (2 * spow - (c * KC + s_off)).astype(F32)
            new = jnp.where(x > thr, jnp.inf, jnp.where(x == thr, rank, NEG_INF))
            e_scr[pl.ds(chunk_start(c), KC), :] = jnp.where(rows, new, x)
            return carry
        lax.fori_loop(0, nch, body, 0)

    def halve(_, c):
        lo, hi, act, tie, n_lo = c
        mid = lo + (hi - lo) * 0.5
        inside = (mid > lo) & (mid < hi)
        cnt = count_gt(mid)
        live = act > 0.0
        run = live & inside
        up = run & (cnt >= topk)
        lo = jnp.where(up, mid, lo)
        n_lo = jnp.where(up, cnt, n_lo)
        hi = jnp.where(run & (cnt < topk), mid, hi)
        collapsed = live & jnp.logical_not(inside)
        tie = jnp.where(collapsed, 1.0, tie)
        act = jnp.where(collapsed | (run & (cnt == topk)), 0.0, act)
        return lo, hi, act, tie, n_lo

    def excess(state):
        return jnp.max(jnp.where(state[2] > 0.0, state[4] - topk, 0).astype(F32))

    def bisect(lo, hi, act, warmup):
        state = (lo, hi, act, jnp.zeros_like(act), jnp.full((1, TQ), 1 << 20, I32))
        state = lax.fori_loop(0, warmup, halve, state)

        def one_more(c):
            state = halve(0, c[1])
            return c[0] + 1, state, excess(state)
        near = lax.while_loop(lambda c: jnp.logical_and(c[0] < BISECT_CAP, c[2] > BISECT_DROP), one_more,
                              (jnp.int32(warmup), state, excess(state)))
        lo, hi, act, tie, n_lo = near[1]
        cur, left = lo, jnp.where(act > 0.0, n_lo - topk, 0)
        for _ in range(BISECT_DROP):
            cur = jnp.where(left > 0, min_gt(cur), cur)
            left = left - 1
        done = (act > 0.0) & (count_gt(cur) == topk)
        lo = jnp.where(done, cur, lo)
        act = jnp.where(done, 0.0, act)

        def body(c):
            state = c[1]
            for _ in range(BISECT_STEPS_PER_CHECK):
                state = halve(0, state)
            return c[0] + BISECT_STEPS_PER_CHECK, state, jnp.max(state[2])
        pending = jnp.max(jnp.maximum(act, tie))
        out = lax.while_loop(lambda c: jnp.logical_and(c[0] < BISECT_CAP, c[2] > 0.0), body,
                             (near[0], (lo, hi, act, tie, n_lo), pending))
        return out[1][0], out[1][1], out[1][3], pending

    fmax = float(jnp.finfo(F32).max)
    gm = gmax_scr[...]
    g_lo = jnp.min(gm, axis=0, keepdims=True)
    g_hi = jnp.max(gm, axis=0, keepdims=True)
    n_vis = qi * TQ + lax.broadcasted_iota(I32, (1, TQ), 1) + 1
    few = n_vis <= topk
    zero_tie = (n_pos < topk) & (n_nonneg >= topk) & jnp.logical_not(few)

    @pl.when(any_row(zero_tie))
    def _():
        rewrite(zero_tie, jnp.zeros((1, TQ), F32))

    rank_lo, rank_hi = float(spow), float(2 * spow + 1)
    below = jnp.maximum(g_lo, -fmax)
    below = below - jnp.abs(below) * (2.0 ** -10) - 1e-30
    lo0 = jnp.where(zero_tie, rank_lo, jnp.maximum(below, -fmax))
    hi0 = jnp.where(zero_tie, rank_hi, g_hi)
    lo, hi, tie, pending = bisect(lo0, hi0, jnp.where(few, 0.0, 1.0), BISECT_WARMUP)
    thr_scr[0:1, :] = jnp.where(few, NEG_INF, lo)

    @pl.when(pending > 0.0)
    def _():
        @pl.when(jnp.max(tie) > 0.0)
        def _():
            rows = tie > 0.0
            rewrite(rows, hi)
            lo2, _, _, _ = bisect(jnp.full((1, TQ), rank_lo, F32), jnp.full((1, TQ), rank_hi, F32), tie, 0)
            thr_scr[0:1, :] = jnp.where(rows, lo2, thr_scr[0:1, :])

    thr = thr_scr[0:1, :]
    qT = qT_ref[0]
    qs = _block_diag_T(qT, H, DH)
    hcols = lambda row, h: row[:, h * TQ:(h + 1) * TQ]

    def chunk_scores(c):
        start = chunk_start(c)
        sel = e_scr[pl.ds(start, KC), :] > thr
        s_all = jnp.dot(k_ref[0, pl.ds(start, KC), :], qs, preferred_element_type=F32)
        return sel, s_all

    def max_sweep():
        def max_chunk(c, m):
            sel, s_all = chunk_scores(c)
            return tuple(jnp.maximum(m[h], jnp.max(jnp.where(sel, hcols(s_all, h), NEG_INF), axis=0, keepdims=True))
                         for h in range(H))
        m = lax.fori_loop(0, nch, max_chunk, (jnp.full((1, TQ), NEG_INF, F32),) * H)
        return [jnp.where(mh == NEG_INF, 0.0, mh) for mh in m]

    def sum_sweep(m):
        acc_scr[...] = jnp.zeros_like(acc_scr)

        def sum_group(width):
            def body(g, l):
                for u in range(width):
                    s_scr[u] = jnp.dot(k_ref[0, pl.ds(chunk_start(width * g + u), KC), :], qs,
                                       preferred_element_type=F32)
                for u in range(width):
                    start = chunk_start(width * g + u)
                    sel = e_scr[pl.ds(start, KC), :] > thr
                    vT_chunk = vT_ref[0, :, pl.ds(start, KC)]
                    l = tuple(_accumulate(h, jnp.exp2(s_scr[u, :, h * TQ:(h + 1) * TQ] + jnp.where(sel, -m[h], NEG_INF)),
                                          vT_chunk, l[h], acc_scr, DH) for h in range(H))
                return l
            return body
        return _chunk_loops(nch, ATT_STAGES, sum_group, (jnp.zeros((1, TQ), F32),) * H)

    chunk_n = lax.broadcasted_iota(I32, kn_scr.shape, 0)
    k2 = jnp.max(jnp.where(chunk_n < nch, kn_scr[...], 0.0), axis=0, keepdims=True)
    l = sum_sweep(_norm_bound(qT.astype(F32), k2, H, DH, TQ))
    _finish_attention(l, acc_scr, y_ref, H, DH)

    @pl.when(_underflowed(l))
    def _():
        _finish_attention(sum_sweep(max_sweep()), acc_scr, y_ref, H, DH)


def _dsa(qT, k, vT, iqT, ik, smT, B, S):
    TQ = min(ATT_TQ, S)
    topk = min(DSA_TOPK_MAX, S // 4)
    assert S % TQ == 0 and topk <= TQ
    spow = 1 << max((S - 1).bit_length(), 1)
    k = k.reshape(B, S, DSA_W)
    ik = ik.reshape(B, S, LANES)
    qblk = lambda n: pl.BlockSpec((1, n, TQ), lambda b, i: (b, 0, i))
    return pl.pallas_call(
        functools.partial(_dsa_kernel, TQ=TQ, topk=topk, spow=spow, n_chunks=S // TQ),
        grid=(B, S // TQ),
        in_specs=[qblk(DSA_W),
                  pl.BlockSpec((1, S, DSA_W), lambda b, i: (b, 0, 0)),
                  pl.BlockSpec((1, DSA_W, S), lambda b, i: (b, 0, 0)),
                  qblk(IDX_HEADS * IDX_DH),
                  pl.BlockSpec((1, S, LANES), lambda b, i: (b, 0, 0)),
                  qblk(16)],
        out_specs=pl.BlockSpec((1, TQ, DSA_W), lambda b, i: (b, i, 0)),
        out_shape=jax.ShapeDtypeStruct((B, S, DSA_W), BF16),
        scratch_shapes=[pltpu.VMEM((S, TQ), F32),
                        pltpu.VMEM((TQ, TQ), F32),
                        pltpu.VMEM((DSA_W, TQ), F32),
                        pltpu.VMEM((SUBLANES, TQ), F32),
                        pltpu.VMEM((-(-(S // TQ) // SUBLANES) * SUBLANES, DSA_HEADS * TQ), F32),
                        pltpu.VMEM((ATT_STAGE, TQ, DSA_HEADS * TQ), F32)],
        compiler_params=_params("parallel", "arbitrary"),
    )(qT, k, vT, iqT, ik, smT)


def _out_mlp_kernel(yml_ref, ymb_ref, yds_ref, x_ref, wout_ref, gpost_ref, gpre_ref,
                    w1_ref, w2_ref, g2_ref, out_ref, h_scr, acc_scr):
    f = pl.program_id(1)

    @pl.when(f == 0)
    def _():
        slab = x_ref.shape[0] // MLP_SLABS
        for r in range(MLP_SLABS):
            rows = slice(r * slab, (r + 1) * slab)
            mix = jnp.dot(yml_ref[rows, :], wout_ref[0:ML_V_W, :], preferred_element_type=F32)
            mix = mix + jnp.dot(ymb_ref[rows, :], wout_ref[ML_V_W:ML_V_W + MOBA_W, :], preferred_element_type=F32)
            mix = mix + jnp.dot(yds_ref[rows, :], wout_ref[ML_V_W + MOBA_W:, :], preferred_element_type=F32)
            x1 = x_ref[rows, :] + _rms(mix, gpost_ref[...])
            out_ref[rows, :] = x1
            h_scr[rows, :] = _rms(x1, gpre_ref[...]).astype(BF16)
        acc_scr[...] = jnp.zeros_like(acc_scr)

    u = jnp.maximum(jnp.dot(h_scr[...], w1_ref[...], preferred_element_type=F32), 0.0)
    acc_scr[...] += jnp.dot((u * u).astype(BF16), w2_ref[...], preferred_element_type=F32)

    @pl.when(f == pl.num_programs(1) - 1)
    def _():
        out_ref[...] = out_ref[...] + _rms(acc_scr[...], g2_ref[...])


def _out_mlp(yml, ymb, yds, x2, w_out, g_post, g_pre, w1, w2, g2):
    T, D = x2.shape
    tm = min(MLP_TM, T)
    tf = min(MLP_TF, D_FF)
    row = lambda n: pl.BlockSpec((tm, n), lambda i, f: (i, 0))
    full = lambda a: pl.BlockSpec(a.shape, lambda i, f: (0,) * a.ndim)
    return pl.pallas_call(
        _out_mlp_kernel,
        grid=(T // tm, D_FF // tf),
        in_specs=[row(ML_V_W), row(MOBA_W), row(DSA_W), row(D), full(w_out), full(g_post), full(g_pre),
                  pl.BlockSpec((D, tf), lambda i, f: (0, f)),
                  pl.BlockSpec((tf, D), lambda i, f: (f, 0)),
                  full(g2)],
        out_specs=row(D),
        out_shape=jax.ShapeDtypeStruct((T, D), F32),
        scratch_shapes=[pltpu.VMEM((tm, D), BF16), pltpu.VMEM((tm, D), F32)],
        compiler_params=_params("parallel", "arbitrary"),
    )(yml, ymb, yds, x2, w_out, g_post, g_pre, w1, w2, g2)


def kernel(x, norm_mix_pre, w_in, ml_conv, ml_i_bias, ml_f_bias, ml_head_norm, w_out, norm_mix_post,
           norm_mlp_pre, w_ff1, w_ff2, norm_mlp_post):
    B, S, D = x.shape
    depth = w_in.shape[0]
    x2 = x.reshape(B * S, D)
    gain = lambda g: g.reshape(1, D).astype(F32)
    for l in range(depth):
        wn, wt = _prep_in_weights(w_in[l])
        (qk, mbk, dsk, ixk, mbqT, mbvT, dsqT, dsvT, ixqT, mlvT, mloT, smT) = _in_proj(
            x2, gain(norm_mix_pre[l]), wn, wt, B, S)
        y_ml = _mlstm(qk, mlvT, mloT, smT, ml_conv[l], ml_i_bias[l], ml_f_bias[l], ml_head_norm[l], B, S)
        y_mb = _moba(mbqT, mbk, mbvT, B, S)
        y_ds = _dsa(dsqT, dsk, dsvT, ixqT, ixk, smT, B, S)
        x2 = _out_mlp(y_ml.reshape(B * S, ML_V_W), y_mb.reshape(B * S, MOBA_W), y_ds.reshape(B * S, DSA_W),
                      x2, w_out[l].astype(BF16), gain(norm_mix_post[l]), gain(norm_mlp_pre[l]),
                      w_ff1[l].astype(BF16), w_ff2[l].astype(BF16), gain(norm_mlp_post[l]))
    return x2.reshape(B, S, D)
```

```python
import functools

import jax
import jax.numpy as jnp
from jax import lax
from jax.experimental import pallas as pl
from jax.experimental.pallas import tpu as pltpu

F32 = jnp.float32
BF16 = jnp.bfloat16
I32 = jnp.int32

D_MODEL = 1024
ML_HEADS, ML_DQK, ML_DV, ML_CONV = 4, 64, 128, 4
MOBA_HEADS, MOBA_DH, MOBA_BLOCK, MOBA_TOPK = 4, 64, 256, 3
DSA_HEADS, DSA_DH, IDX_HEADS, IDX_DH, DSA_TOPK_MAX = 4, 64, 4, 64, 256
D_FF = 4 * D_MODEL
EPS = 1e-6

ML_QK_W = ML_HEADS * ML_DQK
ML_V_W = ML_HEADS * ML_DV
MOBA_W = MOBA_HEADS * MOBA_DH
DSA_W = DSA_HEADS * DSA_DH
IN_SPLITS = (ML_QK_W, ML_QK_W, ML_V_W, ML_V_W, ML_HEADS, ML_HEADS,
             MOBA_W, MOBA_W, MOBA_W,
             DSA_W, DSA_W, DSA_W, IDX_HEADS * IDX_DH, IDX_DH, IDX_HEADS)

LANES = 128
SUBLANES = 8
VMEM_LIMIT = 52 * 1024 * 1024

ML_CHUNK = 512
ATT_TQ = 256
ATT_STAGES = (8, 4, 2, 1)
ATT_STAGE = ATT_STAGES[0]
PROJ_TM = 1024
MLP_TM = 1024
MLP_TF = 2048
MLP_SLABS = 4

NEG_INF = float("-inf")
BISECT_CAP = 300
BISECT_WARMUP = 13
BISECT_DROP = 2
BISECT_STEPS_PER_CHECK = 2
SOFTMAX_L_MIN = 1e-30
assert MOBA_DH == DSA_DH
ATT_QSCALE = MOBA_DH ** -0.5 * 1.4426950408889634

_NT = (((1,), (1,)), ((), ()))


def _chunk_loops(n, widths, make_body, carry, regroup=lambda c, w: c):
    covered = 0
    for i, w in enumerate(widths):
        if i:
            carry = regroup(carry, w)
        carry = lax.fori_loop(covered // w, n // w, make_body(w), carry)
        covered = w * (n // w)
    return carry


def _rms(x, g):
    return x * lax.rsqrt(jnp.mean(x * x, axis=-1, keepdims=True) + EPS) * g


def _log_sigmoid(x):
    return jnp.minimum(x, 0.0) - jnp.log1p(jnp.exp(-jnp.abs(x)))


def _params(*sem):
    return pltpu.CompilerParams(dimension_semantics=sem, vmem_limit_bytes=VMEM_LIMIT)


_N_QK, _N_MBK, _N_DSK, _N_IXK = 0, 512, 768, 1024
_N_TOTAL = 1152
_T_MBQ, _T_MBV, _T_DSQ, _T_DSV, _T_IXQ, _T_MLV, _T_MLO, _T_SM = 0, 256, 512, 768, 1024, 1280, 1792, 2304
_T_TOTAL = 2320


def _in_proj_kernel(x_ref, g_ref, wn_ref, wt_ref,
                    qk_ref, mbk_ref, dsk_ref, ixk_ref,
                    mbqT_ref, mbvT_ref, dsqT_ref, dsvT_ref, ixqT_ref, mlvT_ref, mloT_ref, smT_ref):
    h = _rms(x_ref[...], g_ref[...]).astype(BF16)

    def mm(a, n):
        return jnp.dot(h, wn_ref[:, a:a + n], preferred_element_type=F32)

    def mt(a, n):
        return lax.dot_general(wt_ref[a:a + n, :], h, _NT, preferred_element_type=F32)

    qk_ref[...] = mm(_N_QK, 512)
    mbk_ref[...] = mm(_N_MBK, 256).astype(BF16)
    dsk_ref[...] = mm(_N_DSK, 256).astype(BF16)
    ixk_ref[...] = mm(_N_IXK, 128).astype(BF16)
    mbqT_ref[0] = (mt(_T_MBQ, 256) * ATT_QSCALE).astype(BF16)
    mbvT_ref[0] = mt(_T_MBV, 256).astype(BF16)
    dsqT_ref[0] = (mt(_T_DSQ, 256) * ATT_QSCALE).astype(BF16)
    dsvT_ref[0] = mt(_T_DSV, 256).astype(BF16)
    ixqT_ref[0] = mt(_T_IXQ, 256).astype(BF16)
    mlvT_ref[0] = mt(_T_MLV, 512).astype(BF16)
    mloT_ref[0] = mt(_T_MLO, 512)
    smT_ref[0] = mt(_T_SM, 16)


def _prep_in_weights(w_in):
    pts, acc = [], 0
    for n in IN_SPLITS:
        pts.append((acc, acc + n))
        acc += n
    col = lambda i: w_in[:, pts[i][0]:pts[i][1]]
    (ml_q, ml_k, ml_v, ml_o, ml_i, ml_f, mb_q, mb_k, mb_v,
     ds_q, ds_k, ds_v, ix_q, ix_k, ix_w) = [col(i) for i in range(len(IN_SPLITS))]
    d = w_in.shape[0]
    small = jnp.concatenate([ml_i, ml_f, ix_w], axis=1)
    wn = jnp.concatenate([
        ml_q, ml_k, mb_k, ds_k,
        ix_k, jnp.zeros((d, 128 - IDX_DH), w_in.dtype)], axis=1)
    wt = jnp.concatenate([
        mb_q, mb_v, ds_q, ds_v, ix_q, ml_v, ml_o,
        small, jnp.zeros((d, 16 - small.shape[1]), w_in.dtype)], axis=1).T
    assert wn.shape == (d, _N_TOTAL) and wt.shape == (_T_TOTAL, d)
    return wn.astype(BF16), wt.astype(BF16)


def _in_proj(x2, g, wn, wt, B, S):
    T, D = x2.shape
    tm = min(PROJ_TM, S)
    nsb = S // tm
    row = lambda n: pl.BlockSpec((tm, n), lambda i: (i, 0))
    tr = lambda n: pl.BlockSpec((1, n, tm), lambda i: (i // nsb, 0, i % nsb))
    full = lambda a: pl.BlockSpec(a.shape, lambda i: (0,) * a.ndim)
    out_shape = (
        jax.ShapeDtypeStruct((T, 512), F32),
        jax.ShapeDtypeStruct((T, 256), BF16),
        jax.ShapeDtypeStruct((T, 256), BF16),
        jax.ShapeDtypeStruct((T, 128), BF16),
        jax.ShapeDtypeStruct((B, 256, S), BF16),
        jax.ShapeDtypeStruct((B, 256, S), BF16),
        jax.ShapeDtypeStruct((B, 256, S), BF16),
        jax.ShapeDtypeStruct((B, 256, S), BF16),
        jax.ShapeDtypeStruct((B, 256, S), BF16),
        jax.ShapeDtypeStruct((B, 512, S), BF16),
        jax.ShapeDtypeStruct((B, 512, S), F32),
        jax.ShapeDtypeStruct((B, 16, S), F32),
    )
    out_specs = (row(512), row(256), row(256), row(128),
                 tr(256), tr(256), tr(256), tr(256), tr(256), tr(512), tr(512), tr(16))
    return pl.pallas_call(
        _in_proj_kernel,
        grid=(T // tm,),
        in_specs=[row(D), full(g), full(wn), full(wt)],
        out_specs=out_specs,
        out_shape=out_shape,
        compiler_params=_params("parallel"),
    )(x2, g, wn, wt)


def _dot3(a_f32, b_exact):
    b = b_exact.astype(BF16)
    hi = a_f32.astype(BF16)
    r1 = a_f32 - hi.astype(F32)
    mid = r1.astype(BF16)
    lo = (r1 - mid.astype(F32)).astype(BF16)
    return (jnp.dot(hi, b, preferred_element_type=F32) + jnp.dot(mid, b, preferred_element_type=F32)
            + jnp.dot(lo, b, preferred_element_type=F32))


def _mlstm_kernel(qk_ref, vT_ref, oT_ref, smT_ref, conv_ref, bcol_ref, hn_ref,
                  y_ref, xp_scr, st_scr, m_scr, *, L):
    c = pl.program_id(1)
    halo = SUBLANES
    DK, DV = ML_DQK, ML_DV

    @pl.when(c == 0)
    def _():
        xp_scr[0:halo, :] = jnp.zeros((halo, 2 * ML_QK_W), F32)
        st_scr[...] = jnp.zeros_like(st_scr)
        m_scr[...] = jnp.zeros_like(m_scr)

    cur = qk_ref[0]
    xp_scr[halo:halo + L, :] = cur
    base = halo - (ML_CONV - 1)
    acc = conv_ref[0:1, :] * xp_scr[base:base + L, :]
    for j in range(1, ML_CONV):
        acc = acc + conv_ref[j:j + 1, :] * xp_scr[base + j:base + j + L, :]
    xp_scr[0:halo, :] = cur[L - halo:L, :]
    qk = acc * jax.nn.sigmoid(acc)
    qT_all = (qk[:, :ML_QK_W] * (DK ** -0.5)).T.astype(BF16)
    k_all = qk[:, ML_QK_W:].astype(BF16)

    grow = smT_ref[0][0:SUBLANES, :] + bcol_ref[...]
    si = lax.broadcasted_iota(I32, (L, L), 0)
    ji = lax.broadcasted_iota(I32, (L, L), 1)
    causal = si <= ji
    b_row = _dot3(_log_sigmoid(grow), causal)

    for h in range(ML_HEADS):
        qT = qT_all[h * DK:(h + 1) * DK, :]
        kh = k_all[:, h * DK:(h + 1) * DK]
        vT = vT_ref[0, h * DV:(h + 1) * DV, :]
        b_j = b_row[ML_HEADS + h:ML_HEADS + h + 1, :]
        c_row = grow[h:h + 1, :] - b_j
        c_col = jnp.broadcast_to(c_row, (LANES, L)).T
        dlog = jnp.where(causal, b_j + jnp.concatenate([c_col] * (L // LANES), axis=1), NEG_INF)
        m0 = m_scr[h:h + 1, 0:1]
        inter = b_j + m0
        m_t = jnp.maximum(inter, jnp.max(dlog, axis=0, keepdims=True))
        w_inter = jnp.exp(inter - m_t)
        w_intra = jnp.exp(dlog - m_t) * jnp.dot(kh, qT, preferred_element_type=F32)
        st = st_scr[h]
        cq = jnp.dot(st.astype(BF16), qT, preferred_element_type=F32)
        num = w_inter * cq[:DV] + jnp.dot(vT, w_intra.astype(BF16), preferred_element_type=F32)
        den = w_inter * cq[DV:DV + 1] + jnp.sum(w_intra, axis=0, keepdims=True)
        hh = num / jnp.maximum(jnp.abs(den), jnp.exp(-m_t))
        hh = hh * lax.rsqrt(jnp.mean(hh * hh, axis=0, keepdims=True) + EPS)
        rows = slice(h * DV, (h + 1) * DV)
        y_ref[0, :, rows] = (jax.nn.sigmoid(oT_ref[0, rows, :]) * hh * hn_ref[rows, :]).T.astype(y_ref.dtype)

        b_last = b_j[:, L - 1:L]
        a = b_last + c_row
        m_loc = jnp.max(a, axis=1, keepdims=True)
        wa = jnp.exp(a - m_loc)
        lhs = jnp.concatenate([(vT.astype(F32) * wa).astype(BF16),
                               jnp.broadcast_to(wa, (SUBLANES, L)).astype(BF16)], axis=0)
        c_loc = jnp.dot(lhs, kh, preferred_element_type=F32)
        m_new = jnp.maximum(b_last + m0, m_loc)
        s_old = jnp.exp(b_last + m0 - m_new)
        s_loc = jnp.exp(m_loc - m_new)
        st_scr[h] = s_old * st + s_loc * c_loc
        m_scr[h:h + 1, :] = jnp.broadcast_to(m_new, (1, LANES))


def _mlstm(qk, vT, oT, smT, conv_w, i_bias, f_bias, head_norm, B, S):
    L = min(ML_CHUNK, S)
    assert L % LANES == 0
    nc = S // L
    qk = qk.reshape(B, S, 2 * ML_QK_W)
    bcol = jnp.concatenate([i_bias, f_bias]).astype(F32).reshape(2 * ML_HEADS, 1)
    hn = jnp.broadcast_to(head_norm.astype(F32)[:, None], (ML_V_W, L))
    blkT = lambda n: pl.BlockSpec((1, n, L), lambda b, c: (b, 0, c))
    full = lambda a: pl.BlockSpec(a.shape, lambda b, c: (0,) * a.ndim)
    return pl.pallas_call(
        functools.partial(_mlstm_kernel, L=L),
        grid=(B, nc),
        in_specs=[pl.BlockSpec((1, L, 2 * ML_QK_W), lambda b, c: (b, c, 0)),
                  blkT(ML_V_W), blkT(ML_V_W), blkT(16), full(conv_w), full(bcol), full(hn)],
        out_specs=pl.BlockSpec((1, L, ML_V_W), lambda b, c: (b, c, 0)),
        out_shape=jax.ShapeDtypeStruct((B, S, ML_V_W), BF16),
        scratch_shapes=[pltpu.VMEM((L + SUBLANES, 2 * ML_QK_W), F32),
                        pltpu.VMEM((ML_HEADS, ML_DV + SUBLANES, ML_DQK), F32),
                        pltpu.VMEM((SUBLANES, LANES), F32)],
        compiler_params=_params("parallel", "arbitrary"),
    )(qk, vT, oT, smT, conv_w.astype(F32), bcol, hn)


def _block_diag_T(qT, heads, dh):
    rid = lax.broadcasted_iota(I32, qT.shape, 0)
    zero = jnp.zeros_like(qT)
    return jnp.concatenate(
        [jnp.where((rid >= h * dh) & (rid < (h + 1) * dh), qT, zero) for h in range(heads)], axis=1)


def _accumulate(h, p, vT_chunk, l, acc_scr, dh):
    rows = slice(h * dh, (h + 1) * dh)
    acc_scr[rows, :] += jnp.dot(vT_chunk[rows, :], p.astype(BF16), preferred_element_type=F32)
    return l + jnp.sum(p, axis=0, keepdims=True)


def _finish_attention(l, acc_scr, y_ref, heads, dh):
    outs = [acc_scr[h * dh:(h + 1) * dh, :] / l[h] for h in range(heads)]
    y_ref[0] = jnp.concatenate(outs, axis=0).T.astype(y_ref.dtype)


def _max_key_norm2(kb, heads, dh, TQ):
    ch = lax.broadcasted_iota(I32, (heads * dh, LANES), 0) // dh
    hd = lax.broadcasted_iota(I32, (heads * dh, LANES), 1)
    n2 = jnp.dot((kb * kb).astype(BF16), (ch == hd).astype(BF16), preferred_element_type=F32)
    n2 = jnp.max(n2, axis=0, keepdims=True) * (1.0 + 2.0 ** -7)
    return jnp.concatenate([jnp.broadcast_to(n2[:, h:h + 1], (1, TQ)) for h in range(heads)], axis=1)


def _norm_bound(qTf, k2, heads, dh, TQ):
    out = []
    for h in range(heads):
        qh = qTf[h * dh:(h + 1) * dh, :]
        q2 = jnp.sum(qh * qh, axis=0, keepdims=True)
        out.append(jnp.sqrt(q2 * k2[:, h * TQ:(h + 1) * TQ]) * (1.0 + 2.0 ** -8))
    return out


def _underflowed(l):
    small = l[0]
    for lh in l[1:]:
        small = jnp.minimum(small, lh)
    return jnp.logical_not(jnp.min(small) >= SOFTMAX_L_MIN)


def _moba_kernel(qT_ref, k_ref, vT_ref, y_ref, km_scr, kn_scr, acc_scr, s_scr, *, NB, NBP, n_sel):
    H, DH, BS = MOBA_HEADS, MOBA_DH, MOBA_BLOCK
    TQ = BS
    qi = pl.program_id(1)

    @pl.when(qi == 0)
    def _():
        km_scr[...] = jnp.zeros_like(km_scr)
        kn_scr[...] = jnp.zeros_like(kn_scr)

        def body(j, carry):
            kb = k_ref[0, pl.ds(pl.multiple_of(j * BS, BS), BS), :].astype(F32)
            km_scr[pl.ds(j, 1), :] = jnp.mean(kb, axis=0, keepdims=True)
            kn_scr[pl.ds(j, 1), :] = _max_key_norm2(kb, H, DH, TQ)
            return carry
        lax.fori_loop(0, NB, body, 0)

    qT = qT_ref[0]
    qTf = qT.astype(F32)
    km = km_scr[...]
    gate = jnp.concatenate([_dot3(km[:, h * DH:(h + 1) * DH], qT[h * DH:(h + 1) * DH, :]) for h in range(H)],
                           axis=1) * (1.0 / ATT_QSCALE)
    blk = lax.broadcasted_iota(I32, gate.shape, 0)
    g = jnp.where(blk < qi, gate, NEG_INF)
    sels = []
    for _ in range(n_sel):
        mx = jnp.max(g, axis=0, keepdims=True)
        isel = jnp.min(jnp.where(g == mx, blk, NBP), axis=0, keepdims=True)
        sels.append(jnp.where(mx > NEG_INF, isel, -1))
        g = jnp.where(blk == isel, NEG_INF, g)

    qs = _block_diag_T(qT, H, DH)

    def hit_row(j):
        hit = sels[0] == j
        for s in sels[1:]:
            hit = hit | (s == j)
        return hit

    blk_start = lambda j: pl.multiple_of(j * BS, BS)
    blk_scores = lambda j: jnp.dot(k_ref[0, pl.ds(blk_start(j), BS), :], qs, preferred_element_type=F32)
    hcols = lambda row, h: row[:, h * TQ:(h + 1) * TQ]
    causal = lax.broadcasted_iota(I32, (BS, TQ), 0) <= lax.broadcasted_iota(I32, (BS, TQ), 1)

    def diag_scores():
        s_all = blk_scores(qi)
        return [jnp.where(causal, hcols(s_all, h), NEG_INF) for h in range(H)]

    def max_sweep():
        def max_block(j, m):
            return jnp.maximum(m, jnp.where(hit_row(j), jnp.max(blk_scores(j), axis=0, keepdims=True), NEG_INF))
        m_all = lax.fori_loop(0, qi, max_block, jnp.full((1, H * TQ), NEG_INF, F32))
        s_diag = diag_scores()
        return [jnp.maximum(hcols(m_all, h), jnp.max(s_diag[h], axis=0, keepdims=True)) for h in range(H)]

    def sum_sweep(m):
        acc_scr[...] = jnp.zeros_like(acc_scr)

        def sum_group(width):
            def body(g, l):
                for u in range(width):
                    s_scr[u] = blk_scores(width * g + u)
                for u in range(width):
                    j = width * g + u
                    hit = hit_row(j)
                    vT_blk = vT_ref[0, :, pl.ds(blk_start(j), BS)]
                    l = tuple(_accumulate(h, jnp.exp2(s_scr[u, :, h * TQ:(h + 1) * TQ]
                                                      - jnp.where(hcols(hit, h), m[h], jnp.inf)),
                                          vT_blk, l[h], acc_scr, DH) for h in range(H))
                return l
            return body
        l = _chunk_loops(qi, ATT_STAGES, sum_group, (jnp.zeros((1, TQ), F32),) * H)
        s_diag = diag_scores()
        vT_diag = vT_ref[0, :, pl.ds(blk_start(qi), BS)]
        return [_accumulate(h, jnp.exp2(s_diag[h] - m[h]), vT_diag, l[h], acc_scr, DH) for h in range(H)]

    blk_n = lax.broadcasted_iota(I32, kn_scr.shape, 0)
    k2 = jnp.max(jnp.where(blk_n <= qi, kn_scr[...], 0.0), axis=0, keepdims=True)
    l = sum_sweep(_norm_bound(qTf, k2, H, DH, TQ))
    _finish_attention(l, acc_scr, y_ref, H, DH)

    @pl.when(_underflowed(l))
    def _():
        _finish_attention(sum_sweep(max_sweep()), acc_scr, y_ref, H, DH)


def _moba(qT, k, vT, B, S):
    BS = MOBA_BLOCK
    assert S % BS == 0
    NB = S // BS
    NBP = -(-NB // SUBLANES) * SUBLANES
    n_sel = max(min(MOBA_TOPK, NB - 1), 1)
    k = k.reshape(B, S, MOBA_W)
    return pl.pallas_call(
        functools.partial(_moba_kernel, NB=NB, NBP=NBP, n_sel=n_sel),
        grid=(B, NB),
        in_specs=[pl.BlockSpec((1, MOBA_W, BS), lambda b, i: (b, 0, i)),
                  pl.BlockSpec((1, S, MOBA_W), lambda b, i: (b, 0, 0)),
                  pl.BlockSpec((1, MOBA_W, S), lambda b, i: (b, 0, 0))],
        out_specs=pl.BlockSpec((1, BS, MOBA_W), lambda b, i: (b, i, 0)),
        out_shape=jax.ShapeDtypeStruct((B, S, MOBA_W), BF16),
        scratch_shapes=[pltpu.VMEM((NBP, MOBA_W), F32),
                        pltpu.VMEM((NBP, MOBA_HEADS * BS), F32),
                        pltpu.VMEM((MOBA_W, BS), F32),
                        pltpu.VMEM((ATT_STAGE, BS, MOBA_HEADS * BS), F32)],
        compiler_params=_params("parallel", "arbitrary"),
    )(qT, k, vT)


def _dsa_kernel(qT_ref, k_ref, vT_ref, iqT_ref, ik_ref, smT_ref, y_ref,
                e_scr, gmax_scr, acc_scr, thr_scr, kn_scr, s_scr, *, TQ, topk, spow, n_chunks):
    H, DH = DSA_HEADS, DSA_DH
    KC = TQ
    qi = pl.program_id(1)
    nch = qi + 1
    chunk_start = lambda c: pl.multiple_of(c * KC, KC)

    @pl.when(qi == 0)
    def _():
        kn_scr[...] = jnp.zeros_like(kn_scr)

        def body(c, carry):
            kn_scr[pl.ds(c, 1), :] = _max_key_norm2(k_ref[0, pl.ds(chunk_start(c), KC), :].astype(F32), H, DH, TQ)
            return carry
        lax.fori_loop(0, n_chunks, body, 0)
    sum8 = lambda b: jnp.sum(b.astype(I32).reshape(KC // SUBLANES, SUBLANES, TQ), axis=0)
    any_row = lambda r: jnp.max(r.astype(I32)) > 0

    iqT = iqT_ref[0]
    iq_cat = jnp.concatenate([iqT[h * IDX_DH:(h + 1) * IDX_DH, :] for h in range(IDX_HEADS)], axis=1)
    iw = smT_ref[0][2 * ML_HEADS:2 * ML_HEADS + IDX_HEADS, :] * (IDX_HEADS ** -0.5 * IDX_DH ** -0.5)
    t_pos = qi * TQ + lax.broadcasted_iota(I32, (KC, TQ), 1)
    s_off = lax.broadcasted_iota(I32, (KC, TQ), 0)
    gmax_scr[...] = jnp.full((KC, TQ), NEG_INF, F32)

    def score_group(width, masked=False):
        def body(g, tally):
            for u in range(width):
                ikc = ik_ref[0, pl.ds(chunk_start(width * g + u), KC), :][:, :IDX_DH]
                s_scr[u] = jnp.dot(ikc, iq_cat, preferred_element_type=F32)
            for u in range(width):
                c = width * g + u
                sc = iw[0:1, :] * jnp.maximum(s_scr[u, :, 0:TQ], 0.0)
                for h in range(1, IDX_HEADS):
                    sc = sc + iw[h:h + 1, :] * jnp.maximum(s_scr[u, :, h * TQ:(h + 1) * TQ], 0.0)
                if masked:
                    sc = jnp.where(c * KC + s_off <= t_pos, sc, NEG_INF)
                e_scr[pl.ds(chunk_start(c), KC), :] = sc
                gmax_scr[...] = jnp.maximum(gmax_scr[...], sc)
                tally = tally + jnp.sum(jnp.where(sc > 0.0, 1, jnp.where(sc == 0.0, 1 << 16, 0))
                                        .reshape(KC // SUBLANES, SUBLANES, TQ), axis=0)
            return tally
        return body
    assert n_chunks * KC < (1 << 16)
    zero8 = jnp.zeros((SUBLANES, TQ), I32)
    tally = _chunk_loops(qi, ATT_STAGES, score_group, zero8)
    tally = jnp.sum(score_group(1, True)(qi, tally), axis=0, keepdims=True)
    n_pos = tally & 0xFFFF
    n_nonneg = n_pos + (tally >> 16)

    def sweep_keys(fold, merge, init):
        def group(width):
            def body(g, acc):
                return tuple(merge(acc[u], fold(e_scr[pl.ds(chunk_start(width * g + u), KC), :]))
                             for u in range(width))
            return body

        def regroup(acc, width):
            half = len(acc) // 2
            acc = tuple(merge(acc[u], acc[u + half]) for u in range(half))
            return acc if half == width else regroup(acc, width)
        return _chunk_loops(nch, ATT_STAGES, group, (init,) * ATT_STAGES[0], regroup)[0]

    def count_gt(cand):
        return jnp.sum(sweep_keys(lambda x: sum8(x > cand), lambda p, q: p + q, zero8), axis=0, keepdims=True)

    def min_gt(cand):
        fold = lambda x: jnp.min(jnp.where(x > cand, x, jnp.inf).reshape(KC // SUBLANES, SUBLANES, TQ), axis=0)
        return jnp.min(sweep_keys(fold, jnp.minimum, jnp.full((SUBLANES, TQ), jnp.inf, F32)), axis=0, keepdims=True)

    def rewrite(rows, thr):
        def body(c, carry):
            x = e_scr[pl.ds(chunk_start(c), KC), :]
            rank = (2 * spow - (c * KC + s_off)).astype(F32)
            new = jnp.where(x > thr, jnp.inf, jnp.where(x == thr, rank, NEG_INF))
            e_scr[pl.ds(chunk_start(c), KC), :] = jnp.where(rows, new, x)
            return carry
        lax.fori_loop(0, nch, body, 0)

    def halve(_, c):
        lo, hi, act, tie, n_lo = c
        mid = lo + (hi - lo) * 0.5
        inside = (mid > lo) & (mid < hi)
        cnt = count_gt(mid)
        live = act > 0.0
        run = live & inside
        up = run & (cnt >= topk)
        lo = jnp.where(up, mid, lo)
        n_lo = jnp.where(up, cnt, n_lo)
        hi = jnp.where(run & (cnt < topk), mid, hi)
        collapsed = live & jnp.logical_not(inside)
        tie = jnp.where(collapsed, 1.0, tie)
        act = jnp.where(collapsed | (run & (cnt == topk)), 0.0, act)
        return lo, hi, act, tie, n_lo

    def excess(state):
        return jnp.max(jnp.where(state[2] > 0.0, state[4] - topk, 0).astype(F32))

    def bisect(lo, hi, act, warmup):
        state = (lo, hi, act, jnp.zeros_like(act), jnp.full((1, TQ), 1 << 20, I32))
        state = lax.fori_loop(0, warmup, halve, state)

        near = (jnp.int32(warmup), state)
        lo, hi, act, tie, n_lo = state
        cur, left = lo, jnp.where(act > 0.0, n_lo - topk, 0)
        for _ in range(BISECT_DROP):
            cur = jnp.where(left > 0, min_gt(cur), cur)
            left = left - 1
        done = (act > 0.0) & (count_gt(cur) == topk)
        lo = jnp.where(done, cur, lo)
        act = jnp.where(done, 0.0, act)

        def body(c):
            state = c[1]
            for _ in range(BISECT_STEPS_PER_CHECK):
                state = halve(0, state)
            return c[0] + BISECT_STEPS_PER_CHECK, state, jnp.max(state[2])
        pending = jnp.max(jnp.maximum(act, tie))
        out = lax.while_loop(lambda c: jnp.logical_and(c[0] < BISECT_CAP, c[2] > 0.0), body,
                             (near[0], (lo, hi, act, tie, n_lo), pending))
        return out[1][0], out[1][1], out[1][3], pending

    fmax = float(jnp.finfo(F32).max)
    gm = gmax_scr[...]
    g_lo = jnp.min(gm, axis=0, keepdims=True)
    g_hi = jnp.max(gm, axis=0, keepdims=True)
    n_vis = qi * TQ + lax.broadcasted_iota(I32, (1, TQ), 1) + 1
    few = n_vis <= topk
    zero_tie = (n_pos < topk) & (n_nonneg >= topk) & jnp.logical_not(few)

    @pl.when(any_row(zero_tie))
    def _():
        rewrite(zero_tie, jnp.zeros((1, TQ), F32))

    rank_lo, rank_hi = float(spow), float(2 * spow + 1)
    below = jnp.maximum(g_lo, -fmax)
    below = below - jnp.abs(below) * (2.0 ** -10) - 1e-30
    lo0 = jnp.where(zero_tie, rank_lo, jnp.maximum(below, -fmax))
    hi0 = jnp.where(zero_tie, rank_hi, g_hi)
    lo, hi, tie, pending = bisect(lo0, hi0, jnp.where(few, 0.0, 1.0), BISECT_WARMUP)
    thr_scr[0:1, :] = jnp.where(few, NEG_INF, lo)

    @pl.when(pending > 0.0)
    def _():
        @pl.when(jnp.max(tie) > 0.0)
        def _():
            rows = tie > 0.0
            rewrite(rows, hi)
            lo2, _, _, _ = bisect(jnp.full((1, TQ), rank_lo, F32), jnp.full((1, TQ), rank_hi, F32), tie, 0)
            thr_scr[0:1, :] = jnp.where(rows, lo2, thr_scr[0:1, :])

    thr = thr_scr[0:1, :]
    qT = qT_ref[0]
    qs = _block_diag_T(qT, H, DH)
    hcols = lambda row, h: row[:, h * TQ:(h + 1) * TQ]

    def chunk_scores(c):
        start = chunk_start(c)
        sel = e_scr[pl.ds(start, KC), :] > thr
        s_all = jnp.dot(k_ref[0, pl.ds(start, KC), :], qs, preferred_element_type=F32)
        return sel, s_all

    def max_sweep():
        def max_chunk(c, m):
            sel, s_all = chunk_scores(c)
            return tuple(jnp.maximum(m[h], jnp.max(jnp.where(sel, hcols(s_all, h), NEG_INF), axis=0, keepdims=True))
                         for h in range(H))
        m = lax.fori_loop(0, nch, max_chunk, (jnp.full((1, TQ), NEG_INF, F32),) * H)
        return [jnp.where(mh == NEG_INF, 0.0, mh) for mh in m]

    def sum_sweep(m):
        acc_scr[...] = jnp.zeros_like(acc_scr)

        def sum_group(width):
            def body(g, l):
                for u in range(width):
                    s_scr[u] = jnp.dot(k_ref[0, pl.ds(chunk_start(width * g + u), KC), :], qs,
                                       preferred_element_type=F32)
                for u in range(width):
                    start = chunk_start(width * g + u)
                    sel = e_scr[pl.ds(start, KC), :] > thr
                    vT_chunk = vT_ref[0, :, pl.ds(start, KC)]
                    l = tuple(_accumulate(h, jnp.exp2(s_scr[u, :, h * TQ:(h + 1) * TQ] + jnp.where(sel, -m[h], NEG_INF)),
                                          vT_chunk, l[h], acc_scr, DH) for h in range(H))
                return l
            return body
        return _chunk_loops(nch, ATT_STAGES, sum_group, (jnp.zeros((1, TQ), F32),) * H)

    chunk_n = lax.broadcasted_iota(I32, kn_scr.shape, 0)
    k2 = jnp.max(jnp.where(chunk_n < nch, kn_scr[...], 0.0), axis=0, keepdims=True)
    l = sum_sweep(_norm_bound(qT.astype(F32), k2, H, DH, TQ))
    _finish_attention(l, acc_scr, y_ref, H, DH)

    @pl.when(_underflowed(l))
    def _():
        _finish_attention(sum_sweep(max_sweep()), acc_scr, y_ref, H, DH)


def _dsa(qT, k, vT, iqT, ik, smT, B, S):
    TQ = min(ATT_TQ, S)
    topk = min(DSA_TOPK_MAX, S // 4)
    assert S % TQ == 0 and topk <= TQ
    spow = 1 << max((S - 1).bit_length(), 1)
    k = k.reshape(B, S, DSA_W)
    ik = ik.reshape(B, S, LANES)
    qblk = lambda n: pl.BlockSpec((1, n, TQ), lambda b, i: (b, 0, i))
    return pl.pallas_call(
        functools.partial(_dsa_kernel, TQ=TQ, topk=topk, spow=spow, n_chunks=S // TQ),
        grid=(B, S // TQ),
        in_specs=[qblk(DSA_W),
                  pl.BlockSpec((1, S, DSA_W), lambda b, i: (b, 0, 0)),
                  pl.BlockSpec((1, DSA_W, S), lambda b, i: (b, 0, 0)),
                  qblk(IDX_HEADS * IDX_DH),
                  pl.BlockSpec((1, S, LANES), lambda b, i: (b, 0, 0)),
                  qblk(16)],
        out_specs=pl.BlockSpec((1, TQ, DSA_W), lambda b, i: (b, i, 0)),
        out_shape=jax.ShapeDtypeStruct((B, S, DSA_W), BF16),
        scratch_shapes=[pltpu.VMEM((S, TQ), F32),
                        pltpu.VMEM((TQ, TQ), F32),
                        pltpu.VMEM((DSA_W, TQ), F32),
                        pltpu.VMEM((SUBLANES, TQ), F32),
                        pltpu.VMEM((-(-(S // TQ) // SUBLANES) * SUBLANES, DSA_HEADS * TQ), F32),
                        pltpu.VMEM((ATT_STAGE, TQ, DSA_HEADS * TQ), F32)],
        compiler_params=_params("parallel", "arbitrary"),
    )(qT, k, vT, iqT, ik, smT)


def _out_mlp_kernel(yml_ref, ymb_ref, yds_ref, x_ref, wout_ref, gpost_ref, gpre_ref,
                    w1_ref, w2_ref, g2_ref, out_ref, h_scr, acc_scr):
    f = pl.program_id(1)

    @pl.when(f == 0)
    def _():
        slab = x_ref.shape[0] // MLP_SLABS
        for r in range(MLP_SLABS):
            rows = slice(r * slab, (r + 1) * slab)
            mix = jnp.dot(yml_ref[rows, :], wout_ref[0:ML_V_W, :], preferred_element_type=F32)
            mix = mix + jnp.dot(ymb_ref[rows, :], wout_ref[ML_V_W:ML_V_W + MOBA_W, :], preferred_element_type=F32)
            mix = mix + jnp.dot(yds_ref[rows, :], wout_ref[ML_V_W + MOBA_W:, :], preferred_element_type=F32)
            x1 = x_ref[rows, :] + _rms(mix, gpost_ref[...])
            out_ref[rows, :] = x1
            h_scr[rows, :] = _rms(x1, gpre_ref[...]).astype(BF16)
        acc_scr[...] = jnp.zeros_like(acc_scr)

    u = jnp.maximum(jnp.dot(h_scr[...], w1_ref[...], preferred_element_type=F32), 0.0)
    acc_scr[...] += jnp.dot((u * u).astype(BF16), w2_ref[...], preferred_element_type=F32)

    @pl.when(f == pl.num_programs(1) - 1)
    def _():
        out_ref[...] = out_ref[...] + _rms(acc_scr[...], g2_ref[...])


def _out_mlp(yml, ymb, yds, x2, w_out, g_post, g_pre, w1, w2, g2):
    T, D = x2.shape
    tm = min(MLP_TM, T)
    tf = min(MLP_TF, D_FF)
    row = lambda n: pl.BlockSpec((tm, n), lambda i, f: (i, 0))
    full = lambda a: pl.BlockSpec(a.shape, lambda i, f: (0,) * a.ndim)
    return pl.pallas_call(
        _out_mlp_kernel,
        grid=(T // tm, D_FF // tf),
        in_specs=[row(ML_V_W), row(MOBA_W), row(DSA_W), row(D), full(w_out), full(g_post), full(g_pre),
                  pl.BlockSpec((D, tf), lambda i, f: (0, f)),
                  pl.BlockSpec((tf, D), lambda i, f: (f, 0)),
                  full(g2)],
        out_specs=row(D),
        out_shape=jax.ShapeDtypeStruct((T, D), F32),
        scratch_shapes=[pltpu.VMEM((tm, D), BF16), pltpu.VMEM((tm, D), F32)],
        compiler_params=_params("parallel", "arbitrary"),
    )(yml, ymb, yds, x2, w_out, g_post, g_pre, w1, w2, g2)


def kernel(x, norm_mix_pre, w_in, ml_conv, ml_i_bias, ml_f_bias, ml_head_norm, w_out, norm_mix_post,
           norm_mlp_pre, w_ff1, w_ff2, norm_mlp_post):
    B, S, D = x.shape
    depth = w_in.shape[0]
    x2 = x.reshape(B * S, D)
    gain = lambda g: g.reshape(1, D).astype(F32)
    for l in range(depth):
        wn, wt = _prep_in_weights(w_in[l])
        (qk, mbk, dsk, ixk, mbqT, mbvT, dsqT, dsvT, ixqT, mlvT, mloT, smT) = _in_proj(
            x2, gain(norm_mix_pre[l]), wn, wt, B, S)
        y_ml = _mlstm(qk, mlvT, mloT, smT, ml_conv[l], ml_i_bias[l], ml_f_bias[l], ml_head_norm[l], B, S)
        y_mb = _moba(mbqT, mbk, mbvT, B, S)
        y_ds = _dsa(dsqT, dsk, dsvT, ixqT, ixk, smT, B, S)
        x2 = _out_mlp(y_ml.reshape(B * S, ML_V_W), y_mb.reshape(B * S, MOBA_W), y_ds.reshape(B * S, DSA_W),
                      x2, w_out[l].astype(BF16), gain(norm_mix_post[l]), gain(norm_mlp_pre[l]),
                      w_ff1[l].astype(BF16), w_ff2[l].astype(BF16), gain(norm_mlp_post[l]))
    return x2.reshape(B, S, D)
```
